```python
import math
import jax, jax.numpy as jnp
from jax import lax
import numpy as np

D_MODEL = 1024
BATCH = 16
SEQ = 2048
DEPTH = 4

HEAD_DIM = 64
N_MIXERS = 4
GROUP_WIDTH = D_MODEL // N_MIXERS
GROUP_HEADS = GROUP_WIDTH // HEAD_DIM
CHUNK = 128
POOL_WINDOWS = (2, 4, 8, 16)
POOL_GROUPS = len(POOL_WINDOWS)
POOL_GROUP_DIM = GROUP_WIDTH // POOL_GROUPS
WINDOW = 128
SWA_Q_HEADS = GROUP_HEADS
SWA_KV_HEADS = 2
SWA_GROUP = SWA_Q_HEADS // SWA_KV_HEADS
SB_HEADS = GROUP_HEADS
SB_BLOCK = 128
N_BUCKETS = 32
MAX_DISTANCE = 128
D_FF = ((8 * D_MODEL // 3 + 255) // 256) * 256
EPS = 1e-6

IN_SIZES = (GROUP_WIDTH, GROUP_WIDTH, GROUP_WIDTH,
            GROUP_WIDTH, SWA_KV_HEADS * HEAD_DIM, SWA_KV_HEADS * HEAD_DIM,
            GROUP_WIDTH, GROUP_WIDTH, GROUP_WIDTH)
D_IN = sum(IN_SIZES)

kernel_name = "hybrid_parallel_headgroup_trunk"


def _split_points():
    pts, acc = [], 0
    for s in IN_SIZES[:-1]:
        acc += s
        pts.append(acc)
    return pts


def _rmsnorm(x, g):
    xf = x.astype(jnp.float32)
    y = xf * lax.rsqrt(jnp.mean(xf * xf, axis=-1, keepdims=True) + EPS)
    return (y * g.astype(jnp.float32)).astype(x.dtype)


def _layernorm_noaffine(x):
    xf = x.astype(jnp.float32)
    mu = jnp.mean(xf, axis=-1, keepdims=True)
    xc = xf - mu
    y = xc * lax.rsqrt(jnp.mean(xc * xc, axis=-1, keepdims=True) + EPS)
    return y.astype(x.dtype)


def _chunked_sgu(u, v, w_s, b_s):
    B, S, _ = u.shape
    nc = S // CHUNK
    u = jax.nn.gelu(u)
    v = _layernorm_noaffine(jax.nn.gelu(v).reshape(B, nc, CHUNK, GROUP_HEADS, HEAD_DIM))
    causal = jnp.tril(jnp.ones((CHUNK, CHUNK), w_s.dtype))
    w = w_s * causal[None]
    mix = jnp.einsum('hts,bnshd->bnthd', w, v) + b_s.T[None, None, :, :, None]
    return u * mix.reshape(B, S, GROUP_WIDTH)


def _multiscale_pool(p, w_pool, scale):
    B, S, _ = p.shape
    pg = p.reshape(B, S, POOL_GROUPS, POOL_GROUP_DIM)
    csum = jnp.cumsum(pg.astype(jnp.float32), axis=1)
    csum = jnp.pad(csum, ((0, 0), (1, 0), (0, 0), (0, 0)))
    t = jnp.arange(S)[:, None]
    win = jnp.array(POOL_WINDOWS, jnp.int32)[None, :]
    start = jnp.maximum(t + 1 - win, 0)
    gidx = jnp.arange(POOL_GROUPS)[None, :]
    window_sum = csum[:, 1:] - csum[:, start, gidx]
    count = (t + 1 - start).astype(jnp.float32)
    pooled = window_sum / count[None, :, :, None]
    y = pooled.astype(p.dtype) - pg
    y = jnp.einsum('bsgc,gcd->bsgd', y, w_pool)
    return y.reshape(B, S, GROUP_WIDTH) * scale


def _t5_bucket(dist):
    max_exact = N_BUCKETS // 2
    df = jnp.maximum(dist, 1).astype(jnp.float32)
    large = max_exact + (jnp.log(df / max_exact) / math.log(MAX_DISTANCE / max_exact)
                         * (N_BUCKETS - max_exact)).astype(jnp.int32)
    large = jnp.minimum(large, N_BUCKETS - 1)
    return jnp.where(dist < max_exact, dist, large)


def _swa_sink_attention(q, k, v, sinks, rel_bias):
    B, S, _ = q.shape
    nb = S // WINDOW
    qb = q.reshape(B, nb, WINDOW, SWA_KV_HEADS, SWA_GROUP, HEAD_DIM)
    kb = k.reshape(B, nb, WINDOW, SWA_KV_HEADS, HEAD_DIM)
    vb = v.reshape(B, nb, WINDOW, SWA_KV_HEADS, HEAD_DIM)
    pad = ((0, 0), (1, 0), (0, 0), (0, 0), (0, 0))
    k2 = jnp.concatenate([jnp.pad(kb, pad)[:, :-1], kb], axis=2)
    v2 = jnp.concatenate([jnp.pad(vb, pad)[:, :-1], vb], axis=2)
    logits = jnp.einsum('bnqhgd,bnkhd->bnhgqk', qb, k2,
                        preferred_element_type=jnp.float32) * (HEAD_DIM ** -0.5)
    dist = (jnp.arange(WINDOW)[:, None] + WINDOW) - jnp.arange(2 * WINDOW)[None, :]
    in_window = (dist >= 0) & (dist < WINDOW)
    bias = rel_bias.astype(jnp.float32)[_t5_bucket(jnp.clip(dist, 0, WINDOW - 1))]
    bias = bias.transpose(2, 0, 1).reshape(SWA_KV_HEADS, SWA_GROUP, WINDOW, 2 * WINDOW)
    not_pad = (jnp.arange(nb)[:, None] > 0) | (jnp.arange(2 * WINDOW)[None, :] >= WINDOW)
    mask = in_window[None] & not_pad[:, None, :]
    logits = jnp.where(mask[None, :, None, None], logits + bias, -1e30)
    sink = jnp.broadcast_to(sinks.astype(jnp.float32).reshape(SWA_KV_HEADS, SWA_GROUP, 1, 1),
                            logits.shape[:-1] + (1,))
    probs = jax.nn.softmax(jnp.concatenate([logits, sink], axis=-1), axis=-1)[..., :-1]
    out = jnp.einsum('bnhgqk,bnkhd->bnqhgd', probs.astype(v.dtype), v2)
    return out.reshape(B, S, GROUP_WIDTH)


def _stick_breaking_attention(q, k, v):
    B, S, _ = q.shape
    nb = S // SB_BLOCK
    kh = k.reshape(B, S, SB_HEADS, HEAD_DIM)
    vh = v.reshape(B, S, SB_HEADS, HEAD_DIM)
    qb = q.reshape(B, nb, SB_BLOCK, SB_HEADS, HEAD_DIM).transpose(1, 0, 2, 3, 4)
    key_pos = jnp.arange(S)
    scale = HEAD_DIM ** -0.5

    def block(args):
        qblk, n = args
        z = jnp.einsum('bqhd,bkhd->bhqk', qblk, kh,
                       preferred_element_type=jnp.float32) * scale
        q_pos = n * SB_BLOCK + jnp.arange(SB_BLOCK)
        causal = key_pos[None, :] < q_pos[:, None]
        log_1m = jnp.where(causal, jax.nn.log_sigmoid(-z), 0.0)
        tail = lax.cumsum(log_1m, axis=3, reverse=True) - log_1m
        w = jnp.where(causal, jnp.exp(jax.nn.log_sigmoid(z) + tail), 0.0)
        return jnp.einsum('bhqk,bkhd->bqhd', w.astype(v.dtype), vh)

    out = lax.map(block, (qb, jnp.arange(nb)))
    return out.transpose(1, 0, 2, 3, 4).reshape(B, S, GROUP_WIDTH)


def _fwd_setup_inputs(seed: int = 0) -> dict:
    key = jax.random.key(seed)
    ks = jax.random.split(key, 16)
    f32 = jnp.float32
    nrm = lambda k, shape, s: jax.random.normal(k, shape, f32) * s
    return {
        "x": jax.random.normal(ks[0], (BATCH, SEQ, D_MODEL), f32),
        "w_in": nrm(ks[1], (DEPTH, D_MODEL, D_IN), D_MODEL ** -0.5),
        "w_out": nrm(ks[2], (DEPTH, D_MODEL, D_MODEL), D_MODEL ** -0.5),
        "sgu_w": nrm(ks[3], (DEPTH, GROUP_HEADS, CHUNK, CHUNK), CHUNK ** -0.5),
        "sgu_b": 1.0 + nrm(ks[4], (DEPTH, GROUP_HEADS, CHUNK), 0.02),
        "pool_w": nrm(ks[5], (DEPTH, POOL_GROUPS, POOL_GROUP_DIM, POOL_GROUP_DIM), POOL_GROUP_DIM ** -0.5),
        "pool_scale": 1.0 + nrm(ks[6], (DEPTH, GROUP_WIDTH), 0.02),
        "swa_sinks": nrm(ks[7], (DEPTH, SWA_Q_HEADS), 1.0),
        "rel_bias": nrm(ks[8], (N_BUCKETS, SWA_Q_HEADS), 0.5),
        "mix_out_gain": 1.0 + nrm(ks[9], (DEPTH, D_MODEL), 0.02),
        "norm_mix": 1.0 + nrm(ks[10], (DEPTH, D_MODEL), 0.02),
        "norm_ffn": 1.0 + nrm(ks[11], (DEPTH, D_MODEL), 0.02),
        "w_gate_up": nrm(ks[12], (DEPTH, D_MODEL, 2 * D_FF), D_MODEL ** -0.5),
        "w_down": nrm(ks[13], (DEPTH, D_FF, D_MODEL), D_FF ** -0.5),
        "norm_final": 1.0 + nrm(ks[14], (D_MODEL,), 0.02),
    }


def _fwd_reference(x, w_in, w_out, sgu_w, sgu_b, pool_w, pool_scale, swa_sinks, rel_bias,
              mix_out_gain, norm_mix, norm_ffn, w_gate_up, w_down, norm_final):
    B, S, _ = x.shape
    splits = _split_points()
    for l in range(DEPTH):
        h = _rmsnorm(x, norm_mix[l])
        proj = h @ w_in[l]
        a_u, a_v, b_in, c_q, c_k, c_v, d_q, d_k, d_v = jnp.split(proj, splits, axis=-1)
        y_a = _chunked_sgu(a_u, a_v, sgu_w[l], sgu_b[l])
        y_b = _multiscale_pool(b_in, pool_w[l], pool_scale[l])
        y_c = _swa_sink_attention(c_q, c_k, c_v, swa_sinks[l], rel_bias)
        y_d = _stick_breaking_attention(d_q, d_k, d_v)
        ycat = jnp.stack([y_a, y_b, y_c, y_d], axis=2)
        ycat = _rmsnorm(ycat, mix_out_gain[l].reshape(N_MIXERS, GROUP_WIDTH))
        x = x + ycat.reshape(B, S, D_MODEL) @ w_out[l]
        h = _rmsnorm(x, norm_ffn[l])
        gate, up = jnp.split(h @ w_gate_up[l], 2, axis=-1)
        x = x + (jax.nn.silu(gate) * up) @ w_down[l]
    return _rmsnorm(x, norm_final)


import jax as _jax
import jax.numpy as _jnp

TWIN_FORMAT = 'train_step'
FWD_PARAMS = ['x', 'w_in', 'w_out', 'sgu_w', 'sgu_b', 'pool_w', 'pool_scale', 'swa_sinks', 'rel_bias', 'mix_out_gain', 'norm_mix', 'norm_ffn', 'w_gate_up', 'w_down', 'norm_final']
TWIN_WEIGHTS = ['w_in', 'w_out', 'sgu_w', 'sgu_b', 'pool_w', 'pool_scale', 'swa_sinks', 'rel_bias', 'mix_out_gain', 'norm_mix', 'norm_ffn', 'w_gate_up', 'w_down', 'norm_final']
TWIN_DIFF_INPUT = 'x'
TWIN_INPUTS = ['x', 'w_in', 'w_out', 'sgu_w', 'sgu_b', 'pool_w', 'pool_scale', 'swa_sinks', 'rel_bias', 'mix_out_gain', 'norm_mix', 'norm_ffn', 'w_gate_up', 'w_down', 'norm_final', 'loss_target', 'm_w_in', 'm_w_out', 'm_sgu_w', 'm_sgu_b', 'm_pool_w', 'm_pool_scale', 'm_swa_sinks', 'm_rel_bias', 'm_mix_out_gain', 'm_norm_mix', 'm_norm_ffn', 'm_w_gate_up', 'm_w_down', 'm_norm_final', 'v_w_in', 'v_w_out', 'v_sgu_w', 'v_sgu_b', 'v_pool_w', 'v_pool_scale', 'v_swa_sinks', 'v_rel_bias', 'v_mix_out_gain', 'v_norm_mix', 'v_norm_ffn', 'v_w_gate_up', 'v_w_down', 'v_norm_final']
TWIN_OUTPUTS = ['loss', 'grad_x', 'grad_w_in', 'grad_w_out', 'grad_sgu_w', 'grad_sgu_b', 'grad_pool_w', 'grad_pool_scale', 'grad_swa_sinks', 'grad_rel_bias', 'grad_mix_out_gain', 'grad_norm_mix', 'grad_norm_ffn', 'grad_w_gate_up', 'grad_w_down', 'grad_norm_final', 'delta_w_in', 'delta_w_out', 'delta_sgu_w', 'delta_sgu_b', 'delta_pool_w', 'delta_pool_scale', 'delta_swa_sinks', 'delta_rel_bias', 'delta_mix_out_gain', 'delta_norm_mix', 'delta_norm_ffn', 'delta_w_gate_up', 'delta_w_down', 'delta_norm_final', 'new_m_w_in', 'new_m_w_out', 'new_m_sgu_w', 'new_m_sgu_b', 'new_m_pool_w', 'new_m_pool_scale', 'new_m_swa_sinks', 'new_m_rel_bias', 'new_m_mix_out_gain', 'new_m_norm_mix', 'new_m_norm_ffn', 'new_m_w_gate_up', 'new_m_w_down', 'new_m_norm_final', 'new_v_w_in', 'new_v_w_out', 'new_v_sgu_w', 'new_v_sgu_b', 'new_v_pool_w', 'new_v_pool_scale', 'new_v_swa_sinks', 'new_v_rel_bias', 'new_v_mix_out_gain', 'new_v_norm_mix', 'new_v_norm_ffn', 'new_v_w_gate_up', 'new_v_w_down', 'new_v_norm_final']
TWIN_LEAF_KINDS = {'loss': 'loss', 'grad_x': 'grad_x', 'grad_w_in': 'grad_w', 'grad_w_out': 'grad_w', 'grad_sgu_w': 'grad_w', 'grad_sgu_b': 'grad_w', 'grad_pool_w': 'grad_w', 'grad_pool_scale': 'grad_w', 'grad_swa_sinks': 'grad_w', 'grad_rel_bias': 'grad_w', 'grad_mix_out_gain': 'grad_w', 'grad_norm_mix': 'grad_w', 'grad_norm_ffn': 'grad_w', 'grad_w_gate_up': 'grad_w', 'grad_w_down': 'grad_w', 'grad_norm_final': 'grad_w', 'delta_w_in': 'delta_w', 'delta_w_out': 'delta_w', 'delta_sgu_w': 'delta_w', 'delta_sgu_b': 'delta_w', 'delta_pool_w': 'delta_w', 'delta_pool_scale': 'delta_w', 'delta_swa_sinks': 'delta_w', 'delta_rel_bias': 'delta_w', 'delta_mix_out_gain': 'delta_w', 'delta_norm_mix': 'delta_w', 'delta_norm_ffn': 'delta_w', 'delta_w_gate_up': 'delta_w', 'delta_w_down': 'delta_w', 'delta_norm_final': 'delta_w', 'new_m_w_in': 'new_m', 'new_m_w_out': 'new_m', 'new_m_sgu_w': 'new_m', 'new_m_sgu_b': 'new_m', 'new_m_pool_w': 'new_m', 'new_m_pool_scale': 'new_m', 'new_m_swa_sinks': 'new_m', 'new_m_rel_bias': 'new_m', 'new_m_mix_out_gain': 'new_m', 'new_m_norm_mix': 'new_m', 'new_m_norm_ffn': 'new_m', 'new_m_w_gate_up': 'new_m', 'new_m_w_down': 'new_m', 'new_m_norm_final': 'new_m', 'new_v_w_in': 'new_v', 'new_v_w_out': 'new_v', 'new_v_sgu_w': 'new_v', 'new_v_sgu_b': 'new_v', 'new_v_pool_w': 'new_v', 'new_v_pool_scale': 'new_v', 'new_v_swa_sinks': 'new_v', 'new_v_rel_bias': 'new_v', 'new_v_mix_out_gain': 'new_v', 'new_v_norm_mix': 'new_v', 'new_v_norm_ffn': 'new_v', 'new_v_w_gate_up': 'new_v', 'new_v_w_down': 'new_v', 'new_v_norm_final': 'new_v'}


def _forward(args):
    return _fwd_reference(*[args[k] for k in FWD_PARAMS])


def _output_shape():
    out = _jax.eval_shape(lambda: _forward(_fwd_setup_inputs(0)))
    return out.shape, out.dtype

N_MICROBATCH = 1
ADAM_LR = 0.001
ADAM_B1 = 0.9
ADAM_B2 = 0.999
ADAM_EPS = 1e-08
ADAM_WD = 0.01
ADAM_STEP = 10
PER_EXAMPLE_BATCH_AXIS = {'x': 0, 'loss_target': 0}
SHARED_INPUTS = []
_WEIGHT_DTYPES = {'w_in': _jnp.float32, 'w_out': _jnp.float32, 'sgu_w': _jnp.float32, 'sgu_b': _jnp.float32, 'pool_w': _jnp.float32, 'pool_scale': _jnp.float32, 'swa_sinks': _jnp.float32, 'rel_bias': _jnp.float32, 'mix_out_gain': _jnp.float32, 'norm_mix': _jnp.float32, 'norm_ffn': _jnp.float32, 'w_gate_up': _jnp.float32, 'w_down': _jnp.float32, 'norm_final': _jnp.float32}
MOMENT_SCALE = {'w_in': 1.010465e-01, 'w_out': 1.229810e-01, 'sgu_w': 4.368352e-02, 'sgu_b': 5.958741e-02, 'pool_w': 1.128364e-01, 'pool_scale': 1.165446e-01, 'swa_sinks': 4.804606e-02, 'rel_bias': 2.112889e-01, 'mix_out_gain': 1.211959e-01, 'norm_mix': 1.457269e-01, 'norm_ffn': 8.741742e-02, 'w_gate_up': 3.688053e-02, 'w_down': 6.020306e-02, 'norm_final': 3.220309e+01}


def _to_microbatches(a, axis):
    t = _jnp.moveaxis(a, axis, 0)
    t = t.reshape((N_MICROBATCH, t.shape[0] // N_MICROBATCH) + t.shape[1:])
    return _jnp.moveaxis(t, 1, axis + 1)


def setup_inputs(seed: int = 0) -> dict:
    inp = _fwd_setup_inputs(seed)
    key = _jax.random.fold_in(_jax.random.key(seed), 7919)
    shape, _ = _output_shape()
    out = dict(inp)
    out["loss_target"] = _jax.random.normal(_jax.random.fold_in(key, 0), shape, _jnp.float32)
    for i, name in enumerate(TWIN_WEIGHTS):
        w = inp[name].astype(_jnp.float32)
        if MOMENT_SCALE is None:
            s = _jnp.sqrt(_jnp.mean(_jnp.square(w)) + 1e-30)
        else:
            s = MOMENT_SCALE[name]
        km, kv = _jax.random.split(_jax.random.fold_in(key, i + 1))
        out[name] = w
        out["m_" + name] = s * _jax.random.normal(km, w.shape, _jnp.float32)
        out["v_" + name] = (s * s) * _jax.random.uniform(kv, w.shape, _jnp.float32, 0.5, 1.5)
    if N_MICROBATCH > 1:
        for name, axis in PER_EXAMPLE_BATCH_AXIS.items():
            out[name] = _to_microbatches(out[name], axis)
    return {'x': out['x'], 'w_in': out['w_in'], 'w_out': out['w_out'], 'sgu_w': out['sgu_w'], 'sgu_b': out['sgu_b'], 'pool_w': out['pool_w'], 'pool_scale': out['pool_scale'], 'swa_sinks': out['swa_sinks'], 'rel_bias': out['rel_bias'], 'mix_out_gain': out['mix_out_gain'], 'norm_mix': out['norm_mix'], 'norm_ffn': out['norm_ffn'], 'w_gate_up': out['w_gate_up'], 'w_down': out['w_down'], 'norm_final': out['norm_final'], 'loss_target': out['loss_target'], 'm_w_in': out['m_w_in'], 'm_w_out': out['m_w_out'], 'm_sgu_w': out['m_sgu_w'], 'm_sgu_b': out['m_sgu_b'], 'm_pool_w': out['m_pool_w'], 'm_pool_scale': out['m_pool_scale'], 'm_swa_sinks': out['m_swa_sinks'], 'm_rel_bias': out['m_rel_bias'], 'm_mix_out_gain': out['m_mix_out_gain'], 'm_norm_mix': out['m_norm_mix'], 'm_norm_ffn': out['m_norm_ffn'], 'm_w_gate_up': out['m_w_gate_up'], 'm_w_down': out['m_w_down'], 'm_norm_final': out['m_norm_final'], 'v_w_in': out['v_w_in'], 'v_w_out': out['v_w_out'], 'v_sgu_w': out['v_sgu_w'], 'v_sgu_b': out['v_sgu_b'], 'v_pool_w': out['v_pool_w'], 'v_pool_scale': out['v_pool_scale'], 'v_swa_sinks': out['v_swa_sinks'], 'v_rel_bias': out['v_rel_bias'], 'v_mix_out_gain': out['v_mix_out_gain'], 'v_norm_mix': out['v_norm_mix'], 'v_norm_ffn': out['v_norm_ffn'], 'v_w_gate_up': out['v_w_gate_up'], 'v_w_down': out['v_w_down'], 'v_norm_final': out['v_norm_final']}


def _loss(weights, diff, rest, loss_target):
    with _jax.named_scope("forward"):
        args = {**rest, TWIN_DIFF_INPUT: diff, **{k: w.astype(_WEIGHT_DTYPES[k]) for k, w in weights.items()}}
        y = _forward(args)
    with _jax.named_scope("loss_head"):
        err = _jnp.square(y.astype(_jnp.float32) - loss_target)
        return 0.5 * _jnp.sum(_jnp.mean(err, axis=-1)) if err.ndim else 0.5 * err


def _adamw(w, g, m, v):
    m = ADAM_B1 * m + (1.0 - ADAM_B1) * g
    v = ADAM_B2 * v + (1.0 - ADAM_B2) * _jnp.square(g)
    m_hat = m / (1.0 - ADAM_B1 ** ADAM_STEP)
    v_hat = v / (1.0 - ADAM_B2 ** ADAM_STEP)
    delta = -ADAM_LR * (m_hat / (_jnp.sqrt(v_hat) + ADAM_EPS) + ADAM_WD * w)
    return delta, m, v


def reference(x, w_in, w_out, sgu_w, sgu_b, pool_w, pool_scale, swa_sinks, rel_bias, mix_out_gain, norm_mix, norm_ffn, w_gate_up, w_down, norm_final, loss_target, m_w_in, m_w_out, m_sgu_w, m_sgu_b, m_pool_w, m_pool_scale, m_swa_sinks, m_rel_bias, m_mix_out_gain, m_norm_mix, m_norm_ffn, m_w_gate_up, m_w_down, m_norm_final, v_w_in, v_w_out, v_sgu_w, v_sgu_b, v_pool_w, v_pool_scale, v_swa_sinks, v_rel_bias, v_mix_out_gain, v_norm_mix, v_norm_ffn, v_w_gate_up, v_w_down, v_norm_final):
    given = dict(x=x, w_in=w_in, w_out=w_out, sgu_w=sgu_w, sgu_b=sgu_b, pool_w=pool_w, pool_scale=pool_scale, swa_sinks=swa_sinks, rel_bias=rel_bias, mix_out_gain=mix_out_gain, norm_mix=norm_mix, norm_ffn=norm_ffn, w_gate_up=w_gate_up, w_down=w_down, norm_final=norm_final, loss_target=loss_target, m_w_in=m_w_in, m_w_out=m_w_out, m_sgu_w=m_sgu_w, m_sgu_b=m_sgu_b, m_pool_w=m_pool_w, m_pool_scale=m_pool_scale, m_swa_sinks=m_swa_sinks, m_rel_bias=m_rel_bias, m_mix_out_gain=m_mix_out_gain, m_norm_mix=m_norm_mix, m_norm_ffn=m_norm_ffn, m_w_gate_up=m_w_gate_up, m_w_down=m_w_down, m_norm_final=m_norm_final, v_w_in=v_w_in, v_w_out=v_w_out, v_sgu_w=v_sgu_w, v_sgu_b=v_sgu_b, v_pool_w=v_pool_w, v_pool_scale=v_pool_scale, v_swa_sinks=v_swa_sinks, v_rel_bias=v_rel_bias, v_mix_out_gain=v_mix_out_gain, v_norm_mix=v_norm_mix, v_norm_ffn=v_norm_ffn, v_w_gate_up=v_w_gate_up, v_w_down=v_w_down, v_norm_final=v_norm_final)
    weights = {n: given[n] for n in TWIN_WEIGHTS}
    shared = {n: given[n] for n in SHARED_INPUTS}
    per_example = {n: given[n] for n in ['x']}
    grad_fn = _jax.value_and_grad(_loss, argnums=(0, 1))

    def one_microbatch(ex, loss_target):
        ex = dict(ex)
        diff = ex.pop(TWIN_DIFF_INPUT)
        return grad_fn(weights, diff, {**shared, **ex}, loss_target)

    if N_MICROBATCH == 1:
        loss, (grad_w, grad_x) = one_microbatch(per_example, given["loss_target"])
    else:
        def body(carry, xs):
            loss_sum, grad_sum = carry
            l_k, (gw_k, gx_k) = one_microbatch(xs[0], xs[1])
            with _jax.named_scope("update"):
                return (loss_sum + l_k, _jax.tree.map(_jnp.add, grad_sum, gw_k)), gx_k

        init = (_jnp.zeros((), _jnp.float32), _jax.tree.map(_jnp.zeros_like, weights))
        (loss, grad_w), grad_x = _jax.lax.scan(body, init, (per_example, given["loss_target"]))
    with _jax.named_scope("update"):
        delta_w, new_m, new_v = {}, {}, {}
        for n in TWIN_WEIGHTS:
            delta_w[n], new_m[n], new_v[n] = _adamw(weights[n], grad_w[n], given["m_" + n], given["v_" + n])
    return (loss, grad_x, *[grad_w[n] for n in TWIN_WEIGHTS], *[delta_w[n] for n in TWIN_WEIGHTS],
            *[new_m[n] for n in TWIN_WEIGHTS], *[new_v[n] for n in TWIN_WEIGHTS])
```

```python
import functools

import numpy as np
import jax
import jax.numpy as jnp
from jax import lax
from jax.experimental import pallas as pl
from jax.experimental.pallas import tpu as pltpu

f32 = jnp.float32
bf16 = jnp.bfloat16
_MXU = jnp.bfloat16

EPS = 1e-6
HD = 64
GW = 256
BLK = 128
POOL_WINDOWS = (2, 4, 8, 16)
N_BUCKETS = 32
MAX_DISTANCE = 128
N_CHIPS = 4
N_DEV = 8
VMEM_BYTES_V7X = 64 * 1024 * 1024
VMEM_LIMIT = 48 * 1024 * 1024

ADAM_LR = 0.001
ADAM_B1 = 0.9
ADAM_B2 = 0.999
ADAM_EPS = 1e-08
ADAM_WD = 0.01
ADAM_STEP = 10

SDS = jax.ShapeDtypeStruct
MESH = pl.DeviceIdType.MESH
HIGHEST = lax.Precision.HIGHEST
NT = (((1,), (1,)), ((), ()))
TN = (((0,), (0,)), ((), ()))


def _cp(*sem):
    return pltpu.CompilerParams(dimension_semantics=sem if sem else None, vmem_limit_bytes=VMEM_LIMIT)


def _mx(v):
    return v.astype(_MXU)


def _iota(shape, dim):
    return lax.broadcasted_iota(jnp.int32, shape, dim)


def _split_dot(a, tri):
    hi = a.astype(bf16)
    lo = (a - hi.astype(f32)).astype(bf16)
    return jnp.dot(hi, tri, preferred_element_type=f32) + jnp.dot(lo, tri, preferred_element_type=f32)


def _rms(xv):
    return lax.rsqrt(jnp.mean(xv * xv, axis=-1, keepdims=True) + EPS)


def _norm_mm(x, gain, w, tm):
    T, D = x.shape
    NS, _, ns = w.shape

    def body(x_ref, g_ref, w_ref, h_ref, o_ref):
        @pl.when(pl.program_id(1) == 0)
        def _():
            xv = x_ref[...]
            h_ref[...] = (xv * _rms(xv) * g_ref[...]).astype(bf16)
        o_ref[...] = jnp.dot(_mx(h_ref[...]), w_ref[0], preferred_element_type=f32)

    return pl.pallas_call(
        body, grid=(T // tm, NS),
        in_specs=[pl.BlockSpec((tm, D), lambda i, s: (i, 0)),
                  pl.BlockSpec((1, D), lambda i, s: (0, 0)),
                  pl.BlockSpec((1, D, ns), lambda i, s: (s, 0, 0))],
        out_specs=[pl.BlockSpec((tm, D), lambda i, s: (i, 0)),
                   pl.BlockSpec((tm, ns), lambda i, s: (i, s))],
        out_shape=[SDS((T, D), bf16), SDS((T, NS * ns), f32)],
        name="norm_mm_in", compiler_params=_cp("parallel", "arbitrary"))(x, gain, w)


def _norm_mm_swiglu(x, gain, w, tm):
    T, D = x.shape
    NS, _, ns = w.shape
    half = NS // 2

    def body(x_ref, g_ref, wg_ref, wu_ref, h_ref, gu_ref, a_ref):
        @pl.when(pl.program_id(1) == 0)
        def _():
            xv = x_ref[...]
            h_ref[...] = (xv * _rms(xv) * g_ref[...]).astype(bf16)
        h = _mx(h_ref[...])
        g = jnp.dot(h, wg_ref[0], preferred_element_type=f32)
        u = jnp.dot(h, wu_ref[0], preferred_element_type=f32)
        gu_ref[0] = g.astype(bf16)
        gu_ref[1] = u.astype(bf16)
        a_ref[...] = (jax.nn.silu(g) * u).astype(bf16)

    return pl.pallas_call(
        body, grid=(T // tm, half),
        in_specs=[pl.BlockSpec((tm, D), lambda i, s: (i, 0)),
                  pl.BlockSpec((1, D), lambda i, s: (0, 0)),
                  pl.BlockSpec((1, D, ns), lambda i, s: (s, 0, 0)),
                  pl.BlockSpec((1, D, ns), lambda i, s: (s + half, 0, 0))],
        out_specs=[pl.BlockSpec((tm, D), lambda i, s: (i, 0)),
                   pl.BlockSpec((2, tm, ns), lambda i, s: (0, i, s)),
                   pl.BlockSpec((tm, ns), lambda i, s: (i, s))],
        out_shape=[SDS((T, D), bf16), SDS((2, T, half * ns), bf16), SDS((T, half * ns), bf16)],
        name="norm_mm_swiglu", compiler_params=_cp("parallel", "arbitrary"))(x, gain, w, w)


def _gnorm_mm_res(ys, gain, w, x, tm):
    T, D = x.shape

    def body(ya, yb, yc, yd, g_ref, w_ref, x_ref, yn_ref, o_ref):
        parts = []
        for m, r in enumerate((ya, yb, yc, yd)):
            y = r[...]
            parts.append((y * _rms(y) * g_ref[:, m * GW:(m + 1) * GW]).astype(bf16))
        yn = jnp.concatenate(parts, axis=1)
        yn_ref[...] = yn
        o_ref[...] = x_ref[...] + jnp.dot(_mx(yn), w_ref[...], preferred_element_type=f32)

    yspec = pl.BlockSpec((tm, GW), lambda i: (i, 0))
    return pl.pallas_call(
        body, grid=(T // tm,),
        in_specs=[yspec, yspec, yspec, yspec,
                  pl.BlockSpec((1, D), lambda i: (0, 0)),
                  pl.BlockSpec((D, D), lambda i: (0, 0)),
                  pl.BlockSpec((tm, D), lambda i: (i, 0))],
        out_specs=[pl.BlockSpec((tm, D), lambda i: (i, 0)), pl.BlockSpec((tm, D), lambda i: (i, 0))],
        out_shape=[SDS((T, D), bf16), SDS((T, D), f32)],
        name="gnorm_mm_res", compiler_params=_cp("parallel"))(*ys, gain, w, x)


def _mm_res(a, w, x, tm):
    T, D = x.shape
    K = a.shape[1]

    def body(a_ref, w_ref, x_ref, o_ref):
        o_ref[...] = x_ref[...] + jnp.dot(_mx(a_ref[...]), w_ref[...], preferred_element_type=f32)

    return pl.pallas_call(
        body, grid=(T // tm,),
        in_specs=[pl.BlockSpec((tm, K), lambda i: (i, 0)),
                  pl.BlockSpec((K, D), lambda i: (0, 0)),
                  pl.BlockSpec((tm, D), lambda i: (i, 0))],
        out_specs=pl.BlockSpec((tm, D), lambda i: (i, 0)),
        out_shape=SDS((T, D), f32),
        name="mm_res_down", compiler_params=_cp("parallel"))(a, w, x)


def _final_loss(x, gain, tgt, tm):
    T, D = x.shape

    def body(x_ref, g_ref, t_ref, dx_ref, dg_ref, l_ref):
        @pl.when(pl.program_id(0) == 0)
        def _():
            dg_ref[...] = jnp.zeros_like(dg_ref)
            l_ref[...] = jnp.zeros_like(l_ref)
        xv = x_ref[...]
        g = g_ref[...]
        r = _rms(xv)
        xh = xv * r
        err = xh * g - t_ref[...]
        l_ref[...] += 0.5 * jnp.sum(jnp.mean(err * err, axis=-1, keepdims=True), axis=0, keepdims=True)
        dy = err * (1.0 / D)
        dg_ref[...] += jnp.sum(dy * xh, axis=0, keepdims=True)
        dxh = dy * g
        dx_ref[...] = r * (dxh - xh * jnp.mean(dxh * xh, axis=-1, keepdims=True))

    return pl.pallas_call(
        body, grid=(T // tm,),
        in_specs=[pl.BlockSpec((tm, D), lambda i: (i, 0)),
                  pl.BlockSpec((1, D), lambda i: (0, 0)),
                  pl.BlockSpec((tm, D), lambda i: (i, 0))],
        out_specs=[pl.BlockSpec((tm, D), lambda i: (i, 0)),
                   pl.BlockSpec((1, D), lambda i: (0, 0)),
                   pl.BlockSpec((1, BLK), lambda i: (0, 0))],
        out_shape=[SDS((T, D), f32), SDS((1, D), f32), SDS((1, BLK), f32)],
        name="final_loss", compiler_params=_cp("arbitrary"))(x, gain, tgt)


def _dact(dx, wd, gu, tm):
    T, D = dx.shape
    F = wd.shape[0]
    ns = F // 2

    def body(dx_ref, w_ref, gu_ref, o_ref):
        da = lax.dot_general(_mx(dx_ref[...]), w_ref[...], NT, preferred_element_type=f32)
        g = gu_ref[0].astype(f32)
        u = gu_ref[1].astype(f32)
        sg = jax.nn.sigmoid(g)
        o_ref[0] = (da * u * (sg * (1.0 + g * (1.0 - sg)))).astype(bf16)
        o_ref[1] = (da * (g * sg)).astype(bf16)

    return pl.pallas_call(
        body, grid=(T // tm, 2),
        in_specs=[pl.BlockSpec((tm, D), lambda i, s: (i, 0)),
                  pl.BlockSpec((ns, D), lambda i, s: (s, 0)),
                  pl.BlockSpec((2, tm, ns), lambda i, s: (0, i, s))],
        out_specs=pl.BlockSpec((2, tm, ns), lambda i, s: (0, i, s)),
        out_shape=SDS((2, T, F), bf16),
        name="dact", compiler_params=_cp("parallel", "arbitrary"))(dx, wd, gu)


def _dw(a, b, b_map, ns, NS, tka, tk, name):
    T, Ka = a.shape
    b_block = (tk, ns) if b.ndim == 2 else (1, tk, ns)

    def body(a_ref, b_ref, o_ref):
        bv = b_ref[...] if b.ndim == 2 else b_ref[0]
        part = lax.dot_general(_mx(a_ref[...]), _mx(bv), TN, preferred_element_type=f32)

        @pl.when(pl.program_id(2) == 0)
        def _():
            o_ref[0] = part

        @pl.when(pl.program_id(2) > 0)
        def _():
            o_ref[0] += part

    return pl.pallas_call(
        body, grid=(NS, Ka // tka, T // tk),
        in_specs=[pl.BlockSpec((tk, tka), lambda s, k, t: (t, k)),
                  pl.BlockSpec(b_block, lambda s, k, t: b_map(t, s))],
        out_specs=pl.BlockSpec((1, tka, ns), lambda s, k, t: (s, k, 0)),
        out_shape=SDS((NS, Ka, ns), f32),
        name=name, compiler_params=_cp("parallel", "parallel", "arbitrary"))(a, b)


def _dx_norm_bwd(dy, dy_map, w, x, gain, dxin, nk, tm, name):
    T, D = x.shape
    ns = w.shape[2]
    dy_block = (tm, ns) if dy.ndim == 2 else (1, tm, ns)

    def body(dy_ref, w_ref, x_ref, g_ref, dxin_ref, dx_ref, dg_ref, acc_ref):
        i, s = pl.program_id(0), pl.program_id(1)
        dv = dy_ref[...] if dy.ndim == 2 else dy_ref[0]
        part = lax.dot_general(_mx(dv), w_ref[0], NT, preferred_element_type=f32)

        @pl.when(s == 0)
        def _():
            acc_ref[...] = part

        @pl.when(s > 0)
        def _():
            acc_ref[...] += part

        @pl.when(s == nk - 1)
        def _():
            @pl.when(i == 0)
            def _():
                dg_ref[...] = jnp.zeros_like(dg_ref)
            dh = acc_ref[...]
            xv = x_ref[...]
            r = _rms(xv)
            xh = xv * r
            dg_ref[...] += jnp.sum(dh * xh, axis=0, keepdims=True)
            dxh = dh * g_ref[...]
            dx_ref[...] = dxin_ref[...] + r * (dxh - xh * jnp.mean(dxh * xh, axis=-1, keepdims=True))

    return pl.pallas_call(
        body, grid=(T // tm, nk),
        in_specs=[pl.BlockSpec(dy_block, lambda i, s: dy_map(i, s)),
                  pl.BlockSpec((1, D, ns), lambda i, s: (s, 0, 0)),
                  pl.BlockSpec((tm, D), lambda i, s: (i, 0)),
                  pl.BlockSpec((1, D), lambda i, s: (0, 0)),
                  pl.BlockSpec((tm, D), lambda i, s: (i, 0))],
        out_specs=[pl.BlockSpec((tm, D), lambda i, s: (i, 0)),
                   pl.BlockSpec((1, D), lambda i, s: (0, 0))],
        out_shape=[SDS((T, D), f32), SDS((1, D), f32)],
        scratch_shapes=[pltpu.VMEM((tm, D), f32)],
        name=name, compiler_params=_cp("arbitrary", "arbitrary"))(dy, w, x, gain, dxin)


def _dycat(dx, w, ys, gain, tm):
    T, D = dx.shape

    def body(dx_ref, w_ref, ya, yb, yc, yd, g_ref, da, db, dc, dd, dg_ref):
        @pl.when(pl.program_id(0) == 0)
        def _():
            dg_ref[...] = jnp.zeros_like(dg_ref)
        dyn = lax.dot_general(_mx(dx_ref[...]), w_ref[...], NT, preferred_element_type=f32)
        for m, (r, o) in enumerate(((ya, da), (yb, db), (yc, dc), (yd, dd))):
            cols = slice(m * GW, (m + 1) * GW)
            y = r[...]
            rs = _rms(y)
            yh = y * rs
            d = dyn[:, cols]
            dg_ref[:, cols] += jnp.sum(d * yh, axis=0, keepdims=True)
            dyh = d * g_ref[:, cols]
            o[...] = rs * (dyh - yh * jnp.mean(dyh * yh, axis=-1, keepdims=True))

    yspec = pl.BlockSpec((tm, GW), lambda i: (i, 0))
    return pl.pallas_call(
        body, grid=(T // tm,),
        in_specs=[pl.BlockSpec((tm, D), lambda i: (i, 0)),
                  pl.BlockSpec((D, D), lambda i: (0, 0)),
                  yspec, yspec, yspec, yspec,
                  pl.BlockSpec((1, D), lambda i: (0, 0))],
        out_specs=[yspec, yspec, yspec, yspec, pl.BlockSpec((1, D), lambda i: (0, 0))],
        out_shape=[SDS((T, GW), f32)] * 4 + [SDS((1, D), f32)],
        name="dycat", compiler_params=_cp("arbitrary"))(dx, w, *ys, gain)


def _sgu_consts():
    r, c = _iota((GW, GW), 0), _iota((GW, GW), 1)
    seg = (r // HD == c // HD).astype(f32)
    tr, ts = _iota((BLK, BLK), 0), _iota((BLK, BLK), 1)
    causal = ts <= tr
    lane_head = _iota((BLK, GW), 1) // HD
    return seg, causal, lane_head


def _sgu_chunk(au, av, w, bexp, consts):
    seg, causal, lane_head = consts
    u = jax.nn.gelu(au)
    v = jax.nn.gelu(av)
    mu = jnp.dot(v, seg, precision=HIGHEST, preferred_element_type=f32) * (1.0 / HD)
    vc = v - mu
    var = jnp.dot(vc * vc, seg, precision=HIGHEST, preferred_element_type=f32) * (1.0 / HD)
    vn = _mx(vc * lax.rsqrt(var + EPS))
    mix = bexp
    for h in range(GW // HD):
        wh = _mx(jnp.where(causal, w[h], 0.0))
        mix = mix + jnp.where(lane_head == h, jnp.dot(wh, vn, preferred_element_type=f32), 0.0)
    return u * mix


def _sgu_fwd(proj, w, bexp, B, S):
    def body(au_ref, av_ref, w_ref, b_ref, y_ref):
        consts = _sgu_consts()
        wv, bv = w_ref[...], b_ref[...]

        def chunk(n, c):
            rows = pl.ds(pl.multiple_of(n * BLK, BLK), BLK)
            y_ref[rows, :] = _sgu_chunk(au_ref[rows, :], av_ref[rows, :], wv, bv, consts)
            return c
        lax.fori_loop(0, S // BLK, chunk, 0)

    return pl.pallas_call(
        body, grid=(B,),
        in_specs=[pl.BlockSpec((S, GW), lambda b: (b, 0)),
                  pl.BlockSpec((S, GW), lambda b: (b, 1)),
                  pl.BlockSpec((GW // HD, BLK, BLK), lambda b: (0, 0, 0)),
                  pl.BlockSpec((BLK, GW), lambda b: (0, 0))],
        out_specs=pl.BlockSpec((S, GW), lambda b: (b, 0)),
        out_shape=SDS((B * S, GW), f32),
        name="sgu_fwd", compiler_params=_cp("parallel"))(proj, proj, w, bexp)


def _sgu_bwd(proj, w, bexp, dy, B, S):
    def body(au_ref, av_ref, w_ref, b_ref, dy_ref, dp_ref, dw_ref, db_ref):
        @pl.when(pl.program_id(0) == 0)
        def _():
            dw_ref[...] = jnp.zeros_like(dw_ref)
            db_ref[...] = jnp.zeros_like(db_ref)
        consts = _sgu_consts()
        wv, bv = w_ref[...], b_ref[...]
        fn = lambda au, av, ww, bb: _sgu_chunk(au, av, ww, bb, consts)

        def chunk(n, carry):
            dw_acc, db_acc = carry
            rows = pl.ds(pl.multiple_of(n * BLK, BLK), BLK)
            _, vjp = jax.vjp(fn, au_ref[rows, :], av_ref[rows, :], wv, bv)
            dau, dav, dwc, dbc = vjp(dy_ref[rows, :])
            dp_ref[rows, 0:GW] = dau.astype(bf16)
            dp_ref[rows, GW:2 * GW] = dav.astype(bf16)
            return dw_acc + dwc, db_acc + dbc
        dw_acc, db_acc = lax.fori_loop(0, S // BLK, chunk, (jnp.zeros(wv.shape, f32), jnp.zeros(bv.shape, f32)))
        dw_ref[...] += dw_acc
        db_ref[...] += jnp.dot(db_acc, consts[0], precision=HIGHEST, preferred_element_type=f32)

    return pl.pallas_call(
        body, grid=(B,),
        in_specs=[pl.BlockSpec((S, GW), lambda b: (b, 0)),
                  pl.BlockSpec((S, GW), lambda b: (b, 1)),
                  pl.BlockSpec((GW // HD, BLK, BLK), lambda b: (0, 0, 0)),
                  pl.BlockSpec((BLK, GW), lambda b: (0, 0)),
                  pl.BlockSpec((S, GW), lambda b: (b, 0))],
        out_specs=[pl.BlockSpec((S, 2 * GW), lambda b: (b, 0)),
                   pl.BlockSpec((GW // HD, BLK, BLK), lambda b: (0, 0, 0)),
                   pl.BlockSpec((BLK, GW), lambda b: (0, 0))],
        out_shape=[SDS((B * S, 2 * GW), bf16), SDS((GW // HD, BLK, BLK), f32), SDS((BLK, GW), f32)],
        name="sgu_bwd", compiler_params=_cp("arbitrary"))(proj, proj, w, bexp, dy)


def _pool_parts(p):
    n = p.shape[0]
    r = _iota(p.shape, 0)
    lg = _iota(p.shape, 1) // HD

    def sh(v, k):
        return jnp.where(r >= k, pltpu.roll(v, k, 0), 0.0)
    s2 = p + sh(p, 1)
    s4 = s2 + sh(s2, 2)
    s8 = s4 + sh(s4, 4)
    s16 = s8 + sh(s8, 8)
    ws = jnp.where(lg == 0, s2, jnp.where(lg == 1, s4, jnp.where(lg == 2, s8, s16)))
    wlen = jnp.where(lg == 0, 2, jnp.where(lg == 1, 4, jnp.where(lg == 2, 8, 16)))
    cnt = jnp.minimum(r + 1, wlen).astype(f32)
    del n
    return ws / cnt - p, cnt, lg


def _pool_fwd(proj, wbd, scale, B, S):
    def body(p_ref, w_ref, s_ref, y_ref):
        y, _, _ = _pool_parts(p_ref[...])
        y_ref[...] = jnp.dot(_mx(y), _mx(w_ref[...]), preferred_element_type=f32) * s_ref[...]

    return pl.pallas_call(
        body, grid=(B,),
        in_specs=[pl.BlockSpec((S, GW), lambda b: (b, 2)),
                  pl.BlockSpec((GW, GW), lambda b: (0, 0)),
                  pl.BlockSpec((1, GW), lambda b: (0, 0))],
        out_specs=pl.BlockSpec((S, GW), lambda b: (b, 0)),
        out_shape=SDS((B * S, GW), f32),
        name="pool_fwd", compiler_params=_cp("parallel"))(proj, wbd, scale)


def _pool_bwd(proj, wbd, scale, dy, B, S):
    def body(p_ref, w_ref, s_ref, dy_ref, dp_ref, dw_ref, ds_ref):
        @pl.when(pl.program_id(0) == 0)
        def _():
            dw_ref[...] = jnp.zeros_like(dw_ref)
            ds_ref[...] = jnp.zeros_like(ds_ref)
        y, cnt, lg = _pool_parts(p_ref[...])
        wv = _mx(w_ref[...])
        z = jnp.dot(_mx(y), wv, preferred_element_type=f32)
        dout = dy_ref[...]
        ds_ref[...] += jnp.sum(dout * z, axis=0, keepdims=True)
        dz = _mx(dout * s_ref[...])
        dw_ref[...] += lax.dot_general(_mx(y), dz, TN, preferred_element_type=f32)
        dyv = lax.dot_general(dz, wv, NT, preferred_element_type=f32)
        n = dyv.shape[0]
        r = _iota(dyv.shape, 0)

        def ush(v, k):
            return jnp.where(r < n - k, pltpu.roll(v, n - k, 0), 0.0)
        gq = dyv / cnt
        a2 = gq + ush(gq, 1)
        a4 = a2 + ush(a2, 2)
        a8 = a4 + ush(a4, 4)
        a16 = a8 + ush(a8, 8)
        adj = jnp.where(lg == 0, a2, jnp.where(lg == 1, a4, jnp.where(lg == 2, a8, a16)))
        dp_ref[...] = (adj - dyv).astype(bf16)

    return pl.pallas_call(
        body, grid=(B,),
        in_specs=[pl.BlockSpec((S, GW), lambda b: (b, 2)),
                  pl.BlockSpec((GW, GW), lambda b: (0, 0)),
                  pl.BlockSpec((1, GW), lambda b: (0, 0)),
                  pl.BlockSpec((S, GW), lambda b: (b, 0))],
        out_specs=[pl.BlockSpec((S, GW), lambda b: (b, 0)),
                   pl.BlockSpec((GW, GW), lambda b: (0, 0)),
                   pl.BlockSpec((1, GW), lambda b: (0, 0))],
        out_shape=[SDS((B * S, GW), bf16), SDS((GW, GW), f32), SDS((1, GW), f32)],
        name="pool_bwd", compiler_params=_cp("arbitrary"))(proj, wbd, scale, dy)


def _t5_bucket_table():
    dist = (np.arange(BLK)[:, None] + BLK) - np.arange(2 * BLK)[None, :]
    d = np.clip(dist, 0, BLK - 1)
    max_exact = N_BUCKETS // 2
    df = np.maximum(d, 1).astype(np.float32)
    large = max_exact + (np.log(df / max_exact) / np.float32(np.log(MAX_DISTANCE / max_exact))
                         * (N_BUCKETS - max_exact)).astype(np.int32)
    large = np.minimum(large, N_BUCKETS - 1)
    return np.where(d < max_exact, d, large).astype(np.int32)


def _swa_math(qb, k2, v2, sink, bias, n, p, g):
    ri, ci = _iota((BLK, BLK), 0), _iota((BLK, BLK), 1)
    selq = ((ri - g * HD == ci - p * HD) & (ri >= g * HD) & (ri < (g + 1) * HD)).astype(_MXU)
    selv = ((ci - g * HD == ri - p * HD) & (ci >= g * HD) & (ci < (g + 1) * HD)).astype(_MXU)
    qi, ki = _iota((BLK, 2 * BLK), 0), _iota((BLK, 2 * BLK), 1)
    dist = qi + BLK - ki
    mask = (dist >= 0) & (dist < BLK) & ((ki >= BLK) | (n > 0))
    qs = _mx(jnp.dot(_mx(qb), selq, preferred_element_type=f32))
    z = lax.dot_general(qs, _mx(k2), NT, preferred_element_type=f32) * (HD ** -0.5)
    z = jnp.where(mask, z + bias, -1e30)
    s = jnp.mean(sink, axis=-1, keepdims=True)
    m = jnp.maximum(jnp.max(z, axis=-1, keepdims=True), s)
    e = jnp.exp(z - m)
    pr = e / (jnp.sum(e, axis=-1, keepdims=True) + jnp.exp(s - m))
    vs = _mx(jnp.dot(_mx(v2), selv, preferred_element_type=f32))
    return jnp.dot(_mx(pr), vs, preferred_element_type=f32)


def _swa_fwd(proj, sinks, bias, B, S):
    def body(q_ref, kv_ref, s_ref, b_ref, y_ref):
        def block(n, c):
            rows = pl.ds(pl.multiple_of(n * BLK, BLK), BLK)
            prev = pl.ds(pl.multiple_of(jnp.maximum(n - 1, 0) * BLK, BLK), BLK)
            k2 = jnp.concatenate([kv_ref[prev, 0:BLK], kv_ref[rows, 0:BLK]], axis=0)
            v2 = jnp.concatenate([kv_ref[prev, BLK:2 * BLK], kv_ref[rows, BLK:2 * BLK]], axis=0)
            for p in range(2):
                qb = q_ref[rows, p * BLK:(p + 1) * BLK]
                o = jnp.zeros((BLK, BLK), f32)
                for g in range(2):
                    o = o + _swa_math(qb, k2, v2, s_ref[2 * p + g], b_ref[2 * p + g], n, p, g)
                y_ref[rows, p * BLK:(p + 1) * BLK] = o
            return c
        lax.fori_loop(0, S // BLK, block, 0)

    return pl.pallas_call(
        body, grid=(B,),
        in_specs=[pl.BlockSpec((S, GW), lambda b: (b, 3)),
                  pl.BlockSpec((S, GW), lambda b: (b, 4)),
                  pl.BlockSpec((4, 1, BLK), lambda b: (0, 0, 0)),
                  pl.BlockSpec((4, BLK, 2 * BLK), lambda b: (0, 0, 0))],
        out_specs=pl.BlockSpec((S, GW), lambda b: (b, 0)),
        out_shape=SDS((B * S, GW), f32),
        name="swa_fwd", compiler_params=_cp("parallel"))(proj, proj, sinks, bias)


def _swa_bwd(proj, sinks, bias, dy, B, S):
    def body(q_ref, kv_ref, s_ref, b_ref, dy_ref, dq_ref, dkv_ref, ds_ref, db_ref, acc_ref):
        @pl.when(pl.program_id(0) == 0)
        def _():
            ds_ref[...] = jnp.zeros_like(ds_ref)
            db_ref[...] = jnp.zeros_like(db_ref)
        acc_ref[...] = jnp.zeros_like(acc_ref)

        def block(n, c):
            rows = pl.ds(pl.multiple_of(n * BLK, BLK), BLK)
            prev = pl.ds(pl.multiple_of(jnp.maximum(n - 1, 0) * BLK, BLK), BLK)
            k2 = jnp.concatenate([kv_ref[prev, 0:BLK], kv_ref[rows, 0:BLK]], axis=0)
            v2 = jnp.concatenate([kv_ref[prev, BLK:2 * BLK], kv_ref[rows, BLK:2 * BLK]], axis=0)
            dk2 = jnp.zeros((2 * BLK, BLK), f32)
            dv2 = jnp.zeros((2 * BLK, BLK), f32)
            for p in range(2):
                qb = q_ref[rows, p * BLK:(p + 1) * BLK]
                do = dy_ref[rows, p * BLK:(p + 1) * BLK]
                dq = jnp.zeros((BLK, BLK), f32)
                for g in range(2):
                    h = 2 * p + g
                    fn = functools.partial(_swa_math, n=n, p=p, g=g)
                    _, vjp = jax.vjp(fn, qb, k2, v2, s_ref[h], b_ref[h])
                    dqg, dkg, dvg, dsg, dbg = vjp(do)
                    dq = dq + dqg
                    dk2 = dk2 + dkg
                    dv2 = dv2 + dvg
                    ds_ref[h] += dsg
                    db_ref[h] += dbg
                dq_ref[rows, p * BLK:(p + 1) * BLK] = dq.astype(bf16)
            acc_ref[prev, 0:BLK] += dk2[0:BLK]
            acc_ref[rows, 0:BLK] += dk2[BLK:2 * BLK]
            acc_ref[prev, BLK:2 * BLK] += dv2[0:BLK]
            acc_ref[rows, BLK:2 * BLK] += dv2[BLK:2 * BLK]
            return c
        lax.fori_loop(0, S // BLK, block, 0)
        dkv_ref[...] = acc_ref[...].astype(bf16)

    return pl.pallas_call(
        body, grid=(B,),
        in_specs=[pl.BlockSpec((S, GW), lambda b: (b, 3)),
                  pl.BlockSpec((S, GW), lambda b: (b, 4)),
                  pl.BlockSpec((4, 1, BLK), lambda b: (0, 0, 0)),
                  pl.BlockSpec((4, BLK, 2 * BLK), lambda b: (0, 0, 0)),
                  pl.BlockSpec((S, GW), lambda b: (b, 0))],
        out_specs=[pl.BlockSpec((S, GW), lambda b: (b, 0)),
                   pl.BlockSpec((S, GW), lambda b: (b, 0)),
                   pl.BlockSpec((4, 1, BLK), lambda b: (0, 0, 0)),
                   pl.BlockSpec((4, BLK, 2 * BLK), lambda b: (0, 0, 0))],
        out_shape=[SDS((B * S, GW), bf16), SDS((B * S, GW), bf16), SDS((4, 1, BLK), f32), SDS((4, BLK, 2 * BLK), f32)],
        scratch_shapes=[pltpu.VMEM((S, GW), f32)],
        name="swa_bwd", compiler_params=_cp("arbitrary"))(proj, proj, sinks, bias, dy)


def _log1m(z):
    return jnp.minimum(-z, 0.0) - jnp.log1p(jnp.exp(-jnp.abs(z)))


def _sb_fwd(proj, B, S):
    def body(q_ref, k_ref, v_ref, y_ref, lt_ref):
        ri, ci = _iota((BLK, BLK), 0), _iota((BLK, BLK), 1)
        above = (ri > ci).astype(bf16)
        lh = ci // HD

        def qblock(n, c):
            qrows = pl.ds(pl.multiple_of(n * BLK, BLK), BLK)
            lt = jnp.zeros((BLK, BLK), f32)
            for p in range(2):
                lanes = slice(p * BLK, (p + 1) * BLK)
                q = q_ref[qrows, lanes]
                qg = [_mx(jnp.where(lh == g, q, 0.0)) for g in range(2)]

                def kblock(i, carry):
                    kb = n - i
                    krows = pl.ds(pl.multiple_of(kb * BLK, BLK), BLK)
                    k = _mx(k_ref[krows, lanes])
                    v = _mx(v_ref[krows, lanes])
                    mask = (ci < ri) | (i > 0)
                    out = []
                    for g in range(2):
                        R, acc = carry[2 * g], carry[2 * g + 1]
                        z = lax.dot_general(qg[g], k, NT, preferred_element_type=f32) * (HD ** -0.5)
                        L = jnp.where(mask, _log1m(z), 0.0)
                        a = z + L + R + _split_dot(L, above)
                        w = jnp.where(mask, jnp.exp(a), 0.0)
                        acc = acc + jnp.dot(_mx(w), v, preferred_element_type=f32)
                        out += [R + jnp.sum(L, axis=-1, keepdims=True), acc]
                    return tuple(out)
                z1, z2 = jnp.zeros((BLK, 1), f32), jnp.zeros((BLK, BLK), f32)
                R0, acc0, R1, acc1 = lax.fori_loop(0, n + 1, kblock, (z1, z2, z1, z2))
                y_ref[qrows, lanes] = jnp.where(lh == 0, acc0, acc1)
                lt = lt + jnp.where(ci == 2 * p, R0, 0.0) + jnp.where(ci == 2 * p + 1, R1, 0.0)
            lt_ref[qrows, :] = lt
            return c
        lax.fori_loop(0, S // BLK, qblock, 0)

    spec = lambda j: pl.BlockSpec((S, GW), lambda b: (b, j))
    return pl.pallas_call(
        body, grid=(B,),
        in_specs=[spec(5), spec(6), spec(7)],
        out_specs=[pl.BlockSpec((S, GW), lambda b: (b, 0)), pl.BlockSpec((S, BLK), lambda b: (b, 0))],
        out_shape=[SDS((B * S, GW), f32), SDS((B * S, BLK), f32)],
        name="sb_fwd", compiler_params=_cp("parallel"))(proj, proj, proj)


def _sb_bwd(proj, ltot, dy, B, S):
    def body(q_ref, k_ref, v_ref, lt_ref, dy_ref, dq_ref, dk_ref, dv_ref, dka_ref, dva_ref):
        ri, ci = _iota((BLK, BLK), 0), _iota((BLK, BLK), 1)
        upto = (ri <= ci).astype(bf16)
        below = (ri < ci).astype(bf16)
        lh = ci // HD
        dka_ref[...] = jnp.zeros_like(dka_ref)
        dva_ref[...] = jnp.zeros_like(dva_ref)

        def qblock(n, c):
            qrows = pl.ds(pl.multiple_of(n * BLK, BLK), BLK)
            ltb = lt_ref[qrows, :]
            for p in range(2):
                lanes = slice(p * BLK, (p + 1) * BLK)
                q = q_ref[qrows, lanes]
                do = dy_ref[qrows, lanes]
                qg = [_mx(jnp.where(lh == g, q, 0.0)) for g in range(2)]
                dog = [_mx(jnp.where(lh == g, do, 0.0)) for g in range(2)]
                ltg = [jnp.sum(jnp.where(ci == 2 * p + g, ltb, 0.0), axis=-1, keepdims=True) for g in range(2)]

                def kblock(kb, carry):
                    krows = pl.ds(pl.multiple_of(kb * BLK, BLK), BLK)
                    k = _mx(k_ref[krows, lanes])
                    v = _mx(v_ref[krows, lanes])
                    mask = (ci < ri) | (kb < n)
                    out = []
                    dk_blk = jnp.zeros((BLK, BLK), f32)
                    dv_blk = jnp.zeros((BLK, BLK), f32)
                    for g in range(2):
                        PL, C, dq = carry[3 * g], carry[3 * g + 1], carry[3 * g + 2]
                        z = lax.dot_general(qg[g], k, NT, preferred_element_type=f32) * (HD ** -0.5)
                        L = jnp.where(mask, _log1m(z), 0.0)
                        tail = ltg[g] - PL - _split_dot(L, upto)
                        w = jnp.where(mask, jnp.exp(z + L + tail), 0.0)
                        dw = lax.dot_general(dog[g], v, NT, preferred_element_type=f32)
                        da = w * dw
                        dL = C + _split_dot(da, below)
                        sg = jax.nn.sigmoid(z)
                        dz = _mx(jnp.where(mask, da * (1.0 - sg) - dL * sg, 0.0) * (HD ** -0.5))
                        dq = dq + jnp.dot(dz, k, preferred_element_type=f32)
                        dk_blk = dk_blk + lax.dot_general(dz, qg[g], TN, preferred_element_type=f32)
                        dv_blk = dv_blk + lax.dot_general(_mx(w), dog[g], TN, preferred_element_type=f32)
                        out += [PL + jnp.sum(L, axis=-1, keepdims=True), C + jnp.sum(da, axis=-1, keepdims=True), dq]
                    dka_ref[krows, lanes] += dk_blk
                    dva_ref[krows, lanes] += dv_blk
                    return tuple(out)
                z1, z2 = jnp.zeros((BLK, 1), f32), jnp.zeros((BLK, BLK), f32)
                res = lax.fori_loop(0, n + 1, kblock, (z1, z1, z2, z1, z1, z2))
                dq_ref[qrows, lanes] = jnp.where(lh == 0, res[2], res[5]).astype(bf16)
            return c
        lax.fori_loop(0, S // BLK, qblock, 0)
        dk_ref[...] = dka_ref[...].astype(bf16)
        dv_ref[...] = dva_ref[...].astype(bf16)

    spec = lambda j: pl.BlockSpec((S, GW), lambda b: (b, j))
    o = pl.BlockSpec((S, GW), lambda b: (b, 0))
    return pl.pallas_call(
        body, grid=(B,),
        in_specs=[spec(5), spec(6), spec(7), pl.BlockSpec((S, BLK), lambda b: (b, 0)), o],
        out_specs=[o, o, o],
        out_shape=[SDS((B * S, GW), bf16)] * 3,
        scratch_shapes=[pltpu.VMEM((S, GW), f32), pltpu.VMEM((S, GW), f32)],
        name="sb_bwd", compiler_params=_cp("parallel"))(proj, proj, proj, ltot, dy)


def _bias_expand(rel_bias_t, bucket):
    n = bucket.shape[1]

    def body(r_ref, b_ref, o_ref):
        onehot = (_iota((N_BUCKETS, n), 0) == b_ref[...]).astype(f32)
        o_ref[...] = jnp.dot(r_ref[...], onehot, precision=HIGHEST, preferred_element_type=f32)
    return pl.pallas_call(body, out_shape=SDS((rel_bias_t.shape[0], n), f32), name="bias_expand",
                          compiler_params=_cp())(rel_bias_t, bucket)


def _bias_reduce(dbias, bucket):
    n = bucket.shape[1]

    def body(*refs):
        b_ref, g_ref = refs[-2], refs[-1]
        d = refs[0][...]
        for r in refs[1:-2]:
            d = d + r[...]
        onehot = (_iota((N_BUCKETS, n), 0) == b_ref[...]).astype(f32)
        g_ref[...] = lax.dot_general(d, onehot, NT, precision=HIGHEST, preferred_element_type=f32)
    return pl.pallas_call(body, out_shape=SDS((dbias[0].shape[0], N_BUCKETS), f32), name="bias_reduce",
                          compiler_params=_cp())(*dbias, bucket)


def _adamw(w, g, m, v, tr, name):
    R, C = w.shape

    def body(w_ref, g_ref, m_ref, v_ref, d_ref, m2_ref, v2_ref):
        gv = g_ref[...]
        m2 = ADAM_B1 * m_ref[...] + (1.0 - ADAM_B1) * gv
        v2 = ADAM_B2 * v_ref[...] + (1.0 - ADAM_B2) * (gv * gv)
        m_hat = m2 / (1.0 - ADAM_B1 ** ADAM_STEP)
        v_hat = v2 / (1.0 - ADAM_B2 ** ADAM_STEP)
        d_ref[...] = -ADAM_LR * (m_hat / (jnp.sqrt(v_hat) + ADAM_EPS) + ADAM_WD * w_ref[...])
        m2_ref[...] = m2
        v2_ref[...] = v2

    spec = pl.BlockSpec((tr, C), lambda i: (i, 0))
    return pl.pallas_call(
        body, grid=(R // tr,), in_specs=[spec] * 4, out_specs=[spec] * 3,
        out_shape=[SDS((R, C), f32)] * 3, name=name, compiler_params=_cp("parallel"))(w, g, m, v)


ANY = pl.BlockSpec(memory_space=pl.ANY)


def _place():
    x, y, c = lax.axis_index("x"), lax.axis_index("y"), lax.axis_index("c")
    chips = [(1 - x, y), (x, 1 - y), (1 - x, 1 - y)]
    return x, y, c, chips


def _gather_weights(ws):
    L = ws[0].shape[0]
    nA = len(ws) * L
    out_shape = [SDS((N_CHIPS,) + w.shape[1:], w.dtype) for w in ws for _ in range(L)]

    def body(*refs):
        w_refs, o_refs = refs[:len(ws)], refs[len(ws):len(ws) + nA]
        send1, recv1, send2, recv2, loc = refs[len(ws) + nA:]
        x, y, c, chips = _place()
        k = 2 * x + y
        items = []
        for pi, w in enumerate(ws):
            h = w.shape[1] // 2
            for l in range(L):
                items.append((w_refs[pi].at[l], o_refs[pi * L + l], h))
        local, first, second = [], [], []
        for i, (src, o, h) in enumerate(items):
            cp = pltpu.make_async_copy(src, o.at[k], loc.at[i])
            cp.start()
            local.append(cp)
            mine = pl.ds(c * h, h)
            for j, (cx, cy) in enumerate(chips):
                cp = pltpu.make_async_remote_copy(
                    src_ref=src.at[mine], dst_ref=o.at[k, mine], send_sem=send1.at[3 * i + j], recv_sem=recv1.at[3 * i + j],
                    device_id=(cx, cy, c), device_id_type=MESH)
                cp.start()
                first.append(cp)
        for i, (src, o, h) in enumerate(items):
            mine = pl.ds(c * h, h)
            for j, (cx, cy) in enumerate(chips):
                got = o.at[2 * cx + cy, mine]
                pltpu.make_async_remote_copy(
                    src_ref=got, dst_ref=got, send_sem=send1.at[3 * i + j], recv_sem=recv1.at[3 * i + j],
                    device_id=(cx, cy, c), device_id_type=MESH).wait_recv()
                cp = pltpu.make_async_remote_copy(
                    src_ref=got, dst_ref=got, send_sem=send2.at[3 * i + j], recv_sem=recv2.at[3 * i + j],
                    device_id=(x, y, 1 - c), device_id_type=MESH)
                cp.start()
                second.append(cp)
        for i, (src, o, h) in enumerate(items):
            theirs = pl.ds((1 - c) * h, h)
            for j, (cx, cy) in enumerate(chips):
                got = o.at[2 * cx + cy, theirs]
                pltpu.make_async_remote_copy(
                    src_ref=got, dst_ref=got, send_sem=send2.at[3 * i + j], recv_sem=recv2.at[3 * i + j],
                    device_id=(x, y, 1 - c), device_id_type=MESH).wait_recv()
        for cp in first + second:
            cp.wait_send()
        for cp in local:
            cp.wait()

    return pl.pallas_call(
        body, out_shape=out_shape, in_specs=[ANY] * len(ws), out_specs=[ANY] * nA,
        scratch_shapes=[pltpu.SemaphoreType.DMA((3 * nA,))] * 4 + [pltpu.SemaphoreType.DMA((nA,))],
        name="gather_weights", compiler_params=pltpu.CompilerParams(has_side_effects=True))(*ws)


def _pair_exchange(gs):
    n = len(gs)
    out_shape = [SDS((g.shape[0], g.shape[1] // 2, g.shape[2]), g.dtype) for g in gs]

    def body(*refs):
        g_refs, o_refs = refs[:n], refs[n:2 * n]
        send, recv = refs[2 * n:]
        x, y, c, _ = _place()
        cps = []
        for i, g in enumerate(gs):
            h = g.shape[1] // 2
            cp = pltpu.make_async_remote_copy(
                src_ref=g_refs[i].at[:, pl.ds((1 - c) * h, h)], dst_ref=o_refs[i], send_sem=send.at[i], recv_sem=recv.at[i],
                device_id=(x, y, 1 - c), device_id_type=MESH)
            cp.start()
            cps.append(cp)
        for cp in cps:
            cp.wait()

    return pl.pallas_call(
        body, out_shape=out_shape, in_specs=[ANY] * n, out_specs=[ANY] * n,
        scratch_shapes=[pltpu.SemaphoreType.DMA((n,))] * 2,
        name="grad_pair_exchange", compiler_params=pltpu.CompilerParams(has_side_effects=True))(*gs)


def _pair_add(g, r, cidx, name):
    ns, a, b = g.shape
    h = a // 2
    th = h if h * b * 4 <= 4 * 1024 * 1024 else h // 2

    def body(c_ref, g_ref, r_ref, qf_ref, qb_ref):
        q = g_ref[...] + r_ref[...]
        qf_ref[...] = q
        qb_ref[...] = q.astype(bf16)

    nb = h // th
    spec = pl.BlockSpec((1, th, b), lambda s, i, c_ref: (s, i, 0))
    return pl.pallas_call(
        body,
        grid_spec=pltpu.PrefetchScalarGridSpec(
            num_scalar_prefetch=1, grid=(ns, nb),
            in_specs=[pl.BlockSpec((1, th, b), lambda s, i, c_ref: (s, c_ref[0] * nb + i, 0)), spec],
            out_specs=[spec, spec]),
        out_shape=[SDS((ns, h, b), f32), SDS((ns, h, b), bf16)],
        name=name, compiler_params=_cp("parallel", "parallel"))(cidx, g, r)


def _chip_exchange(qs):
    n = len(qs)
    out_shape = [SDS(q.shape, q.dtype) for q in qs]

    def body(*refs):
        q_refs, o_refs = refs[:n], refs[n:2 * n]
        send, recv, loc = refs[2 * n:]
        x, y, c, chips = _place()
        k = 2 * x + y
        cps, local = [], []
        for i in range(n):
            cp = pltpu.make_async_copy(q_refs[i].at[k], o_refs[i].at[k], loc.at[i])
            cp.start()
            local.append(cp)
            for j, (cx, cy) in enumerate(chips):
                cp = pltpu.make_async_remote_copy(
                    src_ref=q_refs[i].at[2 * cx + cy], dst_ref=o_refs[i].at[k], send_sem=send.at[3 * i + j],
                    recv_sem=recv.at[3 * i + j], device_id=(cx, cy, c), device_id_type=MESH)
                cp.start()
                cps.append(cp)
        for i in range(n):
            for j, (cx, cy) in enumerate(chips):
                got = o_refs[i].at[2 * cx + cy]
                pltpu.make_async_remote_copy(
                    src_ref=got, dst_ref=got, send_sem=send.at[3 * i + j], recv_sem=recv.at[3 * i + j],
                    device_id=(cx, cy, c), device_id_type=MESH).wait_recv()
        for cp in cps:
            cp.wait_send()
        for cp in local:
            cp.wait()

    return pl.pallas_call(
        body, out_shape=out_shape, in_specs=[ANY] * n, out_specs=[ANY] * n,
        scratch_shapes=[pltpu.SemaphoreType.DMA((3 * n,))] * 2 + [pltpu.SemaphoreType.DMA((n,))],
        name="grad_chip_exchange", compiler_params=pltpu.CompilerParams(has_side_effects=True))(*qs)


def _chip_add(qf, r2, kidx, name):
    ns, h, b = r2.shape
    th = h if h * b * 4 <= 4 * 1024 * 1024 else h // 2

    def body(k_ref, qf_ref, r_ref, o_ref):
        k = k_ref[0]
        acc = jnp.zeros(o_ref.shape, f32)
        for j in range(ns):
            acc = acc + jnp.where(k == j, qf_ref[0], r_ref[j].astype(f32))
        o_ref[...] = acc

    return pl.pallas_call(
        body,
        grid_spec=pltpu.PrefetchScalarGridSpec(
            num_scalar_prefetch=1, grid=(h // th,),
            in_specs=[pl.BlockSpec((1, th, b), lambda i, k_ref: (k_ref[0], i, 0)),
                      pl.BlockSpec((ns, th, b), lambda i, k_ref: (0, i, 0))],
            out_specs=pl.BlockSpec((th, b), lambda i, k_ref: (i, 0))),
        out_shape=SDS((h, b), f32),
        name=name, compiler_params=_cp("parallel"))(kidx, qf, r2)


def _pair_share(rs, L):
    n = len(rs)
    nP = n // L
    out_shape = [SDS((L, 2 * rs[pi * L].shape[0], rs[pi * L].shape[1]), f32) for pi in range(nP)]

    def body(*refs):
        r_refs, o_refs = refs[:n], refs[n:n + nP]
        send, recv, loc = refs[n + nP:]
        x, y, c, _ = _place()
        cps, local = [], []
        for i in range(n):
            pi, l = divmod(i, L)
            h = rs[i].shape[0]
            dst = o_refs[pi].at[l, pl.ds(c * h, h)]
            cp = pltpu.make_async_copy(r_refs[i], dst, loc.at[i])
            cp.start()
            local.append(cp)
            cp = pltpu.make_async_remote_copy(
                src_ref=r_refs[i], dst_ref=dst, send_sem=send.at[i], recv_sem=recv.at[i],
                device_id=(x, y, 1 - c), device_id_type=MESH)
            cp.start()
            cps.append(cp)
        for i in range(n):
            pi, l = divmod(i, L)
            h = rs[i].shape[0]
            got = o_refs[pi].at[l, pl.ds((1 - c) * h, h)]
            pltpu.make_async_remote_copy(
                src_ref=got, dst_ref=got, send_sem=send.at[i], recv_sem=recv.at[i],
                device_id=(x, y, 1 - c), device_id_type=MESH).wait_recv()
        for cp in cps:
            cp.wait_send()
        for cp in local:
            cp.wait()

    return pl.pallas_call(
        body, out_shape=out_shape, in_specs=[ANY] * n, out_specs=[ANY] * nP,
        scratch_shapes=[pltpu.SemaphoreType.DMA((n,))] * 3,
        name="grad_pair_share", compiler_params=pltpu.CompilerParams(has_side_effects=True))(*rs)


def _small_allgather(buf):
    R, C = buf.shape

    def body(b_ref, o_ref, send, recv, loc):
        x, y, c, _ = _place()
        me = 4 * x + 2 * y + c
        own = pltpu.make_async_copy(b_ref, o_ref.at[me], loc)
        own.start()
        flips = [(fx, fy, fc) for fx in (0, 1) for fy in (0, 1) for fc in (0, 1)][1:]
        cps = []
        for j, (fx, fy, fc) in enumerate(flips):
            cp = pltpu.make_async_remote_copy(
                src_ref=b_ref, dst_ref=o_ref.at[me], send_sem=send.at[j], recv_sem=recv.at[j],
                device_id=(x ^ fx, y ^ fy, c ^ fc), device_id_type=MESH)
            cp.start()
            cps.append(cp)
        for j, (fx, fy, fc) in enumerate(flips):
            got = o_ref.at[4 * (x ^ fx) + 2 * (y ^ fy) + (c ^ fc)]
            pltpu.make_async_remote_copy(
                src_ref=got, dst_ref=got, send_sem=send.at[j], recv_sem=recv.at[j],
                device_id=(x ^ fx, y ^ fy, c ^ fc), device_id_type=MESH).wait_recv()
        for cp in cps:
            cp.wait_send()
        own.wait()

    return pl.pallas_call(
        body, out_shape=SDS((N_DEV, R, C), f32), in_specs=[ANY], out_specs=ANY,
        scratch_shapes=[pltpu.SemaphoreType.DMA((N_DEV - 1,))] * 2 + [pltpu.SemaphoreType.DMA],
        name="small_allgather", compiler_params=pltpu.CompilerParams(has_side_effects=True))(buf)


def _small_sum(g):
    n, R, C = g.shape

    def body(g_ref, o_ref):
        acc = g_ref[0]
        for j in range(1, n):
            acc = acc + g_ref[j]
        o_ref[...] = acc
    return pl.pallas_call(body, out_shape=SDS((R, C), f32), name="small_sum", compiler_params=_cp())(g)


PACK_COLS = 1024


def _pack(parts):
    flat = jnp.concatenate([p.reshape(-1) for p in parts])
    n = flat.shape[0]
    rows = -(-n // (8 * PACK_COLS)) * 8
    return jnp.pad(flat, (0, rows * PACK_COLS - n)).reshape(rows, PACK_COLS)


def _unpack(buf, shapes):
    flat = buf.reshape(-1)
    out, off = [], 0
    for s in shapes:
        n = int(np.prod(s)) if len(s) else 1
        out.append(flat[off:off + n].reshape(s))
        off += n
    return out


def _block_diag(w):
    g, a, _ = w.shape
    out = jnp.zeros((g * a, g * a), w.dtype)
    for i in range(g):
        out = lax.dynamic_update_slice(out, w[i], (i * a, i * a))
    return out


def kernel(x, w_in, w_out, sgu_w, sgu_b, pool_w, pool_scale, swa_sinks, rel_bias, mix_out_gain, norm_mix, norm_ffn, w_gate_up, w_down, norm_final, loss_target, m_w_in, m_w_out, m_sgu_w, m_sgu_b, m_pool_w, m_pool_scale, m_swa_sinks, m_rel_bias, m_mix_out_gain, m_norm_mix, m_norm_ffn, m_w_gate_up, m_w_down, m_norm_final, v_w_in, v_w_out, v_sgu_w, v_sgu_b, v_pool_w, v_pool_scale, v_swa_sinks, v_rel_bias, v_mix_out_gain, v_norm_mix, v_norm_ffn, v_w_gate_up, v_w_down, v_norm_final):
    B, S, D = x.shape
    T = B * S
    L = w_in.shape[0]
    tm = min(512, T)
    F = w_down.shape[1] * N_CHIPS
    xi, yi, ci = lax.axis_index("x"), lax.axis_index("y"), lax.axis_index("c")
    cidx = jnp.reshape(ci, (1,)).astype(jnp.int32)
    kidx = jnp.reshape(2 * xi + yi, (1,)).astype(jnp.int32)

    big = [w_in, w_out, w_gate_up, w_down]
    gath = _gather_weights([w.astype(bf16) for w in big])
    Win = gath[0:L]
    Wo = [g.reshape(D, D) for g in gath[L:2 * L]]
    Wgu = gath[2 * L:3 * L]
    Wd = [g.reshape(F, D) for g in gath[3 * L:4 * L]]

    bucket = jnp.asarray(_t5_bucket_table().reshape(1, -1))
    bias_tab = _bias_expand(rel_bias.T, bucket).reshape(4, BLK, 2 * BLK)

    row = lambda v: v.reshape(1, -1)
    xc = x.reshape(T, D)
    tgt = loss_target.reshape(T, D)
    saved = []
    for l in range(L):
        h1, proj = _norm_mm(xc, row(norm_mix[l]), Win[l], tm)
        bexp = jnp.repeat(sgu_b[l].T, HD, axis=1)
        wbd = _block_diag(pool_w[l])
        sk = jnp.broadcast_to(swa_sinks[l][:, None, None], (4, 1, BLK))
        ya = _sgu_fwd(proj, sgu_w[l], bexp, B, S)
        yb = _pool_fwd(proj, wbd, row(pool_scale[l]), B, S)
        yc = _swa_fwd(proj, sk, bias_tab, B, S)
        yd, lt = _sb_fwd(proj, B, S)
        ys = (ya, yb, yc, yd)
        ycn, x1 = _gnorm_mm_res(ys, row(mix_out_gain[l]), Wo[l], xc, tm)
        h2, gu, act = _norm_mm_swiglu(x1, row(norm_ffn[l]), Wgu[l], tm)
        x2 = _mm_res(act, Wd[l], x1, tm)
        saved.append((xc, h1, proj, bexp, wbd, sk, ys, lt, ycn, x1, h2, gu, act))
        xc = x2

    dx, g_final, loss_v = _final_loss(xc, row(norm_final), tgt, tm)

    tk = tm
    gW = [[None] * L for _ in range(4)]
    g_sgu_w, g_sgu_b, g_pool_w, g_pool_scale, g_sinks, g_bias = ([None] * L for _ in range(6))
    g_out_gain, g_mix, g_ffn = ([None] * L for _ in range(3))
    for l in reversed(range(L)):
        x0, h1, proj, bexp, wbd, sk, ys, lt, ycn, x1, h2, gu, act = saved[l]
        dgu = _dact(dx, Wd[l], gu, tm)
        gW[3][l] = _dw(act, dx, lambda t, s: (t, 0), D, 1, F // 2, tk, "dw_down").reshape(N_CHIPS, F // N_CHIPS, D)
        gW[2][l] = _dw(h2, dgu, lambda t, s: (s // 2, t, s % 2), F // 2, N_CHIPS, D, tk, "dw_gate_up")
        dx1, g_ffn[l] = _dx_norm_bwd(dgu, lambda i, s: (s // 2, i, s % 2), Wgu[l], x1, row(norm_ffn[l]), dx,
                                     N_CHIPS, tm, "dx_ffn")
        gW[1][l] = _dw(ycn, dx1, lambda t, s: (t, 0), D, 1, D, tk, "dw_out").reshape(N_CHIPS, D // N_CHIPS, D)
        dya, dyb, dyc, dyd, g_out_gain[l] = _dycat(dx1, Wo[l], ys, row(mix_out_gain[l]), tm)
        dpa, g_sgu_w[l], dbf = _sgu_bwd(proj, sgu_w[l], bexp, dya, B, S)
        g_sgu_b[l] = dbf[:, ::HD].T
        dpb, dwbd, dsc = _pool_bwd(proj, wbd, row(pool_scale[l]), dyb, B, S)
        npg = len(POOL_WINDOWS)
        g_pool_w[l] = jnp.stack([dwbd[i * HD:(i + 1) * HD, i * HD:(i + 1) * HD] for i in range(npg)])
        g_pool_scale[l] = dsc[0]
        dcq, dckv, dsk, g_bias[l] = _swa_bwd(proj, sk, bias_tab, dyc, B, S)
        g_sinks[l] = dsk[:, 0, 0] * float(BLK)
        ddq, ddk, ddv = _sb_bwd(proj, lt, dyd, B, S)
        dproj = jnp.concatenate([dpa, dpb, dcq, dckv, ddq, ddk, ddv], axis=1)
        gW[0][l] = _dw(h1, dproj, lambda t, s: (t, s), w_in.shape[2], N_CHIPS, D, tk, "dw_in")
        dx, g_mix[l] = _dx_norm_bwd(dproj, lambda i, s: (i, s), Win[l], x0, row(norm_mix[l]), dx1,
                                    N_CHIPS, tm, "dx_mix")
    grad_x = dx.reshape(B, S, D)

    gs = [gW[pi][l] for pi in range(4) for l in range(L)]
    r1 = _pair_exchange(gs)
    qf, qb = zip(*[_pair_add(g, r, cidx, "grad_pair_add") for g, r in zip(gs, r1)])
    r2 = _chip_exchange(list(qb))
    rs = [_chip_add(a, b, kidx, "grad_chip_add") for a, b in zip(qf, r2)]
    g_big = _pair_share(rs, L)

    g_rel_bias = _bias_reduce([g.reshape(4, -1) for g in g_bias], bucket).T
    small_g = [jnp.stack(g_sgu_w), jnp.stack(g_sgu_b), jnp.stack(g_pool_w), jnp.stack(g_pool_scale), jnp.stack(g_sinks),
               g_rel_bias, jnp.concatenate(g_out_gain), jnp.concatenate(g_mix), jnp.concatenate(g_ffn), g_final[0]]
    small_w = [sgu_w, sgu_b, pool_w, pool_scale, swa_sinks, rel_bias, mix_out_gain, norm_mix, norm_ffn, norm_final]
    small_m = [m_sgu_w, m_sgu_b, m_pool_w, m_pool_scale, m_swa_sinks, m_rel_bias, m_mix_out_gain, m_norm_mix, m_norm_ffn, m_norm_final]
    small_v = [v_sgu_w, v_sgu_b, v_pool_w, v_pool_scale, v_swa_sinks, v_rel_bias, v_mix_out_gain, v_norm_mix, v_norm_ffn, v_norm_final]
    shapes = [w.shape for w in small_w]
    packed = _small_sum(_small_allgather(_pack(small_g + [loss_v[0, 0:1]])))
    *g_small, loss = _unpack(packed, shapes + [()])
    g_small_packed = _pack(g_small)
    ds, ms, vs = _adamw(_pack(small_w), g_small_packed, _pack(small_m), _pack(small_v), g_small_packed.shape[0], "adamw_small")
    d_small, m_small, v_small = _unpack(ds, shapes), _unpack(ms, shapes), _unpack(vs, shapes)

    big_m = [m_w_in, m_w_out, m_w_gate_up, m_w_down]
    big_v = [v_w_in, v_w_out, v_w_gate_up, v_w_down]
    d_big, m_big, v_big = [], [], []
    for w, g, m, v in zip(big, g_big, big_m, big_v):
        two = lambda a: a.reshape(-1, a.shape[-1])
        rows = two(w).shape[0]
        d2, m2, v2 = _adamw(two(w), two(g), two(m), two(v), rows // 8 if rows >= 2048 else rows, "adamw_big")
        d_big.append(d2.reshape(w.shape))
        m_big.append(m2.reshape(w.shape))
        v_big.append(v2.reshape(w.shape))

    def order(bigs, smalls):
        return [bigs[0], bigs[1]] + list(smalls[0:9]) + [bigs[2], bigs[3], smalls[9]]

    return (loss, grad_x, *order(g_big, g_small), *order(d_big, d_small), *order(m_big, m_small), *order(v_big, v_small))
```

```python
import functools

import numpy as np
import jax
import jax.numpy as jnp
from jax import lax
from jax.experimental import pallas as pl
from jax.experimental.pallas import tpu as pltpu

f32 = jnp.float32
bf16 = jnp.bfloat16
_MXU = jnp.bfloat16

EPS = 1e-6
HD = 64
GW = 256
BLK = 128
POOL_WINDOWS = (2, 4, 8, 16)
N_BUCKETS = 32
MAX_DISTANCE = 128
N_CHIPS = 4
N_DEV = 8
VMEM_BYTES_V7X = 64 * 1024 * 1024
VMEM_LIMIT = 48 * 1024 * 1024

ADAM_LR = 0.001
ADAM_B1 = 0.9
ADAM_B2 = 0.999
ADAM_EPS = 1e-08
ADAM_WD = 0.01
ADAM_STEP = 10

SDS = jax.ShapeDtypeStruct
MESH = pl.DeviceIdType.MESH
HIGHEST = lax.Precision.HIGHEST
NT = (((1,), (1,)), ((), ()))
TN = (((0,), (0,)), ((), ()))


def _cp(*sem):
    return pltpu.CompilerParams(dimension_semantics=sem if sem else None, vmem_limit_bytes=VMEM_LIMIT)


def _mx(v):
    return v.astype(_MXU)


def _iota(shape, dim):
    return lax.broadcasted_iota(jnp.int32, shape, dim)


def _split_dot(a, tri):
    hi = a.astype(bf16)
    lo = (a - hi.astype(f32)).astype(bf16)
    return jnp.dot(hi, tri, preferred_element_type=f32) + jnp.dot(lo, tri, preferred_element_type=f32)


def _rms(xv):
    return lax.rsqrt(jnp.mean(xv * xv, axis=-1, keepdims=True) + EPS)


def _norm_mm(x, gain, w, tm):
    T, D = x.shape
    NS, _, ns = w.shape

    def body(x_ref, g_ref, w_ref, h_ref, o_ref):
        @pl.when(pl.program_id(1) == 0)
        def _():
            xv = x_ref[...]
            h_ref[...] = (xv * _rms(xv) * g_ref[...]).astype(bf16)
        o_ref[...] = jnp.dot(_mx(h_ref[...]), w_ref[0], preferred_element_type=f32)

    return pl.pallas_call(
        body, grid=(T // tm, NS),
        in_specs=[pl.BlockSpec((tm, D), lambda i, s: (i, 0)),
                  pl.BlockSpec((1, D), lambda i, s: (0, 0)),
                  pl.BlockSpec((1, D, ns), lambda i, s: (s, 0, 0))],
        out_specs=[pl.BlockSpec((tm, D), lambda i, s: (i, 0)),
                   pl.BlockSpec((tm, ns), lambda i, s: (i, s))],
        out_shape=[SDS((T, D), bf16), SDS((T, NS * ns), f32)],
        name="norm_mm_in", compiler_params=_cp("parallel", "arbitrary"))(x, gain, w)


def _norm_mm_swiglu(x, gain, w, tm):
    T, D = x.shape
    NS, _, ns = w.shape
    half = NS // 2

    def body(x_ref, g_ref, wg_ref, wu_ref, h_ref, gu_ref, a_ref):
        @pl.when(pl.program_id(1) == 0)
        def _():
            xv = x_ref[...]
            h_ref[...] = (xv * _rms(xv) * g_ref[...]).astype(bf16)
        h = _mx(h_ref[...])
        g = jnp.dot(h, wg_ref[0], preferred_element_type=f32)
        u = jnp.dot(h, wu_ref[0], preferred_element_type=f32)
        gu_ref[0] = g.astype(bf16)
        gu_ref[1] = u.astype(bf16)
        a_ref[...] = (jax.nn.silu(g) * u).astype(bf16)

    return pl.pallas_call(
        body, grid=(T // tm, half),
        in_specs=[pl.BlockSpec((tm, D), lambda i, s: (i, 0)),
                  pl.BlockSpec((1, D), lambda i, s: (0, 0)),
                  pl.BlockSpec((1, D, ns), lambda i, s: (s, 0, 0)),
                  pl.BlockSpec((1, D, ns), lambda i, s: (s + half, 0, 0))],
        out_specs=[pl.BlockSpec((tm, D), lambda i, s: (i, 0)),
                   pl.BlockSpec((2, tm, ns), lambda i, s: (0, i, s)),
                   pl.BlockSpec((tm, ns), lambda i, s: (i, s))],
        out_shape=[SDS((T, D), bf16), SDS((2, T, half * ns), bf16), SDS((T, half * ns), bf16)],
        name="norm_mm_swiglu", compiler_params=_cp("parallel", "arbitrary"))(x, gain, w, w)


def _gnorm_mm_res(ys, gain, w, x, tm):
    T, D = x.shape

    def body(ya, yb, yc, yd, g_ref, w_ref, x_ref, yn_ref, o_ref):
        parts = []
        for m, r in enumerate((ya, yb, yc, yd)):
            y = r[...]
            parts.append((y * _rms(y) * g_ref[:, m * GW:(m + 1) * GW]).astype(bf16))
        yn = jnp.concatenate(parts, axis=1)
        yn_ref[...] = yn
        o_ref[...] = x_ref[...] + jnp.dot(_mx(yn), w_ref[...], preferred_element_type=f32)

    yspec = pl.BlockSpec((tm, GW), lambda i: (i, 0))
    return pl.pallas_call(
        body, grid=(T // tm,),
        in_specs=[yspec, yspec, yspec, yspec,
                  pl.BlockSpec((1, D), lambda i: (0, 0)),
                  pl.BlockSpec((D, D), lambda i: (0, 0)),
                  pl.BlockSpec((tm, D), lambda i: (i, 0))],
        out_specs=[pl.BlockSpec((tm, D), lambda i: (i, 0)), pl.BlockSpec((tm, D), lambda i: (i, 0))],
        out_shape=[SDS((T, D), bf16), SDS((T, D), f32)],
        name="gnorm_mm_res", compiler_params=_cp("parallel"))(*ys, gain, w, x)


def _mm_res(a, w, x, tm):
    T, D = x.shape
    K = a.shape[1]

    def body(a_ref, w_ref, x_ref, o_ref):
        o_ref[...] = x_ref[...] + jnp.dot(_mx(a_ref[...]), w_ref[...], preferred_element_type=f32)

    return pl.pallas_call(
        body, grid=(T // tm,),
        in_specs=[pl.BlockSpec((tm, K), lambda i: (i, 0)),
                  pl.BlockSpec((K, D), lambda i: (0, 0)),
                  pl.BlockSpec((tm, D), lambda i: (i, 0))],
        out_specs=pl.BlockSpec((tm, D), lambda i: (i, 0)),
        out_shape=SDS((T, D), f32),
        name="mm_res_down", compiler_params=_cp("parallel"))(a, w, x)


def _final_loss(x, gain, tgt, tm):
    T, D = x.shape

    def body(x_ref, g_ref, t_ref, dx_ref, dg_ref, l_ref):
        @pl.when(pl.program_id(0) == 0)
        def _():
            dg_ref[...] = jnp.zeros_like(dg_ref)
            l_ref[...] = jnp.zeros_like(l_ref)
        xv = x_ref[...]
        g = g_ref[...]
        r = _rms(xv)
        xh = xv * r
        err = xh * g - t_ref[...]
        l_ref[...] += 0.5 * jnp.sum(jnp.mean(err * err, axis=-1, keepdims=True), axis=0, keepdims=True)
        dy = err * (1.0 / D)
        dg_ref[...] += jnp.sum(dy * xh, axis=0, keepdims=True)
        dxh = dy * g
        dx_ref[...] = r * (dxh - xh * jnp.mean(dxh * xh, axis=-1, keepdims=True))

    return pl.pallas_call(
        body, grid=(T // tm,),
        in_specs=[pl.BlockSpec((tm, D), lambda i: (i, 0)),
                  pl.BlockSpec((1, D), lambda i: (0, 0)),
                  pl.BlockSpec((tm, D), lambda i: (i, 0))],
        out_specs=[pl.BlockSpec((tm, D), lambda i: (i, 0)),
                   pl.BlockSpec((1, D), lambda i: (0, 0)),
                   pl.BlockSpec((1, BLK), lambda i: (0, 0))],
        out_shape=[SDS((T, D), f32), SDS((1, D), f32), SDS((1, BLK), f32)],
        name="final_loss", compiler_params=_cp("arbitrary"))(x, gain, tgt)


def _dact(dx, wd, gu, tm):
    T, D = dx.shape
    F = wd.shape[0]
    ns = F // 2

    def body(dx_ref, w_ref, gu_ref, o_ref):
        da = lax.dot_general(_mx(dx_ref[...]), w_ref[...], NT, preferred_element_type=f32)
        g = gu_ref[0].astype(f32)
        u = gu_ref[1].astype(f32)
        sg = jax.nn.sigmoid(g)
        o_ref[0] = (da * u * (sg * (1.0 + g * (1.0 - sg)))).astype(bf16)
        o_ref[1] = (da * (g * sg)).astype(bf16)

    return pl.pallas_call(
        body, grid=(T // tm, 2),
        in_specs=[pl.BlockSpec((tm, D), lambda i, s: (i, 0)),
                  pl.BlockSpec((ns, D), lambda i, s: (s, 0)),
                  pl.BlockSpec((2, tm, ns), lambda i, s: (0, i, s))],
        out_specs=pl.BlockSpec((2, tm, ns), lambda i, s: (0, i, s)),
        out_shape=SDS((2, T, F), bf16),
        name="dact", compiler_params=_cp("parallel", "arbitrary"))(dx, wd, gu)


def _dw(a, b, b_map, ns, NS, tka, tk, name):
    T, Ka = a.shape
    b_block = (tk, ns) if b.ndim == 2 else (1, tk, ns)

    def body(a_ref, b_ref, o_ref):
        bv = b_ref[...] if b.ndim == 2 else b_ref[0]
        part = lax.dot_general(_mx(a_ref[...]), _mx(bv), TN, preferred_element_type=f32)

        @pl.when(pl.program_id(2) == 0)
        def _():
            o_ref[0] = part

        @pl.when(pl.program_id(2) > 0)
        def _():
            o_ref[0] += part

    return pl.pallas_call(
        body, grid=(NS, Ka // tka, T // tk),
        in_specs=[pl.BlockSpec((tk, tka), lambda s, k, t: (t, k)),
                  pl.BlockSpec(b_block, lambda s, k, t: b_map(t, s))],
        out_specs=pl.BlockSpec((1, tka, ns), lambda s, k, t: (s, k, 0)),
        out_shape=SDS((NS, Ka, ns), f32),
        name=name, compiler_params=_cp("parallel", "parallel", "arbitrary"))(a, b)


def _dx_norm_bwd(dy, dy_map, w, x, gain, dxin, nk, tm, name):
    T, D = x.shape
    ns = w.shape[2]
    dy_block = (tm, ns) if dy.ndim == 2 else (1, tm, ns)

    def body(dy_ref, w_ref, x_ref, g_ref, dxin_ref, dx_ref, dg_ref, acc_ref):
        i, s = pl.program_id(0), pl.program_id(1)
        dv = dy_ref[...] if dy.ndim == 2 else dy_ref[0]
        part = lax.dot_general(_mx(dv), w_ref[0], NT, preferred_element_type=f32)

        @pl.when(s == 0)
        def _():
            acc_ref[...] = part

        @pl.when(s > 0)
        def _():
            acc_ref[...] += part

        @pl.when(s == nk - 1)
        def _():
            @pl.when(i == 0)
            def _():
                dg_ref[...] = jnp.zeros_like(dg_ref)
            dh = acc_ref[...]
            xv = x_ref[...]
            r = _rms(xv)
            xh = xv * r
            dg_ref[...] += jnp.sum(dh * xh, axis=0, keepdims=True)
            dxh = dh * g_ref[...]
            dx_ref[...] = dxin_ref[...] + r * (dxh - xh * jnp.mean(dxh * xh, axis=-1, keepdims=True))

    return pl.pallas_call(
        body, grid=(T // tm, nk),
        in_specs=[pl.BlockSpec(dy_block, lambda i, s: dy_map(i, s)),
                  pl.BlockSpec((1, D, ns), lambda i, s: (s, 0, 0)),
                  pl.BlockSpec((tm, D), lambda i, s: (i, 0)),
                  pl.BlockSpec((1, D), lambda i, s: (0, 0)),
                  pl.BlockSpec((tm, D), lambda i, s: (i, 0))],
        out_specs=[pl.BlockSpec((tm, D), lambda i, s: (i, 0)),
                   pl.BlockSpec((1, D), lambda i, s: (0, 0))],
        out_shape=[SDS((T, D), f32), SDS((1, D), f32)],
        scratch_shapes=[pltpu.VMEM((tm, D), f32)],
        name=name, compiler_params=_cp("arbitrary", "arbitrary"))(dy, w, x, gain, dxin)


def _dycat(dx, w, ys, gain, tm):
    T, D = dx.shape

    def body(dx_ref, w_ref, ya, yb, yc, yd, g_ref, da, db, dc, dd, dg_ref):
        @pl.when(pl.program_id(0) == 0)
        def _():
            dg_ref[...] = jnp.zeros_like(dg_ref)
        dyn = lax.dot_general(_mx(dx_ref[...]), w_ref[...], NT, preferred_element_type=f32)
        for m, (r, o) in enumerate(((ya, da), (yb, db), (yc, dc), (yd, dd))):
            cols = slice(m * GW, (m + 1) * GW)
            y = r[...]
            rs = _rms(y)
            yh = y * rs
            d = dyn[:, cols]
            dg_ref[:, cols] += jnp.sum(d * yh, axis=0, keepdims=True)
            dyh = d * g_ref[:, cols]
            o[...] = rs * (dyh - yh * jnp.mean(dyh * yh, axis=-1, keepdims=True))

    yspec = pl.BlockSpec((tm, GW), lambda i: (i, 0))
    return pl.pallas_call(
        body, grid=(T // tm,),
        in_specs=[pl.BlockSpec((tm, D), lambda i: (i, 0)),
                  pl.BlockSpec((D, D), lambda i: (0, 0)),
                  yspec, yspec, yspec, yspec,
                  pl.BlockSpec((1, D), lambda i: (0, 0))],
        out_specs=[yspec, yspec, yspec, yspec, pl.BlockSpec((1, D), lambda i: (0, 0))],
        out_shape=[SDS((T, GW), f32)] * 4 + [SDS((1, D), f32)],
        name="dycat", compiler_params=_cp("arbitrary"))(dx, w, *ys, gain)


def _sgu_consts():
    r, c = _iota((GW, GW), 0), _iota((GW, GW), 1)
    seg = (r // HD == c // HD).astype(f32)
    tr, ts = _iota((BLK, BLK), 0), _iota((BLK, BLK), 1)
    causal = ts <= tr
    lane_head = _iota((BLK, GW), 1) // HD
    return seg, causal, lane_head


def _sgu_chunk(au, av, w, bexp, consts):
    seg, causal, lane_head = consts
    u = jax.nn.gelu(au)
    v = jax.nn.gelu(av)
    mu = jnp.dot(v, seg, precision=HIGHEST, preferred_element_type=f32) * (1.0 / HD)
    vc = v - mu
    var = jnp.dot(vc * vc, seg, precision=HIGHEST, preferred_element_type=f32) * (1.0 / HD)
    vn = _mx(vc * lax.rsqrt(var + EPS))
    mix = bexp
    for h in range(GW // HD):
        wh = _mx(jnp.where(causal, w[h], 0.0))
        mix = mix + jnp.where(lane_head == h, jnp.dot(wh, vn, preferred_element_type=f32), 0.0)
    return u * mix


def _sgu_fwd(proj, w, bexp, B, S):
    def body(au_ref, av_ref, w_ref, b_ref, y_ref):
        consts = _sgu_consts()
        wv, bv = w_ref[...], b_ref[...]

        def chunk(n, c):
            rows = pl.ds(pl.multiple_of(n * BLK, BLK), BLK)
            y_ref[rows, :] = _sgu_chunk(au_ref[rows, :], av_ref[rows, :], wv, bv, consts)
            return c
        lax.fori_loop(0, S // BLK, chunk, 0)

    return pl.pallas_call(
        body, grid=(B,),
        in_specs=[pl.BlockSpec((S, GW), lambda b: (b, 0)),
                  pl.BlockSpec((S, GW), lambda b: (b, 1)),
                  pl.BlockSpec((GW // HD, BLK, BLK), lambda b: (0, 0, 0)),
                  pl.BlockSpec((BLK, GW), lambda b: (0, 0))],
        out_specs=pl.BlockSpec((S, GW), lambda b: (b, 0)),
        out_shape=SDS((B * S, GW), f32),
        name="sgu_fwd", compiler_params=_cp("parallel"))(proj, proj, w, bexp)


def _sgu_bwd(proj, w, bexp, dy, B, S):
    def body(au_ref, av_ref, w_ref, b_ref, dy_ref, dp_ref, dw_ref, db_ref):
        @pl.when(pl.program_id(0) == 0)
        def _():
            dw_ref[...] = jnp.zeros_like(dw_ref)
            db_ref[...] = jnp.zeros_like(db_ref)
        consts = _sgu_consts()
        wv, bv = w_ref[...], b_ref[...]
        fn = lambda au, av, ww, bb: _sgu_chunk(au, av, ww, bb, consts)

        def chunk(n, carry):
            dw_acc, db_acc = carry
            rows = pl.ds(pl.multiple_of(n * BLK, BLK), BLK)
            _, vjp = jax.vjp(fn, au_ref[rows, :], av_ref[rows, :], wv, bv)
            dau, dav, dwc, dbc = vjp(dy_ref[rows, :])
            dp_ref[rows, 0:GW] = dau.astype(bf16)
            dp_ref[rows, GW:2 * GW] = dav.astype(bf16)
            return dw_acc + dwc, db_acc + dbc
        dw_acc, db_acc = lax.fori_loop(0, S // BLK, chunk, (jnp.zeros(wv.shape, f32), jnp.zeros(bv.shape, f32)))
        dw_ref[...] += dw_acc
        db_ref[...] += jnp.dot(db_acc, consts[0], precision=HIGHEST, preferred_element_type=f32)

    return pl.pallas_call(
        body, grid=(B,),
        in_specs=[pl.BlockSpec((S, GW), lambda b: (b, 0)),
                  pl.BlockSpec((S, GW), lambda b: (b, 1)),
                  pl.BlockSpec((GW // HD, BLK, BLK), lambda b: (0, 0, 0)),
                  pl.BlockSpec((BLK, GW), lambda b: (0, 0)),
                  pl.BlockSpec((S, GW), lambda b: (b, 0))],
        out_specs=[pl.BlockSpec((S, 2 * GW), lambda b: (b, 0)),
                   pl.BlockSpec((GW // HD, BLK, BLK), lambda b: (0, 0, 0)),
                   pl.BlockSpec((BLK, GW), lambda b: (0, 0))],
        out_shape=[SDS((B * S, 2 * GW), bf16), SDS((GW // HD, BLK, BLK), f32), SDS((BLK, GW), f32)],
        name="sgu_bwd", compiler_params=_cp("arbitrary"))(proj, proj, w, bexp, dy)


def _pool_parts(p):
    n = p.shape[0]
    r = _iota(p.shape, 0)
    lg = _iota(p.shape, 1) // HD

    def sh(v, k):
        return jnp.where(r >= k, pltpu.roll(v, k, 0), 0.0)
    s2 = p + sh(p, 1)
    s4 = s2 + sh(s2, 2)
    s8 = s4 + sh(s4, 4)
    s16 = s8 + sh(s8, 8)
    ws = jnp.where(lg == 0, s2, jnp.where(lg == 1, s4, jnp.where(lg == 2, s8, s16)))
    wlen = jnp.where(lg == 0, 2, jnp.where(lg == 1, 4, jnp.where(lg == 2, 8, 16)))
    cnt = jnp.minimum(r + 1, wlen).astype(f32)
    del n
    return ws / cnt - p, cnt, lg


def _pool_fwd(proj, wbd, scale, B, S):
    def body(p_ref, w_ref, s_ref, y_ref):
        y, _, _ = _pool_parts(p_ref[...])
        y_ref[...] = jnp.dot(_mx(y), _mx(w_ref[...]), preferred_element_type=f32) * s_ref[...]

    return pl.pallas_call(
        body, grid=(B,),
        in_specs=[pl.BlockSpec((S, GW), lambda b: (b, 2)),
                  pl.BlockSpec((GW, GW), lambda b: (0, 0)),
                  pl.BlockSpec((1, GW), lambda b: (0, 0))],
        out_specs=pl.BlockSpec((S, GW), lambda b: (b, 0)),
        out_shape=SDS((B * S, GW), f32),
        name="pool_fwd", compiler_params=_cp("parallel"))(proj, wbd, scale)


def _pool_bwd(proj, wbd, scale, dy, B, S):
    def body(p_ref, w_ref, s_ref, dy_ref, dp_ref, dw_ref, ds_ref):
        @pl.when(pl.program_id(0) == 0)
        def _():
            dw_ref[...] = jnp.zeros_like(dw_ref)
            ds_ref[...] = jnp.zeros_like(ds_ref)
        y, cnt, lg = _pool_parts(p_ref[...])
        wv = _mx(w_ref[...])
        z = jnp.dot(_mx(y), wv, preferred_element_type=f32)
        dout = dy_ref[...]
        ds_ref[...] += jnp.sum(dout * z, axis=0, keepdims=True)
        dz = _mx(dout * s_ref[...])
        dw_ref[...] += lax.dot_general(_mx(y), dz, TN, preferred_element_type=f32)
        dyv = lax.dot_general(dz, wv, NT, preferred_element_type=f32)
        n = dyv.shape[0]
        r = _iota(dyv.shape, 0)

        def ush(v, k):
            return jnp.where(r < n - k, pltpu.roll(v, n - k, 0), 0.0)
        gq = dyv / cnt
        a2 = gq + ush(gq, 1)
        a4 = a2 + ush(a2, 2)
        a8 = a4 + ush(a4, 4)
        a16 = a8 + ush(a8, 8)
        adj = jnp.where(lg == 0, a2, jnp.where(lg == 1, a4, jnp.where(lg == 2, a8, a16)))
        dp_ref[...] = (adj - dyv).astype(bf16)

    return pl.pallas_call(
        body, grid=(B,),
        in_specs=[pl.BlockSpec((S, GW), lambda b: (b, 2)),
                  pl.BlockSpec((GW, GW), lambda b: (0, 0)),
                  pl.BlockSpec((1, GW), lambda b: (0, 0)),
                  pl.BlockSpec((S, GW), lambda b: (b, 0))],
        out_specs=[pl.BlockSpec((S, GW), lambda b: (b, 0)),
                   pl.BlockSpec((GW, GW), lambda b: (0, 0)),
                   pl.BlockSpec((1, GW), lambda b: (0, 0))],
        out_shape=[SDS((B * S, GW), bf16), SDS((GW, GW), f32), SDS((1, GW), f32)],
        name="pool_bwd", compiler_params=_cp("arbitrary"))(proj, wbd, scale, dy)


def _t5_bucket_table():
    dist = (np.arange(BLK)[:, None] + BLK) - np.arange(2 * BLK)[None, :]
    d = np.clip(dist, 0, BLK - 1)
    max_exact = N_BUCKETS // 2
    df = np.maximum(d, 1).astype(np.float32)
    large = max_exact + (np.log(df / max_exact) / np.float32(np.log(MAX_DISTANCE / max_exact))
                         * (N_BUCKETS - max_exact)).astype(np.int32)
    large = np.minimum(large, N_BUCKETS - 1)
    return np.where(d < max_exact, d, large).astype(np.int32)


def _swa_math(qb, k2, v2, sink, bias, n, p, g):
    ri, ci = _iota((BLK, BLK), 0), _iota((BLK, BLK), 1)
    selq = ((ri - g * HD == ci - p * HD) & (ri >= g * HD) & (ri < (g + 1) * HD)).astype(_MXU)
    selv = ((ci - g * HD == ri - p * HD) & (ci >= g * HD) & (ci < (g + 1) * HD)).astype(_MXU)
    qi, ki = _iota((BLK, 2 * BLK), 0), _iota((BLK, 2 * BLK), 1)
    dist = qi + BLK - ki
    mask = (dist >= 0) & (dist < BLK) & ((ki >= BLK) | (n > 0))
    qs = _mx(jnp.dot(_mx(qb), selq, preferred_element_type=f32))
    z = lax.dot_general(qs, _mx(k2), NT, preferred_element_type=f32) * (HD ** -0.5)
    z = jnp.where(mask, z + bias, -1e30)
    s = jnp.mean(sink, axis=-1, keepdims=True)
    m = jnp.maximum(jnp.max(z, axis=-1, keepdims=True), s)
    e = jnp.exp(z - m)
    pr = e / (jnp.sum(e, axis=-1, keepdims=True) + jnp.exp(s - m))
    vs = _mx(jnp.dot(_mx(v2), selv, preferred_element_type=f32))
    return jnp.dot(_mx(pr), vs, preferred_element_type=f32)


def _swa_fwd(proj, sinks, bias, B, S):
    def body(q_ref, kv_ref, s_ref, b_ref, y_ref):
        def block(n, c):
            rows = pl.ds(pl.multiple_of(n * BLK, BLK), BLK)
            prev = pl.ds(pl.multiple_of(jnp.maximum(n - 1, 0) * BLK, BLK), BLK)
            k2 = jnp.concatenate([kv_ref[prev, 0:BLK], kv_ref[rows, 0:BLK]], axis=0)
            v2 = jnp.concatenate([kv_ref[prev, BLK:2 * BLK], kv_ref[rows, BLK:2 * BLK]], axis=0)
            for p in range(2):
                qb = q_ref[rows, p * BLK:(p + 1) * BLK]
                o = jnp.zeros((BLK, BLK), f32)
                for g in range(2):
                    o = o + _swa_math(qb, k2, v2, s_ref[2 * p + g], b_ref[2 * p + g], n, p, g)
                y_ref[rows, p * BLK:(p + 1) * BLK] = o
            return c
        lax.fori_loop(0, S // BLK, block, 0)

    return pl.pallas_call(
        body, grid=(B,),
        in_specs=[pl.BlockSpec((S, GW), lambda b: (b, 3)),
                  pl.BlockSpec((S, GW), lambda b: (b, 4)),
                  pl.BlockSpec((4, 1, BLK), lambda b: (0, 0, 0)),
                  pl.BlockSpec((4, BLK, 2 * BLK), lambda b: (0, 0, 0))],
        out_specs=pl.BlockSpec((S, GW), lambda b: (b, 0)),
        out_shape=SDS((B * S, GW), f32),
        name="swa_fwd", compiler_params=_cp("parallel"))(proj, proj, sinks, bias)


def _swa_bwd(proj, sinks, bias, dy, B, S):
    def body(q_ref, kv_ref, s_ref, b_ref, dy_ref, dq_ref, dkv_ref, ds_ref, db_ref, acc_ref):
        @pl.when(pl.program_id(0) == 0)
        def _():
            ds_ref[...] = jnp.zeros_like(ds_ref)
            db_ref[...] = jnp.zeros_like(db_ref)
        acc_ref[...] = jnp.zeros_like(acc_ref)

        def block(n, c):
            rows = pl.ds(pl.multiple_of(n * BLK, BLK), BLK)
            prev = pl.ds(pl.multiple_of(jnp.maximum(n - 1, 0) * BLK, BLK), BLK)
            k2 = jnp.concatenate([kv_ref[prev, 0:BLK], kv_ref[rows, 0:BLK]], axis=0)
            v2 = jnp.concatenate([kv_ref[prev, BLK:2 * BLK], kv_ref[rows, BLK:2 * BLK]], axis=0)
            dk2 = jnp.zeros((2 * BLK, BLK), f32)
            dv2 = jnp.zeros((2 * BLK, BLK), f32)
            for p in range(2):
                qb = q_ref[rows, p * BLK:(p + 1) * BLK]
                do = dy_ref[rows, p * BLK:(p + 1) * BLK]
                dq = jnp.zeros((BLK, BLK), f32)
                for g in range(2):
                    h = 2 * p + g
                    fn = functools.partial(_swa_math, n=n, p=p, g=g)
                    _, vjp = jax.vjp(fn, qb, k2, v2, s_ref[h], b_ref[h])
                    dqg, dkg, dvg, dsg, dbg = vjp(do)
                    dq = dq + dqg
                    dk2 = dk2 + dkg
                    dv2 = dv2 + dvg
                    ds_ref[h] += dsg
                    db_ref[h] += dbg
                dq_ref[rows, p * BLK:(p + 1) * BLK] = dq.astype(bf16)
            acc_ref[prev, 0:BLK] += dk2[0:BLK]
            acc_ref[rows, 0:BLK] += dk2[BLK:2 * BLK]
            acc_ref[prev, BLK:2 * BLK] += dv2[0:BLK]
            acc_ref[rows, BLK:2 * BLK] += dv2[BLK:2 * BLK]
            return c
        lax.fori_loop(0, S // BLK, block, 0)
        dkv_ref[...] = acc_ref[...].astype(bf16)

    return pl.pallas_call(
        body, grid=(B,),
        in_specs=[pl.BlockSpec((S, GW), lambda b: (b, 3)),
                  pl.BlockSpec((S, GW), lambda b: (b, 4)),
                  pl.BlockSpec((4, 1, BLK), lambda b: (0, 0, 0)),
                  pl.BlockSpec((4, BLK, 2 * BLK), lambda b: (0, 0, 0)),
                  pl.BlockSpec((S, GW), lambda b: (b, 0))],
        out_specs=[pl.BlockSpec((S, GW), lambda b: (b, 0)),
                   pl.BlockSpec((S, GW), lambda b: (b, 0)),
                   pl.BlockSpec((4, 1, BLK), lambda b: (0, 0, 0)),
                   pl.BlockSpec((4, BLK, 2 * BLK), lambda b: (0, 0, 0))],
        out_shape=[SDS((B * S, GW), bf16), SDS((B * S, GW), bf16), SDS((4, 1, BLK), f32), SDS((4, BLK, 2 * BLK), f32)],
        scratch_shapes=[pltpu.VMEM((S, GW), f32)],
        name="swa_bwd", compiler_params=_cp("arbitrary"))(proj, proj, sinks, bias, dy)


def _log1m(z):
    return jnp.minimum(-z, 0.0) - jnp.log1p(jnp.exp(-jnp.abs(z)))


def _sb_consts(tri):
    r2, c2 = _iota((2 * BLK, 2 * BLK), 0), _iota((2 * BLK, 2 * BLK), 1)
    tri2 = (tri(r2, c2) & (r2 // BLK == c2 // BLK)).astype(bf16)
    ri, ci = _iota((BLK, 2 * BLK), 0), _iota((BLK, 2 * BLK), 1)
    strict2 = (ci % BLK) < ri
    head0 = _iota((BLK, BLK), 1) < HD
    return tri2, strict2, head0


def _sb_load_kv(k_ref, v_ref, kb, head0):
    krows = pl.ds(pl.multiple_of(kb * BLK, BLK), BLK)
    kks, vvs = [], []
    for p in range(2):
        k = k_ref[krows, p * BLK:(p + 1) * BLK]
        v = v_ref[krows, p * BLK:(p + 1) * BLK]
        kks.append(_mx(jnp.concatenate([jnp.where(head0, k, 0.0), jnp.where(head0, 0.0, k)], axis=0)))
        vvs.append(_mx(jnp.concatenate([jnp.where(head0, v, 0.0), jnp.where(head0, 0.0, v)], axis=0)))
    return kks, vvs


def _two_halves(a, b):
    return jnp.concatenate([jnp.broadcast_to(a, (BLK, BLK)), jnp.broadcast_to(b, (BLK, BLK))], axis=1)


def _half_sums(t):
    return jnp.sum(t[:, :BLK], axis=-1, keepdims=True), jnp.sum(t[:, BLK:], axis=-1, keepdims=True)


def _sb_fwd(proj, B, S):
    def body(q_ref, k_ref, v_ref, y_ref, lt_ref):
        ci = _iota((BLK, BLK), 1)
        above2, strict2, head0 = _sb_consts(lambda r, c: r > c)

        def step(qs, kb, carry, diag):
            kks, vvs = _sb_load_kv(k_ref, v_ref, kb, head0)
            zs = [lax.dot_general(qs[p], kks[p], NT, preferred_element_type=f32) for p in range(2)]
            Ls = [_log1m(z) for z in zs]
            if diag:
                Ls = [jnp.where(strict2, L, 0.0) for L in Ls]
            tails = [_split_dot(L, above2) for L in Ls]
            out = []
            for p in range(2):
                R0, R1, acc = carry[3 * p:3 * p + 3]
                w = jnp.exp(zs[p] + Ls[p] + tails[p] + _two_halves(R0, R1))
                if diag:
                    w = jnp.where(strict2, w, 0.0)
                acc = acc + jnp.dot(_mx(w), vvs[p], preferred_element_type=f32)
                s0, s1 = _half_sums(Ls[p])
                out += [R0 + s0, R1 + s1, acc]
            return tuple(out)

        def qblock(n, c):
            qrows = pl.ds(pl.multiple_of(n * BLK, BLK), BLK)
            qs = [_mx(q_ref[qrows, p * BLK:(p + 1) * BLK] * (HD ** -0.5)) for p in range(2)]
            z1, z2 = jnp.zeros((BLK, 1), f32), jnp.zeros((BLK, BLK), f32)
            carry = step(qs, n, (z1, z1, z2, z1, z1, z2), True)
            res = lax.fori_loop(0, n, lambda i, cr: step(qs, n - 1 - i, cr, False), carry)
            lt = jnp.zeros((BLK, BLK), f32)
            for p in range(2):
                y_ref[qrows, p * BLK:(p + 1) * BLK] = res[3 * p + 2]
                lt = lt + jnp.where(ci == 2 * p, res[3 * p], 0.0) + jnp.where(ci == 2 * p + 1, res[3 * p + 1], 0.0)
            lt_ref[qrows, :] = lt
            return c
        lax.fori_loop(0, S // BLK, qblock, 0)

    spec = lambda j: pl.BlockSpec((S, GW), lambda b: (b, j))
    return pl.pallas_call(
        body, grid=(B,),
        in_specs=[spec(5), spec(6), spec(7)],
        out_specs=[pl.BlockSpec((S, GW), lambda b: (b, 0)), pl.BlockSpec((S, BLK), lambda b: (b, 0))],
        out_shape=[SDS((B * S, GW), f32), SDS((B * S, BLK), f32)],
        name="sb_fwd", compiler_params=_cp("parallel"))(proj, proj, proj)


def _sb_bwd(proj, ltot, dy, B, S):
    def body(q_ref, k_ref, v_ref, lt_ref, dy_ref, dq_ref, dk_ref, dv_ref, dka_ref, dva_ref):
        ci = _iota((BLK, BLK), 1)
        upto2, strict2, head0 = _sb_consts(lambda r, c: r <= c)
        below2, _, _ = _sb_consts(lambda r, c: r < c)
        dka_ref[...] = jnp.zeros_like(dka_ref)
        dva_ref[...] = jnp.zeros_like(dva_ref)

        def step(qs, dos, lts, kb, carry, diag):
            krows = pl.ds(pl.multiple_of(kb * BLK, BLK), BLK)
            kks, vvs = _sb_load_kv(k_ref, v_ref, kb, head0)
            zs = [lax.dot_general(qs[p], kks[p], NT, preferred_element_type=f32) for p in range(2)]
            dws = [lax.dot_general(dos[p], vvs[p], NT, preferred_element_type=f32) for p in range(2)]
            Ls = [_log1m(z) for z in zs]
            if diag:
                Ls = [jnp.where(strict2, L, 0.0) for L in Ls]
            pins = [_split_dot(L, upto2) for L in Ls]
            ws, das = [], []
            for p in range(2):
                PL0, PL1 = carry[5 * p], carry[5 * p + 1]
                tail = _two_halves(lts[2 * p] - PL0, lts[2 * p + 1] - PL1) - pins[p]
                w = jnp.exp(zs[p] + Ls[p] + tail)
                if diag:
                    w = jnp.where(strict2, w, 0.0)
                ws.append(w)
                das.append(w * dws[p])
            pexs = [_split_dot(da, below2) for da in das]
            dzs = []
            for p in range(2):
                dL = _two_halves(carry[5 * p + 2], carry[5 * p + 3]) + pexs[p]
                sg = jax.nn.sigmoid(zs[p])
                dz = das[p] * (1.0 - sg) - dL * sg
                if diag:
                    dz = jnp.where(strict2, dz, 0.0)
                dzs.append(_mx(dz))
            dqs = [jnp.dot(dzs[p], kks[p], preferred_element_type=f32) for p in range(2)]
            dks = [lax.dot_general(dzs[p], qs[p], TN, preferred_element_type=f32) for p in range(2)]
            dvs = [lax.dot_general(_mx(ws[p]), dos[p], TN, preferred_element_type=f32) for p in range(2)]
            out = []
            for p in range(2):
                lanes = slice(p * BLK, (p + 1) * BLK)
                dka_ref[krows, lanes] += jnp.where(head0, dks[p][:BLK], dks[p][BLK:])
                dva_ref[krows, lanes] += jnp.where(head0, dvs[p][:BLK], dvs[p][BLK:])
                l0, l1 = _half_sums(Ls[p])
                a0, a1 = _half_sums(das[p])
                out += [carry[5 * p] + l0, carry[5 * p + 1] + l1, carry[5 * p + 2] + a0, carry[5 * p + 3] + a1,
                        carry[5 * p + 4] + dqs[p]]
            return tuple(out)

        def qblock(n, c):
            qrows = pl.ds(pl.multiple_of(n * BLK, BLK), BLK)
            ltb = lt_ref[qrows, :]
            lts = [jnp.sum(jnp.where(ci == h, ltb, 0.0), axis=-1, keepdims=True) for h in range(4)]
            qs = [_mx(q_ref[qrows, p * BLK:(p + 1) * BLK] * (HD ** -0.5)) for p in range(2)]
            dos = [_mx(dy_ref[qrows, p * BLK:(p + 1) * BLK]) for p in range(2)]
            z1, z2 = jnp.zeros((BLK, 1), f32), jnp.zeros((BLK, BLK), f32)
            carry = lax.fori_loop(0, n, lambda kb, cr: step(qs, dos, lts, kb, cr, False), (z1, z1, z1, z1, z2) * 2)
            res = step(qs, dos, lts, n, carry, True)
            for p in range(2):
                dq_ref[qrows, p * BLK:(p + 1) * BLK] = (res[5 * p + 4] * (HD ** -0.5)).astype(bf16)
            return c
        lax.fori_loop(0, S // BLK, qblock, 0)
        dk_ref[...] = dka_ref[...].astype(bf16)
        dv_ref[...] = dva_ref[...].astype(bf16)

    spec = lambda j: pl.BlockSpec((S, GW), lambda b: (b, j))
    o = pl.BlockSpec((S, GW), lambda b: (b, 0))
    return pl.pallas_call(
        body, grid=(B,),
        in_specs=[spec(5), spec(6), spec(7), pl.BlockSpec((S, BLK), lambda b: (b, 0)), o],
        out_specs=[o, o, o],
        out_shape=[SDS((B * S, GW), bf16)] * 3,
        scratch_shapes=[pltpu.VMEM((S, GW), f32), pltpu.VMEM((S, GW), f32)],
        name="sb_bwd", compiler_params=_cp("parallel"))(proj, proj, proj, ltot, dy)


def _bias_expand(rel_bias_t, bucket):
    n = bucket.shape[1]

    def body(r_ref, b_ref, o_ref):
        onehot = (_iota((N_BUCKETS, n), 0) == b_ref[...]).astype(f32)
        o_ref[...] = jnp.dot(r_ref[...], onehot, precision=HIGHEST, preferred_element_type=f32)
    return pl.pallas_call(body, out_shape=SDS((rel_bias_t.shape[0], n), f32), name="bias_expand",
                          compiler_params=_cp())(rel_bias_t, bucket)


def _bias_reduce(dbias, bucket):
    n = bucket.shape[1]

    def body(*refs):
        b_ref, g_ref = refs[-2], refs[-1]
        d = refs[0][...]
        for r in refs[1:-2]:
            d = d + r[...]
        onehot = (_iota((N_BUCKETS, n), 0) == b_ref[...]).astype(f32)
        g_ref[...] = lax.dot_general(d, onehot, NT, precision=HIGHEST, preferred_element_type=f32)
    return pl.pallas_call(body, out_shape=SDS((dbias[0].shape[0], N_BUCKETS), f32), name="bias_reduce",
                          compiler_params=_cp())(*dbias, bucket)


def _adamw(w, g, m, v, tr, name):
    R, C = w.shape

    def body(w_ref, g_ref, m_ref, v_ref, d_ref, m2_ref, v2_ref):
        gv = g_ref[...]
        m2 = ADAM_B1 * m_ref[...] + (1.0 - ADAM_B1) * gv
        v2 = ADAM_B2 * v_ref[...] + (1.0 - ADAM_B2) * (gv * gv)
        m_hat = m2 / (1.0 - ADAM_B1 ** ADAM_STEP)
        v_hat = v2 / (1.0 - ADAM_B2 ** ADAM_STEP)
        d_ref[...] = -ADAM_LR * (m_hat / (jnp.sqrt(v_hat) + ADAM_EPS) + ADAM_WD * w_ref[...])
        m2_ref[...] = m2
        v2_ref[...] = v2

    spec = pl.BlockSpec((tr, C), lambda i: (i, 0))
    return pl.pallas_call(
        body, grid=(R // tr,), in_specs=[spec] * 4, out_specs=[spec] * 3,
        out_shape=[SDS((R, C), f32)] * 3, name=name, compiler_params=_cp("parallel"))(w, g, m, v)


ANY = pl.BlockSpec(memory_space=pl.ANY)


def _place():
    x, y, c = lax.axis_index("x"), lax.axis_index("y"), lax.axis_index("c")
    chips = [(1 - x, y), (x, 1 - y), (1 - x, 1 - y)]
    return x, y, c, chips


def _cast_slots(w, kidx):
    L, a, b = w.shape
    ta = a // 2

    def body(k_ref, *refs):
        for l in range(L):
            refs[L + l][0] = refs[l][0].astype(bf16)

    return pl.pallas_call(
        body,
        grid_spec=pltpu.PrefetchScalarGridSpec(
            num_scalar_prefetch=1, grid=(a // ta,),
            in_specs=[pl.BlockSpec((1, ta, b), functools.partial(lambda i, k_ref, l: (l, i, 0), l=l)) for l in range(L)],
            out_specs=[pl.BlockSpec((1, ta, b), lambda i, k_ref: (k_ref[0], i, 0)) for _ in range(L)]),
        out_shape=[SDS((N_CHIPS, a, b), bf16)] * L,
        name="cast_slots", compiler_params=_cp("parallel"))(kidx, *([w] * L))


def _gather_weights(bufs):
    nA = len(bufs)

    def body(*refs):
        i_refs, o_refs = refs[:nA], refs[nA:2 * nA]
        send1, recv1, send2, recv2 = refs[2 * nA:]
        x, y, c, chips = _place()
        k = 2 * x + y
        first, second = [], []
        for i in range(nA):
            h = bufs[i].shape[1] // 2
            mine = pl.ds(c * h, h)
            for j, (cx, cy) in enumerate(chips):
                cp = pltpu.make_async_remote_copy(
                    src_ref=i_refs[i].at[k, mine], dst_ref=o_refs[i].at[k, mine], send_sem=send1.at[3 * i + j],
                    recv_sem=recv1.at[3 * i + j], device_id=(cx, cy, c), device_id_type=MESH)
                cp.start()
                first.append(cp)
        for i in range(nA):
            h = bufs[i].shape[1] // 2
            mine = pl.ds(c * h, h)
            for j, (cx, cy) in enumerate(chips):
                got = o_refs[i].at[2 * cx + cy, mine]
                pltpu.make_async_remote_copy(
                    src_ref=got, dst_ref=got, send_sem=send1.at[3 * i + j], recv_sem=recv1.at[3 * i + j],
                    device_id=(cx, cy, c), device_id_type=MESH).wait_recv()
                cp = pltpu.make_async_remote_copy(
                    src_ref=got, dst_ref=got, send_sem=send2.at[3 * i + j], recv_sem=recv2.at[3 * i + j],
                    device_id=(x, y, 1 - c), device_id_type=MESH)
                cp.start()
                second.append(cp)
        for i in range(nA):
            h = bufs[i].shape[1] // 2
            theirs = pl.ds((1 - c) * h, h)
            for j, (cx, cy) in enumerate(chips):
                got = o_refs[i].at[2 * cx + cy, theirs]
                pltpu.make_async_remote_copy(
                    src_ref=got, dst_ref=got, send_sem=send2.at[3 * i + j], recv_sem=recv2.at[3 * i + j],
                    device_id=(x, y, 1 - c), device_id_type=MESH).wait_recv()
        for cp in first + second:
            cp.wait_send()

    return pl.pallas_call(
        body, out_shape=[SDS(b.shape, b.dtype) for b in bufs], in_specs=[ANY] * nA, out_specs=[ANY] * nA,
        input_output_aliases={i: i for i in range(nA)},
        scratch_shapes=[pltpu.SemaphoreType.DMA((3 * nA,))] * 4,
        name="gather_weights", compiler_params=pltpu.CompilerParams(has_side_effects=True))(*bufs)


def _pair_exchange(gs):
    n = len(gs)
    out_shape = [SDS((g.shape[0], g.shape[1] // 2, g.shape[2]), g.dtype) for g in gs]

    def body(*refs):
        g_refs, o_refs = refs[:n], refs[n:2 * n]
        send, recv = refs[2 * n:]
        x, y, c, _ = _place()
        cps = []
        for i, g in enumerate(gs):
            h = g.shape[1] // 2
            cp = pltpu.make_async_remote_copy(
                src_ref=g_refs[i].at[:, pl.ds((1 - c) * h, h)], dst_ref=o_refs[i], send_sem=send.at[i], recv_sem=recv.at[i],
                device_id=(x, y, 1 - c), device_id_type=MESH)
            cp.start()
            cps.append(cp)
        for cp in cps:
            cp.wait()

    return pl.pallas_call(
        body, out_shape=out_shape, in_specs=[ANY] * n, out_specs=[ANY] * n,
        scratch_shapes=[pltpu.SemaphoreType.DMA((n,))] * 2,
        name="grad_pair_exchange", compiler_params=pltpu.CompilerParams(has_side_effects=True))(*gs)


def _pair_add(g, r, cidx, name):
    ns, a, b = g.shape
    h = a // 2
    th = h if h * b * 4 <= 4 * 1024 * 1024 else h // 2

    def body(c_ref, g_ref, r_ref, qf_ref, qb_ref):
        q = g_ref[...] + r_ref[...]
        qf_ref[...] = q
        qb_ref[...] = q.astype(bf16)

    nb = h // th
    spec = pl.BlockSpec((1, th, b), lambda s, i, c_ref: (s, i, 0))
    return pl.pallas_call(
        body,
        grid_spec=pltpu.PrefetchScalarGridSpec(
            num_scalar_prefetch=1, grid=(ns, nb),
            in_specs=[pl.BlockSpec((1, th, b), lambda s, i, c_ref: (s, c_ref[0] * nb + i, 0)), spec],
            out_specs=[spec, spec]),
        out_shape=[SDS((ns, h, b), f32), SDS((ns, h, b), bf16)],
        name=name, compiler_params=_cp("parallel", "parallel"))(cidx, g, r)


def _chip_exchange(qs):
    n = len(qs)
    out_shape = [SDS(q.shape, q.dtype) for q in qs]

    def body(*refs):
        q_refs, o_refs = refs[:n], refs[n:2 * n]
        send, recv = refs[2 * n:]
        x, y, c, chips = _place()
        k = 2 * x + y
        cps = []
        for i in range(n):
            for j, (cx, cy) in enumerate(chips):
                cp = pltpu.make_async_remote_copy(
                    src_ref=q_refs[i].at[2 * cx + cy], dst_ref=o_refs[i].at[k], send_sem=send.at[3 * i + j],
                    recv_sem=recv.at[3 * i + j], device_id=(cx, cy, c), device_id_type=MESH)
                cp.start()
                cps.append(cp)
        for i in range(n):
            for j, (cx, cy) in enumerate(chips):
                got = o_refs[i].at[2 * cx + cy]
                pltpu.make_async_remote_copy(
                    src_ref=got, dst_ref=got, send_sem=send.at[3 * i + j], recv_sem=recv.at[3 * i + j],
                    device_id=(cx, cy, c), device_id_type=MESH).wait_recv()
        for cp in cps:
            cp.wait_send()

    return pl.pallas_call(
        body, out_shape=out_shape, in_specs=[ANY] * n, out_specs=[ANY] * n,
        scratch_shapes=[pltpu.SemaphoreType.DMA((3 * n,))] * 2,
        name="grad_chip_exchange", compiler_params=pltpu.CompilerParams(has_side_effects=True))(*qs)


def _chip_add(qf, r2, idx, prev, L, name):
    ns, h, b = r2.shape
    th = h if h * b * 4 <= 4 * 1024 * 1024 else h // 2
    nb = h // th

    def body(s_ref, qf_ref, r1_ref, r2_ref, r3_ref, *rest):
        o_ref = rest[-1]
        o_ref[0] = qf_ref[0] + r1_ref[0].astype(f32) + r2_ref[0].astype(f32) + r3_ref[0].astype(f32)

    other = lambda d: pl.BlockSpec((1, th, b), lambda i, s_ref: ((s_ref[0] + d) % ns, i, 0))
    in_specs = [pl.BlockSpec((1, th, b), lambda i, s_ref: (s_ref[0], i, 0)), other(1), other(2), other(3)]
    args = [idx, qf, r2, r2, r2]
    aliases = {}
    if prev is not None:
        in_specs.append(ANY)
        args.append(prev)
        aliases = {5: 0}
    return pl.pallas_call(
        body,
        grid_spec=pltpu.PrefetchScalarGridSpec(
            num_scalar_prefetch=1, grid=(nb,), in_specs=in_specs,
            out_specs=pl.BlockSpec((1, th, b), lambda i, s_ref: (s_ref[2], s_ref[1] * nb + i, 0))),
        out_shape=SDS((L, 2 * h, b), f32), input_output_aliases=aliases,
        name=name, compiler_params=_cp("arbitrary"))(*args)


def _pair_share(gs, hs):
    n = len(gs)
    L = gs[0].shape[0]

    def body(*refs):
        i_refs, o_refs = refs[:n], refs[n:2 * n]
        send, recv = refs[2 * n:]
        x, y, c, _ = _place()
        cps = []
        for i in range(n):
            for l in range(L):
                mine = pl.ds(c * hs[i], hs[i])
                cp = pltpu.make_async_remote_copy(
                    src_ref=i_refs[i].at[l, mine], dst_ref=o_refs[i].at[l, mine], send_sem=send.at[i * L + l],
                    recv_sem=recv.at[i * L + l], device_id=(x, y, 1 - c), device_id_type=MESH)
                cp.start()
                cps.append(cp)
        for i in range(n):
            for l in range(L):
                got = o_refs[i].at[l, pl.ds((1 - c) * hs[i], hs[i])]
                pltpu.make_async_remote_copy(
                    src_ref=got, dst_ref=got, send_sem=send.at[i * L + l], recv_sem=recv.at[i * L + l],
                    device_id=(x, y, 1 - c), device_id_type=MESH).wait_recv()
        for cp in cps:
            cp.wait_send()

    return pl.pallas_call(
        body, out_shape=[SDS(g.shape, g.dtype) for g in gs], in_specs=[ANY] * n, out_specs=[ANY] * n,
        input_output_aliases={i: i for i in range(n)},
        scratch_shapes=[pltpu.SemaphoreType.DMA((n * L,))] * 2,
        name="grad_pair_share", compiler_params=pltpu.CompilerParams(has_side_effects=True))(*gs)


def _small_allgather(buf):
    R, C = buf.shape

    def body(b_ref, o_ref, send, recv, loc):
        x, y, c, _ = _place()
        me = 4 * x + 2 * y + c
        own = pltpu.make_async_copy(b_ref, o_ref.at[me], loc)
        own.start()
        flips = [(fx, fy, fc) for fx in (0, 1) for fy in (0, 1) for fc in (0, 1)][1:]
        cps = []
        for j, (fx, fy, fc) in enumerate(flips):
            cp = pltpu.make_async_remote_copy(
                src_ref=b_ref, dst_ref=o_ref.at[me], send_sem=send.at[j], recv_sem=recv.at[j],
                device_id=(x ^ fx, y ^ fy, c ^ fc), device_id_type=MESH)
            cp.start()
            cps.append(cp)
        for j, (fx, fy, fc) in enumerate(flips):
            got = o_ref.at[4 * (x ^ fx) + 2 * (y ^ fy) + (c ^ fc)]
            pltpu.make_async_remote_copy(
                src_ref=got, dst_ref=got, send_sem=send.at[j], recv_sem=recv.at[j],
                device_id=(x ^ fx, y ^ fy, c ^ fc), device_id_type=MESH).wait_recv()
        for cp in cps:
            cp.wait_send()
        own.wait()

    return pl.pallas_call(
        body, out_shape=SDS((N_DEV, R, C), f32), in_specs=[ANY], out_specs=ANY,
        scratch_shapes=[pltpu.SemaphoreType.DMA((N_DEV - 1,))] * 2 + [pltpu.SemaphoreType.DMA],
        name="small_allgather", compiler_params=pltpu.CompilerParams(has_side_effects=True))(buf)


def _small_sum(g):
    n, R, C = g.shape

    def body(g_ref, o_ref):
        acc = g_ref[0]
        for j in range(1, n):
            acc = acc + g_ref[j]
        o_ref[...] = acc
    return pl.pallas_call(body, out_shape=SDS((R, C), f32), name="small_sum", compiler_params=_cp())(g)


PACK_COLS = 1024


def _rows_of(shape):
    n = int(np.prod(shape)) if len(shape) else 1
    return -(-n // (8 * PACK_COLS)) * 8


def _pack(parts):
    blocks = []
    for p in parts:
        flat = p.reshape(-1)
        r = _rows_of(p.shape)
        blocks.append(jnp.pad(flat, (0, r * PACK_COLS - flat.shape[0])).reshape(r, PACK_COLS))
    return jnp.concatenate(blocks, axis=0)


def _unpack(buf, shapes):
    out, off = [], 0
    for s in shapes:
        n = int(np.prod(s)) if len(s) else 1
        r = _rows_of(s)
        out.append(buf[off:off + r].reshape(-1)[:n].reshape(s))
        off += r
    return out


def _block_diag(w):
    g, a, _ = w.shape
    out = jnp.zeros((g * a, g * a), w.dtype)
    for i in range(g):
        out = lax.dynamic_update_slice(out, w[i], (i * a, i * a))
    return out


def kernel(x, w_in, w_out, sgu_w, sgu_b, pool_w, pool_scale, swa_sinks, rel_bias, mix_out_gain, norm_mix, norm_ffn, w_gate_up, w_down, norm_final, loss_target, m_w_in, m_w_out, m_sgu_w, m_sgu_b, m_pool_w, m_pool_scale, m_swa_sinks, m_rel_bias, m_mix_out_gain, m_norm_mix, m_norm_ffn, m_w_gate_up, m_w_down, m_norm_final, v_w_in, v_w_out, v_sgu_w, v_sgu_b, v_pool_w, v_pool_scale, v_swa_sinks, v_rel_bias, v_mix_out_gain, v_norm_mix, v_norm_ffn, v_w_gate_up, v_w_down, v_norm_final):
    B, S, D = x.shape
    T = B * S
    L = w_in.shape[0]
    tm = min(512, T)
    F = w_down.shape[1] * N_CHIPS
    xi, yi, ci = lax.axis_index("x"), lax.axis_index("y"), lax.axis_index("c")
    cidx = jnp.reshape(ci, (1,)).astype(jnp.int32)
    kidx = jnp.reshape(2 * xi + yi, (1,)).astype(jnp.int32)

    big = [w_in, w_out, w_gate_up, w_down]
    gath = _gather_weights([b for w in big for b in _cast_slots(w, kidx)])
    Win = gath[0:L]
    Wo = [g.reshape(D, D) for g in gath[L:2 * L]]
    Wgu = gath[2 * L:3 * L]
    Wd = [g.reshape(F, D) for g in gath[3 * L:4 * L]]

    bucket = jnp.asarray(_t5_bucket_table().reshape(1, -1))
    bias_tab = _bias_expand(rel_bias.T, bucket).reshape(4, BLK, 2 * BLK)

    row = lambda v: v.reshape(1, -1)
    xc = x.reshape(T, D)
    tgt = loss_target.reshape(T, D)
    saved = []
    for l in range(L):
        h1, proj = _norm_mm(xc, row(norm_mix[l]), Win[l], tm)
        bexp = jnp.repeat(sgu_b[l].T, HD, axis=1)
        wbd = _block_diag(pool_w[l])
        sk = jnp.broadcast_to(swa_sinks[l][:, None, None], (4, 1, BLK))
        ya = _sgu_fwd(proj, sgu_w[l], bexp, B, S)
        yb = _pool_fwd(proj, wbd, row(pool_scale[l]), B, S)
        yc = _swa_fwd(proj, sk, bias_tab, B, S)
        yd, lt = _sb_fwd(proj, B, S)
        ys = (ya, yb, yc, yd)
        ycn, x1 = _gnorm_mm_res(ys, row(mix_out_gain[l]), Wo[l], xc, tm)
        h2, gu, act = _norm_mm_swiglu(x1, row(norm_ffn[l]), Wgu[l], tm)
        x2 = _mm_res(act, Wd[l], x1, tm)
        saved.append((xc, h1, proj, bexp, wbd, sk, ys, lt, ycn, x1, h2, gu, act))
        xc = x2

    dx, g_final, loss_v = _final_loss(xc, row(norm_final), tgt, tm)

    tk = tm
    gW = [[None] * L for _ in range(4)]
    g_sgu_w, g_sgu_b, g_pool_w, g_pool_scale, g_sinks, g_bias = ([None] * L for _ in range(6))
    g_out_gain, g_mix, g_ffn = ([None] * L for _ in range(3))
    for l in reversed(range(L)):
        x0, h1, proj, bexp, wbd, sk, ys, lt, ycn, x1, h2, gu, act = saved[l]
        dgu = _dact(dx, Wd[l], gu, tm)
        gW[3][l] = _dw(act, dx, lambda t, s: (t, 0), D, 1, F // 2, tk, "dw_down").reshape(N_CHIPS, F // N_CHIPS, D)
        gW[2][l] = _dw(h2, dgu, lambda t, s: (s // 2, t, s % 2), F // 2, N_CHIPS, D, tk, "dw_gate_up")
        dx1, g_ffn[l] = _dx_norm_bwd(dgu, lambda i, s: (s // 2, i, s % 2), Wgu[l], x1, row(norm_ffn[l]), dx,
                                     N_CHIPS, tm, "dx_ffn")
        gW[1][l] = _dw(ycn, dx1, lambda t, s: (t, 0), D, 1, D, tk, "dw_out").reshape(N_CHIPS, D // N_CHIPS, D)
        dya, dyb, dyc, dyd, g_out_gain[l] = _dycat(dx1, Wo[l], ys, row(mix_out_gain[l]), tm)
        dpa, g_sgu_w[l], dbf = _sgu_bwd(proj, sgu_w[l], bexp, dya, B, S)
        g_sgu_b[l] = dbf[:, ::HD].T
        dpb, dwbd, dsc = _pool_bwd(proj, wbd, row(pool_scale[l]), dyb, B, S)
        npg = len(POOL_WINDOWS)
        g_pool_w[l] = jnp.stack([dwbd[i * HD:(i + 1) * HD, i * HD:(i + 1) * HD] for i in range(npg)])
        g_pool_scale[l] = dsc[0]
        dcq, dckv, dsk, g_bias[l] = _swa_bwd(proj, sk, bias_tab, dyc, B, S)
        g_sinks[l] = dsk[:, 0, 0] * float(BLK)
        ddq, ddk, ddv = _sb_bwd(proj, lt, dyd, B, S)
        dproj = jnp.concatenate([dpa, dpb, dcq, dckv, ddq, ddk, ddv], axis=1)
        gW[0][l] = _dw(h1, dproj, lambda t, s: (t, s), w_in.shape[2], N_CHIPS, D, tk, "dw_in")
        dx, g_mix[l] = _dx_norm_bwd(dproj, lambda i, s: (i, s), Win[l], x0, row(norm_mix[l]), dx1,
                                    N_CHIPS, tm, "dx_mix")
    grad_x = dx.reshape(B, S, D)

    gs = [gW[pi][l] for pi in range(4) for l in range(L)]
    r1 = _pair_exchange(gs)
    qf, qb = zip(*[_pair_add(g, r, cidx, "grad_pair_add") for g, r in zip(gs, r1)])
    r2 = _chip_exchange(list(qb))
    reduced = []
    for pi in range(4):
        acc = None
        for l in range(L):
            idx = jnp.stack([2 * xi + yi, ci, jnp.int32(l)]).astype(jnp.int32)
            acc = _chip_add(qf[pi * L + l], r2[pi * L + l], idx, acc, L, "grad_chip_add")
        reduced.append(acc)
    g_big = _pair_share(reduced, [g.shape[1] // 2 for g in reduced])

    g_rel_bias = _bias_reduce([g.reshape(4, -1) for g in g_bias], bucket).T
    small_g = [jnp.stack(g_sgu_w), jnp.stack(g_sgu_b), jnp.stack(g_pool_w), jnp.stack(g_pool_scale), jnp.stack(g_sinks),
               g_rel_bias, jnp.concatenate(g_out_gain), jnp.concatenate(g_mix), jnp.concatenate(g_ffn), g_final[0]]
    small_w = [sgu_w, sgu_b, pool_w, pool_scale, swa_sinks, rel_bias, mix_out_gain, norm_mix, norm_ffn, norm_final]
    small_m = [m_sgu_w, m_sgu_b, m_pool_w, m_pool_scale, m_swa_sinks, m_rel_bias, m_mix_out_gain, m_norm_mix, m_norm_ffn, m_norm_final]
    small_v = [v_sgu_w, v_sgu_b, v_pool_w, v_pool_scale, v_swa_sinks, v_rel_bias, v_mix_out_gain, v_norm_mix, v_norm_ffn, v_norm_final]
    shapes = [w.shape for w in small_w]
    packed = _small_sum(_small_allgather(_pack(small_g + [loss_v[0, 0:1]])))
    *g_small, loss = _unpack(packed, shapes + [()])
    g_small_packed = _pack(g_small)
    ds, ms, vs = _adamw(_pack(small_w), g_small_packed, _pack(small_m), _pack(small_v), g_small_packed.shape[0], "adamw_small")
    d_small, m_small, v_small = _unpack(ds, shapes), _unpack(ms, shapes), _unpack(vs, shapes)

    big_m = [m_w_in, m_w_out, m_w_gate_up, m_w_down]
    big_v = [v_w_in, v_w_out, v_w_gate_up, v_w_down]
    d_big, m_big, v_big = [], [], []
    for w, g, m, v in zip(big, g_big, big_m, big_v):
        two = lambda a: a.reshape(-1, a.shape[-1])
        rows = two(w).shape[0]
        d2, m2, v2 = _adamw(two(w), two(g), two(m), two(v), rows // 8 if rows >= 2048 else rows, "adamw_big")
        d_big.append(d2.reshape(w.shape))
        m_big.append(m2.reshape(w.shape))
        v_big.append(v2.reshape(w.shape))

    def order(bigs, smalls):
        return [bigs[0], bigs[1]] + list(smalls[0:9]) + [bigs[2], bigs[3], smalls[9]]

    return (loss, grad_x, *order(g_big, g_small), *order(d_big, d_small), *order(m_big, m_small), *order(v_big, v_small))
```

```python
import functools

import numpy as np
import jax
import jax.numpy as jnp
from jax import lax
from jax.experimental import pallas as pl
from jax.experimental.pallas import tpu as pltpu

f32 = jnp.float32
bf16 = jnp.bfloat16
_MXU = jnp.bfloat16

EPS = 1e-6
HD = 64
GW = 256
BLK = 128
POOL_WINDOWS = (2, 4, 8, 16)
N_BUCKETS = 32
MAX_DISTANCE = 128
N_CHIPS = 4
N_DEV = 8
VMEM_BYTES_V7X = 64 * 1024 * 1024
VMEM_LIMIT = 48 * 1024 * 1024

ADAM_LR = 0.001
ADAM_B1 = 0.9
ADAM_B2 = 0.999
ADAM_EPS = 1e-08
ADAM_WD = 0.01
ADAM_STEP = 10

SDS = jax.ShapeDtypeStruct
MESH = pl.DeviceIdType.MESH
HIGHEST = lax.Precision.HIGHEST
NT = (((1,), (1,)), ((), ()))
TN = (((0,), (0,)), ((), ()))


def _cp(*sem):
    return pltpu.CompilerParams(dimension_semantics=sem if sem else None, vmem_limit_bytes=VMEM_LIMIT)


def _mx(v):
    return v.astype(_MXU)


def _iota(shape, dim):
    return lax.broadcasted_iota(jnp.int32, shape, dim)


def _split_dot(a, tri):
    hi = a.astype(bf16)
    lo = (a - hi.astype(f32)).astype(bf16)
    return jnp.dot(hi, tri, preferred_element_type=f32) + jnp.dot(lo, tri, preferred_element_type=f32)


def _rms(xv):
    return lax.rsqrt(jnp.mean(xv * xv, axis=-1, keepdims=True) + EPS)


def _norm_mm(x, gain, w, tm):
    T, D = x.shape
    NS, _, ns = w.shape

    def body(x_ref, g_ref, w_ref, h_ref, o_ref):
        @pl.when(pl.program_id(1) == 0)
        def _():
            xv = x_ref[...]
            h_ref[...] = (xv * _rms(xv) * g_ref[...]).astype(bf16)
        o_ref[...] = jnp.dot(_mx(h_ref[...]), w_ref[0], preferred_element_type=f32)

    return pl.pallas_call(
        body, grid=(T // tm, NS),
        in_specs=[pl.BlockSpec((tm, D), lambda i, s: (i, 0)),
                  pl.BlockSpec((1, D), lambda i, s: (0, 0)),
                  pl.BlockSpec((1, D, ns), lambda i, s: (s, 0, 0))],
        out_specs=[pl.BlockSpec((tm, D), lambda i, s: (i, 0)),
                   pl.BlockSpec((tm, ns), lambda i, s: (i, s))],
        out_shape=[SDS((T, D), bf16), SDS((T, NS * ns), f32)],
        name="norm_mm_in", compiler_params=_cp("parallel", "arbitrary"))(x, gain, w)


def _norm_mm_swiglu(x, gain, w, tm):
    T, D = x.shape
    NS, _, ns = w.shape
    half = NS // 2

    def body(x_ref, g_ref, wg_ref, wu_ref, h_ref, gu_ref, a_ref):
        @pl.when(pl.program_id(1) == 0)
        def _():
            xv = x_ref[...]
            h_ref[...] = (xv * _rms(xv) * g_ref[...]).astype(bf16)
        h = _mx(h_ref[...])
        g = jnp.dot(h, wg_ref[0], preferred_element_type=f32)
        u = jnp.dot(h, wu_ref[0], preferred_element_type=f32)
        gu_ref[0] = g.astype(bf16)
        gu_ref[1] = u.astype(bf16)
        a_ref[...] = (jax.nn.silu(g) * u).astype(bf16)

    return pl.pallas_call(
        body, grid=(T // tm, half),
        in_specs=[pl.BlockSpec((tm, D), lambda i, s: (i, 0)),
                  pl.BlockSpec((1, D), lambda i, s: (0, 0)),
                  pl.BlockSpec((1, D, ns), lambda i, s: (s, 0, 0)),
                  pl.BlockSpec((1, D, ns), lambda i, s: (s + half, 0, 0))],
        out_specs=[pl.BlockSpec((tm, D), lambda i, s: (i, 0)),
                   pl.BlockSpec((2, tm, ns), lambda i, s: (0, i, s)),
                   pl.BlockSpec((tm, ns), lambda i, s: (i, s))],
        out_shape=[SDS((T, D), bf16), SDS((2, T, half * ns), bf16), SDS((T, half * ns), bf16)],
        name="norm_mm_swiglu", compiler_params=_cp("parallel", "arbitrary"))(x, gain, w, w)


def _gnorm_mm_res(ys, gain, w, x, tm):
    T, D = x.shape

    def body(ya, yb, yc, yd, g_ref, w_ref, x_ref, yn_ref, o_ref):
        parts = []
        for m, r in enumerate((ya, yb, yc, yd)):
            y = r[...]
            parts.append((y * _rms(y) * g_ref[:, m * GW:(m + 1) * GW]).astype(bf16))
        yn = jnp.concatenate(parts, axis=1)
        yn_ref[...] = yn
        o_ref[...] = x_ref[...] + jnp.dot(_mx(yn), w_ref[...], preferred_element_type=f32)

    yspec = pl.BlockSpec((tm, GW), lambda i: (i, 0))
    return pl.pallas_call(
        body, grid=(T // tm,),
        in_specs=[yspec, yspec, yspec, yspec,
                  pl.BlockSpec((1, D), lambda i: (0, 0)),
                  pl.BlockSpec((D, D), lambda i: (0, 0)),
                  pl.BlockSpec((tm, D), lambda i: (i, 0))],
        out_specs=[pl.BlockSpec((tm, D), lambda i: (i, 0)), pl.BlockSpec((tm, D), lambda i: (i, 0))],
        out_shape=[SDS((T, D), bf16), SDS((T, D), f32)],
        name="gnorm_mm_res", compiler_params=_cp("parallel"))(*ys, gain, w, x)


def _mm_res(a, w, x, tm):
    T, D = x.shape
    K = a.shape[1]

    def body(a_ref, w_ref, x_ref, o_ref):
        o_ref[...] = x_ref[...] + jnp.dot(_mx(a_ref[...]), w_ref[...], preferred_element_type=f32)

    return pl.pallas_call(
        body, grid=(T // tm,),
        in_specs=[pl.BlockSpec((tm, K), lambda i: (i, 0)),
                  pl.BlockSpec((K, D), lambda i: (0, 0)),
                  pl.BlockSpec((tm, D), lambda i: (i, 0))],
        out_specs=pl.BlockSpec((tm, D), lambda i: (i, 0)),
        out_shape=SDS((T, D), f32),
        name="mm_res_down", compiler_params=_cp("parallel"))(a, w, x)


def _final_loss(x, gain, tgt, tm):
    T, D = x.shape

    def body(x_ref, g_ref, t_ref, dx_ref, dg_ref, l_ref):
        @pl.when(pl.program_id(0) == 0)
        def _():
            dg_ref[...] = jnp.zeros_like(dg_ref)
            l_ref[...] = jnp.zeros_like(l_ref)
        xv = x_ref[...]
        g = g_ref[...]
        r = _rms(xv)
        xh = xv * r
        err = xh * g - t_ref[...]
        l_ref[...] += 0.5 * jnp.sum(jnp.mean(err * err, axis=-1, keepdims=True), axis=0, keepdims=True)
        dy = err * (1.0 / D)
        dg_ref[...] += jnp.sum(dy * xh, axis=0, keepdims=True)
        dxh = dy * g
        dx_ref[...] = r * (dxh - xh * jnp.mean(dxh * xh, axis=-1, keepdims=True))

    return pl.pallas_call(
        body, grid=(T // tm,),
        in_specs=[pl.BlockSpec((tm, D), lambda i: (i, 0)),
                  pl.BlockSpec((1, D), lambda i: (0, 0)),
                  pl.BlockSpec((tm, D), lambda i: (i, 0))],
        out_specs=[pl.BlockSpec((tm, D), lambda i: (i, 0)),
                   pl.BlockSpec((1, D), lambda i: (0, 0)),
                   pl.BlockSpec((1, BLK), lambda i: (0, 0))],
        out_shape=[SDS((T, D), f32), SDS((1, D), f32), SDS((1, BLK), f32)],
        name="final_loss", compiler_params=_cp("arbitrary"))(x, gain, tgt)


def _dact(dx, wd, gu, tm):
    T, D = dx.shape
    F = wd.shape[0]
    ns = F // 2

    def body(dx_ref, w_ref, gu_ref, o_ref):
        da = lax.dot_general(_mx(dx_ref[...]), w_ref[...], NT, preferred_element_type=f32)
        g = gu_ref[0].astype(f32)
        u = gu_ref[1].astype(f32)
        sg = jax.nn.sigmoid(g)
        o_ref[0] = (da * u * (sg * (1.0 + g * (1.0 - sg)))).astype(bf16)
        o_ref[1] = (da * (g * sg)).astype(bf16)

    return pl.pallas_call(
        body, grid=(T // tm, 2),
        in_specs=[pl.BlockSpec((tm, D), lambda i, s: (i, 0)),
                  pl.BlockSpec((ns, D), lambda i, s: (s, 0)),
                  pl.BlockSpec((2, tm, ns), lambda i, s: (0, i, s))],
        out_specs=pl.BlockSpec((2, tm, ns), lambda i, s: (0, i, s)),
        out_shape=SDS((2, T, F), bf16),
        name="dact", compiler_params=_cp("parallel", "arbitrary"))(dx, wd, gu)


def _dw(a, b, b_map, ns, NS, tka, tk, name):
    T, Ka = a.shape
    b_block = (tk, ns) if b.ndim == 2 else (1, tk, ns)

    def body(a_ref, b_ref, o_ref):
        bv = b_ref[...] if b.ndim == 2 else b_ref[0]
        part = lax.dot_general(_mx(a_ref[...]), _mx(bv), TN, preferred_element_type=f32)

        @pl.when(pl.program_id(2) == 0)
        def _():
            o_ref[0] = part

        @pl.when(pl.program_id(2) > 0)
        def _():
            o_ref[0] += part

    return pl.pallas_call(
        body, grid=(NS, Ka // tka, T // tk),
        in_specs=[pl.BlockSpec((tk, tka), lambda s, k, t: (t, k)),
                  pl.BlockSpec(b_block, lambda s, k, t: b_map(t, s))],
        out_specs=pl.BlockSpec((1, tka, ns), lambda s, k, t: (s, k, 0)),
        out_shape=SDS((NS, Ka, ns), f32),
        name=name, compiler_params=_cp("parallel", "parallel", "arbitrary"))(a, b)


def _dx_norm_bwd(dy, dy_map, w, x, gain, dxin, nk, tm, name, comm=None):
    T, D = x.shape
    ns = w.shape[2]
    dy_block = (tm, ns) if dy.ndim == 2 else (1, tm, ns)

    def body(dy_ref, w_ref, x_ref, g_ref, dxin_ref, dx_ref, dg_ref, acc_ref):
        i, s = pl.program_id(0), pl.program_id(1)
        dv = dy_ref[...] if dy.ndim == 2 else dy_ref[0]
        part = lax.dot_general(_mx(dv), w_ref[0], NT, preferred_element_type=f32)

        @pl.when(s == 0)
        def _():
            acc_ref[...] = part

        @pl.when(s > 0)
        def _():
            acc_ref[...] += part

        @pl.when(s == nk - 1)
        def _():
            @pl.when(i == 0)
            def _():
                dg_ref[...] = jnp.zeros_like(dg_ref)
            dh = acc_ref[...]
            xv = x_ref[...]
            r = _rms(xv)
            xh = xv * r
            dg_ref[...] += jnp.sum(dh * xh, axis=0, keepdims=True)
            dxh = dh * g_ref[...]
            dx_ref[...] = dxin_ref[...] + r * (dxh - xh * jnp.mean(dxh * xh, axis=-1, keepdims=True))

    c_args, c_in, c_out, c_shapes, aliases, c_scratch = _host_specs(comm, 5, 2)
    first = lambda: (pl.program_id(0) == 0) & (pl.program_id(1) == 0)
    last = lambda: (pl.program_id(0) == T // tm - 1) & (pl.program_id(1) == nk - 1)
    return pl.pallas_call(
        _host(body, 5, 2, 1, comm, first, last), grid=(T // tm, nk),
        in_specs=[pl.BlockSpec(dy_block, lambda i, s: dy_map(i, s)),
                  pl.BlockSpec((1, D, ns), lambda i, s: (s, 0, 0)),
                  pl.BlockSpec((tm, D), lambda i, s: (i, 0)),
                  pl.BlockSpec((1, D), lambda i, s: (0, 0)),
                  pl.BlockSpec((tm, D), lambda i, s: (i, 0))] + c_in,
        out_specs=[pl.BlockSpec((tm, D), lambda i, s: (i, 0)),
                   pl.BlockSpec((1, D), lambda i, s: (0, 0))] + c_out,
        out_shape=[SDS((T, D), f32), SDS((1, D), f32)] + c_shapes,
        input_output_aliases=aliases,
        scratch_shapes=[pltpu.VMEM((tm, D), f32)] + c_scratch,
        name=name, compiler_params=_cp("arbitrary", "arbitrary"))(dy, w, x, gain, dxin, *c_args)


def _dycat(dx, w, ys, gain, tm):
    T, D = dx.shape

    def body(dx_ref, w_ref, ya, yb, yc, yd, g_ref, da, db, dc, dd, dg_ref):
        @pl.when(pl.program_id(0) == 0)
        def _():
            dg_ref[...] = jnp.zeros_like(dg_ref)
        dyn = lax.dot_general(_mx(dx_ref[...]), w_ref[...], NT, preferred_element_type=f32)
        for m, (r, o) in enumerate(((ya, da), (yb, db), (yc, dc), (yd, dd))):
            cols = slice(m * GW, (m + 1) * GW)
            y = r[...]
            rs = _rms(y)
            yh = y * rs
            d = dyn[:, cols]
            dg_ref[:, cols] += jnp.sum(d * yh, axis=0, keepdims=True)
            dyh = d * g_ref[:, cols]
            o[...] = rs * (dyh - yh * jnp.mean(dyh * yh, axis=-1, keepdims=True))

    yspec = pl.BlockSpec((tm, GW), lambda i: (i, 0))
    return pl.pallas_call(
        body, grid=(T // tm,),
        in_specs=[pl.BlockSpec((tm, D), lambda i: (i, 0)),
                  pl.BlockSpec((D, D), lambda i: (0, 0)),
                  yspec, yspec, yspec, yspec,
                  pl.BlockSpec((1, D), lambda i: (0, 0))],
        out_specs=[yspec, yspec, yspec, yspec, pl.BlockSpec((1, D), lambda i: (0, 0))],
        out_shape=[SDS((T, GW), f32)] * 4 + [SDS((1, D), f32)],
        name="dycat", compiler_params=_cp("arbitrary"))(dx, w, *ys, gain)


def _sgu_consts():
    r, c = _iota((GW, GW), 0), _iota((GW, GW), 1)
    seg = (r // HD == c // HD).astype(f32)
    tr, ts = _iota((BLK, BLK), 0), _iota((BLK, BLK), 1)
    causal = ts <= tr
    lane_head = _iota((BLK, GW), 1) // HD
    return seg, causal, lane_head


def _sgu_chunk(au, av, w, bexp, consts):
    seg, causal, lane_head = consts
    u = jax.nn.gelu(au)
    v = jax.nn.gelu(av)
    mu = jnp.dot(v, seg, precision=HIGHEST, preferred_element_type=f32) * (1.0 / HD)
    vc = v - mu
    var = jnp.dot(vc * vc, seg, precision=HIGHEST, preferred_element_type=f32) * (1.0 / HD)
    vn = _mx(vc * lax.rsqrt(var + EPS))
    mix = bexp
    for h in range(GW // HD):
        wh = _mx(jnp.where(causal, w[h], 0.0))
        mix = mix + jnp.where(lane_head == h, jnp.dot(wh, vn, preferred_element_type=f32), 0.0)
    return u * mix


def _sgu_fwd(proj, w, bexp, B, S):
    def body(au_ref, av_ref, w_ref, b_ref, y_ref):
        consts = _sgu_consts()
        wv, bv = w_ref[...], b_ref[...]

        def chunk(n, c):
            rows = pl.ds(pl.multiple_of(n * BLK, BLK), BLK)
            y_ref[rows, :] = _sgu_chunk(au_ref[rows, :], av_ref[rows, :], wv, bv, consts)
            return c
        lax.fori_loop(0, S // BLK, chunk, 0)

    return pl.pallas_call(
        body, grid=(B,),
        in_specs=[pl.BlockSpec((S, GW), lambda b: (b, 0)),
                  pl.BlockSpec((S, GW), lambda b: (b, 1)),
                  pl.BlockSpec((GW // HD, BLK, BLK), lambda b: (0, 0, 0)),
                  pl.BlockSpec((BLK, GW), lambda b: (0, 0))],
        out_specs=pl.BlockSpec((S, GW), lambda b: (b, 0)),
        out_shape=SDS((B * S, GW), f32),
        name="sgu_fwd", compiler_params=_cp("parallel"))(proj, proj, w, bexp)


def _sgu_bwd(proj, w, bexp, dy, B, S):
    def body(au_ref, av_ref, w_ref, b_ref, dy_ref, dp_ref, dw_ref, db_ref):
        @pl.when(pl.program_id(0) == 0)
        def _():
            dw_ref[...] = jnp.zeros_like(dw_ref)
            db_ref[...] = jnp.zeros_like(db_ref)
        consts = _sgu_consts()
        wv, bv = w_ref[...], b_ref[...]
        fn = lambda au, av, ww, bb: _sgu_chunk(au, av, ww, bb, consts)

        def chunk(n, carry):
            dw_acc, db_acc = carry
            rows = pl.ds(pl.multiple_of(n * BLK, BLK), BLK)
            _, vjp = jax.vjp(fn, au_ref[rows, :], av_ref[rows, :], wv, bv)
            dau, dav, dwc, dbc = vjp(dy_ref[rows, :])
            dp_ref[rows, 0:GW] = dau.astype(bf16)
            dp_ref[rows, GW:2 * GW] = dav.astype(bf16)
            return dw_acc + dwc, db_acc + dbc
        dw_acc, db_acc = lax.fori_loop(0, S // BLK, chunk, (jnp.zeros(wv.shape, f32), jnp.zeros(bv.shape, f32)))
        dw_ref[...] += dw_acc
        db_ref[...] += jnp.dot(db_acc, consts[0], precision=HIGHEST, preferred_element_type=f32)

    return pl.pallas_call(
        body, grid=(B,),
        in_specs=[pl.BlockSpec((S, GW), lambda b: (b, 0)),
                  pl.BlockSpec((S, GW), lambda b: (b, 1)),
                  pl.BlockSpec((GW // HD, BLK, BLK), lambda b: (0, 0, 0)),
                  pl.BlockSpec((BLK, GW), lambda b: (0, 0)),
                  pl.BlockSpec((S, GW), lambda b: (b, 0))],
        out_specs=[pl.BlockSpec((S, 2 * GW), lambda b: (b, 0)),
                   pl.BlockSpec((GW // HD, BLK, BLK), lambda b: (0, 0, 0)),
                   pl.BlockSpec((BLK, GW), lambda b: (0, 0))],
        out_shape=[SDS((B * S, 2 * GW), bf16), SDS((GW // HD, BLK, BLK), f32), SDS((BLK, GW), f32)],
        name="sgu_bwd", compiler_params=_cp("arbitrary"))(proj, proj, w, bexp, dy)


def _pool_parts(p):
    n = p.shape[0]
    r = _iota(p.shape, 0)
    lg = _iota(p.shape, 1) // HD

    def sh(v, k):
        return jnp.where(r >= k, pltpu.roll(v, k, 0), 0.0)
    s2 = p + sh(p, 1)
    s4 = s2 + sh(s2, 2)
    s8 = s4 + sh(s4, 4)
    s16 = s8 + sh(s8, 8)
    ws = jnp.where(lg == 0, s2, jnp.where(lg == 1, s4, jnp.where(lg == 2, s8, s16)))
    wlen = jnp.where(lg == 0, 2, jnp.where(lg == 1, 4, jnp.where(lg == 2, 8, 16)))
    cnt = jnp.minimum(r + 1, wlen).astype(f32)
    del n
    return ws / cnt - p, cnt, lg


def _pool_fwd(proj, wbd, scale, B, S):
    def body(p_ref, w_ref, s_ref, y_ref):
        y, _, _ = _pool_parts(p_ref[...])
        y_ref[...] = jnp.dot(_mx(y), _mx(w_ref[...]), preferred_element_type=f32) * s_ref[...]

    return pl.pallas_call(
        body, grid=(B,),
        in_specs=[pl.BlockSpec((S, GW), lambda b: (b, 2)),
                  pl.BlockSpec((GW, GW), lambda b: (0, 0)),
                  pl.BlockSpec((1, GW), lambda b: (0, 0))],
        out_specs=pl.BlockSpec((S, GW), lambda b: (b, 0)),
        out_shape=SDS((B * S, GW), f32),
        name="pool_fwd", compiler_params=_cp("parallel"))(proj, wbd, scale)


def _pool_bwd(proj, wbd, scale, dy, B, S):
    def body(p_ref, w_ref, s_ref, dy_ref, dp_ref, dw_ref, ds_ref):
        @pl.when(pl.program_id(0) == 0)
        def _():
            dw_ref[...] = jnp.zeros_like(dw_ref)
            ds_ref[...] = jnp.zeros_like(ds_ref)
        y, cnt, lg = _pool_parts(p_ref[...])
        wv = _mx(w_ref[...])
        z = jnp.dot(_mx(y), wv, preferred_element_type=f32)
        dout = dy_ref[...]
        ds_ref[...] += jnp.sum(dout * z, axis=0, keepdims=True)
        dz = _mx(dout * s_ref[...])
        dw_ref[...] += lax.dot_general(_mx(y), dz, TN, preferred_element_type=f32)
        dyv = lax.dot_general(dz, wv, NT, preferred_element_type=f32)
        n = dyv.shape[0]
        r = _iota(dyv.shape, 0)

        def ush(v, k):
            return jnp.where(r < n - k, pltpu.roll(v, n - k, 0), 0.0)
        gq = dyv / cnt
        a2 = gq + ush(gq, 1)
        a4 = a2 + ush(a2, 2)
        a8 = a4 + ush(a4, 4)
        a16 = a8 + ush(a8, 8)
        adj = jnp.where(lg == 0, a2, jnp.where(lg == 1, a4, jnp.where(lg == 2, a8, a16)))
        dp_ref[...] = (adj - dyv).astype(bf16)

    return pl.pallas_call(
        body, grid=(B,),
        in_specs=[pl.BlockSpec((S, GW), lambda b: (b, 2)),
                  pl.BlockSpec((GW, GW), lambda b: (0, 0)),
                  pl.BlockSpec((1, GW), lambda b: (0, 0)),
                  pl.BlockSpec((S, GW), lambda b: (b, 0))],
        out_specs=[pl.BlockSpec((S, GW), lambda b: (b, 0)),
                   pl.BlockSpec((GW, GW), lambda b: (0, 0)),
                   pl.BlockSpec((1, GW), lambda b: (0, 0))],
        out_shape=[SDS((B * S, GW), bf16), SDS((GW, GW), f32), SDS((1, GW), f32)],
        name="pool_bwd", compiler_params=_cp("arbitrary"))(proj, wbd, scale, dy)


def _t5_bucket_table():
    dist = (np.arange(BLK)[:, None] + BLK) - np.arange(2 * BLK)[None, :]
    d = np.clip(dist, 0, BLK - 1)
    max_exact = N_BUCKETS // 2
    df = np.maximum(d, 1).astype(np.float32)
    large = max_exact + (np.log(df / max_exact) / np.float32(np.log(MAX_DISTANCE / max_exact))
                         * (N_BUCKETS - max_exact)).astype(np.int32)
    large = np.minimum(large, N_BUCKETS - 1)
    return np.where(d < max_exact, d, large).astype(np.int32)


def _swa_math(qb, k2, v2, sink, bias, n, p, g):
    ri, ci = _iota((BLK, BLK), 0), _iota((BLK, BLK), 1)
    selq = ((ri - g * HD == ci - p * HD) & (ri >= g * HD) & (ri < (g + 1) * HD)).astype(_MXU)
    selv = ((ci - g * HD == ri - p * HD) & (ci >= g * HD) & (ci < (g + 1) * HD)).astype(_MXU)
    qi, ki = _iota((BLK, 2 * BLK), 0), _iota((BLK, 2 * BLK), 1)
    dist = qi + BLK - ki
    mask = (dist >= 0) & (dist < BLK) & ((ki >= BLK) | (n > 0))
    qs = _mx(jnp.dot(_mx(qb), selq, preferred_element_type=f32))
    z = lax.dot_general(qs, _mx(k2), NT, preferred_element_type=f32) * (HD ** -0.5)
    z = jnp.where(mask, z + bias, -1e30)
    s = jnp.mean(sink, axis=-1, keepdims=True)
    m = jnp.maximum(jnp.max(z, axis=-1, keepdims=True), s)
    e = jnp.exp(z - m)
    pr = e / (jnp.sum(e, axis=-1, keepdims=True) + jnp.exp(s - m))
    vs = _mx(jnp.dot(_mx(v2), selv, preferred_element_type=f32))
    return jnp.dot(_mx(pr), vs, preferred_element_type=f32)


def _swa_fwd(proj, sinks, bias, B, S):
    def body(q_ref, kv_ref, s_ref, b_ref, y_ref):
        def block(n, c):
            rows = pl.ds(pl.multiple_of(n * BLK, BLK), BLK)
            prev = pl.ds(pl.multiple_of(jnp.maximum(n - 1, 0) * BLK, BLK), BLK)
            k2 = jnp.concatenate([kv_ref[prev, 0:BLK], kv_ref[rows, 0:BLK]], axis=0)
            v2 = jnp.concatenate([kv_ref[prev, BLK:2 * BLK], kv_ref[rows, BLK:2 * BLK]], axis=0)
            for p in range(2):
                qb = q_ref[rows, p * BLK:(p + 1) * BLK]
                o = jnp.zeros((BLK, BLK), f32)
                for g in range(2):
                    o = o + _swa_math(qb, k2, v2, s_ref[2 * p + g], b_ref[2 * p + g], n, p, g)
                y_ref[rows, p * BLK:(p + 1) * BLK] = o
            return c
        lax.fori_loop(0, S // BLK, block, 0)

    return pl.pallas_call(
        body, grid=(B,),
        in_specs=[pl.BlockSpec((S, GW), lambda b: (b, 3)),
                  pl.BlockSpec((S, GW), lambda b: (b, 4)),
                  pl.BlockSpec((4, 1, BLK), lambda b: (0, 0, 0)),
                  pl.BlockSpec((4, BLK, 2 * BLK), lambda b: (0, 0, 0))],
        out_specs=pl.BlockSpec((S, GW), lambda b: (b, 0)),
        out_shape=SDS((B * S, GW), f32),
        name="swa_fwd", compiler_params=_cp("parallel"))(proj, proj, sinks, bias)


def _swa_bwd(proj, sinks, bias, dy, B, S):
    def body(q_ref, kv_ref, s_ref, b_ref, dy_ref, dq_ref, dkv_ref, ds_ref, db_ref, acc_ref):
        @pl.when(pl.program_id(0) == 0)
        def _():
            ds_ref[...] = jnp.zeros_like(ds_ref)
            db_ref[...] = jnp.zeros_like(db_ref)
        acc_ref[...] = jnp.zeros_like(acc_ref)

        def block(n, c):
            rows = pl.ds(pl.multiple_of(n * BLK, BLK), BLK)
            prev = pl.ds(pl.multiple_of(jnp.maximum(n - 1, 0) * BLK, BLK), BLK)
            k2 = jnp.concatenate([kv_ref[prev, 0:BLK], kv_ref[rows, 0:BLK]], axis=0)
            v2 = jnp.concatenate([kv_ref[prev, BLK:2 * BLK], kv_ref[rows, BLK:2 * BLK]], axis=0)
            dk2 = jnp.zeros((2 * BLK, BLK), f32)
            dv2 = jnp.zeros((2 * BLK, BLK), f32)
            for p in range(2):
                qb = q_ref[rows, p * BLK:(p + 1) * BLK]
                do = dy_ref[rows, p * BLK:(p + 1) * BLK]
                dq = jnp.zeros((BLK, BLK), f32)
                for g in range(2):
                    h = 2 * p + g
                    fn = functools.partial(_swa_math, n=n, p=p, g=g)
                    _, vjp = jax.vjp(fn, qb, k2, v2, s_ref[h], b_ref[h])
                    dqg, dkg, dvg, dsg, dbg = vjp(do)
                    dq = dq + dqg
                    dk2 = dk2 + dkg
                    dv2 = dv2 + dvg
                    ds_ref[h] += dsg
                    db_ref[h] += dbg
                dq_ref[rows, p * BLK:(p + 1) * BLK] = dq.astype(bf16)
            acc_ref[prev, 0:BLK] += dk2[0:BLK]
            acc_ref[rows, 0:BLK] += dk2[BLK:2 * BLK]
            acc_ref[prev, BLK:2 * BLK] += dv2[0:BLK]
            acc_ref[rows, BLK:2 * BLK] += dv2[BLK:2 * BLK]
            return c
        lax.fori_loop(0, S // BLK, block, 0)
        dkv_ref[...] = acc_ref[...].astype(bf16)

    return pl.pallas_call(
        body, grid=(B,),
        in_specs=[pl.BlockSpec((S, GW), lambda b: (b, 3)),
                  pl.BlockSpec((S, GW), lambda b: (b, 4)),
                  pl.BlockSpec((4, 1, BLK), lambda b: (0, 0, 0)),
                  pl.BlockSpec((4, BLK, 2 * BLK), lambda b: (0, 0, 0)),
                  pl.BlockSpec((S, GW), lambda b: (b, 0))],
        out_specs=[pl.BlockSpec((S, GW), lambda b: (b, 0)),
                   pl.BlockSpec((S, GW), lambda b: (b, 0)),
                   pl.BlockSpec((4, 1, BLK), lambda b: (0, 0, 0)),
                   pl.BlockSpec((4, BLK, 2 * BLK), lambda b: (0, 0, 0))],
        out_shape=[SDS((B * S, GW), bf16), SDS((B * S, GW), bf16), SDS((4, 1, BLK), f32), SDS((4, BLK, 2 * BLK), f32)],
        scratch_shapes=[pltpu.VMEM((S, GW), f32)],
        name="swa_bwd", compiler_params=_cp("arbitrary"))(proj, proj, sinks, bias, dy)


def _log1m(z):
    return jnp.minimum(-z, 0.0) - jnp.log1p(jnp.exp(-jnp.abs(z)))


def _sb_consts(tri):
    r2, c2 = _iota((2 * BLK, 2 * BLK), 0), _iota((2 * BLK, 2 * BLK), 1)
    tri2 = (tri(r2, c2) & (r2 // BLK == c2 // BLK)).astype(bf16)
    ri, ci = _iota((BLK, 2 * BLK), 0), _iota((BLK, 2 * BLK), 1)
    strict2 = (ci % BLK) < ri
    head0 = _iota((BLK, BLK), 1) < HD
    return tri2, strict2, head0


def _sb_load_kv(k_ref, v_ref, kb, head0):
    krows = pl.ds(pl.multiple_of(kb * BLK, BLK), BLK)
    kks, vvs = [], []
    for p in range(2):
        k = k_ref[krows, p * BLK:(p + 1) * BLK]
        v = v_ref[krows, p * BLK:(p + 1) * BLK]
        kks.append(_mx(jnp.concatenate([jnp.where(head0, k, 0.0), jnp.where(head0, 0.0, k)], axis=0)))
        vvs.append(_mx(jnp.concatenate([jnp.where(head0, v, 0.0), jnp.where(head0, 0.0, v)], axis=0)))
    return kks, vvs


def _two_halves(a, b):
    return jnp.concatenate([jnp.broadcast_to(a, (BLK, BLK)), jnp.broadcast_to(b, (BLK, BLK))], axis=1)


def _half_sums(t):
    return jnp.sum(t[:, :BLK], axis=-1, keepdims=True), jnp.sum(t[:, BLK:], axis=-1, keepdims=True)


def _sb_fwd(proj, B, S, comm=None):
    def body(q_ref, k_ref, v_ref, y_ref, lt_ref):
        ci = _iota((BLK, BLK), 1)
        above2, strict2, head0 = _sb_consts(lambda r, c: r > c)

        def step(qs, kb, carry, diag):
            kks, vvs = _sb_load_kv(k_ref, v_ref, kb, head0)
            zs = [lax.dot_general(qs[p], kks[p], NT, preferred_element_type=f32) for p in range(2)]
            Ls = [_log1m(z) for z in zs]
            if diag:
                Ls = [jnp.where(strict2, L, 0.0) for L in Ls]
            tails = [_split_dot(L, above2) for L in Ls]
            out = []
            for p in range(2):
                R0, R1, acc = carry[3 * p:3 * p + 3]
                w = jnp.exp(zs[p] + Ls[p] + tails[p] + _two_halves(R0, R1))
                if diag:
                    w = jnp.where(strict2, w, 0.0)
                acc = acc + jnp.dot(_mx(w), vvs[p], preferred_element_type=f32)
                s0, s1 = _half_sums(Ls[p])
                out += [R0 + s0, R1 + s1, acc]
            return tuple(out)

        def qblock(n, c):
            qrows = pl.ds(pl.multiple_of(n * BLK, BLK), BLK)
            qs = [_mx(q_ref[qrows, p * BLK:(p + 1) * BLK] * (HD ** -0.5)) for p in range(2)]
            z1, z2 = jnp.zeros((BLK, 1), f32), jnp.zeros((BLK, BLK), f32)
            carry = step(qs, n, (z1, z1, z2, z1, z1, z2), True)
            res = lax.fori_loop(0, n, lambda i, cr: step(qs, n - 1 - i, cr, False), carry)
            lt = jnp.zeros((BLK, BLK), f32)
            for p in range(2):
                y_ref[qrows, p * BLK:(p + 1) * BLK] = res[3 * p + 2]
                lt = lt + jnp.where(ci == 2 * p, res[3 * p], 0.0) + jnp.where(ci == 2 * p + 1, res[3 * p + 1], 0.0)
            lt_ref[qrows, :] = lt
            return c
        lax.fori_loop(0, S // BLK, qblock, 0)

    spec = lambda j: pl.BlockSpec((S, GW), lambda b: (b, j))
    c_args, c_in, c_out, c_shapes, aliases, c_scratch = _host_specs(comm, 3, 2)
    step = lambda v: (lambda: pl.program_id(0) == v)
    return pl.pallas_call(
        _host(body, 3, 2, 0, comm, step(0), step(B - 1)), grid=(B,),
        in_specs=[spec(5), spec(6), spec(7)] + c_in,
        out_specs=[pl.BlockSpec((S, GW), lambda b: (b, 0)), pl.BlockSpec((S, BLK), lambda b: (b, 0))] + c_out,
        out_shape=[SDS((B * S, GW), f32), SDS((B * S, BLK), f32)] + c_shapes,
        input_output_aliases=aliases, scratch_shapes=c_scratch,
        name="sb_fwd" if comm is None else "sb_fwd_gather",
        compiler_params=_cp("arbitrary"))(proj, proj, proj, *c_args)


def _sb_bwd(proj, ltot, dy, B, S, comm=None):
    def body(q_ref, k_ref, v_ref, lt_ref, dy_ref, dq_ref, dk_ref, dv_ref, dka_ref, dva_ref):
        ci = _iota((BLK, BLK), 1)
        upto2, strict2, head0 = _sb_consts(lambda r, c: r <= c)
        below2, _, _ = _sb_consts(lambda r, c: r < c)
        dka_ref[...] = jnp.zeros_like(dka_ref)
        dva_ref[...] = jnp.zeros_like(dva_ref)

        def step(qs, dos, lts, kb, carry, diag):
            krows = pl.ds(pl.multiple_of(kb * BLK, BLK), BLK)
            kks, vvs = _sb_load_kv(k_ref, v_ref, kb, head0)
            zs = [lax.dot_general(qs[p], kks[p], NT, preferred_element_type=f32) for p in range(2)]
            dws = [lax.dot_general(dos[p], vvs[p], NT, preferred_element_type=f32) for p in range(2)]
            Ls = [_log1m(z) for z in zs]
            if diag:
                Ls = [jnp.where(strict2, L, 0.0) for L in Ls]
            pins = [_split_dot(L, upto2) for L in Ls]
            ws, das = [], []
            for p in range(2):
                PL0, PL1 = carry[5 * p], carry[5 * p + 1]
                tail = _two_halves(lts[2 * p] - PL0, lts[2 * p + 1] - PL1) - pins[p]
                w = jnp.exp(zs[p] + Ls[p] + tail)
                if diag:
                    w = jnp.where(strict2, w, 0.0)
                ws.append(w)
                das.append(w * dws[p])
            pexs = [_split_dot(da, below2) for da in das]
            dzs = []
            for p in range(2):
                dL = _two_halves(carry[5 * p + 2], carry[5 * p + 3]) + pexs[p]
                sg = jax.nn.sigmoid(zs[p])
                dz = das[p] * (1.0 - sg) - dL * sg
                if diag:
                    dz = jnp.where(strict2, dz, 0.0)
                dzs.append(_mx(dz))
            dqs = [jnp.dot(dzs[p], kks[p], preferred_element_type=f32) for p in range(2)]
            dks = [lax.dot_general(dzs[p], qs[p], TN, preferred_element_type=f32) for p in range(2)]
            dvs = [lax.dot_general(_mx(ws[p]), dos[p], TN, preferred_element_type=f32) for p in range(2)]
            out = []
            for p in range(2):
                lanes = slice(p * BLK, (p + 1) * BLK)
                dka_ref[krows, lanes] += jnp.where(head0, dks[p][:BLK], dks[p][BLK:])
                dva_ref[krows, lanes] += jnp.where(head0, dvs[p][:BLK], dvs[p][BLK:])
                l0, l1 = _half_sums(Ls[p])
                a0, a1 = _half_sums(das[p])
                out += [carry[5 * p] + l0, carry[5 * p + 1] + l1, carry[5 * p + 2] + a0, carry[5 * p + 3] + a1,
                        carry[5 * p + 4] + dqs[p]]
            return tuple(out)

        def qblock(n, c):
            qrows = pl.ds(pl.multiple_of(n * BLK, BLK), BLK)
            ltb = lt_ref[qrows, :]
            lts = [jnp.sum(jnp.where(ci == h, ltb, 0.0), axis=-1, keepdims=True) for h in range(4)]
            qs = [_mx(q_ref[qrows, p * BLK:(p + 1) * BLK] * (HD ** -0.5)) for p in range(2)]
            dos = [_mx(dy_ref[qrows, p * BLK:(p + 1) * BLK]) for p in range(2)]
            z1, z2 = jnp.zeros((BLK, 1), f32), jnp.zeros((BLK, BLK), f32)
            carry = lax.fori_loop(0, n, lambda kb, cr: step(qs, dos, lts, kb, cr, False), (z1, z1, z1, z1, z2) * 2)
            res = step(qs, dos, lts, n, carry, True)
            for p in range(2):
                dq_ref[qrows, p * BLK:(p + 1) * BLK] = (res[5 * p + 4] * (HD ** -0.5)).astype(bf16)
            return c
        lax.fori_loop(0, S // BLK, qblock, 0)
        dk_ref[...] = dka_ref[...].astype(bf16)
        dv_ref[...] = dva_ref[...].astype(bf16)

    spec = lambda j: pl.BlockSpec((S, GW), lambda b: (b, j))
    o = pl.BlockSpec((S, GW), lambda b: (b, 0))
    c_args, c_in, c_out, c_shapes, aliases, c_scratch = _host_specs(comm, 5, 3)
    step = lambda v: (lambda: pl.program_id(0) == v)
    return pl.pallas_call(
        _host(body, 5, 3, 2, comm, step(0), step(B - 1)), grid=(B,),
        in_specs=[spec(5), spec(6), spec(7), pl.BlockSpec((S, BLK), lambda b: (b, 0)), o] + c_in,
        out_specs=[o, o, o] + c_out,
        out_shape=[SDS((B * S, GW), bf16)] * 3 + c_shapes,
        input_output_aliases=aliases,
        scratch_shapes=[pltpu.VMEM((S, GW), f32), pltpu.VMEM((S, GW), f32)] + c_scratch,
        name="sb_bwd" if comm is None else "sb_bwd_exchange",
        compiler_params=_cp("arbitrary"))(proj, proj, proj, ltot, dy, *c_args)


def _bias_expand(rel_bias_t, bucket):
    n = bucket.shape[1]

    def body(r_ref, b_ref, o_ref):
        onehot = (_iota((N_BUCKETS, n), 0) == b_ref[...]).astype(f32)
        o_ref[...] = jnp.dot(r_ref[...], onehot, precision=HIGHEST, preferred_element_type=f32)
    return pl.pallas_call(body, out_shape=SDS((rel_bias_t.shape[0], n), f32), name="bias_expand",
                          compiler_params=_cp())(rel_bias_t, bucket)


def _bias_reduce(dbias, bucket):
    n = bucket.shape[1]

    def body(*refs):
        b_ref, g_ref = refs[-2], refs[-1]
        d = refs[0][...]
        for r in refs[1:-2]:
            d = d + r[...]
        onehot = (_iota((N_BUCKETS, n), 0) == b_ref[...]).astype(f32)
        g_ref[...] = lax.dot_general(d, onehot, NT, precision=HIGHEST, preferred_element_type=f32)
    return pl.pallas_call(body, out_shape=SDS((dbias[0].shape[0], N_BUCKETS), f32), name="bias_reduce",
                          compiler_params=_cp())(*dbias, bucket)


def _adamw(w, g, m, v, tr, name):
    R, C = w.shape

    def body(w_ref, g_ref, m_ref, v_ref, d_ref, m2_ref, v2_ref):
        gv = g_ref[...]
        m2 = ADAM_B1 * m_ref[...] + (1.0 - ADAM_B1) * gv
        v2 = ADAM_B2 * v_ref[...] + (1.0 - ADAM_B2) * (gv * gv)
        m_hat = m2 / (1.0 - ADAM_B1 ** ADAM_STEP)
        v_hat = v2 / (1.0 - ADAM_B2 ** ADAM_STEP)
        d_ref[...] = -ADAM_LR * (m_hat / (jnp.sqrt(v_hat) + ADAM_EPS) + ADAM_WD * w_ref[...])
        m2_ref[...] = m2
        v2_ref[...] = v2

    spec = pl.BlockSpec((tr, C), lambda i: (i, 0))
    return pl.pallas_call(
        body, grid=(R // tr,), in_specs=[spec] * 4, out_specs=[spec] * 3,
        out_shape=[SDS((R, C), f32)] * 3, name=name, compiler_params=_cp("parallel"))(w, g, m, v)


ANY = pl.BlockSpec(memory_space=pl.ANY)


def _place():
    x, y, c = lax.axis_index("x"), lax.axis_index("y"), lax.axis_index("c")
    chips = [(1 - x, y), (x, 1 - y), (1 - x, 1 - y)]
    return x, y, c, chips


def _cast_slots(w, kidx):
    L, a, b = w.shape
    ta = a // 2

    def body(k_ref, *refs):
        for l in range(L):
            refs[L + l][0] = refs[l][0].astype(bf16)

    return pl.pallas_call(
        body,
        grid_spec=pltpu.PrefetchScalarGridSpec(
            num_scalar_prefetch=1, grid=(a // ta,),
            in_specs=[pl.BlockSpec((1, ta, b), functools.partial(lambda i, k_ref, l: (l, i, 0), l=l)) for l in range(L)],
            out_specs=[pl.BlockSpec((1, ta, b), lambda i, k_ref: (k_ref[0], i, 0)) for _ in range(L)]),
        out_shape=[SDS((N_CHIPS, a, b), bf16)] * L,
        name="cast_slots", compiler_params=_cp("parallel"))(kidx, *([w] * L))


class _GatherComm:
    def __init__(self, bufs):
        self.inputs = list(bufs)
        self.out_shape = [SDS(b.shape, b.dtype) for b in bufs]
        self.aliased = True
        self.scratch = [pltpu.SemaphoreType.DMA((3 * len(bufs),))] * 4

    def _copies(self, i_refs, o_refs, sems):
        send1, recv1, send2, recv2 = sems
        x, y, c, chips = _place()
        k = 2 * x + y
        first, got1, second, got2 = [], [], [], []
        for i, buf in enumerate(self.inputs):
            h = buf.shape[1] // 2
            mine, theirs = pl.ds(c * h, h), pl.ds((1 - c) * h, h)
            for j, (cx, cy) in enumerate(chips):
                s = 3 * i + j
                first.append(pltpu.make_async_remote_copy(
                    src_ref=i_refs[i].at[k, mine], dst_ref=o_refs[i].at[k, mine], send_sem=send1.at[s],
                    recv_sem=recv1.at[s], device_id=(cx, cy, c), device_id_type=MESH))
                a = o_refs[i].at[2 * cx + cy, mine]
                got1.append(pltpu.make_async_remote_copy(
                    src_ref=a, dst_ref=a, send_sem=send1.at[s], recv_sem=recv1.at[s],
                    device_id=(cx, cy, c), device_id_type=MESH))
                second.append(pltpu.make_async_remote_copy(
                    src_ref=a, dst_ref=a, send_sem=send2.at[s], recv_sem=recv2.at[s],
                    device_id=(x, y, 1 - c), device_id_type=MESH))
                b = o_refs[i].at[2 * cx + cy, theirs]
                got2.append(pltpu.make_async_remote_copy(
                    src_ref=b, dst_ref=b, send_sem=send2.at[s], recv_sem=recv2.at[s],
                    device_id=(x, y, 1 - c), device_id_type=MESH))
        return first, got1, second, got2

    def start(self, i_refs, o_refs, sems):
        for cp in self._copies(i_refs, o_refs, sems)[0]:
            cp.start()

    def finish(self, i_refs, o_refs, sems):
        first, got1, second, got2 = self._copies(i_refs, o_refs, sems)
        for g, cp in zip(got1, second):
            g.wait_recv()
            cp.start()
        for g in got2:
            g.wait_recv()
        for cp in first + second:
            cp.wait_send()


class _PairExchangeComm:
    def __init__(self, gs):
        self.inputs = list(gs)
        self.out_shape = [SDS((g.shape[0], g.shape[1] // 2, g.shape[2]), g.dtype) for g in gs]
        self.aliased = False
        self.scratch = [pltpu.SemaphoreType.DMA((len(gs),))] * 2

    def _copies(self, i_refs, o_refs, sems):
        send, recv = sems
        x, y, c, _ = _place()
        cps = []
        for i, g in enumerate(self.inputs):
            h = g.shape[1] // 2
            cps.append(pltpu.make_async_remote_copy(
                src_ref=i_refs[i].at[:, pl.ds((1 - c) * h, h)], dst_ref=o_refs[i], send_sem=send.at[i], recv_sem=recv.at[i],
                device_id=(x, y, 1 - c), device_id_type=MESH))
        return cps

    def start(self, i_refs, o_refs, sems):
        for cp in self._copies(i_refs, o_refs, sems):
            cp.start()

    def finish(self, i_refs, o_refs, sems):
        for cp in self._copies(i_refs, o_refs, sems):
            cp.wait()


class _ChipExchangeComm:
    def __init__(self, qs):
        self.inputs = list(qs)
        self.out_shape = [SDS(q.shape, q.dtype) for q in qs]
        self.aliased = False
        self.scratch = [pltpu.SemaphoreType.DMA((3 * len(qs),))] * 2

    def _copies(self, i_refs, o_refs, sems):
        send, recv = sems
        x, y, c, chips = _place()
        k = 2 * x + y
        cps, got = [], []
        for i in range(len(self.inputs)):
            for j, (cx, cy) in enumerate(chips):
                s = 3 * i + j
                cps.append(pltpu.make_async_remote_copy(
                    src_ref=i_refs[i].at[2 * cx + cy], dst_ref=o_refs[i].at[k], send_sem=send.at[s],
                    recv_sem=recv.at[s], device_id=(cx, cy, c), device_id_type=MESH))
                a = o_refs[i].at[2 * cx + cy]
                got.append(pltpu.make_async_remote_copy(
                    src_ref=a, dst_ref=a, send_sem=send.at[s], recv_sem=recv.at[s],
                    device_id=(cx, cy, c), device_id_type=MESH))
        return cps, got

    def start(self, i_refs, o_refs, sems):
        for cp in self._copies(i_refs, o_refs, sems)[0]:
            cp.start()

    def finish(self, i_refs, o_refs, sems):
        cps, got = self._copies(i_refs, o_refs, sems)
        for g in got:
            g.wait_recv()
        for cp in cps:
            cp.wait_send()


def _comm_only(comm, name):
    n = len(comm.inputs)

    def body(*refs):
        i_refs, o_refs, sems = refs[:n], refs[n:n + len(comm.out_shape)], refs[n + len(comm.out_shape):]
        comm.start(i_refs, o_refs, sems)
        comm.finish(i_refs, o_refs, sems)

    return pl.pallas_call(
        body, out_shape=comm.out_shape, in_specs=[ANY] * n, out_specs=[ANY] * len(comm.out_shape),
        input_output_aliases={i: i for i in range(n)} if comm.aliased else {},
        scratch_shapes=comm.scratch, name=name,
        compiler_params=pltpu.CompilerParams(has_side_effects=True))(*comm.inputs)


def _host(body, n_in, n_out, n_scratch, comm, first, last):
    if comm is None:
        return body
    ci, co = len(comm.inputs), len(comm.out_shape)

    def wrapped(*refs):
        o = 0
        parts = []
        for n in (n_in, ci, n_out, co, n_scratch):
            parts.append(refs[o:o + n])
            o += n
        hin, cin, hout, cout, hs = parts
        sems = refs[o:]

        @pl.when(first())
        def _():
            comm.start(cin, cout, sems)
        body(*hin, *hout, *hs)

        @pl.when(last())
        def _():
            comm.finish(cin, cout, sems)
    return wrapped


def _host_specs(comm, n_in, n_out):
    if comm is None:
        return [], [], [], [], {}, []
    ci, co = len(comm.inputs), len(comm.out_shape)
    aliases = {n_in + i: n_out + i for i in range(ci)} if comm.aliased else {}
    return comm.inputs, [ANY] * ci, [ANY] * co, comm.out_shape, aliases, comm.scratch


def _pair_add(g, r, cidx, name):
    ns, a, b = g.shape
    h = a // 2
    th = h if h * b * 4 <= 4 * 1024 * 1024 else h // 2

    def body(c_ref, g_ref, r_ref, qf_ref, qb_ref):
        q = g_ref[...] + r_ref[...]
        qf_ref[...] = q
        qb_ref[...] = q.astype(bf16)

    nb = h // th
    spec = pl.BlockSpec((1, th, b), lambda s, i, c_ref: (s, i, 0))
    return pl.pallas_call(
        body,
        grid_spec=pltpu.PrefetchScalarGridSpec(
            num_scalar_prefetch=1, grid=(ns, nb),
            in_specs=[pl.BlockSpec((1, th, b), lambda s, i, c_ref: (s, c_ref[0] * nb + i, 0)), spec],
            out_specs=[spec, spec]),
        out_shape=[SDS((ns, h, b), f32), SDS((ns, h, b), bf16)],
        name=name, compiler_params=_cp("parallel", "parallel"))(cidx, g, r)


def _chip_add(qf, r2, idx, prev, L, name):
    ns, h, b = r2.shape
    th = h if h * b * 4 <= 4 * 1024 * 1024 else h // 2
    nb = h // th

    def body(s_ref, qf_ref, r1_ref, r2_ref, r3_ref, *rest):
        o_ref = rest[-1]
        o_ref[0] = qf_ref[0] + r1_ref[0].astype(f32) + r2_ref[0].astype(f32) + r3_ref[0].astype(f32)

    other = lambda d: pl.BlockSpec((1, th, b), lambda i, s_ref: ((s_ref[0] + d) % ns, i, 0))
    in_specs = [pl.BlockSpec((1, th, b), lambda i, s_ref: (s_ref[0], i, 0)), other(1), other(2), other(3)]
    args = [idx, qf, r2, r2, r2]
    aliases = {}
    if prev is not None:
        in_specs.append(ANY)
        args.append(prev)
        aliases = {5: 0}
    return pl.pallas_call(
        body,
        grid_spec=pltpu.PrefetchScalarGridSpec(
            num_scalar_prefetch=1, grid=(nb,), in_specs=in_specs,
            out_specs=pl.BlockSpec((1, th, b), lambda i, s_ref: (s_ref[2], s_ref[1] * nb + i, 0))),
        out_shape=SDS((L, 2 * h, b), f32), input_output_aliases=aliases,
        name=name, compiler_params=_cp("arbitrary"))(*args)


def _pair_share(gs, hs):
    n = len(gs)
    L = gs[0].shape[0]

    def body(*refs):
        i_refs, o_refs = refs[:n], refs[n:2 * n]
        send, recv = refs[2 * n:]
        x, y, c, _ = _place()
        cps = []
        for i in range(n):
            for l in range(L):
                mine = pl.ds(c * hs[i], hs[i])
                cp = pltpu.make_async_remote_copy(
                    src_ref=i_refs[i].at[l, mine], dst_ref=o_refs[i].at[l, mine], send_sem=send.at[i * L + l],
                    recv_sem=recv.at[i * L + l], device_id=(x, y, 1 - c), device_id_type=MESH)
                cp.start()
                cps.append(cp)
        for i in range(n):
            for l in range(L):
                got = o_refs[i].at[l, pl.ds((1 - c) * hs[i], hs[i])]
                pltpu.make_async_remote_copy(
                    src_ref=got, dst_ref=got, send_sem=send.at[i * L + l], recv_sem=recv.at[i * L + l],
                    device_id=(x, y, 1 - c), device_id_type=MESH).wait_recv()
        for cp in cps:
            cp.wait_send()

    return pl.pallas_call(
        body, out_shape=[SDS(g.shape, g.dtype) for g in gs], in_specs=[ANY] * n, out_specs=[ANY] * n,
        input_output_aliases={i: i for i in range(n)},
        scratch_shapes=[pltpu.SemaphoreType.DMA((n * L,))] * 2,
        name="grad_pair_share", compiler_params=pltpu.CompilerParams(has_side_effects=True))(*gs)


def _small_allgather(buf):
    R, C = buf.shape

    def body(b_ref, o_ref, send, recv, loc):
        x, y, c, _ = _place()
        me = 4 * x + 2 * y + c
        own = pltpu.make_async_copy(b_ref, o_ref.at[me], loc)
        own.start()
        flips = [(fx, fy, fc) for fx in (0, 1) for fy in (0, 1) for fc in (0, 1)][1:]
        cps = []
        for j, (fx, fy, fc) in enumerate(flips):
            cp = pltpu.make_async_remote_copy(
                src_ref=b_ref, dst_ref=o_ref.at[me], send_sem=send.at[j], recv_sem=recv.at[j],
                device_id=(x ^ fx, y ^ fy, c ^ fc), device_id_type=MESH)
            cp.start()
            cps.append(cp)
        for j, (fx, fy, fc) in enumerate(flips):
            got = o_ref.at[4 * (x ^ fx) + 2 * (y ^ fy) + (c ^ fc)]
            pltpu.make_async_remote_copy(
                src_ref=got, dst_ref=got, send_sem=send.at[j], recv_sem=recv.at[j],
                device_id=(x ^ fx, y ^ fy, c ^ fc), device_id_type=MESH).wait_recv()
        for cp in cps:
            cp.wait_send()
        own.wait()

    return pl.pallas_call(
        body, out_shape=SDS((N_DEV, R, C), f32), in_specs=[ANY], out_specs=ANY,
        scratch_shapes=[pltpu.SemaphoreType.DMA((N_DEV - 1,))] * 2 + [pltpu.SemaphoreType.DMA],
        name="small_allgather", compiler_params=pltpu.CompilerParams(has_side_effects=True))(buf)


def _small_sum(g):
    n, R, C = g.shape

    def body(g_ref, o_ref):
        acc = g_ref[0]
        for j in range(1, n):
            acc = acc + g_ref[j]
        o_ref[...] = acc
    return pl.pallas_call(body, out_shape=SDS((R, C), f32), name="small_sum", compiler_params=_cp())(g)


PACK_COLS = 1024


def _rows_of(shape):
    n = int(np.prod(shape)) if len(shape) else 1
    return -(-n // (8 * PACK_COLS)) * 8


def _pack(parts):
    blocks = []
    for p in parts:
        flat = p.reshape(-1)
        r = _rows_of(p.shape)
        blocks.append(jnp.pad(flat, (0, r * PACK_COLS - flat.shape[0])).reshape(r, PACK_COLS))
    return jnp.concatenate(blocks, axis=0)


def _unpack(buf, shapes):
    out, off = [], 0
    for s in shapes:
        n = int(np.prod(s)) if len(s) else 1
        r = _rows_of(s)
        out.append(buf[off:off + r].reshape(-1)[:n].reshape(s))
        off += r
    return out


def _block_diag(w):
    g, a, _ = w.shape
    out = jnp.zeros((g * a, g * a), w.dtype)
    for i in range(g):
        out = lax.dynamic_update_slice(out, w[i], (i * a, i * a))
    return out


def kernel(x, w_in, w_out, sgu_w, sgu_b, pool_w, pool_scale, swa_sinks, rel_bias, mix_out_gain, norm_mix, norm_ffn, w_gate_up, w_down, norm_final, loss_target, m_w_in, m_w_out, m_sgu_w, m_sgu_b, m_pool_w, m_pool_scale, m_swa_sinks, m_rel_bias, m_mix_out_gain, m_norm_mix, m_norm_ffn, m_w_gate_up, m_w_down, m_norm_final, v_w_in, v_w_out, v_sgu_w, v_sgu_b, v_pool_w, v_pool_scale, v_swa_sinks, v_rel_bias, v_mix_out_gain, v_norm_mix, v_norm_ffn, v_w_gate_up, v_w_down, v_norm_final):
    B, S, D = x.shape
    T = B * S
    L = w_in.shape[0]
    tm = min(512, T)
    F = w_down.shape[1] * N_CHIPS
    xi, yi, ci = lax.axis_index("x"), lax.axis_index("y"), lax.axis_index("c")
    cidx = jnp.reshape(ci, (1,)).astype(jnp.int32)
    kidx = jnp.reshape(2 * xi + yi, (1,)).astype(jnp.int32)

    big = [w_in, w_out, w_gate_up, w_down]
    slots = [_cast_slots(w, kidx) for w in big]
    weights = [None] * L
    weights[0] = _comm_only(_GatherComm([slots[pi][0] for pi in range(4)]), "gather_weights")

    bucket = jnp.asarray(_t5_bucket_table().reshape(1, -1))
    bias_tab = _bias_expand(rel_bias.T, bucket).reshape(4, BLK, 2 * BLK)

    row = lambda v: v.reshape(1, -1)
    xc = x.reshape(T, D)
    tgt = loss_target.reshape(T, D)
    saved = []
    Win, Wo, Wgu, Wd = ([None] * L for _ in range(4))
    for l in range(L):
        Win[l], Wgu[l] = weights[l][0], weights[l][2]
        Wo[l], Wd[l] = weights[l][1].reshape(D, D), weights[l][3].reshape(F, D)
        h1, proj = _norm_mm(xc, row(norm_mix[l]), Win[l], tm)
        bexp = jnp.repeat(sgu_b[l].T, HD, axis=1)
        wbd = _block_diag(pool_w[l])
        sk = jnp.broadcast_to(swa_sinks[l][:, None, None], (4, 1, BLK))
        ya = _sgu_fwd(proj, sgu_w[l], bexp, B, S)
        yb = _pool_fwd(proj, wbd, row(pool_scale[l]), B, S)
        yc = _swa_fwd(proj, sk, bias_tab, B, S)
        if l + 1 < L:
            yd, lt, *weights[l + 1] = _sb_fwd(proj, B, S, _GatherComm([slots[pi][l + 1] for pi in range(4)]))
        else:
            yd, lt = _sb_fwd(proj, B, S)
        ys = (ya, yb, yc, yd)
        ycn, x1 = _gnorm_mm_res(ys, row(mix_out_gain[l]), Wo[l], xc, tm)
        h2, gu, act = _norm_mm_swiglu(x1, row(norm_ffn[l]), Wgu[l], tm)
        x2 = _mm_res(act, Wd[l], x1, tm)
        saved.append((xc, h1, proj, bexp, wbd, sk, ys, lt, ycn, x1, h2, gu, act))
        xc = x2

    dx, g_final, loss_v = _final_loss(xc, row(norm_final), tgt, tm)

    tk = tm
    gW = [[None] * L for _ in range(4)]
    g_sgu_w, g_sgu_b, g_pool_w, g_pool_scale, g_sinks, g_bias = ([None] * L for _ in range(6))
    g_out_gain, g_mix, g_ffn = ([None] * L for _ in range(3))
    reduced = [None] * 4

    def pair_sums(l, r1):
        return zip(*[_pair_add(gW[pi][l], r1[pi], cidx, "grad_pair_add") for pi in range(4)])

    def chip_sums(l, qf, r2):
        idx = jnp.stack([2 * xi + yi, ci, jnp.int32(l)]).astype(jnp.int32)
        for pi in range(4):
            reduced[pi] = _chip_add(qf[pi], r2[pi], idx, reduced[pi], L, "grad_chip_add")

    for l in reversed(range(L)):
        x0, h1, proj, bexp, wbd, sk, ys, lt, ycn, x1, h2, gu, act = saved[l]
        above = l + 1 if l + 1 < L else None
        dgu = _dact(dx, Wd[l], gu, tm)
        gW[3][l] = _dw(act, dx, lambda t, s: (t, 0), D, 1, F // 2, tk, "dw_down").reshape(N_CHIPS, F // N_CHIPS, D)
        gW[2][l] = _dw(h2, dgu, lambda t, s: (s // 2, t, s % 2), F // 2, N_CHIPS, D, tk, "dw_gate_up")
        comm = None if above is None else _PairExchangeComm([gW[pi][above] for pi in range(4)])
        dx1, g_ffn[l], *r1 = _dx_norm_bwd(dgu, lambda i, s: (s // 2, i, s % 2), Wgu[l], x1, row(norm_ffn[l]), dx,
                                          N_CHIPS, tm, "dx_ffn" if comm is None else "dx_ffn_exchange", comm)
        if above is not None:
            qf, qb = pair_sums(above, r1)
        gW[1][l] = _dw(ycn, dx1, lambda t, s: (t, 0), D, 1, D, tk, "dw_out").reshape(N_CHIPS, D // N_CHIPS, D)
        dya, dyb, dyc, dyd, g_out_gain[l] = _dycat(dx1, Wo[l], ys, row(mix_out_gain[l]), tm)
        dpa, g_sgu_w[l], dbf = _sgu_bwd(proj, sgu_w[l], bexp, dya, B, S)
        g_sgu_b[l] = dbf[:, ::HD].T
        dpb, dwbd, dsc = _pool_bwd(proj, wbd, row(pool_scale[l]), dyb, B, S)
        npg = len(POOL_WINDOWS)
        g_pool_w[l] = jnp.stack([dwbd[i * HD:(i + 1) * HD, i * HD:(i + 1) * HD] for i in range(npg)])
        g_pool_scale[l] = dsc[0]
        dcq, dckv, dsk, g_bias[l] = _swa_bwd(proj, sk, bias_tab, dyc, B, S)
        g_sinks[l] = dsk[:, 0, 0] * float(BLK)
        comm = None if above is None else _ChipExchangeComm(list(qb))
        ddq, ddk, ddv, *r2 = _sb_bwd(proj, lt, dyd, B, S, comm)
        if above is not None:
            chip_sums(above, qf, r2)
        dproj = jnp.concatenate([dpa, dpb, dcq, dckv, ddq, ddk, ddv], axis=1)
        gW[0][l] = _dw(h1, dproj, lambda t, s: (t, s), w_in.shape[2], N_CHIPS, D, tk, "dw_in")
        dx, g_mix[l] = _dx_norm_bwd(dproj, lambda i, s: (i, s), Win[l], x0, row(norm_mix[l]), dx1,
                                    N_CHIPS, tm, "dx_mix")
    grad_x = dx.reshape(B, S, D)

    r1 = _comm_only(_PairExchangeComm([gW[pi][0] for pi in range(4)]), "grad_pair_exchange")
    qf, qb = pair_sums(0, r1)
    r2 = _comm_only(_ChipExchangeComm(list(qb)), "grad_chip_exchange")
    chip_sums(0, qf, r2)
    g_big = _pair_share(reduced, [g.shape[1] // 2 for g in reduced])

    g_rel_bias = _bias_reduce([g.reshape(4, -1) for g in g_bias], bucket).T
    small_g = [jnp.stack(g_sgu_w), jnp.stack(g_sgu_b), jnp.stack(g_pool_w), jnp.stack(g_pool_scale), jnp.stack(g_sinks),
               g_rel_bias, jnp.concatenate(g_out_gain), jnp.concatenate(g_mix), jnp.concatenate(g_ffn), g_final[0]]
    small_w = [sgu_w, sgu_b, pool_w, pool_scale, swa_sinks, rel_bias, mix_out_gain, norm_mix, norm_ffn, norm_final]
    small_m = [m_sgu_w, m_sgu_b, m_pool_w, m_pool_scale, m_swa_sinks, m_rel_bias, m_mix_out_gain, m_norm_mix, m_norm_ffn, m_norm_final]
    small_v = [v_sgu_w, v_sgu_b, v_pool_w, v_pool_scale, v_swa_sinks, v_rel_bias, v_mix_out_gain, v_norm_mix, v_norm_ffn, v_norm_final]
    shapes = [w.shape for w in small_w]
    packed = _small_sum(_small_allgather(_pack(small_g + [loss_v[0, 0:1]])))
    *g_small, loss = _unpack(packed, shapes + [()])
    g_small_packed = _pack(g_small)
    ds, ms, vs = _adamw(_pack(small_w), g_small_packed, _pack(small_m), _pack(small_v), g_small_packed.shape[0], "adamw_small")
    d_small, m_small, v_small = _unpack(ds, shapes), _unpack(ms, shapes), _unpack(vs, shapes)

    big_m = [m_w_in, m_w_out, m_w_gate_up, m_w_down]
    big_v = [v_w_in, v_w_out, v_w_gate_up, v_w_down]
    d_big, m_big, v_big = [], [], []
    for w, g, m, v in zip(big, g_big, big_m, big_v):
        two = lambda a: a.reshape(-1, a.shape[-1])
        rows = two(w).shape[0]
        d2, m2, v2 = _adamw(two(w), two(g), two(m), two(v), rows // 8 if rows >= 2048 else rows, "adamw_big")
        d_big.append(d2.reshape(w.shape))
        m_big.append(m2.reshape(w.shape))
        v_big.append(v2.reshape(w.shape))

    def order(bigs, smalls):
        return [bigs[0], bigs[1]] + list(smalls[0:9]) + [bigs[2], bigs[3], smalls[9]]

    return (loss, grad_x, *order(g_big, g_small), *order(d_big, d_small), *order(m_big, m_small), *order(v_big, v_small))
```

```python
import functools

import numpy as np
import jax
import jax.numpy as jnp
from jax import lax
from jax.experimental import pallas as pl
from jax.experimental.pallas import tpu as pltpu

f32 = jnp.float32
bf16 = jnp.bfloat16
_MXU = jnp.bfloat16

EPS = 1e-6
HD = 64
GW = 256
BLK = 128
SB_UNROLL = 2
POOL_WINDOWS = (2, 4, 8, 16)
N_BUCKETS = 32
MAX_DISTANCE = 128
N_CHIPS = 4
N_DEV = 8
VMEM_BYTES_V7X = 64 * 1024 * 1024
VMEM_LIMIT = 48 * 1024 * 1024

ADAM_LR = 0.001
ADAM_B1 = 0.9
ADAM_B2 = 0.999
ADAM_EPS = 1e-08
ADAM_WD = 0.01
ADAM_STEP = 10

SDS = jax.ShapeDtypeStruct
MESH = pl.DeviceIdType.MESH
HIGHEST = lax.Precision.HIGHEST
NT = (((1,), (1,)), ((), ()))
TN = (((0,), (0,)), ((), ()))


def _cp(*sem):
    return pltpu.CompilerParams(dimension_semantics=sem if sem else None, vmem_limit_bytes=VMEM_LIMIT)


def _mx(v):
    return v.astype(_MXU)


def _iota(shape, dim):
    return lax.broadcasted_iota(jnp.int32, shape, dim)


def _split_dot(a, tri):
    hi = a.astype(bf16)
    lo = (a - hi.astype(f32)).astype(bf16)
    return jnp.dot(hi, tri, preferred_element_type=f32) + jnp.dot(lo, tri, preferred_element_type=f32)


def _rms(xv):
    return lax.rsqrt(jnp.mean(xv * xv, axis=-1, keepdims=True) + EPS)


def _norm_mm(x, gain, w, tm):
    T, D = x.shape
    NS, _, ns = w.shape

    def body(x_ref, g_ref, w_ref, h_ref, o_ref):
        @pl.when(pl.program_id(1) == 0)
        def _():
            xv = x_ref[...]
            h_ref[...] = (xv * _rms(xv) * g_ref[...]).astype(bf16)
        o_ref[...] = jnp.dot(_mx(h_ref[...]), w_ref[0], preferred_element_type=f32)

    return pl.pallas_call(
        body, grid=(T // tm, NS),
        in_specs=[pl.BlockSpec((tm, D), lambda i, s: (i, 0)),
                  pl.BlockSpec((1, D), lambda i, s: (0, 0)),
                  pl.BlockSpec((1, D, ns), lambda i, s: (s, 0, 0))],
        out_specs=[pl.BlockSpec((tm, D), lambda i, s: (i, 0)),
                   pl.BlockSpec((tm, ns), lambda i, s: (i, s))],
        out_shape=[SDS((T, D), bf16), SDS((T, NS * ns), f32)],
        name="norm_mm_in", compiler_params=_cp("parallel", "arbitrary"))(x, gain, w)


def _norm_mm_swiglu(x, gain, w, tm):
    T, D = x.shape
    NS, _, ns = w.shape
    half = NS // 2

    def body(x_ref, g_ref, wg_ref, wu_ref, h_ref, gu_ref, a_ref):
        @pl.when(pl.program_id(1) == 0)
        def _():
            xv = x_ref[...]
            h_ref[...] = (xv * _rms(xv) * g_ref[...]).astype(bf16)
        h = _mx(h_ref[...])
        g = jnp.dot(h, wg_ref[0], preferred_element_type=f32)
        u = jnp.dot(h, wu_ref[0], preferred_element_type=f32)
        gu_ref[0] = g.astype(bf16)
        gu_ref[1] = u.astype(bf16)
        a_ref[...] = (jax.nn.silu(g) * u).astype(bf16)

    return pl.pallas_call(
        body, grid=(T // tm, half),
        in_specs=[pl.BlockSpec((tm, D), lambda i, s: (i, 0)),
                  pl.BlockSpec((1, D), lambda i, s: (0, 0)),
                  pl.BlockSpec((1, D, ns), lambda i, s: (s, 0, 0)),
                  pl.BlockSpec((1, D, ns), lambda i, s: (s + half, 0, 0))],
        out_specs=[pl.BlockSpec((tm, D), lambda i, s: (i, 0)),
                   pl.BlockSpec((2, tm, ns), lambda i, s: (0, i, s)),
                   pl.BlockSpec((tm, ns), lambda i, s: (i, s))],
        out_shape=[SDS((T, D), bf16), SDS((2, T, half * ns), bf16), SDS((T, half * ns), bf16)],
        name="norm_mm_swiglu", compiler_params=_cp("parallel", "arbitrary"))(x, gain, w, w)


def _gnorm_mm_res(ys, gain, w, x, tm):
    T, D = x.shape

    def body(ya, yb, yc, yd, g_ref, w_ref, x_ref, yn_ref, o_ref):
        parts = []
        for m, r in enumerate((ya, yb, yc, yd)):
            y = r[...]
            parts.append((y * _rms(y) * g_ref[:, m * GW:(m + 1) * GW]).astype(bf16))
        yn = jnp.concatenate(parts, axis=1)
        yn_ref[...] = yn
        o_ref[...] = x_ref[...] + jnp.dot(_mx(yn), w_ref[...], preferred_element_type=f32)

    yspec = pl.BlockSpec((tm, GW), lambda i: (i, 0))
    return pl.pallas_call(
        body, grid=(T // tm,),
        in_specs=[yspec, yspec, yspec, yspec,
                  pl.BlockSpec((1, D), lambda i: (0, 0)),
                  pl.BlockSpec((D, D), lambda i: (0, 0)),
                  pl.BlockSpec((tm, D), lambda i: (i, 0))],
        out_specs=[pl.BlockSpec((tm, D), lambda i: (i, 0)), pl.BlockSpec((tm, D), lambda i: (i, 0))],
        out_shape=[SDS((T, D), bf16), SDS((T, D), f32)],
        name="gnorm_mm_res", compiler_params=_cp("parallel"))(*ys, gain, w, x)


def _mm_res(a, w, x, tm):
    T, D = x.shape
    K = a.shape[1]

    def body(a_ref, w_ref, x_ref, o_ref):
        o_ref[...] = x_ref[...] + jnp.dot(_mx(a_ref[...]), w_ref[...], preferred_element_type=f32)

    return pl.pallas_call(
        body, grid=(T // tm,),
        in_specs=[pl.BlockSpec((tm, K), lambda i: (i, 0)),
                  pl.BlockSpec((K, D), lambda i: (0, 0)),
                  pl.BlockSpec((tm, D), lambda i: (i, 0))],
        out_specs=pl.BlockSpec((tm, D), lambda i: (i, 0)),
        out_shape=SDS((T, D), f32),
        name="mm_res_down", compiler_params=_cp("parallel"))(a, w, x)


def _final_loss(x, gain, tgt, tm):
    T, D = x.shape

    def body(x_ref, g_ref, t_ref, dx_ref, dg_ref, l_ref):
        @pl.when(pl.program_id(0) == 0)
        def _():
            dg_ref[...] = jnp.zeros_like(dg_ref)
            l_ref[...] = jnp.zeros_like(l_ref)
        xv = x_ref[...]
        g = g_ref[...]
        r = _rms(xv)
        xh = xv * r
        err = xh * g - t_ref[...]
        l_ref[...] += 0.5 * jnp.sum(jnp.mean(err * err, axis=-1, keepdims=True), axis=0, keepdims=True)
        dy = err * (1.0 / D)
        dg_ref[...] += jnp.sum(dy * xh, axis=0, keepdims=True)
        dxh = dy * g
        dx_ref[...] = r * (dxh - xh * jnp.mean(dxh * xh, axis=-1, keepdims=True))

    return pl.pallas_call(
        body, grid=(T // tm,),
        in_specs=[pl.BlockSpec((tm, D), lambda i: (i, 0)),
                  pl.BlockSpec((1, D), lambda i: (0, 0)),
                  pl.BlockSpec((tm, D), lambda i: (i, 0))],
        out_specs=[pl.BlockSpec((tm, D), lambda i: (i, 0)),
                   pl.BlockSpec((1, D), lambda i: (0, 0)),
                   pl.BlockSpec((1, BLK), lambda i: (0, 0))],
        out_shape=[SDS((T, D), f32), SDS((1, D), f32), SDS((1, BLK), f32)],
        name="final_loss", compiler_params=_cp("arbitrary"))(x, gain, tgt)


def _dact(dx, wd, gu, tm):
    T, D = dx.shape
    F = wd.shape[0]
    ns = F // 2

    def body(dx_ref, w_ref, gu_ref, o_ref):
        da = lax.dot_general(_mx(dx_ref[...]), w_ref[...], NT, preferred_element_type=f32)
        g = gu_ref[0].astype(f32)
        u = gu_ref[1].astype(f32)
        sg = jax.nn.sigmoid(g)
        o_ref[0] = (da * u * (sg * (1.0 + g * (1.0 - sg)))).astype(bf16)
        o_ref[1] = (da * (g * sg)).astype(bf16)

    return pl.pallas_call(
        body, grid=(T // tm, 2),
        in_specs=[pl.BlockSpec((tm, D), lambda i, s: (i, 0)),
                  pl.BlockSpec((ns, D), lambda i, s: (s, 0)),
                  pl.BlockSpec((2, tm, ns), lambda i, s: (0, i, s))],
        out_specs=pl.BlockSpec((2, tm, ns), lambda i, s: (0, i, s)),
        out_shape=SDS((2, T, F), bf16),
        name="dact", compiler_params=_cp("parallel", "arbitrary"))(dx, wd, gu)


def _dw(a, b, b_map, ns, NS, tka, tk, name):
    T, Ka = a.shape
    b_block = (tk, ns) if b.ndim == 2 else (1, tk, ns)

    def body(a_ref, b_ref, o_ref):
        bv = b_ref[...] if b.ndim == 2 else b_ref[0]
        part = lax.dot_general(_mx(a_ref[...]), _mx(bv), TN, preferred_element_type=f32)

        @pl.when(pl.program_id(2) == 0)
        def _():
            o_ref[0] = part

        @pl.when(pl.program_id(2) > 0)
        def _():
            o_ref[0] += part

    return pl.pallas_call(
        body, grid=(NS, Ka // tka, T // tk),
        in_specs=[pl.BlockSpec((tk, tka), lambda s, k, t: (t, k)),
                  pl.BlockSpec(b_block, lambda s, k, t: b_map(t, s))],
        out_specs=pl.BlockSpec((1, tka, ns), lambda s, k, t: (s, k, 0)),
        out_shape=SDS((NS, Ka, ns), f32),
        name=name, compiler_params=_cp("parallel", "parallel", "arbitrary"))(a, b)


def _dx_norm_bwd(dy, dy_map, w, x, gain, dxin, nk, tm, name, comm=None):
    T, D = x.shape
    ns = w.shape[2]
    dy_block = (tm, ns) if dy.ndim == 2 else (1, tm, ns)

    def body(dy_ref, w_ref, x_ref, g_ref, dxin_ref, dx_ref, dg_ref, acc_ref):
        i, s = pl.program_id(0), pl.program_id(1)
        dv = dy_ref[...] if dy.ndim == 2 else dy_ref[0]
        part = lax.dot_general(_mx(dv), w_ref[0], NT, preferred_element_type=f32)

        @pl.when(s == 0)
        def _():
            acc_ref[...] = part

        @pl.when(s > 0)
        def _():
            acc_ref[...] += part

        @pl.when(s == nk - 1)
        def _():
            @pl.when(i == 0)
            def _():
                dg_ref[...] = jnp.zeros_like(dg_ref)
            dh = acc_ref[...]
            xv = x_ref[...]
            r = _rms(xv)
            xh = xv * r
            dg_ref[...] += jnp.sum(dh * xh, axis=0, keepdims=True)
            dxh = dh * g_ref[...]
            dx_ref[...] = dxin_ref[...] + r * (dxh - xh * jnp.mean(dxh * xh, axis=-1, keepdims=True))

    c_args, c_in, c_out, c_shapes, aliases, c_scratch = _host_specs(comm, 5, 2)
    first = lambda: (pl.program_id(0) == 0) & (pl.program_id(1) == 0)
    last = lambda: (pl.program_id(0) == T // tm - 1) & (pl.program_id(1) == nk - 1)
    return pl.pallas_call(
        _host(body, 5, 2, 1, comm, first, last), grid=(T // tm, nk),
        in_specs=[pl.BlockSpec(dy_block, lambda i, s: dy_map(i, s)),
                  pl.BlockSpec((1, D, ns), lambda i, s: (s, 0, 0)),
                  pl.BlockSpec((tm, D), lambda i, s: (i, 0)),
                  pl.BlockSpec((1, D), lambda i, s: (0, 0)),
                  pl.BlockSpec((tm, D), lambda i, s: (i, 0))] + c_in,
        out_specs=[pl.BlockSpec((tm, D), lambda i, s: (i, 0)),
                   pl.BlockSpec((1, D), lambda i, s: (0, 0))] + c_out,
        out_shape=[SDS((T, D), f32), SDS((1, D), f32)] + c_shapes,
        input_output_aliases=aliases,
        scratch_shapes=[pltpu.VMEM((tm, D), f32)] + c_scratch,
        name=name, compiler_params=_cp("arbitrary", "arbitrary"))(dy, w, x, gain, dxin, *c_args)


def _dycat(dx, w, ys, gain, tm):
    T, D = dx.shape

    def body(dx_ref, w_ref, ya, yb, yc, yd, g_ref, da, db, dc, dd, dg_ref):
        @pl.when(pl.program_id(0) == 0)
        def _():
            dg_ref[...] = jnp.zeros_like(dg_ref)
        dyn = lax.dot_general(_mx(dx_ref[...]), w_ref[...], NT, preferred_element_type=f32)
        for m, (r, o) in enumerate(((ya, da), (yb, db), (yc, dc), (yd, dd))):
            cols = slice(m * GW, (m + 1) * GW)
            y = r[...]
            rs = _rms(y)
            yh = y * rs
            d = dyn[:, cols]
            dg_ref[:, cols] += jnp.sum(d * yh, axis=0, keepdims=True)
            dyh = d * g_ref[:, cols]
            o[...] = rs * (dyh - yh * jnp.mean(dyh * yh, axis=-1, keepdims=True))

    yspec = pl.BlockSpec((tm, GW), lambda i: (i, 0))
    return pl.pallas_call(
        body, grid=(T // tm,),
        in_specs=[pl.BlockSpec((tm, D), lambda i: (i, 0)),
                  pl.BlockSpec((D, D), lambda i: (0, 0)),
                  yspec, yspec, yspec, yspec,
                  pl.BlockSpec((1, D), lambda i: (0, 0))],
        out_specs=[yspec, yspec, yspec, yspec, pl.BlockSpec((1, D), lambda i: (0, 0))],
        out_shape=[SDS((T, GW), f32)] * 4 + [SDS((1, D), f32)],
        name="dycat", compiler_params=_cp("arbitrary"))(dx, w, *ys, gain)


def _sgu_consts():
    r, c = _iota((GW, GW), 0), _iota((GW, GW), 1)
    seg = (r // HD == c // HD).astype(f32)
    tr, ts = _iota((BLK, BLK), 0), _iota((BLK, BLK), 1)
    causal = ts <= tr
    lane_head = _iota((BLK, GW), 1) // HD
    return seg, causal, lane_head


def _split3_dot(a, ones):
    hi = a.astype(bf16)
    r1 = a - hi.astype(f32)
    mid = r1.astype(bf16)
    lo = (r1 - mid.astype(f32)).astype(bf16)
    dot = functools.partial(jnp.dot, preferred_element_type=f32)
    return dot(hi, ones) + dot(mid, ones) + dot(lo, ones)


def _sgu_chunks(aus, avs, w, bexp, consts):
    seg, causal, lane_head = consts
    segb = seg.astype(bf16)
    nh = GW // HD
    vs = [jax.nn.gelu(av) for av in avs]
    mus = [_split3_dot(v, segb) * (1.0 / HD) for v in vs]
    vcs = [v - mu for v, mu in zip(vs, mus)]
    vars_ = [_split3_dot(vc * vc, segb) * (1.0 / HD) for vc in vcs]
    vns = [_mx(vc * lax.rsqrt(var + EPS)) for vc, var in zip(vcs, vars_)]
    whs = [_mx(jnp.where(causal, w[h], 0.0)) for h in range(nh)]
    mixes = [[jnp.dot(whs[h], vn, preferred_element_type=f32) for h in range(nh)] for vn in vns]
    out = []
    for au, ms in zip(aus, mixes):
        mix = bexp
        for h in range(nh):
            mix = mix + jnp.where(lane_head == h, ms[h], 0.0)
        out.append(jax.nn.gelu(au) * mix)
    return out


def _sgu_group(S):
    nc = S // BLK
    return 4 if nc % 4 == 0 else (2 if nc % 2 == 0 else 1)


def _sgu_fwd(proj, w, bexp, B, S):
    G = _sgu_group(S)

    def body(au_ref, av_ref, w_ref, b_ref, y_ref):
        consts = _sgu_consts()
        wv, bv = w_ref[...], b_ref[...]

        def group(n, c):
            rows = [pl.ds(pl.multiple_of((n * G + j) * BLK, BLK), BLK) for j in range(G)]
            ys = _sgu_chunks([au_ref[r, :] for r in rows], [av_ref[r, :] for r in rows], wv, bv, consts)
            for r, y in zip(rows, ys):
                y_ref[r, :] = y
            return c
        lax.fori_loop(0, S // BLK // G, group, 0)

    return pl.pallas_call(
        body, grid=(B,),
        in_specs=[pl.BlockSpec((S, GW), lambda b: (b, 0)),
                  pl.BlockSpec((S, GW), lambda b: (b, 1)),
                  pl.BlockSpec((GW // HD, BLK, BLK), lambda b: (0, 0, 0)),
                  pl.BlockSpec((BLK, GW), lambda b: (0, 0))],
        out_specs=pl.BlockSpec((S, GW), lambda b: (b, 0)),
        out_shape=SDS((B * S, GW), f32),
        name="sgu_fwd", compiler_params=_cp("parallel"))(proj, proj, w, bexp)


def _sgu_bwd(proj, w, bexp, dy, B, S):
    def body(au_ref, av_ref, w_ref, b_ref, dy_ref, dp_ref, dw_ref, db_ref):
        @pl.when(pl.program_id(0) == 0)
        def _():
            dw_ref[...] = jnp.zeros_like(dw_ref)
            db_ref[...] = jnp.zeros_like(db_ref)
        consts = _sgu_consts()
        wv, bv = w_ref[...], b_ref[...]
        fn = lambda aus, avs, ww, bb: _sgu_chunks(aus, avs, ww, bb, consts)
        G = _sgu_group(S)

        def group(n, carry):
            dw_acc, db_acc = carry
            rows = [pl.ds(pl.multiple_of((n * G + j) * BLK, BLK), BLK) for j in range(G)]
            _, vjp = jax.vjp(fn, [au_ref[r, :] for r in rows], [av_ref[r, :] for r in rows], wv, bv)
            daus, davs, dwc, dbc = vjp([dy_ref[r, :] for r in rows])
            for r, dau, dav in zip(rows, daus, davs):
                dp_ref[r, 0:GW] = dau.astype(bf16)
                dp_ref[r, GW:2 * GW] = dav.astype(bf16)
            return dw_acc + dwc, db_acc + dbc
        dw_acc, db_acc = lax.fori_loop(0, S // BLK // G, group, (jnp.zeros(wv.shape, f32), jnp.zeros(bv.shape, f32)))
        dw_ref[...] += dw_acc
        db_ref[...] += jnp.dot(db_acc, consts[0], precision=HIGHEST, preferred_element_type=f32)

    return pl.pallas_call(
        body, grid=(B,),
        in_specs=[pl.BlockSpec((S, GW), lambda b: (b, 0)),
                  pl.BlockSpec((S, GW), lambda b: (b, 1)),
                  pl.BlockSpec((GW // HD, BLK, BLK), lambda b: (0, 0, 0)),
                  pl.BlockSpec((BLK, GW), lambda b: (0, 0)),
                  pl.BlockSpec((S, GW), lambda b: (b, 0))],
        out_specs=[pl.BlockSpec((S, 2 * GW), lambda b: (b, 0)),
                   pl.BlockSpec((GW // HD, BLK, BLK), lambda b: (0, 0, 0)),
                   pl.BlockSpec((BLK, GW), lambda b: (0, 0))],
        out_shape=[SDS((B * S, 2 * GW), bf16), SDS((GW // HD, BLK, BLK), f32), SDS((BLK, GW), f32)],
        name="sgu_bwd", compiler_params=_cp("arbitrary"))(proj, proj, w, bexp, dy)


def _pool_parts(p):
    n = p.shape[0]
    r = _iota(p.shape, 0)
    lg = _iota(p.shape, 1) // HD

    def sh(v, k):
        return jnp.where(r >= k, pltpu.roll(v, k, 0), 0.0)
    s2 = p + sh(p, 1)
    s4 = s2 + sh(s2, 2)
    s8 = s4 + sh(s4, 4)
    s16 = s8 + sh(s8, 8)
    ws = jnp.where(lg == 0, s2, jnp.where(lg == 1, s4, jnp.where(lg == 2, s8, s16)))
    wlen = jnp.where(lg == 0, 2, jnp.where(lg == 1, 4, jnp.where(lg == 2, 8, 16)))
    cnt = jnp.minimum(r + 1, wlen).astype(f32)
    del n
    return ws / cnt - p, cnt, lg


def _pool_fwd(proj, wbd, scale, B, S):
    def body(p_ref, w_ref, s_ref, y_ref):
        y, _, _ = _pool_parts(p_ref[...])
        y_ref[...] = jnp.dot(_mx(y), _mx(w_ref[...]), preferred_element_type=f32) * s_ref[...]

    return pl.pallas_call(
        body, grid=(B,),
        in_specs=[pl.BlockSpec((S, GW), lambda b: (b, 2)),
                  pl.BlockSpec((GW, GW), lambda b: (0, 0)),
                  pl.BlockSpec((1, GW), lambda b: (0, 0))],
        out_specs=pl.BlockSpec((S, GW), lambda b: (b, 0)),
        out_shape=SDS((B * S, GW), f32),
        name="pool_fwd", compiler_params=_cp("parallel"))(proj, wbd, scale)


def _pool_bwd(proj, wbd, scale, dy, B, S):
    def body(p_ref, w_ref, s_ref, dy_ref, dp_ref, dw_ref, ds_ref):
        @pl.when(pl.program_id(0) == 0)
        def _():
            dw_ref[...] = jnp.zeros_like(dw_ref)
            ds_ref[...] = jnp.zeros_like(ds_ref)
        y, cnt, lg = _pool_parts(p_ref[...])
        wv = _mx(w_ref[...])
        z = jnp.dot(_mx(y), wv, preferred_element_type=f32)
        dout = dy_ref[...]
        ds_ref[...] += jnp.sum(dout * z, axis=0, keepdims=True)
        dz = _mx(dout * s_ref[...])
        dw_ref[...] += lax.dot_general(_mx(y), dz, TN, preferred_element_type=f32)
        dyv = lax.dot_general(dz, wv, NT, preferred_element_type=f32)
        n = dyv.shape[0]
        r = _iota(dyv.shape, 0)

        def ush(v, k):
            return jnp.where(r < n - k, pltpu.roll(v, n - k, 0), 0.0)
        gq = dyv / cnt
        a2 = gq + ush(gq, 1)
        a4 = a2 + ush(a2, 2)
        a8 = a4 + ush(a4, 4)
        a16 = a8 + ush(a8, 8)
        adj = jnp.where(lg == 0, a2, jnp.where(lg == 1, a4, jnp.where(lg == 2, a8, a16)))
        dp_ref[...] = (adj - dyv).astype(bf16)

    return pl.pallas_call(
        body, grid=(B,),
        in_specs=[pl.BlockSpec((S, GW), lambda b: (b, 2)),
                  pl.BlockSpec((GW, GW), lambda b: (0, 0)),
                  pl.BlockSpec((1, GW), lambda b: (0, 0)),
                  pl.BlockSpec((S, GW), lambda b: (b, 0))],
        out_specs=[pl.BlockSpec((S, GW), lambda b: (b, 0)),
                   pl.BlockSpec((GW, GW), lambda b: (0, 0)),
                   pl.BlockSpec((1, GW), lambda b: (0, 0))],
        out_shape=[SDS((B * S, GW), bf16), SDS((GW, GW), f32), SDS((1, GW), f32)],
        name="pool_bwd", compiler_params=_cp("arbitrary"))(proj, wbd, scale, dy)


def _t5_bucket_table():
    dist = (np.arange(BLK)[:, None] + BLK) - np.arange(2 * BLK)[None, :]
    d = np.clip(dist, 0, BLK - 1)
    max_exact = N_BUCKETS // 2
    df = np.maximum(d, 1).astype(np.float32)
    large = max_exact + (np.log(df / max_exact) / np.float32(np.log(MAX_DISTANCE / max_exact))
                         * (N_BUCKETS - max_exact)).astype(np.int32)
    large = np.minimum(large, N_BUCKETS - 1)
    return np.where(d < max_exact, d, large).astype(np.int32)


def _swa_block(qb0, qb1, k2, v2, sinks, biases, n):
    heads = [(p, g) for p in range(2) for g in range(2)]
    ri, ci = _iota((BLK, BLK), 0), _iota((BLK, BLK), 1)
    qi, ki = _iota((BLK, 2 * BLK), 0), _iota((BLK, 2 * BLK), 1)
    dist = qi + BLK - ki
    mask = (dist >= 0) & (dist < BLK) & ((ki >= BLK) | (n > 0))
    qbs, kb, vb = (_mx(qb0), _mx(qb1)), _mx(k2), _mx(v2)
    qs, vs = [], []
    for p, g in heads:
        selq = ((ri - g * HD == ci - p * HD) & (ri >= g * HD) & (ri < (g + 1) * HD)).astype(_MXU)
        selv = ((ci - g * HD == ri - p * HD) & (ci >= g * HD) & (ci < (g + 1) * HD)).astype(_MXU)
        qs.append(_mx(jnp.dot(qbs[p], selq, preferred_element_type=f32)))
        vs.append(_mx(jnp.dot(vb, selv, preferred_element_type=f32)))
    zs = [lax.dot_general(q, kb, NT, preferred_element_type=f32) * (HD ** -0.5) for q in qs]
    prs = []
    for h in range(4):
        z = jnp.where(mask, zs[h] + biases[h], -1e30)
        s = jnp.mean(sinks[h], axis=-1, keepdims=True)
        m = jnp.maximum(jnp.max(z, axis=-1, keepdims=True), s)
        e = jnp.exp(z - m)
        prs.append(_mx(e / (jnp.sum(e, axis=-1, keepdims=True) + jnp.exp(s - m))))
    outs = [jnp.dot(prs[h], vs[h], preferred_element_type=f32) for h in range(4)]
    return outs[0] + outs[1], outs[2] + outs[3]


def _swa_fwd(proj, sinks, bias, B, S):
    def body(q_ref, kv_ref, s_ref, b_ref, y_ref):
        def block(n, c):
            rows = pl.ds(pl.multiple_of(n * BLK, BLK), BLK)
            prev = pl.ds(pl.multiple_of(jnp.maximum(n - 1, 0) * BLK, BLK), BLK)
            k2 = jnp.concatenate([kv_ref[prev, 0:BLK], kv_ref[rows, 0:BLK]], axis=0)
            v2 = jnp.concatenate([kv_ref[prev, BLK:2 * BLK], kv_ref[rows, BLK:2 * BLK]], axis=0)
            o0, o1 = _swa_block(q_ref[rows, 0:BLK], q_ref[rows, BLK:2 * BLK], k2, v2,
                                [s_ref[h] for h in range(4)], [b_ref[h] for h in range(4)], n)
            y_ref[rows, 0:BLK] = o0
            y_ref[rows, BLK:2 * BLK] = o1
            return c
        lax.fori_loop(0, S // BLK, block, 0)

    return pl.pallas_call(
        body, grid=(B,),
        in_specs=[pl.BlockSpec((S, GW), lambda b: (b, 3)),
                  pl.BlockSpec((S, GW), lambda b: (b, 4)),
                  pl.BlockSpec((4, 1, BLK), lambda b: (0, 0, 0)),
                  pl.BlockSpec((4, BLK, 2 * BLK), lambda b: (0, 0, 0))],
        out_specs=pl.BlockSpec((S, GW), lambda b: (b, 0)),
        out_shape=SDS((B * S, GW), f32),
        name="swa_fwd", compiler_params=_cp("parallel"))(proj, proj, sinks, bias)


def _swa_bwd(proj, sinks, bias, dy, B, S):
    def body(q_ref, kv_ref, s_ref, b_ref, dy_ref, dq_ref, dkv_ref, ds_ref, db_ref, acc_ref):
        @pl.when(pl.program_id(0) == 0)
        def _():
            ds_ref[...] = jnp.zeros_like(ds_ref)
            db_ref[...] = jnp.zeros_like(db_ref)
        acc_ref[...] = jnp.zeros_like(acc_ref)

        def block(n, c):
            rows = pl.ds(pl.multiple_of(n * BLK, BLK), BLK)
            prev = pl.ds(pl.multiple_of(jnp.maximum(n - 1, 0) * BLK, BLK), BLK)
            k2 = jnp.concatenate([kv_ref[prev, 0:BLK], kv_ref[rows, 0:BLK]], axis=0)
            v2 = jnp.concatenate([kv_ref[prev, BLK:2 * BLK], kv_ref[rows, BLK:2 * BLK]], axis=0)
            fn = functools.partial(_swa_block, n=n)
            _, vjp = jax.vjp(fn, q_ref[rows, 0:BLK], q_ref[rows, BLK:2 * BLK], k2, v2,
                             [s_ref[h] for h in range(4)], [b_ref[h] for h in range(4)])
            dq0, dq1, dk2, dv2, dss, dbs = vjp((dy_ref[rows, 0:BLK], dy_ref[rows, BLK:2 * BLK]))
            dq_ref[rows, 0:BLK] = dq0.astype(bf16)
            dq_ref[rows, BLK:2 * BLK] = dq1.astype(bf16)
            for h in range(4):
                ds_ref[h] += dss[h]
                db_ref[h] += dbs[h]
            acc_ref[prev, 0:BLK] += dk2[0:BLK]
            acc_ref[rows, 0:BLK] += dk2[BLK:2 * BLK]
            acc_ref[prev, BLK:2 * BLK] += dv2[0:BLK]
            acc_ref[rows, BLK:2 * BLK] += dv2[BLK:2 * BLK]
            return c
        lax.fori_loop(0, S // BLK, block, 0)
        dkv_ref[...] = acc_ref[...].astype(bf16)

    return pl.pallas_call(
        body, grid=(B,),
        in_specs=[pl.BlockSpec((S, GW), lambda b: (b, 3)),
                  pl.BlockSpec((S, GW), lambda b: (b, 4)),
                  pl.BlockSpec((4, 1, BLK), lambda b: (0, 0, 0)),
                  pl.BlockSpec((4, BLK, 2 * BLK), lambda b: (0, 0, 0)),
                  pl.BlockSpec((S, GW), lambda b: (b, 0))],
        out_specs=[pl.BlockSpec((S, GW), lambda b: (b, 0)),
                   pl.BlockSpec((S, GW), lambda b: (b, 0)),
                   pl.BlockSpec((4, 1, BLK), lambda b: (0, 0, 0)),
                   pl.BlockSpec((4, BLK, 2 * BLK), lambda b: (0, 0, 0))],
        out_shape=[SDS((B * S, GW), bf16), SDS((B * S, GW), bf16), SDS((4, 1, BLK), f32), SDS((4, BLK, 2 * BLK), f32)],
        scratch_shapes=[pltpu.VMEM((S, GW), f32)],
        name="swa_bwd", compiler_params=_cp("arbitrary"))(proj, proj, sinks, bias, dy)


def _log1m(z):
    return jnp.minimum(-z, 0.0) - jnp.log1p(jnp.exp(-jnp.abs(z)))


def _sb_consts(tri):
    r2, c2 = _iota((2 * BLK, 2 * BLK), 0), _iota((2 * BLK, 2 * BLK), 1)
    tri2 = (tri(r2, c2) & (r2 // BLK == c2 // BLK)).astype(bf16)
    ri, ci = _iota((BLK, 2 * BLK), 0), _iota((BLK, 2 * BLK), 1)
    strict2 = (ci % BLK) < ri
    head0 = _iota((BLK, BLK), 1) < HD
    return tri2, strict2, head0


def _sb_load_kv(k_ref, v_ref, kb, head0):
    krows = pl.ds(pl.multiple_of(kb * BLK, BLK), BLK)
    kks, vvs = [], []
    for p in range(2):
        k = k_ref[krows, p * BLK:(p + 1) * BLK]
        v = v_ref[krows, p * BLK:(p + 1) * BLK]
        kks.append(_mx(jnp.concatenate([jnp.where(head0, k, 0.0), jnp.where(head0, 0.0, k)], axis=0)))
        vvs.append(_mx(jnp.concatenate([jnp.where(head0, v, 0.0), jnp.where(head0, 0.0, v)], axis=0)))
    return kks, vvs


def _two_halves(a, b):
    return jnp.concatenate([jnp.broadcast_to(a, (BLK, BLK)), jnp.broadcast_to(b, (BLK, BLK))], axis=1)


def _half_sums(t):
    return jnp.sum(t[:, :BLK], axis=-1, keepdims=True), jnp.sum(t[:, BLK:], axis=-1, keepdims=True)


def _sb_fwd(proj, B, S, comm=None):
    def body(q_ref, k_ref, v_ref, y_ref, lt_ref):
        ci = _iota((BLK, BLK), 1)
        above2, strict2, head0 = _sb_consts(lambda r, c: r > c)

        def step(qs, kbs, carry, diag):
            ok = [None if diag else kb >= 0 for kb in kbs]
            kv = [_sb_load_kv(k_ref, v_ref, jnp.maximum(kb, 0), head0) for kb in kbs]
            zs = [[lax.dot_general(qs[p], kks[p], NT, preferred_element_type=f32) for p in range(2)] for kks, _ in kv]
            Ls = [[_log1m(z) for z in zu] for zu in zs]
            if diag:
                Ls = [[jnp.where(strict2, L, 0.0) for L in Lu] for Lu in Ls]
            tails = [[_split_dot(L, above2) for L in Lu] for Lu in Ls]
            carry = list(carry)
            for u in range(len(kbs)):
                for p in range(2):
                    R0, R1, acc = carry[3 * p:3 * p + 3]
                    w = jnp.exp(zs[u][p] + Ls[u][p] + tails[u][p] + _two_halves(R0, R1))
                    s0, s1 = _half_sums(Ls[u][p])
                    if diag:
                        w = jnp.where(strict2, w, 0.0)
                    else:
                        w, s0, s1 = (jnp.where(ok[u], t, 0.0) for t in (w, s0, s1))
                    acc = acc + jnp.dot(_mx(w), kv[u][1][p], preferred_element_type=f32)
                    carry[3 * p:3 * p + 3] = [R0 + s0, R1 + s1, acc]
            return tuple(carry)

        def qblock(n, c):
            qrows = pl.ds(pl.multiple_of(n * BLK, BLK), BLK)
            qs = [_mx(q_ref[qrows, p * BLK:(p + 1) * BLK] * (HD ** -0.5)) for p in range(2)]
            z1, z2 = jnp.zeros((BLK, 1), f32), jnp.zeros((BLK, BLK), f32)
            carry = step(qs, [n], (z1, z1, z2, z1, z1, z2), True)
            res = lax.fori_loop(0, (n + SB_UNROLL - 1) // SB_UNROLL,
                                lambda i, cr: step(qs, [n - 1 - SB_UNROLL * i - u for u in range(SB_UNROLL)], cr, False), carry)
            lt = jnp.zeros((BLK, BLK), f32)
            for p in range(2):
                y_ref[qrows, p * BLK:(p + 1) * BLK] = res[3 * p + 2]
                lt = lt + jnp.where(ci == 2 * p, res[3 * p], 0.0) + jnp.where(ci == 2 * p + 1, res[3 * p + 1], 0.0)
            lt_ref[qrows, :] = lt
            return c
        lax.fori_loop(0, S // BLK, qblock, 0)

    spec = lambda j: pl.BlockSpec((S, GW), lambda b: (b, j))
    c_args, c_in, c_out, c_shapes, aliases, c_scratch = _host_specs(comm, 3, 2)
    step = lambda v: (lambda: pl.program_id(0) == v)
    return pl.pallas_call(
        _host(body, 3, 2, 0, comm, step(0), step(B - 1)), grid=(B,),
        in_specs=[spec(5), spec(6), spec(7)] + c_in,
        out_specs=[pl.BlockSpec((S, GW), lambda b: (b, 0)), pl.BlockSpec((S, BLK), lambda b: (b, 0))] + c_out,
        out_shape=[SDS((B * S, GW), f32), SDS((B * S, BLK), f32)] + c_shapes,
        input_output_aliases=aliases, scratch_shapes=c_scratch,
        name="sb_fwd" if comm is None else "sb_fwd_gather",
        compiler_params=_cp("arbitrary"))(proj, proj, proj, *c_args)


def _sb_bwd(proj, ltot, dy, B, S, comm=None):
    def body(q_ref, k_ref, v_ref, lt_ref, dy_ref, dq_ref, dk_ref, dv_ref, dka_ref, dva_ref):
        ci = _iota((BLK, BLK), 1)
        upto2, strict2, head0 = _sb_consts(lambda r, c: r <= c)
        below2, _, _ = _sb_consts(lambda r, c: r < c)
        dka_ref[...] = jnp.zeros_like(dka_ref)
        dva_ref[...] = jnp.zeros_like(dva_ref)

        def step(qs, dos, lts, kbs, last, carry, diag):
            U = range(len(kbs))
            ok = [None if diag else kb <= last for kb in kbs]
            kbs = [jnp.minimum(kb, last) for kb in kbs]
            kv = [_sb_load_kv(k_ref, v_ref, kb, head0) for kb in kbs]
            zs = [[lax.dot_general(qs[p], kv[u][0][p], NT, preferred_element_type=f32) for p in range(2)] for u in U]
            dws = [[lax.dot_general(dos[p], kv[u][1][p], NT, preferred_element_type=f32) for p in range(2)] for u in U]
            Ls = [[_log1m(z) for z in zu] for zu in zs]
            if diag:
                Ls = [[jnp.where(strict2, L, 0.0) for L in Lu] for Lu in Ls]
            pins = [[_split_dot(L, upto2) for L in Lu] for Lu in Ls]
            carry = list(carry)
            ws, das = [], []
            for u in U:
                wu, dau = [], []
                for p in range(2):
                    PL0, PL1 = carry[5 * p], carry[5 * p + 1]
                    tail = _two_halves(lts[2 * p] - PL0, lts[2 * p + 1] - PL1) - pins[u][p]
                    w = jnp.exp(zs[u][p] + Ls[u][p] + tail)
                    l0, l1 = _half_sums(Ls[u][p])
                    if diag:
                        w = jnp.where(strict2, w, 0.0)
                    else:
                        w, l0, l1 = (jnp.where(ok[u], t, 0.0) for t in (w, l0, l1))
                    carry[5 * p], carry[5 * p + 1] = PL0 + l0, PL1 + l1
                    wu.append(w)
                    dau.append(w * dws[u][p])
                ws.append(wu)
                das.append(dau)
            pexs = [[_split_dot(da, below2) for da in dau] for dau in das]
            dzs = []
            for u in U:
                dzu = []
                for p in range(2):
                    dL = _two_halves(carry[5 * p + 2], carry[5 * p + 3]) + pexs[u][p]
                    sg = jax.nn.sigmoid(zs[u][p])
                    dz = das[u][p] * (1.0 - sg) - dL * sg
                    dz = jnp.where(strict2 if diag else ok[u], dz, 0.0)
                    a0, a1 = _half_sums(das[u][p])
                    carry[5 * p + 2], carry[5 * p + 3] = carry[5 * p + 2] + a0, carry[5 * p + 3] + a1
                    dzu.append(_mx(dz))
                dzs.append(dzu)
            dqs = [[jnp.dot(dzs[u][p], kv[u][0][p], preferred_element_type=f32) for p in range(2)] for u in U]
            dks = [[lax.dot_general(dzs[u][p], qs[p], TN, preferred_element_type=f32) for p in range(2)] for u in U]
            dvs = [[lax.dot_general(_mx(ws[u][p]), dos[p], TN, preferred_element_type=f32) for p in range(2)] for u in U]
            for u in U:
                krows = pl.ds(pl.multiple_of(kbs[u] * BLK, BLK), BLK)
                for p in range(2):
                    lanes = slice(p * BLK, (p + 1) * BLK)
                    dka_ref[krows, lanes] += jnp.where(head0, dks[u][p][:BLK], dks[u][p][BLK:])
                    dva_ref[krows, lanes] += jnp.where(head0, dvs[u][p][:BLK], dvs[u][p][BLK:])
                    carry[5 * p + 4] = carry[5 * p + 4] + dqs[u][p]
            return tuple(carry)

        def qblock(n, c):
            qrows = pl.ds(pl.multiple_of(n * BLK, BLK), BLK)
            ltb = lt_ref[qrows, :]
            lts = [jnp.sum(jnp.where(ci == h, ltb, 0.0), axis=-1, keepdims=True) for h in range(4)]
            qs = [_mx(q_ref[qrows, p * BLK:(p + 1) * BLK] * (HD ** -0.5)) for p in range(2)]
            dos = [_mx(dy_ref[qrows, p * BLK:(p + 1) * BLK]) for p in range(2)]
            z1, z2 = jnp.zeros((BLK, 1), f32), jnp.zeros((BLK, BLK), f32)
            carry = lax.fori_loop(
                0, (n + SB_UNROLL - 1) // SB_UNROLL,
                lambda i, cr: step(qs, dos, lts, [SB_UNROLL * i + u for u in range(SB_UNROLL)], n - 1, cr, False),
                (z1, z1, z1, z1, z2) * 2)
            res = step(qs, dos, lts, [n], n, carry, True)
            for p in range(2):
                dq_ref[qrows, p * BLK:(p + 1) * BLK] = (res[5 * p + 4] * (HD ** -0.5)).astype(bf16)
            return c
        lax.fori_loop(0, S // BLK, qblock, 0)
        dk_ref[...] = dka_ref[...].astype(bf16)
        dv_ref[...] = dva_ref[...].astype(bf16)

    spec = lambda j: pl.BlockSpec((S, GW), lambda b: (b, j))
    o = pl.BlockSpec((S, GW), lambda b: (b, 0))
    c_args, c_in, c_out, c_shapes, aliases, c_scratch = _host_specs(comm, 5, 3)
    step = lambda v: (lambda: pl.program_id(0) == v)
    return pl.pallas_call(
        _host(body, 5, 3, 2, comm, step(0), step(B - 1)), grid=(B,),
        in_specs=[spec(5), spec(6), spec(7), pl.BlockSpec((S, BLK), lambda b: (b, 0)), o] + c_in,
        out_specs=[o, o, o] + c_out,
        out_shape=[SDS((B * S, GW), bf16)] * 3 + c_shapes,
        input_output_aliases=aliases,
        scratch_shapes=[pltpu.VMEM((S, GW), f32), pltpu.VMEM((S, GW), f32)] + c_scratch,
        name="sb_bwd" if comm is None else "sb_bwd_exchange",
        compiler_params=_cp("arbitrary"))(proj, proj, proj, ltot, dy, *c_args)


def _bias_expand(rel_bias_t, bucket):
    n = bucket.shape[1]

    def body(r_ref, b_ref, o_ref):
        onehot = (_iota((N_BUCKETS, n), 0) == b_ref[...]).astype(f32)
        o_ref[...] = jnp.dot(r_ref[...], onehot, precision=HIGHEST, preferred_element_type=f32)
    return pl.pallas_call(body, out_shape=SDS((rel_bias_t.shape[0], n), f32), name="bias_expand",
                          compiler_params=_cp())(rel_bias_t, bucket)


def _bias_reduce(dbias, bucket):
    n = bucket.shape[1]

    def body(*refs):
        b_ref, g_ref = refs[-2], refs[-1]
        d = refs[0][...]
        for r in refs[1:-2]:
            d = d + r[...]
        onehot = (_iota((N_BUCKETS, n), 0) == b_ref[...]).astype(f32)
        g_ref[...] = lax.dot_general(d, onehot, NT, precision=HIGHEST, preferred_element_type=f32)
    return pl.pallas_call(body, out_shape=SDS((dbias[0].shape[0], N_BUCKETS), f32), name="bias_reduce",
                          compiler_params=_cp())(*dbias, bucket)


def _adamw(w, g, m, v, tr, name):
    R, C = w.shape

    def body(w_ref, g_ref, m_ref, v_ref, d_ref, m2_ref, v2_ref):
        gv = g_ref[...]
        m2 = ADAM_B1 * m_ref[...] + (1.0 - ADAM_B1) * gv
        v2 = ADAM_B2 * v_ref[...] + (1.0 - ADAM_B2) * (gv * gv)
        m_hat = m2 / (1.0 - ADAM_B1 ** ADAM_STEP)
        v_hat = v2 / (1.0 - ADAM_B2 ** ADAM_STEP)
        d_ref[...] = -ADAM_LR * (m_hat / (jnp.sqrt(v_hat) + ADAM_EPS) + ADAM_WD * w_ref[...])
        m2_ref[...] = m2
        v2_ref[...] = v2

    spec = pl.BlockSpec((tr, C), lambda i: (i, 0))
    return pl.pallas_call(
        body, grid=(R // tr,), in_specs=[spec] * 4, out_specs=[spec] * 3,
        out_shape=[SDS((R, C), f32)] * 3, name=name, compiler_params=_cp("parallel"))(w, g, m, v)


ANY = pl.BlockSpec(memory_space=pl.ANY)


def _place():
    x, y, c = lax.axis_index("x"), lax.axis_index("y"), lax.axis_index("c")
    chips = [(1 - x, y), (x, 1 - y), (1 - x, 1 - y)]
    return x, y, c, chips


def _cast_slots(w, kidx):
    L, a, b = w.shape
    ta = a // 2

    def body(k_ref, *refs):
        for l in range(L):
            refs[L + l][0] = refs[l][0].astype(bf16)

    return pl.pallas_call(
        body,
        grid_spec=pltpu.PrefetchScalarGridSpec(
            num_scalar_prefetch=1, grid=(a // ta,),
            in_specs=[pl.BlockSpec((1, ta, b), functools.partial(lambda i, k_ref, l: (l, i, 0), l=l)) for l in range(L)],
            out_specs=[pl.BlockSpec((1, ta, b), lambda i, k_ref: (k_ref[0], i, 0)) for _ in range(L)]),
        out_shape=[SDS((N_CHIPS, a, b), bf16)] * L,
        name="cast_slots", compiler_params=_cp("parallel"))(kidx, *([w] * L))


class _GatherComm:
    def __init__(self, bufs):
        self.inputs = list(bufs)
        self.out_shape = [SDS(b.shape, b.dtype) for b in bufs]
        self.aliased = True
        self.scratch = [pltpu.SemaphoreType.DMA((3 * len(bufs),))] * 4

    def _copies(self, i_refs, o_refs, sems):
        send1, recv1, send2, recv2 = sems
        x, y, c, chips = _place()
        k = 2 * x + y
        first, got1, second, got2 = [], [], [], []
        for i, buf in enumerate(self.inputs):
            h = buf.shape[1] // 2
            mine, theirs = pl.ds(c * h, h), pl.ds((1 - c) * h, h)
            for j, (cx, cy) in enumerate(chips):
                s = 3 * i + j
                first.append(pltpu.make_async_remote_copy(
                    src_ref=i_refs[i].at[k, mine], dst_ref=o_refs[i].at[k, mine], send_sem=send1.at[s],
                    recv_sem=recv1.at[s], device_id=(cx, cy, c), device_id_type=MESH))
                a = o_refs[i].at[2 * cx + cy, mine]
                got1.append(pltpu.make_async_remote_copy(
                    src_ref=a, dst_ref=a, send_sem=send1.at[s], recv_sem=recv1.at[s],
                    device_id=(cx, cy, c), device_id_type=MESH))
                second.append(pltpu.make_async_remote_copy(
                    src_ref=a, dst_ref=a, send_sem=send2.at[s], recv_sem=recv2.at[s],
                    device_id=(x, y, 1 - c), device_id_type=MESH))
                b = o_refs[i].at[2 * cx + cy, theirs]
                got2.append(pltpu.make_async_remote_copy(
                    src_ref=b, dst_ref=b, send_sem=send2.at[s], recv_sem=recv2.at[s],
                    device_id=(x, y, 1 - c), device_id_type=MESH))
        return first, got1, second, got2

    def start(self, i_refs, o_refs, sems):
        for cp in self._copies(i_refs, o_refs, sems)[0]:
            cp.start()

    def finish(self, i_refs, o_refs, sems):
        first, got1, second, got2 = self._copies(i_refs, o_refs, sems)
        for g, cp in zip(got1, second):
            g.wait_recv()
            cp.start()
        for g in got2:
            g.wait_recv()
        for cp in first + second:
            cp.wait_send()


class _PairExchangeComm:
    def __init__(self, gs):
        self.inputs = list(gs)
        self.out_shape = [SDS((g.shape[0], g.shape[1] // 2, g.shape[2]), g.dtype) for g in gs]
        self.aliased = False
        self.scratch = [pltpu.SemaphoreType.DMA((len(gs),))] * 2

    def _copies(self, i_refs, o_refs, sems):
        send, recv = sems
        x, y, c, _ = _place()
        cps = []
        for i, g in enumerate(self.inputs):
            h = g.shape[1] // 2
            cps.append(pltpu.make_async_remote_copy(
                src_ref=i_refs[i].at[:, pl.ds((1 - c) * h, h)], dst_ref=o_refs[i], send_sem=send.at[i], recv_sem=recv.at[i],
                device_id=(x, y, 1 - c), device_id_type=MESH))
        return cps

    def start(self, i_refs, o_refs, sems):
        for cp in self._copies(i_refs, o_refs, sems):
            cp.start()

    def finish(self, i_refs, o_refs, sems):
        for cp in self._copies(i_refs, o_refs, sems):
            cp.wait()


class _ChipExchangeComm:
    def __init__(self, qs):
        self.inputs = list(qs)
        self.out_shape = [SDS(q.shape, q.dtype) for q in qs]
        self.aliased = False
        self.scratch = [pltpu.SemaphoreType.DMA((3 * len(qs),))] * 2

    def _copies(self, i_refs, o_refs, sems):
        send, recv = sems
        x, y, c, chips = _place()
        k = 2 * x + y
        cps, got = [], []
        for i in range(len(self.inputs)):
            for j, (cx, cy) in enumerate(chips):
                s = 3 * i + j
                cps.append(pltpu.make_async_remote_copy(
                    src_ref=i_refs[i].at[2 * cx + cy], dst_ref=o_refs[i].at[k], send_sem=send.at[s],
                    recv_sem=recv.at[s], device_id=(cx, cy, c), device_id_type=MESH))
                a = o_refs[i].at[2 * cx + cy]
                got.append(pltpu.make_async_remote_copy(
                    src_ref=a, dst_ref=a, send_sem=send.at[s], recv_sem=recv.at[s],
                    device_id=(cx, cy, c), device_id_type=MESH))
        return cps, got

    def start(self, i_refs, o_refs, sems):
        for cp in self._copies(i_refs, o_refs, sems)[0]:
            cp.start()

    def finish(self, i_refs, o_refs, sems):
        cps, got = self._copies(i_refs, o_refs, sems)
        for g in got:
            g.wait_recv()
        for cp in cps:
            cp.wait_send()


def _comm_only(comm, name):
    n = len(comm.inputs)

    def body(*refs):
        i_refs, o_refs, sems = refs[:n], refs[n:n + len(comm.out_shape)], refs[n + len(comm.out_shape):]
        comm.start(i_refs, o_refs, sems)
        comm.finish(i_refs, o_refs, sems)

    return pl.pallas_call(
        body, out_shape=comm.out_shape, in_specs=[ANY] * n, out_specs=[ANY] * len(comm.out_shape),
        input_output_aliases={i: i for i in range(n)} if comm.aliased else {},
        scratch_shapes=comm.scratch, name=name,
        compiler_params=pltpu.CompilerParams(has_side_effects=True))(*comm.inputs)


def _host(body, n_in, n_out, n_scratch, comm, first, last):
    if comm is None:
        return body
    ci, co = len(comm.inputs), len(comm.out_shape)

    def wrapped(*refs):
        o = 0
        parts = []
        for n in (n_in, ci, n_out, co, n_scratch):
            parts.append(refs[o:o + n])
            o += n
        hin, cin, hout, cout, hs = parts
        sems = refs[o:]

        @pl.when(first())
        def _():
            comm.start(cin, cout, sems)
        body(*hin, *hout, *hs)

        @pl.when(last())
        def _():
            comm.finish(cin, cout, sems)
    return wrapped


def _host_specs(comm, n_in, n_out):
    if comm is None:
        return [], [], [], [], {}, []
    ci, co = len(comm.inputs), len(comm.out_shape)
    aliases = {n_in + i: n_out + i for i in range(ci)} if comm.aliased else {}
    return comm.inputs, [ANY] * ci, [ANY] * co, comm.out_shape, aliases, comm.scratch


def _pair_add(g, r, cidx, name):
    ns, a, b = g.shape
    h = a // 2
    th = h if h * b * 4 <= 4 * 1024 * 1024 else h // 2

    def body(c_ref, g_ref, r_ref, qf_ref, qb_ref):
        q = g_ref[...] + r_ref[...]
        qf_ref[...] = q
        qb_ref[...] = q.astype(bf16)

    nb = h // th
    spec = pl.BlockSpec((1, th, b), lambda s, i, c_ref: (s, i, 0))
    return pl.pallas_call(
        body,
        grid_spec=pltpu.PrefetchScalarGridSpec(
            num_scalar_prefetch=1, grid=(ns, nb),
            in_specs=[pl.BlockSpec((1, th, b), lambda s, i, c_ref: (s, c_ref[0] * nb + i, 0)), spec],
            out_specs=[spec, spec]),
        out_shape=[SDS((ns, h, b), f32), SDS((ns, h, b), bf16)],
        name=name, compiler_params=_cp("parallel", "parallel"))(cidx, g, r)


def _chip_add(qf, r2, idx, prev, L, name):
    ns, h, b = r2.shape
    th = h if h * b * 4 <= 4 * 1024 * 1024 else h // 2
    nb = h // th

    def body(s_ref, qf_ref, r1_ref, r2_ref, r3_ref, *rest):
        o_ref = rest[-1]
        o_ref[0] = qf_ref[0] + r1_ref[0].astype(f32) + r2_ref[0].astype(f32) + r3_ref[0].astype(f32)

    other = lambda d: pl.BlockSpec((1, th, b), lambda i, s_ref: ((s_ref[0] + d) % ns, i, 0))
    in_specs = [pl.BlockSpec((1, th, b), lambda i, s_ref: (s_ref[0], i, 0)), other(1), other(2), other(3)]
    args = [idx, qf, r2, r2, r2]
    aliases = {}
    if prev is not None:
        in_specs.append(ANY)
        args.append(prev)
        aliases = {5: 0}
    return pl.pallas_call(
        body,
        grid_spec=pltpu.PrefetchScalarGridSpec(
            num_scalar_prefetch=1, grid=(nb,), in_specs=in_specs,
            out_specs=pl.BlockSpec((1, th, b), lambda i, s_ref: (s_ref[2], s_ref[1] * nb + i, 0))),
        out_shape=SDS((L, 2 * h, b), f32), input_output_aliases=aliases,
        name=name, compiler_params=_cp("arbitrary"))(*args)


def _pair_share(gs, hs):
    n = len(gs)
    L = gs[0].shape[0]

    def body(*refs):
        i_refs, o_refs = refs[:n], refs[n:2 * n]
        send, recv = refs[2 * n:]
        x, y, c, _ = _place()
        cps = []
        for i in range(n):
            for l in range(L):
                mine = pl.ds(c * hs[i], hs[i])
                cp = pltpu.make_async_remote_copy(
                    src_ref=i_refs[i].at[l, mine], dst_ref=o_refs[i].at[l, mine], send_sem=send.at[i * L + l],
                    recv_sem=recv.at[i * L + l], device_id=(x, y, 1 - c), device_id_type=MESH)
                cp.start()
                cps.append(cp)
        for i in range(n):
            for l in range(L):
                got = o_refs[i].at[l, pl.ds((1 - c) * hs[i], hs[i])]
                pltpu.make_async_remote_copy(
                    src_ref=got, dst_ref=got, send_sem=send.at[i * L + l], recv_sem=recv.at[i * L + l],
                    device_id=(x, y, 1 - c), device_id_type=MESH).wait_recv()
        for cp in cps:
            cp.wait_send()

    return pl.pallas_call(
        body, out_shape=[SDS(g.shape, g.dtype) for g in gs], in_specs=[ANY] * n, out_specs=[ANY] * n,
        input_output_aliases={i: i for i in range(n)},
        scratch_shapes=[pltpu.SemaphoreType.DMA((n * L,))] * 2,
        name="grad_pair_share", compiler_params=pltpu.CompilerParams(has_side_effects=True))(*gs)


def _small_allgather(buf):
    R, C = buf.shape

    def body(b_ref, o_ref, send, recv, loc):
        x, y, c, _ = _place()
        me = 4 * x + 2 * y + c
        own = pltpu.make_async_copy(b_ref, o_ref.at[me], loc)
        own.start()
        flips = [(fx, fy, fc) for fx in (0, 1) for fy in (0, 1) for fc in (0, 1)][1:]
        cps = []
        for j, (fx, fy, fc) in enumerate(flips):
            cp = pltpu.make_async_remote_copy(
                src_ref=b_ref, dst_ref=o_ref.at[me], send_sem=send.at[j], recv_sem=recv.at[j],
                device_id=(x ^ fx, y ^ fy, c ^ fc), device_id_type=MESH)
            cp.start()
            cps.append(cp)
        for j, (fx, fy, fc) in enumerate(flips):
            got = o_ref.at[4 * (x ^ fx) + 2 * (y ^ fy) + (c ^ fc)]
            pltpu.make_async_remote_copy(
                src_ref=got, dst_ref=got, send_sem=send.at[j], recv_sem=recv.at[j],
                device_id=(x ^ fx, y ^ fy, c ^ fc), device_id_type=MESH).wait_recv()
        for cp in cps:
            cp.wait_send()
        own.wait()

    return pl.pallas_call(
        body, out_shape=SDS((N_DEV, R, C), f32), in_specs=[ANY], out_specs=ANY,
        scratch_shapes=[pltpu.SemaphoreType.DMA((N_DEV - 1,))] * 2 + [pltpu.SemaphoreType.DMA],
        name="small_allgather", compiler_params=pltpu.CompilerParams(has_side_effects=True))(buf)


def _small_sum(g):
    n, R, C = g.shape

    def body(g_ref, o_ref):
        acc = g_ref[0]
        for j in range(1, n):
            acc = acc + g_ref[j]
        o_ref[...] = acc
    return pl.pallas_call(body, out_shape=SDS((R, C), f32), name="small_sum", compiler_params=_cp())(g)


PACK_COLS = 1024


def _rows_of(shape):
    n = int(np.prod(shape)) if len(shape) else 1
    return -(-n // (8 * PACK_COLS)) * 8


def _pack(parts):
    blocks = []
    for p in parts:
        flat = p.reshape(-1)
        r = _rows_of(p.shape)
        blocks.append(jnp.pad(flat, (0, r * PACK_COLS - flat.shape[0])).reshape(r, PACK_COLS))
    return jnp.concatenate(blocks, axis=0)


def _unpack(buf, shapes):
    out, off = [], 0
    for s in shapes:
        n = int(np.prod(s)) if len(s) else 1
        r = _rows_of(s)
        out.append(buf[off:off + r].reshape(-1)[:n].reshape(s))
        off += r
    return out


def _block_diag(w):
    g, a, _ = w.shape
    out = jnp.zeros((g * a, g * a), w.dtype)
    for i in range(g):
        out = lax.dynamic_update_slice(out, w[i], (i * a, i * a))
    return out


def kernel(x, w_in, w_out, sgu_w, sgu_b, pool_w, pool_scale, swa_sinks, rel_bias, mix_out_gain, norm_mix, norm_ffn, w_gate_up, w_down, norm_final, loss_target, m_w_in, m_w_out, m_sgu_w, m_sgu_b, m_pool_w, m_pool_scale, m_swa_sinks, m_rel_bias, m_mix_out_gain, m_norm_mix, m_norm_ffn, m_w_gate_up, m_w_down, m_norm_final, v_w_in, v_w_out, v_sgu_w, v_sgu_b, v_pool_w, v_pool_scale, v_swa_sinks, v_rel_bias, v_mix_out_gain, v_norm_mix, v_norm_ffn, v_w_gate_up, v_w_down, v_norm_final):
    B, S, D = x.shape
    T = B * S
    L = w_in.shape[0]
    tm = min(512, T)
    F = w_down.shape[1] * N_CHIPS
    xi, yi, ci = lax.axis_index("x"), lax.axis_index("y"), lax.axis_index("c")
    cidx = jnp.reshape(ci, (1,)).astype(jnp.int32)
    kidx = jnp.reshape(2 * xi + yi, (1,)).astype(jnp.int32)

    big = [w_in, w_out, w_gate_up, w_down]
    slots = [_cast_slots(w, kidx) for w in big]
    weights = [None] * L
    weights[0] = _comm_only(_GatherComm([slots[pi][0] for pi in range(4)]), "gather_weights")

    bucket = jnp.asarray(_t5_bucket_table().reshape(1, -1))
    bias_tab = _bias_expand(rel_bias.T, bucket).reshape(4, BLK, 2 * BLK)

    row = lambda v: v.reshape(1, -1)
    xc = x.reshape(T, D)
    tgt = loss_target.reshape(T, D)
    saved = []
    Win, Wo, Wgu, Wd = ([None] * L for _ in range(4))
    for l in range(L):
        Win[l], Wgu[l] = weights[l][0], weights[l][2]
        Wo[l], Wd[l] = weights[l][1].reshape(D, D), weights[l][3].reshape(F, D)
        h1, proj = _norm_mm(xc, row(norm_mix[l]), Win[l], tm)
        bexp = jnp.repeat(sgu_b[l].T, HD, axis=1)
        wbd = _block_diag(pool_w[l])
        sk = jnp.broadcast_to(swa_sinks[l][:, None, None], (4, 1, BLK))
        ya = _sgu_fwd(proj, sgu_w[l], bexp, B, S)
        yb = _pool_fwd(proj, wbd, row(pool_scale[l]), B, S)
        yc = _swa_fwd(proj, sk, bias_tab, B, S)
        if l + 1 < L:
            yd, lt, *weights[l + 1] = _sb_fwd(proj, B, S, _GatherComm([slots[pi][l + 1] for pi in range(4)]))
        else:
            yd, lt = _sb_fwd(proj, B, S)
        ys = (ya, yb, yc, yd)
        ycn, x1 = _gnorm_mm_res(ys, row(mix_out_gain[l]), Wo[l], xc, tm)
        h2, gu, act = _norm_mm_swiglu(x1, row(norm_ffn[l]), Wgu[l], tm)
        x2 = _mm_res(act, Wd[l], x1, tm)
        saved.append((xc, h1, proj, bexp, wbd, sk, ys, lt, ycn, x1, h2, gu, act))
        xc = x2

    dx, g_final, loss_v = _final_loss(xc, row(norm_final), tgt, tm)

    tk = tm
    gW = [[None] * L for _ in range(4)]
    g_sgu_w, g_sgu_b, g_pool_w, g_pool_scale, g_sinks, g_bias = ([None] * L for _ in range(6))
    g_out_gain, g_mix, g_ffn = ([None] * L for _ in range(3))
    reduced = [None] * 4

    def pair_sums(l, r1):
        return zip(*[_pair_add(gW[pi][l], r1[pi], cidx, "grad_pair_add") for pi in range(4)])

    def chip_sums(l, qf, r2):
        idx = jnp.stack([2 * xi + yi, ci, jnp.int32(l)]).astype(jnp.int32)
        for pi in range(4):
            reduced[pi] = _chip_add(qf[pi], r2[pi], idx, reduced[pi], L, "grad_chip_add")

    for l in reversed(range(L)):
        x0, h1, proj, bexp, wbd, sk, ys, lt, ycn, x1, h2, gu, act = saved[l]
        above = l + 1 if l + 1 < L else None
        dgu = _dact(dx, Wd[l], gu, tm)
        gW[3][l] = _dw(act, dx, lambda t, s: (t, 0), D, 1, F // 2, tk, "dw_down").reshape(N_CHIPS, F // N_CHIPS, D)
        gW[2][l] = _dw(h2, dgu, lambda t, s: (s // 2, t, s % 2), F // 2, N_CHIPS, D, tk, "dw_gate_up")
        comm = None if above is None else _PairExchangeComm([gW[pi][above] for pi in range(4)])
        dx1, g_ffn[l], *r1 = _dx_norm_bwd(dgu, lambda i, s: (s // 2, i, s % 2), Wgu[l], x1, row(norm_ffn[l]), dx,
                                          N_CHIPS, tm, "dx_ffn" if comm is None else "dx_ffn_exchange", comm)
        if above is not None:
            qf, qb = pair_sums(above, r1)
        gW[1][l] = _dw(ycn, dx1, lambda t, s: (t, 0), D, 1, D, tk, "dw_out").reshape(N_CHIPS, D // N_CHIPS, D)
        dya, dyb, dyc, dyd, g_out_gain[l] = _dycat(dx1, Wo[l], ys, row(mix_out_gain[l]), tm)
        dpa, g_sgu_w[l], dbf = _sgu_bwd(proj, sgu_w[l], bexp, dya, B, S)
        g_sgu_b[l] = dbf[:, ::HD].T
        dpb, dwbd, dsc = _pool_bwd(proj, wbd, row(pool_scale[l]), dyb, B, S)
        npg = len(POOL_WINDOWS)
        g_pool_w[l] = jnp.stack([dwbd[i * HD:(i + 1) * HD, i * HD:(i + 1) * HD] for i in range(npg)])
        g_pool_scale[l] = dsc[0]
        dcq, dckv, dsk, g_bias[l] = _swa_bwd(proj, sk, bias_tab, dyc, B, S)
        g_sinks[l] = dsk[:, 0, 0] * float(BLK)
        comm = None if above is None else _ChipExchangeComm(list(qb))
        ddq, ddk, ddv, *r2 = _sb_bwd(proj, lt, dyd, B, S, comm)
        if above is not None:
            chip_sums(above, qf, r2)
        dproj = jnp.concatenate([dpa, dpb, dcq, dckv, ddq, ddk, ddv], axis=1)
        gW[0][l] = _dw(h1, dproj, lambda t, s: (t, s), w_in.shape[2], N_CHIPS, D, tk, "dw_in")
        dx, g_mix[l] = _dx_norm_bwd(dproj, lambda i, s: (i, s), Win[l], x0, row(norm_mix[l]), dx1,
                                    N_CHIPS, tm, "dx_mix")
    grad_x = dx.reshape(B, S, D)

    r1 = _comm_only(_PairExchangeComm([gW[pi][0] for pi in range(4)]), "grad_pair_exchange")
    qf, qb = pair_sums(0, r1)
    r2 = _comm_only(_ChipExchangeComm(list(qb)), "grad_chip_exchange")
    chip_sums(0, qf, r2)
    g_big = _pair_share(reduced, [g.shape[1] // 2 for g in reduced])

    g_rel_bias = _bias_reduce([g.reshape(4, -1) for g in g_bias], bucket).T
    small_g = [jnp.stack(g_sgu_w), jnp.stack(g_sgu_b), jnp.stack(g_pool_w), jnp.stack(g_pool_scale), jnp.stack(g_sinks),
               g_rel_bias, jnp.concatenate(g_out_gain), jnp.concatenate(g_mix), jnp.concatenate(g_ffn), g_final[0]]
    small_w = [sgu_w, sgu_b, pool_w, pool_scale, swa_sinks, rel_bias, mix_out_gain, norm_mix, norm_ffn, norm_final]
    small_m = [m_sgu_w, m_sgu_b, m_pool_w, m_pool_scale, m_swa_sinks, m_rel_bias, m_mix_out_gain, m_norm_mix, m_norm_ffn, m_norm_final]
    small_v = [v_sgu_w, v_sgu_b, v_pool_w, v_pool_scale, v_swa_sinks, v_rel_bias, v_mix_out_gain, v_norm_mix, v_norm_ffn, v_norm_final]
    shapes = [w.shape for w in small_w]
    packed = _small_sum(_small_allgather(_pack(small_g + [loss_v[0, 0:1]])))
    *g_small, loss = _unpack(packed, shapes + [()])
    g_small_packed = _pack(g_small)
    ds, ms, vs = _adamw(_pack(small_w), g_small_packed, _pack(small_m), _pack(small_v), g_small_packed.shape[0], "adamw_small")
    d_small, m_small, v_small = _unpack(ds, shapes), _unpack(ms, shapes), _unpack(vs, shapes)

    big_m = [m_w_in, m_w_out, m_w_gate_up, m_w_down]
    big_v = [v_w_in, v_w_out, v_w_gate_up, v_w_down]
    d_big, m_big, v_big = [], [], []
    for w, g, m, v in zip(big, g_big, big_m, big_v):
        two = lambda a: a.reshape(-1, a.shape[-1])
        rows = two(w).shape[0]
        d2, m2, v2 = _adamw(two(w), two(g), two(m), two(v), rows // 8 if rows >= 2048 else rows, "adamw_big")
        d_big.append(d2.reshape(w.shape))
        m_big.append(m2.reshape(w.shape))
        v_big.append(v2.reshape(w.shape))

    def order(bigs, smalls):
        return [bigs[0], bigs[1]] + list(smalls[0:9]) + [bigs[2], bigs[3], smalls[9]]

    return (loss, grad_x, *order(g_big, g_small), *order(d_big, d_small), *order(m_big, m_small), *order(v_big, v_small))
```

```python
import functools

import numpy as np
import jax
import jax.numpy as jnp
from jax import lax
from jax.experimental import pallas as pl
from jax.experimental.pallas import tpu as pltpu

f32 = jnp.float32
bf16 = jnp.bfloat16
_MXU = jnp.bfloat16

EPS = 1e-6
HD = 64
GW = 256
BLK = 128
SB_UNROLL = 2
POOL_WINDOWS = (2, 4, 8, 16)
N_BUCKETS = 32
MAX_DISTANCE = 128
N_CHIPS = 4
N_DEV = 8
VMEM_BYTES_V7X = 64 * 1024 * 1024
VMEM_LIMIT = 48 * 1024 * 1024

ADAM_LR = 0.001
ADAM_B1 = 0.9
ADAM_B2 = 0.999
ADAM_EPS = 1e-08
ADAM_WD = 0.01
ADAM_STEP = 10

SDS = jax.ShapeDtypeStruct
MESH = pl.DeviceIdType.MESH
HIGHEST = lax.Precision.HIGHEST
RESIDENT = pl.Buffered(1)
NT = (((1,), (1,)), ((), ()))
TN = (((0,), (0,)), ((), ()))


def _cp(*sem):
    return pltpu.CompilerParams(dimension_semantics=sem if sem else None, vmem_limit_bytes=VMEM_LIMIT)


def _mx(v):
    return v.astype(_MXU)


def _iota(shape, dim):
    return lax.broadcasted_iota(jnp.int32, shape, dim)


def _split_dot(a, tri):
    hi = a.astype(bf16)
    lo = (a - hi.astype(f32)).astype(bf16)
    return jnp.dot(hi, tri, preferred_element_type=f32) + jnp.dot(lo, tri, preferred_element_type=f32)


def _rms(xv):
    return lax.rsqrt(jnp.mean(xv * xv, axis=-1, keepdims=True) + EPS)


def _norm_mm(x, gain, w, tm):
    T, D = x.shape
    NS, _, ns = w.shape

    def body(x_ref, g_ref, w_ref, h_ref, o_ref):
        xv = x_ref[...]
        h = (xv * _rms(xv) * g_ref[...]).astype(bf16)
        h_ref[...] = h
        for s in range(NS):
            o_ref[:, s * ns:(s + 1) * ns] = jnp.dot(_mx(h), w_ref[s], preferred_element_type=f32)

    return pl.pallas_call(
        body, grid=(T // tm,),
        in_specs=[pl.BlockSpec((tm, D), lambda i: (i, 0)),
                  pl.BlockSpec((1, D), lambda i: (0, 0)),
                  pl.BlockSpec((NS, D, ns), lambda i: (0, 0, 0), pipeline_mode=RESIDENT)],
        out_specs=[pl.BlockSpec((tm, D), lambda i: (i, 0)),
                   pl.BlockSpec((tm, NS * ns), lambda i: (i, 0))],
        out_shape=[SDS((T, D), bf16), SDS((T, NS * ns), f32)],
        name="norm_mm_in", compiler_params=_cp("parallel"))(x, gain, w)


def _norm_mm_swiglu(x, gain, w, tm):
    T, D = x.shape
    NS, _, ns = w.shape
    half = NS // 2

    def body(x_ref, g_ref, w_ref, h_ref, gu_ref, a_ref):
        xv = x_ref[...]
        hb = (xv * _rms(xv) * g_ref[...]).astype(bf16)
        h_ref[...] = hb
        h = _mx(hb)
        for s in range(half):
            cols = slice(s * ns, (s + 1) * ns)
            g = jnp.dot(h, w_ref[s], preferred_element_type=f32)
            u = jnp.dot(h, w_ref[s + half], preferred_element_type=f32)
            gu_ref[0, :, cols] = g.astype(bf16)
            gu_ref[1, :, cols] = u.astype(bf16)
            a_ref[:, cols] = (jax.nn.silu(g) * u).astype(bf16)

    return pl.pallas_call(
        body, grid=(T // tm,),
        in_specs=[pl.BlockSpec((tm, D), lambda i: (i, 0)),
                  pl.BlockSpec((1, D), lambda i: (0, 0)),
                  pl.BlockSpec((NS, D, ns), lambda i: (0, 0, 0), pipeline_mode=RESIDENT)],
        out_specs=[pl.BlockSpec((tm, D), lambda i: (i, 0)),
                   pl.BlockSpec((2, tm, half * ns), lambda i: (0, i, 0)),
                   pl.BlockSpec((tm, half * ns), lambda i: (i, 0))],
        out_shape=[SDS((T, D), bf16), SDS((2, T, half * ns), bf16), SDS((T, half * ns), bf16)],
        name="norm_mm_swiglu", compiler_params=_cp("parallel"))(x, gain, w)


def _gnorm_mm_res(ys, gain, w, x, tm):
    T, D = x.shape

    def body(ya, yb, yc, yd, g_ref, w_ref, x_ref, yn_ref, o_ref):
        parts = []
        for m, r in enumerate((ya, yb, yc, yd)):
            y = r[...]
            parts.append((y * _rms(y) * g_ref[:, m * GW:(m + 1) * GW]).astype(bf16))
        yn = jnp.concatenate(parts, axis=1)
        yn_ref[...] = yn
        o_ref[...] = x_ref[...] + jnp.dot(_mx(yn), w_ref[...], preferred_element_type=f32)

    yspec = pl.BlockSpec((tm, GW), lambda i: (i, 0))
    return pl.pallas_call(
        body, grid=(T // tm,),
        in_specs=[yspec, yspec, yspec, yspec,
                  pl.BlockSpec((1, D), lambda i: (0, 0)),
                  pl.BlockSpec((D, D), lambda i: (0, 0)),
                  pl.BlockSpec((tm, D), lambda i: (i, 0))],
        out_specs=[pl.BlockSpec((tm, D), lambda i: (i, 0)), pl.BlockSpec((tm, D), lambda i: (i, 0))],
        out_shape=[SDS((T, D), bf16), SDS((T, D), f32)],
        name="gnorm_mm_res", compiler_params=_cp("parallel"))(*ys, gain, w, x)


def _mm_res(a, w, x, tm):
    T, D = x.shape
    K = a.shape[1]

    def body(a_ref, w_ref, x_ref, o_ref):
        o_ref[...] = x_ref[...] + jnp.dot(_mx(a_ref[...]), w_ref[...], preferred_element_type=f32)

    return pl.pallas_call(
        body, grid=(T // tm,),
        in_specs=[pl.BlockSpec((tm, K), lambda i: (i, 0)),
                  pl.BlockSpec((K, D), lambda i: (0, 0)),
                  pl.BlockSpec((tm, D), lambda i: (i, 0))],
        out_specs=pl.BlockSpec((tm, D), lambda i: (i, 0)),
        out_shape=SDS((T, D), f32),
        name="mm_res_down", compiler_params=_cp("parallel"))(a, w, x)


def _final_loss(x, gain, tgt, tm):
    T, D = x.shape

    def body(x_ref, g_ref, t_ref, dx_ref, dg_ref, l_ref):
        @pl.when(pl.program_id(0) == 0)
        def _():
            dg_ref[...] = jnp.zeros_like(dg_ref)
            l_ref[...] = jnp.zeros_like(l_ref)
        xv = x_ref[...]
        g = g_ref[...]
        r = _rms(xv)
        xh = xv * r
        err = xh * g - t_ref[...]
        l_ref[...] += 0.5 * jnp.sum(jnp.mean(err * err, axis=-1, keepdims=True), axis=0, keepdims=True)
        dy = err * (1.0 / D)
        dg_ref[...] += jnp.sum(dy * xh, axis=0, keepdims=True)
        dxh = dy * g
        dx_ref[...] = r * (dxh - xh * jnp.mean(dxh * xh, axis=-1, keepdims=True))

    return pl.pallas_call(
        body, grid=(T // tm,),
        in_specs=[pl.BlockSpec((tm, D), lambda i: (i, 0)),
                  pl.BlockSpec((1, D), lambda i: (0, 0)),
                  pl.BlockSpec((tm, D), lambda i: (i, 0))],
        out_specs=[pl.BlockSpec((tm, D), lambda i: (i, 0)),
                   pl.BlockSpec((1, D), lambda i: (0, 0)),
                   pl.BlockSpec((1, BLK), lambda i: (0, 0))],
        out_shape=[SDS((T, D), f32), SDS((1, D), f32), SDS((1, BLK), f32)],
        name="final_loss", compiler_params=_cp("arbitrary"))(x, gain, tgt)


def _dact(dx, wd, gu, tm):
    T, D = dx.shape
    F = wd.shape[0]
    ns = F // 2

    def body(dx_ref, w_ref, gu_ref, o_ref):
        dxb = _mx(dx_ref[...])
        for s in range(2):
            cols = slice(s * ns, (s + 1) * ns)
            da = lax.dot_general(dxb, w_ref[s * ns:(s + 1) * ns, :], NT, preferred_element_type=f32)
            g = gu_ref[0, :, cols].astype(f32)
            u = gu_ref[1, :, cols].astype(f32)
            sg = jax.nn.sigmoid(g)
            o_ref[0, :, cols] = (da * u * (sg * (1.0 + g * (1.0 - sg)))).astype(bf16)
            o_ref[1, :, cols] = (da * (g * sg)).astype(bf16)

    return pl.pallas_call(
        body, grid=(T // tm,),
        in_specs=[pl.BlockSpec((tm, D), lambda i: (i, 0)),
                  pl.BlockSpec((F, D), lambda i: (0, 0), pipeline_mode=RESIDENT),
                  pl.BlockSpec((2, tm, F), lambda i: (0, i, 0))],
        out_specs=pl.BlockSpec((2, tm, F), lambda i: (0, i, 0)),
        out_shape=SDS((2, T, F), bf16),
        name="dact", compiler_params=_cp("parallel"))(dx, wd, gu)


def _dw(a, b, b_map, ns, NS, tka, tk, name):
    T, Ka = a.shape
    b_block = (tk, ns) if b.ndim == 2 else (1, tk, ns)

    def body(a_ref, b_ref, o_ref):
        bv = b_ref[...] if b.ndim == 2 else b_ref[0]
        part = lax.dot_general(_mx(a_ref[...]), _mx(bv), TN, preferred_element_type=f32)

        @pl.when(pl.program_id(2) == 0)
        def _():
            o_ref[0] = part

        @pl.when(pl.program_id(2) > 0)
        def _():
            o_ref[0] += part

    return pl.pallas_call(
        body, grid=(NS, Ka // tka, T // tk),
        in_specs=[pl.BlockSpec((tk, tka), lambda s, k, t: (t, k)),
                  pl.BlockSpec(b_block, lambda s, k, t: b_map(t, s))],
        out_specs=pl.BlockSpec((1, tka, ns), lambda s, k, t: (s, k, 0)),
        out_shape=SDS((NS, Ka, ns), f32),
        name=name, compiler_params=_cp("parallel", "parallel", "arbitrary"))(a, b)


def _dx_norm_bwd(dy, w, x, gain, dxin, tm, name, comm=None):
    T, D = x.shape
    NS, _, ns = w.shape
    half = NS // 2

    def body(dy_ref, w_ref, x_ref, g_ref, dxin_ref, dx_ref, dg_ref):
        @pl.when(pl.program_id(0) == 0)
        def _():
            dg_ref[...] = jnp.zeros_like(dg_ref)
        dh = None
        for s in range(NS):
            if dy.ndim == 2:
                dv = dy_ref[:, s * ns:(s + 1) * ns]
            else:
                dv = dy_ref[s // half, :, (s % half) * ns:(s % half + 1) * ns]
            part = lax.dot_general(_mx(dv), w_ref[s], NT, preferred_element_type=f32)
            dh = part if dh is None else dh + part
        xv = x_ref[...]
        r = _rms(xv)
        xh = xv * r
        dg_ref[...] += jnp.sum(dh * xh, axis=0, keepdims=True)
        dxh = dh * g_ref[...]
        dx_ref[...] = dxin_ref[...] + r * (dxh - xh * jnp.mean(dxh * xh, axis=-1, keepdims=True))

    dy_spec = (pl.BlockSpec((tm, NS * ns), lambda i: (i, 0)) if dy.ndim == 2
               else pl.BlockSpec((2, tm, half * ns), lambda i: (0, i, 0)))
    c_args, c_in, c_out, c_shapes, aliases, c_scratch = _host_specs(comm, 5, 2)
    step = lambda v: (lambda: pl.program_id(0) == v)
    return pl.pallas_call(
        _host(body, 5, 2, 0, comm, step(0), step(T // tm - 1)), grid=(T // tm,),
        in_specs=[dy_spec,
                  pl.BlockSpec((NS, D, ns), lambda i: (0, 0, 0), pipeline_mode=RESIDENT),
                  pl.BlockSpec((tm, D), lambda i: (i, 0)),
                  pl.BlockSpec((1, D), lambda i: (0, 0)),
                  pl.BlockSpec((tm, D), lambda i: (i, 0))] + c_in,
        out_specs=[pl.BlockSpec((tm, D), lambda i: (i, 0)),
                   pl.BlockSpec((1, D), lambda i: (0, 0))] + c_out,
        out_shape=[SDS((T, D), f32), SDS((1, D), f32)] + c_shapes,
        input_output_aliases=aliases, scratch_shapes=c_scratch,
        name=name, compiler_params=_cp("arbitrary"))(dy, w, x, gain, dxin, *c_args)


def _dycat(dx, w, ys, gain, tm):
    T, D = dx.shape

    def body(dx_ref, w_ref, ya, yb, yc, yd, g_ref, da, db, dc, dd, dg_ref):
        @pl.when(pl.program_id(0) == 0)
        def _():
            dg_ref[...] = jnp.zeros_like(dg_ref)
        dyn = lax.dot_general(_mx(dx_ref[...]), w_ref[...], NT, preferred_element_type=f32)
        for m, (r, o) in enumerate(((ya, da), (yb, db), (yc, dc), (yd, dd))):
            cols = slice(m * GW, (m + 1) * GW)
            y = r[...]
            rs = _rms(y)
            yh = y * rs
            d = dyn[:, cols]
            dg_ref[:, cols] += jnp.sum(d * yh, axis=0, keepdims=True)
            dyh = d * g_ref[:, cols]
            o[...] = rs * (dyh - yh * jnp.mean(dyh * yh, axis=-1, keepdims=True))

    yspec = pl.BlockSpec((tm, GW), lambda i: (i, 0))
    return pl.pallas_call(
        body, grid=(T // tm,),
        in_specs=[pl.BlockSpec((tm, D), lambda i: (i, 0)),
                  pl.BlockSpec((D, D), lambda i: (0, 0)),
                  yspec, yspec, yspec, yspec,
                  pl.BlockSpec((1, D), lambda i: (0, 0))],
        out_specs=[yspec, yspec, yspec, yspec, pl.BlockSpec((1, D), lambda i: (0, 0))],
        out_shape=[SDS((T, GW), f32)] * 4 + [SDS((1, D), f32)],
        name="dycat", compiler_params=_cp("arbitrary"))(dx, w, *ys, gain)


def _sgu_consts():
    r, c = _iota((GW, GW), 0), _iota((GW, GW), 1)
    seg = (r // HD == c // HD).astype(f32)
    tr, ts = _iota((BLK, BLK), 0), _iota((BLK, BLK), 1)
    causal = ts <= tr
    lane_head = _iota((BLK, GW), 1) // HD
    return seg, causal, lane_head


def _split3_dot(a, ones):
    hi = a.astype(bf16)
    r1 = a - hi.astype(f32)
    mid = r1.astype(bf16)
    lo = (r1 - mid.astype(f32)).astype(bf16)
    dot = functools.partial(jnp.dot, preferred_element_type=f32)
    return dot(hi, ones) + dot(mid, ones) + dot(lo, ones)


def _sgu_chunks(aus, avs, w, bexp, consts):
    seg, causal, lane_head = consts
    segb = seg.astype(bf16)
    nh = GW // HD
    vs = [jax.nn.gelu(av) for av in avs]
    mus = [_split3_dot(v, segb) * (1.0 / HD) for v in vs]
    vcs = [v - mu for v, mu in zip(vs, mus)]
    vars_ = [_split3_dot(vc * vc, segb) * (1.0 / HD) for vc in vcs]
    vns = [_mx(vc * lax.rsqrt(var + EPS)) for vc, var in zip(vcs, vars_)]
    whs = [_mx(jnp.where(causal, w[h], 0.0)) for h in range(nh)]
    mixes = [[jnp.dot(whs[h], vn, preferred_element_type=f32) for h in range(nh)] for vn in vns]
    out = []
    for au, ms in zip(aus, mixes):
        mix = bexp
        for h in range(nh):
            mix = mix + jnp.where(lane_head == h, ms[h], 0.0)
        out.append(jax.nn.gelu(au) * mix)
    return out


def _sgu_group(S):
    nc = S // BLK
    return 4 if nc % 4 == 0 else (2 if nc % 2 == 0 else 1)


def _sgu_fwd(proj, w, bexp, B, S):
    G = _sgu_group(S)

    def body(au_ref, av_ref, w_ref, b_ref, y_ref):
        consts = _sgu_consts()
        wv, bv = w_ref[...], b_ref[...]

        def group(n, c):
            rows = [pl.ds(pl.multiple_of((n * G + j) * BLK, BLK), BLK) for j in range(G)]
            ys = _sgu_chunks([au_ref[r, :] for r in rows], [av_ref[r, :] for r in rows], wv, bv, consts)
            for r, y in zip(rows, ys):
                y_ref[r, :] = y
            return c
        lax.fori_loop(0, S // BLK // G, group, 0)

    return pl.pallas_call(
        body, grid=(B,),
        in_specs=[pl.BlockSpec((S, GW), lambda b: (b, 0)),
                  pl.BlockSpec((S, GW), lambda b: (b, 1)),
                  pl.BlockSpec((GW // HD, BLK, BLK), lambda b: (0, 0, 0)),
                  pl.BlockSpec((BLK, GW), lambda b: (0, 0))],
        out_specs=pl.BlockSpec((S, GW), lambda b: (b, 0)),
        out_shape=SDS((B * S, GW), f32),
        name="sgu_fwd", compiler_params=_cp("parallel"))(proj, proj, w, bexp)


def _sgu_bwd(proj, w, bexp, dy, B, S):
    def body(au_ref, av_ref, w_ref, b_ref, dy_ref, dp_ref, dw_ref, db_ref):
        @pl.when(pl.program_id(0) == 0)
        def _():
            dw_ref[...] = jnp.zeros_like(dw_ref)
            db_ref[...] = jnp.zeros_like(db_ref)
        consts = _sgu_consts()
        wv, bv = w_ref[...], b_ref[...]
        fn = lambda aus, avs, ww, bb: _sgu_chunks(aus, avs, ww, bb, consts)
        G = _sgu_group(S)

        def group(n, carry):
            dw_acc, db_acc = carry
            rows = [pl.ds(pl.multiple_of((n * G + j) * BLK, BLK), BLK) for j in range(G)]
            _, vjp = jax.vjp(fn, [au_ref[r, :] for r in rows], [av_ref[r, :] for r in rows], wv, bv)
            daus, davs, dwc, dbc = vjp([dy_ref[r, :] for r in rows])
            for r, dau, dav in zip(rows, daus, davs):
                dp_ref[r, 0:GW] = dau.astype(bf16)
                dp_ref[r, GW:2 * GW] = dav.astype(bf16)
            return dw_acc + dwc, db_acc + dbc
        dw_acc, db_acc = lax.fori_loop(0, S // BLK // G, group, (jnp.zeros(wv.shape, f32), jnp.zeros(bv.shape, f32)))
        dw_ref[...] += dw_acc
        db_ref[...] += jnp.dot(db_acc, consts[0], precision=HIGHEST, preferred_element_type=f32)

    return pl.pallas_call(
        body, grid=(B,),
        in_specs=[pl.BlockSpec((S, GW), lambda b: (b, 0)),
                  pl.BlockSpec((S, GW), lambda b: (b, 1)),
                  pl.BlockSpec((GW // HD, BLK, BLK), lambda b: (0, 0, 0)),
                  pl.BlockSpec((BLK, GW), lambda b: (0, 0)),
                  pl.BlockSpec((S, GW), lambda b: (b, 0))],
        out_specs=[pl.BlockSpec((S, 2 * GW), lambda b: (b, 0)),
                   pl.BlockSpec((GW // HD, BLK, BLK), lambda b: (0, 0, 0)),
                   pl.BlockSpec((BLK, GW), lambda b: (0, 0))],
        out_shape=[SDS((B * S, 2 * GW), bf16), SDS((GW // HD, BLK, BLK), f32), SDS((BLK, GW), f32)],
        name="sgu_bwd", compiler_params=_cp("arbitrary"))(proj, proj, w, bexp, dy)


def _pool_parts(p):
    n = p.shape[0]
    r = _iota(p.shape, 0)
    lg = _iota(p.shape, 1) // HD

    def sh(v, k):
        return jnp.where(r >= k, pltpu.roll(v, k, 0), 0.0)
    s2 = p + sh(p, 1)
    s4 = s2 + sh(s2, 2)
    s8 = s4 + sh(s4, 4)
    s16 = s8 + sh(s8, 8)
    ws = jnp.where(lg == 0, s2, jnp.where(lg == 1, s4, jnp.where(lg == 2, s8, s16)))
    wlen = jnp.where(lg == 0, 2, jnp.where(lg == 1, 4, jnp.where(lg == 2, 8, 16)))
    cnt = jnp.minimum(r + 1, wlen).astype(f32)
    del n
    return ws / cnt - p, cnt, lg


def _pool_fwd(proj, wbd, scale, B, S):
    def body(p_ref, w_ref, s_ref, y_ref):
        y, _, _ = _pool_parts(p_ref[...])
        y_ref[...] = jnp.dot(_mx(y), _mx(w_ref[...]), preferred_element_type=f32) * s_ref[...]

    return pl.pallas_call(
        body, grid=(B,),
        in_specs=[pl.BlockSpec((S, GW), lambda b: (b, 2)),
                  pl.BlockSpec((GW, GW), lambda b: (0, 0)),
                  pl.BlockSpec((1, GW), lambda b: (0, 0))],
        out_specs=pl.BlockSpec((S, GW), lambda b: (b, 0)),
        out_shape=SDS((B * S, GW), f32),
        name="pool_fwd", compiler_params=_cp("parallel"))(proj, wbd, scale)


def _pool_bwd(proj, wbd, scale, dy, B, S):
    def body(p_ref, w_ref, s_ref, dy_ref, dp_ref, dw_ref, ds_ref):
        @pl.when(pl.program_id(0) == 0)
        def _():
            dw_ref[...] = jnp.zeros_like(dw_ref)
            ds_ref[...] = jnp.zeros_like(ds_ref)
        y, cnt, lg = _pool_parts(p_ref[...])
        wv = _mx(w_ref[...])
        z = jnp.dot(_mx(y), wv, preferred_element_type=f32)
        dout = dy_ref[...]
        ds_ref[...] += jnp.sum(dout * z, axis=0, keepdims=True)
        dz = _mx(dout * s_ref[...])
        dw_ref[...] += lax.dot_general(_mx(y), dz, TN, preferred_element_type=f32)
        dyv = lax.dot_general(dz, wv, NT, preferred_element_type=f32)
        n = dyv.shape[0]
        r = _iota(dyv.shape, 0)

        def ush(v, k):
            return jnp.where(r < n - k, pltpu.roll(v, n - k, 0), 0.0)
        gq = dyv / cnt
        a2 = gq + ush(gq, 1)
        a4 = a2 + ush(a2, 2)
        a8 = a4 + ush(a4, 4)
        a16 = a8 + ush(a8, 8)
        adj = jnp.where(lg == 0, a2, jnp.where(lg == 1, a4, jnp.where(lg == 2, a8, a16)))
        dp_ref[...] = (adj - dyv).astype(bf16)

    return pl.pallas_call(
        body, grid=(B,),
        in_specs=[pl.BlockSpec((S, GW), lambda b: (b, 2)),
                  pl.BlockSpec((GW, GW), lambda b: (0, 0)),
                  pl.BlockSpec((1, GW), lambda b: (0, 0)),
                  pl.BlockSpec((S, GW), lambda b: (b, 0))],
        out_specs=[pl.BlockSpec((S, GW), lambda b: (b, 0)),
                   pl.BlockSpec((GW, GW), lambda b: (0, 0)),
                   pl.BlockSpec((1, GW), lambda b: (0, 0))],
        out_shape=[SDS((B * S, GW), bf16), SDS((GW, GW), f32), SDS((1, GW), f32)],
        name="pool_bwd", compiler_params=_cp("arbitrary"))(proj, wbd, scale, dy)


def _t5_bucket_table():
    dist = (np.arange(BLK)[:, None] + BLK) - np.arange(2 * BLK)[None, :]
    d = np.clip(dist, 0, BLK - 1)
    max_exact = N_BUCKETS // 2
    df = np.maximum(d, 1).astype(np.float32)
    large = max_exact + (np.log(df / max_exact) / np.float32(np.log(MAX_DISTANCE / max_exact))
                         * (N_BUCKETS - max_exact)).astype(np.int32)
    large = np.minimum(large, N_BUCKETS - 1)
    return np.where(d < max_exact, d, large).astype(np.int32)


def _swa_block(qb0, qb1, k2, v2, sinks, biases, n):
    heads = [(p, g) for p in range(2) for g in range(2)]
    ri, ci = _iota((BLK, BLK), 0), _iota((BLK, BLK), 1)
    qi, ki = _iota((BLK, 2 * BLK), 0), _iota((BLK, 2 * BLK), 1)
    dist = qi + BLK - ki
    mask = (dist >= 0) & (dist < BLK) & ((ki >= BLK) | (n > 0))
    qbs, kb, vb = (_mx(qb0), _mx(qb1)), _mx(k2), _mx(v2)
    qs, vs = [], []
    for p, g in heads:
        selq = ((ri - g * HD == ci - p * HD) & (ri >= g * HD) & (ri < (g + 1) * HD)).astype(_MXU)
        selv = ((ci - g * HD == ri - p * HD) & (ci >= g * HD) & (ci < (g + 1) * HD)).astype(_MXU)
        qs.append(_mx(jnp.dot(qbs[p], selq, preferred_element_type=f32)))
        vs.append(_mx(jnp.dot(vb, selv, preferred_element_type=f32)))
    zs = [lax.dot_general(q, kb, NT, preferred_element_type=f32) * (HD ** -0.5) for q in qs]
    prs = []
    for h in range(4):
        z = jnp.where(mask, zs[h] + biases[h], -1e30)
        s = jnp.mean(sinks[h], axis=-1, keepdims=True)
        m = jnp.maximum(jnp.max(z, axis=-1, keepdims=True), s)
        e = jnp.exp(z - m)
        prs.append(_mx(e / (jnp.sum(e, axis=-1, keepdims=True) + jnp.exp(s - m))))
    outs = [jnp.dot(prs[h], vs[h], preferred_element_type=f32) for h in range(4)]
    return outs[0] + outs[1], outs[2] + outs[3]


def _swa_fwd(proj, sinks, bias, B, S):
    def body(q_ref, kv_ref, s_ref, b_ref, y_ref):
        def block(n, c):
            rows = pl.ds(pl.multiple_of(n * BLK, BLK), BLK)
            prev = pl.ds(pl.multiple_of(jnp.maximum(n - 1, 0) * BLK, BLK), BLK)
            k2 = jnp.concatenate([kv_ref[prev, 0:BLK], kv_ref[rows, 0:BLK]], axis=0)
            v2 = jnp.concatenate([kv_ref[prev, BLK:2 * BLK], kv_ref[rows, BLK:2 * BLK]], axis=0)
            o0, o1 = _swa_block(q_ref[rows, 0:BLK], q_ref[rows, BLK:2 * BLK], k2, v2,
                                [s_ref[h] for h in range(4)], [b_ref[h] for h in range(4)], n)
            y_ref[rows, 0:BLK] = o0
            y_ref[rows, BLK:2 * BLK] = o1
            return c
        lax.fori_loop(0, S // BLK, block, 0)

    return pl.pallas_call(
        body, grid=(B,),
        in_specs=[pl.BlockSpec((S, GW), lambda b: (b, 3)),
                  pl.BlockSpec((S, GW), lambda b: (b, 4)),
                  pl.BlockSpec((4, 1, BLK), lambda b: (0, 0, 0)),
                  pl.BlockSpec((4, BLK, 2 * BLK), lambda b: (0, 0, 0))],
        out_specs=pl.BlockSpec((S, GW), lambda b: (b, 0)),
        out_shape=SDS((B * S, GW), f32),
        name="swa_fwd", compiler_params=_cp("parallel"))(proj, proj, sinks, bias)


def _swa_bwd(proj, sinks, bias, dy, B, S):
    def body(q_ref, kv_ref, s_ref, b_ref, dy_ref, dq_ref, dkv_ref, ds_ref, db_ref, acc_ref):
        @pl.when(pl.program_id(0) == 0)
        def _():
            ds_ref[...] = jnp.zeros_like(ds_ref)
            db_ref[...] = jnp.zeros_like(db_ref)
        acc_ref[...] = jnp.zeros_like(acc_ref)

        def block(n, c):
            rows = pl.ds(pl.multiple_of(n * BLK, BLK), BLK)
            prev = pl.ds(pl.multiple_of(jnp.maximum(n - 1, 0) * BLK, BLK), BLK)
            k2 = jnp.concatenate([kv_ref[prev, 0:BLK], kv_ref[rows, 0:BLK]], axis=0)
            v2 = jnp.concatenate([kv_ref[prev, BLK:2 * BLK], kv_ref[rows, BLK:2 * BLK]], axis=0)
            fn = functools.partial(_swa_block, n=n)
            _, vjp = jax.vjp(fn, q_ref[rows, 0:BLK], q_ref[rows, BLK:2 * BLK], k2, v2,
                             [s_ref[h] for h in range(4)], [b_ref[h] for h in range(4)])
            dq0, dq1, dk2, dv2, dss, dbs = vjp((dy_ref[rows, 0:BLK], dy_ref[rows, BLK:2 * BLK]))
            dq_ref[rows, 0:BLK] = dq0.astype(bf16)
            dq_ref[rows, BLK:2 * BLK] = dq1.astype(bf16)
            for h in range(4):
                ds_ref[h] += dss[h]
                db_ref[h] += dbs[h]
            acc_ref[prev, 0:BLK] += dk2[0:BLK]
            acc_ref[rows, 0:BLK] += dk2[BLK:2 * BLK]
            acc_ref[prev, BLK:2 * BLK] += dv2[0:BLK]
            acc_ref[rows, BLK:2 * BLK] += dv2[BLK:2 * BLK]
            return c
        lax.fori_loop(0, S // BLK, block, 0)
        dkv_ref[...] = acc_ref[...].astype(bf16)

    return pl.pallas_call(
        body, grid=(B,),
        in_specs=[pl.BlockSpec((S, GW), lambda b: (b, 3)),
                  pl.BlockSpec((S, GW), lambda b: (b, 4)),
                  pl.BlockSpec((4, 1, BLK), lambda b: (0, 0, 0)),
                  pl.BlockSpec((4, BLK, 2 * BLK), lambda b: (0, 0, 0)),
                  pl.BlockSpec((S, GW), lambda b: (b, 0))],
        out_specs=[pl.BlockSpec((S, GW), lambda b: (b, 0)),
                   pl.BlockSpec((S, GW), lambda b: (b, 0)),
                   pl.BlockSpec((4, 1, BLK), lambda b: (0, 0, 0)),
                   pl.BlockSpec((4, BLK, 2 * BLK), lambda b: (0, 0, 0))],
        out_shape=[SDS((B * S, GW), bf16), SDS((B * S, GW), bf16), SDS((4, 1, BLK), f32), SDS((4, BLK, 2 * BLK), f32)],
        scratch_shapes=[pltpu.VMEM((S, GW), f32)],
        name="swa_bwd", compiler_params=_cp("arbitrary"))(proj, proj, sinks, bias, dy)


def _log1m(z):
    return jnp.minimum(-z, 0.0) - jnp.log1p(jnp.exp(-jnp.abs(z)))


def _sb_consts(tri):
    r2, c2 = _iota((2 * BLK, 2 * BLK), 0), _iota((2 * BLK, 2 * BLK), 1)
    tri2 = (tri(r2, c2) & (r2 // BLK == c2 // BLK)).astype(bf16)
    ri, ci = _iota((BLK, 2 * BLK), 0), _iota((BLK, 2 * BLK), 1)
    strict2 = (ci % BLK) < ri
    head0 = _iota((BLK, BLK), 1) < HD
    return tri2, strict2, head0


def _sb_load_kv(k_ref, v_ref, kb, head0):
    krows = pl.ds(pl.multiple_of(kb * BLK, BLK), BLK)
    kks, vvs = [], []
    for p in range(2):
        k = k_ref[krows, p * BLK:(p + 1) * BLK]
        v = v_ref[krows, p * BLK:(p + 1) * BLK]
        kks.append(_mx(jnp.concatenate([jnp.where(head0, k, 0.0), jnp.where(head0, 0.0, k)], axis=0)))
        vvs.append(_mx(jnp.concatenate([jnp.where(head0, v, 0.0), jnp.where(head0, 0.0, v)], axis=0)))
    return kks, vvs


def _two_halves(a, b):
    return jnp.concatenate([jnp.broadcast_to(a, (BLK, BLK)), jnp.broadcast_to(b, (BLK, BLK))], axis=1)


def _half_sums(t):
    return jnp.sum(t[:, :BLK], axis=-1, keepdims=True), jnp.sum(t[:, BLK:], axis=-1, keepdims=True)


def _sb_fwd(proj, B, S, comm=None):
    def body(q_ref, k_ref, v_ref, y_ref, lt_ref):
        ci = _iota((BLK, BLK), 1)
        above2, strict2, head0 = _sb_consts(lambda r, c: r > c)

        def step(qs, kbs, carry, diag):
            ok = [None if diag else kb >= 0 for kb in kbs]
            kv = [_sb_load_kv(k_ref, v_ref, jnp.maximum(kb, 0), head0) for kb in kbs]
            zs = [[lax.dot_general(qs[p], kks[p], NT, preferred_element_type=f32) for p in range(2)] for kks, _ in kv]
            Ls = [[_log1m(z) for z in zu] for zu in zs]
            if diag:
                Ls = [[jnp.where(strict2, L, 0.0) for L in Lu] for Lu in Ls]
            tails = [[_split_dot(L, above2) for L in Lu] for Lu in Ls]
            carry = list(carry)
            for u in range(len(kbs)):
                for p in range(2):
                    R0, R1, acc = carry[3 * p:3 * p + 3]
                    w = jnp.exp(zs[u][p] + Ls[u][p] + tails[u][p] + _two_halves(R0, R1))
                    s0, s1 = _half_sums(Ls[u][p])
                    if diag:
                        w = jnp.where(strict2, w, 0.0)
                    else:
                        w, s0, s1 = (jnp.where(ok[u], t, 0.0) for t in (w, s0, s1))
                    acc = acc + jnp.dot(_mx(w), kv[u][1][p], preferred_element_type=f32)
                    carry[3 * p:3 * p + 3] = [R0 + s0, R1 + s1, acc]
            return tuple(carry)

        def qblock(n, c):
            qrows = pl.ds(pl.multiple_of(n * BLK, BLK), BLK)
            qs = [_mx(q_ref[qrows, p * BLK:(p + 1) * BLK] * (HD ** -0.5)) for p in range(2)]
            z1, z2 = jnp.zeros((BLK, 1), f32), jnp.zeros((BLK, BLK), f32)
            carry = step(qs, [n], (z1, z1, z2, z1, z1, z2), True)
            res = lax.fori_loop(0, (n + SB_UNROLL - 1) // SB_UNROLL,
                                lambda i, cr: step(qs, [n - 1 - SB_UNROLL * i - u for u in range(SB_UNROLL)], cr, False), carry)
            lt = jnp.zeros((BLK, BLK), f32)
            for p in range(2):
                y_ref[qrows, p * BLK:(p + 1) * BLK] = res[3 * p + 2]
                lt = lt + jnp.where(ci == 2 * p, res[3 * p], 0.0) + jnp.where(ci == 2 * p + 1, res[3 * p + 1], 0.0)
            lt_ref[qrows, :] = lt
            return c
        lax.fori_loop(0, S // BLK, qblock, 0)

    spec = lambda j: pl.BlockSpec((S, GW), lambda b: (b, j))
    c_args, c_in, c_out, c_shapes, aliases, c_scratch = _host_specs(comm, 3, 2)
    step = lambda v: (lambda: pl.program_id(0) == v)
    return pl.pallas_call(
        _host(body, 3, 2, 0, comm, step(0), step(B - 1)), grid=(B,),
        in_specs=[spec(5), spec(6), spec(7)] + c_in,
        out_specs=[pl.BlockSpec((S, GW), lambda b: (b, 0)), pl.BlockSpec((S, BLK), lambda b: (b, 0))] + c_out,
        out_shape=[SDS((B * S, GW), f32), SDS((B * S, BLK), f32)] + c_shapes,
        input_output_aliases=aliases, scratch_shapes=c_scratch,
        name="sb_fwd" if comm is None else "sb_fwd_gather",
        compiler_params=_cp("arbitrary"))(proj, proj, proj, *c_args)


def _sb_bwd(proj, ltot, dy, B, S, comm=None):
    def body(q_ref, k_ref, v_ref, lt_ref, dy_ref, dq_ref, dk_ref, dv_ref, dka_ref, dva_ref):
        ci = _iota((BLK, BLK), 1)
        upto2, strict2, head0 = _sb_consts(lambda r, c: r <= c)
        below2, _, _ = _sb_consts(lambda r, c: r < c)
        dka_ref[...] = jnp.zeros_like(dka_ref)
        dva_ref[...] = jnp.zeros_like(dva_ref)

        def step(qs, dos, lts, kbs, last, carry, diag):
            U = range(len(kbs))
            ok = [None if diag else kb <= last for kb in kbs]
            kbs = [jnp.minimum(kb, last) for kb in kbs]
            kv = [_sb_load_kv(k_ref, v_ref, kb, head0) for kb in kbs]
            zs = [[lax.dot_general(qs[p], kv[u][0][p], NT, preferred_element_type=f32) for p in range(2)] for u in U]
            dws = [[lax.dot_general(dos[p], kv[u][1][p], NT, preferred_element_type=f32) for p in range(2)] for u in U]
            Ls = [[_log1m(z) for z in zu] for zu in zs]
            if diag:
                Ls = [[jnp.where(strict2, L, 0.0) for L in Lu] for Lu in Ls]
            pins = [[_split_dot(L, upto2) for L in Lu] for Lu in Ls]
            carry = list(carry)
            ws, das = [], []
            for u in U:
                wu, dau = [], []
                for p in range(2):
                    PL0, PL1 = carry[5 * p], carry[5 * p + 1]
                    tail = _two_halves(lts[2 * p] - PL0, lts[2 * p + 1] - PL1) - pins[u][p]
                    w = jnp.exp(zs[u][p] + Ls[u][p] + tail)
                    l0, l1 = _half_sums(Ls[u][p])
                    if diag:
                        w = jnp.where(strict2, w, 0.0)
                    else:
                        w, l0, l1 = (jnp.where(ok[u], t, 0.0) for t in (w, l0, l1))
                    carry[5 * p], carry[5 * p + 1] = PL0 + l0, PL1 + l1
                    wu.append(w)
                    dau.append(w * dws[u][p])
                ws.append(wu)
                das.append(dau)
            pexs = [[_split_dot(da, below2) for da in dau] for dau in das]
            dzs = []
            for u in U:
                dzu = []
                for p in range(2):
                    dL = _two_halves(carry[5 * p + 2], carry[5 * p + 3]) + pexs[u][p]
                    sg = jax.nn.sigmoid(zs[u][p])
                    dz = das[u][p] * (1.0 - sg) - dL * sg
                    dz = jnp.where(strict2 if diag else ok[u], dz, 0.0)
                    a0, a1 = _half_sums(das[u][p])
                    carry[5 * p + 2], carry[5 * p + 3] = carry[5 * p + 2] + a0, carry[5 * p + 3] + a1
                    dzu.append(_mx(dz))
                dzs.append(dzu)
            dqs = [[jnp.dot(dzs[u][p], kv[u][0][p], preferred_element_type=f32) for p in range(2)] for u in U]
            dks = [[lax.dot_general(dzs[u][p], qs[p], TN, preferred_element_type=f32) for p in range(2)] for u in U]
            dvs = [[lax.dot_general(_mx(ws[u][p]), dos[p], TN, preferred_element_type=f32) for p in range(2)] for u in U]
            for u in U:
                krows = pl.ds(pl.multiple_of(kbs[u] * BLK, BLK), BLK)
                for p in range(2):
                    lanes = slice(p * BLK, (p + 1) * BLK)
                    dka_ref[krows, lanes] += jnp.where(head0, dks[u][p][:BLK], dks[u][p][BLK:])
                    dva_ref[krows, lanes] += jnp.where(head0, dvs[u][p][:BLK], dvs[u][p][BLK:])
                    carry[5 * p + 4] = carry[5 * p + 4] + dqs[u][p]
            return tuple(carry)

        def qblock(n, c):
            qrows = pl.ds(pl.multiple_of(n * BLK, BLK), BLK)
            ltb = lt_ref[qrows, :]
            lts = [jnp.sum(jnp.where(ci == h, ltb, 0.0), axis=-1, keepdims=True) for h in range(4)]
            qs = [_mx(q_ref[qrows, p * BLK:(p + 1) * BLK] * (HD ** -0.5)) for p in range(2)]
            dos = [_mx(dy_ref[qrows, p * BLK:(p + 1) * BLK]) for p in range(2)]
            z1, z2 = jnp.zeros((BLK, 1), f32), jnp.zeros((BLK, BLK), f32)
            carry = lax.fori_loop(
                0, (n + SB_UNROLL - 1) // SB_UNROLL,
                lambda i, cr: step(qs, dos, lts, [SB_UNROLL * i + u for u in range(SB_UNROLL)], n - 1, cr, False),
                (z1, z1, z1, z1, z2) * 2)
            res = step(qs, dos, lts, [n], n, carry, True)
            for p in range(2):
                dq_ref[qrows, p * BLK:(p + 1) * BLK] = (res[5 * p + 4] * (HD ** -0.5)).astype(bf16)
            return c
        lax.fori_loop(0, S // BLK, qblock, 0)
        dk_ref[...] = dka_ref[...].astype(bf16)
        dv_ref[...] = dva_ref[...].astype(bf16)

    spec = lambda j: pl.BlockSpec((S, GW), lambda b: (b, j))
    o = pl.BlockSpec((S, GW), lambda b: (b, 0))
    c_args, c_in, c_out, c_shapes, aliases, c_scratch = _host_specs(comm, 5, 3)
    step = lambda v: (lambda: pl.program_id(0) == v)
    return pl.pallas_call(
        _host(body, 5, 3, 2, comm, step(0), step(B - 1)), grid=(B,),
        in_specs=[spec(5), spec(6), spec(7), pl.BlockSpec((S, BLK), lambda b: (b, 0)), o] + c_in,
        out_specs=[o, o, o] + c_out,
        out_shape=[SDS((B * S, GW), bf16)] * 3 + c_shapes,
        input_output_aliases=aliases,
        scratch_shapes=[pltpu.VMEM((S, GW), f32), pltpu.VMEM((S, GW), f32)] + c_scratch,
        name="sb_bwd" if comm is None else "sb_bwd_exchange",
        compiler_params=_cp("arbitrary"))(proj, proj, proj, ltot, dy, *c_args)


def _bias_expand(rel_bias_t, bucket):
    n = bucket.shape[1]

    def body(r_ref, b_ref, o_ref):
        onehot = (_iota((N_BUCKETS, n), 0) == b_ref[...]).astype(f32)
        o_ref[...] = jnp.dot(r_ref[...], onehot, precision=HIGHEST, preferred_element_type=f32)
    return pl.pallas_call(body, out_shape=SDS((rel_bias_t.shape[0], n), f32), name="bias_expand",
                          compiler_params=_cp())(rel_bias_t, bucket)


def _bias_reduce(dbias, bucket):
    n = bucket.shape[1]

    def body(*refs):
        b_ref, g_ref = refs[-2], refs[-1]
        d = refs[0][...]
        for r in refs[1:-2]:
            d = d + r[...]
        onehot = (_iota((N_BUCKETS, n), 0) == b_ref[...]).astype(f32)
        g_ref[...] = lax.dot_general(d, onehot, NT, precision=HIGHEST, preferred_element_type=f32)
    return pl.pallas_call(body, out_shape=SDS((dbias[0].shape[0], N_BUCKETS), f32), name="bias_reduce",
                          compiler_params=_cp())(*dbias, bucket)


def _adamw(w, g, m, v, tr, name):
    R, C = w.shape

    def body(w_ref, g_ref, m_ref, v_ref, d_ref, m2_ref, v2_ref):
        gv = g_ref[...]
        m2 = ADAM_B1 * m_ref[...] + (1.0 - ADAM_B1) * gv
        v2 = ADAM_B2 * v_ref[...] + (1.0 - ADAM_B2) * (gv * gv)
        m_hat = m2 / (1.0 - ADAM_B1 ** ADAM_STEP)
        v_hat = v2 / (1.0 - ADAM_B2 ** ADAM_STEP)
        d_ref[...] = -ADAM_LR * (m_hat / (jnp.sqrt(v_hat) + ADAM_EPS) + ADAM_WD * w_ref[...])
        m2_ref[...] = m2
        v2_ref[...] = v2

    spec = pl.BlockSpec((tr, C), lambda i: (i, 0))
    return pl.pallas_call(
        body, grid=(R // tr,), in_specs=[spec] * 4, out_specs=[spec] * 3,
        out_shape=[SDS((R, C), f32)] * 3, name=name, compiler_params=_cp("parallel"))(w, g, m, v)


ANY = pl.BlockSpec(memory_space=pl.ANY)


def _place():
    x, y, c = lax.axis_index("x"), lax.axis_index("y"), lax.axis_index("c")
    chips = [(1 - x, y), (x, 1 - y), (1 - x, 1 - y)]
    return x, y, c, chips


def _cast_slots(w, kidx):
    L, a, b = w.shape
    ta = a // 2

    def body(k_ref, *refs):
        for l in range(L):
            refs[L + l][0] = refs[l][0].astype(bf16)

    return pl.pallas_call(
        body,
        grid_spec=pltpu.PrefetchScalarGridSpec(
            num_scalar_prefetch=1, grid=(a // ta,),
            in_specs=[pl.BlockSpec((1, ta, b), functools.partial(lambda i, k_ref, l: (l, i, 0), l=l)) for l in range(L)],
            out_specs=[pl.BlockSpec((1, ta, b), lambda i, k_ref: (k_ref[0], i, 0)) for _ in range(L)]),
        out_shape=[SDS((N_CHIPS, a, b), bf16)] * L,
        name="cast_slots", compiler_params=_cp("parallel"))(kidx, *([w] * L))


class _GatherComm:
    def __init__(self, bufs):
        self.inputs = list(bufs)
        self.out_shape = [SDS(b.shape, b.dtype) for b in bufs]
        self.aliased = True
        self.scratch = [pltpu.SemaphoreType.DMA((3 * len(bufs),))] * 4

    def _copies(self, i_refs, o_refs, sems):
        send1, recv1, send2, recv2 = sems
        x, y, c, chips = _place()
        k = 2 * x + y
        first, got1, second, got2 = [], [], [], []
        for i, buf in enumerate(self.inputs):
            h = buf.shape[1] // 2
            mine, theirs = pl.ds(c * h, h), pl.ds((1 - c) * h, h)
            for j, (cx, cy) in enumerate(chips):
                s = 3 * i + j
                first.append(pltpu.make_async_remote_copy(
                    src_ref=i_refs[i].at[k, mine], dst_ref=o_refs[i].at[k, mine], send_sem=send1.at[s],
                    recv_sem=recv1.at[s], device_id=(cx, cy, c), device_id_type=MESH))
                a = o_refs[i].at[2 * cx + cy, mine]
                got1.append(pltpu.make_async_remote_copy(
                    src_ref=a, dst_ref=a, send_sem=send1.at[s], recv_sem=recv1.at[s],
                    device_id=(cx, cy, c), device_id_type=MESH))
                second.append(pltpu.make_async_remote_copy(
                    src_ref=a, dst_ref=a, send_sem=send2.at[s], recv_sem=recv2.at[s],
                    device_id=(x, y, 1 - c), device_id_type=MESH))
                b = o_refs[i].at[2 * cx + cy, theirs]
                got2.append(pltpu.make_async_remote_copy(
                    src_ref=b, dst_ref=b, send_sem=send2.at[s], recv_sem=recv2.at[s],
                    device_id=(x, y, 1 - c), device_id_type=MESH))
        return first, got1, second, got2

    def start(self, i_refs, o_refs, sems):
        for cp in self._copies(i_refs, o_refs, sems)[0]:
            cp.start()

    def finish(self, i_refs, o_refs, sems):
        first, got1, second, got2 = self._copies(i_refs, o_refs, sems)
        for g, cp in zip(got1, second):
            g.wait_recv()
            cp.start()
        for g in got2:
            g.wait_recv()
        for cp in first + second:
            cp.wait_send()


class _PairExchangeComm:
    def __init__(self, gs):
        self.inputs = list(gs)
        self.out_shape = [SDS((g.shape[0], g.shape[1] // 2, g.shape[2]), g.dtype) for g in gs]
        self.aliased = False
        self.scratch = [pltpu.SemaphoreType.DMA((len(gs),))] * 2

    def _copies(self, i_refs, o_refs, sems):
        send, recv = sems
        x, y, c, _ = _place()
        cps = []
        for i, g in enumerate(self.inputs):
            h = g.shape[1] // 2
            cps.append(pltpu.make_async_remote_copy(
                src_ref=i_refs[i].at[:, pl.ds((1 - c) * h, h)], dst_ref=o_refs[i], send_sem=send.at[i], recv_sem=recv.at[i],
                device_id=(x, y, 1 - c), device_id_type=MESH))
        return cps

    def start(self, i_refs, o_refs, sems):
        for cp in self._copies(i_refs, o_refs, sems):
            cp.start()

    def finish(self, i_refs, o_refs, sems):
        for cp in self._copies(i_refs, o_refs, sems):
            cp.wait()


class _ChipExchangeComm:
    def __init__(self, qs):
        self.inputs = list(qs)
        self.out_shape = [SDS(q.shape, q.dtype) for q in qs]
        self.aliased = False
        self.scratch = [pltpu.SemaphoreType.DMA((3 * len(qs),))] * 2

    def _copies(self, i_refs, o_refs, sems):
        send, recv = sems
        x, y, c, chips = _place()
        k = 2 * x + y
        cps, got = [], []
        for i in range(len(self.inputs)):
            for j, (cx, cy) in enumerate(chips):
                s = 3 * i + j
                cps.append(pltpu.make_async_remote_copy(
                    src_ref=i_refs[i].at[2 * cx + cy], dst_ref=o_refs[i].at[k], send_sem=send.at[s],
                    recv_sem=recv.at[s], device_id=(cx, cy, c), device_id_type=MESH))
                a = o_refs[i].at[2 * cx + cy]
                got.append(pltpu.make_async_remote_copy(
                    src_ref=a, dst_ref=a, send_sem=send.at[s], recv_sem=recv.at[s],
                    device_id=(cx, cy, c), device_id_type=MESH))
        return cps, got

    def start(self, i_refs, o_refs, sems):
        for cp in self._copies(i_refs, o_refs, sems)[0]:
            cp.start()

    def finish(self, i_refs, o_refs, sems):
        cps, got = self._copies(i_refs, o_refs, sems)
        for g in got:
            g.wait_recv()
        for cp in cps:
            cp.wait_send()


def _comm_only(comm, name):
    n = len(comm.inputs)

    def body(*refs):
        i_refs, o_refs, sems = refs[:n], refs[n:n + len(comm.out_shape)], refs[n + len(comm.out_shape):]
        comm.start(i_refs, o_refs, sems)
        comm.finish(i_refs, o_refs, sems)

    return pl.pallas_call(
        body, out_shape=comm.out_shape, in_specs=[ANY] * n, out_specs=[ANY] * len(comm.out_shape),
        input_output_aliases={i: i for i in range(n)} if comm.aliased else {},
        scratch_shapes=comm.scratch, name=name,
        compiler_params=pltpu.CompilerParams(has_side_effects=True))(*comm.inputs)


def _host(body, n_in, n_out, n_scratch, comm, first, last):
    if comm is None:
        return body
    ci, co = len(comm.inputs), len(comm.out_shape)

    def wrapped(*refs):
        o = 0
        parts = []
        for n in (n_in, ci, n_out, co, n_scratch):
            parts.append(refs[o:o + n])
            o += n
        hin, cin, hout, cout, hs = parts
        sems = refs[o:]

        @pl.when(first())
        def _():
            comm.start(cin, cout, sems)
        body(*hin, *hout, *hs)

        @pl.when(last())
        def _():
            comm.finish(cin, cout, sems)
    return wrapped


def _host_specs(comm, n_in, n_out):
    if comm is None:
        return [], [], [], [], {}, []
    ci, co = len(comm.inputs), len(comm.out_shape)
    aliases = {n_in + i: n_out + i for i in range(ci)} if comm.aliased else {}
    return comm.inputs, [ANY] * ci, [ANY] * co, comm.out_shape, aliases, comm.scratch


def _pair_add(g, r, cidx, name):
    ns, a, b = g.shape
    h = a // 2
    th = h if h * b * 4 <= 4 * 1024 * 1024 else h // 2

    def body(c_ref, g_ref, r_ref, qf_ref, qb_ref):
        q = g_ref[...] + r_ref[...]
        qf_ref[...] = q
        qb_ref[...] = q.astype(bf16)

    nb = h // th
    spec = pl.BlockSpec((1, th, b), lambda s, i, c_ref: (s, i, 0))
    return pl.pallas_call(
        body,
        grid_spec=pltpu.PrefetchScalarGridSpec(
            num_scalar_prefetch=1, grid=(ns, nb),
            in_specs=[pl.BlockSpec((1, th, b), lambda s, i, c_ref: (s, c_ref[0] * nb + i, 0)), spec],
            out_specs=[spec, spec]),
        out_shape=[SDS((ns, h, b), f32), SDS((ns, h, b), bf16)],
        name=name, compiler_params=_cp("parallel", "parallel"))(cidx, g, r)


def _chip_add(qf, r2, idx, prev, L, name):
    ns, h, b = r2.shape
    th = h if h * b * 4 <= 4 * 1024 * 1024 else h // 2
    nb = h // th

    def body(s_ref, qf_ref, r1_ref, r2_ref, r3_ref, *rest):
        o_ref = rest[-1]
        o_ref[0] = qf_ref[0] + r1_ref[0].astype(f32) + r2_ref[0].astype(f32) + r3_ref[0].astype(f32)

    other = lambda d: pl.BlockSpec((1, th, b), lambda i, s_ref: ((s_ref[0] + d) % ns, i, 0))
    in_specs = [pl.BlockSpec((1, th, b), lambda i, s_ref: (s_ref[0], i, 0)), other(1), other(2), other(3)]
    args = [idx, qf, r2, r2, r2]
    aliases = {}
    if prev is not None:
        in_specs.append(ANY)
        args.append(prev)
        aliases = {5: 0}
    return pl.pallas_call(
        body,
        grid_spec=pltpu.PrefetchScalarGridSpec(
            num_scalar_prefetch=1, grid=(nb,), in_specs=in_specs,
            out_specs=pl.BlockSpec((1, th, b), lambda i, s_ref: (s_ref[2], s_ref[1] * nb + i, 0))),
        out_shape=SDS((L, 2 * h, b), f32), input_output_aliases=aliases,
        name=name, compiler_params=_cp("arbitrary"))(*args)


def _pair_share(gs, hs):
    n = len(gs)
    L = gs[0].shape[0]

    def body(*refs):
        i_refs, o_refs = refs[:n], refs[n:2 * n]
        send, recv = refs[2 * n:]
        x, y, c, _ = _place()
        cps = []
        for i in range(n):
            for l in range(L):
                mine = pl.ds(c * hs[i], hs[i])
                cp = pltpu.make_async_remote_copy(
                    src_ref=i_refs[i].at[l, mine], dst_ref=o_refs[i].at[l, mine], send_sem=send.at[i * L + l],
                    recv_sem=recv.at[i * L + l], device_id=(x, y, 1 - c), device_id_type=MESH)
                cp.start()
                cps.append(cp)
        for i in range(n):
            for l in range(L):
                got = o_refs[i].at[l, pl.ds((1 - c) * hs[i], hs[i])]
                pltpu.make_async_remote_copy(
                    src_ref=got, dst_ref=got, send_sem=send.at[i * L + l], recv_sem=recv.at[i * L + l],
                    device_id=(x, y, 1 - c), device_id_type=MESH).wait_recv()
        for cp in cps:
            cp.wait_send()

    return pl.pallas_call(
        body, out_shape=[SDS(g.shape, g.dtype) for g in gs], in_specs=[ANY] * n, out_specs=[ANY] * n,
        input_output_aliases={i: i for i in range(n)},
        scratch_shapes=[pltpu.SemaphoreType.DMA((n * L,))] * 2,
        name="grad_pair_share", compiler_params=pltpu.CompilerParams(has_side_effects=True))(*gs)


def _small_allgather(buf):
    R, C = buf.shape

    def body(b_ref, o_ref, send, recv, loc):
        x, y, c, _ = _place()
        me = 4 * x + 2 * y + c
        own = pltpu.make_async_copy(b_ref, o_ref.at[me], loc)
        own.start()
        flips = [(fx, fy, fc) for fx in (0, 1) for fy in (0, 1) for fc in (0, 1)][1:]
        cps = []
        for j, (fx, fy, fc) in enumerate(flips):
            cp = pltpu.make_async_remote_copy(
                src_ref=b_ref, dst_ref=o_ref.at[me], send_sem=send.at[j], recv_sem=recv.at[j],
                device_id=(x ^ fx, y ^ fy, c ^ fc), device_id_type=MESH)
            cp.start()
            cps.append(cp)
        for j, (fx, fy, fc) in enumerate(flips):
            got = o_ref.at[4 * (x ^ fx) + 2 * (y ^ fy) + (c ^ fc)]
            pltpu.make_async_remote_copy(
                src_ref=got, dst_ref=got, send_sem=send.at[j], recv_sem=recv.at[j],
                device_id=(x ^ fx, y ^ fy, c ^ fc), device_id_type=MESH).wait_recv()
        for cp in cps:
            cp.wait_send()
        own.wait()

    return pl.pallas_call(
        body, out_shape=SDS((N_DEV, R, C), f32), in_specs=[ANY], out_specs=ANY,
        scratch_shapes=[pltpu.SemaphoreType.DMA((N_DEV - 1,))] * 2 + [pltpu.SemaphoreType.DMA],
        name="small_allgather", compiler_params=pltpu.CompilerParams(has_side_effects=True))(buf)


def _small_sum(g):
    n, R, C = g.shape

    def body(g_ref, o_ref):
        acc = g_ref[0]
        for j in range(1, n):
            acc = acc + g_ref[j]
        o_ref[...] = acc
    return pl.pallas_call(body, out_shape=SDS((R, C), f32), name="small_sum", compiler_params=_cp())(g)


PACK_COLS = 1024


def _rows_of(shape):
    n = int(np.prod(shape)) if len(shape) else 1
    return -(-n // (8 * PACK_COLS)) * 8


def _pack(parts):
    blocks = []
    for p in parts:
        flat = p.reshape(-1)
        r = _rows_of(p.shape)
        blocks.append(jnp.pad(flat, (0, r * PACK_COLS - flat.shape[0])).reshape(r, PACK_COLS))
    return jnp.concatenate(blocks, axis=0)


def _unpack(buf, shapes):
    out, off = [], 0
    for s in shapes:
        n = int(np.prod(s)) if len(s) else 1
        r = _rows_of(s)
        out.append(buf[off:off + r].reshape(-1)[:n].reshape(s))
        off += r
    return out


def _block_diag(w):
    g, a, _ = w.shape
    out = jnp.zeros((g * a, g * a), w.dtype)
    for i in range(g):
        out = lax.dynamic_update_slice(out, w[i], (i * a, i * a))
    return out


def kernel(x, w_in, w_out, sgu_w, sgu_b, pool_w, pool_scale, swa_sinks, rel_bias, mix_out_gain, norm_mix, norm_ffn, w_gate_up, w_down, norm_final, loss_target, m_w_in, m_w_out, m_sgu_w, m_sgu_b, m_pool_w, m_pool_scale, m_swa_sinks, m_rel_bias, m_mix_out_gain, m_norm_mix, m_norm_ffn, m_w_gate_up, m_w_down, m_norm_final, v_w_in, v_w_out, v_sgu_w, v_sgu_b, v_pool_w, v_pool_scale, v_swa_sinks, v_rel_bias, v_mix_out_gain, v_norm_mix, v_norm_ffn, v_w_gate_up, v_w_down, v_norm_final):
    B, S, D = x.shape
    T = B * S
    L = w_in.shape[0]
    tm = min(512, T)
    F = w_down.shape[1] * N_CHIPS
    xi, yi, ci = lax.axis_index("x"), lax.axis_index("y"), lax.axis_index("c")
    cidx = jnp.reshape(ci, (1,)).astype(jnp.int32)
    kidx = jnp.reshape(2 * xi + yi, (1,)).astype(jnp.int32)

    big = [w_in, w_out, w_gate_up, w_down]
    slots = [_cast_slots(w, kidx) for w in big]
    weights = [None] * L
    weights[0] = _comm_only(_GatherComm([slots[pi][0] for pi in range(4)]), "gather_weights")

    bucket = jnp.asarray(_t5_bucket_table().reshape(1, -1))
    bias_tab = _bias_expand(rel_bias.T, bucket).reshape(4, BLK, 2 * BLK)

    row = lambda v: v.reshape(1, -1)
    xc = x.reshape(T, D)
    tgt = loss_target.reshape(T, D)
    saved = []
    Win, Wo, Wgu, Wd = ([None] * L for _ in range(4))
    for l in range(L):
        Win[l], Wgu[l] = weights[l][0], weights[l][2]
        Wo[l], Wd[l] = weights[l][1].reshape(D, D), weights[l][3].reshape(F, D)
        h1, proj = _norm_mm(xc, row(norm_mix[l]), Win[l], tm)
        bexp = jnp.repeat(sgu_b[l].T, HD, axis=1)
        wbd = _block_diag(pool_w[l])
        sk = jnp.broadcast_to(swa_sinks[l][:, None, None], (4, 1, BLK))
        ya = _sgu_fwd(proj, sgu_w[l], bexp, B, S)
        yb = _pool_fwd(proj, wbd, row(pool_scale[l]), B, S)
        yc = _swa_fwd(proj, sk, bias_tab, B, S)
        if l + 1 < L:
            yd, lt, *weights[l + 1] = _sb_fwd(proj, B, S, _GatherComm([slots[pi][l + 1] for pi in range(4)]))
        else:
            yd, lt = _sb_fwd(proj, B, S)
        ys = (ya, yb, yc, yd)
        ycn, x1 = _gnorm_mm_res(ys, row(mix_out_gain[l]), Wo[l], xc, tm)
        h2, gu, act = _norm_mm_swiglu(x1, row(norm_ffn[l]), Wgu[l], tm)
        x2 = _mm_res(act, Wd[l], x1, tm)
        saved.append((xc, h1, proj, bexp, wbd, sk, ys, lt, ycn, x1, h2, gu, act))
        xc = x2

    dx, g_final, loss_v = _final_loss(xc, row(norm_final), tgt, tm)

    tk = min(T, 2048)
    gW = [[None] * L for _ in range(4)]
    g_sgu_w, g_sgu_b, g_pool_w, g_pool_scale, g_sinks, g_bias = ([None] * L for _ in range(6))
    g_out_gain, g_mix, g_ffn = ([None] * L for _ in range(3))
    reduced = [None] * 4

    def pair_sums(l, r1):
        return zip(*[_pair_add(gW[pi][l], r1[pi], cidx, "grad_pair_add") for pi in range(4)])

    def chip_sums(l, qf, r2):
        idx = jnp.stack([2 * xi + yi, ci, jnp.int32(l)]).astype(jnp.int32)
        for pi in range(4):
            reduced[pi] = _chip_add(qf[pi], r2[pi], idx, reduced[pi], L, "grad_chip_add")

    for l in reversed(range(L)):
        x0, h1, proj, bexp, wbd, sk, ys, lt, ycn, x1, h2, gu, act = saved[l]
        above = l + 1 if l + 1 < L else None
        dgu = _dact(dx, Wd[l], gu, tm)
        gW[3][l] = _dw(act, dx, lambda t, s: (t, 0), D, 1, F // 2, tk // 2, "dw_down").reshape(N_CHIPS, F // N_CHIPS, D)
        gW[2][l] = _dw(h2, dgu, lambda t, s: (s // 2, t, s % 2), F // 2, N_CHIPS, D, tk, "dw_gate_up")
        comm = None if above is None else _PairExchangeComm([gW[pi][above] for pi in range(4)])
        dx1, g_ffn[l], *r1 = _dx_norm_bwd(dgu, Wgu[l], x1, row(norm_ffn[l]), dx, tm,
                                          "dx_ffn" if comm is None else "dx_ffn_exchange", comm)
        if above is not None:
            qf, qb = pair_sums(above, r1)
        gW[1][l] = _dw(ycn, dx1, lambda t, s: (t, 0), D, 1, D, tk, "dw_out").reshape(N_CHIPS, D // N_CHIPS, D)
        dya, dyb, dyc, dyd, g_out_gain[l] = _dycat(dx1, Wo[l], ys, row(mix_out_gain[l]), tm)
        dpa, g_sgu_w[l], dbf = _sgu_bwd(proj, sgu_w[l], bexp, dya, B, S)
        g_sgu_b[l] = dbf[:, ::HD].T
        dpb, dwbd, dsc = _pool_bwd(proj, wbd, row(pool_scale[l]), dyb, B, S)
        npg = len(POOL_WINDOWS)
        g_pool_w[l] = jnp.stack([dwbd[i * HD:(i + 1) * HD, i * HD:(i + 1) * HD] for i in range(npg)])
        g_pool_scale[l] = dsc[0]
        dcq, dckv, dsk, g_bias[l] = _swa_bwd(proj, sk, bias_tab, dyc, B, S)
        g_sinks[l] = dsk[:, 0, 0] * float(BLK)
        comm = None if above is None else _ChipExchangeComm(list(qb))
        ddq, ddk, ddv, *r2 = _sb_bwd(proj, lt, dyd, B, S, comm)
        if above is not None:
            chip_sums(above, qf, r2)
        dproj = jnp.concatenate([dpa, dpb, dcq, dckv, ddq, ddk, ddv], axis=1)
        gW[0][l] = _dw(h1, dproj, lambda t, s: (t, s), w_in.shape[2], N_CHIPS, D, tk, "dw_in")
        dx, g_mix[l] = _dx_norm_bwd(dproj, Win[l], x0, row(norm_mix[l]), dx1, tm, "dx_mix")
    grad_x = dx.reshape(B, S, D)

    r1 = _comm_only(_PairExchangeComm([gW[pi][0] for pi in range(4)]), "grad_pair_exchange")
    qf, qb = pair_sums(0, r1)
    r2 = _comm_only(_ChipExchangeComm(list(qb)), "grad_chip_exchange")
    chip_sums(0, qf, r2)
    g_big = _pair_share(reduced, [g.shape[1] // 2 for g in reduced])

    g_rel_bias = _bias_reduce([g.reshape(4, -1) for g in g_bias], bucket).T
    small_g = [jnp.stack(g_sgu_w), jnp.stack(g_sgu_b), jnp.stack(g_pool_w), jnp.stack(g_pool_scale), jnp.stack(g_sinks),
               g_rel_bias, jnp.concatenate(g_out_gain), jnp.concatenate(g_mix), jnp.concatenate(g_ffn), g_final[0]]
    small_w = [sgu_w, sgu_b, pool_w, pool_scale, swa_sinks, rel_bias, mix_out_gain, norm_mix, norm_ffn, norm_final]
    small_m = [m_sgu_w, m_sgu_b, m_pool_w, m_pool_scale, m_swa_sinks, m_rel_bias, m_mix_out_gain, m_norm_mix, m_norm_ffn, m_norm_final]
    small_v = [v_sgu_w, v_sgu_b, v_pool_w, v_pool_scale, v_swa_sinks, v_rel_bias, v_mix_out_gain, v_norm_mix, v_norm_ffn, v_norm_final]
    shapes = [w.shape for w in small_w]
    packed = _small_sum(_small_allgather(_pack(small_g + [loss_v[0, 0:1]])))
    *g_small, loss = _unpack(packed, shapes + [()])
    g_small_packed = _pack(g_small)
    ds, ms, vs = _adamw(_pack(small_w), g_small_packed, _pack(small_m), _pack(small_v), g_small_packed.shape[0], "adamw_small")
    d_small, m_small, v_small = _unpack(ds, shapes), _unpack(ms, shapes), _unpack(vs, shapes)

    big_m = [m_w_in, m_w_out, m_w_gate_up, m_w_down]
    big_v = [v_w_in, v_w_out, v_w_gate_up, v_w_down]
    d_big, m_big, v_big = [], [], []
    for w, g, m, v in zip(big, g_big, big_m, big_v):
        two = lambda a: a.reshape(-1, a.shape[-1])
        rows = two(w).shape[0]
        d2, m2, v2 = _adamw(two(w), two(g), two(m), two(v), rows // 8 if rows >= 2048 else rows, "adamw_big")
        d_big.append(d2.reshape(w.shape))
        m_big.append(m2.reshape(w.shape))
        v_big.append(v2.reshape(w.shape))

    def order(bigs, smalls):
        return [bigs[0], bigs[1]] + list(smalls[0:9]) + [bigs[2], bigs[3], smalls[9]]

    return (loss, grad_x, *order(g_big, g_small), *order(d_big, d_small), *order(m_big, m_small), *order(v_big, v_small))
```

```python
import functools

import numpy as np
import jax
import jax.numpy as jnp
from jax import lax
from jax.experimental import pallas as pl
from jax.experimental.pallas import tpu as pltpu

f32 = jnp.float32
bf16 = jnp.bfloat16
_MXU = jnp.bfloat16

EPS = 1e-6
HD = 64
GW = 256
BLK = 128
SB_UNROLL = 2
POOL_WINDOWS = (2, 4, 8, 16)
N_BUCKETS = 32
MAX_DISTANCE = 128
N_CHIPS = 4
N_DEV = 8
VMEM_BYTES_V7X = 64 * 1024 * 1024
VMEM_LIMIT = 48 * 1024 * 1024

ADAM_LR = 0.001
ADAM_B1 = 0.9
ADAM_B2 = 0.999
ADAM_EPS = 1e-08
ADAM_WD = 0.01
ADAM_STEP = 10

SDS = jax.ShapeDtypeStruct
MESH = pl.DeviceIdType.MESH
HIGHEST = lax.Precision.HIGHEST
RESIDENT = pl.Buffered(1)
NT = (((1,), (1,)), ((), ()))
TN = (((0,), (0,)), ((), ()))


def _cp(*sem):
    return pltpu.CompilerParams(dimension_semantics=sem if sem else None, vmem_limit_bytes=VMEM_LIMIT)


def _mx(v):
    return v.astype(_MXU)


def _iota(shape, dim):
    return lax.broadcasted_iota(jnp.int32, shape, dim)


def _split_dot(a, tri):
    hi = a.astype(bf16)
    lo = (a - hi.astype(f32)).astype(bf16)
    return jnp.dot(hi, tri, preferred_element_type=f32) + jnp.dot(lo, tri, preferred_element_type=f32)


def _rms(xv):
    return lax.rsqrt(jnp.mean(xv * xv, axis=-1, keepdims=True) + EPS)


def _norm_mm(x, gain, w, tm):
    T, D = x.shape
    NS, _, ns = w.shape

    def body(x_ref, g_ref, w_ref, h_ref, o_ref):
        xv = x_ref[...]
        h = (xv * _rms(xv) * g_ref[...]).astype(bf16)
        h_ref[...] = h
        for s in range(NS):
            o_ref[:, s * ns:(s + 1) * ns] = jnp.dot(_mx(h), w_ref[s], preferred_element_type=f32)

    return pl.pallas_call(
        body, grid=(T // tm,),
        in_specs=[pl.BlockSpec((tm, D), lambda i: (i, 0)),
                  pl.BlockSpec((1, D), lambda i: (0, 0)),
                  pl.BlockSpec((NS, D, ns), lambda i: (0, 0, 0), pipeline_mode=RESIDENT)],
        out_specs=[pl.BlockSpec((tm, D), lambda i: (i, 0)),
                   pl.BlockSpec((tm, NS * ns), lambda i: (i, 0))],
        out_shape=[SDS((T, D), bf16), SDS((T, NS * ns), f32)],
        name="norm_mm_in", compiler_params=_cp("parallel"))(x, gain, w)


def _norm_mm_swiglu(x, gain, w, tm, comm=None):
    T, D = x.shape
    NS, _, ns = w.shape
    half = NS // 2

    def body(x_ref, g_ref, w_ref, h_ref, gu_ref, a_ref):
        xv = x_ref[...]
        hb = (xv * _rms(xv) * g_ref[...]).astype(bf16)
        h_ref[...] = hb
        h = _mx(hb)
        for s in range(half):
            cols = slice(s * ns, (s + 1) * ns)
            g = jnp.dot(h, w_ref[s], preferred_element_type=f32)
            u = jnp.dot(h, w_ref[s + half], preferred_element_type=f32)
            gu_ref[0, :, cols] = g.astype(bf16)
            gu_ref[1, :, cols] = u.astype(bf16)
            a_ref[:, cols] = (jax.nn.silu(g) * u).astype(bf16)

    c_args, c_in, c_out, c_shapes, aliases, c_scratch = _host_specs(comm, 3, 3)
    step = lambda v: (lambda: pl.program_id(0) == v)
    return pl.pallas_call(
        _host(body, 3, 3, 0, comm, step(0), step(T // tm - 1)), grid=(T // tm,),
        in_specs=[pl.BlockSpec((tm, D), lambda i: (i, 0)),
                  pl.BlockSpec((1, D), lambda i: (0, 0)),
                  pl.BlockSpec((NS, D, ns), lambda i: (0, 0, 0), pipeline_mode=RESIDENT)] + c_in,
        out_specs=[pl.BlockSpec((tm, D), lambda i: (i, 0)),
                   pl.BlockSpec((2, tm, half * ns), lambda i: (0, i, 0)),
                   pl.BlockSpec((tm, half * ns), lambda i: (i, 0))] + c_out,
        out_shape=[SDS((T, D), bf16), SDS((2, T, half * ns), bf16), SDS((T, half * ns), bf16)] + c_shapes,
        input_output_aliases=aliases, scratch_shapes=c_scratch,
        name="norm_mm_swiglu" if comm is None else "norm_mm_swiglu_gather",
        compiler_params=_cp("arbitrary"))(x, gain, w, *c_args)


def _gnorm_mm_res(ys, gain, w, x, tm):
    T, D = x.shape

    def body(ya, yb, yc, yd, g_ref, w_ref, x_ref, yn_ref, o_ref):
        parts = []
        for m, r in enumerate((ya, yb, yc, yd)):
            y = r[...]
            parts.append((y * _rms(y) * g_ref[:, m * GW:(m + 1) * GW]).astype(bf16))
        yn = jnp.concatenate(parts, axis=1)
        yn_ref[...] = yn
        o_ref[...] = x_ref[...] + jnp.dot(_mx(yn), w_ref[...], preferred_element_type=f32)

    yspec = pl.BlockSpec((tm, GW), lambda i: (i, 0))
    return pl.pallas_call(
        body, grid=(T // tm,),
        in_specs=[yspec, yspec, yspec, yspec,
                  pl.BlockSpec((1, D), lambda i: (0, 0)),
                  pl.BlockSpec((D, D), lambda i: (0, 0)),
                  pl.BlockSpec((tm, D), lambda i: (i, 0))],
        out_specs=[pl.BlockSpec((tm, D), lambda i: (i, 0)), pl.BlockSpec((tm, D), lambda i: (i, 0))],
        out_shape=[SDS((T, D), bf16), SDS((T, D), f32)],
        name="gnorm_mm_res", compiler_params=_cp("parallel"))(*ys, gain, w, x)


def _mm_res(a, w, x, tm):
    T, D = x.shape
    K = a.shape[1]

    def body(a_ref, w_ref, x_ref, o_ref):
        o_ref[...] = x_ref[...] + jnp.dot(_mx(a_ref[...]), w_ref[...], preferred_element_type=f32)

    return pl.pallas_call(
        body, grid=(T // tm,),
        in_specs=[pl.BlockSpec((tm, K), lambda i: (i, 0)),
                  pl.BlockSpec((K, D), lambda i: (0, 0)),
                  pl.BlockSpec((tm, D), lambda i: (i, 0))],
        out_specs=pl.BlockSpec((tm, D), lambda i: (i, 0)),
        out_shape=SDS((T, D), f32),
        name="mm_res_down", compiler_params=_cp("parallel"))(a, w, x)


def _final_loss(x, gain, tgt, tm):
    T, D = x.shape

    def body(x_ref, g_ref, t_ref, dx_ref, dg_ref, l_ref):
        @pl.when(pl.program_id(0) == 0)
        def _():
            dg_ref[...] = jnp.zeros_like(dg_ref)
            l_ref[...] = jnp.zeros_like(l_ref)
        xv = x_ref[...]
        g = g_ref[...]
        r = _rms(xv)
        xh = xv * r
        err = xh * g - t_ref[...]
        l_ref[...] += 0.5 * jnp.sum(jnp.mean(err * err, axis=-1, keepdims=True), axis=0, keepdims=True)
        dy = err * (1.0 / D)
        dg_ref[...] += jnp.sum(dy * xh, axis=0, keepdims=True)
        dxh = dy * g
        dx_ref[...] = r * (dxh - xh * jnp.mean(dxh * xh, axis=-1, keepdims=True))

    return pl.pallas_call(
        body, grid=(T // tm,),
        in_specs=[pl.BlockSpec((tm, D), lambda i: (i, 0)),
                  pl.BlockSpec((1, D), lambda i: (0, 0)),
                  pl.BlockSpec((tm, D), lambda i: (i, 0))],
        out_specs=[pl.BlockSpec((tm, D), lambda i: (i, 0)),
                   pl.BlockSpec((1, D), lambda i: (0, 0)),
                   pl.BlockSpec((1, BLK), lambda i: (0, 0))],
        out_shape=[SDS((T, D), f32), SDS((1, D), f32), SDS((1, BLK), f32)],
        name="final_loss", compiler_params=_cp("arbitrary"))(x, gain, tgt)


def _dact(dx, wd, gu, tm):
    T, D = dx.shape
    F = wd.shape[0]
    ns = F // 2

    def body(dx_ref, w_ref, gu_ref, o_ref):
        dxb = _mx(dx_ref[...])
        for s in range(2):
            cols = slice(s * ns, (s + 1) * ns)
            da = lax.dot_general(dxb, w_ref[s * ns:(s + 1) * ns, :], NT, preferred_element_type=f32)
            g = gu_ref[0, :, cols].astype(f32)
            u = gu_ref[1, :, cols].astype(f32)
            sg = jax.nn.sigmoid(g)
            o_ref[0, :, cols] = (da * u * (sg * (1.0 + g * (1.0 - sg)))).astype(bf16)
            o_ref[1, :, cols] = (da * (g * sg)).astype(bf16)

    return pl.pallas_call(
        body, grid=(T // tm,),
        in_specs=[pl.BlockSpec((tm, D), lambda i: (i, 0)),
                  pl.BlockSpec((F, D), lambda i: (0, 0), pipeline_mode=RESIDENT),
                  pl.BlockSpec((2, tm, F), lambda i: (0, i, 0))],
        out_specs=pl.BlockSpec((2, tm, F), lambda i: (0, i, 0)),
        out_shape=SDS((2, T, F), bf16),
        name="dact", compiler_params=_cp("parallel"))(dx, wd, gu)


def _dw(a, b, b_map, ns, NS, tka, tk, name):
    T, Ka = a.shape
    b_block = (tk, ns) if b.ndim == 2 else (1, tk, ns)

    def body(a_ref, b_ref, o_ref):
        bv = b_ref[...] if b.ndim == 2 else b_ref[0]
        part = lax.dot_general(_mx(a_ref[...]), _mx(bv), TN, preferred_element_type=f32)

        @pl.when(pl.program_id(2) == 0)
        def _():
            o_ref[0] = part

        @pl.when(pl.program_id(2) > 0)
        def _():
            o_ref[0] += part

    return pl.pallas_call(
        body, grid=(NS, Ka // tka, T // tk),
        in_specs=[pl.BlockSpec((tk, tka), lambda s, k, t: (t, k)),
                  pl.BlockSpec(b_block, lambda s, k, t: b_map(t, s))],
        out_specs=pl.BlockSpec((1, tka, ns), lambda s, k, t: (s, k, 0)),
        out_shape=SDS((NS, Ka, ns), f32),
        name=name, compiler_params=_cp("parallel", "parallel", "arbitrary"))(a, b)


def _dx_norm_bwd(dy, w, x, gain, dxin, tm, name, comm=None):
    T, D = x.shape
    NS, _, ns = w.shape
    half = NS // 2

    def body(dy_ref, w_ref, x_ref, g_ref, dxin_ref, dx_ref, dg_ref):
        @pl.when(pl.program_id(0) == 0)
        def _():
            dg_ref[...] = jnp.zeros_like(dg_ref)
        dh = None
        for s in range(NS):
            if dy.ndim == 2:
                dv = dy_ref[:, s * ns:(s + 1) * ns]
            else:
                dv = dy_ref[s // half, :, (s % half) * ns:(s % half + 1) * ns]
            part = lax.dot_general(_mx(dv), w_ref[s], NT, preferred_element_type=f32)
            dh = part if dh is None else dh + part
        xv = x_ref[...]
        r = _rms(xv)
        xh = xv * r
        dg_ref[...] += jnp.sum(dh * xh, axis=0, keepdims=True)
        dxh = dh * g_ref[...]
        dx_ref[...] = dxin_ref[...] + r * (dxh - xh * jnp.mean(dxh * xh, axis=-1, keepdims=True))

    dy_spec = (pl.BlockSpec((tm, NS * ns), lambda i: (i, 0)) if dy.ndim == 2
               else pl.BlockSpec((2, tm, half * ns), lambda i: (0, i, 0)))
    c_args, c_in, c_out, c_shapes, aliases, c_scratch = _host_specs(comm, 5, 2)
    step = lambda v: (lambda: pl.program_id(0) == v)
    return pl.pallas_call(
        _host(body, 5, 2, 0, comm, step(0), step(T // tm - 1)), grid=(T // tm,),
        in_specs=[dy_spec,
                  pl.BlockSpec((NS, D, ns), lambda i: (0, 0, 0), pipeline_mode=RESIDENT),
                  pl.BlockSpec((tm, D), lambda i: (i, 0)),
                  pl.BlockSpec((1, D), lambda i: (0, 0)),
                  pl.BlockSpec((tm, D), lambda i: (i, 0))] + c_in,
        out_specs=[pl.BlockSpec((tm, D), lambda i: (i, 0)),
                   pl.BlockSpec((1, D), lambda i: (0, 0))] + c_out,
        out_shape=[SDS((T, D), f32), SDS((1, D), f32)] + c_shapes,
        input_output_aliases=aliases, scratch_shapes=c_scratch,
        name=name, compiler_params=_cp("arbitrary"))(dy, w, x, gain, dxin, *c_args)


def _dycat(dx, w, ys, gain, tm):
    T, D = dx.shape

    def body(dx_ref, w_ref, ya, yb, yc, yd, g_ref, da, db, dc, dd, dg_ref):
        @pl.when(pl.program_id(0) == 0)
        def _():
            dg_ref[...] = jnp.zeros_like(dg_ref)
        dyn = lax.dot_general(_mx(dx_ref[...]), w_ref[...], NT, preferred_element_type=f32)
        for m, (r, o) in enumerate(((ya, da), (yb, db), (yc, dc), (yd, dd))):
            cols = slice(m * GW, (m + 1) * GW)
            y = r[...]
            rs = _rms(y)
            yh = y * rs
            d = dyn[:, cols]
            dg_ref[:, cols] += jnp.sum(d * yh, axis=0, keepdims=True)
            dyh = d * g_ref[:, cols]
            o[...] = rs * (dyh - yh * jnp.mean(dyh * yh, axis=-1, keepdims=True))

    yspec = pl.BlockSpec((tm, GW), lambda i: (i, 0))
    return pl.pallas_call(
        body, grid=(T // tm,),
        in_specs=[pl.BlockSpec((tm, D), lambda i: (i, 0)),
                  pl.BlockSpec((D, D), lambda i: (0, 0)),
                  yspec, yspec, yspec, yspec,
                  pl.BlockSpec((1, D), lambda i: (0, 0))],
        out_specs=[yspec, yspec, yspec, yspec, pl.BlockSpec((1, D), lambda i: (0, 0))],
        out_shape=[SDS((T, GW), f32)] * 4 + [SDS((1, D), f32)],
        name="dycat", compiler_params=_cp("arbitrary"))(dx, w, *ys, gain)


def _sgu_consts():
    r, c = _iota((GW, GW), 0), _iota((GW, GW), 1)
    seg = (r // HD == c // HD).astype(f32)
    tr, ts = _iota((BLK, BLK), 0), _iota((BLK, BLK), 1)
    causal = ts <= tr
    lane_head = _iota((BLK, GW), 1) // HD
    return seg, causal, lane_head


def _split3_dot(a, ones):
    hi = a.astype(bf16)
    r1 = a - hi.astype(f32)
    mid = r1.astype(bf16)
    lo = (r1 - mid.astype(f32)).astype(bf16)
    dot = functools.partial(jnp.dot, preferred_element_type=f32)
    return dot(hi, ones) + dot(mid, ones) + dot(lo, ones)


def _sgu_chunks(aus, avs, w, bexp, consts):
    seg, causal, lane_head = consts
    segb = seg.astype(bf16)
    nh = GW // HD
    vs = [jax.nn.gelu(av) for av in avs]
    mus = [_split3_dot(v, segb) * (1.0 / HD) for v in vs]
    vcs = [v - mu for v, mu in zip(vs, mus)]
    vars_ = [_split3_dot(vc * vc, segb) * (1.0 / HD) for vc in vcs]
    vns = [_mx(vc * lax.rsqrt(var + EPS)) for vc, var in zip(vcs, vars_)]
    whs = [_mx(jnp.where(causal, w[h], 0.0)) for h in range(nh)]
    mixes = [[jnp.dot(whs[h], vn, preferred_element_type=f32) for h in range(nh)] for vn in vns]
    out = []
    for au, ms in zip(aus, mixes):
        mix = bexp
        for h in range(nh):
            mix = mix + jnp.where(lane_head == h, ms[h], 0.0)
        out.append(jax.nn.gelu(au) * mix)
    return out


def _sgu_group(S):
    nc = S // BLK
    return 4 if nc % 4 == 0 else (2 if nc % 2 == 0 else 1)


def _sgu_fwd(proj, w, bexp, B, S):
    G = _sgu_group(S)

    def body(au_ref, av_ref, w_ref, b_ref, y_ref):
        consts = _sgu_consts()
        wv, bv = w_ref[...], b_ref[...]

        def group(n, c):
            rows = [pl.ds(pl.multiple_of((n * G + j) * BLK, BLK), BLK) for j in range(G)]
            ys = _sgu_chunks([au_ref[r, :] for r in rows], [av_ref[r, :] for r in rows], wv, bv, consts)
            for r, y in zip(rows, ys):
                y_ref[r, :] = y
            return c
        lax.fori_loop(0, S // BLK // G, group, 0)

    return pl.pallas_call(
        body, grid=(B,),
        in_specs=[pl.BlockSpec((S, GW), lambda b: (b, 0)),
                  pl.BlockSpec((S, GW), lambda b: (b, 1)),
                  pl.BlockSpec((GW // HD, BLK, BLK), lambda b: (0, 0, 0)),
                  pl.BlockSpec((BLK, GW), lambda b: (0, 0))],
        out_specs=pl.BlockSpec((S, GW), lambda b: (b, 0)),
        out_shape=SDS((B * S, GW), f32),
        name="sgu_fwd", compiler_params=_cp("parallel"))(proj, proj, w, bexp)


def _sgu_bwd(proj, w, bexp, dy, B, S):
    def body(au_ref, av_ref, w_ref, b_ref, dy_ref, dp_ref, dw_ref, db_ref):
        @pl.when(pl.program_id(0) == 0)
        def _():
            dw_ref[...] = jnp.zeros_like(dw_ref)
            db_ref[...] = jnp.zeros_like(db_ref)
        consts = _sgu_consts()
        wv, bv = w_ref[...], b_ref[...]
        fn = lambda aus, avs, ww, bb: _sgu_chunks(aus, avs, ww, bb, consts)
        G = _sgu_group(S)

        def group(n, carry):
            dw_acc, db_acc = carry
            rows = [pl.ds(pl.multiple_of((n * G + j) * BLK, BLK), BLK) for j in range(G)]
            _, vjp = jax.vjp(fn, [au_ref[r, :] for r in rows], [av_ref[r, :] for r in rows], wv, bv)
            daus, davs, dwc, dbc = vjp([dy_ref[r, :] for r in rows])
            for r, dau, dav in zip(rows, daus, davs):
                dp_ref[r, 0:GW] = dau.astype(bf16)
                dp_ref[r, GW:2 * GW] = dav.astype(bf16)
            return dw_acc + dwc, db_acc + dbc
        dw_acc, db_acc = lax.fori_loop(0, S // BLK // G, group, (jnp.zeros(wv.shape, f32), jnp.zeros(bv.shape, f32)))
        dw_ref[...] += dw_acc
        db_ref[...] += jnp.dot(db_acc, consts[0], precision=HIGHEST, preferred_element_type=f32)

    return pl.pallas_call(
        body, grid=(B,),
        in_specs=[pl.BlockSpec((S, GW), lambda b: (b, 0)),
                  pl.BlockSpec((S, GW), lambda b: (b, 1)),
                  pl.BlockSpec((GW // HD, BLK, BLK), lambda b: (0, 0, 0)),
                  pl.BlockSpec((BLK, GW), lambda b: (0, 0)),
                  pl.BlockSpec((S, GW), lambda b: (b, 0))],
        out_specs=[pl.BlockSpec((S, 2 * GW), lambda b: (b, 0)),
                   pl.BlockSpec((GW // HD, BLK, BLK), lambda b: (0, 0, 0)),
                   pl.BlockSpec((BLK, GW), lambda b: (0, 0))],
        out_shape=[SDS((B * S, 2 * GW), bf16), SDS((GW // HD, BLK, BLK), f32), SDS((BLK, GW), f32)],
        name="sgu_bwd", compiler_params=_cp("arbitrary"))(proj, proj, w, bexp, dy)


def _pool_parts(p):
    n = p.shape[0]
    r = _iota(p.shape, 0)
    lg = _iota(p.shape, 1) // HD

    def sh(v, k):
        return jnp.where(r >= k, pltpu.roll(v, k, 0), 0.0)
    s2 = p + sh(p, 1)
    s4 = s2 + sh(s2, 2)
    s8 = s4 + sh(s4, 4)
    s16 = s8 + sh(s8, 8)
    ws = jnp.where(lg == 0, s2, jnp.where(lg == 1, s4, jnp.where(lg == 2, s8, s16)))
    wlen = jnp.where(lg == 0, 2, jnp.where(lg == 1, 4, jnp.where(lg == 2, 8, 16)))
    cnt = jnp.minimum(r + 1, wlen).astype(f32)
    del n
    return ws / cnt - p, cnt, lg


def _pool_fwd(proj, wbd, scale, B, S):
    def body(p_ref, w_ref, s_ref, y_ref):
        y, _, _ = _pool_parts(p_ref[...])
        y_ref[...] = jnp.dot(_mx(y), _mx(w_ref[...]), preferred_element_type=f32) * s_ref[...]

    return pl.pallas_call(
        body, grid=(B,),
        in_specs=[pl.BlockSpec((S, GW), lambda b: (b, 2)),
                  pl.BlockSpec((GW, GW), lambda b: (0, 0)),
                  pl.BlockSpec((1, GW), lambda b: (0, 0))],
        out_specs=pl.BlockSpec((S, GW), lambda b: (b, 0)),
        out_shape=SDS((B * S, GW), f32),
        name="pool_fwd", compiler_params=_cp("parallel"))(proj, wbd, scale)


def _pool_bwd(proj, wbd, scale, dy, B, S):
    def body(p_ref, w_ref, s_ref, dy_ref, dp_ref, dw_ref, ds_ref):
        @pl.when(pl.program_id(0) == 0)
        def _():
            dw_ref[...] = jnp.zeros_like(dw_ref)
            ds_ref[...] = jnp.zeros_like(ds_ref)
        y, cnt, lg = _pool_parts(p_ref[...])
        wv = _mx(w_ref[...])
        z = jnp.dot(_mx(y), wv, preferred_element_type=f32)
        dout = dy_ref[...]
        ds_ref[...] += jnp.sum(dout * z, axis=0, keepdims=True)
        dz = _mx(dout * s_ref[...])
        dw_ref[...] += lax.dot_general(_mx(y), dz, TN, preferred_element_type=f32)
        dyv = lax.dot_general(dz, wv, NT, preferred_element_type=f32)
        n = dyv.shape[0]
        r = _iota(dyv.shape, 0)

        def ush(v, k):
            return jnp.where(r < n - k, pltpu.roll(v, n - k, 0), 0.0)
        gq = dyv / cnt
        a2 = gq + ush(gq, 1)
        a4 = a2 + ush(a2, 2)
        a8 = a4 + ush(a4, 4)
        a16 = a8 + ush(a8, 8)
        adj = jnp.where(lg == 0, a2, jnp.where(lg == 1, a4, jnp.where(lg == 2, a8, a16)))
        dp_ref[...] = (adj - dyv).astype(bf16)

    return pl.pallas_call(
        body, grid=(B,),
        in_specs=[pl.BlockSpec((S, GW), lambda b: (b, 2)),
                  pl.BlockSpec((GW, GW), lambda b: (0, 0)),
                  pl.BlockSpec((1, GW), lambda b: (0, 0)),
                  pl.BlockSpec((S, GW), lambda b: (b, 0))],
        out_specs=[pl.BlockSpec((S, GW), lambda b: (b, 0)),
                   pl.BlockSpec((GW, GW), lambda b: (0, 0)),
                   pl.BlockSpec((1, GW), lambda b: (0, 0))],
        out_shape=[SDS((B * S, GW), bf16), SDS((GW, GW), f32), SDS((1, GW), f32)],
        name="pool_bwd", compiler_params=_cp("arbitrary"))(proj, wbd, scale, dy)


def _t5_bucket_table():
    dist = (np.arange(BLK)[:, None] + BLK) - np.arange(2 * BLK)[None, :]
    d = np.clip(dist, 0, BLK - 1)
    max_exact = N_BUCKETS // 2
    df = np.maximum(d, 1).astype(np.float32)
    large = max_exact + (np.log(df / max_exact) / np.float32(np.log(MAX_DISTANCE / max_exact))
                         * (N_BUCKETS - max_exact)).astype(np.int32)
    large = np.minimum(large, N_BUCKETS - 1)
    return np.where(d < max_exact, d, large).astype(np.int32)


def _swa_block(qb0, qb1, k2, v2, sinks, biases, n):
    heads = [(p, g) for p in range(2) for g in range(2)]
    ri, ci = _iota((BLK, BLK), 0), _iota((BLK, BLK), 1)
    qi, ki = _iota((BLK, 2 * BLK), 0), _iota((BLK, 2 * BLK), 1)
    dist = qi + BLK - ki
    mask = (dist >= 0) & (dist < BLK) & ((ki >= BLK) | (n > 0))
    qbs, kb, vb = (_mx(qb0), _mx(qb1)), _mx(k2), _mx(v2)
    qs, vs = [], []
    for p, g in heads:
        selq = ((ri - g * HD == ci - p * HD) & (ri >= g * HD) & (ri < (g + 1) * HD)).astype(_MXU)
        selv = ((ci - g * HD == ri - p * HD) & (ci >= g * HD) & (ci < (g + 1) * HD)).astype(_MXU)
        qs.append(_mx(jnp.dot(qbs[p], selq, preferred_element_type=f32)))
        vs.append(_mx(jnp.dot(vb, selv, preferred_element_type=f32)))
    zs = [lax.dot_general(q, kb, NT, preferred_element_type=f32) * (HD ** -0.5) for q in qs]
    prs = []
    for h in range(4):
        z = jnp.where(mask, zs[h] + biases[h], -1e30)
        s = jnp.mean(sinks[h], axis=-1, keepdims=True)
        m = jnp.maximum(jnp.max(z, axis=-1, keepdims=True), s)
        e = jnp.exp(z - m)
        prs.append(_mx(e / (jnp.sum(e, axis=-1, keepdims=True) + jnp.exp(s - m))))
    outs = [jnp.dot(prs[h], vs[h], preferred_element_type=f32) for h in range(4)]
    return outs[0] + outs[1], outs[2] + outs[3]


def _swa_fwd(proj, sinks, bias, B, S):
    def body(q_ref, kv_ref, s_ref, b_ref, y_ref):
        def block(n, c):
            rows = pl.ds(pl.multiple_of(n * BLK, BLK), BLK)
            prev = pl.ds(pl.multiple_of(jnp.maximum(n - 1, 0) * BLK, BLK), BLK)
            k2 = jnp.concatenate([kv_ref[prev, 0:BLK], kv_ref[rows, 0:BLK]], axis=0)
            v2 = jnp.concatenate([kv_ref[prev, BLK:2 * BLK], kv_ref[rows, BLK:2 * BLK]], axis=0)
            o0, o1 = _swa_block(q_ref[rows, 0:BLK], q_ref[rows, BLK:2 * BLK], k2, v2,
                                [s_ref[h] for h in range(4)], [b_ref[h] for h in range(4)], n)
            y_ref[rows, 0:BLK] = o0
            y_ref[rows, BLK:2 * BLK] = o1
            return c
        lax.fori_loop(0, S // BLK, block, 0)

    return pl.pallas_call(
        body, grid=(B,),
        in_specs=[pl.BlockSpec((S, GW), lambda b: (b, 3)),
                  pl.BlockSpec((S, GW), lambda b: (b, 4)),
                  pl.BlockSpec((4, 1, BLK), lambda b: (0, 0, 0)),
                  pl.BlockSpec((4, BLK, 2 * BLK), lambda b: (0, 0, 0))],
        out_specs=pl.BlockSpec((S, GW), lambda b: (b, 0)),
        out_shape=SDS((B * S, GW), f32),
        name="swa_fwd", compiler_params=_cp("parallel"))(proj, proj, sinks, bias)


def _swa_bwd(proj, sinks, bias, dy, B, S, comm=None):
    def body(q_ref, kv_ref, s_ref, b_ref, dy_ref, dq_ref, dkv_ref, ds_ref, db_ref, acc_ref):
        @pl.when(pl.program_id(0) == 0)
        def _():
            ds_ref[...] = jnp.zeros_like(ds_ref)
            db_ref[...] = jnp.zeros_like(db_ref)
        acc_ref[...] = jnp.zeros_like(acc_ref)

        def block(n, c):
            rows = pl.ds(pl.multiple_of(n * BLK, BLK), BLK)
            prev = pl.ds(pl.multiple_of(jnp.maximum(n - 1, 0) * BLK, BLK), BLK)
            k2 = jnp.concatenate([kv_ref[prev, 0:BLK], kv_ref[rows, 0:BLK]], axis=0)
            v2 = jnp.concatenate([kv_ref[prev, BLK:2 * BLK], kv_ref[rows, BLK:2 * BLK]], axis=0)
            fn = functools.partial(_swa_block, n=n)
            _, vjp = jax.vjp(fn, q_ref[rows, 0:BLK], q_ref[rows, BLK:2 * BLK], k2, v2,
                             [s_ref[h] for h in range(4)], [b_ref[h] for h in range(4)])
            dq0, dq1, dk2, dv2, dss, dbs = vjp((dy_ref[rows, 0:BLK], dy_ref[rows, BLK:2 * BLK]))
            dq_ref[rows, 0:BLK] = dq0.astype(bf16)
            dq_ref[rows, BLK:2 * BLK] = dq1.astype(bf16)
            for h in range(4):
                ds_ref[h] += dss[h]
                db_ref[h] += dbs[h]
            acc_ref[prev, 0:BLK] += dk2[0:BLK]
            acc_ref[rows, 0:BLK] += dk2[BLK:2 * BLK]
            acc_ref[prev, BLK:2 * BLK] += dv2[0:BLK]
            acc_ref[rows, BLK:2 * BLK] += dv2[BLK:2 * BLK]
            return c
        lax.fori_loop(0, S // BLK, block, 0)
        dkv_ref[...] = acc_ref[...].astype(bf16)

    c_args, c_in, c_out, c_shapes, aliases, c_scratch = _host_specs(comm, 5, 4)
    step = lambda v: (lambda: pl.program_id(0) == v)
    return pl.pallas_call(
        _host(body, 5, 4, 1, comm, step(0), step(B - 1)), grid=(B,),
        in_specs=[pl.BlockSpec((S, GW), lambda b: (b, 3)),
                  pl.BlockSpec((S, GW), lambda b: (b, 4)),
                  pl.BlockSpec((4, 1, BLK), lambda b: (0, 0, 0)),
                  pl.BlockSpec((4, BLK, 2 * BLK), lambda b: (0, 0, 0)),
                  pl.BlockSpec((S, GW), lambda b: (b, 0))] + c_in,
        out_specs=[pl.BlockSpec((S, GW), lambda b: (b, 0)),
                   pl.BlockSpec((S, GW), lambda b: (b, 0)),
                   pl.BlockSpec((4, 1, BLK), lambda b: (0, 0, 0)),
                   pl.BlockSpec((4, BLK, 2 * BLK), lambda b: (0, 0, 0))] + c_out,
        out_shape=[SDS((B * S, GW), bf16), SDS((B * S, GW), bf16), SDS((4, 1, BLK), f32),
                   SDS((4, BLK, 2 * BLK), f32)] + c_shapes,
        input_output_aliases=aliases, scratch_shapes=[pltpu.VMEM((S, GW), f32)] + c_scratch,
        name="swa_bwd" if comm is None else "swa_bwd_exchange",
        compiler_params=_cp("arbitrary"))(proj, proj, sinks, bias, dy, *c_args)


def _log1m_parts(z):
    t = jnp.exp(-jnp.abs(z))
    return jnp.minimum(-z, 0.0) - jnp.log(1.0 + t), t


def _log1m(z):
    return _log1m_parts(z)[0]


def _sigmoid_from(z, t):
    return jnp.where(z >= 0.0, 1.0, t) / (1.0 + t)


def _sb_consts(tri):
    r2, c2 = _iota((2 * BLK, 2 * BLK), 0), _iota((2 * BLK, 2 * BLK), 1)
    tri2 = (tri(r2, c2) & (r2 // BLK == c2 // BLK)).astype(bf16)
    ri, ci = _iota((BLK, 2 * BLK), 0), _iota((BLK, 2 * BLK), 1)
    strict2 = (ci % BLK) < ri
    head0 = _iota((BLK, BLK), 1) < HD
    return tri2, strict2, head0


def _sb_stack_kv(k_ref, v_ref, kst_ref, vst_ref, head0, nb):
    def one(kb, c):
        krows = pl.ds(pl.multiple_of(kb * BLK, BLK), BLK)
        for p in range(2):
            for src, dst in ((k_ref, kst_ref), (v_ref, vst_ref)):
                t = src[krows, p * BLK:(p + 1) * BLK]
                dst[p, kb] = _mx(jnp.concatenate([jnp.where(head0, t, 0.0), jnp.where(head0, 0.0, t)], axis=0))
        return c
    lax.fori_loop(0, nb, one, 0)


def _sb_load_kv(kst_ref, vst_ref, kb):
    return [kst_ref[p, kb] for p in range(2)], [vst_ref[p, kb] for p in range(2)]


def _two_halves(a, b):
    return jnp.concatenate([jnp.broadcast_to(a, (BLK, BLK)), jnp.broadcast_to(b, (BLK, BLK))], axis=1)


def _half_sums(t):
    return jnp.sum(t[:, :BLK], axis=-1, keepdims=True), jnp.sum(t[:, BLK:], axis=-1, keepdims=True)


def _sb_fwd(proj, B, S, comm=None):
    def body(q_ref, k_ref, v_ref, y_ref, lt_ref, kst_ref, vst_ref):
        ci = _iota((BLK, BLK), 1)
        above2, strict2, head0 = _sb_consts(lambda r, c: r > c)
        _sb_stack_kv(k_ref, v_ref, kst_ref, vst_ref, head0, S // BLK)

        def step(qs, kbs, carry, diag):
            ok = [None if diag else kb >= 0 for kb in kbs]
            kv = [_sb_load_kv(kst_ref, vst_ref, jnp.maximum(kb, 0)) for kb in kbs]
            zs = [[lax.dot_general(qs[p], kks[p], NT, preferred_element_type=f32) for p in range(2)] for kks, _ in kv]
            Ls = [[_log1m(z) for z in zu] for zu in zs]
            if diag:
                Ls = [[jnp.where(strict2, L, 0.0) for L in Lu] for Lu in Ls]
            tails = [[_split_dot(L, above2) for L in Lu] for Lu in Ls]
            carry = list(carry)
            for u in range(len(kbs)):
                for p in range(2):
                    R0, R1, acc = carry[3 * p:3 * p + 3]
                    w = jnp.exp(zs[u][p] + Ls[u][p] + tails[u][p] + _two_halves(R0, R1))
                    s0, s1 = _half_sums(Ls[u][p])
                    if diag:
                        w = jnp.where(strict2, w, 0.0)
                    else:
                        w, s0, s1 = (jnp.where(ok[u], t, 0.0) for t in (w, s0, s1))
                    acc = acc + jnp.dot(_mx(w), kv[u][1][p], preferred_element_type=f32)
                    carry[3 * p:3 * p + 3] = [R0 + s0, R1 + s1, acc]
            return tuple(carry)

        def qblock(n, c):
            qrows = pl.ds(pl.multiple_of(n * BLK, BLK), BLK)
            qs = [_mx(q_ref[qrows, p * BLK:(p + 1) * BLK] * (HD ** -0.5)) for p in range(2)]
            z1, z2 = jnp.zeros((BLK, 1), f32), jnp.zeros((BLK, BLK), f32)
            carry = step(qs, [n], (z1, z1, z2, z1, z1, z2), True)
            res = lax.fori_loop(0, (n + SB_UNROLL - 1) // SB_UNROLL,
                                lambda i, cr: step(qs, [n - 1 - SB_UNROLL * i - u for u in range(SB_UNROLL)], cr, False), carry)
            lt = jnp.zeros((BLK, BLK), f32)
            for p in range(2):
                y_ref[qrows, p * BLK:(p + 1) * BLK] = res[3 * p + 2]
                lt = lt + jnp.where(ci == 2 * p, res[3 * p], 0.0) + jnp.where(ci == 2 * p + 1, res[3 * p + 1], 0.0)
            lt_ref[qrows, :] = lt
            return c
        lax.fori_loop(0, S // BLK, qblock, 0)

    spec = lambda j: pl.BlockSpec((S, GW), lambda b: (b, j))
    c_args, c_in, c_out, c_shapes, aliases, c_scratch = _host_specs(comm, 3, 2)
    step = lambda v: (lambda: pl.program_id(0) == v)
    stacked = pltpu.VMEM((2, S // BLK, 2 * BLK, BLK), _MXU)
    return pl.pallas_call(
        _host(body, 3, 2, 2, comm, step(0), step(B - 1)), grid=(B,),
        in_specs=[spec(5), spec(6), spec(7)] + c_in,
        out_specs=[pl.BlockSpec((S, GW), lambda b: (b, 0)), pl.BlockSpec((S, BLK), lambda b: (b, 0))] + c_out,
        out_shape=[SDS((B * S, GW), f32), SDS((B * S, BLK), f32)] + c_shapes,
        input_output_aliases=aliases, scratch_shapes=[stacked, stacked] + c_scratch,
        name="sb_fwd" if comm is None else "sb_fwd_gather",
        compiler_params=_cp("arbitrary"))(proj, proj, proj, *c_args)


def _sb_bwd(proj, ltot, dy, B, S, comm=None):
    def body(q_ref, k_ref, v_ref, lt_ref, dy_ref, dq_ref, dk_ref, dv_ref, dka_ref, dva_ref, kst_ref, vst_ref):
        ci = _iota((BLK, BLK), 1)
        upto2, strict2, head0 = _sb_consts(lambda r, c: r <= c)
        below2, _, _ = _sb_consts(lambda r, c: r < c)
        dka_ref[...] = jnp.zeros_like(dka_ref)
        dva_ref[...] = jnp.zeros_like(dva_ref)
        _sb_stack_kv(k_ref, v_ref, kst_ref, vst_ref, head0, S // BLK)

        def step(qs, dos, lts, kbs, last, carry, diag):
            U = range(len(kbs))
            ok = [None if diag else kb <= last for kb in kbs]
            kbs = [jnp.minimum(kb, last) for kb in kbs]
            kv = [_sb_load_kv(kst_ref, vst_ref, kb) for kb in kbs]
            zs = [[lax.dot_general(qs[p], kv[u][0][p], NT, preferred_element_type=f32) for p in range(2)] for u in U]
            dws = [[lax.dot_general(dos[p], kv[u][1][p], NT, preferred_element_type=f32) for p in range(2)] for u in U]
            parts = [[_log1m_parts(z) for z in zu] for zu in zs]
            Ls = [[lt[0] for lt in pu] for pu in parts]
            if diag:
                Ls = [[jnp.where(strict2, L, 0.0) for L in Lu] for Lu in Ls]
            pins = [[_split_dot(L, upto2) for L in Lu] for Lu in Ls]
            carry = list(carry)
            ws, das = [], []
            for u in U:
                wu, dau = [], []
                for p in range(2):
                    PL0, PL1 = carry[5 * p], carry[5 * p + 1]
                    tail = _two_halves(lts[2 * p] - PL0, lts[2 * p + 1] - PL1) - pins[u][p]
                    w = jnp.exp(zs[u][p] + Ls[u][p] + tail)
                    l0, l1 = _half_sums(Ls[u][p])
                    if diag:
                        w = jnp.where(strict2, w, 0.0)
                    else:
                        w, l0, l1 = (jnp.where(ok[u], t, 0.0) for t in (w, l0, l1))
                    carry[5 * p], carry[5 * p + 1] = PL0 + l0, PL1 + l1
                    wu.append(w)
                    dau.append(w * dws[u][p])
                ws.append(wu)
                das.append(dau)
            pexs = [[_split_dot(da, below2) for da in dau] for dau in das]
            dzs = []
            for u in U:
                dzu = []
                for p in range(2):
                    dL = _two_halves(carry[5 * p + 2], carry[5 * p + 3]) + pexs[u][p]
                    sg = _sigmoid_from(zs[u][p], parts[u][p][1])
                    dz = das[u][p] * (1.0 - sg) - dL * sg
                    dz = jnp.where(strict2 if diag else ok[u], dz, 0.0)
                    a0, a1 = _half_sums(das[u][p])
                    carry[5 * p + 2], carry[5 * p + 3] = carry[5 * p + 2] + a0, carry[5 * p + 3] + a1
                    dzu.append(_mx(dz))
                dzs.append(dzu)
            dqs = [[jnp.dot(dzs[u][p], kv[u][0][p], preferred_element_type=f32) for p in range(2)] for u in U]
            dks = [[lax.dot_general(dzs[u][p], qs[p], TN, preferred_element_type=f32) for p in range(2)] for u in U]
            dvs = [[lax.dot_general(_mx(ws[u][p]), dos[p], TN, preferred_element_type=f32) for p in range(2)] for u in U]
            for u in U:
                krows = pl.ds(pl.multiple_of(kbs[u] * BLK, BLK), BLK)
                for p in range(2):
                    lanes = slice(p * BLK, (p + 1) * BLK)
                    dka_ref[krows, lanes] += jnp.where(head0, dks[u][p][:BLK], dks[u][p][BLK:])
                    dva_ref[krows, lanes] += jnp.where(head0, dvs[u][p][:BLK], dvs[u][p][BLK:])
                    carry[5 * p + 4] = carry[5 * p + 4] + dqs[u][p]
            return tuple(carry)

        def qblock(n, c):
            qrows = pl.ds(pl.multiple_of(n * BLK, BLK), BLK)
            ltb = lt_ref[qrows, :]
            lts = [jnp.sum(jnp.where(ci == h, ltb, 0.0), axis=-1, keepdims=True) for h in range(4)]
            qs = [_mx(q_ref[qrows, p * BLK:(p + 1) * BLK] * (HD ** -0.5)) for p in range(2)]
            dos = [_mx(dy_ref[qrows, p * BLK:(p + 1) * BLK]) for p in range(2)]
            z1, z2 = jnp.zeros((BLK, 1), f32), jnp.zeros((BLK, BLK), f32)
            carry = lax.fori_loop(
                0, (n + SB_UNROLL - 1) // SB_UNROLL,
                lambda i, cr: step(qs, dos, lts, [SB_UNROLL * i + u for u in range(SB_UNROLL)], n - 1, cr, False),
                (z1, z1, z1, z1, z2) * 2)
            res = step(qs, dos, lts, [n], n, carry, True)
            for p in range(2):
                dq_ref[qrows, p * BLK:(p + 1) * BLK] = (res[5 * p + 4] * (HD ** -0.5)).astype(bf16)
            return c
        lax.fori_loop(0, S // BLK, qblock, 0)
        dk_ref[...] = dka_ref[...].astype(bf16)
        dv_ref[...] = dva_ref[...].astype(bf16)

    spec = lambda j: pl.BlockSpec((S, GW), lambda b: (b, j))
    o = pl.BlockSpec((S, GW), lambda b: (b, 0))
    c_args, c_in, c_out, c_shapes, aliases, c_scratch = _host_specs(comm, 5, 3)
    step = lambda v: (lambda: pl.program_id(0) == v)
    stacked = pltpu.VMEM((2, S // BLK, 2 * BLK, BLK), _MXU)
    return pl.pallas_call(
        _host(body, 5, 3, 4, comm, step(0), step(B - 1)), grid=(B,),
        in_specs=[spec(5), spec(6), spec(7), pl.BlockSpec((S, BLK), lambda b: (b, 0)), o] + c_in,
        out_specs=[o, o, o] + c_out,
        out_shape=[SDS((B * S, GW), bf16)] * 3 + c_shapes,
        input_output_aliases=aliases,
        scratch_shapes=[pltpu.VMEM((S, GW), f32), pltpu.VMEM((S, GW), f32), stacked, stacked] + c_scratch,
        name="sb_bwd" if comm is None else "sb_bwd_exchange",
        compiler_params=_cp("arbitrary"))(proj, proj, proj, ltot, dy, *c_args)


def _bias_expand(rel_bias_t, bucket):
    n = bucket.shape[1]

    def body(r_ref, b_ref, o_ref):
        onehot = (_iota((N_BUCKETS, n), 0) == b_ref[...]).astype(f32)
        o_ref[...] = jnp.dot(r_ref[...], onehot, precision=HIGHEST, preferred_element_type=f32)
    return pl.pallas_call(body, out_shape=SDS((rel_bias_t.shape[0], n), f32), name="bias_expand",
                          compiler_params=_cp())(rel_bias_t, bucket)


def _bias_reduce(dbias, bucket):
    n = bucket.shape[1]

    def body(*refs):
        b_ref, g_ref = refs[-2], refs[-1]
        d = refs[0][...]
        for r in refs[1:-2]:
            d = d + r[...]
        onehot = (_iota((N_BUCKETS, n), 0) == b_ref[...]).astype(f32)
        g_ref[...] = lax.dot_general(d, onehot, NT, precision=HIGHEST, preferred_element_type=f32)
    return pl.pallas_call(body, out_shape=SDS((dbias[0].shape[0], N_BUCKETS), f32), name="bias_reduce",
                          compiler_params=_cp())(*dbias, bucket)


def _adamw(w, g, m, v, tr, name):
    R, C = w.shape

    def body(w_ref, g_ref, m_ref, v_ref, d_ref, m2_ref, v2_ref):
        gv = g_ref[...]
        m2 = ADAM_B1 * m_ref[...] + (1.0 - ADAM_B1) * gv
        v2 = ADAM_B2 * v_ref[...] + (1.0 - ADAM_B2) * (gv * gv)
        m_hat = m2 / (1.0 - ADAM_B1 ** ADAM_STEP)
        v_hat = v2 / (1.0 - ADAM_B2 ** ADAM_STEP)
        d_ref[...] = -ADAM_LR * (m_hat / (jnp.sqrt(v_hat) + ADAM_EPS) + ADAM_WD * w_ref[...])
        m2_ref[...] = m2
        v2_ref[...] = v2

    spec = pl.BlockSpec((tr, C), lambda i: (i, 0))
    return pl.pallas_call(
        body, grid=(R // tr,), in_specs=[spec] * 4, out_specs=[spec] * 3,
        out_shape=[SDS((R, C), f32)] * 3, name=name, compiler_params=_cp("parallel"))(w, g, m, v)


ANY = pl.BlockSpec(memory_space=pl.ANY)


def _place():
    x, y, c = lax.axis_index("x"), lax.axis_index("y"), lax.axis_index("c")
    chips = [(1 - x, y), (x, 1 - y), (1 - x, 1 - y)]
    return x, y, c, chips


def _cast_slots(w, kidx):
    L, a, b = w.shape
    ta = a // 2

    def body(k_ref, *refs):
        for l in range(L):
            refs[L + l][0] = refs[l][0].astype(bf16)

    return pl.pallas_call(
        body,
        grid_spec=pltpu.PrefetchScalarGridSpec(
            num_scalar_prefetch=1, grid=(a // ta,),
            in_specs=[pl.BlockSpec((1, ta, b), functools.partial(lambda i, k_ref, l: (l, i, 0), l=l)) for l in range(L)],
            out_specs=[pl.BlockSpec((1, ta, b), lambda i, k_ref: (k_ref[0], i, 0)) for _ in range(L)]),
        out_shape=[SDS((N_CHIPS, a, b), bf16)] * L,
        name="cast_slots", compiler_params=_cp("parallel"))(kidx, *([w] * L))


class _GatherComm:
    def __init__(self, bufs):
        self.inputs = list(bufs)
        self.out_shape = [SDS(b.shape, b.dtype) for b in bufs]
        self.aliased = True
        self.scratch = [pltpu.SemaphoreType.DMA((3 * len(bufs),))] * 4

    def _copies(self, i_refs, o_refs, sems):
        send1, recv1, send2, recv2 = sems
        x, y, c, chips = _place()
        k = 2 * x + y
        first, got1, second, got2 = [], [], [], []
        for i, buf in enumerate(self.inputs):
            h = buf.shape[1] // 2
            mine, theirs = pl.ds(c * h, h), pl.ds((1 - c) * h, h)
            for j, (cx, cy) in enumerate(chips):
                s = 3 * i + j
                first.append(pltpu.make_async_remote_copy(
                    src_ref=i_refs[i].at[k, mine], dst_ref=o_refs[i].at[k, mine], send_sem=send1.at[s],
                    recv_sem=recv1.at[s], device_id=(cx, cy, c), device_id_type=MESH))
                a = o_refs[i].at[2 * cx + cy, mine]
                got1.append(pltpu.make_async_remote_copy(
                    src_ref=a, dst_ref=a, send_sem=send1.at[s], recv_sem=recv1.at[s],
                    device_id=(cx, cy, c), device_id_type=MESH))
                second.append(pltpu.make_async_remote_copy(
                    src_ref=a, dst_ref=a, send_sem=send2.at[s], recv_sem=recv2.at[s],
                    device_id=(x, y, 1 - c), device_id_type=MESH))
                b = o_refs[i].at[2 * cx + cy, theirs]
                got2.append(pltpu.make_async_remote_copy(
                    src_ref=b, dst_ref=b, send_sem=send2.at[s], recv_sem=recv2.at[s],
                    device_id=(x, y, 1 - c), device_id_type=MESH))
        return first, got1, second, got2

    def start(self, i_refs, o_refs, sems):
        for cp in self._copies(i_refs, o_refs, sems)[0]:
            cp.start()

    def finish(self, i_refs, o_refs, sems):
        first, got1, second, got2 = self._copies(i_refs, o_refs, sems)
        for g, cp in zip(got1, second):
            g.wait_recv()
            cp.start()
        for g in got2:
            g.wait_recv()
        for cp in first + second:
            cp.wait_send()


class _PairExchangeComm:
    def __init__(self, gs):
        self.inputs = list(gs)
        self.out_shape = [SDS((g.shape[0], g.shape[1] // 2, g.shape[2]), g.dtype) for g in gs]
        self.aliased = False
        self.scratch = [pltpu.SemaphoreType.DMA((len(gs),))] * 2

    def _copies(self, i_refs, o_refs, sems):
        send, recv = sems
        x, y, c, _ = _place()
        cps = []
        for i, g in enumerate(self.inputs):
            h = g.shape[1] // 2
            cps.append(pltpu.make_async_remote_copy(
                src_ref=i_refs[i].at[:, pl.ds((1 - c) * h, h)], dst_ref=o_refs[i], send_sem=send.at[i], recv_sem=recv.at[i],
                device_id=(x, y, 1 - c), device_id_type=MESH))
        return cps

    def start(self, i_refs, o_refs, sems):
        for cp in self._copies(i_refs, o_refs, sems):
            cp.start()

    def finish(self, i_refs, o_refs, sems):
        for cp in self._copies(i_refs, o_refs, sems):
            cp.wait()


class _ChipExchangeComm:
    def __init__(self, qs):
        self.inputs = list(qs)
        self.out_shape = [SDS(q.shape, q.dtype) for q in qs]
        self.aliased = False
        self.scratch = [pltpu.SemaphoreType.DMA((3 * len(qs),))] * 2

    def _copies(self, i_refs, o_refs, sems):
        send, recv = sems
        x, y, c, chips = _place()
        k = 2 * x + y
        cps, got = [], []
        for i in range(len(self.inputs)):
            for j, (cx, cy) in enumerate(chips):
                s = 3 * i + j
                cps.append(pltpu.make_async_remote_copy(
                    src_ref=i_refs[i].at[2 * cx + cy], dst_ref=o_refs[i].at[k], send_sem=send.at[s],
                    recv_sem=recv.at[s], device_id=(cx, cy, c), device_id_type=MESH))
                a = o_refs[i].at[2 * cx + cy]
                got.append(pltpu.make_async_remote_copy(
                    src_ref=a, dst_ref=a, send_sem=send.at[s], recv_sem=recv.at[s],
                    device_id=(cx, cy, c), device_id_type=MESH))
        return cps, got

    def start(self, i_refs, o_refs, sems):
        for cp in self._copies(i_refs, o_refs, sems)[0]:
            cp.start()

    def finish(self, i_refs, o_refs, sems):
        cps, got = self._copies(i_refs, o_refs, sems)
        for g in got:
            g.wait_recv()
        for cp in cps:
            cp.wait_send()


def _comm_only(comm, name):
    n = len(comm.inputs)

    def body(*refs):
        i_refs, o_refs, sems = refs[:n], refs[n:n + len(comm.out_shape)], refs[n + len(comm.out_shape):]
        comm.start(i_refs, o_refs, sems)
        comm.finish(i_refs, o_refs, sems)

    return pl.pallas_call(
        body, out_shape=comm.out_shape, in_specs=[ANY] * n, out_specs=[ANY] * len(comm.out_shape),
        input_output_aliases={i: i for i in range(n)} if comm.aliased else {},
        scratch_shapes=comm.scratch, name=name,
        compiler_params=pltpu.CompilerParams(has_side_effects=True))(*comm.inputs)


def _host(body, n_in, n_out, n_scratch, comm, first, last):
    if comm is None:
        return body
    ci, co = len(comm.inputs), len(comm.out_shape)

    def wrapped(*refs):
        o = 0
        parts = []
        for n in (n_in, ci, n_out, co, n_scratch):
            parts.append(refs[o:o + n])
            o += n
        hin, cin, hout, cout, hs = parts
        sems = refs[o:]

        @pl.when(first())
        def _():
            comm.start(cin, cout, sems)
        body(*hin, *hout, *hs)

        @pl.when(last())
        def _():
            comm.finish(cin, cout, sems)
    return wrapped


def _host_specs(comm, n_in, n_out):
    if comm is None:
        return [], [], [], [], {}, []
    ci, co = len(comm.inputs), len(comm.out_shape)
    aliases = {n_in + i: n_out + i for i in range(ci)} if comm.aliased else {}
    return comm.inputs, [ANY] * ci, [ANY] * co, comm.out_shape, aliases, comm.scratch


def _pair_add(g, r, cidx, name):
    ns, a, b = g.shape
    h = a // 2
    th = h if h * b * 4 <= 4 * 1024 * 1024 else h // 2

    def body(c_ref, g_ref, r_ref, qf_ref, qb_ref):
        q = g_ref[...] + r_ref[...]
        qf_ref[...] = q
        qb_ref[...] = q.astype(bf16)

    nb = h // th
    spec = pl.BlockSpec((1, th, b), lambda s, i, c_ref: (s, i, 0))
    return pl.pallas_call(
        body,
        grid_spec=pltpu.PrefetchScalarGridSpec(
            num_scalar_prefetch=1, grid=(ns, nb),
            in_specs=[pl.BlockSpec((1, th, b), lambda s, i, c_ref: (s, c_ref[0] * nb + i, 0)), spec],
            out_specs=[spec, spec]),
        out_shape=[SDS((ns, h, b), f32), SDS((ns, h, b), bf16)],
        name=name, compiler_params=_cp("parallel", "parallel"))(cidx, g, r)


def _chip_add(qf, r2, idx, prev, L, name):
    ns, h, b = r2.shape
    th = h if h * b * 4 <= 4 * 1024 * 1024 else h // 2
    nb = h // th

    def body(s_ref, qf_ref, r1_ref, r2_ref, r3_ref, *rest):
        o_ref = rest[-1]
        o_ref[0] = qf_ref[0] + r1_ref[0].astype(f32) + r2_ref[0].astype(f32) + r3_ref[0].astype(f32)

    other = lambda d: pl.BlockSpec((1, th, b), lambda i, s_ref: ((s_ref[0] + d) % ns, i, 0))
    in_specs = [pl.BlockSpec((1, th, b), lambda i, s_ref: (s_ref[0], i, 0)), other(1), other(2), other(3)]
    args = [idx, qf, r2, r2, r2]
    aliases = {}
    if prev is not None:
        in_specs.append(ANY)
        args.append(prev)
        aliases = {5: 0}
    return pl.pallas_call(
        body,
        grid_spec=pltpu.PrefetchScalarGridSpec(
            num_scalar_prefetch=1, grid=(nb,), in_specs=in_specs,
            out_specs=pl.BlockSpec((1, th, b), lambda i, s_ref: (s_ref[2], s_ref[1] * nb + i, 0))),
        out_shape=SDS((L, 2 * h, b), f32), input_output_aliases=aliases,
        name=name, compiler_params=_cp("arbitrary"))(*args)


def _pair_share(gs, hs):
    n = len(gs)
    L = gs[0].shape[0]

    def body(*refs):
        i_refs, o_refs = refs[:n], refs[n:2 * n]
        send, recv = refs[2 * n:]
        x, y, c, _ = _place()
        cps = []
        for i in range(n):
            for l in range(L):
                mine = pl.ds(c * hs[i], hs[i])
                cp = pltpu.make_async_remote_copy(
                    src_ref=i_refs[i].at[l, mine], dst_ref=o_refs[i].at[l, mine], send_sem=send.at[i * L + l],
                    recv_sem=recv.at[i * L + l], device_id=(x, y, 1 - c), device_id_type=MESH)
                cp.start()
                cps.append(cp)
        for i in range(n):
            for l in range(L):
                got = o_refs[i].at[l, pl.ds((1 - c) * hs[i], hs[i])]
                pltpu.make_async_remote_copy(
                    src_ref=got, dst_ref=got, send_sem=send.at[i * L + l], recv_sem=recv.at[i * L + l],
                    device_id=(x, y, 1 - c), device_id_type=MESH).wait_recv()
        for cp in cps:
            cp.wait_send()

    return pl.pallas_call(
        body, out_shape=[SDS(g.shape, g.dtype) for g in gs], in_specs=[ANY] * n, out_specs=[ANY] * n,
        input_output_aliases={i: i for i in range(n)},
        scratch_shapes=[pltpu.SemaphoreType.DMA((n * L,))] * 2,
        name="grad_pair_share", compiler_params=pltpu.CompilerParams(has_side_effects=True))(*gs)


class _SwapComm:
    def __init__(self, arrays):
        self.inputs = list(arrays)
        self.out_shape = [SDS(a.shape, a.dtype) for a in arrays]
        self.aliased = False
        self.scratch = [pltpu.SemaphoreType.DMA((len(arrays),))] * 2

    def _copies(self, i_refs, o_refs, sems):
        send, recv = sems
        x, y, c, _ = _place()
        return [pltpu.make_async_remote_copy(
            src_ref=i_refs[i], dst_ref=o_refs[i], send_sem=send.at[i], recv_sem=recv.at[i],
            device_id=(x, y, 1 - c), device_id_type=MESH) for i in range(len(self.inputs))]

    def start(self, i_refs, o_refs, sems):
        for cp in self._copies(i_refs, o_refs, sems):
            cp.start()

    def finish(self, i_refs, o_refs, sems):
        for cp in self._copies(i_refs, o_refs, sems):
            cp.wait()


class _SlotShareComm:
    def __init__(self, bufs):
        self.inputs = list(bufs)
        self.out_shape = [SDS(b.shape, b.dtype) for b in bufs]
        self.aliased = True
        self.scratch = [pltpu.SemaphoreType.DMA((3 * len(bufs),))] * 2

    def _copies(self, i_refs, o_refs, sems):
        send, recv = sems
        x, y, c, chips = _place()
        k = 2 * x + y
        cps, got = [], []
        for i in range(len(self.inputs)):
            for j, (cx, cy) in enumerate(chips):
                s = 3 * i + j
                cps.append(pltpu.make_async_remote_copy(
                    src_ref=i_refs[i].at[k], dst_ref=o_refs[i].at[k], send_sem=send.at[s], recv_sem=recv.at[s],
                    device_id=(cx, cy, c), device_id_type=MESH))
                a = o_refs[i].at[2 * cx + cy]
                got.append(pltpu.make_async_remote_copy(
                    src_ref=a, dst_ref=a, send_sem=send.at[s], recv_sem=recv.at[s],
                    device_id=(cx, cy, c), device_id_type=MESH))
        return cps, got

    def start(self, i_refs, o_refs, sems):
        for cp in self._copies(i_refs, o_refs, sems)[0]:
            cp.start()

    def finish(self, i_refs, o_refs, sems):
        cps, got = self._copies(i_refs, o_refs, sems)
        for g in got:
            g.wait_recv()
        for cp in cps:
            cp.wait_send()


def _pair_sum_slot(mine, theirs, kidx):
    R, C = mine.shape

    def body(k_ref, a_ref, b_ref, o_ref):
        o_ref[0] = a_ref[...] + b_ref[...]

    spec = pl.BlockSpec((R, C), lambda i, k_ref: (0, 0))
    return pl.pallas_call(
        body,
        grid_spec=pltpu.PrefetchScalarGridSpec(
            num_scalar_prefetch=1, grid=(1,), in_specs=[spec, spec],
            out_specs=pl.BlockSpec((1, R, C), lambda i, k_ref: (k_ref[0], 0, 0))),
        out_shape=SDS((N_CHIPS, R, C), f32), name="small_pair_sum", compiler_params=_cp("arbitrary"))(kidx, mine, theirs)


def _small_sum(g):
    n, R, C = g.shape

    def body(g_ref, o_ref):
        acc = g_ref[0]
        for j in range(1, n):
            acc = acc + g_ref[j]
        o_ref[...] = acc
    return pl.pallas_call(body, out_shape=SDS((R, C), f32), name="small_sum", compiler_params=_cp())(g)


PACK_COLS = 1024


def _rows_of(shape):
    n = int(np.prod(shape)) if len(shape) else 1
    return -(-n // (8 * PACK_COLS)) * 8


def _pack(parts):
    blocks = []
    for p in parts:
        flat = p.reshape(-1)
        r = _rows_of(p.shape)
        blocks.append(jnp.pad(flat, (0, r * PACK_COLS - flat.shape[0])).reshape(r, PACK_COLS))
    return jnp.concatenate(blocks, axis=0)


def _unpack(buf, shapes):
    out, off = [], 0
    for s in shapes:
        n = int(np.prod(s)) if len(s) else 1
        r = _rows_of(s)
        out.append(buf[off:off + r].reshape(-1)[:n].reshape(s))
        off += r
    return out


def _block_diag(w):
    g, a, _ = w.shape
    out = jnp.zeros((g * a, g * a), w.dtype)
    for i in range(g):
        out = lax.dynamic_update_slice(out, w[i], (i * a, i * a))
    return out


def kernel(x, w_in, w_out, sgu_w, sgu_b, pool_w, pool_scale, swa_sinks, rel_bias, mix_out_gain, norm_mix, norm_ffn, w_gate_up, w_down, norm_final, loss_target, m_w_in, m_w_out, m_sgu_w, m_sgu_b, m_pool_w, m_pool_scale, m_swa_sinks, m_rel_bias, m_mix_out_gain, m_norm_mix, m_norm_ffn, m_w_gate_up, m_w_down, m_norm_final, v_w_in, v_w_out, v_sgu_w, v_sgu_b, v_pool_w, v_pool_scale, v_swa_sinks, v_rel_bias, v_mix_out_gain, v_norm_mix, v_norm_ffn, v_w_gate_up, v_w_down, v_norm_final):
    B, S, D = x.shape
    T = B * S
    L = w_in.shape[0]
    tm = min(512, T)
    F = w_down.shape[1] * N_CHIPS
    xi, yi, ci = lax.axis_index("x"), lax.axis_index("y"), lax.axis_index("c")
    cidx = jnp.reshape(ci, (1,)).astype(jnp.int32)
    kidx = jnp.reshape(2 * xi + yi, (1,)).astype(jnp.int32)

    big = [w_in, w_out, w_gate_up, w_down]
    slots = [_cast_slots(w, kidx) for w in big]
    Win, rest = [None] * L, [None] * L
    Win[0], = _comm_only(_GatherComm([slots[0][0]]), "gather_weights")

    bucket = jnp.asarray(_t5_bucket_table().reshape(1, -1))
    bias_tab = _bias_expand(rel_bias.T, bucket).reshape(4, BLK, 2 * BLK)

    row = lambda v: v.reshape(1, -1)
    xc = x.reshape(T, D)
    tgt = loss_target.reshape(T, D)
    saved = []
    Wo, Wgu, Wd = ([None] * L for _ in range(3))
    for l in range(L):
        h1, proj = _norm_mm(xc, row(norm_mix[l]), Win[l], tm)
        bexp = jnp.repeat(sgu_b[l].T, HD, axis=1)
        wbd = _block_diag(pool_w[l])
        sk = jnp.broadcast_to(swa_sinks[l][:, None, None], (4, 1, BLK))
        ya = _sgu_fwd(proj, sgu_w[l], bexp, B, S)
        yb = _pool_fwd(proj, wbd, row(pool_scale[l]), B, S)
        yc = _swa_fwd(proj, sk, bias_tab, B, S)
        yd, lt, *rest[l] = _sb_fwd(proj, B, S, _GatherComm([slots[pi][l] for pi in (1, 2, 3)]))
        Wo[l], Wgu[l], Wd[l] = rest[l][0].reshape(D, D), rest[l][1], rest[l][2].reshape(F, D)
        ys = (ya, yb, yc, yd)
        ycn, x1 = _gnorm_mm_res(ys, row(mix_out_gain[l]), Wo[l], xc, tm)
        if l + 1 < L:
            h2, gu, act, Win[l + 1] = _norm_mm_swiglu(x1, row(norm_ffn[l]), Wgu[l], tm, _GatherComm([slots[0][l + 1]]))
        else:
            h2, gu, act = _norm_mm_swiglu(x1, row(norm_ffn[l]), Wgu[l], tm)
        x2 = _mm_res(act, Wd[l], x1, tm)
        saved.append((xc, h1, proj, bexp, wbd, sk, ys, lt, ycn, x1, h2, gu, act))
        xc = x2

    dx, g_final, loss_v = _final_loss(xc, row(norm_final), tgt, tm)

    tk = min(T, 2048)
    gW = [[None] * L for _ in range(4)]
    g_sgu_w, g_sgu_b, g_pool_w, g_pool_scale, g_sinks, g_bias = ([None] * L for _ in range(6))
    g_out_gain, g_mix, g_ffn = ([None] * L for _ in range(3))
    reduced = [None] * 4
    fresh, summed = [], []

    def pair_comm():
        return _PairExchangeComm([gW[pi][l] for pi, l in fresh]) if fresh else None

    def after_pair(r1):
        for (pi, l), r in zip(fresh, r1):
            qf, qb = _pair_add(gW[pi][l], r, cidx, "grad_pair_add")
            summed.append((pi, l, qf, qb))
        fresh.clear()

    def chip_comm():
        return _ChipExchangeComm([s[3] for s in summed]) if summed else None

    def after_chip(r2):
        for (pi, l, qf, _), r in zip(summed, r2):
            idx = jnp.stack([2 * xi + yi, ci, jnp.int32(l)]).astype(jnp.int32)
            reduced[pi] = _chip_add(qf, r, idx, reduced[pi], L, "grad_chip_add")
        summed.clear()

    for l in reversed(range(L)):
        x0, h1, proj, bexp, wbd, sk, ys, lt, ycn, x1, h2, gu, act = saved[l]
        dgu = _dact(dx, Wd[l], gu, tm)
        gW[3][l] = _dw(act, dx, lambda t, s: (t, 0), D, 1, F // 2, tk // 2, "dw_down").reshape(N_CHIPS, F // N_CHIPS, D)
        gW[2][l] = _dw(h2, dgu, lambda t, s: (s // 2, t, s % 2), F // 2, N_CHIPS, D, tk, "dw_gate_up")
        fresh += [(3, l), (2, l)]
        comm = pair_comm()
        dx1, g_ffn[l], *r1 = _dx_norm_bwd(dgu, Wgu[l], x1, row(norm_ffn[l]), dx, tm, "dx_ffn_exchange", comm)
        after_pair(r1)
        gW[1][l] = _dw(ycn, dx1, lambda t, s: (t, 0), D, 1, D, tk, "dw_out").reshape(N_CHIPS, D // N_CHIPS, D)
        fresh.append((1, l))
        dya, dyb, dyc, dyd, g_out_gain[l] = _dycat(dx1, Wo[l], ys, row(mix_out_gain[l]), tm)
        dpa, g_sgu_w[l], dbf = _sgu_bwd(proj, sgu_w[l], bexp, dya, B, S)
        g_sgu_b[l] = dbf[:, ::HD].T
        dpb, dwbd, dsc = _pool_bwd(proj, wbd, row(pool_scale[l]), dyb, B, S)
        npg = len(POOL_WINDOWS)
        g_pool_w[l] = jnp.stack([dwbd[i * HD:(i + 1) * HD, i * HD:(i + 1) * HD] for i in range(npg)])
        g_pool_scale[l] = dsc[0]
        dcq, dckv, dsk, g_bias[l], *r1 = _swa_bwd(proj, sk, bias_tab, dyc, B, S, pair_comm())
        after_pair(r1)
        g_sinks[l] = dsk[:, 0, 0] * float(BLK)
        ddq, ddk, ddv, *r2 = _sb_bwd(proj, lt, dyd, B, S, chip_comm())
        after_chip(r2)
        dproj = jnp.concatenate([dpa, dpb, dcq, dckv, ddq, ddk, ddv], axis=1)
        gW[0][l] = _dw(h1, dproj, lambda t, s: (t, s), w_in.shape[2], N_CHIPS, D, tk, "dw_in")
        fresh.append((0, l))
        dx, g_mix[l] = _dx_norm_bwd(dproj, Win[l], x0, row(norm_mix[l]), dx1, tm, "dx_mix")
    grad_x = dx.reshape(B, S, D)

    after_pair(_comm_only(pair_comm(), "grad_pair_exchange"))
    after_chip(_comm_only(chip_comm(), "grad_chip_exchange"))
    g_big = _pair_share(reduced, [g.shape[1] // 2 for g in reduced])

    g_rel_bias = _bias_reduce([g.reshape(4, -1) for g in g_bias], bucket).T
    small_g = [jnp.stack(g_sgu_w), jnp.stack(g_sgu_b), jnp.stack(g_pool_w), jnp.stack(g_pool_scale), jnp.stack(g_sinks),
               g_rel_bias, jnp.concatenate(g_out_gain), jnp.concatenate(g_mix), jnp.concatenate(g_ffn), g_final[0]]
    small_w = [sgu_w, sgu_b, pool_w, pool_scale, swa_sinks, rel_bias, mix_out_gain, norm_mix, norm_ffn, norm_final]
    small_m = [m_sgu_w, m_sgu_b, m_pool_w, m_pool_scale, m_swa_sinks, m_rel_bias, m_mix_out_gain, m_norm_mix, m_norm_ffn, m_norm_final]
    small_v = [v_sgu_w, v_sgu_b, v_pool_w, v_pool_scale, v_swa_sinks, v_rel_bias, v_mix_out_gain, v_norm_mix, v_norm_ffn, v_norm_final]
    shapes = [w.shape for w in small_w]
    mine = _pack(small_g + [loss_v[0, 0:1]])
    theirs, = _comm_only(_SwapComm([mine]), "small_pair_swap")
    shared, = _comm_only(_SlotShareComm([_pair_sum_slot(mine, theirs, kidx)]), "small_chip_share")
    packed = _small_sum(shared)
    *g_small, loss = _unpack(packed, shapes + [()])
    g_small_packed = _pack(g_small)
    ds, ms, vs = _adamw(_pack(small_w), g_small_packed, _pack(small_m), _pack(small_v), g_small_packed.shape[0], "adamw_small")
    d_small, m_small, v_small = _unpack(ds, shapes), _unpack(ms, shapes), _unpack(vs, shapes)

    big_m = [m_w_in, m_w_out, m_w_gate_up, m_w_down]
    big_v = [v_w_in, v_w_out, v_w_gate_up, v_w_down]
    d_big, m_big, v_big = [], [], []
    for w, g, m, v in zip(big, g_big, big_m, big_v):
        two = lambda a: a.reshape(-1, a.shape[-1])
        rows = two(w).shape[0]
        d2, m2, v2 = _adamw(two(w), two(g), two(m), two(v), rows // 8 if rows >= 2048 else rows, "adamw_big")
        d_big.append(d2.reshape(w.shape))
        m_big.append(m2.reshape(w.shape))
        v_big.append(v2.reshape(w.shape))

    def order(bigs, smalls):
        return [bigs[0], bigs[1]] + list(smalls[0:9]) + [bigs[2], bigs[3], smalls[9]]

    return (loss, grad_x, *order(g_big, g_small), *order(d_big, d_small), *order(m_big, m_small), *order(v_big, v_small))
```

```python
import functools

import numpy as np
import jax
import jax.numpy as jnp
from jax import lax
from jax.experimental import pallas as pl
from jax.experimental.pallas import tpu as pltpu

f32 = jnp.float32
bf16 = jnp.bfloat16
_MXU = jnp.bfloat16

EPS = 1e-6
HD = 64
GW = 256
BLK = 128
SB_UNROLL = 2
SB_HEADS = 4
SB_CUT = -110.0
POOL_WINDOWS = (2, 4, 8, 16)
N_BUCKETS = 32
MAX_DISTANCE = 128
N_CHIPS = 4
N_DEV = 8
VMEM_BYTES_V7X = 64 * 1024 * 1024
VMEM_LIMIT = 48 * 1024 * 1024

ADAM_LR = 0.001
ADAM_B1 = 0.9
ADAM_B2 = 0.999
ADAM_EPS = 1e-08
ADAM_WD = 0.01
ADAM_STEP = 10

SDS = jax.ShapeDtypeStruct
MESH = pl.DeviceIdType.MESH
HIGHEST = lax.Precision.HIGHEST
RESIDENT = pl.Buffered(1)
NT = (((1,), (1,)), ((), ()))
TN = (((0,), (0,)), ((), ()))


def _cp(*sem):
    return pltpu.CompilerParams(dimension_semantics=sem if sem else None, vmem_limit_bytes=VMEM_LIMIT)


def _mx(v):
    return v.astype(_MXU)


def _iota(shape, dim):
    return lax.broadcasted_iota(jnp.int32, shape, dim)


def _split_dot(a, tri):
    hi = a.astype(bf16)
    lo = (a - hi.astype(f32)).astype(bf16)
    return jnp.dot(hi, tri, preferred_element_type=f32) + jnp.dot(lo, tri, preferred_element_type=f32)


def _rms(xv):
    return lax.rsqrt(jnp.mean(xv * xv, axis=-1, keepdims=True) + EPS)


def _norm_mm(x, gain, w, tm):
    T, D = x.shape
    NS, _, ns = w.shape

    def body(x_ref, g_ref, w_ref, h_ref, o_ref):
        xv = x_ref[...]
        h = (xv * _rms(xv) * g_ref[...]).astype(bf16)
        h_ref[...] = h
        for s in range(NS):
            o_ref[:, s * ns:(s + 1) * ns] = jnp.dot(_mx(h), w_ref[s], preferred_element_type=f32)

    return pl.pallas_call(
        body, grid=(T // tm,),
        in_specs=[pl.BlockSpec((tm, D), lambda i: (i, 0)),
                  pl.BlockSpec((1, D), lambda i: (0, 0)),
                  pl.BlockSpec((NS, D, ns), lambda i: (0, 0, 0), pipeline_mode=RESIDENT)],
        out_specs=[pl.BlockSpec((tm, D), lambda i: (i, 0)),
                   pl.BlockSpec((tm, NS * ns), lambda i: (i, 0))],
        out_shape=[SDS((T, D), bf16), SDS((T, NS * ns), f32)],
        name="norm_mm_in", compiler_params=_cp("parallel"))(x, gain, w)


def _norm_mm_swiglu(x, gain, w, tm, comm=None):
    T, D = x.shape
    NS, _, ns = w.shape
    half = NS // 2

    def body(x_ref, g_ref, w_ref, h_ref, gu_ref, a_ref):
        xv = x_ref[...]
        hb = (xv * _rms(xv) * g_ref[...]).astype(bf16)
        h_ref[...] = hb
        h = _mx(hb)
        for s in range(half):
            cols = slice(s * ns, (s + 1) * ns)
            g = jnp.dot(h, w_ref[s], preferred_element_type=f32)
            u = jnp.dot(h, w_ref[s + half], preferred_element_type=f32)
            gu_ref[0, :, cols] = g.astype(bf16)
            gu_ref[1, :, cols] = u.astype(bf16)
            a_ref[:, cols] = (jax.nn.silu(g) * u).astype(bf16)

    c_args, c_in, c_out, c_shapes, aliases, c_scratch = _host_specs(comm, 3, 3)
    step = lambda v: (lambda: pl.program_id(0) == v)
    return pl.pallas_call(
        _host(body, 3, 3, 0, comm, step(0), step(T // tm - 1)), grid=(T // tm,),
        in_specs=[pl.BlockSpec((tm, D), lambda i: (i, 0)),
                  pl.BlockSpec((1, D), lambda i: (0, 0)),
                  pl.BlockSpec((NS, D, ns), lambda i: (0, 0, 0), pipeline_mode=RESIDENT)] + c_in,
        out_specs=[pl.BlockSpec((tm, D), lambda i: (i, 0)),
                   pl.BlockSpec((2, tm, half * ns), lambda i: (0, i, 0)),
                   pl.BlockSpec((tm, half * ns), lambda i: (i, 0))] + c_out,
        out_shape=[SDS((T, D), bf16), SDS((2, T, half * ns), bf16), SDS((T, half * ns), bf16)] + c_shapes,
        input_output_aliases=aliases, scratch_shapes=c_scratch,
        name="norm_mm_swiglu" if comm is None else "norm_mm_swiglu_gather",
        compiler_params=_cp("arbitrary"))(x, gain, w, *c_args)


def _gnorm_mm_res(ys, gain, w, x, tm):
    T, D = x.shape

    def body(ya, yb, yc, yd, g_ref, w_ref, x_ref, yn_ref, o_ref):
        parts = []
        for m, r in enumerate((ya, yb, yc, yd)):
            y = r[...]
            parts.append((y * _rms(y) * g_ref[:, m * GW:(m + 1) * GW]).astype(bf16))
        yn = jnp.concatenate(parts, axis=1)
        yn_ref[...] = yn
        o_ref[...] = x_ref[...] + jnp.dot(_mx(yn), w_ref[...], preferred_element_type=f32)

    yspec = pl.BlockSpec((tm, GW), lambda i: (i, 0))
    return pl.pallas_call(
        body, grid=(T // tm,),
        in_specs=[yspec, yspec, yspec, yspec,
                  pl.BlockSpec((1, D), lambda i: (0, 0)),
                  pl.BlockSpec((D, D), lambda i: (0, 0)),
                  pl.BlockSpec((tm, D), lambda i: (i, 0))],
        out_specs=[pl.BlockSpec((tm, D), lambda i: (i, 0)), pl.BlockSpec((tm, D), lambda i: (i, 0))],
        out_shape=[SDS((T, D), bf16), SDS((T, D), f32)],
        name="gnorm_mm_res", compiler_params=_cp("parallel"))(*ys, gain, w, x)


def _mm_res(a, w, x, tm):
    T, D = x.shape
    K = a.shape[1]

    def body(a_ref, w_ref, x_ref, o_ref):
        o_ref[...] = x_ref[...] + jnp.dot(_mx(a_ref[...]), w_ref[...], preferred_element_type=f32)

    return pl.pallas_call(
        body, grid=(T // tm,),
        in_specs=[pl.BlockSpec((tm, K), lambda i: (i, 0)),
                  pl.BlockSpec((K, D), lambda i: (0, 0)),
                  pl.BlockSpec((tm, D), lambda i: (i, 0))],
        out_specs=pl.BlockSpec((tm, D), lambda i: (i, 0)),
        out_shape=SDS((T, D), f32),
        name="mm_res_down", compiler_params=_cp("parallel"))(a, w, x)


def _final_loss(x, gain, tgt, tm):
    T, D = x.shape

    def body(x_ref, g_ref, t_ref, dx_ref, dg_ref, l_ref):
        @pl.when(pl.program_id(0) == 0)
        def _():
            dg_ref[...] = jnp.zeros_like(dg_ref)
            l_ref[...] = jnp.zeros_like(l_ref)
        xv = x_ref[...]
        g = g_ref[...]
        r = _rms(xv)
        xh = xv * r
        err = xh * g - t_ref[...]
        l_ref[...] += 0.5 * jnp.sum(jnp.mean(err * err, axis=-1, keepdims=True), axis=0, keepdims=True)
        dy = err * (1.0 / D)
        dg_ref[...] += jnp.sum(dy * xh, axis=0, keepdims=True)
        dxh = dy * g
        dx_ref[...] = r * (dxh - xh * jnp.mean(dxh * xh, axis=-1, keepdims=True))

    return pl.pallas_call(
        body, grid=(T // tm,),
        in_specs=[pl.BlockSpec((tm, D), lambda i: (i, 0)),
                  pl.BlockSpec((1, D), lambda i: (0, 0)),
                  pl.BlockSpec((tm, D), lambda i: (i, 0))],
        out_specs=[pl.BlockSpec((tm, D), lambda i: (i, 0)),
                   pl.BlockSpec((1, D), lambda i: (0, 0)),
                   pl.BlockSpec((1, BLK), lambda i: (0, 0))],
        out_shape=[SDS((T, D), f32), SDS((1, D), f32), SDS((1, BLK), f32)],
        name="final_loss", compiler_params=_cp("arbitrary"))(x, gain, tgt)


def _dact(dx, wd, gu, tm):
    T, D = dx.shape
    F = wd.shape[0]
    ns = F // 2

    def body(dx_ref, w_ref, gu_ref, o_ref):
        dxb = _mx(dx_ref[...])
        for s in range(2):
            cols = slice(s * ns, (s + 1) * ns)
            da = lax.dot_general(dxb, w_ref[s * ns:(s + 1) * ns, :], NT, preferred_element_type=f32)
            g = gu_ref[0, :, cols].astype(f32)
            u = gu_ref[1, :, cols].astype(f32)
            sg = jax.nn.sigmoid(g)
            o_ref[0, :, cols] = (da * u * (sg * (1.0 + g * (1.0 - sg)))).astype(bf16)
            o_ref[1, :, cols] = (da * (g * sg)).astype(bf16)

    return pl.pallas_call(
        body, grid=(T // tm,),
        in_specs=[pl.BlockSpec((tm, D), lambda i: (i, 0)),
                  pl.BlockSpec((F, D), lambda i: (0, 0), pipeline_mode=RESIDENT),
                  pl.BlockSpec((2, tm, F), lambda i: (0, i, 0))],
        out_specs=pl.BlockSpec((2, tm, F), lambda i: (0, i, 0)),
        out_shape=SDS((2, T, F), bf16),
        name="dact", compiler_params=_cp("parallel"))(dx, wd, gu)


def _dw(a, b, b_map, ns, NS, tka, tk, name):
    T, Ka = a.shape
    b_block = (tk, ns) if b.ndim == 2 else (1, tk, ns)

    def body(a_ref, b_ref, o_ref):
        bv = b_ref[...] if b.ndim == 2 else b_ref[0]
        part = lax.dot_general(_mx(a_ref[...]), _mx(bv), TN, preferred_element_type=f32)

        @pl.when(pl.program_id(2) == 0)
        def _():
            o_ref[0] = part

        @pl.when(pl.program_id(2) > 0)
        def _():
            o_ref[0] += part

    return pl.pallas_call(
        body, grid=(NS, Ka // tka, T // tk),
        in_specs=[pl.BlockSpec((tk, tka), lambda s, k, t: (t, k)),
                  pl.BlockSpec(b_block, lambda s, k, t: b_map(t, s))],
        out_specs=pl.BlockSpec((1, tka, ns), lambda s, k, t: (s, k, 0)),
        out_shape=SDS((NS, Ka, ns), f32),
        name=name, compiler_params=_cp("parallel", "parallel", "arbitrary"))(a, b)


def _dx_norm_bwd(dy, w, x, gain, dxin, tm, name, comm=None):
    T, D = x.shape
    NS, _, ns = w.shape
    half = NS // 2

    def body(dy_ref, w_ref, x_ref, g_ref, dxin_ref, dx_ref, dg_ref):
        @pl.when(pl.program_id(0) == 0)
        def _():
            dg_ref[...] = jnp.zeros_like(dg_ref)
        dh = None
        for s in range(NS):
            if dy.ndim == 2:
                dv = dy_ref[:, s * ns:(s + 1) * ns]
            else:
                dv = dy_ref[s // half, :, (s % half) * ns:(s % half + 1) * ns]
            part = lax.dot_general(_mx(dv), w_ref[s], NT, preferred_element_type=f32)
            dh = part if dh is None else dh + part
        xv = x_ref[...]
        r = _rms(xv)
        xh = xv * r
        dg_ref[...] += jnp.sum(dh * xh, axis=0, keepdims=True)
        dxh = dh * g_ref[...]
        dx_ref[...] = dxin_ref[...] + r * (dxh - xh * jnp.mean(dxh * xh, axis=-1, keepdims=True))

    dy_spec = (pl.BlockSpec((tm, NS * ns), lambda i: (i, 0)) if dy.ndim == 2
               else pl.BlockSpec((2, tm, half * ns), lambda i: (0, i, 0)))
    c_args, c_in, c_out, c_shapes, aliases, c_scratch = _host_specs(comm, 5, 2)
    step = lambda v: (lambda: pl.program_id(0) == v)
    return pl.pallas_call(
        _host(body, 5, 2, 0, comm, step(0), step(T // tm - 1)), grid=(T // tm,),
        in_specs=[dy_spec,
                  pl.BlockSpec((NS, D, ns), lambda i: (0, 0, 0), pipeline_mode=RESIDENT),
                  pl.BlockSpec((tm, D), lambda i: (i, 0)),
                  pl.BlockSpec((1, D), lambda i: (0, 0)),
                  pl.BlockSpec((tm, D), lambda i: (i, 0))] + c_in,
        out_specs=[pl.BlockSpec((tm, D), lambda i: (i, 0)),
                   pl.BlockSpec((1, D), lambda i: (0, 0))] + c_out,
        out_shape=[SDS((T, D), f32), SDS((1, D), f32)] + c_shapes,
        input_output_aliases=aliases, scratch_shapes=c_scratch,
        name=name, compiler_params=_cp("arbitrary"))(dy, w, x, gain, dxin, *c_args)


def _dycat(dx, w, ys, gain, tm):
    T, D = dx.shape

    def body(dx_ref, w_ref, ya, yb, yc, yd, g_ref, da, db, dc, dd, dg_ref):
        @pl.when(pl.program_id(0) == 0)
        def _():
            dg_ref[...] = jnp.zeros_like(dg_ref)
        dyn = lax.dot_general(_mx(dx_ref[...]), w_ref[...], NT, preferred_element_type=f32)
        for m, (r, o) in enumerate(((ya, da), (yb, db), (yc, dc), (yd, dd))):
            cols = slice(m * GW, (m + 1) * GW)
            y = r[...]
            rs = _rms(y)
            yh = y * rs
            d = dyn[:, cols]
            dg_ref[:, cols] += jnp.sum(d * yh, axis=0, keepdims=True)
            dyh = d * g_ref[:, cols]
            o[...] = rs * (dyh - yh * jnp.mean(dyh * yh, axis=-1, keepdims=True))

    yspec = pl.BlockSpec((tm, GW), lambda i: (i, 0))
    return pl.pallas_call(
        body, grid=(T // tm,),
        in_specs=[pl.BlockSpec((tm, D), lambda i: (i, 0)),
                  pl.BlockSpec((D, D), lambda i: (0, 0)),
                  yspec, yspec, yspec, yspec,
                  pl.BlockSpec((1, D), lambda i: (0, 0))],
        out_specs=[yspec, yspec, yspec, yspec, pl.BlockSpec((1, D), lambda i: (0, 0))],
        out_shape=[SDS((T, GW), f32)] * 4 + [SDS((1, D), f32)],
        name="dycat", compiler_params=_cp("arbitrary"))(dx, w, *ys, gain)


def _sgu_consts():
    r, c = _iota((GW, GW), 0), _iota((GW, GW), 1)
    seg = (r // HD == c // HD).astype(f32)
    tr, ts = _iota((BLK, BLK), 0), _iota((BLK, BLK), 1)
    causal = ts <= tr
    lane_head = _iota((BLK, GW), 1) // HD
    return seg, causal, lane_head


def _split3_dot(a, ones):
    hi = a.astype(bf16)
    r1 = a - hi.astype(f32)
    mid = r1.astype(bf16)
    lo = (r1 - mid.astype(f32)).astype(bf16)
    dot = functools.partial(jnp.dot, preferred_element_type=f32)
    return dot(hi, ones) + dot(mid, ones) + dot(lo, ones)


def _sgu_chunks(aus, avs, w, bexp, consts):
    seg, causal, lane_head = consts
    segb = seg.astype(bf16)
    nh = GW // HD
    vs = [jax.nn.gelu(av) for av in avs]
    mus = [_split3_dot(v, segb) * (1.0 / HD) for v in vs]
    vcs = [v - mu for v, mu in zip(vs, mus)]
    vars_ = [_split3_dot(vc * vc, segb) * (1.0 / HD) for vc in vcs]
    vns = [_mx(vc * lax.rsqrt(var + EPS)) for vc, var in zip(vcs, vars_)]
    whs = [_mx(jnp.where(causal, w[h], 0.0)) for h in range(nh)]
    mixes = [[jnp.dot(whs[h], vn, preferred_element_type=f32) for h in range(nh)] for vn in vns]
    out = []
    for au, ms in zip(aus, mixes):
        mix = bexp
        for h in range(nh):
            mix = mix + jnp.where(lane_head == h, ms[h], 0.0)
        out.append(jax.nn.gelu(au) * mix)
    return out


def _sgu_group(S):
    nc = S // BLK
    return 4 if nc % 4 == 0 else (2 if nc % 2 == 0 else 1)


def _sgu_fwd(proj, w, bexp, B, S):
    G = _sgu_group(S)

    def body(au_ref, av_ref, w_ref, b_ref, y_ref):
        consts = _sgu_consts()
        wv, bv = w_ref[...], b_ref[...]

        def group(n, c):
            rows = [pl.ds(pl.multiple_of((n * G + j) * BLK, BLK), BLK) for j in range(G)]
            ys = _sgu_chunks([au_ref[r, :] for r in rows], [av_ref[r, :] for r in rows], wv, bv, consts)
            for r, y in zip(rows, ys):
                y_ref[r, :] = y
            return c
        lax.fori_loop(0, S // BLK // G, group, 0)

    return pl.pallas_call(
        body, grid=(B,),
        in_specs=[pl.BlockSpec((S, GW), lambda b: (b, 0)),
                  pl.BlockSpec((S, GW), lambda b: (b, 1)),
                  pl.BlockSpec((GW // HD, BLK, BLK), lambda b: (0, 0, 0)),
                  pl.BlockSpec((BLK, GW), lambda b: (0, 0))],
        out_specs=pl.BlockSpec((S, GW), lambda b: (b, 0)),
        out_shape=SDS((B * S, GW), f32),
        name="sgu_fwd", compiler_params=_cp("parallel"))(proj, proj, w, bexp)


def _sgu_bwd(proj, w, bexp, dy, B, S):
    def body(au_ref, av_ref, w_ref, b_ref, dy_ref, dp_ref, dw_ref, db_ref):
        @pl.when(pl.program_id(0) == 0)
        def _():
            dw_ref[...] = jnp.zeros_like(dw_ref)
            db_ref[...] = jnp.zeros_like(db_ref)
        consts = _sgu_consts()
        wv, bv = w_ref[...], b_ref[...]
        fn = lambda aus, avs, ww, bb: _sgu_chunks(aus, avs, ww, bb, consts)
        G = _sgu_group(S)

        def group(n, carry):
            dw_acc, db_acc = carry
            rows = [pl.ds(pl.multiple_of((n * G + j) * BLK, BLK), BLK) for j in range(G)]
            _, vjp = jax.vjp(fn, [au_ref[r, :] for r in rows], [av_ref[r, :] for r in rows], wv, bv)
            daus, davs, dwc, dbc = vjp([dy_ref[r, :] for r in rows])
            for r, dau, dav in zip(rows, daus, davs):
                dp_ref[r, 0:GW] = dau.astype(bf16)
                dp_ref[r, GW:2 * GW] = dav.astype(bf16)
            return dw_acc + dwc, db_acc + dbc
        dw_acc, db_acc = lax.fori_loop(0, S // BLK // G, group, (jnp.zeros(wv.shape, f32), jnp.zeros(bv.shape, f32)))
        dw_ref[...] += dw_acc
        db_ref[...] += jnp.dot(db_acc, consts[0], precision=HIGHEST, preferred_element_type=f32)

    return pl.pallas_call(
        body, grid=(B,),
        in_specs=[pl.BlockSpec((S, GW), lambda b: (b, 0)),
                  pl.BlockSpec((S, GW), lambda b: (b, 1)),
                  pl.BlockSpec((GW // HD, BLK, BLK), lambda b: (0, 0, 0)),
                  pl.BlockSpec((BLK, GW), lambda b: (0, 0)),
                  pl.BlockSpec((S, GW), lambda b: (b, 0))],
        out_specs=[pl.BlockSpec((S, 2 * GW), lambda b: (b, 0)),
                   pl.BlockSpec((GW // HD, BLK, BLK), lambda b: (0, 0, 0)),
                   pl.BlockSpec((BLK, GW), lambda b: (0, 0))],
        out_shape=[SDS((B * S, 2 * GW), bf16), SDS((GW // HD, BLK, BLK), f32), SDS((BLK, GW), f32)],
        name="sgu_bwd", compiler_params=_cp("arbitrary"))(proj, proj, w, bexp, dy)


def _pool_parts(p):
    n = p.shape[0]
    r = _iota(p.shape, 0)
    lg = _iota(p.shape, 1) // HD

    def sh(v, k):
        return jnp.where(r >= k, pltpu.roll(v, k, 0), 0.0)
    s2 = p + sh(p, 1)
    s4 = s2 + sh(s2, 2)
    s8 = s4 + sh(s4, 4)
    s16 = s8 + sh(s8, 8)
    ws = jnp.where(lg == 0, s2, jnp.where(lg == 1, s4, jnp.where(lg == 2, s8, s16)))
    wlen = jnp.where(lg == 0, 2, jnp.where(lg == 1, 4, jnp.where(lg == 2, 8, 16)))
    cnt = jnp.minimum(r + 1, wlen).astype(f32)
    del n
    return ws / cnt - p, cnt, lg


def _pool_fwd(proj, wbd, scale, B, S):
    def body(p_ref, w_ref, s_ref, y_ref):
        y, _, _ = _pool_parts(p_ref[...])
        y_ref[...] = jnp.dot(_mx(y), _mx(w_ref[...]), preferred_element_type=f32) * s_ref[...]

    return pl.pallas_call(
        body, grid=(B,),
        in_specs=[pl.BlockSpec((S, GW), lambda b: (b, 2)),
                  pl.BlockSpec((GW, GW), lambda b: (0, 0)),
                  pl.BlockSpec((1, GW), lambda b: (0, 0))],
        out_specs=pl.BlockSpec((S, GW), lambda b: (b, 0)),
        out_shape=SDS((B * S, GW), f32),
        name="pool_fwd", compiler_params=_cp("parallel"))(proj, wbd, scale)


def _pool_bwd(proj, wbd, scale, dy, B, S):
    def body(p_ref, w_ref, s_ref, dy_ref, dp_ref, dw_ref, ds_ref):
        @pl.when(pl.program_id(0) == 0)
        def _():
            dw_ref[...] = jnp.zeros_like(dw_ref)
            ds_ref[...] = jnp.zeros_like(ds_ref)
        y, cnt, lg = _pool_parts(p_ref[...])
        wv = _mx(w_ref[...])
        z = jnp.dot(_mx(y), wv, preferred_element_type=f32)
        dout = dy_ref[...]
        ds_ref[...] += jnp.sum(dout * z, axis=0, keepdims=True)
        dz = _mx(dout * s_ref[...])
        dw_ref[...] += lax.dot_general(_mx(y), dz, TN, preferred_element_type=f32)
        dyv = lax.dot_general(dz, wv, NT, preferred_element_type=f32)
        n = dyv.shape[0]
        r = _iota(dyv.shape, 0)

        def ush(v, k):
            return jnp.where(r < n - k, pltpu.roll(v, n - k, 0), 0.0)
        gq = dyv / cnt
        a2 = gq + ush(gq, 1)
        a4 = a2 + ush(a2, 2)
        a8 = a4 + ush(a4, 4)
        a16 = a8 + ush(a8, 8)
        adj = jnp.where(lg == 0, a2, jnp.where(lg == 1, a4, jnp.where(lg == 2, a8, a16)))
        dp_ref[...] = (adj - dyv).astype(bf16)

    return pl.pallas_call(
        body, grid=(B,),
        in_specs=[pl.BlockSpec((S, GW), lambda b: (b, 2)),
                  pl.BlockSpec((GW, GW), lambda b: (0, 0)),
                  pl.BlockSpec((1, GW), lambda b: (0, 0)),
                  pl.BlockSpec((S, GW), lambda b: (b, 0))],
        out_specs=[pl.BlockSpec((S, GW), lambda b: (b, 0)),
                   pl.BlockSpec((GW, GW), lambda b: (0, 0)),
                   pl.BlockSpec((1, GW), lambda b: (0, 0))],
        out_shape=[SDS((B * S, GW), bf16), SDS((GW, GW), f32), SDS((1, GW), f32)],
        name="pool_bwd", compiler_params=_cp("arbitrary"))(proj, wbd, scale, dy)


def _t5_bucket_table():
    dist = (np.arange(BLK)[:, None] + BLK) - np.arange(2 * BLK)[None, :]
    d = np.clip(dist, 0, BLK - 1)
    max_exact = N_BUCKETS // 2
    df = np.maximum(d, 1).astype(np.float32)
    large = max_exact + (np.log(df / max_exact) / np.float32(np.log(MAX_DISTANCE / max_exact))
                         * (N_BUCKETS - max_exact)).astype(np.int32)
    large = np.minimum(large, N_BUCKETS - 1)
    return np.where(d < max_exact, d, large).astype(np.int32)


def _swa_block(qb0, qb1, k2, v2, sinks, biases, n):
    heads = [(p, g) for p in range(2) for g in range(2)]
    ri, ci = _iota((BLK, BLK), 0), _iota((BLK, BLK), 1)
    qi, ki = _iota((BLK, 2 * BLK), 0), _iota((BLK, 2 * BLK), 1)
    dist = qi + BLK - ki
    mask = (dist >= 0) & (dist < BLK) & ((ki >= BLK) | (n > 0))
    qbs, kb, vb = (_mx(qb0), _mx(qb1)), _mx(k2), _mx(v2)
    qs, vs = [], []
    for p, g in heads:
        selq = ((ri - g * HD == ci - p * HD) & (ri >= g * HD) & (ri < (g + 1) * HD)).astype(_MXU)
        selv = ((ci - g * HD == ri - p * HD) & (ci >= g * HD) & (ci < (g + 1) * HD)).astype(_MXU)
        qs.append(_mx(jnp.dot(qbs[p], selq, preferred_element_type=f32)))
        vs.append(_mx(jnp.dot(vb, selv, preferred_element_type=f32)))
    zs = [lax.dot_general(q, kb, NT, preferred_element_type=f32) * (HD ** -0.5) for q in qs]
    prs = []
    for h in range(4):
        z = jnp.where(mask, zs[h] + biases[h], -1e30)
        s = jnp.mean(sinks[h], axis=-1, keepdims=True)
        m = jnp.maximum(jnp.max(z, axis=-1, keepdims=True), s)
        e = jnp.exp(z - m)
        prs.append(_mx(e / (jnp.sum(e, axis=-1, keepdims=True) + jnp.exp(s - m))))
    outs = [jnp.dot(prs[h], vs[h], preferred_element_type=f32) for h in range(4)]
    return outs[0] + outs[1], outs[2] + outs[3]


def _swa_fwd(proj, sinks, bias, B, S):
    def body(q_ref, kv_ref, s_ref, b_ref, y_ref):
        def block(n, c):
            rows = pl.ds(pl.multiple_of(n * BLK, BLK), BLK)
            prev = pl.ds(pl.multiple_of(jnp.maximum(n - 1, 0) * BLK, BLK), BLK)
            k2 = jnp.concatenate([kv_ref[prev, 0:BLK], kv_ref[rows, 0:BLK]], axis=0)
            v2 = jnp.concatenate([kv_ref[prev, BLK:2 * BLK], kv_ref[rows, BLK:2 * BLK]], axis=0)
            o0, o1 = _swa_block(q_ref[rows, 0:BLK], q_ref[rows, BLK:2 * BLK], k2, v2,
                                [s_ref[h] for h in range(4)], [b_ref[h] for h in range(4)], n)
            y_ref[rows, 0:BLK] = o0
            y_ref[rows, BLK:2 * BLK] = o1
            return c
        lax.fori_loop(0, S // BLK, block, 0)

    return pl.pallas_call(
        body, grid=(B,),
        in_specs=[pl.BlockSpec((S, GW), lambda b: (b, 3)),
                  pl.BlockSpec((S, GW), lambda b: (b, 4)),
                  pl.BlockSpec((4, 1, BLK), lambda b: (0, 0, 0)),
                  pl.BlockSpec((4, BLK, 2 * BLK), lambda b: (0, 0, 0))],
        out_specs=pl.BlockSpec((S, GW), lambda b: (b, 0)),
        out_shape=SDS((B * S, GW), f32),
        name="swa_fwd", compiler_params=_cp("parallel"))(proj, proj, sinks, bias)


def _swa_bwd(proj, sinks, bias, dy, B, S, comm=None):
    def body(q_ref, kv_ref, s_ref, b_ref, dy_ref, dq_ref, dkv_ref, ds_ref, db_ref, acc_ref):
        @pl.when(pl.program_id(0) == 0)
        def _():
            ds_ref[...] = jnp.zeros_like(ds_ref)
            db_ref[...] = jnp.zeros_like(db_ref)
        acc_ref[...] = jnp.zeros_like(acc_ref)

        def block(n, c):
            rows = pl.ds(pl.multiple_of(n * BLK, BLK), BLK)
            prev = pl.ds(pl.multiple_of(jnp.maximum(n - 1, 0) * BLK, BLK), BLK)
            k2 = jnp.concatenate([kv_ref[prev, 0:BLK], kv_ref[rows, 0:BLK]], axis=0)
            v2 = jnp.concatenate([kv_ref[prev, BLK:2 * BLK], kv_ref[rows, BLK:2 * BLK]], axis=0)
            fn = functools.partial(_swa_block, n=n)
            _, vjp = jax.vjp(fn, q_ref[rows, 0:BLK], q_ref[rows, BLK:2 * BLK], k2, v2,
                             [s_ref[h] for h in range(4)], [b_ref[h] for h in range(4)])
            dq0, dq1, dk2, dv2, dss, dbs = vjp((dy_ref[rows, 0:BLK], dy_ref[rows, BLK:2 * BLK]))
            dq_ref[rows, 0:BLK] = dq0.astype(bf16)
            dq_ref[rows, BLK:2 * BLK] = dq1.astype(bf16)
            for h in range(4):
                ds_ref[h] += dss[h]
                db_ref[h] += dbs[h]
            acc_ref[prev, 0:BLK] += dk2[0:BLK]
            acc_ref[rows, 0:BLK] += dk2[BLK:2 * BLK]
            acc_ref[prev, BLK:2 * BLK] += dv2[0:BLK]
            acc_ref[rows, BLK:2 * BLK] += dv2[BLK:2 * BLK]
            return c
        lax.fori_loop(0, S // BLK, block, 0)
        dkv_ref[...] = acc_ref[...].astype(bf16)

    c_args, c_in, c_out, c_shapes, aliases, c_scratch = _host_specs(comm, 5, 4)
    step = lambda v: (lambda: pl.program_id(0) == v)
    return pl.pallas_call(
        _host(body, 5, 4, 1, comm, step(0), step(B - 1)), grid=(B,),
        in_specs=[pl.BlockSpec((S, GW), lambda b: (b, 3)),
                  pl.BlockSpec((S, GW), lambda b: (b, 4)),
                  pl.BlockSpec((4, 1, BLK), lambda b: (0, 0, 0)),
                  pl.BlockSpec((4, BLK, 2 * BLK), lambda b: (0, 0, 0)),
                  pl.BlockSpec((S, GW), lambda b: (b, 0))] + c_in,
        out_specs=[pl.BlockSpec((S, GW), lambda b: (b, 0)),
                   pl.BlockSpec((S, GW), lambda b: (b, 0)),
                   pl.BlockSpec((4, 1, BLK), lambda b: (0, 0, 0)),
                   pl.BlockSpec((4, BLK, 2 * BLK), lambda b: (0, 0, 0))] + c_out,
        out_shape=[SDS((B * S, GW), bf16), SDS((B * S, GW), bf16), SDS((4, 1, BLK), f32),
                   SDS((4, BLK, 2 * BLK), f32)] + c_shapes,
        input_output_aliases=aliases, scratch_shapes=[pltpu.VMEM((S, GW), f32)] + c_scratch,
        name="swa_bwd" if comm is None else "swa_bwd_exchange",
        compiler_params=_cp("arbitrary"))(proj, proj, sinks, bias, dy, *c_args)


def _log1m_parts(z):
    t = jnp.exp(-jnp.abs(z))
    return jnp.minimum(-z, 0.0) - jnp.log(1.0 + t), t


def _log1m(z):
    return _log1m_parts(z)[0]


def _sigmoid_from(z, t):
    return jnp.where(z >= 0.0, 1.0, t) / (1.0 + t)


def _sb_consts(tri):
    r2, c2 = _iota((2 * BLK, 2 * BLK), 0), _iota((2 * BLK, 2 * BLK), 1)
    tri2 = (tri(r2, c2) & (r2 // BLK == c2 // BLK)).astype(bf16)
    ri, ci = _iota((BLK, 2 * BLK), 0), _iota((BLK, 2 * BLK), 1)
    strict2 = (ci % BLK) < ri
    head0 = _iota((BLK, BLK), 1) < HD
    return tri2, strict2, head0


def _sb_stack_kv(k_ref, v_ref, kst_ref, vst_ref, head0, nb):
    def one(kb, c):
        krows = pl.ds(pl.multiple_of(kb * BLK, BLK), BLK)
        for p in range(2):
            for src, dst in ((k_ref, kst_ref), (v_ref, vst_ref)):
                t = src[krows, p * BLK:(p + 1) * BLK]
                dst[p, kb] = _mx(jnp.concatenate([jnp.where(head0, t, 0.0), jnp.where(head0, 0.0, t)], axis=0))
        return c
    lax.fori_loop(0, nb, one, 0)


def _sb_load_kv(kst_ref, vst_ref, kb):
    return [kst_ref[p, kb] for p in range(2)], [vst_ref[p, kb] for p in range(2)]


def _two_halves(a, b):
    return jnp.concatenate([jnp.broadcast_to(a, (BLK, BLK)), jnp.broadcast_to(b, (BLK, BLK))], axis=1)


def _half_sums(t):
    return jnp.sum(t[:, :BLK], axis=-1, keepdims=True), jnp.sum(t[:, BLK:], axis=-1, keepdims=True)


def _sb_fwd(proj, B, S, comm=None):
    def body(q_ref, k_ref, v_ref, y_ref, lt_ref, kst_ref, vst_ref):
        ci = _iota((BLK, BLK), 1)
        above2, strict2, head0 = _sb_consts(lambda r, c: r > c)
        _sb_stack_kv(k_ref, v_ref, kst_ref, vst_ref, head0, S // BLK)

        def step(qs, kbs, carry, diag):
            ok = [None if diag else kb >= 0 for kb in kbs]
            kv = [_sb_load_kv(kst_ref, vst_ref, jnp.maximum(kb, 0)) for kb in kbs]
            zs = [[lax.dot_general(qs[p], kks[p], NT, preferred_element_type=f32) for p in range(2)] for kks, _ in kv]
            Ls = [[_log1m(z) for z in zu] for zu in zs]
            if diag:
                Ls = [[jnp.where(strict2, L, 0.0) for L in Lu] for Lu in Ls]
            tails = [[_split_dot(L, above2) for L in Lu] for Lu in Ls]
            carry = list(carry)
            for u in range(len(kbs)):
                for p in range(2):
                    R0, R1, acc = carry[3 * p:3 * p + 3]
                    w = jnp.exp(zs[u][p] + Ls[u][p] + tails[u][p] + _two_halves(R0, R1))
                    s0, s1 = _half_sums(Ls[u][p])
                    if diag:
                        w = jnp.where(strict2, w, 0.0)
                    else:
                        w, s0, s1 = (jnp.where(ok[u], t, 0.0) for t in (w, s0, s1))
                    acc = acc + jnp.dot(_mx(w), kv[u][1][p], preferred_element_type=f32)
                    carry[3 * p:3 * p + 3] = [R0 + s0, R1 + s1, acc]
            return tuple(carry)

        def qblock(n, c):
            qrows = pl.ds(pl.multiple_of(n * BLK, BLK), BLK)
            qs = [_mx(q_ref[qrows, p * BLK:(p + 1) * BLK] * (HD ** -0.5)) for p in range(2)]
            z1, z2 = jnp.zeros((BLK, 1), f32), jnp.zeros((BLK, BLK), f32)
            carry = step(qs, [n], (z1, z1, z2, z1, z1, z2), True)
            trips = (n + SB_UNROLL - 1) // SB_UNROLL

            def live(st):
                worst = jnp.maximum(jnp.maximum(st[1], st[2]), jnp.maximum(st[4], st[5]))
                return (st[0] < trips) & (jnp.max(worst) > SB_CUT)

            def trip(st):
                i = st[0]
                return (i + 1,) + step(qs, [n - 1 - SB_UNROLL * i - u for u in range(SB_UNROLL)], st[1:], False)
            done, *res = lax.while_loop(live, trip, (jnp.int32(0),) + carry)
            lt = jnp.where(ci == SB_HEADS, done.astype(f32), 0.0)
            for p in range(2):
                y_ref[qrows, p * BLK:(p + 1) * BLK] = res[3 * p + 2]
                lt = lt + jnp.where(ci == 2 * p, res[3 * p], 0.0) + jnp.where(ci == 2 * p + 1, res[3 * p + 1], 0.0)
            lt_ref[qrows, :] = lt
            return c
        lax.fori_loop(0, S // BLK, qblock, 0)

    spec = lambda j: pl.BlockSpec((S, GW), lambda b: (b, j))
    c_args, c_in, c_out, c_shapes, aliases, c_scratch = _host_specs(comm, 3, 2)
    step = lambda v: (lambda: pl.program_id(0) == v)
    stacked = pltpu.VMEM((2, S // BLK, 2 * BLK, BLK), _MXU)
    return pl.pallas_call(
        _host(body, 3, 2, 2, comm, step(0), step(B - 1)), grid=(B,),
        in_specs=[spec(5), spec(6), spec(7)] + c_in,
        out_specs=[pl.BlockSpec((S, GW), lambda b: (b, 0)), pl.BlockSpec((S, BLK), lambda b: (b, 0))] + c_out,
        out_shape=[SDS((B * S, GW), f32), SDS((B * S, BLK), f32)] + c_shapes,
        input_output_aliases=aliases, scratch_shapes=[stacked, stacked] + c_scratch,
        name="sb_fwd" if comm is None else "sb_fwd_gather",
        compiler_params=_cp("arbitrary"))(proj, proj, proj, *c_args)


def _sb_bwd(proj, ltot, dy, B, S, comm=None):
    def body(q_ref, k_ref, v_ref, lt_ref, dy_ref, dq_ref, dk_ref, dv_ref, dka_ref, dva_ref, kst_ref, vst_ref):
        ci = _iota((BLK, BLK), 1)
        upto2, strict2, head0 = _sb_consts(lambda r, c: r <= c)
        below2, _, _ = _sb_consts(lambda r, c: r < c)
        dka_ref[...] = jnp.zeros_like(dka_ref)
        dva_ref[...] = jnp.zeros_like(dva_ref)
        _sb_stack_kv(k_ref, v_ref, kst_ref, vst_ref, head0, S // BLK)

        def step(qs, dos, lts, kbs, last, carry, diag):
            U = range(len(kbs))
            ok = [None if diag else kb <= last for kb in kbs]
            kbs = [jnp.minimum(kb, last) for kb in kbs]
            kv = [_sb_load_kv(kst_ref, vst_ref, kb) for kb in kbs]
            zs = [[lax.dot_general(qs[p], kv[u][0][p], NT, preferred_element_type=f32) for p in range(2)] for u in U]
            dws = [[lax.dot_general(dos[p], kv[u][1][p], NT, preferred_element_type=f32) for p in range(2)] for u in U]
            parts = [[_log1m_parts(z) for z in zu] for zu in zs]
            Ls = [[lt[0] for lt in pu] for pu in parts]
            if diag:
                Ls = [[jnp.where(strict2, L, 0.0) for L in Lu] for Lu in Ls]
            pins = [[_split_dot(L, upto2) for L in Lu] for Lu in Ls]
            carry = list(carry)
            ws, das = [], []
            for u in U:
                wu, dau = [], []
                for p in range(2):
                    PL0, PL1 = carry[5 * p], carry[5 * p + 1]
                    tail = _two_halves(lts[2 * p] - PL0, lts[2 * p + 1] - PL1) - pins[u][p]
                    w = jnp.exp(zs[u][p] + Ls[u][p] + tail)
                    l0, l1 = _half_sums(Ls[u][p])
                    if diag:
                        w = jnp.where(strict2, w, 0.0)
                    else:
                        w, l0, l1 = (jnp.where(ok[u], t, 0.0) for t in (w, l0, l1))
                    carry[5 * p], carry[5 * p + 1] = PL0 + l0, PL1 + l1
                    wu.append(w)
                    dau.append(w * dws[u][p])
                ws.append(wu)
                das.append(dau)
            pexs = [[_split_dot(da, below2) for da in dau] for dau in das]
            dzs = []
            for u in U:
                dzu = []
                for p in range(2):
                    dL = _two_halves(carry[5 * p + 2], carry[5 * p + 3]) + pexs[u][p]
                    sg = _sigmoid_from(zs[u][p], parts[u][p][1])
                    dz = das[u][p] * (1.0 - sg) - dL * sg
                    dz = jnp.where(strict2 if diag else ok[u], dz, 0.0)
                    a0, a1 = _half_sums(das[u][p])
                    carry[5 * p + 2], carry[5 * p + 3] = carry[5 * p + 2] + a0, carry[5 * p + 3] + a1
                    dzu.append(_mx(dz))
                dzs.append(dzu)
            dqs = [[jnp.dot(dzs[u][p], kv[u][0][p], preferred_element_type=f32) for p in range(2)] for u in U]
            dks = [[lax.dot_general(dzs[u][p], qs[p], TN, preferred_element_type=f32) for p in range(2)] for u in U]
            dvs = [[lax.dot_general(_mx(ws[u][p]), dos[p], TN, preferred_element_type=f32) for p in range(2)] for u in U]
            for u in U:
                krows = pl.ds(pl.multiple_of(kbs[u] * BLK, BLK), BLK)
                for p in range(2):
                    lanes = slice(p * BLK, (p + 1) * BLK)
                    dka_ref[krows, lanes] += jnp.where(head0, dks[u][p][:BLK], dks[u][p][BLK:])
                    dva_ref[krows, lanes] += jnp.where(head0, dvs[u][p][:BLK], dvs[u][p][BLK:])
                    carry[5 * p + 4] = carry[5 * p + 4] + dqs[u][p]
            return tuple(carry)

        def qblock(n, c):
            qrows = pl.ds(pl.multiple_of(n * BLK, BLK), BLK)
            ltb = lt_ref[qrows, :]
            lts = [jnp.sum(jnp.where(ci == h, ltb, 0.0), axis=-1, keepdims=True) for h in range(4)]
            qs = [_mx(q_ref[qrows, p * BLK:(p + 1) * BLK] * (HD ** -0.5)) for p in range(2)]
            dos = [_mx(dy_ref[qrows, p * BLK:(p + 1) * BLK]) for p in range(2)]
            z1, z2 = jnp.zeros((BLK, 1), f32), jnp.zeros((BLK, BLK), f32)
            done = jnp.max(jnp.where(ci == SB_HEADS, ltb, 0.0)).astype(jnp.int32)
            first = jnp.maximum(n - SB_UNROLL * done, 0)
            carry = lax.fori_loop(
                0, (n - first + SB_UNROLL - 1) // SB_UNROLL,
                lambda i, cr: step(qs, dos, lts, [first + SB_UNROLL * i + u for u in range(SB_UNROLL)], n - 1, cr, False),
                (z1, z1, z1, z1, z2) * 2)
            res = step(qs, dos, lts, [n], n, carry, True)
            for p in range(2):
                dq_ref[qrows, p * BLK:(p + 1) * BLK] = (res[5 * p + 4] * (HD ** -0.5)).astype(bf16)
            return c
        lax.fori_loop(0, S // BLK, qblock, 0)
        dk_ref[...] = dka_ref[...].astype(bf16)
        dv_ref[...] = dva_ref[...].astype(bf16)

    spec = lambda j: pl.BlockSpec((S, GW), lambda b: (b, j))
    o = pl.BlockSpec((S, GW), lambda b: (b, 0))
    c_args, c_in, c_out, c_shapes, aliases, c_scratch = _host_specs(comm, 5, 3)
    step = lambda v: (lambda: pl.program_id(0) == v)
    stacked = pltpu.VMEM((2, S // BLK, 2 * BLK, BLK), _MXU)
    return pl.pallas_call(
        _host(body, 5, 3, 4, comm, step(0), step(B - 1)), grid=(B,),
        in_specs=[spec(5), spec(6), spec(7), pl.BlockSpec((S, BLK), lambda b: (b, 0)), o] + c_in,
        out_specs=[o, o, o] + c_out,
        out_shape=[SDS((B * S, GW), bf16)] * 3 + c_shapes,
        input_output_aliases=aliases,
        scratch_shapes=[pltpu.VMEM((S, GW), f32), pltpu.VMEM((S, GW), f32), stacked, stacked] + c_scratch,
        name="sb_bwd" if comm is None else "sb_bwd_exchange",
        compiler_params=_cp("arbitrary"))(proj, proj, proj, ltot, dy, *c_args)


def _bias_expand(rel_bias_t, bucket):
    n = bucket.shape[1]

    def body(r_ref, b_ref, o_ref):
        onehot = (_iota((N_BUCKETS, n), 0) == b_ref[...]).astype(f32)
        o_ref[...] = jnp.dot(r_ref[...], onehot, precision=HIGHEST, preferred_element_type=f32)
    return pl.pallas_call(body, out_shape=SDS((rel_bias_t.shape[0], n), f32), name="bias_expand",
                          compiler_params=_cp())(rel_bias_t, bucket)


def _bias_reduce(dbias, bucket):
    n = bucket.shape[1]

    def body(*refs):
        b_ref, g_ref = refs[-2], refs[-1]
        d = refs[0][...]
        for r in refs[1:-2]:
            d = d + r[...]
        onehot = (_iota((N_BUCKETS, n), 0) == b_ref[...]).astype(f32)
        g_ref[...] = lax.dot_general(d, onehot, NT, precision=HIGHEST, preferred_element_type=f32)
    return pl.pallas_call(body, out_shape=SDS((dbias[0].shape[0], N_BUCKETS), f32), name="bias_reduce",
                          compiler_params=_cp())(*dbias, bucket)


def _adamw(w, g, m, v, tr, name):
    R, C = w.shape

    def body(w_ref, g_ref, m_ref, v_ref, d_ref, m2_ref, v2_ref):
        gv = g_ref[...]
        m2 = ADAM_B1 * m_ref[...] + (1.0 - ADAM_B1) * gv
        v2 = ADAM_B2 * v_ref[...] + (1.0 - ADAM_B2) * (gv * gv)
        m_hat = m2 / (1.0 - ADAM_B1 ** ADAM_STEP)
        v_hat = v2 / (1.0 - ADAM_B2 ** ADAM_STEP)
        d_ref[...] = -ADAM_LR * (m_hat / (jnp.sqrt(v_hat) + ADAM_EPS) + ADAM_WD * w_ref[...])
        m2_ref[...] = m2
        v2_ref[...] = v2

    spec = pl.BlockSpec((tr, C), lambda i: (i, 0))
    return pl.pallas_call(
        body, grid=(R // tr,), in_specs=[spec] * 4, out_specs=[spec] * 3,
        out_shape=[SDS((R, C), f32)] * 3, name=name, compiler_params=_cp("parallel"))(w, g, m, v)


ANY = pl.BlockSpec(memory_space=pl.ANY)


def _place():
    x, y, c = lax.axis_index("x"), lax.axis_index("y"), lax.axis_index("c")
    chips = [(1 - x, y), (x, 1 - y), (1 - x, 1 - y)]
    return x, y, c, chips


def _cast_slots(w, kidx):
    L, a, b = w.shape
    ta = a // 2

    def body(k_ref, *refs):
        for l in range(L):
            refs[L + l][0] = refs[l][0].astype(bf16)

    return pl.pallas_call(
        body,
        grid_spec=pltpu.PrefetchScalarGridSpec(
            num_scalar_prefetch=1, grid=(a // ta,),
            in_specs=[pl.BlockSpec((1, ta, b), functools.partial(lambda i, k_ref, l: (l, i, 0), l=l)) for l in range(L)],
            out_specs=[pl.BlockSpec((1, ta, b), lambda i, k_ref: (k_ref[0], i, 0)) for _ in range(L)]),
        out_shape=[SDS((N_CHIPS, a, b), bf16)] * L,
        name="cast_slots", compiler_params=_cp("parallel"))(kidx, *([w] * L))


class _GatherComm:
    def __init__(self, bufs):
        self.inputs = list(bufs)
        self.out_shape = [SDS(b.shape, b.dtype) for b in bufs]
        self.aliased = True
        self.scratch = [pltpu.SemaphoreType.DMA((3 * len(bufs),))] * 4

    def _copies(self, i_refs, o_refs, sems):
        send1, recv1, send2, recv2 = sems
        x, y, c, chips = _place()
        k = 2 * x + y
        first, got1, second, got2 = [], [], [], []
        for i, buf in enumerate(self.inputs):
            h = buf.shape[1] // 2
            mine, theirs = pl.ds(c * h, h), pl.ds((1 - c) * h, h)
            for j, (cx, cy) in enumerate(chips):
                s = 3 * i + j
                first.append(pltpu.make_async_remote_copy(
                    src_ref=i_refs[i].at[k, mine], dst_ref=o_refs[i].at[k, mine], send_sem=send1.at[s],
                    recv_sem=recv1.at[s], device_id=(cx, cy, c), device_id_type=MESH))
                a = o_refs[i].at[2 * cx + cy, mine]
                got1.append(pltpu.make_async_remote_copy(
                    src_ref=a, dst_ref=a, send_sem=send1.at[s], recv_sem=recv1.at[s],
                    device_id=(cx, cy, c), device_id_type=MESH))
                second.append(pltpu.make_async_remote_copy(
                    src_ref=a, dst_ref=a, send_sem=send2.at[s], recv_sem=recv2.at[s],
                    device_id=(x, y, 1 - c), device_id_type=MESH))
                b = o_refs[i].at[2 * cx + cy, theirs]
                got2.append(pltpu.make_async_remote_copy(
                    src_ref=b, dst_ref=b, send_sem=send2.at[s], recv_sem=recv2.at[s],
                    device_id=(x, y, 1 - c), device_id_type=MESH))
        return first, got1, second, got2

    def start(self, i_refs, o_refs, sems):
        for cp in self._copies(i_refs, o_refs, sems)[0]:
            cp.start()

    def finish(self, i_refs, o_refs, sems):
        first, got1, second, got2 = self._copies(i_refs, o_refs, sems)
        for g, cp in zip(got1, second):
            g.wait_recv()
            cp.start()
        for g in got2:
            g.wait_recv()
        for cp in first + second:
            cp.wait_send()


class _PairExchangeComm:
    def __init__(self, gs):
        self.inputs = list(gs)
        self.out_shape = [SDS((g.shape[0], g.shape[1] // 2, g.shape[2]), g.dtype) for g in gs]
        self.aliased = False
        self.scratch = [pltpu.SemaphoreType.DMA((len(gs),))] * 2

    def _copies(self, i_refs, o_refs, sems):
        send, recv = sems
        x, y, c, _ = _place()
        cps = []
        for i, g in enumerate(self.inputs):
            h = g.shape[1] // 2
            cps.append(pltpu.make_async_remote_copy(
                src_ref=i_refs[i].at[:, pl.ds((1 - c) * h, h)], dst_ref=o_refs[i], send_sem=send.at[i], recv_sem=recv.at[i],
                device_id=(x, y, 1 - c), device_id_type=MESH))
        return cps

    def start(self, i_refs, o_refs, sems):
        for cp in self._copies(i_refs, o_refs, sems):
            cp.start()

    def finish(self, i_refs, o_refs, sems):
        for cp in self._copies(i_refs, o_refs, sems):
            cp.wait()


class _ChipExchangeComm:
    def __init__(self, qs):
        self.inputs = list(qs)
        self.out_shape = [SDS(q.shape, q.dtype) for q in qs]
        self.aliased = False
        self.scratch = [pltpu.SemaphoreType.DMA((3 * len(qs),))] * 2

    def _copies(self, i_refs, o_refs, sems):
        send, recv = sems
        x, y, c, chips = _place()
        k = 2 * x + y
        cps, got = [], []
        for i in range(len(self.inputs)):
            for j, (cx, cy) in enumerate(chips):
                s = 3 * i + j
                cps.append(pltpu.make_async_remote_copy(
                    src_ref=i_refs[i].at[2 * cx + cy], dst_ref=o_refs[i].at[k], send_sem=send.at[s],
                    recv_sem=recv.at[s], device_id=(cx, cy, c), device_id_type=MESH))
                a = o_refs[i].at[2 * cx + cy]
                got.append(pltpu.make_async_remote_copy(
                    src_ref=a, dst_ref=a, send_sem=send.at[s], recv_sem=recv.at[s],
                    device_id=(cx, cy, c), device_id_type=MESH))
        return cps, got

    def start(self, i_refs, o_refs, sems):
        for cp in self._copies(i_refs, o_refs, sems)[0]:
            cp.start()

    def finish(self, i_refs, o_refs, sems):
        cps, got = self._copies(i_refs, o_refs, sems)
        for g in got:
            g.wait_recv()
        for cp in cps:
            cp.wait_send()


def _comm_only(comm, name):
    n = len(comm.inputs)

    def body(*refs):
        i_refs, o_refs, sems = refs[:n], refs[n:n + len(comm.out_shape)], refs[n + len(comm.out_shape):]
        comm.start(i_refs, o_refs, sems)
        comm.finish(i_refs, o_refs, sems)

    return pl.pallas_call(
        body, out_shape=comm.out_shape, in_specs=[ANY] * n, out_specs=[ANY] * len(comm.out_shape),
        input_output_aliases={i: i for i in range(n)} if comm.aliased else {},
        scratch_shapes=comm.scratch, name=name,
        compiler_params=pltpu.CompilerParams(has_side_effects=True))(*comm.inputs)


def _host(body, n_in, n_out, n_scratch, comm, first, last):
    if comm is None:
        return body
    ci, co = len(comm.inputs), len(comm.out_shape)

    def wrapped(*refs):
        o = 0
        parts = []
        for n in (n_in, ci, n_out, co, n_scratch):
            parts.append(refs[o:o + n])
            o += n
        hin, cin, hout, cout, hs = parts
        sems = refs[o:]

        @pl.when(first())
        def _():
            comm.start(cin, cout, sems)
        body(*hin, *hout, *hs)

        @pl.when(last())
        def _():
            comm.finish(cin, cout, sems)
    return wrapped


def _host_specs(comm, n_in, n_out):
    if comm is None:
        return [], [], [], [], {}, []
    ci, co = len(comm.inputs), len(comm.out_shape)
    aliases = {n_in + i: n_out + i for i in range(ci)} if comm.aliased else {}
    return comm.inputs, [ANY] * ci, [ANY] * co, comm.out_shape, aliases, comm.scratch


def _pair_add(g, r, cidx, name):
    ns, a, b = g.shape
    h = a // 2
    th = h if h * b * 4 <= 4 * 1024 * 1024 else h // 2

    def body(c_ref, g_ref, r_ref, qf_ref, qb_ref):
        q = g_ref[...] + r_ref[...]
        qf_ref[...] = q
        qb_ref[...] = q.astype(bf16)

    nb = h // th
    spec = pl.BlockSpec((1, th, b), lambda s, i, c_ref: (s, i, 0))
    return pl.pallas_call(
        body,
        grid_spec=pltpu.PrefetchScalarGridSpec(
            num_scalar_prefetch=1, grid=(ns, nb),
            in_specs=[pl.BlockSpec((1, th, b), lambda s, i, c_ref: (s, c_ref[0] * nb + i, 0)), spec],
            out_specs=[spec, spec]),
        out_shape=[SDS((ns, h, b), f32), SDS((ns, h, b), bf16)],
        name=name, compiler_params=_cp("parallel", "parallel"))(cidx, g, r)


def _chip_add(qf, r2, idx, prev, L, name):
    ns, h, b = r2.shape
    th = h if h * b * 4 <= 4 * 1024 * 1024 else h // 2
    nb = h // th

    def body(s_ref, qf_ref, r1_ref, r2_ref, r3_ref, *rest):
        o_ref = rest[-1]
        o_ref[0] = qf_ref[0] + r1_ref[0].astype(f32) + r2_ref[0].astype(f32) + r3_ref[0].astype(f32)

    other = lambda d: pl.BlockSpec((1, th, b), lambda i, s_ref: ((s_ref[0] + d) % ns, i, 0))
    in_specs = [pl.BlockSpec((1, th, b), lambda i, s_ref: (s_ref[0], i, 0)), other(1), other(2), other(3)]
    args = [idx, qf, r2, r2, r2]
    aliases = {}
    if prev is not None:
        in_specs.append(ANY)
        args.append(prev)
        aliases = {5: 0}
    return pl.pallas_call(
        body,
        grid_spec=pltpu.PrefetchScalarGridSpec(
            num_scalar_prefetch=1, grid=(nb,), in_specs=in_specs,
            out_specs=pl.BlockSpec((1, th, b), lambda i, s_ref: (s_ref[2], s_ref[1] * nb + i, 0))),
        out_shape=SDS((L, 2 * h, b), f32), input_output_aliases=aliases,
        name=name, compiler_params=_cp("arbitrary"))(*args)


def _pair_share(gs, hs):
    n = len(gs)
    L = gs[0].shape[0]

    def body(*refs):
        i_refs, o_refs = refs[:n], refs[n:2 * n]
        send, recv = refs[2 * n:]
        x, y, c, _ = _place()
        cps = []
        for i in range(n):
            for l in range(L):
                mine = pl.ds(c * hs[i], hs[i])
                cp = pltpu.make_async_remote_copy(
                    src_ref=i_refs[i].at[l, mine], dst_ref=o_refs[i].at[l, mine], send_sem=send.at[i * L + l],
                    recv_sem=recv.at[i * L + l], device_id=(x, y, 1 - c), device_id_type=MESH)
                cp.start()
                cps.append(cp)
        for i in range(n):
            for l in range(L):
                got = o_refs[i].at[l, pl.ds((1 - c) * hs[i], hs[i])]
                pltpu.make_async_remote_copy(
                    src_ref=got, dst_ref=got, send_sem=send.at[i * L + l], recv_sem=recv.at[i * L + l],
                    device_id=(x, y, 1 - c), device_id_type=MESH).wait_recv()
        for cp in cps:
            cp.wait_send()

    return pl.pallas_call(
        body, out_shape=[SDS(g.shape, g.dtype) for g in gs], in_specs=[ANY] * n, out_specs=[ANY] * n,
        input_output_aliases={i: i for i in range(n)},
        scratch_shapes=[pltpu.SemaphoreType.DMA((n * L,))] * 2,
        name="grad_pair_share", compiler_params=pltpu.CompilerParams(has_side_effects=True))(*gs)


class _SwapComm:
    def __init__(self, arrays):
        self.inputs = list(arrays)
        self.out_shape = [SDS(a.shape, a.dtype) for a in arrays]
        self.aliased = False
        self.scratch = [pltpu.SemaphoreType.DMA((len(arrays),))] * 2

    def _copies(self, i_refs, o_refs, sems):
        send, recv = sems
        x, y, c, _ = _place()
        return [pltpu.make_async_remote_copy(
            src_ref=i_refs[i], dst_ref=o_refs[i], send_sem=send.at[i], recv_sem=recv.at[i],
            device_id=(x, y, 1 - c), device_id_type=MESH) for i in range(len(self.inputs))]

    def start(self, i_refs, o_refs, sems):
        for cp in self._copies(i_refs, o_refs, sems):
            cp.start()

    def finish(self, i_refs, o_refs, sems):
        for cp in self._copies(i_refs, o_refs, sems):
            cp.wait()


class _SlotShareComm:
    def __init__(self, bufs):
        self.inputs = list(bufs)
        self.out_shape = [SDS(b.shape, b.dtype) for b in bufs]
        self.aliased = True
        self.scratch = [pltpu.SemaphoreType.DMA((3 * len(bufs),))] * 2

    def _copies(self, i_refs, o_refs, sems):
        send, recv = sems
        x, y, c, chips = _place()
        k = 2 * x + y
        cps, got = [], []
        for i in range(len(self.inputs)):
            for j, (cx, cy) in enumerate(chips):
                s = 3 * i + j
                cps.append(pltpu.make_async_remote_copy(
                    src_ref=i_refs[i].at[k], dst_ref=o_refs[i].at[k], send_sem=send.at[s], recv_sem=recv.at[s],
                    device_id=(cx, cy, c), device_id_type=MESH))
                a = o_refs[i].at[2 * cx + cy]
                got.append(pltpu.make_async_remote_copy(
                    src_ref=a, dst_ref=a, send_sem=send.at[s], recv_sem=recv.at[s],
                    device_id=(cx, cy, c), device_id_type=MESH))
        return cps, got

    def start(self, i_refs, o_refs, sems):
        for cp in self._copies(i_refs, o_refs, sems)[0]:
            cp.start()

    def finish(self, i_refs, o_refs, sems):
        cps, got = self._copies(i_refs, o_refs, sems)
        for g in got:
            g.wait_recv()
        for cp in cps:
            cp.wait_send()


def _pair_sum_slot(mine, theirs, kidx):
    R, C = mine.shape

    def body(k_ref, a_ref, b_ref, o_ref):
        o_ref[0] = a_ref[...] + b_ref[...]

    spec = pl.BlockSpec((R, C), lambda i, k_ref: (0, 0))
    return pl.pallas_call(
        body,
        grid_spec=pltpu.PrefetchScalarGridSpec(
            num_scalar_prefetch=1, grid=(1,), in_specs=[spec, spec],
            out_specs=pl.BlockSpec((1, R, C), lambda i, k_ref: (k_ref[0], 0, 0))),
        out_shape=SDS((N_CHIPS, R, C), f32), name="small_pair_sum", compiler_params=_cp("arbitrary"))(kidx, mine, theirs)


def _small_sum(g):
    n, R, C = g.shape

    def body(g_ref, o_ref):
        acc = g_ref[0]
        for j in range(1, n):
            acc = acc + g_ref[j]
        o_ref[...] = acc
    return pl.pallas_call(body, out_shape=SDS((R, C), f32), name="small_sum", compiler_params=_cp())(g)


PACK_COLS = 1024


def _rows_of(shape):
    n = int(np.prod(shape)) if len(shape) else 1
    return -(-n // (8 * PACK_COLS)) * 8


def _pack(parts):
    blocks = []
    for p in parts:
        flat = p.reshape(-1)
        r = _rows_of(p.shape)
        blocks.append(jnp.pad(flat, (0, r * PACK_COLS - flat.shape[0])).reshape(r, PACK_COLS))
    return jnp.concatenate(blocks, axis=0)


def _unpack(buf, shapes):
    out, off = [], 0
    for s in shapes:
        n = int(np.prod(s)) if len(s) else 1
        r = _rows_of(s)
        out.append(buf[off:off + r].reshape(-1)[:n].reshape(s))
        off += r
    return out


def _block_diag(w):
    g, a, _ = w.shape
    out = jnp.zeros((g * a, g * a), w.dtype)
    for i in range(g):
        out = lax.dynamic_update_slice(out, w[i], (i * a, i * a))
    return out


def kernel(x, w_in, w_out, sgu_w, sgu_b, pool_w, pool_scale, swa_sinks, rel_bias, mix_out_gain, norm_mix, norm_ffn, w_gate_up, w_down, norm_final, loss_target, m_w_in, m_w_out, m_sgu_w, m_sgu_b, m_pool_w, m_pool_scale, m_swa_sinks, m_rel_bias, m_mix_out_gain, m_norm_mix, m_norm_ffn, m_w_gate_up, m_w_down, m_norm_final, v_w_in, v_w_out, v_sgu_w, v_sgu_b, v_pool_w, v_pool_scale, v_swa_sinks, v_rel_bias, v_mix_out_gain, v_norm_mix, v_norm_ffn, v_w_gate_up, v_w_down, v_norm_final):
    B, S, D = x.shape
    T = B * S
    L = w_in.shape[0]
    tm = min(512, T)
    F = w_down.shape[1] * N_CHIPS
    xi, yi, ci = lax.axis_index("x"), lax.axis_index("y"), lax.axis_index("c")
    cidx = jnp.reshape(ci, (1,)).astype(jnp.int32)
    kidx = jnp.reshape(2 * xi + yi, (1,)).astype(jnp.int32)

    big = [w_in, w_out, w_gate_up, w_down]
    slots = [_cast_slots(w, kidx) for w in big]
    Win, rest = [None] * L, [None] * L
    Win[0], = _comm_only(_GatherComm([slots[0][0]]), "gather_weights")

    bucket = jnp.asarray(_t5_bucket_table().reshape(1, -1))
    bias_tab = _bias_expand(rel_bias.T, bucket).reshape(4, BLK, 2 * BLK)

    row = lambda v: v.reshape(1, -1)
    xc = x.reshape(T, D)
    tgt = loss_target.reshape(T, D)
    saved = []
    Wo, Wgu, Wd = ([None] * L for _ in range(3))
    for l in range(L):
        h1, proj = _norm_mm(xc, row(norm_mix[l]), Win[l], tm)
        bexp = jnp.repeat(sgu_b[l].T, HD, axis=1)
        wbd = _block_diag(pool_w[l])
        sk = jnp.broadcast_to(swa_sinks[l][:, None, None], (4, 1, BLK))
        ya = _sgu_fwd(proj, sgu_w[l], bexp, B, S)
        yb = _pool_fwd(proj, wbd, row(pool_scale[l]), B, S)
        yc = _swa_fwd(proj, sk, bias_tab, B, S)
        yd, lt, *rest[l] = _sb_fwd(proj, B, S, _GatherComm([slots[pi][l] for pi in (1, 2, 3)]))
        Wo[l], Wgu[l], Wd[l] = rest[l][0].reshape(D, D), rest[l][1], rest[l][2].reshape(F, D)
        ys = (ya, yb, yc, yd)
        ycn, x1 = _gnorm_mm_res(ys, row(mix_out_gain[l]), Wo[l], xc, tm)
        if l + 1 < L:
            h2, gu, act, Win[l + 1] = _norm_mm_swiglu(x1, row(norm_ffn[l]), Wgu[l], tm, _GatherComm([slots[0][l + 1]]))
        else:
            h2, gu, act = _norm_mm_swiglu(x1, row(norm_ffn[l]), Wgu[l], tm)
        x2 = _mm_res(act, Wd[l], x1, tm)
        saved.append((xc, h1, proj, bexp, wbd, sk, ys, lt, ycn, x1, h2, gu, act))
        xc = x2

    dx, g_final, loss_v = _final_loss(xc, row(norm_final), tgt, tm)

    tk = min(T, 2048)
    gW = [[None] * L for _ in range(4)]
    g_sgu_w, g_sgu_b, g_pool_w, g_pool_scale, g_sinks, g_bias = ([None] * L for _ in range(6))
    g_out_gain, g_mix, g_ffn = ([None] * L for _ in range(3))
    reduced = [None] * 4
    fresh, summed = [], []

    def pair_comm():
        return _PairExchangeComm([gW[pi][l] for pi, l in fresh]) if fresh else None

    def after_pair(r1):
        for (pi, l), r in zip(fresh, r1):
            qf, qb = _pair_add(gW[pi][l], r, cidx, "grad_pair_add")
            summed.append((pi, l, qf, qb))
        fresh.clear()

    def chip_comm():
        return _ChipExchangeComm([s[3] for s in summed]) if summed else None

    def after_chip(r2):
        for (pi, l, qf, _), r in zip(summed, r2):
            idx = jnp.stack([2 * xi + yi, ci, jnp.int32(l)]).astype(jnp.int32)
            reduced[pi] = _chip_add(qf, r, idx, reduced[pi], L, "grad_chip_add")
        summed.clear()

    for l in reversed(range(L)):
        x0, h1, proj, bexp, wbd, sk, ys, lt, ycn, x1, h2, gu, act = saved[l]
        dgu = _dact(dx, Wd[l], gu, tm)
        gW[3][l] = _dw(act, dx, lambda t, s: (t, 0), D, 1, F // 2, tk // 2, "dw_down").reshape(N_CHIPS, F // N_CHIPS, D)
        gW[2][l] = _dw(h2, dgu, lambda t, s: (s // 2, t, s % 2), F // 2, N_CHIPS, D, tk, "dw_gate_up")
        fresh += [(3, l), (2, l)]
        comm = pair_comm()
        dx1, g_ffn[l], *r1 = _dx_norm_bwd(dgu, Wgu[l], x1, row(norm_ffn[l]), dx, tm, "dx_ffn_exchange", comm)
        after_pair(r1)
        gW[1][l] = _dw(ycn, dx1, lambda t, s: (t, 0), D, 1, D, tk, "dw_out").reshape(N_CHIPS, D // N_CHIPS, D)
        fresh.append((1, l))
        dya, dyb, dyc, dyd, g_out_gain[l] = _dycat(dx1, Wo[l], ys, row(mix_out_gain[l]), tm)
        dpa, g_sgu_w[l], dbf = _sgu_bwd(proj, sgu_w[l], bexp, dya, B, S)
        g_sgu_b[l] = dbf[:, ::HD].T
        dpb, dwbd, dsc = _pool_bwd(proj, wbd, row(pool_scale[l]), dyb, B, S)
        npg = len(POOL_WINDOWS)
        g_pool_w[l] = jnp.stack([dwbd[i * HD:(i + 1) * HD, i * HD:(i + 1) * HD] for i in range(npg)])
        g_pool_scale[l] = dsc[0]
        dcq, dckv, dsk, g_bias[l], *r1 = _swa_bwd(proj, sk, bias_tab, dyc, B, S, pair_comm())
        after_pair(r1)
        g_sinks[l] = dsk[:, 0, 0] * float(BLK)
        ddq, ddk, ddv, *r2 = _sb_bwd(proj, lt, dyd, B, S, chip_comm())
        after_chip(r2)
        dproj = jnp.concatenate([dpa, dpb, dcq, dckv, ddq, ddk, ddv], axis=1)
        gW[0][l] = _dw(h1, dproj, lambda t, s: (t, s), w_in.shape[2], N_CHIPS, D, tk, "dw_in")
        fresh.append((0, l))
        dx, g_mix[l] = _dx_norm_bwd(dproj, Win[l], x0, row(norm_mix[l]), dx1, tm, "dx_mix")
    grad_x = dx.reshape(B, S, D)

    after_pair(_comm_only(pair_comm(), "grad_pair_exchange"))
    after_chip(_comm_only(chip_comm(), "grad_chip_exchange"))
    g_big = _pair_share(reduced, [g.shape[1] // 2 for g in reduced])

    g_rel_bias = _bias_reduce([g.reshape(4, -1) for g in g_bias], bucket).T
    small_g = [jnp.stack(g_sgu_w), jnp.stack(g_sgu_b), jnp.stack(g_pool_w), jnp.stack(g_pool_scale), jnp.stack(g_sinks),
               g_rel_bias, jnp.concatenate(g_out_gain), jnp.concatenate(g_mix), jnp.concatenate(g_ffn), g_final[0]]
    small_w = [sgu_w, sgu_b, pool_w, pool_scale, swa_sinks, rel_bias, mix_out_gain, norm_mix, norm_ffn, norm_final]
    small_m = [m_sgu_w, m_sgu_b, m_pool_w, m_pool_scale, m_swa_sinks, m_rel_bias, m_mix_out_gain, m_norm_mix, m_norm_ffn, m_norm_final]
    small_v = [v_sgu_w, v_sgu_b, v_pool_w, v_pool_scale, v_swa_sinks, v_rel_bias, v_mix_out_gain, v_norm_mix, v_norm_ffn, v_norm_final]
    shapes = [w.shape for w in small_w]
    mine = _pack(small_g + [loss_v[0, 0:1]])
    theirs, = _comm_only(_SwapComm([mine]), "small_pair_swap")
    shared, = _comm_only(_SlotShareComm([_pair_sum_slot(mine, theirs, kidx)]), "small_chip_share")
    packed = _small_sum(shared)
    *g_small, loss = _unpack(packed, shapes + [()])
    g_small_packed = _pack(g_small)
    ds, ms, vs = _adamw(_pack(small_w), g_small_packed, _pack(small_m), _pack(small_v), g_small_packed.shape[0], "adamw_small")
    d_small, m_small, v_small = _unpack(ds, shapes), _unpack(ms, shapes), _unpack(vs, shapes)

    big_m = [m_w_in, m_w_out, m_w_gate_up, m_w_down]
    big_v = [v_w_in, v_w_out, v_w_gate_up, v_w_down]
    d_big, m_big, v_big = [], [], []
    for w, g, m, v in zip(big, g_big, big_m, big_v):
        two = lambda a: a.reshape(-1, a.shape[-1])
        rows = two(w).shape[0]
        d2, m2, v2 = _adamw(two(w), two(g), two(m), two(v), rows // 8 if rows >= 2048 else rows, "adamw_big")
        d_big.append(d2.reshape(w.shape))
        m_big.append(m2.reshape(w.shape))
        v_big.append(v2.reshape(w.shape))

    def order(bigs, smalls):
        return [bigs[0], bigs[1]] + list(smalls[0:9]) + [bigs[2], bigs[3], smalls[9]]

    return (loss, grad_x, *order(g_big, g_small), *order(d_big, d_small), *order(m_big, m_small), *order(v_big, v_small))
```

```python
import functools

import numpy as np
import jax
import jax.numpy as jnp
from jax import lax
from jax.experimental import pallas as pl
from jax.experimental.pallas import tpu as pltpu

f32 = jnp.float32
bf16 = jnp.bfloat16
_MXU = jnp.bfloat16

EPS = 1e-6
HD = 64
GW = 256
BLK = 128
SB_UNROLL = 2
SB_HEADS = 4
SB_CUT = -110.0
POOL_WINDOWS = (2, 4, 8, 16)
N_BUCKETS = 32
MAX_DISTANCE = 128
N_CHIPS = 4
N_DEV = 8
VMEM_BYTES_V7X = 64 * 1024 * 1024
VMEM_LIMIT = 48 * 1024 * 1024

ADAM_LR = 0.001
ADAM_B1 = 0.9
ADAM_B2 = 0.999
ADAM_EPS = 1e-08
ADAM_WD = 0.01
ADAM_STEP = 10

SDS = jax.ShapeDtypeStruct
MESH = pl.DeviceIdType.MESH
HIGHEST = lax.Precision.HIGHEST
RESIDENT = pl.Buffered(1)
NT = (((1,), (1,)), ((), ()))
TN = (((0,), (0,)), ((), ()))


def _cp(*sem):
    return pltpu.CompilerParams(dimension_semantics=sem if sem else None, vmem_limit_bytes=VMEM_LIMIT)


def _mx(v):
    return v.astype(_MXU)


def _iota(shape, dim):
    return lax.broadcasted_iota(jnp.int32, shape, dim)


def _split_dot(a, tri):
    hi = a.astype(bf16)
    lo = (a - hi.astype(f32)).astype(bf16)
    return jnp.dot(hi, tri, preferred_element_type=f32) + jnp.dot(lo, tri, preferred_element_type=f32)


def _rms(xv):
    return lax.rsqrt(jnp.mean(xv * xv, axis=-1, keepdims=True) + EPS)


def _hosted_call(body, steps, in_specs, out_specs, out_shape, scratch, args, name, comm):
    n_in, n_out = len(in_specs), len(out_specs)
    c_args, c_in, c_out, c_shapes, aliases, c_scratch = _host_specs(comm, n_in, n_out)
    step = lambda v: (lambda: pl.program_id(0) == v)
    return pl.pallas_call(
        _host(body, n_in, n_out, len(scratch), comm, step(0), step(steps - 1)), grid=(steps,),
        in_specs=list(in_specs) + c_in, out_specs=list(out_specs) + c_out, out_shape=list(out_shape) + c_shapes,
        input_output_aliases=aliases, scratch_shapes=list(scratch) + c_scratch,
        name=name if comm is None else name + "_comm", compiler_params=_cp("arbitrary"))(*args, *c_args)


def _norm_mm(x, gain, w, tm, comm=None):
    T, D = x.shape
    NS, _, ns = w.shape

    def body(x_ref, g_ref, w_ref, h_ref, o_ref):
        xv = x_ref[...]
        h = (xv * _rms(xv) * g_ref[...]).astype(bf16)
        h_ref[...] = h
        for s in range(NS):
            o_ref[:, s * ns:(s + 1) * ns] = jnp.dot(_mx(h), w_ref[s], preferred_element_type=f32)

    return _hosted_call(
        body, T // tm,
        [pl.BlockSpec((tm, D), lambda i: (i, 0)),
         pl.BlockSpec((1, D), lambda i: (0, 0)),
         pl.BlockSpec((NS, D, ns), lambda i: (0, 0, 0), pipeline_mode=RESIDENT)],
        [pl.BlockSpec((tm, D), lambda i: (i, 0)), pl.BlockSpec((tm, NS * ns), lambda i: (i, 0))],
        [SDS((T, D), bf16), SDS((T, NS * ns), f32)], [], (x, gain, w), "norm_mm_in", comm)


def _norm_mm_swiglu(x, gain, w, tm, comm=None):
    T, D = x.shape
    NS, _, ns = w.shape
    half = NS // 2

    def body(x_ref, g_ref, w_ref, h_ref, gu_ref, a_ref):
        xv = x_ref[...]
        hb = (xv * _rms(xv) * g_ref[...]).astype(bf16)
        h_ref[...] = hb
        h = _mx(hb)
        for s in range(half):
            cols = slice(s * ns, (s + 1) * ns)
            g = jnp.dot(h, w_ref[s], preferred_element_type=f32)
            u = jnp.dot(h, w_ref[s + half], preferred_element_type=f32)
            gu_ref[0, :, cols] = g.astype(bf16)
            gu_ref[1, :, cols] = u.astype(bf16)
            a_ref[:, cols] = (jax.nn.silu(g) * u).astype(bf16)

    c_args, c_in, c_out, c_shapes, aliases, c_scratch = _host_specs(comm, 3, 3)
    step = lambda v: (lambda: pl.program_id(0) == v)
    return pl.pallas_call(
        _host(body, 3, 3, 0, comm, step(0), step(T // tm - 1)), grid=(T // tm,),
        in_specs=[pl.BlockSpec((tm, D), lambda i: (i, 0)),
                  pl.BlockSpec((1, D), lambda i: (0, 0)),
                  pl.BlockSpec((NS, D, ns), lambda i: (0, 0, 0), pipeline_mode=RESIDENT)] + c_in,
        out_specs=[pl.BlockSpec((tm, D), lambda i: (i, 0)),
                   pl.BlockSpec((2, tm, half * ns), lambda i: (0, i, 0)),
                   pl.BlockSpec((tm, half * ns), lambda i: (i, 0))] + c_out,
        out_shape=[SDS((T, D), bf16), SDS((2, T, half * ns), bf16), SDS((T, half * ns), bf16)] + c_shapes,
        input_output_aliases=aliases, scratch_shapes=c_scratch,
        name="norm_mm_swiglu" if comm is None else "norm_mm_swiglu_gather",
        compiler_params=_cp("arbitrary"))(x, gain, w, *c_args)


def _gnorm_mm_res(ys, gain, w, x, tm):
    T, D = x.shape

    def body(ya, yb, yc, yd, g_ref, w_ref, x_ref, yn_ref, o_ref):
        parts = []
        for m, r in enumerate((ya, yb, yc, yd)):
            y = r[...]
            parts.append((y * _rms(y) * g_ref[:, m * GW:(m + 1) * GW]).astype(bf16))
        yn = jnp.concatenate(parts, axis=1)
        yn_ref[...] = yn
        o_ref[...] = x_ref[...] + jnp.dot(_mx(yn), w_ref[...], preferred_element_type=f32)

    yspec = pl.BlockSpec((tm, GW), lambda i: (i, 0))
    return pl.pallas_call(
        body, grid=(T // tm,),
        in_specs=[yspec, yspec, yspec, yspec,
                  pl.BlockSpec((1, D), lambda i: (0, 0)),
                  pl.BlockSpec((D, D), lambda i: (0, 0)),
                  pl.BlockSpec((tm, D), lambda i: (i, 0))],
        out_specs=[pl.BlockSpec((tm, D), lambda i: (i, 0)), pl.BlockSpec((tm, D), lambda i: (i, 0))],
        out_shape=[SDS((T, D), bf16), SDS((T, D), f32)],
        name="gnorm_mm_res", compiler_params=_cp("parallel"))(*ys, gain, w, x)


def _mm_res(a, w, x, tm, comm=None):
    T, D = x.shape
    K = a.shape[1]

    def body(a_ref, w_ref, x_ref, o_ref):
        o_ref[...] = x_ref[...] + jnp.dot(_mx(a_ref[...]), w_ref[...], preferred_element_type=f32)

    return _hosted_call(
        body, T // tm,
        [pl.BlockSpec((tm, K), lambda i: (i, 0)),
         pl.BlockSpec((K, D), lambda i: (0, 0), pipeline_mode=RESIDENT),
         pl.BlockSpec((tm, D), lambda i: (i, 0))],
        [pl.BlockSpec((tm, D), lambda i: (i, 0))], [SDS((T, D), f32)], [], (a, w, x), "mm_res_down", comm)


def _final_loss(x, gain, tgt, tm):
    T, D = x.shape

    def body(x_ref, g_ref, t_ref, dx_ref, dg_ref, l_ref):
        @pl.when(pl.program_id(0) == 0)
        def _():
            dg_ref[...] = jnp.zeros_like(dg_ref)
            l_ref[...] = jnp.zeros_like(l_ref)
        xv = x_ref[...]
        g = g_ref[...]
        r = _rms(xv)
        xh = xv * r
        err = xh * g - t_ref[...]
        l_ref[...] += 0.5 * jnp.sum(jnp.mean(err * err, axis=-1, keepdims=True), axis=0, keepdims=True)
        dy = err * (1.0 / D)
        dg_ref[...] += jnp.sum(dy * xh, axis=0, keepdims=True)
        dxh = dy * g
        dx_ref[...] = r * (dxh - xh * jnp.mean(dxh * xh, axis=-1, keepdims=True))

    return pl.pallas_call(
        body, grid=(T // tm,),
        in_specs=[pl.BlockSpec((tm, D), lambda i: (i, 0)),
                  pl.BlockSpec((1, D), lambda i: (0, 0)),
                  pl.BlockSpec((tm, D), lambda i: (i, 0))],
        out_specs=[pl.BlockSpec((tm, D), lambda i: (i, 0)),
                   pl.BlockSpec((1, D), lambda i: (0, 0)),
                   pl.BlockSpec((1, BLK), lambda i: (0, 0))],
        out_shape=[SDS((T, D), f32), SDS((1, D), f32), SDS((1, BLK), f32)],
        name="final_loss", compiler_params=_cp("arbitrary"))(x, gain, tgt)


def _dact(dx, wd, gu, tm, comm=None):
    T, D = dx.shape
    F = wd.shape[0]
    ns = F // 2

    def body(dx_ref, w_ref, gu_ref, o_ref):
        dxb = _mx(dx_ref[...])
        for s in range(2):
            cols = slice(s * ns, (s + 1) * ns)
            da = lax.dot_general(dxb, w_ref[s * ns:(s + 1) * ns, :], NT, preferred_element_type=f32)
            g = gu_ref[0, :, cols].astype(f32)
            u = gu_ref[1, :, cols].astype(f32)
            sg = jax.nn.sigmoid(g)
            o_ref[0, :, cols] = (da * u * (sg * (1.0 + g * (1.0 - sg)))).astype(bf16)
            o_ref[1, :, cols] = (da * (g * sg)).astype(bf16)

    return _hosted_call(
        body, T // tm,
        [pl.BlockSpec((tm, D), lambda i: (i, 0)),
         pl.BlockSpec((F, D), lambda i: (0, 0), pipeline_mode=RESIDENT),
         pl.BlockSpec((2, tm, F), lambda i: (0, i, 0))],
        [pl.BlockSpec((2, tm, F), lambda i: (0, i, 0))], [SDS((2, T, F), bf16)], [], (dx, wd, gu), "dact", comm)


def _dw(a, b, b_map, ns, NS, tka, tk, name):
    T, Ka = a.shape
    b_block = (tk, ns) if b.ndim == 2 else (1, tk, ns)

    def body(a_ref, b_ref, o_ref):
        bv = b_ref[...] if b.ndim == 2 else b_ref[0]
        part = lax.dot_general(_mx(a_ref[...]), _mx(bv), TN, preferred_element_type=f32)

        @pl.when(pl.program_id(2) == 0)
        def _():
            o_ref[0] = part

        @pl.when(pl.program_id(2) > 0)
        def _():
            o_ref[0] += part

    return pl.pallas_call(
        body, grid=(NS, Ka // tka, T // tk),
        in_specs=[pl.BlockSpec((tk, tka), lambda s, k, t: (t, k)),
                  pl.BlockSpec(b_block, lambda s, k, t: b_map(t, s))],
        out_specs=pl.BlockSpec((1, tka, ns), lambda s, k, t: (s, k, 0)),
        out_shape=SDS((NS, Ka, ns), f32),
        name=name, compiler_params=_cp("parallel", "parallel", "arbitrary"))(a, b)


def _dx_norm_bwd(dy, w, x, gain, dxin, tm, name, comm=None):
    T, D = x.shape
    NS, _, ns = w.shape
    half = NS // 2

    def body(dy_ref, w_ref, x_ref, g_ref, dxin_ref, dx_ref, dg_ref):
        @pl.when(pl.program_id(0) == 0)
        def _():
            dg_ref[...] = jnp.zeros_like(dg_ref)
        dh = None
        for s in range(NS):
            if dy.ndim == 2:
                dv = dy_ref[:, s * ns:(s + 1) * ns]
            else:
                dv = dy_ref[s // half, :, (s % half) * ns:(s % half + 1) * ns]
            part = lax.dot_general(_mx(dv), w_ref[s], NT, preferred_element_type=f32)
            dh = part if dh is None else dh + part
        xv = x_ref[...]
        r = _rms(xv)
        xh = xv * r
        dg_ref[...] += jnp.sum(dh * xh, axis=0, keepdims=True)
        dxh = dh * g_ref[...]
        dx_ref[...] = dxin_ref[...] + r * (dxh - xh * jnp.mean(dxh * xh, axis=-1, keepdims=True))

    dy_spec = (pl.BlockSpec((tm, NS * ns), lambda i: (i, 0)) if dy.ndim == 2
               else pl.BlockSpec((2, tm, half * ns), lambda i: (0, i, 0)))
    c_args, c_in, c_out, c_shapes, aliases, c_scratch = _host_specs(comm, 5, 2)
    step = lambda v: (lambda: pl.program_id(0) == v)
    return pl.pallas_call(
        _host(body, 5, 2, 0, comm, step(0), step(T // tm - 1)), grid=(T // tm,),
        in_specs=[dy_spec,
                  pl.BlockSpec((NS, D, ns), lambda i: (0, 0, 0), pipeline_mode=RESIDENT),
                  pl.BlockSpec((tm, D), lambda i: (i, 0)),
                  pl.BlockSpec((1, D), lambda i: (0, 0)),
                  pl.BlockSpec((tm, D), lambda i: (i, 0))] + c_in,
        out_specs=[pl.BlockSpec((tm, D), lambda i: (i, 0)),
                   pl.BlockSpec((1, D), lambda i: (0, 0))] + c_out,
        out_shape=[SDS((T, D), f32), SDS((1, D), f32)] + c_shapes,
        input_output_aliases=aliases, scratch_shapes=c_scratch,
        name=name, compiler_params=_cp("arbitrary"))(dy, w, x, gain, dxin, *c_args)


def _dycat(dx, w, ys, gain, tm, comm=None):
    T, D = dx.shape

    def body(dx_ref, w_ref, ya, yb, yc, yd, g_ref, da, db, dc, dd, dg_ref):
        @pl.when(pl.program_id(0) == 0)
        def _():
            dg_ref[...] = jnp.zeros_like(dg_ref)
        dyn = lax.dot_general(_mx(dx_ref[...]), w_ref[...], NT, preferred_element_type=f32)
        for m, (r, o) in enumerate(((ya, da), (yb, db), (yc, dc), (yd, dd))):
            cols = slice(m * GW, (m + 1) * GW)
            y = r[...]
            rs = _rms(y)
            yh = y * rs
            d = dyn[:, cols]
            dg_ref[:, cols] += jnp.sum(d * yh, axis=0, keepdims=True)
            dyh = d * g_ref[:, cols]
            o[...] = rs * (dyh - yh * jnp.mean(dyh * yh, axis=-1, keepdims=True))

    yspec = pl.BlockSpec((tm, GW), lambda i: (i, 0))
    return _hosted_call(
        body, T // tm,
        [pl.BlockSpec((tm, D), lambda i: (i, 0)),
         pl.BlockSpec((D, D), lambda i: (0, 0), pipeline_mode=RESIDENT),
         yspec, yspec, yspec, yspec,
         pl.BlockSpec((1, D), lambda i: (0, 0))],
        [yspec, yspec, yspec, yspec, pl.BlockSpec((1, D), lambda i: (0, 0))],
        [SDS((T, GW), f32)] * 4 + [SDS((1, D), f32)], [], (dx, w, *ys, gain), "dycat", comm)


def _sgu_consts():
    r, c = _iota((GW, GW), 0), _iota((GW, GW), 1)
    seg = (r // HD == c // HD).astype(f32)
    tr, ts = _iota((BLK, BLK), 0), _iota((BLK, BLK), 1)
    causal = ts <= tr
    lane_head = _iota((BLK, GW), 1) // HD
    return seg, causal, lane_head


def _split3_dot(a, ones):
    hi = a.astype(bf16)
    r1 = a - hi.astype(f32)
    mid = r1.astype(bf16)
    lo = (r1 - mid.astype(f32)).astype(bf16)
    dot = functools.partial(jnp.dot, preferred_element_type=f32)
    return dot(hi, ones) + dot(mid, ones) + dot(lo, ones)


def _sgu_chunks(aus, avs, w, bexp, consts):
    seg, causal, lane_head = consts
    segb = seg.astype(bf16)
    nh = GW // HD
    vs = [jax.nn.gelu(av) for av in avs]
    mus = [_split3_dot(v, segb) * (1.0 / HD) for v in vs]
    vcs = [v - mu for v, mu in zip(vs, mus)]
    vars_ = [_split3_dot(vc * vc, segb) * (1.0 / HD) for vc in vcs]
    vns = [_mx(vc * lax.rsqrt(var + EPS)) for vc, var in zip(vcs, vars_)]
    whs = [_mx(jnp.where(causal, w[h], 0.0)) for h in range(nh)]
    mixes = [[jnp.dot(whs[h], vn, preferred_element_type=f32) for h in range(nh)] for vn in vns]
    out = []
    for au, ms in zip(aus, mixes):
        mix = bexp
        for h in range(nh):
            mix = mix + jnp.where(lane_head == h, ms[h], 0.0)
        out.append(jax.nn.gelu(au) * mix)
    return out


def _sgu_group(S):
    nc = S // BLK
    return 4 if nc % 4 == 0 else (2 if nc % 2 == 0 else 1)


def _sgu_fwd(proj, w, bexp, B, S):
    G = _sgu_group(S)

    def body(au_ref, av_ref, w_ref, b_ref, y_ref):
        consts = _sgu_consts()
        wv, bv = w_ref[...], b_ref[...]

        def group(n, c):
            rows = [pl.ds(pl.multiple_of((n * G + j) * BLK, BLK), BLK) for j in range(G)]
            ys = _sgu_chunks([au_ref[r, :] for r in rows], [av_ref[r, :] for r in rows], wv, bv, consts)
            for r, y in zip(rows, ys):
                y_ref[r, :] = y
            return c
        lax.fori_loop(0, S // BLK // G, group, 0)

    return pl.pallas_call(
        body, grid=(B,),
        in_specs=[pl.BlockSpec((S, GW), lambda b: (b, 0)),
                  pl.BlockSpec((S, GW), lambda b: (b, 1)),
                  pl.BlockSpec((GW // HD, BLK, BLK), lambda b: (0, 0, 0)),
                  pl.BlockSpec((BLK, GW), lambda b: (0, 0))],
        out_specs=pl.BlockSpec((S, GW), lambda b: (b, 0)),
        out_shape=SDS((B * S, GW), f32),
        name="sgu_fwd", compiler_params=_cp("parallel"))(proj, proj, w, bexp)


def _sgu_bwd(proj, w, bexp, dy, B, S, comm=None):
    def body(au_ref, av_ref, w_ref, b_ref, dy_ref, dp_ref, dw_ref, db_ref):
        @pl.when(pl.program_id(0) == 0)
        def _():
            dw_ref[...] = jnp.zeros_like(dw_ref)
            db_ref[...] = jnp.zeros_like(db_ref)
        consts = _sgu_consts()
        wv, bv = w_ref[...], b_ref[...]
        fn = lambda aus, avs, ww, bb: _sgu_chunks(aus, avs, ww, bb, consts)
        G = _sgu_group(S)

        def group(n, carry):
            dw_acc, db_acc = carry
            rows = [pl.ds(pl.multiple_of((n * G + j) * BLK, BLK), BLK) for j in range(G)]
            _, vjp = jax.vjp(fn, [au_ref[r, :] for r in rows], [av_ref[r, :] for r in rows], wv, bv)
            daus, davs, dwc, dbc = vjp([dy_ref[r, :] for r in rows])
            for r, dau, dav in zip(rows, daus, davs):
                dp_ref[r, 0:GW] = dau.astype(bf16)
                dp_ref[r, GW:2 * GW] = dav.astype(bf16)
            return dw_acc + dwc, db_acc + dbc
        dw_acc, db_acc = lax.fori_loop(0, S // BLK // G, group, (jnp.zeros(wv.shape, f32), jnp.zeros(bv.shape, f32)))
        dw_ref[...] += dw_acc
        db_ref[...] += jnp.dot(db_acc, consts[0], precision=HIGHEST, preferred_element_type=f32)

    return _hosted_call(
        body, B,
        [pl.BlockSpec((S, GW), lambda b: (b, 0)),
         pl.BlockSpec((S, GW), lambda b: (b, 1)),
         pl.BlockSpec((GW // HD, BLK, BLK), lambda b: (0, 0, 0)),
         pl.BlockSpec((BLK, GW), lambda b: (0, 0)),
         pl.BlockSpec((S, GW), lambda b: (b, 0))],
        [pl.BlockSpec((S, 2 * GW), lambda b: (b, 0)),
         pl.BlockSpec((GW // HD, BLK, BLK), lambda b: (0, 0, 0)),
         pl.BlockSpec((BLK, GW), lambda b: (0, 0))],
        [SDS((B * S, 2 * GW), bf16), SDS((GW // HD, BLK, BLK), f32), SDS((BLK, GW), f32)], [],
        (proj, proj, w, bexp, dy), "sgu_bwd", comm)


def _pool_parts(p):
    n = p.shape[0]
    r = _iota(p.shape, 0)
    lg = _iota(p.shape, 1) // HD

    def sh(v, k):
        return jnp.where(r >= k, pltpu.roll(v, k, 0), 0.0)
    s2 = p + sh(p, 1)
    s4 = s2 + sh(s2, 2)
    s8 = s4 + sh(s4, 4)
    s16 = s8 + sh(s8, 8)
    ws = jnp.where(lg == 0, s2, jnp.where(lg == 1, s4, jnp.where(lg == 2, s8, s16)))
    wlen = jnp.where(lg == 0, 2, jnp.where(lg == 1, 4, jnp.where(lg == 2, 8, 16)))
    cnt = jnp.minimum(r + 1, wlen).astype(f32)
    del n
    return ws / cnt - p, cnt, lg


def _pool_fwd(proj, wbd, scale, B, S):
    def body(p_ref, w_ref, s_ref, y_ref):
        y, _, _ = _pool_parts(p_ref[...])
        y_ref[...] = jnp.dot(_mx(y), _mx(w_ref[...]), preferred_element_type=f32) * s_ref[...]

    return pl.pallas_call(
        body, grid=(B,),
        in_specs=[pl.BlockSpec((S, GW), lambda b: (b, 2)),
                  pl.BlockSpec((GW, GW), lambda b: (0, 0)),
                  pl.BlockSpec((1, GW), lambda b: (0, 0))],
        out_specs=pl.BlockSpec((S, GW), lambda b: (b, 0)),
        out_shape=SDS((B * S, GW), f32),
        name="pool_fwd", compiler_params=_cp("parallel"))(proj, wbd, scale)


def _pool_bwd(proj, wbd, scale, dy, B, S):
    def body(p_ref, w_ref, s_ref, dy_ref, dp_ref, dw_ref, ds_ref):
        @pl.when(pl.program_id(0) == 0)
        def _():
            dw_ref[...] = jnp.zeros_like(dw_ref)
            ds_ref[...] = jnp.zeros_like(ds_ref)
        y, cnt, lg = _pool_parts(p_ref[...])
        wv = _mx(w_ref[...])
        z = jnp.dot(_mx(y), wv, preferred_element_type=f32)
        dout = dy_ref[...]
        ds_ref[...] += jnp.sum(dout * z, axis=0, keepdims=True)
        dz = _mx(dout * s_ref[...])
        dw_ref[...] += lax.dot_general(_mx(y), dz, TN, preferred_element_type=f32)
        dyv = lax.dot_general(dz, wv, NT, preferred_element_type=f32)
        n = dyv.shape[0]
        r = _iota(dyv.shape, 0)

        def ush(v, k):
            return jnp.where(r < n - k, pltpu.roll(v, n - k, 0), 0.0)
        gq = dyv / cnt
        a2 = gq + ush(gq, 1)
        a4 = a2 + ush(a2, 2)
        a8 = a4 + ush(a4, 4)
        a16 = a8 + ush(a8, 8)
        adj = jnp.where(lg == 0, a2, jnp.where(lg == 1, a4, jnp.where(lg == 2, a8, a16)))
        dp_ref[...] = (adj - dyv).astype(bf16)

    return pl.pallas_call(
        body, grid=(B,),
        in_specs=[pl.BlockSpec((S, GW), lambda b: (b, 2)),
                  pl.BlockSpec((GW, GW), lambda b: (0, 0)),
                  pl.BlockSpec((1, GW), lambda b: (0, 0)),
                  pl.BlockSpec((S, GW), lambda b: (b, 0))],
        out_specs=[pl.BlockSpec((S, GW), lambda b: (b, 0)),
                   pl.BlockSpec((GW, GW), lambda b: (0, 0)),
                   pl.BlockSpec((1, GW), lambda b: (0, 0))],
        out_shape=[SDS((B * S, GW), bf16), SDS((GW, GW), f32), SDS((1, GW), f32)],
        name="pool_bwd", compiler_params=_cp("arbitrary"))(proj, wbd, scale, dy)


def _t5_bucket_table():
    dist = (np.arange(BLK)[:, None] + BLK) - np.arange(2 * BLK)[None, :]
    d = np.clip(dist, 0, BLK - 1)
    max_exact = N_BUCKETS // 2
    df = np.maximum(d, 1).astype(np.float32)
    large = max_exact + (np.log(df / max_exact) / np.float32(np.log(MAX_DISTANCE / max_exact))
                         * (N_BUCKETS - max_exact)).astype(np.int32)
    large = np.minimum(large, N_BUCKETS - 1)
    return np.where(d < max_exact, d, large).astype(np.int32)


def _swa_block(qb0, qb1, k2, v2, sinks, biases, n):
    heads = [(p, g) for p in range(2) for g in range(2)]
    ri, ci = _iota((BLK, BLK), 0), _iota((BLK, BLK), 1)
    qi, ki = _iota((BLK, 2 * BLK), 0), _iota((BLK, 2 * BLK), 1)
    dist = qi + BLK - ki
    mask = (dist >= 0) & (dist < BLK) & ((ki >= BLK) | (n > 0))
    qbs, kb, vb = (_mx(qb0), _mx(qb1)), _mx(k2), _mx(v2)
    qs, vs = [], []
    for p, g in heads:
        selq = ((ri - g * HD == ci - p * HD) & (ri >= g * HD) & (ri < (g + 1) * HD)).astype(_MXU)
        selv = ((ci - g * HD == ri - p * HD) & (ci >= g * HD) & (ci < (g + 1) * HD)).astype(_MXU)
        qs.append(_mx(jnp.dot(qbs[p], selq, preferred_element_type=f32)))
        vs.append(_mx(jnp.dot(vb, selv, preferred_element_type=f32)))
    zs = [lax.dot_general(q, kb, NT, preferred_element_type=f32) * (HD ** -0.5) for q in qs]
    prs = []
    for h in range(4):
        z = jnp.where(mask, zs[h] + biases[h], -1e30)
        s = jnp.mean(sinks[h], axis=-1, keepdims=True)
        m = jnp.maximum(jnp.max(z, axis=-1, keepdims=True), s)
        e = jnp.exp(z - m)
        prs.append(_mx(e / (jnp.sum(e, axis=-1, keepdims=True) + jnp.exp(s - m))))
    outs = [jnp.dot(prs[h], vs[h], preferred_element_type=f32) for h in range(4)]
    return outs[0] + outs[1], outs[2] + outs[3]


def _swa_fwd(proj, sinks, bias, B, S):
    def body(q_ref, kv_ref, s_ref, b_ref, y_ref):
        def block(n, c):
            rows = pl.ds(pl.multiple_of(n * BLK, BLK), BLK)
            prev = pl.ds(pl.multiple_of(jnp.maximum(n - 1, 0) * BLK, BLK), BLK)
            k2 = jnp.concatenate([kv_ref[prev, 0:BLK], kv_ref[rows, 0:BLK]], axis=0)
            v2 = jnp.concatenate([kv_ref[prev, BLK:2 * BLK], kv_ref[rows, BLK:2 * BLK]], axis=0)
            o0, o1 = _swa_block(q_ref[rows, 0:BLK], q_ref[rows, BLK:2 * BLK], k2, v2,
                                [s_ref[h] for h in range(4)], [b_ref[h] for h in range(4)], n)
            y_ref[rows, 0:BLK] = o0
            y_ref[rows, BLK:2 * BLK] = o1
            return c
        lax.fori_loop(0, S // BLK, block, 0)

    return pl.pallas_call(
        body, grid=(B,),
        in_specs=[pl.BlockSpec((S, GW), lambda b: (b, 3)),
                  pl.BlockSpec((S, GW), lambda b: (b, 4)),
                  pl.BlockSpec((4, 1, BLK), lambda b: (0, 0, 0)),
                  pl.BlockSpec((4, BLK, 2 * BLK), lambda b: (0, 0, 0))],
        out_specs=pl.BlockSpec((S, GW), lambda b: (b, 0)),
        out_shape=SDS((B * S, GW), f32),
        name="swa_fwd", compiler_params=_cp("parallel"))(proj, proj, sinks, bias)


def _swa_bwd(proj, sinks, bias, dy, B, S, comm=None):
    def body(q_ref, kv_ref, s_ref, b_ref, dy_ref, dq_ref, dkv_ref, ds_ref, db_ref, acc_ref):
        @pl.when(pl.program_id(0) == 0)
        def _():
            ds_ref[...] = jnp.zeros_like(ds_ref)
            db_ref[...] = jnp.zeros_like(db_ref)
        acc_ref[...] = jnp.zeros_like(acc_ref)

        def block(n, c):
            rows = pl.ds(pl.multiple_of(n * BLK, BLK), BLK)
            prev = pl.ds(pl.multiple_of(jnp.maximum(n - 1, 0) * BLK, BLK), BLK)
            k2 = jnp.concatenate([kv_ref[prev, 0:BLK], kv_ref[rows, 0:BLK]], axis=0)
            v2 = jnp.concatenate([kv_ref[prev, BLK:2 * BLK], kv_ref[rows, BLK:2 * BLK]], axis=0)
            fn = functools.partial(_swa_block, n=n)
            _, vjp = jax.vjp(fn, q_ref[rows, 0:BLK], q_ref[rows, BLK:2 * BLK], k2, v2,
                             [s_ref[h] for h in range(4)], [b_ref[h] for h in range(4)])
            dq0, dq1, dk2, dv2, dss, dbs = vjp((dy_ref[rows, 0:BLK], dy_ref[rows, BLK:2 * BLK]))
            dq_ref[rows, 0:BLK] = dq0.astype(bf16)
            dq_ref[rows, BLK:2 * BLK] = dq1.astype(bf16)
            for h in range(4):
                ds_ref[h] += dss[h]
                db_ref[h] += dbs[h]
            acc_ref[prev, 0:BLK] += dk2[0:BLK]
            acc_ref[rows, 0:BLK] += dk2[BLK:2 * BLK]
            acc_ref[prev, BLK:2 * BLK] += dv2[0:BLK]
            acc_ref[rows, BLK:2 * BLK] += dv2[BLK:2 * BLK]
            return c
        lax.fori_loop(0, S // BLK, block, 0)
        dkv_ref[...] = acc_ref[...].astype(bf16)

    c_args, c_in, c_out, c_shapes, aliases, c_scratch = _host_specs(comm, 5, 4)
    step = lambda v: (lambda: pl.program_id(0) == v)
    return pl.pallas_call(
        _host(body, 5, 4, 1, comm, step(0), step(B - 1)), grid=(B,),
        in_specs=[pl.BlockSpec((S, GW), lambda b: (b, 3)),
                  pl.BlockSpec((S, GW), lambda b: (b, 4)),
                  pl.BlockSpec((4, 1, BLK), lambda b: (0, 0, 0)),
                  pl.BlockSpec((4, BLK, 2 * BLK), lambda b: (0, 0, 0)),
                  pl.BlockSpec((S, GW), lambda b: (b, 0))] + c_in,
        out_specs=[pl.BlockSpec((S, GW), lambda b: (b, 0)),
                   pl.BlockSpec((S, GW), lambda b: (b, 0)),
                   pl.BlockSpec((4, 1, BLK), lambda b: (0, 0, 0)),
                   pl.BlockSpec((4, BLK, 2 * BLK), lambda b: (0, 0, 0))] + c_out,
        out_shape=[SDS((B * S, GW), bf16), SDS((B * S, GW), bf16), SDS((4, 1, BLK), f32),
                   SDS((4, BLK, 2 * BLK), f32)] + c_shapes,
        input_output_aliases=aliases, scratch_shapes=[pltpu.VMEM((S, GW), f32)] + c_scratch,
        name="swa_bwd" if comm is None else "swa_bwd_exchange",
        compiler_params=_cp("arbitrary"))(proj, proj, sinks, bias, dy, *c_args)


def _log1m_parts(z):
    t = jnp.exp(-jnp.abs(z))
    return jnp.minimum(-z, 0.0) - jnp.log(1.0 + t), t


def _log1m(z):
    return _log1m_parts(z)[0]


def _sigmoid_from(z, t):
    return jnp.where(z >= 0.0, 1.0, t) / (1.0 + t)


def _sb_consts(tri):
    r2, c2 = _iota((2 * BLK, 2 * BLK), 0), _iota((2 * BLK, 2 * BLK), 1)
    tri2 = (tri(r2, c2) & (r2 // BLK == c2 // BLK)).astype(bf16)
    ri, ci = _iota((BLK, 2 * BLK), 0), _iota((BLK, 2 * BLK), 1)
    strict2 = (ci % BLK) < ri
    head0 = _iota((BLK, BLK), 1) < HD
    return tri2, strict2, head0


def _sb_stack_kv(k_ref, v_ref, kst_ref, vst_ref, head0, nb):
    def one(kb, c):
        krows = pl.ds(pl.multiple_of(kb * BLK, BLK), BLK)
        for p in range(2):
            for src, dst in ((k_ref, kst_ref), (v_ref, vst_ref)):
                t = src[krows, p * BLK:(p + 1) * BLK]
                dst[p, kb] = _mx(jnp.concatenate([jnp.where(head0, t, 0.0), jnp.where(head0, 0.0, t)], axis=0))
        return c
    lax.fori_loop(0, nb, one, 0)


def _sb_load_kv(kst_ref, vst_ref, kb):
    return [kst_ref[p, kb] for p in range(2)], [vst_ref[p, kb] for p in range(2)]


def _two_halves(a, b):
    return jnp.concatenate([jnp.broadcast_to(a, (BLK, BLK)), jnp.broadcast_to(b, (BLK, BLK))], axis=1)


def _half_sums(t):
    return jnp.sum(t[:, :BLK], axis=-1, keepdims=True), jnp.sum(t[:, BLK:], axis=-1, keepdims=True)


def _sb_fwd(proj, B, S, comm=None):
    def body(q_ref, k_ref, v_ref, y_ref, lt_ref, kst_ref, vst_ref):
        ci = _iota((BLK, BLK), 1)
        above2, strict2, head0 = _sb_consts(lambda r, c: r > c)
        _sb_stack_kv(k_ref, v_ref, kst_ref, vst_ref, head0, S // BLK)

        def step(qs, kbs, carry, diag):
            ok = [None if diag else kb >= 0 for kb in kbs]
            kv = [_sb_load_kv(kst_ref, vst_ref, jnp.maximum(kb, 0)) for kb in kbs]
            zs = [[lax.dot_general(qs[p], kks[p], NT, preferred_element_type=f32) for p in range(2)] for kks, _ in kv]
            Ls = [[_log1m(z) for z in zu] for zu in zs]
            if diag:
                Ls = [[jnp.where(strict2, L, 0.0) for L in Lu] for Lu in Ls]
            tails = [[_split_dot(L, above2) for L in Lu] for Lu in Ls]
            carry = list(carry)
            for u in range(len(kbs)):
                for p in range(2):
                    R0, R1, acc = carry[3 * p:3 * p + 3]
                    w = jnp.exp(zs[u][p] + Ls[u][p] + tails[u][p] + _two_halves(R0, R1))
                    s0, s1 = _half_sums(Ls[u][p])
                    if diag:
                        w = jnp.where(strict2, w, 0.0)
                    else:
                        w, s0, s1 = (jnp.where(ok[u], t, 0.0) for t in (w, s0, s1))
                    acc = acc + jnp.dot(_mx(w), kv[u][1][p], preferred_element_type=f32)
                    carry[3 * p:3 * p + 3] = [R0 + s0, R1 + s1, acc]
            return tuple(carry)

        def qblock(n, c):
            qrows = pl.ds(pl.multiple_of(n * BLK, BLK), BLK)
            qs = [_mx(q_ref[qrows, p * BLK:(p + 1) * BLK] * (HD ** -0.5)) for p in range(2)]
            z1, z2 = jnp.zeros((BLK, 1), f32), jnp.zeros((BLK, BLK), f32)
            carry = step(qs, [n], (z1, z1, z2, z1, z1, z2), True)
            trips = (n + SB_UNROLL - 1) // SB_UNROLL

            def live(st):
                worst = jnp.maximum(jnp.maximum(st[1], st[2]), jnp.maximum(st[4], st[5]))
                return (st[0] < trips) & (jnp.max(worst) > SB_CUT)

            def trip(st):
                i = st[0]
                return (i + 1,) + step(qs, [n - 1 - SB_UNROLL * i - u for u in range(SB_UNROLL)], st[1:], False)
            done, *res = lax.while_loop(live, trip, (jnp.int32(0),) + carry)
            lt = jnp.where(ci == SB_HEADS, done.astype(f32), 0.0)
            for p in range(2):
                y_ref[qrows, p * BLK:(p + 1) * BLK] = res[3 * p + 2]
                lt = lt + jnp.where(ci == 2 * p, res[3 * p], 0.0) + jnp.where(ci == 2 * p + 1, res[3 * p + 1], 0.0)
            lt_ref[qrows, :] = lt
            return c
        lax.fori_loop(0, S // BLK, qblock, 0)

    spec = lambda j: pl.BlockSpec((S, GW), lambda b: (b, j))
    c_args, c_in, c_out, c_shapes, aliases, c_scratch = _host_specs(comm, 3, 2)
    step = lambda v: (lambda: pl.program_id(0) == v)
    stacked = pltpu.VMEM((2, S // BLK, 2 * BLK, BLK), _MXU)
    return pl.pallas_call(
        _host(body, 3, 2, 2, comm, step(0), step(B - 1)), grid=(B,),
        in_specs=[spec(5), spec(6), spec(7)] + c_in,
        out_specs=[pl.BlockSpec((S, GW), lambda b: (b, 0)), pl.BlockSpec((S, BLK), lambda b: (b, 0))] + c_out,
        out_shape=[SDS((B * S, GW), f32), SDS((B * S, BLK), f32)] + c_shapes,
        input_output_aliases=aliases, scratch_shapes=[stacked, stacked] + c_scratch,
        name="sb_fwd" if comm is None else "sb_fwd_gather",
        compiler_params=_cp("arbitrary"))(proj, proj, proj, *c_args)


def _sb_bwd(proj, ltot, dy, B, S, comm=None):
    def body(q_ref, k_ref, v_ref, lt_ref, dy_ref, dq_ref, dk_ref, dv_ref, dka_ref, dva_ref, kst_ref, vst_ref):
        ci = _iota((BLK, BLK), 1)
        upto2, strict2, head0 = _sb_consts(lambda r, c: r <= c)
        below2, _, _ = _sb_consts(lambda r, c: r < c)
        dka_ref[...] = jnp.zeros_like(dka_ref)
        dva_ref[...] = jnp.zeros_like(dva_ref)
        _sb_stack_kv(k_ref, v_ref, kst_ref, vst_ref, head0, S // BLK)

        def step(qs, dos, lts, kbs, last, carry, diag):
            U = range(len(kbs))
            ok = [None if diag else kb <= last for kb in kbs]
            kbs = [jnp.minimum(kb, last) for kb in kbs]
            kv = [_sb_load_kv(kst_ref, vst_ref, kb) for kb in kbs]
            zs = [[lax.dot_general(qs[p], kv[u][0][p], NT, preferred_element_type=f32) for p in range(2)] for u in U]
            dws = [[lax.dot_general(dos[p], kv[u][1][p], NT, preferred_element_type=f32) for p in range(2)] for u in U]
            parts = [[_log1m_parts(z) for z in zu] for zu in zs]
            Ls = [[lt[0] for lt in pu] for pu in parts]
            if diag:
                Ls = [[jnp.where(strict2, L, 0.0) for L in Lu] for Lu in Ls]
            pins = [[_split_dot(L, upto2) for L in Lu] for Lu in Ls]
            carry = list(carry)
            ws, das = [], []
            for u in U:
                wu, dau = [], []
                for p in range(2):
                    PL0, PL1 = carry[5 * p], carry[5 * p + 1]
                    tail = _two_halves(lts[2 * p] - PL0, lts[2 * p + 1] - PL1) - pins[u][p]
                    w = jnp.exp(zs[u][p] + Ls[u][p] + tail)
                    l0, l1 = _half_sums(Ls[u][p])
                    if diag:
                        w = jnp.where(strict2, w, 0.0)
                    else:
                        w, l0, l1 = (jnp.where(ok[u], t, 0.0) for t in (w, l0, l1))
                    carry[5 * p], carry[5 * p + 1] = PL0 + l0, PL1 + l1
                    wu.append(w)
                    dau.append(w * dws[u][p])
                ws.append(wu)
                das.append(dau)
            pexs = [[_split_dot(da, below2) for da in dau] for dau in das]
            dzs = []
            for u in U:
                dzu = []
                for p in range(2):
                    dL = _two_halves(carry[5 * p + 2], carry[5 * p + 3]) + pexs[u][p]
                    sg = _sigmoid_from(zs[u][p], parts[u][p][1])
                    dz = das[u][p] * (1.0 - sg) - dL * sg
                    dz = jnp.where(strict2 if diag else ok[u], dz, 0.0)
                    a0, a1 = _half_sums(das[u][p])
                    carry[5 * p + 2], carry[5 * p + 3] = carry[5 * p + 2] + a0, carry[5 * p + 3] + a1
                    dzu.append(_mx(dz))
                dzs.append(dzu)
            dqs = [[jnp.dot(dzs[u][p], kv[u][0][p], preferred_element_type=f32) for p in range(2)] for u in U]
            dks = [[lax.dot_general(dzs[u][p], qs[p], TN, preferred_element_type=f32) for p in range(2)] for u in U]
            dvs = [[lax.dot_general(_mx(ws[u][p]), dos[p], TN, preferred_element_type=f32) for p in range(2)] for u in U]
            for u in U:
                krows = pl.ds(pl.multiple_of(kbs[u] * BLK, BLK), BLK)
                for p in range(2):
                    lanes = slice(p * BLK, (p + 1) * BLK)
                    dka_ref[krows, lanes] += jnp.where(head0, dks[u][p][:BLK], dks[u][p][BLK:])
                    dva_ref[krows, lanes] += jnp.where(head0, dvs[u][p][:BLK], dvs[u][p][BLK:])
                    carry[5 * p + 4] = carry[5 * p + 4] + dqs[u][p]
            return tuple(carry)

        def qblock(n, c):
            qrows = pl.ds(pl.multiple_of(n * BLK, BLK), BLK)
            ltb = lt_ref[qrows, :]
            lts = [jnp.sum(jnp.where(ci == h, ltb, 0.0), axis=-1, keepdims=True) for h in range(4)]
            qs = [_mx(q_ref[qrows, p * BLK:(p + 1) * BLK] * (HD ** -0.5)) for p in range(2)]
            dos = [_mx(dy_ref[qrows, p * BLK:(p + 1) * BLK]) for p in range(2)]
            z1, z2 = jnp.zeros((BLK, 1), f32), jnp.zeros((BLK, BLK), f32)
            done = jnp.max(jnp.where(ci == SB_HEADS, ltb, 0.0)).astype(jnp.int32)
            first = jnp.maximum(n - SB_UNROLL * done, 0)
            carry = lax.fori_loop(
                0, (n - first + SB_UNROLL - 1) // SB_UNROLL,
                lambda i, cr: step(qs, dos, lts, [first + SB_UNROLL * i + u for u in range(SB_UNROLL)], n - 1, cr, False),
                (z1, z1, z1, z1, z2) * 2)
            res = step(qs, dos, lts, [n], n, carry, True)
            for p in range(2):
                dq_ref[qrows, p * BLK:(p + 1) * BLK] = (res[5 * p + 4] * (HD ** -0.5)).astype(bf16)
            return c
        lax.fori_loop(0, S // BLK, qblock, 0)
        dk_ref[...] = dka_ref[...].astype(bf16)
        dv_ref[...] = dva_ref[...].astype(bf16)

    spec = lambda j: pl.BlockSpec((S, GW), lambda b: (b, j))
    o = pl.BlockSpec((S, GW), lambda b: (b, 0))
    c_args, c_in, c_out, c_shapes, aliases, c_scratch = _host_specs(comm, 5, 3)
    step = lambda v: (lambda: pl.program_id(0) == v)
    stacked = pltpu.VMEM((2, S // BLK, 2 * BLK, BLK), _MXU)
    return pl.pallas_call(
        _host(body, 5, 3, 4, comm, step(0), step(B - 1)), grid=(B,),
        in_specs=[spec(5), spec(6), spec(7), pl.BlockSpec((S, BLK), lambda b: (b, 0)), o] + c_in,
        out_specs=[o, o, o] + c_out,
        out_shape=[SDS((B * S, GW), bf16)] * 3 + c_shapes,
        input_output_aliases=aliases,
        scratch_shapes=[pltpu.VMEM((S, GW), f32), pltpu.VMEM((S, GW), f32), stacked, stacked] + c_scratch,
        name="sb_bwd" if comm is None else "sb_bwd_exchange",
        compiler_params=_cp("arbitrary"))(proj, proj, proj, ltot, dy, *c_args)


def _bias_expand(rel_bias_t, bucket):
    n = bucket.shape[1]

    def body(r_ref, b_ref, o_ref):
        onehot = (_iota((N_BUCKETS, n), 0) == b_ref[...]).astype(f32)
        o_ref[...] = jnp.dot(r_ref[...], onehot, precision=HIGHEST, preferred_element_type=f32)
    return pl.pallas_call(body, out_shape=SDS((rel_bias_t.shape[0], n), f32), name="bias_expand",
                          compiler_params=_cp())(rel_bias_t, bucket)


def _bias_reduce(dbias, bucket):
    n = bucket.shape[1]

    def body(*refs):
        b_ref, g_ref = refs[-2], refs[-1]
        d = refs[0][...]
        for r in refs[1:-2]:
            d = d + r[...]
        onehot = (_iota((N_BUCKETS, n), 0) == b_ref[...]).astype(f32)
        g_ref[...] = lax.dot_general(d, onehot, NT, precision=HIGHEST, preferred_element_type=f32)
    return pl.pallas_call(body, out_shape=SDS((dbias[0].shape[0], N_BUCKETS), f32), name="bias_reduce",
                          compiler_params=_cp())(*dbias, bucket)


def _adamw(w, g, m, v, tr, name):
    R, C = w.shape

    def body(w_ref, g_ref, m_ref, v_ref, d_ref, m2_ref, v2_ref):
        gv = g_ref[...]
        m2 = ADAM_B1 * m_ref[...] + (1.0 - ADAM_B1) * gv
        v2 = ADAM_B2 * v_ref[...] + (1.0 - ADAM_B2) * (gv * gv)
        m_hat = m2 / (1.0 - ADAM_B1 ** ADAM_STEP)
        v_hat = v2 / (1.0 - ADAM_B2 ** ADAM_STEP)
        d_ref[...] = -ADAM_LR * (m_hat / (jnp.sqrt(v_hat) + ADAM_EPS) + ADAM_WD * w_ref[...])
        m2_ref[...] = m2
        v2_ref[...] = v2

    spec = pl.BlockSpec((tr, C), lambda i: (i, 0))
    return pl.pallas_call(
        body, grid=(R // tr,), in_specs=[spec] * 4, out_specs=[spec] * 3,
        out_shape=[SDS((R, C), f32)] * 3, name=name, compiler_params=_cp("parallel"))(w, g, m, v)


ANY = pl.BlockSpec(memory_space=pl.ANY)


def _place():
    x, y, c = lax.axis_index("x"), lax.axis_index("y"), lax.axis_index("c")
    chips = [(1 - x, y), (x, 1 - y), (1 - x, 1 - y)]
    return x, y, c, chips


def _cast_slots(w, kidx):
    L, a, b = w.shape
    ta = a // 2

    def body(k_ref, *refs):
        for l in range(L):
            refs[L + l][0] = refs[l][0].astype(bf16)

    return pl.pallas_call(
        body,
        grid_spec=pltpu.PrefetchScalarGridSpec(
            num_scalar_prefetch=1, grid=(a // ta,),
            in_specs=[pl.BlockSpec((1, ta, b), functools.partial(lambda i, k_ref, l: (l, i, 0), l=l)) for l in range(L)],
            out_specs=[pl.BlockSpec((1, ta, b), lambda i, k_ref: (k_ref[0], i, 0)) for _ in range(L)]),
        out_shape=[SDS((N_CHIPS, a, b), bf16)] * L,
        name="cast_slots", compiler_params=_cp("parallel"))(kidx, *([w] * L))


class _GatherComm:
    def __init__(self, bufs):
        self.inputs = list(bufs)
        self.out_shape = [SDS(b.shape, b.dtype) for b in bufs]
        self.aliased = True
        self.scratch = [pltpu.SemaphoreType.DMA((3 * len(bufs),))] * 4

    def _copies(self, i_refs, o_refs, sems):
        send1, recv1, send2, recv2 = sems
        x, y, c, chips = _place()
        k = 2 * x + y
        first, got1, second, got2 = [], [], [], []
        for i, buf in enumerate(self.inputs):
            h = buf.shape[1] // 2
            mine, theirs = pl.ds(c * h, h), pl.ds((1 - c) * h, h)
            for j, (cx, cy) in enumerate(chips):
                s = 3 * i + j
                first.append(pltpu.make_async_remote_copy(
                    src_ref=i_refs[i].at[k, mine], dst_ref=o_refs[i].at[k, mine], send_sem=send1.at[s],
                    recv_sem=recv1.at[s], device_id=(cx, cy, c), device_id_type=MESH))
                a = o_refs[i].at[2 * cx + cy, mine]
                got1.append(pltpu.make_async_remote_copy(
                    src_ref=a, dst_ref=a, send_sem=send1.at[s], recv_sem=recv1.at[s],
                    device_id=(cx, cy, c), device_id_type=MESH))
                second.append(pltpu.make_async_remote_copy(
                    src_ref=a, dst_ref=a, send_sem=send2.at[s], recv_sem=recv2.at[s],
                    device_id=(x, y, 1 - c), device_id_type=MESH))
                b = o_refs[i].at[2 * cx + cy, theirs]
                got2.append(pltpu.make_async_remote_copy(
                    src_ref=b, dst_ref=b, send_sem=send2.at[s], recv_sem=recv2.at[s],
                    device_id=(x, y, 1 - c), device_id_type=MESH))
        return first, got1, second, got2

    def start(self, i_refs, o_refs, sems):
        for cp in self._copies(i_refs, o_refs, sems)[0]:
            cp.start()

    def finish(self, i_refs, o_refs, sems):
        first, got1, second, got2 = self._copies(i_refs, o_refs, sems)
        for g, cp in zip(got1, second):
            g.wait_recv()
            cp.start()
        for g in got2:
            g.wait_recv()
        for cp in first + second:
            cp.wait_send()


class _PairExchangeComm:
    def __init__(self, gs):
        self.inputs = list(gs)
        self.out_shape = [SDS((g.shape[0], g.shape[1] // 2, g.shape[2]), g.dtype) for g in gs]
        self.aliased = False
        self.scratch = [pltpu.SemaphoreType.DMA((len(gs),))] * 2

    def _copies(self, i_refs, o_refs, sems):
        send, recv = sems
        x, y, c, _ = _place()
        cps = []
        for i, g in enumerate(self.inputs):
            h = g.shape[1] // 2
            cps.append(pltpu.make_async_remote_copy(
                src_ref=i_refs[i].at[:, pl.ds((1 - c) * h, h)], dst_ref=o_refs[i], send_sem=send.at[i], recv_sem=recv.at[i],
                device_id=(x, y, 1 - c), device_id_type=MESH))
        return cps

    def start(self, i_refs, o_refs, sems):
        for cp in self._copies(i_refs, o_refs, sems):
            cp.start()

    def finish(self, i_refs, o_refs, sems):
        for cp in self._copies(i_refs, o_refs, sems):
            cp.wait()


class _ChipExchangeComm:
    def __init__(self, qs):
        self.inputs = list(qs)
        self.out_shape = [SDS(q.shape, q.dtype) for q in qs]
        self.aliased = False
        self.scratch = [pltpu.SemaphoreType.DMA((3 * len(qs),))] * 2

    def _copies(self, i_refs, o_refs, sems):
        send, recv = sems
        x, y, c, chips = _place()
        k = 2 * x + y
        cps, got = [], []
        for i in range(len(self.inputs)):
            for j, (cx, cy) in enumerate(chips):
                s = 3 * i + j
                cps.append(pltpu.make_async_remote_copy(
                    src_ref=i_refs[i].at[2 * cx + cy], dst_ref=o_refs[i].at[k], send_sem=send.at[s],
                    recv_sem=recv.at[s], device_id=(cx, cy, c), device_id_type=MESH))
                a = o_refs[i].at[2 * cx + cy]
                got.append(pltpu.make_async_remote_copy(
                    src_ref=a, dst_ref=a, send_sem=send.at[s], recv_sem=recv.at[s],
                    device_id=(cx, cy, c), device_id_type=MESH))
        return cps, got

    def start(self, i_refs, o_refs, sems):
        for cp in self._copies(i_refs, o_refs, sems)[0]:
            cp.start()

    def finish(self, i_refs, o_refs, sems):
        cps, got = self._copies(i_refs, o_refs, sems)
        for g in got:
            g.wait_recv()
        for cp in cps:
            cp.wait_send()


def _comm_only(comm, name):
    n = len(comm.inputs)

    def body(*refs):
        i_refs, o_refs, sems = refs[:n], refs[n:n + len(comm.out_shape)], refs[n + len(comm.out_shape):]
        comm.start(i_refs, o_refs, sems)
        comm.finish(i_refs, o_refs, sems)

    return pl.pallas_call(
        body, out_shape=comm.out_shape, in_specs=[ANY] * n, out_specs=[ANY] * len(comm.out_shape),
        input_output_aliases={i: i for i in range(n)} if comm.aliased else {},
        scratch_shapes=comm.scratch, name=name,
        compiler_params=pltpu.CompilerParams(has_side_effects=True))(*comm.inputs)


def _host(body, n_in, n_out, n_scratch, comm, first, last):
    if comm is None:
        return body
    ci, co = len(comm.inputs), len(comm.out_shape)

    def wrapped(*refs):
        o = 0
        parts = []
        for n in (n_in, ci, n_out, co, n_scratch):
            parts.append(refs[o:o + n])
            o += n
        hin, cin, hout, cout, hs = parts
        sems = refs[o:]

        @pl.when(first())
        def _():
            comm.start(cin, cout, sems)
        body(*hin, *hout, *hs)

        @pl.when(last())
        def _():
            comm.finish(cin, cout, sems)
    return wrapped


def _host_specs(comm, n_in, n_out):
    if comm is None:
        return [], [], [], [], {}, []
    ci, co = len(comm.inputs), len(comm.out_shape)
    aliases = {n_in + i: n_out + i for i in range(ci)} if comm.aliased else {}
    return comm.inputs, [ANY] * ci, [ANY] * co, comm.out_shape, aliases, comm.scratch


def _pair_add(g, r, cidx, name):
    ns, a, b = g.shape
    h = a // 2
    th = h if h * b * 4 <= 4 * 1024 * 1024 else h // 2

    def body(c_ref, g_ref, r_ref, qf_ref, qb_ref):
        q = g_ref[...] + r_ref[...]
        qf_ref[...] = q
        qb_ref[...] = q.astype(bf16)

    nb = h // th
    spec = pl.BlockSpec((1, th, b), lambda s, i, c_ref: (s, i, 0))
    return pl.pallas_call(
        body,
        grid_spec=pltpu.PrefetchScalarGridSpec(
            num_scalar_prefetch=1, grid=(ns, nb),
            in_specs=[pl.BlockSpec((1, th, b), lambda s, i, c_ref: (s, c_ref[0] * nb + i, 0)), spec],
            out_specs=[spec, spec]),
        out_shape=[SDS((ns, h, b), f32), SDS((ns, h, b), bf16)],
        name=name, compiler_params=_cp("parallel", "parallel"))(cidx, g, r)


def _chip_add(qf, r2, idx, prev, L, name):
    ns, h, b = r2.shape
    th = h if h * b * 4 <= 4 * 1024 * 1024 else h // 2
    nb = h // th

    def body(s_ref, qf_ref, r1_ref, r2_ref, r3_ref, *rest):
        o_ref = rest[-1]
        o_ref[0] = qf_ref[0] + r1_ref[0].astype(f32) + r2_ref[0].astype(f32) + r3_ref[0].astype(f32)

    other = lambda d: pl.BlockSpec((1, th, b), lambda i, s_ref: ((s_ref[0] + d) % ns, i, 0))
    in_specs = [pl.BlockSpec((1, th, b), lambda i, s_ref: (s_ref[0], i, 0)), other(1), other(2), other(3)]
    args = [idx, qf, r2, r2, r2]
    aliases = {}
    if prev is not None:
        in_specs.append(ANY)
        args.append(prev)
        aliases = {5: 0}
    return pl.pallas_call(
        body,
        grid_spec=pltpu.PrefetchScalarGridSpec(
            num_scalar_prefetch=1, grid=(nb,), in_specs=in_specs,
            out_specs=pl.BlockSpec((1, th, b), lambda i, s_ref: (s_ref[2], s_ref[1] * nb + i, 0))),
        out_shape=SDS((L, 2 * h, b), f32), input_output_aliases=aliases,
        name=name, compiler_params=_cp("arbitrary"))(*args)


def _pair_share(gs, hs):
    n = len(gs)
    L = gs[0].shape[0]

    def body(*refs):
        i_refs, o_refs = refs[:n], refs[n:2 * n]
        send, recv = refs[2 * n:]
        x, y, c, _ = _place()
        cps = []
        for i in range(n):
            for l in range(L):
                mine = pl.ds(c * hs[i], hs[i])
                cp = pltpu.make_async_remote_copy(
                    src_ref=i_refs[i].at[l, mine], dst_ref=o_refs[i].at[l, mine], send_sem=send.at[i * L + l],
                    recv_sem=recv.at[i * L + l], device_id=(x, y, 1 - c), device_id_type=MESH)
                cp.start()
                cps.append(cp)
        for i in range(n):
            for l in range(L):
                got = o_refs[i].at[l, pl.ds((1 - c) * hs[i], hs[i])]
                pltpu.make_async_remote_copy(
                    src_ref=got, dst_ref=got, send_sem=send.at[i * L + l], recv_sem=recv.at[i * L + l],
                    device_id=(x, y, 1 - c), device_id_type=MESH).wait_recv()
        for cp in cps:
            cp.wait_send()

    return pl.pallas_call(
        body, out_shape=[SDS(g.shape, g.dtype) for g in gs], in_specs=[ANY] * n, out_specs=[ANY] * n,
        input_output_aliases={i: i for i in range(n)},
        scratch_shapes=[pltpu.SemaphoreType.DMA((n * L,))] * 2,
        name="grad_pair_share", compiler_params=pltpu.CompilerParams(has_side_effects=True))(*gs)


class _SwapComm:
    def __init__(self, arrays):
        self.inputs = list(arrays)
        self.out_shape = [SDS(a.shape, a.dtype) for a in arrays]
        self.aliased = False
        self.scratch = [pltpu.SemaphoreType.DMA((len(arrays),))] * 2

    def _copies(self, i_refs, o_refs, sems):
        send, recv = sems
        x, y, c, _ = _place()
        return [pltpu.make_async_remote_copy(
            src_ref=i_refs[i], dst_ref=o_refs[i], send_sem=send.at[i], recv_sem=recv.at[i],
            device_id=(x, y, 1 - c), device_id_type=MESH) for i in range(len(self.inputs))]

    def start(self, i_refs, o_refs, sems):
        for cp in self._copies(i_refs, o_refs, sems):
            cp.start()

    def finish(self, i_refs, o_refs, sems):
        for cp in self._copies(i_refs, o_refs, sems):
            cp.wait()


class _SlotShareComm:
    def __init__(self, bufs):
        self.inputs = list(bufs)
        self.out_shape = [SDS(b.shape, b.dtype) for b in bufs]
        self.aliased = True
        self.scratch = [pltpu.SemaphoreType.DMA((3 * len(bufs),))] * 2

    def _copies(self, i_refs, o_refs, sems):
        send, recv = sems
        x, y, c, chips = _place()
        k = 2 * x + y
        cps, got = [], []
        for i in range(len(self.inputs)):
            for j, (cx, cy) in enumerate(chips):
                s = 3 * i + j
                cps.append(pltpu.make_async_remote_copy(
                    src_ref=i_refs[i].at[k], dst_ref=o_refs[i].at[k], send_sem=send.at[s], recv_sem=recv.at[s],
                    device_id=(cx, cy, c), device_id_type=MESH))
                a = o_refs[i].at[2 * cx + cy]
                got.append(pltpu.make_async_remote_copy(
                    src_ref=a, dst_ref=a, send_sem=send.at[s], recv_sem=recv.at[s],
                    device_id=(cx, cy, c), device_id_type=MESH))
        return cps, got

    def start(self, i_refs, o_refs, sems):
        for cp in self._copies(i_refs, o_refs, sems)[0]:
            cp.start()

    def finish(self, i_refs, o_refs, sems):
        cps, got = self._copies(i_refs, o_refs, sems)
        for g in got:
            g.wait_recv()
        for cp in cps:
            cp.wait_send()


def _pair_sum_slot(mine, theirs, kidx):
    R, C = mine.shape

    def body(k_ref, a_ref, b_ref, o_ref):
        o_ref[0] = a_ref[...] + b_ref[...]

    spec = pl.BlockSpec((R, C), lambda i, k_ref: (0, 0))
    return pl.pallas_call(
        body,
        grid_spec=pltpu.PrefetchScalarGridSpec(
            num_scalar_prefetch=1, grid=(1,), in_specs=[spec, spec],
            out_specs=pl.BlockSpec((1, R, C), lambda i, k_ref: (k_ref[0], 0, 0))),
        out_shape=SDS((N_CHIPS, R, C), f32), name="small_pair_sum", compiler_params=_cp("arbitrary"))(kidx, mine, theirs)


def _small_sum(g):
    n, R, C = g.shape

    def body(g_ref, o_ref):
        acc = g_ref[0]
        for j in range(1, n):
            acc = acc + g_ref[j]
        o_ref[...] = acc
    return pl.pallas_call(body, out_shape=SDS((R, C), f32), name="small_sum", compiler_params=_cp())(g)


PACK_COLS = 1024


def _rows_of(shape):
    n = int(np.prod(shape)) if len(shape) else 1
    return -(-n // (8 * PACK_COLS)) * 8


def _pack(parts):
    blocks = []
    for p in parts:
        flat = p.reshape(-1)
        r = _rows_of(p.shape)
        blocks.append(jnp.pad(flat, (0, r * PACK_COLS - flat.shape[0])).reshape(r, PACK_COLS))
    return jnp.concatenate(blocks, axis=0)


def _unpack(buf, shapes):
    out, off = [], 0
    for s in shapes:
        n = int(np.prod(s)) if len(s) else 1
        r = _rows_of(s)
        out.append(buf[off:off + r].reshape(-1)[:n].reshape(s))
        off += r
    return out


def _block_diag(w):
    g, a, _ = w.shape
    out = jnp.zeros((g * a, g * a), w.dtype)
    for i in range(g):
        out = lax.dynamic_update_slice(out, w[i], (i * a, i * a))
    return out


def kernel(x, w_in, w_out, sgu_w, sgu_b, pool_w, pool_scale, swa_sinks, rel_bias, mix_out_gain, norm_mix, norm_ffn, w_gate_up, w_down, norm_final, loss_target, m_w_in, m_w_out, m_sgu_w, m_sgu_b, m_pool_w, m_pool_scale, m_swa_sinks, m_rel_bias, m_mix_out_gain, m_norm_mix, m_norm_ffn, m_w_gate_up, m_w_down, m_norm_final, v_w_in, v_w_out, v_sgu_w, v_sgu_b, v_pool_w, v_pool_scale, v_swa_sinks, v_rel_bias, v_mix_out_gain, v_norm_mix, v_norm_ffn, v_w_gate_up, v_w_down, v_norm_final):
    B, S, D = x.shape
    T = B * S
    L = w_in.shape[0]
    tm = min(512, T)
    F = w_down.shape[1] * N_CHIPS
    xi, yi, ci = lax.axis_index("x"), lax.axis_index("y"), lax.axis_index("c")
    cidx = jnp.reshape(ci, (1,)).astype(jnp.int32)
    kidx = jnp.reshape(2 * xi + yi, (1,)).astype(jnp.int32)

    big = [w_in, w_out, w_gate_up, w_down]
    slots = [_cast_slots(w, kidx) for w in big]
    gather = lambda pi, l: _GatherComm([slots[pi][l]])
    Win, Wo, Wgu, Wd = ([None] * L for _ in range(4))
    Win[0], = _comm_only(gather(0, 0), "gather_weights")

    bucket = jnp.asarray(_t5_bucket_table().reshape(1, -1))
    bias_tab = _bias_expand(rel_bias.T, bucket).reshape(4, BLK, 2 * BLK)

    row = lambda v: v.reshape(1, -1)
    xc = x.reshape(T, D)
    tgt = loss_target.reshape(T, D)
    saved = []
    for l in range(L):
        h1, proj, wo = _norm_mm(xc, row(norm_mix[l]), Win[l], tm, gather(1, l))
        Wo[l] = wo.reshape(D, D)
        bexp = jnp.repeat(sgu_b[l].T, HD, axis=1)
        wbd = _block_diag(pool_w[l])
        sk = jnp.broadcast_to(swa_sinks[l][:, None, None], (4, 1, BLK))
        ya = _sgu_fwd(proj, sgu_w[l], bexp, B, S)
        yb = _pool_fwd(proj, wbd, row(pool_scale[l]), B, S)
        yc = _swa_fwd(proj, sk, bias_tab, B, S)
        if l == 0:
            yd, lt, wd, Wgu[0] = _sb_fwd(proj, B, S, _GatherComm([slots[3][0], slots[2][0]]))
        else:
            yd, lt, wd = _sb_fwd(proj, B, S, gather(3, l))
        Wd[l] = wd.reshape(F, D)
        ys = (ya, yb, yc, yd)
        ycn, x1 = _gnorm_mm_res(ys, row(mix_out_gain[l]), Wo[l], xc, tm)
        if l + 1 < L:
            h2, gu, act, Wgu[l + 1] = _norm_mm_swiglu(x1, row(norm_ffn[l]), Wgu[l], tm, gather(2, l + 1))
            x2, Win[l + 1] = _mm_res(act, Wd[l], x1, tm, gather(0, l + 1))
        else:
            h2, gu, act = _norm_mm_swiglu(x1, row(norm_ffn[l]), Wgu[l], tm)
            x2, = _mm_res(act, Wd[l], x1, tm)
        saved.append((xc, h1, proj, bexp, wbd, sk, ys, lt, ycn, x1, h2, gu, act))
        xc = x2

    dx, g_final, loss_v = _final_loss(xc, row(norm_final), tgt, tm)

    tk = min(T, 2048)
    gW = [[None] * L for _ in range(4)]
    g_sgu_w, g_sgu_b, g_pool_w, g_pool_scale, g_sinks, g_bias = ([None] * L for _ in range(6))
    g_out_gain, g_mix, g_ffn = ([None] * L for _ in range(3))
    reduced = [None] * 4
    sums = {}

    def pair_comm(keys):
        return _PairExchangeComm([gW[pi][l] for pi, l in keys])

    def after_pair(keys, r1):
        for (pi, l), r in zip(keys, r1):
            sums[pi, l] = _pair_add(gW[pi][l], r, cidx, "grad_pair_add")

    def chip_comm(keys):
        return _ChipExchangeComm([sums[k][1] for k in keys])

    def after_chip(keys, r2):
        for (pi, l), r in zip(keys, r2):
            idx = jnp.stack([2 * xi + yi, ci, jnp.int32(l)]).astype(jnp.int32)
            reduced[pi] = _chip_add(sums.pop((pi, l))[0], r, idx, reduced[pi], L, "grad_chip_add")

    for l in reversed(range(L)):
        x0, h1, proj, bexp, wbd, sk, ys, lt, ycn, x1, h2, gu, act = saved[l]
        if l + 1 < L:
            dgu, *r2 = _dact(dx, Wd[l], gu, tm, chip_comm([(0, l + 1)]))
            after_chip([(0, l + 1)], r2)
        else:
            dgu, = _dact(dx, Wd[l], gu, tm)
        gW[3][l] = _dw(act, dx, lambda t, s: (t, 0), D, 1, F // 2, tk // 2, "dw_down").reshape(N_CHIPS, F // N_CHIPS, D)
        gW[2][l] = _dw(h2, dgu, lambda t, s: (s // 2, t, s % 2), F // 2, N_CHIPS, D, tk, "dw_gate_up")
        keys = [(2, l), (3, l)]
        dx1, g_ffn[l], *r1 = _dx_norm_bwd(dgu, Wgu[l], x1, row(norm_ffn[l]), dx, tm, "dx_ffn_exchange", pair_comm(keys))
        after_pair(keys, r1)
        gW[1][l] = _dw(ycn, dx1, lambda t, s: (t, 0), D, 1, D, tk, "dw_out").reshape(N_CHIPS, D // N_CHIPS, D)
        dya, dyb, dyc, dyd, g_out_gain[l], *r1 = _dycat(dx1, Wo[l], ys, row(mix_out_gain[l]), tm, pair_comm([(1, l)]))
        after_pair([(1, l)], r1)
        dpa, g_sgu_w[l], dbf, *r2 = _sgu_bwd(proj, sgu_w[l], bexp, dya, B, S, chip_comm([(1, l)]))
        after_chip([(1, l)], r2)
        g_sgu_b[l] = dbf[:, ::HD].T
        dpb, dwbd, dsc = _pool_bwd(proj, wbd, row(pool_scale[l]), dyb, B, S)
        npg = len(POOL_WINDOWS)
        g_pool_w[l] = jnp.stack([dwbd[i * HD:(i + 1) * HD, i * HD:(i + 1) * HD] for i in range(npg)])
        g_pool_scale[l] = dsc[0]
        dcq, dckv, dsk, g_bias[l], *r2 = _swa_bwd(proj, sk, bias_tab, dyc, B, S, chip_comm([(3, l)]))
        after_chip([(3, l)], r2)
        g_sinks[l] = dsk[:, 0, 0] * float(BLK)
        ddq, ddk, ddv, *r2 = _sb_bwd(proj, lt, dyd, B, S, chip_comm([(2, l)]))
        after_chip([(2, l)], r2)
        dproj = jnp.concatenate([dpa, dpb, dcq, dckv, ddq, ddk, ddv], axis=1)
        gW[0][l] = _dw(h1, dproj, lambda t, s: (t, s), w_in.shape[2], N_CHIPS, D, tk, "dw_in")
        dx, g_mix[l], *r1 = _dx_norm_bwd(dproj, Win[l], x0, row(norm_mix[l]), dx1, tm, "dx_mix_exchange", pair_comm([(0, l)]))
        after_pair([(0, l)], r1)
    grad_x = dx.reshape(B, S, D)

    after_chip([(0, 0)], _comm_only(chip_comm([(0, 0)]), "grad_chip_exchange"))
    g_big = _pair_share(reduced, [g.shape[1] // 2 for g in reduced])

    g_rel_bias = _bias_reduce([g.reshape(4, -1) for g in g_bias], bucket).T
    small_g = [jnp.stack(g_sgu_w), jnp.stack(g_sgu_b), jnp.stack(g_pool_w), jnp.stack(g_pool_scale), jnp.stack(g_sinks),
               g_rel_bias, jnp.concatenate(g_out_gain), jnp.concatenate(g_mix), jnp.concatenate(g_ffn), g_final[0]]
    small_w = [sgu_w, sgu_b, pool_w, pool_scale, swa_sinks, rel_bias, mix_out_gain, norm_mix, norm_ffn, norm_final]
    small_m = [m_sgu_w, m_sgu_b, m_pool_w, m_pool_scale, m_swa_sinks, m_rel_bias, m_mix_out_gain, m_norm_mix, m_norm_ffn, m_norm_final]
    small_v = [v_sgu_w, v_sgu_b, v_pool_w, v_pool_scale, v_swa_sinks, v_rel_bias, v_mix_out_gain, v_norm_mix, v_norm_ffn, v_norm_final]
    shapes = [w.shape for w in small_w]
    mine = _pack(small_g + [loss_v[0, 0:1]])
    theirs, = _comm_only(_SwapComm([mine]), "small_pair_swap")
    shared, = _comm_only(_SlotShareComm([_pair_sum_slot(mine, theirs, kidx)]), "small_chip_share")
    packed = _small_sum(shared)
    *g_small, loss = _unpack(packed, shapes + [()])
    g_small_packed = _pack(g_small)
    ds, ms, vs = _adamw(_pack(small_w), g_small_packed, _pack(small_m), _pack(small_v), g_small_packed.shape[0], "adamw_small")
    d_small, m_small, v_small = _unpack(ds, shapes), _unpack(ms, shapes), _unpack(vs, shapes)

    big_m = [m_w_in, m_w_out, m_w_gate_up, m_w_down]
    big_v = [v_w_in, v_w_out, v_w_gate_up, v_w_down]
    d_big, m_big, v_big = [], [], []
    for w, g, m, v in zip(big, g_big, big_m, big_v):
        two = lambda a: a.reshape(-1, a.shape[-1])
        rows = two(w).shape[0]
        d2, m2, v2 = _adamw(two(w), two(g), two(m), two(v), rows // 8 if rows >= 2048 else rows, "adamw_big")
        d_big.append(d2.reshape(w.shape))
        m_big.append(m2.reshape(w.shape))
        v_big.append(v2.reshape(w.shape))

    def order(bigs, smalls):
        return [bigs[0], bigs[1]] + list(smalls[0:9]) + [bigs[2], bigs[3], smalls[9]]

    return (loss, grad_x, *order(g_big, g_small), *order(d_big, d_small), *order(m_big, m_small), *order(v_big, v_small))
```

```python
import functools

import numpy as np
import jax
import jax.numpy as jnp
from jax import lax
from jax.experimental import pallas as pl
from jax.experimental.pallas import tpu as pltpu

f32 = jnp.float32
bf16 = jnp.bfloat16
_MXU = jnp.bfloat16

EPS = 1e-6
HD = 64
GW = 256
BLK = 128
SB_UNROLL = 2
SB_HEADS = 4
SB_CUT = -110.0
POOL_WINDOWS = (2, 4, 8, 16)
N_BUCKETS = 32
MAX_DISTANCE = 128
N_CHIPS = 4
N_DEV = 8
VMEM_LIMIT = 48 * 1024 * 1024

ADAM_LR = 0.001
ADAM_B1 = 0.9
ADAM_B2 = 0.999
ADAM_EPS = 1e-08
ADAM_WD = 0.01
ADAM_STEP = 10

SDS = jax.ShapeDtypeStruct
MESH = pl.DeviceIdType.MESH
HIGHEST = lax.Precision.HIGHEST
RESIDENT = pl.Buffered(1)
NT = (((1,), (1,)), ((), ()))
TN = (((0,), (0,)), ((), ()))


def _cp(*sem):
    return pltpu.CompilerParams(dimension_semantics=sem if sem else None, vmem_limit_bytes=VMEM_LIMIT)


def _mx(v):
    return v.astype(_MXU)


def _iota(shape, dim):
    return lax.broadcasted_iota(jnp.int32, shape, dim)


def _split_dot(a, tri):
    hi = a.astype(bf16)
    lo = (a - hi.astype(f32)).astype(bf16)
    return jnp.dot(hi, tri, preferred_element_type=f32) + jnp.dot(lo, tri, preferred_element_type=f32)


def _rms(xv):
    return lax.rsqrt(jnp.mean(xv * xv, axis=-1, keepdims=True) + EPS)


def _hosted_call(body, steps, in_specs, out_specs, out_shape, scratch, args, name, comm):
    n_in, n_out = len(in_specs), len(out_specs)
    c_args, c_in, c_out, c_shapes, aliases, c_scratch = _host_specs(comm, n_in, n_out)
    step = lambda v: (lambda: pl.program_id(0) == v)
    return pl.pallas_call(
        _host(body, n_in, n_out, len(scratch), comm, step(0), step(steps - 1)), grid=(steps,),
        in_specs=list(in_specs) + c_in, out_specs=list(out_specs) + c_out, out_shape=list(out_shape) + c_shapes,
        input_output_aliases=aliases, scratch_shapes=list(scratch) + c_scratch,
        name=name if comm is None else name + "_comm", compiler_params=_cp("arbitrary"))(*args, *c_args)


def _norm_mm(x, gain, w, tm, comm=None):
    T, D = x.shape
    NS, _, ns = w.shape

    def body(x_ref, g_ref, w_ref, h_ref, o_ref):
        xv = x_ref[...]
        h = (xv * _rms(xv) * g_ref[...]).astype(bf16)
        h_ref[...] = h
        for s in range(NS):
            o_ref[:, s * ns:(s + 1) * ns] = jnp.dot(_mx(h), w_ref[s], preferred_element_type=f32).astype(bf16)

    return _hosted_call(
        body, T // tm,
        [pl.BlockSpec((tm, D), lambda i: (i, 0)),
         pl.BlockSpec((1, D), lambda i: (0, 0)),
         pl.BlockSpec((NS, D, ns), lambda i: (0, 0, 0), pipeline_mode=RESIDENT)],
        [pl.BlockSpec((tm, D), lambda i: (i, 0)), pl.BlockSpec((tm, NS * ns), lambda i: (i, 0))],
        [SDS((T, D), bf16), SDS((T, NS * ns), bf16)], [], (x, gain, w), "norm_mm_in", comm)


def _norm_mm_swiglu(x, gain, w, tm, comm=None):
    T, D = x.shape
    NS, _, ns = w.shape
    half = NS // 2

    def body(x_ref, g_ref, w_ref, h_ref, gu_ref, a_ref):
        xv = x_ref[...]
        hb = (xv * _rms(xv) * g_ref[...]).astype(bf16)
        h_ref[...] = hb
        h = _mx(hb)
        for s in range(half):
            cols = slice(s * ns, (s + 1) * ns)
            g = jnp.dot(h, w_ref[s], preferred_element_type=f32)
            u = jnp.dot(h, w_ref[s + half], preferred_element_type=f32)
            gu_ref[0, :, cols] = g.astype(bf16)
            gu_ref[1, :, cols] = u.astype(bf16)
            a_ref[:, cols] = (jax.nn.silu(g) * u).astype(bf16)

    c_args, c_in, c_out, c_shapes, aliases, c_scratch = _host_specs(comm, 3, 3)
    step = lambda v: (lambda: pl.program_id(0) == v)
    return pl.pallas_call(
        _host(body, 3, 3, 0, comm, step(0), step(T // tm - 1)), grid=(T // tm,),
        in_specs=[pl.BlockSpec((tm, D), lambda i: (i, 0)),
                  pl.BlockSpec((1, D), lambda i: (0, 0)),
                  pl.BlockSpec((NS, D, ns), lambda i: (0, 0, 0), pipeline_mode=RESIDENT)] + c_in,
        out_specs=[pl.BlockSpec((tm, D), lambda i: (i, 0)),
                   pl.BlockSpec((2, tm, half * ns), lambda i: (0, i, 0)),
                   pl.BlockSpec((tm, half * ns), lambda i: (i, 0))] + c_out,
        out_shape=[SDS((T, D), bf16), SDS((2, T, half * ns), bf16), SDS((T, half * ns), bf16)] + c_shapes,
        input_output_aliases=aliases, scratch_shapes=c_scratch,
        name="norm_mm_swiglu" if comm is None else "norm_mm_swiglu_gather",
        compiler_params=_cp("arbitrary"))(x, gain, w, *c_args)


def _gnorm_mm_res(ys, gain, w, x, tm):
    T, D = x.shape

    def body(ya, yb, yc, yd, g_ref, w_ref, x_ref, yn_ref, o_ref):
        parts = []
        for m, r in enumerate((ya, yb, yc, yd)):
            y = r[...].astype(f32)
            parts.append((y * _rms(y) * g_ref[:, m * GW:(m + 1) * GW]).astype(bf16))
        yn = jnp.concatenate(parts, axis=1)
        yn_ref[...] = yn
        o_ref[...] = x_ref[...] + jnp.dot(_mx(yn), w_ref[...], preferred_element_type=f32)

    yspec = pl.BlockSpec((tm, GW), lambda i: (i, 0))
    return pl.pallas_call(
        body, grid=(T // tm,),
        in_specs=[yspec, yspec, yspec, yspec,
                  pl.BlockSpec((1, D), lambda i: (0, 0)),
                  pl.BlockSpec((D, D), lambda i: (0, 0)),
                  pl.BlockSpec((tm, D), lambda i: (i, 0))],
        out_specs=[pl.BlockSpec((tm, D), lambda i: (i, 0)), pl.BlockSpec((tm, D), lambda i: (i, 0))],
        out_shape=[SDS((T, D), bf16), SDS((T, D), f32)],
        name="gnorm_mm_res", compiler_params=_cp("parallel"))(*ys, gain, w, x)


def _mm_res(a, w, x, tm, comm=None):
    T, D = x.shape
    K = a.shape[1]

    def body(a_ref, w_ref, x_ref, o_ref):
        o_ref[...] = x_ref[...] + jnp.dot(_mx(a_ref[...]), w_ref[...], preferred_element_type=f32)

    return _hosted_call(
        body, T // tm,
        [pl.BlockSpec((tm, K), lambda i: (i, 0)),
         pl.BlockSpec((K, D), lambda i: (0, 0), pipeline_mode=RESIDENT),
         pl.BlockSpec((tm, D), lambda i: (i, 0))],
        [pl.BlockSpec((tm, D), lambda i: (i, 0))], [SDS((T, D), f32)], [], (a, w, x), "mm_res_down", comm)


def _final_loss(x, gain, tgt, tm):
    T, D = x.shape

    def body(x_ref, g_ref, t_ref, dx_ref, dg_ref, l_ref):
        @pl.when(pl.program_id(0) == 0)
        def _():
            dg_ref[...] = jnp.zeros_like(dg_ref)
            l_ref[...] = jnp.zeros_like(l_ref)
        xv = x_ref[...]
        g = g_ref[...]
        r = _rms(xv)
        xh = xv * r
        err = xh * g - t_ref[...]
        l_ref[...] += 0.5 * jnp.sum(jnp.mean(err * err, axis=-1, keepdims=True), axis=0, keepdims=True)
        dy = err * (1.0 / D)
        dg_ref[...] += jnp.sum(dy * xh, axis=0, keepdims=True)
        dxh = dy * g
        dx_ref[...] = r * (dxh - xh * jnp.mean(dxh * xh, axis=-1, keepdims=True))

    return pl.pallas_call(
        body, grid=(T // tm,),
        in_specs=[pl.BlockSpec((tm, D), lambda i: (i, 0)),
                  pl.BlockSpec((1, D), lambda i: (0, 0)),
                  pl.BlockSpec((tm, D), lambda i: (i, 0))],
        out_specs=[pl.BlockSpec((tm, D), lambda i: (i, 0)),
                   pl.BlockSpec((1, D), lambda i: (0, 0)),
                   pl.BlockSpec((1, BLK), lambda i: (0, 0))],
        out_shape=[SDS((T, D), f32), SDS((1, D), f32), SDS((1, BLK), f32)],
        name="final_loss", compiler_params=_cp("arbitrary"))(x, gain, tgt)


def _dact(dx, wd, gu, tm, comm=None):
    T, D = dx.shape
    F = wd.shape[0]
    ns = F // 2

    def body(dx_ref, w_ref, gu_ref, o_ref):
        dxb = _mx(dx_ref[...])
        for s in range(2):
            cols = slice(s * ns, (s + 1) * ns)
            da = lax.dot_general(dxb, w_ref[s * ns:(s + 1) * ns, :], NT, preferred_element_type=f32)
            g = gu_ref[0, :, cols].astype(f32)
            u = gu_ref[1, :, cols].astype(f32)
            sg = jax.nn.sigmoid(g)
            o_ref[0, :, cols] = (da * u * (sg * (1.0 + g * (1.0 - sg)))).astype(bf16)
            o_ref[1, :, cols] = (da * (g * sg)).astype(bf16)

    return _hosted_call(
        body, T // tm,
        [pl.BlockSpec((tm, D), lambda i: (i, 0)),
         pl.BlockSpec((F, D), lambda i: (0, 0), pipeline_mode=RESIDENT),
         pl.BlockSpec((2, tm, F), lambda i: (0, i, 0))],
        [pl.BlockSpec((2, tm, F), lambda i: (0, i, 0))], [SDS((2, T, F), bf16)], [], (dx, wd, gu), "dact", comm)


def _dw(a, b, b_map, ns, NS, tka, tk, name):
    T, Ka = a.shape
    b_block = (tk, ns) if b.ndim == 2 else (1, tk, ns)

    def body(a_ref, b_ref, o_ref):
        bv = b_ref[...] if b.ndim == 2 else b_ref[0]
        part = lax.dot_general(_mx(a_ref[...]), _mx(bv), TN, preferred_element_type=f32)

        @pl.when(pl.program_id(2) == 0)
        def _():
            o_ref[0] = part

        @pl.when(pl.program_id(2) > 0)
        def _():
            o_ref[0] += part

    return pl.pallas_call(
        body, grid=(NS, Ka // tka, T // tk),
        in_specs=[pl.BlockSpec((tk, tka), lambda s, k, t: (t, k)),
                  pl.BlockSpec(b_block, lambda s, k, t: b_map(t, s))],
        out_specs=pl.BlockSpec((1, tka, ns), lambda s, k, t: (s, k, 0)),
        out_shape=SDS((NS, Ka, ns), f32),
        name=name, compiler_params=_cp("parallel", "parallel", "arbitrary"))(a, b)


def _dx_norm_bwd(dy, w, x, gain, dxin, tm, name, comm=None):
    T, D = x.shape
    NS, _, ns = w.shape
    half = NS // 2

    def body(dy_ref, w_ref, x_ref, g_ref, dxin_ref, dx_ref, dg_ref):
        @pl.when(pl.program_id(0) == 0)
        def _():
            dg_ref[...] = jnp.zeros_like(dg_ref)
        dh = None
        for s in range(NS):
            if dy.ndim == 2:
                dv = dy_ref[:, s * ns:(s + 1) * ns]
            else:
                dv = dy_ref[s // half, :, (s % half) * ns:(s % half + 1) * ns]
            part = lax.dot_general(_mx(dv), w_ref[s], NT, preferred_element_type=f32)
            dh = part if dh is None else dh + part
        xv = x_ref[...]
        r = _rms(xv)
        xh = xv * r
        dg_ref[...] += jnp.sum(dh * xh, axis=0, keepdims=True)
        dxh = dh * g_ref[...]
        dx_ref[...] = dxin_ref[...] + r * (dxh - xh * jnp.mean(dxh * xh, axis=-1, keepdims=True))

    dy_spec = (pl.BlockSpec((tm, NS * ns), lambda i: (i, 0)) if dy.ndim == 2
               else pl.BlockSpec((2, tm, half * ns), lambda i: (0, i, 0)))
    c_args, c_in, c_out, c_shapes, aliases, c_scratch = _host_specs(comm, 5, 2)
    step = lambda v: (lambda: pl.program_id(0) == v)
    return pl.pallas_call(
        _host(body, 5, 2, 0, comm, step(0), step(T // tm - 1)), grid=(T // tm,),
        in_specs=[dy_spec,
                  pl.BlockSpec((NS, D, ns), lambda i: (0, 0, 0), pipeline_mode=RESIDENT),
                  pl.BlockSpec((tm, D), lambda i: (i, 0)),
                  pl.BlockSpec((1, D), lambda i: (0, 0)),
                  pl.BlockSpec((tm, D), lambda i: (i, 0))] + c_in,
        out_specs=[pl.BlockSpec((tm, D), lambda i: (i, 0)),
                   pl.BlockSpec((1, D), lambda i: (0, 0))] + c_out,
        out_shape=[SDS((T, D), f32), SDS((1, D), f32)] + c_shapes,
        input_output_aliases=aliases, scratch_shapes=c_scratch,
        name=name, compiler_params=_cp("arbitrary"))(dy, w, x, gain, dxin, *c_args)


def _dycat(dx, w, ys, gain, tm, comm=None):
    T, D = dx.shape

    def body(dx_ref, w_ref, ya, yb, yc, yd, g_ref, da, db, dc, dd, dg_ref):
        @pl.when(pl.program_id(0) == 0)
        def _():
            dg_ref[...] = jnp.zeros_like(dg_ref)
        dyn = lax.dot_general(_mx(dx_ref[...]), w_ref[...], NT, preferred_element_type=f32)
        for m, (r, o) in enumerate(((ya, da), (yb, db), (yc, dc), (yd, dd))):
            cols = slice(m * GW, (m + 1) * GW)
            y = r[...].astype(f32)
            rs = _rms(y)
            yh = y * rs
            d = dyn[:, cols]
            dg_ref[:, cols] += jnp.sum(d * yh, axis=0, keepdims=True)
            dyh = d * g_ref[:, cols]
            o[...] = (rs * (dyh - yh * jnp.mean(dyh * yh, axis=-1, keepdims=True))).astype(bf16)

    yspec = pl.BlockSpec((tm, GW), lambda i: (i, 0))
    return _hosted_call(
        body, T // tm,
        [pl.BlockSpec((tm, D), lambda i: (i, 0)),
         pl.BlockSpec((D, D), lambda i: (0, 0), pipeline_mode=RESIDENT),
         yspec, yspec, yspec, yspec,
         pl.BlockSpec((1, D), lambda i: (0, 0))],
        [yspec, yspec, yspec, yspec, pl.BlockSpec((1, D), lambda i: (0, 0))],
        [SDS((T, GW), bf16)] * 4 + [SDS((1, D), f32)], [], (dx, w, *ys, gain), "dycat", comm)


def _sgu_consts():
    r, c = _iota((GW, GW), 0), _iota((GW, GW), 1)
    seg = (r // HD == c // HD).astype(f32)
    tr, ts = _iota((BLK, BLK), 0), _iota((BLK, BLK), 1)
    causal = ts <= tr
    lane_head = _iota((BLK, GW), 1) // HD
    return seg, causal, lane_head


def _split3_dot(a, ones):
    hi = a.astype(bf16)
    r1 = a - hi.astype(f32)
    mid = r1.astype(bf16)
    lo = (r1 - mid.astype(f32)).astype(bf16)
    dot = functools.partial(jnp.dot, preferred_element_type=f32)
    return dot(hi, ones) + dot(mid, ones) + dot(lo, ones)


def _sgu_chunks(aus, avs, w, bexp, consts):
    seg, causal, lane_head = consts
    segb = seg.astype(bf16)
    nh = GW // HD
    vs = [jax.nn.gelu(av) for av in avs]
    mus = [_split3_dot(v, segb) * (1.0 / HD) for v in vs]
    vcs = [v - mu for v, mu in zip(vs, mus)]
    vars_ = [_split3_dot(vc * vc, segb) * (1.0 / HD) for vc in vcs]
    vns = [_mx(vc * lax.rsqrt(var + EPS)) for vc, var in zip(vcs, vars_)]
    whs = [_mx(jnp.where(causal, w[h], 0.0)) for h in range(nh)]
    mixes = [[jnp.dot(whs[h], vn, preferred_element_type=f32) for h in range(nh)] for vn in vns]
    out = []
    for au, ms in zip(aus, mixes):
        mix = bexp
        for h in range(nh):
            mix = mix + jnp.where(lane_head == h, ms[h], 0.0)
        out.append(jax.nn.gelu(au) * mix)
    return out


def _sgu_group(S):
    nc = S // BLK
    return 4 if nc % 4 == 0 else (2 if nc % 2 == 0 else 1)


def _sgu_fwd(proj, w, bexp, B, S):
    G = _sgu_group(S)

    def body(au_ref, av_ref, w_ref, b_ref, y_ref):
        consts = _sgu_consts()
        wv, bv = w_ref[...], b_ref[...]

        def group(n, c):
            rows = [pl.ds(pl.multiple_of((n * G + j) * BLK, BLK), BLK) for j in range(G)]
            ys = _sgu_chunks([au_ref[r, :].astype(f32) for r in rows], [av_ref[r, :].astype(f32) for r in rows],
                             wv, bv, consts)
            for r, y in zip(rows, ys):
                y_ref[r, :] = y.astype(bf16)
            return c
        lax.fori_loop(0, S // BLK // G, group, 0)

    return pl.pallas_call(
        body, grid=(B,),
        in_specs=[pl.BlockSpec((S, GW), lambda b: (b, 0)),
                  pl.BlockSpec((S, GW), lambda b: (b, 1)),
                  pl.BlockSpec((GW // HD, BLK, BLK), lambda b: (0, 0, 0)),
                  pl.BlockSpec((BLK, GW), lambda b: (0, 0))],
        out_specs=pl.BlockSpec((S, GW), lambda b: (b, 0)),
        out_shape=SDS((B * S, GW), bf16),
        name="sgu_fwd", compiler_params=_cp("parallel"))(proj, proj, w, bexp)


def _sgu_bwd(proj, w, bexp, dy, B, S, comm=None):
    def body(au_ref, av_ref, w_ref, b_ref, dy_ref, dp_ref, dw_ref, db_ref):
        @pl.when(pl.program_id(0) == 0)
        def _():
            dw_ref[...] = jnp.zeros_like(dw_ref)
            db_ref[...] = jnp.zeros_like(db_ref)
        consts = _sgu_consts()
        wv, bv = w_ref[...], b_ref[...]
        fn = lambda aus, avs, ww, bb: _sgu_chunks(aus, avs, ww, bb, consts)
        G = _sgu_group(S)

        def group(n, carry):
            dw_acc, db_acc = carry
            rows = [pl.ds(pl.multiple_of((n * G + j) * BLK, BLK), BLK) for j in range(G)]
            _, vjp = jax.vjp(fn, [au_ref[r, :].astype(f32) for r in rows], [av_ref[r, :].astype(f32) for r in rows], wv, bv)
            daus, davs, dwc, dbc = vjp([dy_ref[r, :].astype(f32) for r in rows])
            for r, dau, dav in zip(rows, daus, davs):
                dp_ref[r, 0:GW] = dau.astype(bf16)
                dp_ref[r, GW:2 * GW] = dav.astype(bf16)
            return dw_acc + dwc, db_acc + dbc
        dw_acc, db_acc = lax.fori_loop(0, S // BLK // G, group, (jnp.zeros(wv.shape, f32), jnp.zeros(bv.shape, f32)))
        dw_ref[...] += dw_acc
        db_ref[...] += jnp.dot(db_acc, consts[0], precision=HIGHEST, preferred_element_type=f32)

    return _hosted_call(
        body, B,
        [pl.BlockSpec((S, GW), lambda b: (b, 0)),
         pl.BlockSpec((S, GW), lambda b: (b, 1)),
         pl.BlockSpec((GW // HD, BLK, BLK), lambda b: (0, 0, 0)),
         pl.BlockSpec((BLK, GW), lambda b: (0, 0)),
         pl.BlockSpec((S, GW), lambda b: (b, 0))],
        [pl.BlockSpec((S, 2 * GW), lambda b: (b, 0)),
         pl.BlockSpec((GW // HD, BLK, BLK), lambda b: (0, 0, 0)),
         pl.BlockSpec((BLK, GW), lambda b: (0, 0))],
        [SDS((B * S, 2 * GW), bf16), SDS((GW // HD, BLK, BLK), f32), SDS((BLK, GW), f32)], [],
        (proj, proj, w, bexp, dy), "sgu_bwd", comm)


def _pool_parts(p):
    n = p.shape[0]
    r = _iota(p.shape, 0)
    lg = _iota(p.shape, 1) // HD

    def sh(v, k):
        return jnp.where(r >= k, pltpu.roll(v, k, 0), 0.0)
    s2 = p + sh(p, 1)
    s4 = s2 + sh(s2, 2)
    s8 = s4 + sh(s4, 4)
    s16 = s8 + sh(s8, 8)
    ws = jnp.where(lg == 0, s2, jnp.where(lg == 1, s4, jnp.where(lg == 2, s8, s16)))
    wlen = jnp.where(lg == 0, 2, jnp.where(lg == 1, 4, jnp.where(lg == 2, 8, 16)))
    cnt = jnp.minimum(r + 1, wlen).astype(f32)
    del n
    return ws / cnt - p, cnt, lg


def _pool_fwd(proj, wbd, scale, B, S):
    def body(p_ref, w_ref, s_ref, y_ref):
        y, _, _ = _pool_parts(p_ref[...].astype(f32))
        y_ref[...] = (jnp.dot(_mx(y), _mx(w_ref[...]), preferred_element_type=f32) * s_ref[...]).astype(bf16)

    return pl.pallas_call(
        body, grid=(B,),
        in_specs=[pl.BlockSpec((S, GW), lambda b: (b, 2)),
                  pl.BlockSpec((GW, GW), lambda b: (0, 0)),
                  pl.BlockSpec((1, GW), lambda b: (0, 0))],
        out_specs=pl.BlockSpec((S, GW), lambda b: (b, 0)),
        out_shape=SDS((B * S, GW), bf16),
        name="pool_fwd", compiler_params=_cp("parallel"))(proj, wbd, scale)


def _pool_bwd(proj, wbd, scale, dy, B, S):
    def body(p_ref, w_ref, s_ref, dy_ref, dp_ref, dw_ref, ds_ref):
        @pl.when(pl.program_id(0) == 0)
        def _():
            dw_ref[...] = jnp.zeros_like(dw_ref)
            ds_ref[...] = jnp.zeros_like(ds_ref)
        y, cnt, lg = _pool_parts(p_ref[...].astype(f32))
        wv = _mx(w_ref[...])
        z = jnp.dot(_mx(y), wv, preferred_element_type=f32)
        dout = dy_ref[...].astype(f32)
        ds_ref[...] += jnp.sum(dout * z, axis=0, keepdims=True)
        dz = _mx(dout * s_ref[...])
        dw_ref[...] += lax.dot_general(_mx(y), dz, TN, preferred_element_type=f32)
        dyv = lax.dot_general(dz, wv, NT, preferred_element_type=f32)
        n = dyv.shape[0]
        r = _iota(dyv.shape, 0)

        def ush(v, k):
            return jnp.where(r < n - k, pltpu.roll(v, n - k, 0), 0.0)
        gq = dyv / cnt
        a2 = gq + ush(gq, 1)
        a4 = a2 + ush(a2, 2)
        a8 = a4 + ush(a4, 4)
        a16 = a8 + ush(a8, 8)
        adj = jnp.where(lg == 0, a2, jnp.where(lg == 1, a4, jnp.where(lg == 2, a8, a16)))
        dp_ref[...] = (adj - dyv).astype(bf16)

    return pl.pallas_call(
        body, grid=(B,),
        in_specs=[pl.BlockSpec((S, GW), lambda b: (b, 2)),
                  pl.BlockSpec((GW, GW), lambda b: (0, 0)),
                  pl.BlockSpec((1, GW), lambda b: (0, 0)),
                  pl.BlockSpec((S, GW), lambda b: (b, 0))],
        out_specs=[pl.BlockSpec((S, GW), lambda b: (b, 0)),
                   pl.BlockSpec((GW, GW), lambda b: (0, 0)),
                   pl.BlockSpec((1, GW), lambda b: (0, 0))],
        out_shape=[SDS((B * S, GW), bf16), SDS((GW, GW), f32), SDS((1, GW), f32)],
        name="pool_bwd", compiler_params=_cp("arbitrary"))(proj, wbd, scale, dy)


def _t5_bucket_table():
    dist = (np.arange(BLK)[:, None] + BLK) - np.arange(2 * BLK)[None, :]
    d = np.clip(dist, 0, BLK - 1)
    max_exact = N_BUCKETS // 2
    df = np.maximum(d, 1).astype(np.float32)
    large = max_exact + (np.log(df / max_exact) / np.float32(np.log(MAX_DISTANCE / max_exact))
                         * (N_BUCKETS - max_exact)).astype(np.int32)
    large = np.minimum(large, N_BUCKETS - 1)
    return np.where(d < max_exact, d, large).astype(np.int32)


def _swa_block(qb0, qb1, k2, v2, sinks, biases, n):
    heads = [(p, g) for p in range(2) for g in range(2)]
    ri, ci = _iota((BLK, BLK), 0), _iota((BLK, BLK), 1)
    qi, ki = _iota((BLK, 2 * BLK), 0), _iota((BLK, 2 * BLK), 1)
    dist = qi + BLK - ki
    mask = (dist >= 0) & (dist < BLK) & ((ki >= BLK) | (n > 0))
    qbs, kb, vb = (_mx(qb0), _mx(qb1)), _mx(k2), _mx(v2)
    qs, vs = [], []
    for p, g in heads:
        selq = ((ri - g * HD == ci - p * HD) & (ri >= g * HD) & (ri < (g + 1) * HD)).astype(_MXU)
        selv = ((ci - g * HD == ri - p * HD) & (ci >= g * HD) & (ci < (g + 1) * HD)).astype(_MXU)
        qs.append(_mx(jnp.dot(qbs[p], selq, preferred_element_type=f32)))
        vs.append(_mx(jnp.dot(vb, selv, preferred_element_type=f32)))
    zs = [lax.dot_general(q, kb, NT, preferred_element_type=f32) * (HD ** -0.5) for q in qs]
    prs = []
    for h in range(4):
        z = jnp.where(mask, zs[h] + biases[h], -1e30)
        s = jnp.mean(sinks[h], axis=-1, keepdims=True)
        m = jnp.maximum(jnp.max(z, axis=-1, keepdims=True), s)
        e = jnp.exp(z - m)
        prs.append(_mx(e / (jnp.sum(e, axis=-1, keepdims=True) + jnp.exp(s - m))))
    outs = [jnp.dot(prs[h], vs[h], preferred_element_type=f32) for h in range(4)]
    return outs[0] + outs[1], outs[2] + outs[3]


def _swa_fwd(proj, sinks, bias, B, S):
    def body(q_ref, kv_ref, s_ref, b_ref, y_ref):
        def block(n, c):
            rows = pl.ds(pl.multiple_of(n * BLK, BLK), BLK)
            prev = pl.ds(pl.multiple_of(jnp.maximum(n - 1, 0) * BLK, BLK), BLK)
            k2 = jnp.concatenate([kv_ref[prev, 0:BLK], kv_ref[rows, 0:BLK]], axis=0).astype(f32)
            v2 = jnp.concatenate([kv_ref[prev, BLK:2 * BLK], kv_ref[rows, BLK:2 * BLK]], axis=0).astype(f32)
            o0, o1 = _swa_block(q_ref[rows, 0:BLK].astype(f32), q_ref[rows, BLK:2 * BLK].astype(f32), k2, v2,
                                [s_ref[h] for h in range(4)], [b_ref[h] for h in range(4)], n)
            y_ref[rows, 0:BLK] = o0.astype(bf16)
            y_ref[rows, BLK:2 * BLK] = o1.astype(bf16)
            return c
        lax.fori_loop(0, S // BLK, block, 0)

    return pl.pallas_call(
        body, grid=(B,),
        in_specs=[pl.BlockSpec((S, GW), lambda b: (b, 3)),
                  pl.BlockSpec((S, GW), lambda b: (b, 4)),
                  pl.BlockSpec((4, 1, BLK), lambda b: (0, 0, 0)),
                  pl.BlockSpec((4, BLK, 2 * BLK), lambda b: (0, 0, 0))],
        out_specs=pl.BlockSpec((S, GW), lambda b: (b, 0)),
        out_shape=SDS((B * S, GW), bf16),
        name="swa_fwd", compiler_params=_cp("parallel"))(proj, proj, sinks, bias)


def _swa_bwd(proj, sinks, bias, dy, B, S, comm=None):
    def body(q_ref, kv_ref, s_ref, b_ref, dy_ref, dq_ref, dkv_ref, ds_ref, db_ref, acc_ref):
        @pl.when(pl.program_id(0) == 0)
        def _():
            ds_ref[...] = jnp.zeros_like(ds_ref)
            db_ref[...] = jnp.zeros_like(db_ref)
        acc_ref[...] = jnp.zeros_like(acc_ref)

        def block(n, c):
            rows = pl.ds(pl.multiple_of(n * BLK, BLK), BLK)
            prev = pl.ds(pl.multiple_of(jnp.maximum(n - 1, 0) * BLK, BLK), BLK)
            k2 = jnp.concatenate([kv_ref[prev, 0:BLK], kv_ref[rows, 0:BLK]], axis=0).astype(f32)
            v2 = jnp.concatenate([kv_ref[prev, BLK:2 * BLK], kv_ref[rows, BLK:2 * BLK]], axis=0).astype(f32)
            fn = functools.partial(_swa_block, n=n)
            _, vjp = jax.vjp(fn, q_ref[rows, 0:BLK].astype(f32), q_ref[rows, BLK:2 * BLK].astype(f32), k2, v2,
                             [s_ref[h] for h in range(4)], [b_ref[h] for h in range(4)])
            dq0, dq1, dk2, dv2, dss, dbs = vjp((dy_ref[rows, 0:BLK].astype(f32), dy_ref[rows, BLK:2 * BLK].astype(f32)))
            dq_ref[rows, 0:BLK] = dq0.astype(bf16)
            dq_ref[rows, BLK:2 * BLK] = dq1.astype(bf16)
            for h in range(4):
                ds_ref[h] += dss[h]
                db_ref[h] += dbs[h]
            acc_ref[prev, 0:BLK] += dk2[0:BLK]
            acc_ref[rows, 0:BLK] += dk2[BLK:2 * BLK]
            acc_ref[prev, BLK:2 * BLK] += dv2[0:BLK]
            acc_ref[rows, BLK:2 * BLK] += dv2[BLK:2 * BLK]
            return c
        lax.fori_loop(0, S // BLK, block, 0)
        dkv_ref[...] = acc_ref[...].astype(bf16)

    c_args, c_in, c_out, c_shapes, aliases, c_scratch = _host_specs(comm, 5, 4)
    step = lambda v: (lambda: pl.program_id(0) == v)
    return pl.pallas_call(
        _host(body, 5, 4, 1, comm, step(0), step(B - 1)), grid=(B,),
        in_specs=[pl.BlockSpec((S, GW), lambda b: (b, 3)),
                  pl.BlockSpec((S, GW), lambda b: (b, 4)),
                  pl.BlockSpec((4, 1, BLK), lambda b: (0, 0, 0)),
                  pl.BlockSpec((4, BLK, 2 * BLK), lambda b: (0, 0, 0)),
                  pl.BlockSpec((S, GW), lambda b: (b, 0))] + c_in,
        out_specs=[pl.BlockSpec((S, GW), lambda b: (b, 0)),
                   pl.BlockSpec((S, GW), lambda b: (b, 0)),
                   pl.BlockSpec((4, 1, BLK), lambda b: (0, 0, 0)),
                   pl.BlockSpec((4, BLK, 2 * BLK), lambda b: (0, 0, 0))] + c_out,
        out_shape=[SDS((B * S, GW), bf16), SDS((B * S, GW), bf16), SDS((4, 1, BLK), f32),
                   SDS((4, BLK, 2 * BLK), f32)] + c_shapes,
        input_output_aliases=aliases, scratch_shapes=[pltpu.VMEM((S, GW), f32)] + c_scratch,
        name="swa_bwd" if comm is None else "swa_bwd_exchange",
        compiler_params=_cp("arbitrary"))(proj, proj, sinks, bias, dy, *c_args)


def _log1m_parts(z):
    t = jnp.exp(-jnp.abs(z))
    return jnp.minimum(-z, 0.0) - jnp.log(1.0 + t), t


def _log1m(z):
    return _log1m_parts(z)[0]


def _sigmoid_from(z, t):
    return jnp.where(z >= 0.0, 1.0, t) / (1.0 + t)


def _sb_consts(tri):
    r2, c2 = _iota((2 * BLK, 2 * BLK), 0), _iota((2 * BLK, 2 * BLK), 1)
    tri2 = (tri(r2, c2) & (r2 // BLK == c2 // BLK)).astype(bf16)
    ri, ci = _iota((BLK, 2 * BLK), 0), _iota((BLK, 2 * BLK), 1)
    strict2 = (ci % BLK) < ri
    head0 = _iota((BLK, BLK), 1) < HD
    return tri2, strict2, head0


def _sb_stack_kv(k_ref, v_ref, kst_ref, vst_ref, head0, nb):
    def one(kb, c):
        krows = pl.ds(pl.multiple_of(kb * BLK, BLK), BLK)
        for p in range(2):
            for src, dst in ((k_ref, kst_ref), (v_ref, vst_ref)):
                t = src[krows, p * BLK:(p + 1) * BLK]
                dst[p, kb] = _mx(jnp.concatenate([jnp.where(head0, t, 0.0), jnp.where(head0, 0.0, t)], axis=0))
        return c
    lax.fori_loop(0, nb, one, 0)


def _sb_load_kv(kst_ref, vst_ref, kb):
    return [kst_ref[p, kb] for p in range(2)], [vst_ref[p, kb] for p in range(2)]


def _two_halves(a, b):
    return jnp.concatenate([jnp.broadcast_to(a, (BLK, BLK)), jnp.broadcast_to(b, (BLK, BLK))], axis=1)


def _half_sums(t):
    return jnp.sum(t[:, :BLK], axis=-1, keepdims=True), jnp.sum(t[:, BLK:], axis=-1, keepdims=True)


def _sb_fwd(proj, B, S, comm=None):
    def body(q_ref, k_ref, v_ref, y_ref, lt_ref, kst_ref, vst_ref):
        ci = _iota((BLK, BLK), 1)
        above2, strict2, head0 = _sb_consts(lambda r, c: r > c)
        _sb_stack_kv(k_ref, v_ref, kst_ref, vst_ref, head0, S // BLK)

        def step(qs, kbs, carry, diag):
            ok = [None if diag else kb >= 0 for kb in kbs]
            kv = [_sb_load_kv(kst_ref, vst_ref, jnp.maximum(kb, 0)) for kb in kbs]
            zs = [[lax.dot_general(qs[p], kks[p], NT, preferred_element_type=f32) for p in range(2)] for kks, _ in kv]
            Ls = [[_log1m(z) for z in zu] for zu in zs]
            if diag:
                Ls = [[jnp.where(strict2, L, 0.0) for L in Lu] for Lu in Ls]
            tails = [[_split_dot(L, above2) for L in Lu] for Lu in Ls]
            carry = list(carry)
            for u in range(len(kbs)):
                for p in range(2):
                    R0, R1, acc = carry[3 * p:3 * p + 3]
                    w = jnp.exp(zs[u][p] + Ls[u][p] + tails[u][p] + _two_halves(R0, R1))
                    s0, s1 = _half_sums(Ls[u][p])
                    if diag:
                        w = jnp.where(strict2, w, 0.0)
                    else:
                        w, s0, s1 = (jnp.where(ok[u], t, 0.0) for t in (w, s0, s1))
                    acc = acc + jnp.dot(_mx(w), kv[u][1][p], preferred_element_type=f32)
                    carry[3 * p:3 * p + 3] = [R0 + s0, R1 + s1, acc]
            return tuple(carry)

        def qblock(n, c):
            qrows = pl.ds(pl.multiple_of(n * BLK, BLK), BLK)
            qs = [_mx(q_ref[qrows, p * BLK:(p + 1) * BLK] * (HD ** -0.5)) for p in range(2)]
            z1, z2 = jnp.zeros((BLK, 1), f32), jnp.zeros((BLK, BLK), f32)
            carry = step(qs, [n], (z1, z1, z2, z1, z1, z2), True)
            trips = (n + SB_UNROLL - 1) // SB_UNROLL

            def live(st):
                worst = jnp.maximum(jnp.maximum(st[1], st[2]), jnp.maximum(st[4], st[5]))
                return (st[0] < trips) & (jnp.max(worst) > SB_CUT)

            def trip(st):
                i = st[0]
                return (i + 1,) + step(qs, [n - 1 - SB_UNROLL * i - u for u in range(SB_UNROLL)], st[1:], False)
            done, *res = lax.while_loop(live, trip, (jnp.int32(0),) + carry)
            lt = jnp.where(ci == SB_HEADS, done.astype(f32), 0.0)
            for p in range(2):
                y_ref[qrows, p * BLK:(p + 1) * BLK] = res[3 * p + 2].astype(bf16)
                lt = lt + jnp.where(ci == 2 * p, res[3 * p], 0.0) + jnp.where(ci == 2 * p + 1, res[3 * p + 1], 0.0)
            lt_ref[qrows, :] = lt
            return c
        lax.fori_loop(0, S // BLK, qblock, 0)

    spec = lambda j: pl.BlockSpec((S, GW), lambda b: (b, j))
    c_args, c_in, c_out, c_shapes, aliases, c_scratch = _host_specs(comm, 3, 2)
    step = lambda v: (lambda: pl.program_id(0) == v)
    stacked = pltpu.VMEM((2, S // BLK, 2 * BLK, BLK), _MXU)
    return pl.pallas_call(
        _host(body, 3, 2, 2, comm, step(0), step(B - 1)), grid=(B,),
        in_specs=[spec(5), spec(6), spec(7)] + c_in,
        out_specs=[pl.BlockSpec((S, GW), lambda b: (b, 0)), pl.BlockSpec((S, BLK), lambda b: (b, 0))] + c_out,
        out_shape=[SDS((B * S, GW), bf16), SDS((B * S, BLK), f32)] + c_shapes,
        input_output_aliases=aliases, scratch_shapes=[stacked, stacked] + c_scratch,
        name="sb_fwd" if comm is None else "sb_fwd_gather",
        compiler_params=_cp("arbitrary"))(proj, proj, proj, *c_args)


def _sb_bwd(proj, ltot, dy, B, S, comm=None):
    def body(q_ref, k_ref, v_ref, lt_ref, dy_ref, dq_ref, dk_ref, dv_ref, dka_ref, dva_ref, kst_ref, vst_ref):
        ci = _iota((BLK, BLK), 1)
        upto2, strict2, head0 = _sb_consts(lambda r, c: r <= c)
        below2, _, _ = _sb_consts(lambda r, c: r < c)
        dka_ref[...] = jnp.zeros_like(dka_ref)
        dva_ref[...] = jnp.zeros_like(dva_ref)
        _sb_stack_kv(k_ref, v_ref, kst_ref, vst_ref, head0, S // BLK)

        def step(qs, dos, lts, kbs, last, carry, diag):
            U = range(len(kbs))
            ok = [None if diag else kb <= last for kb in kbs]
            kbs = [jnp.minimum(kb, last) for kb in kbs]
            kv = [_sb_load_kv(kst_ref, vst_ref, kb) for kb in kbs]
            zs = [[lax.dot_general(qs[p], kv[u][0][p], NT, preferred_element_type=f32) for p in range(2)] for u in U]
            dws = [[lax.dot_general(dos[p], kv[u][1][p], NT, preferred_element_type=f32) for p in range(2)] for u in U]
            parts = [[_log1m_parts(z) for z in zu] for zu in zs]
            Ls = [[lt[0] for lt in pu] for pu in parts]
            if diag:
                Ls = [[jnp.where(strict2, L, 0.0) for L in Lu] for Lu in Ls]
            pins = [[_split_dot(L, upto2) for L in Lu] for Lu in Ls]
            carry = list(carry)
            ws, das = [], []
            for u in U:
                wu, dau = [], []
                for p in range(2):
                    PL0, PL1 = carry[5 * p], carry[5 * p + 1]
                    tail = _two_halves(lts[2 * p] - PL0, lts[2 * p + 1] - PL1) - pins[u][p]
                    w = jnp.exp(zs[u][p] + Ls[u][p] + tail)
                    l0, l1 = _half_sums(Ls[u][p])
                    if diag:
                        w = jnp.where(strict2, w, 0.0)
                    else:
                        w, l0, l1 = (jnp.where(ok[u], t, 0.0) for t in (w, l0, l1))
                    carry[5 * p], carry[5 * p + 1] = PL0 + l0, PL1 + l1
                    wu.append(w)
                    dau.append(w * dws[u][p])
                ws.append(wu)
                das.append(dau)
            pexs = [[_split_dot(da, below2) for da in dau] for dau in das]
            dzs = []
            for u in U:
                dzu = []
                for p in range(2):
                    dL = _two_halves(carry[5 * p + 2], carry[5 * p + 3]) + pexs[u][p]
                    sg = _sigmoid_from(zs[u][p], parts[u][p][1])
                    dz = das[u][p] * (1.0 - sg) - dL * sg
                    dz = jnp.where(strict2 if diag else ok[u], dz, 0.0)
                    a0, a1 = _half_sums(das[u][p])
                    carry[5 * p + 2], carry[5 * p + 3] = carry[5 * p + 2] + a0, carry[5 * p + 3] + a1
                    dzu.append(_mx(dz))
                dzs.append(dzu)
            dqs = [[jnp.dot(dzs[u][p], kv[u][0][p], preferred_element_type=f32) for p in range(2)] for u in U]
            dks = [[lax.dot_general(dzs[u][p], qs[p], TN, preferred_element_type=f32) for p in range(2)] for u in U]
            dvs = [[lax.dot_general(_mx(ws[u][p]), dos[p], TN, preferred_element_type=f32) for p in range(2)] for u in U]
            for u in U:
                krows = pl.ds(pl.multiple_of(kbs[u] * BLK, BLK), BLK)
                for p in range(2):
                    lanes = slice(p * BLK, (p + 1) * BLK)
                    dka_ref[krows, lanes] += jnp.where(head0, dks[u][p][:BLK], dks[u][p][BLK:])
                    dva_ref[krows, lanes] += jnp.where(head0, dvs[u][p][:BLK], dvs[u][p][BLK:])
                    carry[5 * p + 4] = carry[5 * p + 4] + dqs[u][p]
            return tuple(carry)

        def qblock(n, c):
            qrows = pl.ds(pl.multiple_of(n * BLK, BLK), BLK)
            ltb = lt_ref[qrows, :]
            lts = [jnp.sum(jnp.where(ci == h, ltb, 0.0), axis=-1, keepdims=True) for h in range(4)]
            qs = [_mx(q_ref[qrows, p * BLK:(p + 1) * BLK] * (HD ** -0.5)) for p in range(2)]
            dos = [_mx(dy_ref[qrows, p * BLK:(p + 1) * BLK]) for p in range(2)]
            z1, z2 = jnp.zeros((BLK, 1), f32), jnp.zeros((BLK, BLK), f32)
            done = jnp.max(jnp.where(ci == SB_HEADS, ltb, 0.0)).astype(jnp.int32)
            first = jnp.maximum(n - SB_UNROLL * done, 0)
            carry = lax.fori_loop(
                0, (n - first + SB_UNROLL - 1) // SB_UNROLL,
                lambda i, cr: step(qs, dos, lts, [first + SB_UNROLL * i + u for u in range(SB_UNROLL)], n - 1, cr, False),
                (z1, z1, z1, z1, z2) * 2)
            res = step(qs, dos, lts, [n], n, carry, True)
            for p in range(2):
                dq_ref[qrows, p * BLK:(p + 1) * BLK] = (res[5 * p + 4] * (HD ** -0.5)).astype(bf16)
            return c
        lax.fori_loop(0, S // BLK, qblock, 0)
        dk_ref[...] = dka_ref[...].astype(bf16)
        dv_ref[...] = dva_ref[...].astype(bf16)

    spec = lambda j: pl.BlockSpec((S, GW), lambda b: (b, j))
    o = pl.BlockSpec((S, GW), lambda b: (b, 0))
    c_args, c_in, c_out, c_shapes, aliases, c_scratch = _host_specs(comm, 5, 3)
    step = lambda v: (lambda: pl.program_id(0) == v)
    stacked = pltpu.VMEM((2, S // BLK, 2 * BLK, BLK), _MXU)
    return pl.pallas_call(
        _host(body, 5, 3, 4, comm, step(0), step(B - 1)), grid=(B,),
        in_specs=[spec(5), spec(6), spec(7), pl.BlockSpec((S, BLK), lambda b: (b, 0)), o] + c_in,
        out_specs=[o, o, o] + c_out,
        out_shape=[SDS((B * S, GW), bf16)] * 3 + c_shapes,
        input_output_aliases=aliases,
        scratch_shapes=[pltpu.VMEM((S, GW), f32), pltpu.VMEM((S, GW), f32), stacked, stacked] + c_scratch,
        name="sb_bwd" if comm is None else "sb_bwd_exchange",
        compiler_params=_cp("arbitrary"))(proj, proj, proj, ltot, dy, *c_args)


def _bias_expand(rel_bias_t, bucket):
    n = bucket.shape[1]

    def body(r_ref, b_ref, o_ref):
        onehot = (_iota((N_BUCKETS, n), 0) == b_ref[...]).astype(f32)
        o_ref[...] = jnp.dot(r_ref[...], onehot, precision=HIGHEST, preferred_element_type=f32)
    return pl.pallas_call(body, out_shape=SDS((rel_bias_t.shape[0], n), f32), name="bias_expand",
                          compiler_params=_cp())(rel_bias_t, bucket)


def _bias_reduce(dbias, bucket):
    n = bucket.shape[1]

    def body(*refs):
        b_ref, g_ref = refs[-2], refs[-1]
        d = refs[0][...]
        for r in refs[1:-2]:
            d = d + r[...]
        onehot = (_iota((N_BUCKETS, n), 0) == b_ref[...]).astype(f32)
        g_ref[...] = lax.dot_general(d, onehot, NT, precision=HIGHEST, preferred_element_type=f32)
    return pl.pallas_call(body, out_shape=SDS((dbias[0].shape[0], N_BUCKETS), f32), name="bias_reduce",
                          compiler_params=_cp())(*dbias, bucket)


def _adamw(w, g, m, v, tr, name):
    R, C = w.shape

    def body(w_ref, g_ref, m_ref, v_ref, d_ref, m2_ref, v2_ref):
        gv = g_ref[...]
        m2 = ADAM_B1 * m_ref[...] + (1.0 - ADAM_B1) * gv
        v2 = ADAM_B2 * v_ref[...] + (1.0 - ADAM_B2) * (gv * gv)
        m_hat = m2 / (1.0 - ADAM_B1 ** ADAM_STEP)
        v_hat = v2 / (1.0 - ADAM_B2 ** ADAM_STEP)
        d_ref[...] = -ADAM_LR * (m_hat / (jnp.sqrt(v_hat) + ADAM_EPS) + ADAM_WD * w_ref[...])
        m2_ref[...] = m2
        v2_ref[...] = v2

    spec = pl.BlockSpec((tr, C), lambda i: (i, 0))
    return pl.pallas_call(
        body, grid=(R // tr,), in_specs=[spec] * 4, out_specs=[spec] * 3,
        out_shape=[SDS((R, C), f32)] * 3, name=name, compiler_params=_cp("parallel"))(w, g, m, v)


ANY = pl.BlockSpec(memory_space=pl.ANY)


def _place():
    x, y, c = lax.axis_index("x"), lax.axis_index("y"), lax.axis_index("c")
    chips = [(1 - x, y), (x, 1 - y), (1 - x, 1 - y)]
    return x, y, c, chips


def _cast_slots(w, kidx):
    L, a, b = w.shape
    ta = a // 2

    def body(k_ref, *refs):
        for l in range(L):
            refs[L + l][0] = refs[l][0].astype(bf16)

    return pl.pallas_call(
        body,
        grid_spec=pltpu.PrefetchScalarGridSpec(
            num_scalar_prefetch=1, grid=(a // ta,),
            in_specs=[pl.BlockSpec((1, ta, b), functools.partial(lambda i, k_ref, l: (l, i, 0), l=l)) for l in range(L)],
            out_specs=[pl.BlockSpec((1, ta, b), lambda i, k_ref: (k_ref[0], i, 0)) for _ in range(L)]),
        out_shape=[SDS((N_CHIPS, a, b), bf16)] * L,
        name="cast_slots", compiler_params=_cp("parallel"))(kidx, *([w] * L))


class _GatherComm:
    def __init__(self, bufs):
        self.inputs = list(bufs)
        self.out_shape = [SDS(b.shape, b.dtype) for b in bufs]
        self.aliased = True
        self.scratch = [pltpu.SemaphoreType.DMA((3 * len(bufs),))] * 4

    def _copies(self, i_refs, o_refs, sems):
        send1, recv1, send2, recv2 = sems
        x, y, c, chips = _place()
        k = 2 * x + y
        first, got1, second, got2 = [], [], [], []
        for i, buf in enumerate(self.inputs):
            h = buf.shape[1] // 2
            mine, theirs = pl.ds(c * h, h), pl.ds((1 - c) * h, h)
            for j, (cx, cy) in enumerate(chips):
                s = 3 * i + j
                first.append(pltpu.make_async_remote_copy(
                    src_ref=i_refs[i].at[k, mine], dst_ref=o_refs[i].at[k, mine], send_sem=send1.at[s],
                    recv_sem=recv1.at[s], device_id=(cx, cy, c), device_id_type=MESH))
                a = o_refs[i].at[2 * cx + cy, mine]
                got1.append(pltpu.make_async_remote_copy(
                    src_ref=a, dst_ref=a, send_sem=send1.at[s], recv_sem=recv1.at[s],
                    device_id=(cx, cy, c), device_id_type=MESH))
                second.append(pltpu.make_async_remote_copy(
                    src_ref=a, dst_ref=a, send_sem=send2.at[s], recv_sem=recv2.at[s],
                    device_id=(x, y, 1 - c), device_id_type=MESH))
                b = o_refs[i].at[2 * cx + cy, theirs]
                got2.append(pltpu.make_async_remote_copy(
                    src_ref=b, dst_ref=b, send_sem=send2.at[s], recv_sem=recv2.at[s],
                    device_id=(x, y, 1 - c), device_id_type=MESH))
        return first, got1, second, got2

    def start(self, i_refs, o_refs, sems):
        for cp in self._copies(i_refs, o_refs, sems)[0]:
            cp.start()

    def finish(self, i_refs, o_refs, sems):
        first, got1, second, got2 = self._copies(i_refs, o_refs, sems)
        for g, cp in zip(got1, second):
            g.wait_recv()
            cp.start()
        for g in got2:
            g.wait_recv()
        for cp in first + second:
            cp.wait_send()


class _PairExchangeComm:
    def __init__(self, gs):
        self.inputs = list(gs)
        self.out_shape = [SDS((g.shape[0], g.shape[1] // 2, g.shape[2]), g.dtype) for g in gs]
        self.aliased = False
        self.scratch = [pltpu.SemaphoreType.DMA((len(gs),))] * 2

    def _copies(self, i_refs, o_refs, sems):
        send, recv = sems
        x, y, c, _ = _place()
        cps = []
        for i, g in enumerate(self.inputs):
            h = g.shape[1] // 2
            cps.append(pltpu.make_async_remote_copy(
                src_ref=i_refs[i].at[:, pl.ds((1 - c) * h, h)], dst_ref=o_refs[i], send_sem=send.at[i], recv_sem=recv.at[i],
                device_id=(x, y, 1 - c), device_id_type=MESH))
        return cps

    def start(self, i_refs, o_refs, sems):
        for cp in self._copies(i_refs, o_refs, sems):
            cp.start()

    def finish(self, i_refs, o_refs, sems):
        for cp in self._copies(i_refs, o_refs, sems):
            cp.wait()


class _ChipExchangeComm:
    def __init__(self, qs):
        self.inputs = list(qs)
        self.out_shape = [SDS(q.shape, q.dtype) for q in qs]
        self.aliased = False
        self.scratch = [pltpu.SemaphoreType.DMA((3 * len(qs),))] * 2

    def _copies(self, i_refs, o_refs, sems):
        send, recv = sems
        x, y, c, chips = _place()
        k = 2 * x + y
        cps, got = [], []
        for i in range(len(self.inputs)):
            for j, (cx, cy) in enumerate(chips):
                s = 3 * i + j
                cps.append(pltpu.make_async_remote_copy(
                    src_ref=i_refs[i].at[2 * cx + cy], dst_ref=o_refs[i].at[k], send_sem=send.at[s],
                    recv_sem=recv.at[s], device_id=(cx, cy, c), device_id_type=MESH))
                a = o_refs[i].at[2 * cx + cy]
                got.append(pltpu.make_async_remote_copy(
                    src_ref=a, dst_ref=a, send_sem=send.at[s], recv_sem=recv.at[s],
                    device_id=(cx, cy, c), device_id_type=MESH))
        return cps, got

    def start(self, i_refs, o_refs, sems):
        for cp in self._copies(i_refs, o_refs, sems)[0]:
            cp.start()

    def finish(self, i_refs, o_refs, sems):
        cps, got = self._copies(i_refs, o_refs, sems)
        for g in got:
            g.wait_recv()
        for cp in cps:
            cp.wait_send()


def _comm_only(comm, name):
    n = len(comm.inputs)

    def body(*refs):
        i_refs, o_refs, sems = refs[:n], refs[n:n + len(comm.out_shape)], refs[n + len(comm.out_shape):]
        comm.start(i_refs, o_refs, sems)
        comm.finish(i_refs, o_refs, sems)

    return pl.pallas_call(
        body, out_shape=comm.out_shape, in_specs=[ANY] * n, out_specs=[ANY] * len(comm.out_shape),
        input_output_aliases={i: i for i in range(n)} if comm.aliased else {},
        scratch_shapes=comm.scratch, name=name,
        compiler_params=pltpu.CompilerParams(has_side_effects=True))(*comm.inputs)


def _host(body, n_in, n_out, n_scratch, comm, first, last):
    if comm is None:
        return body
    ci, co = len(comm.inputs), len(comm.out_shape)

    def wrapped(*refs):
        o = 0
        parts = []
        for n in (n_in, ci, n_out, co, n_scratch):
            parts.append(refs[o:o + n])
            o += n
        hin, cin, hout, cout, hs = parts
        sems = refs[o:]

        @pl.when(first())
        def _():
            comm.start(cin, cout, sems)
        body(*hin, *hout, *hs)

        @pl.when(last())
        def _():
            comm.finish(cin, cout, sems)
    return wrapped


def _host_specs(comm, n_in, n_out):
    if comm is None:
        return [], [], [], [], {}, []
    ci, co = len(comm.inputs), len(comm.out_shape)
    aliases = {n_in + i: n_out + i for i in range(ci)} if comm.aliased else {}
    return comm.inputs, [ANY] * ci, [ANY] * co, comm.out_shape, aliases, comm.scratch


def _pair_add(g, r, cidx, name):
    ns, a, b = g.shape
    h = a // 2
    th = h if h * b * 4 <= 4 * 1024 * 1024 else h // 2

    def body(c_ref, g_ref, r_ref, qb_ref):
        qb_ref[...] = (g_ref[...] + r_ref[...]).astype(bf16)

    nb = h // th
    spec = pl.BlockSpec((1, th, b), lambda s, i, c_ref: (s, i, 0))
    return pl.pallas_call(
        body,
        grid_spec=pltpu.PrefetchScalarGridSpec(
            num_scalar_prefetch=1, grid=(ns, nb),
            in_specs=[pl.BlockSpec((1, th, b), lambda s, i, c_ref: (s, c_ref[0] * nb + i, 0)), spec],
            out_specs=spec),
        out_shape=SDS((ns, h, b), bf16),
        name=name, compiler_params=_cp("parallel", "parallel"))(cidx, g, r)


def _chip_add(g, r1, r2, idx, prev, L, name):
    ns, h, b = r2.shape
    th = h if h * b * 4 <= 4 * 1024 * 1024 else h // 2
    nb = h // th

    def body(s_ref, g_ref, r1_ref, a_ref, b_ref, c_ref, *rest):
        o_ref = rest[-1]
        o_ref[0] = (g_ref[0] + r1_ref[0]) + a_ref[0].astype(f32) + b_ref[0].astype(f32) + c_ref[0].astype(f32)

    other = lambda d: pl.BlockSpec((1, th, b), lambda i, s_ref: ((s_ref[0] + d) % ns, i, 0))
    in_specs = [pl.BlockSpec((1, th, b), lambda i, s_ref: (s_ref[0], s_ref[1] * nb + i, 0)),
                pl.BlockSpec((1, th, b), lambda i, s_ref: (s_ref[0], i, 0)), other(1), other(2), other(3)]
    args = [idx, g, r1, r2, r2, r2]
    aliases = {}
    if prev is not None:
        in_specs.append(ANY)
        args.append(prev)
        aliases = {6: 0}
    return pl.pallas_call(
        body,
        grid_spec=pltpu.PrefetchScalarGridSpec(
            num_scalar_prefetch=1, grid=(nb,), in_specs=in_specs,
            out_specs=pl.BlockSpec((1, th, b), lambda i, s_ref: (s_ref[2], s_ref[1] * nb + i, 0))),
        out_shape=SDS((L, 2 * h, b), f32), input_output_aliases=aliases,
        name=name, compiler_params=_cp("arbitrary"))(*args)


def _pair_share(gs, hs):
    n = len(gs)
    L = gs[0].shape[0]

    def body(*refs):
        i_refs, o_refs = refs[:n], refs[n:2 * n]
        send, recv = refs[2 * n:]
        x, y, c, _ = _place()
        cps = []
        for i in range(n):
            for l in range(L):
                mine = pl.ds(c * hs[i], hs[i])
                cp = pltpu.make_async_remote_copy(
                    src_ref=i_refs[i].at[l, mine], dst_ref=o_refs[i].at[l, mine], send_sem=send.at[i * L + l],
                    recv_sem=recv.at[i * L + l], device_id=(x, y, 1 - c), device_id_type=MESH)
                cp.start()
                cps.append(cp)
        for i in range(n):
            for l in range(L):
                got = o_refs[i].at[l, pl.ds((1 - c) * hs[i], hs[i])]
                pltpu.make_async_remote_copy(
                    src_ref=got, dst_ref=got, send_sem=send.at[i * L + l], recv_sem=recv.at[i * L + l],
                    device_id=(x, y, 1 - c), device_id_type=MESH).wait_recv()
        for cp in cps:
            cp.wait_send()

    return pl.pallas_call(
        body, out_shape=[SDS(g.shape, g.dtype) for g in gs], in_specs=[ANY] * n, out_specs=[ANY] * n,
        input_output_aliases={i: i for i in range(n)},
        scratch_shapes=[pltpu.SemaphoreType.DMA((n * L,))] * 2,
        name="grad_pair_share", compiler_params=pltpu.CompilerParams(has_side_effects=True))(*gs)


class _SwapComm:
    def __init__(self, arrays):
        self.inputs = list(arrays)
        self.out_shape = [SDS(a.shape, a.dtype) for a in arrays]
        self.aliased = False
        self.scratch = [pltpu.SemaphoreType.DMA((len(arrays),))] * 2

    def _copies(self, i_refs, o_refs, sems):
        send, recv = sems
        x, y, c, _ = _place()
        return [pltpu.make_async_remote_copy(
            src_ref=i_refs[i], dst_ref=o_refs[i], send_sem=send.at[i], recv_sem=recv.at[i],
            device_id=(x, y, 1 - c), device_id_type=MESH) for i in range(len(self.inputs))]

    def start(self, i_refs, o_refs, sems):
        for cp in self._copies(i_refs, o_refs, sems):
            cp.start()

    def finish(self, i_refs, o_refs, sems):
        for cp in self._copies(i_refs, o_refs, sems):
            cp.wait()


class _SlotShareComm:
    def __init__(self, bufs):
        self.inputs = list(bufs)
        self.out_shape = [SDS(b.shape, b.dtype) for b in bufs]
        self.aliased = True
        self.scratch = [pltpu.SemaphoreType.DMA((3 * len(bufs),))] * 2

    def _copies(self, i_refs, o_refs, sems):
        send, recv = sems
        x, y, c, chips = _place()
        k = 2 * x + y
        cps, got = [], []
        for i in range(len(self.inputs)):
            for j, (cx, cy) in enumerate(chips):
                s = 3 * i + j
                cps.append(pltpu.make_async_remote_copy(
                    src_ref=i_refs[i].at[k], dst_ref=o_refs[i].at[k], send_sem=send.at[s], recv_sem=recv.at[s],
                    device_id=(cx, cy, c), device_id_type=MESH))
                a = o_refs[i].at[2 * cx + cy]
                got.append(pltpu.make_async_remote_copy(
                    src_ref=a, dst_ref=a, send_sem=send.at[s], recv_sem=recv.at[s],
                    device_id=(cx, cy, c), device_id_type=MESH))
        return cps, got

    def start(self, i_refs, o_refs, sems):
        for cp in self._copies(i_refs, o_refs, sems)[0]:
            cp.start()

    def finish(self, i_refs, o_refs, sems):
        cps, got = self._copies(i_refs, o_refs, sems)
        for g in got:
            g.wait_recv()
        for cp in cps:
            cp.wait_send()


def _pair_sum_slot(mine, theirs, kidx):
    R, C = mine.shape

    def body(k_ref, a_ref, b_ref, o_ref):
        o_ref[0] = a_ref[...] + b_ref[...]

    spec = pl.BlockSpec((R, C), lambda i, k_ref: (0, 0))
    return pl.pallas_call(
        body,
        grid_spec=pltpu.PrefetchScalarGridSpec(
            num_scalar_prefetch=1, grid=(1,), in_specs=[spec, spec],
            out_specs=pl.BlockSpec((1, R, C), lambda i, k_ref: (k_ref[0], 0, 0))),
        out_shape=SDS((N_CHIPS, R, C), f32), name="small_pair_sum", compiler_params=_cp("arbitrary"))(kidx, mine, theirs)


def _small_sum(g):
    n, R, C = g.shape

    def body(g_ref, o_ref):
        acc = g_ref[0]
        for j in range(1, n):
            acc = acc + g_ref[j]
        o_ref[...] = acc
    return pl.pallas_call(body, out_shape=SDS((R, C), f32), name="small_sum", compiler_params=_cp())(g)


PACK_COLS = 1024


def _rows_of(shape):
    n = int(np.prod(shape)) if len(shape) else 1
    return -(-n // (8 * PACK_COLS)) * 8


def _pack(parts):
    blocks = []
    for p in parts:
        flat = p.reshape(-1)
        r = _rows_of(p.shape)
        blocks.append(jnp.pad(flat, (0, r * PACK_COLS - flat.shape[0])).reshape(r, PACK_COLS))
    return jnp.concatenate(blocks, axis=0)


def _unpack(buf, shapes):
    out, off = [], 0
    for s in shapes:
        n = int(np.prod(s)) if len(s) else 1
        r = _rows_of(s)
        out.append(buf[off:off + r].reshape(-1)[:n].reshape(s))
        off += r
    return out


def _block_diag(w):
    g, a, _ = w.shape
    out = jnp.zeros((g * a, g * a), w.dtype)
    for i in range(g):
        out = lax.dynamic_update_slice(out, w[i], (i * a, i * a))
    return out


def kernel(x, w_in, w_out, sgu_w, sgu_b, pool_w, pool_scale, swa_sinks, rel_bias, mix_out_gain, norm_mix, norm_ffn, w_gate_up, w_down, norm_final, loss_target, m_w_in, m_w_out, m_sgu_w, m_sgu_b, m_pool_w, m_pool_scale, m_swa_sinks, m_rel_bias, m_mix_out_gain, m_norm_mix, m_norm_ffn, m_w_gate_up, m_w_down, m_norm_final, v_w_in, v_w_out, v_sgu_w, v_sgu_b, v_pool_w, v_pool_scale, v_swa_sinks, v_rel_bias, v_mix_out_gain, v_norm_mix, v_norm_ffn, v_w_gate_up, v_w_down, v_norm_final):
    B, S, D = x.shape
    T = B * S
    L = w_in.shape[0]
    tm = min(512, T)
    F = w_down.shape[1] * N_CHIPS
    xi, yi, ci = lax.axis_index("x"), lax.axis_index("y"), lax.axis_index("c")
    cidx = jnp.reshape(ci, (1,)).astype(jnp.int32)
    kidx = jnp.reshape(2 * xi + yi, (1,)).astype(jnp.int32)

    big = [w_in, w_out, w_gate_up, w_down]
    slots = [_cast_slots(w, kidx) for w in big]
    gather = lambda pi, l: _GatherComm([slots[pi][l]])
    Win, Wo, Wgu, Wd = ([None] * L for _ in range(4))
    Win[0], = _comm_only(gather(0, 0), "gather_weights")

    bucket = jnp.asarray(_t5_bucket_table().reshape(1, -1))
    bias_tab = _bias_expand(rel_bias.T, bucket).reshape(4, BLK, 2 * BLK)

    row = lambda v: v.reshape(1, -1)
    xc = x.reshape(T, D)
    tgt = loss_target.reshape(T, D)
    saved = []
    for l in range(L):
        h1, proj, wo = _norm_mm(xc, row(norm_mix[l]), Win[l], tm, gather(1, l))
        Wo[l] = wo.reshape(D, D)
        bexp = jnp.repeat(sgu_b[l].T, HD, axis=1)
        wbd = _block_diag(pool_w[l])
        sk = jnp.broadcast_to(swa_sinks[l][:, None, None], (4, 1, BLK))
        ya = _sgu_fwd(proj, sgu_w[l], bexp, B, S)
        yb = _pool_fwd(proj, wbd, row(pool_scale[l]), B, S)
        yc = _swa_fwd(proj, sk, bias_tab, B, S)
        if l == 0:
            yd, lt, wd, Wgu[0] = _sb_fwd(proj, B, S, _GatherComm([slots[3][0], slots[2][0]]))
        else:
            yd, lt, wd = _sb_fwd(proj, B, S, gather(3, l))
        Wd[l] = wd.reshape(F, D)
        ys = (ya, yb, yc, yd)
        ycn, x1 = _gnorm_mm_res(ys, row(mix_out_gain[l]), Wo[l], xc, tm)
        if l + 1 < L:
            h2, gu, act, Wgu[l + 1] = _norm_mm_swiglu(x1, row(norm_ffn[l]), Wgu[l], tm, gather(2, l + 1))
            x2, Win[l + 1] = _mm_res(act, Wd[l], x1, tm, gather(0, l + 1))
        else:
            h2, gu, act = _norm_mm_swiglu(x1, row(norm_ffn[l]), Wgu[l], tm)
            x2, = _mm_res(act, Wd[l], x1, tm)
        saved.append((xc, h1, proj, bexp, wbd, sk, ys, lt, ycn, x1, h2, gu, act))
        xc = x2

    dx, g_final, loss_v = _final_loss(xc, row(norm_final), tgt, tm)

    tk = min(T, 2048)
    gW = [[None] * L for _ in range(4)]
    g_sgu_w, g_sgu_b, g_pool_w, g_pool_scale, g_sinks, g_bias = ([None] * L for _ in range(6))
    g_out_gain, g_mix, g_ffn = ([None] * L for _ in range(3))
    reduced = [None] * 4
    sums = {}

    def pair_comm(keys):
        return _PairExchangeComm([gW[pi][l] for pi, l in keys])

    def after_pair(keys, r1):
        for (pi, l), r in zip(keys, r1):
            sums[pi, l] = (r, _pair_add(gW[pi][l], r, cidx, "grad_pair_add"))

    def chip_comm(keys):
        return _ChipExchangeComm([sums[k][1] for k in keys])

    def after_chip(keys, r2):
        for (pi, l), r in zip(keys, r2):
            idx = jnp.stack([2 * xi + yi, ci, jnp.int32(l)]).astype(jnp.int32)
            reduced[pi] = _chip_add(gW[pi][l], sums.pop((pi, l))[0], r, idx, reduced[pi], L, "grad_chip_add")

    for l in reversed(range(L)):
        x0, h1, proj, bexp, wbd, sk, ys, lt, ycn, x1, h2, gu, act = saved[l]
        if l + 1 < L:
            dgu, *r2 = _dact(dx, Wd[l], gu, tm, chip_comm([(0, l + 1)]))
            after_chip([(0, l + 1)], r2)
        else:
            dgu, = _dact(dx, Wd[l], gu, tm)
        gW[3][l] = _dw(act, dx, lambda t, s: (t, 0), D, 1, F // 2, tk // 2, "dw_down").reshape(N_CHIPS, F // N_CHIPS, D)
        gW[2][l] = _dw(h2, dgu, lambda t, s: (s // 2, t, s % 2), F // 2, N_CHIPS, D, tk, "dw_gate_up")
        keys = [(2, l), (3, l)]
        dx1, g_ffn[l], *r1 = _dx_norm_bwd(dgu, Wgu[l], x1, row(norm_ffn[l]), dx, tm, "dx_ffn_exchange", pair_comm(keys))
        after_pair(keys, r1)
        gW[1][l] = _dw(ycn, dx1, lambda t, s: (t, 0), D, 1, D, tk, "dw_out").reshape(N_CHIPS, D // N_CHIPS, D)
        dya, dyb, dyc, dyd, g_out_gain[l], *r1 = _dycat(dx1, Wo[l], ys, row(mix_out_gain[l]), tm, pair_comm([(1, l)]))
        after_pair([(1, l)], r1)
        dpa, g_sgu_w[l], dbf, *r2 = _sgu_bwd(proj, sgu_w[l], bexp, dya, B, S, chip_comm([(1, l)]))
        after_chip([(1, l)], r2)
        g_sgu_b[l] = dbf[:, ::HD].T
        dpb, dwbd, dsc = _pool_bwd(proj, wbd, row(pool_scale[l]), dyb, B, S)
        npg = len(POOL_WINDOWS)
        g_pool_w[l] = jnp.stack([dwbd[i * HD:(i + 1) * HD, i * HD:(i + 1) * HD] for i in range(npg)])
        g_pool_scale[l] = dsc[0]
        dcq, dckv, dsk, g_bias[l], *r2 = _swa_bwd(proj, sk, bias_tab, dyc, B, S, chip_comm([(3, l)]))
        after_chip([(3, l)], r2)
        g_sinks[l] = dsk[:, 0, 0] * float(BLK)
        ddq, ddk, ddv, *r2 = _sb_bwd(proj, lt, dyd, B, S, chip_comm([(2, l)]))
        after_chip([(2, l)], r2)
        dproj = jnp.concatenate([dpa, dpb, dcq, dckv, ddq, ddk, ddv], axis=1)
        gW[0][l] = _dw(h1, dproj, lambda t, s: (t, s), w_in.shape[2], N_CHIPS, D, tk, "dw_in")
        dx, g_mix[l], *r1 = _dx_norm_bwd(dproj, Win[l], x0, row(norm_mix[l]), dx1, tm, "dx_mix_exchange", pair_comm([(0, l)]))
        after_pair([(0, l)], r1)
    grad_x = dx.reshape(B, S, D)

    after_chip([(0, 0)], _comm_only(chip_comm([(0, 0)]), "grad_chip_exchange"))
    g_big = _pair_share(reduced, [g.shape[1] // 2 for g in reduced])

    g_rel_bias = _bias_reduce([g.reshape(4, -1) for g in g_bias], bucket).T
    small_g = [jnp.stack(g_sgu_w), jnp.stack(g_sgu_b), jnp.stack(g_pool_w), jnp.stack(g_pool_scale), jnp.stack(g_sinks),
               g_rel_bias, jnp.concatenate(g_out_gain), jnp.concatenate(g_mix), jnp.concatenate(g_ffn), g_final[0]]
    small_w = [sgu_w, sgu_b, pool_w, pool_scale, swa_sinks, rel_bias, mix_out_gain, norm_mix, norm_ffn, norm_final]
    small_m = [m_sgu_w, m_sgu_b, m_pool_w, m_pool_scale, m_swa_sinks, m_rel_bias, m_mix_out_gain, m_norm_mix, m_norm_ffn, m_norm_final]
    small_v = [v_sgu_w, v_sgu_b, v_pool_w, v_pool_scale, v_swa_sinks, v_rel_bias, v_mix_out_gain, v_norm_mix, v_norm_ffn, v_norm_final]
    shapes = [w.shape for w in small_w]
    mine = _pack(small_g + [loss_v[0, 0:1]])
    theirs, = _comm_only(_SwapComm([mine]), "small_pair_swap")
    shared, = _comm_only(_SlotShareComm([_pair_sum_slot(mine, theirs, kidx)]), "small_chip_share")
    packed = _small_sum(shared)
    *g_small, loss = _unpack(packed, shapes + [()])
    g_small_packed = _pack(g_small)
    ds, ms, vs = _adamw(_pack(small_w), g_small_packed, _pack(small_m), _pack(small_v), g_small_packed.shape[0], "adamw_small")
    d_small, m_small, v_small = _unpack(ds, shapes), _unpack(ms, shapes), _unpack(vs, shapes)

    big_m = [m_w_in, m_w_out, m_w_gate_up, m_w_down]
    big_v = [v_w_in, v_w_out, v_w_gate_up, v_w_down]
    d_big, m_big, v_big = [], [], []
    for w, g, m, v in zip(big, g_big, big_m, big_v):
        two = lambda a: a.reshape(-1, a.shape[-1])
        rows = two(w).shape[0]
        d2, m2, v2 = _adamw(two(w), two(g), two(m), two(v), rows // 8 if rows >= 2048 else rows, "adamw_big")
        d_big.append(d2.reshape(w.shape))
        m_big.append(m2.reshape(w.shape))
        v_big.append(v2.reshape(w.shape))

    def order(bigs, smalls):
        return [bigs[0], bigs[1]] + list(smalls[0:9]) + [bigs[2], bigs[3], smalls[9]]

    return (loss, grad_x, *order(g_big, g_small), *order(d_big, d_small), *order(m_big, m_small), *order(v_big, v_small))
```

```python
import functools

import numpy as np
import jax
import jax.numpy as jnp
from jax import lax
from jax.experimental import pallas as pl
from jax.experimental.pallas import tpu as pltpu

f32 = jnp.float32
bf16 = jnp.bfloat16
_MXU = jnp.bfloat16

EPS = 1e-6
HD = 64
GW = 256
BLK = 128
SB_UNROLL = 2
SB_HEADS = 4
SB_CUT = -110.0
POOL_WINDOWS = (2, 4, 8, 16)
N_BUCKETS = 32
MAX_DISTANCE = 128
N_CHIPS = 4
N_DEV = 8
VMEM_LIMIT = 48 * 1024 * 1024

ADAM_LR = 0.001
ADAM_B1 = 0.9
ADAM_B2 = 0.999
ADAM_EPS = 1e-08
ADAM_WD = 0.01
ADAM_STEP = 10

SDS = jax.ShapeDtypeStruct
MESH = pl.DeviceIdType.MESH
HIGHEST = lax.Precision.HIGHEST
RESIDENT = pl.Buffered(1)
NT = (((1,), (1,)), ((), ()))
TN = (((0,), (0,)), ((), ()))


def _cp(*sem):
    return pltpu.CompilerParams(dimension_semantics=sem if sem else None, vmem_limit_bytes=VMEM_LIMIT)


def _mx(v):
    return v.astype(_MXU)


def _iota(shape, dim):
    return lax.broadcasted_iota(jnp.int32, shape, dim)


def _split_dot(a, tri):
    hi = a.astype(bf16)
    lo = (a - hi.astype(f32)).astype(bf16)
    return jnp.dot(hi, tri, preferred_element_type=f32) + jnp.dot(lo, tri, preferred_element_type=f32)


def _rms(xv):
    return lax.rsqrt(jnp.mean(xv * xv, axis=-1, keepdims=True) + EPS)


def _hosted_call(body, steps, in_specs, out_specs, out_shape, scratch, args, name, comm):
    n_in, n_out = len(in_specs), len(out_specs)
    c_args, c_in, c_out, c_shapes, aliases, c_scratch = _host_specs(comm, n_in, n_out)
    step = lambda v: (lambda: pl.program_id(0) == v)
    return pl.pallas_call(
        _host(body, n_in, n_out, len(scratch), comm, step(0), step(steps - 1)), grid=(steps,),
        in_specs=list(in_specs) + c_in, out_specs=list(out_specs) + c_out, out_shape=list(out_shape) + c_shapes,
        input_output_aliases=aliases, scratch_shapes=list(scratch) + c_scratch,
        name=name if comm is None else name + "_comm", compiler_params=_cp("arbitrary"))(*args, *c_args)


def _norm_mm(x, gain, w, tm, comm=None):
    T, D = x.shape
    NS, _, ns = w.shape

    def body(x_ref, g_ref, w_ref, h_ref, o_ref):
        xv = x_ref[...]
        h = (xv * _rms(xv) * g_ref[...]).astype(bf16)
        h_ref[...] = h
        for s in range(NS):
            o_ref[:, s * ns:(s + 1) * ns] = jnp.dot(_mx(h), w_ref[s], preferred_element_type=f32).astype(bf16)

    return _hosted_call(
        body, T // tm,
        [pl.BlockSpec((tm, D), lambda i: (i, 0)),
         pl.BlockSpec((1, D), lambda i: (0, 0)),
         pl.BlockSpec((NS, D, ns), lambda i: (0, 0, 0), pipeline_mode=RESIDENT)],
        [pl.BlockSpec((tm, D), lambda i: (i, 0)), pl.BlockSpec((tm, NS * ns), lambda i: (i, 0))],
        [SDS((T, D), bf16), SDS((T, NS * ns), bf16)], [], (x, gain, w), "norm_mm_in", comm)


def _norm_mm_swiglu(x, gain, w, tm, comm=None):
    T, D = x.shape
    NS, _, ns = w.shape
    half = NS // 2

    def body(x_ref, g_ref, w_ref, h_ref, gu_ref, a_ref):
        xv = x_ref[...]
        hb = (xv * _rms(xv) * g_ref[...]).astype(bf16)
        h_ref[...] = hb
        h = _mx(hb)
        for s in range(half):
            cols = slice(s * ns, (s + 1) * ns)
            g = jnp.dot(h, w_ref[s], preferred_element_type=f32)
            u = jnp.dot(h, w_ref[s + half], preferred_element_type=f32)
            gu_ref[0, :, cols] = g.astype(bf16)
            gu_ref[1, :, cols] = u.astype(bf16)
            a_ref[:, cols] = (jax.nn.silu(g) * u).astype(bf16)

    c_args, c_in, c_out, c_shapes, aliases, c_scratch = _host_specs(comm, 3, 3)
    step = lambda v: (lambda: pl.program_id(0) == v)
    return pl.pallas_call(
        _host(body, 3, 3, 0, comm, step(0), step(T // tm - 1)), grid=(T // tm,),
        in_specs=[pl.BlockSpec((tm, D), lambda i: (i, 0)),
                  pl.BlockSpec((1, D), lambda i: (0, 0)),
                  pl.BlockSpec((NS, D, ns), lambda i: (0, 0, 0), pipeline_mode=RESIDENT)] + c_in,
        out_specs=[pl.BlockSpec((tm, D), lambda i: (i, 0)),
                   pl.BlockSpec((2, tm, half * ns), lambda i: (0, i, 0)),
                   pl.BlockSpec((tm, half * ns), lambda i: (i, 0))] + c_out,
        out_shape=[SDS((T, D), bf16), SDS((2, T, half * ns), bf16), SDS((T, half * ns), bf16)] + c_shapes,
        input_output_aliases=aliases, scratch_shapes=c_scratch,
        name="norm_mm_swiglu" if comm is None else "norm_mm_swiglu_gather",
        compiler_params=_cp("arbitrary"))(x, gain, w, *c_args)


def _gnorm_mm_res(ys, gain, w, x, tm):
    T, D = x.shape

    def body(ya, yb, yc, yd, g_ref, w_ref, x_ref, yn_ref, o_ref):
        parts = []
        for m, r in enumerate((ya, yb, yc, yd)):
            y = r[...].astype(f32)
            parts.append((y * _rms(y) * g_ref[:, m * GW:(m + 1) * GW]).astype(bf16))
        yn = jnp.concatenate(parts, axis=1)
        yn_ref[...] = yn
        o_ref[...] = x_ref[...] + jnp.dot(_mx(yn), w_ref[...], preferred_element_type=f32)

    yspec = pl.BlockSpec((tm, GW), lambda i: (i, 0))
    return pl.pallas_call(
        body, grid=(T // tm,),
        in_specs=[yspec, yspec, yspec, yspec,
                  pl.BlockSpec((1, D), lambda i: (0, 0)),
                  pl.BlockSpec((D, D), lambda i: (0, 0)),
                  pl.BlockSpec((tm, D), lambda i: (i, 0))],
        out_specs=[pl.BlockSpec((tm, D), lambda i: (i, 0)), pl.BlockSpec((tm, D), lambda i: (i, 0))],
        out_shape=[SDS((T, D), bf16), SDS((T, D), f32)],
        name="gnorm_mm_res", compiler_params=_cp("parallel"))(*ys, gain, w, x)


def _mm_res(a, w, x, tm, comm=None):
    T, D = x.shape
    K = a.shape[1]

    def body(a_ref, w_ref, x_ref, o_ref):
        o_ref[...] = x_ref[...] + jnp.dot(_mx(a_ref[...]), w_ref[...], preferred_element_type=f32)

    return _hosted_call(
        body, T // tm,
        [pl.BlockSpec((tm, K), lambda i: (i, 0)),
         pl.BlockSpec((K, D), lambda i: (0, 0), pipeline_mode=RESIDENT),
         pl.BlockSpec((tm, D), lambda i: (i, 0))],
        [pl.BlockSpec((tm, D), lambda i: (i, 0))], [SDS((T, D), f32)], [], (a, w, x), "mm_res_down", comm)


def _final_loss(x, gain, tgt, tm):
    T, D = x.shape

    def body(x_ref, g_ref, t_ref, dx_ref, dg_ref, l_ref):
        @pl.when(pl.program_id(0) == 0)
        def _():
            dg_ref[...] = jnp.zeros_like(dg_ref)
            l_ref[...] = jnp.zeros_like(l_ref)
        xv = x_ref[...]
        g = g_ref[...]
        r = _rms(xv)
        xh = xv * r
        err = xh * g - t_ref[...]
        l_ref[...] += 0.5 * jnp.sum(jnp.mean(err * err, axis=-1, keepdims=True), axis=0, keepdims=True)
        dy = err * (1.0 / D)
        dg_ref[...] += jnp.sum(dy * xh, axis=0, keepdims=True)
        dxh = dy * g
        dx_ref[...] = r * (dxh - xh * jnp.mean(dxh * xh, axis=-1, keepdims=True))

    return pl.pallas_call(
        body, grid=(T // tm,),
        in_specs=[pl.BlockSpec((tm, D), lambda i: (i, 0)),
                  pl.BlockSpec((1, D), lambda i: (0, 0)),
                  pl.BlockSpec((tm, D), lambda i: (i, 0))],
        out_specs=[pl.BlockSpec((tm, D), lambda i: (i, 0)),
                   pl.BlockSpec((1, D), lambda i: (0, 0)),
                   pl.BlockSpec((1, BLK), lambda i: (0, 0))],
        out_shape=[SDS((T, D), f32), SDS((1, D), f32), SDS((1, BLK), f32)],
        name="final_loss", compiler_params=_cp("arbitrary"))(x, gain, tgt)


def _dact(dx, wd, gu, tm, comm=None):
    T, D = dx.shape
    F = wd.shape[0]
    ns = F // 2

    def body(dx_ref, w_ref, gu_ref, o_ref):
        dxb = _mx(dx_ref[...])
        for s in range(2):
            cols = slice(s * ns, (s + 1) * ns)
            da = lax.dot_general(dxb, w_ref[s * ns:(s + 1) * ns, :], NT, preferred_element_type=f32)
            g = gu_ref[0, :, cols].astype(f32)
            u = gu_ref[1, :, cols].astype(f32)
            sg = jax.nn.sigmoid(g)
            o_ref[0, :, cols] = (da * u * (sg * (1.0 + g * (1.0 - sg)))).astype(bf16)
            o_ref[1, :, cols] = (da * (g * sg)).astype(bf16)

    return _hosted_call(
        body, T // tm,
        [pl.BlockSpec((tm, D), lambda i: (i, 0)),
         pl.BlockSpec((F, D), lambda i: (0, 0), pipeline_mode=RESIDENT),
         pl.BlockSpec((2, tm, F), lambda i: (0, i, 0))],
        [pl.BlockSpec((2, tm, F), lambda i: (0, i, 0))], [SDS((2, T, F), bf16)], [], (dx, wd, gu), "dact", comm)


def _dw(a, b, b_map, ns, NS, tka, tk, name):
    T, Ka = a.shape
    b_block = (tk, ns) if b.ndim == 2 else (1, tk, ns)

    def body(a_ref, b_ref, o_ref):
        bv = b_ref[...] if b.ndim == 2 else b_ref[0]
        part = lax.dot_general(_mx(a_ref[...]), _mx(bv), TN, preferred_element_type=f32)

        @pl.when(pl.program_id(2) == 0)
        def _():
            o_ref[0] = part

        @pl.when(pl.program_id(2) > 0)
        def _():
            o_ref[0] += part

    return pl.pallas_call(
        body, grid=(NS, Ka // tka, T // tk),
        in_specs=[pl.BlockSpec((tk, tka), lambda s, k, t: (t, k)),
                  pl.BlockSpec(b_block, lambda s, k, t: b_map(t, s))],
        out_specs=pl.BlockSpec((1, tka, ns), lambda s, k, t: (s, k, 0)),
        out_shape=SDS((NS, Ka, ns), f32),
        name=name, compiler_params=_cp("parallel", "parallel", "arbitrary"))(a, b)


def _dw_pieces(a, pieces, ns, NS, tk, name):
    T, Ka = a.shape
    n = len(pieces)

    def body(*refs):
        a_ref, b_refs, o_ref = refs[0], refs[1:1 + n], refs[1 + n]
        full = jnp.concatenate([r[...] for r in b_refs], axis=1)
        av = _mx(a_ref[...])
        parts = [lax.dot_general(av, _mx(full[:, s * ns:(s + 1) * ns]), TN, preferred_element_type=f32) for s in range(NS)]

        @pl.when(pl.program_id(0) == 0)
        def _():
            for s in range(NS):
                o_ref[s] = parts[s]

        @pl.when(pl.program_id(0) > 0)
        def _():
            for s in range(NS):
                o_ref[s] += parts[s]

    return pl.pallas_call(
        body, grid=(T // tk,),
        in_specs=[pl.BlockSpec((tk, Ka), lambda t: (t, 0))] + [pl.BlockSpec((tk, p.shape[1]), lambda t: (t, 0)) for p in pieces],
        out_specs=pl.BlockSpec((NS, Ka, ns), lambda t: (0, 0, 0)),
        out_shape=SDS((NS, Ka, ns), f32),
        name=name, compiler_params=_cp("arbitrary"))(a, *pieces)


def _dx_norm_bwd(dy, w, x, gain, dxin, tm, name, comm=None):
    T, D = x.shape
    NS, _, ns = w.shape
    half = NS // 2
    pieces = list(dy) if isinstance(dy, (list, tuple)) else None
    n_dy = len(pieces) if pieces else 1

    def body(*refs):
        dy_refs = refs[:n_dy]
        w_ref, x_ref, g_ref, dxin_ref, dx_ref, dg_ref = refs[n_dy:]

        @pl.when(pl.program_id(0) == 0)
        def _():
            dg_ref[...] = jnp.zeros_like(dg_ref)
        if pieces:
            full = jnp.concatenate([r[...] for r in dy_refs], axis=1)
        dh = None
        for s in range(NS):
            if pieces:
                dv = full[:, s * ns:(s + 1) * ns]
            else:
                dv = dy_refs[0][s // half, :, (s % half) * ns:(s % half + 1) * ns]
            part = lax.dot_general(_mx(dv), w_ref[s], NT, preferred_element_type=f32)
            dh = part if dh is None else dh + part
        xv = x_ref[...]
        r = _rms(xv)
        xh = xv * r
        dg_ref[...] += jnp.sum(dh * xh, axis=0, keepdims=True)
        dxh = dh * g_ref[...]
        dx_ref[...] = dxin_ref[...] + r * (dxh - xh * jnp.mean(dxh * xh, axis=-1, keepdims=True))

    if pieces:
        dy_specs = [pl.BlockSpec((tm, p.shape[1]), lambda i: (i, 0)) for p in pieces]
    else:
        dy_specs = [pl.BlockSpec((2, tm, half * ns), lambda i: (0, i, 0))]
    return _hosted_call(
        body, T // tm,
        dy_specs + [pl.BlockSpec((NS, D, ns), lambda i: (0, 0, 0), pipeline_mode=RESIDENT),
                    pl.BlockSpec((tm, D), lambda i: (i, 0)),
                    pl.BlockSpec((1, D), lambda i: (0, 0)),
                    pl.BlockSpec((tm, D), lambda i: (i, 0))],
        [pl.BlockSpec((tm, D), lambda i: (i, 0)), pl.BlockSpec((1, D), lambda i: (0, 0))],
        [SDS((T, D), f32), SDS((1, D), f32)], [], (*(pieces or [dy]), w, x, gain, dxin), name, comm)


def _dycat(dx, w, ys, gain, tm, comm=None):
    T, D = dx.shape

    def body(dx_ref, w_ref, ya, yb, yc, yd, g_ref, da, db, dc, dd, dg_ref):
        @pl.when(pl.program_id(0) == 0)
        def _():
            dg_ref[...] = jnp.zeros_like(dg_ref)
        dyn = lax.dot_general(_mx(dx_ref[...]), w_ref[...], NT, preferred_element_type=f32)
        for m, (r, o) in enumerate(((ya, da), (yb, db), (yc, dc), (yd, dd))):
            cols = slice(m * GW, (m + 1) * GW)
            y = r[...].astype(f32)
            rs = _rms(y)
            yh = y * rs
            d = dyn[:, cols]
            dg_ref[:, cols] += jnp.sum(d * yh, axis=0, keepdims=True)
            dyh = d * g_ref[:, cols]
            o[...] = (rs * (dyh - yh * jnp.mean(dyh * yh, axis=-1, keepdims=True))).astype(bf16)

    yspec = pl.BlockSpec((tm, GW), lambda i: (i, 0))
    return _hosted_call(
        body, T // tm,
        [pl.BlockSpec((tm, D), lambda i: (i, 0)),
         pl.BlockSpec((D, D), lambda i: (0, 0), pipeline_mode=RESIDENT),
         yspec, yspec, yspec, yspec,
         pl.BlockSpec((1, D), lambda i: (0, 0))],
        [yspec, yspec, yspec, yspec, pl.BlockSpec((1, D), lambda i: (0, 0))],
        [SDS((T, GW), bf16)] * 4 + [SDS((1, D), f32)], [], (dx, w, *ys, gain), "dycat", comm)


def _sgu_consts():
    r, c = _iota((GW, GW), 0), _iota((GW, GW), 1)
    seg = (r // HD == c // HD).astype(f32)
    tr, ts = _iota((BLK, BLK), 0), _iota((BLK, BLK), 1)
    causal = ts <= tr
    lane_head = _iota((BLK, GW), 1) // HD
    return seg, causal, lane_head


def _split3_dot(a, ones):
    hi = a.astype(bf16)
    r1 = a - hi.astype(f32)
    mid = r1.astype(bf16)
    lo = (r1 - mid.astype(f32)).astype(bf16)
    dot = functools.partial(jnp.dot, preferred_element_type=f32)
    return dot(hi, ones) + dot(mid, ones) + dot(lo, ones)


def _sgu_chunks(aus, avs, w, bexp, consts):
    seg, causal, lane_head = consts
    segb = seg.astype(bf16)
    nh = GW // HD
    vs = [jax.nn.gelu(av) for av in avs]
    mus = [_split3_dot(v, segb) * (1.0 / HD) for v in vs]
    vcs = [v - mu for v, mu in zip(vs, mus)]
    vars_ = [_split3_dot(vc * vc, segb) * (1.0 / HD) for vc in vcs]
    vns = [_mx(vc * lax.rsqrt(var + EPS)) for vc, var in zip(vcs, vars_)]
    whs = [_mx(jnp.where(causal, w[h], 0.0)) for h in range(nh)]
    mixes = [[jnp.dot(whs[h], vn, preferred_element_type=f32) for h in range(nh)] for vn in vns]
    out = []
    for au, ms in zip(aus, mixes):
        mix = bexp
        for h in range(nh):
            mix = mix + jnp.where(lane_head == h, ms[h], 0.0)
        out.append(jax.nn.gelu(au) * mix)
    return out


def _sgu_group(S):
    nc = S // BLK
    return 4 if nc % 4 == 0 else (2 if nc % 2 == 0 else 1)


def _sgu_fwd(proj, w, bexp, B, S):
    G = _sgu_group(S)

    def body(au_ref, av_ref, w_ref, b_ref, y_ref):
        consts = _sgu_consts()
        wv, bv = w_ref[...], b_ref[...]

        def group(n, c):
            rows = [pl.ds(pl.multiple_of((n * G + j) * BLK, BLK), BLK) for j in range(G)]
            ys = _sgu_chunks([au_ref[r, :].astype(f32) for r in rows], [av_ref[r, :].astype(f32) for r in rows],
                             wv, bv, consts)
            for r, y in zip(rows, ys):
                y_ref[r, :] = y.astype(bf16)
            return c
        lax.fori_loop(0, S // BLK // G, group, 0)

    return pl.pallas_call(
        body, grid=(B,),
        in_specs=[pl.BlockSpec((S, GW), lambda b: (b, 0)),
                  pl.BlockSpec((S, GW), lambda b: (b, 1)),
                  pl.BlockSpec((GW // HD, BLK, BLK), lambda b: (0, 0, 0)),
                  pl.BlockSpec((BLK, GW), lambda b: (0, 0))],
        out_specs=pl.BlockSpec((S, GW), lambda b: (b, 0)),
        out_shape=SDS((B * S, GW), bf16),
        name="sgu_fwd", compiler_params=_cp("parallel"))(proj, proj, w, bexp)


def _sgu_bwd(proj, w, bexp, dy, B, S, comm=None):
    def body(au_ref, av_ref, w_ref, b_ref, dy_ref, dp_ref, dw_ref, db_ref):
        @pl.when(pl.program_id(0) == 0)
        def _():
            dw_ref[...] = jnp.zeros_like(dw_ref)
            db_ref[...] = jnp.zeros_like(db_ref)
        consts = _sgu_consts()
        wv, bv = w_ref[...], b_ref[...]
        fn = lambda aus, avs, ww, bb: _sgu_chunks(aus, avs, ww, bb, consts)
        G = _sgu_group(S)

        def group(n, carry):
            dw_acc, db_acc = carry
            rows = [pl.ds(pl.multiple_of((n * G + j) * BLK, BLK), BLK) for j in range(G)]
            _, vjp = jax.vjp(fn, [au_ref[r, :].astype(f32) for r in rows], [av_ref[r, :].astype(f32) for r in rows], wv, bv)
            daus, davs, dwc, dbc = vjp([dy_ref[r, :].astype(f32) for r in rows])
            for r, dau, dav in zip(rows, daus, davs):
                dp_ref[r, 0:GW] = dau.astype(bf16)
                dp_ref[r, GW:2 * GW] = dav.astype(bf16)
            return dw_acc + dwc, db_acc + dbc
        dw_acc, db_acc = lax.fori_loop(0, S // BLK // G, group, (jnp.zeros(wv.shape, f32), jnp.zeros(bv.shape, f32)))
        dw_ref[...] += dw_acc
        db_ref[...] += jnp.dot(db_acc, consts[0], precision=HIGHEST, preferred_element_type=f32)

    return _hosted_call(
        body, B,
        [pl.BlockSpec((S, GW), lambda b: (b, 0)),
         pl.BlockSpec((S, GW), lambda b: (b, 1)),
         pl.BlockSpec((GW // HD, BLK, BLK), lambda b: (0, 0, 0)),
         pl.BlockSpec((BLK, GW), lambda b: (0, 0)),
         pl.BlockSpec((S, GW), lambda b: (b, 0))],
        [pl.BlockSpec((S, 2 * GW), lambda b: (b, 0)),
         pl.BlockSpec((GW // HD, BLK, BLK), lambda b: (0, 0, 0)),
         pl.BlockSpec((BLK, GW), lambda b: (0, 0))],
        [SDS((B * S, 2 * GW), bf16), SDS((GW // HD, BLK, BLK), f32), SDS((BLK, GW), f32)], [],
        (proj, proj, w, bexp, dy), "sgu_bwd", comm)


def _pool_parts(p):
    n = p.shape[0]
    r = _iota(p.shape, 0)
    lg = _iota(p.shape, 1) // HD

    def sh(v, k):
        return jnp.where(r >= k, pltpu.roll(v, k, 0), 0.0)
    s2 = p + sh(p, 1)
    s4 = s2 + sh(s2, 2)
    s8 = s4 + sh(s4, 4)
    s16 = s8 + sh(s8, 8)
    ws = jnp.where(lg == 0, s2, jnp.where(lg == 1, s4, jnp.where(lg == 2, s8, s16)))
    wlen = jnp.where(lg == 0, 2, jnp.where(lg == 1, 4, jnp.where(lg == 2, 8, 16)))
    cnt = jnp.minimum(r + 1, wlen).astype(f32)
    del n
    return ws / cnt - p, cnt, lg


def _pool_fwd(proj, wbd, scale, B, S):
    def body(p_ref, w_ref, s_ref, y_ref):
        y, _, _ = _pool_parts(p_ref[...].astype(f32))
        y_ref[...] = (jnp.dot(_mx(y), _mx(w_ref[...]), preferred_element_type=f32) * s_ref[...]).astype(bf16)

    return pl.pallas_call(
        body, grid=(B,),
        in_specs=[pl.BlockSpec((S, GW), lambda b: (b, 2)),
                  pl.BlockSpec((GW, GW), lambda b: (0, 0)),
                  pl.BlockSpec((1, GW), lambda b: (0, 0))],
        out_specs=pl.BlockSpec((S, GW), lambda b: (b, 0)),
        out_shape=SDS((B * S, GW), bf16),
        name="pool_fwd", compiler_params=_cp("parallel"))(proj, wbd, scale)


def _pool_bwd(proj, wbd, scale, dy, B, S):
    def body(p_ref, w_ref, s_ref, dy_ref, dp_ref, dw_ref, ds_ref):
        @pl.when(pl.program_id(0) == 0)
        def _():
            dw_ref[...] = jnp.zeros_like(dw_ref)
            ds_ref[...] = jnp.zeros_like(ds_ref)
        y, cnt, lg = _pool_parts(p_ref[...].astype(f32))
        wv = _mx(w_ref[...])
        z = jnp.dot(_mx(y), wv, preferred_element_type=f32)
        dout = dy_ref[...].astype(f32)
        ds_ref[...] += jnp.sum(dout * z, axis=0, keepdims=True)
        dz = _mx(dout * s_ref[...])
        dw_ref[...] += lax.dot_general(_mx(y), dz, TN, preferred_element_type=f32)
        dyv = lax.dot_general(dz, wv, NT, preferred_element_type=f32)
        n = dyv.shape[0]
        r = _iota(dyv.shape, 0)

        def ush(v, k):
            return jnp.where(r < n - k, pltpu.roll(v, n - k, 0), 0.0)
        gq = dyv / cnt
        a2 = gq + ush(gq, 1)
        a4 = a2 + ush(a2, 2)
        a8 = a4 + ush(a4, 4)
        a16 = a8 + ush(a8, 8)
        adj = jnp.where(lg == 0, a2, jnp.where(lg == 1, a4, jnp.where(lg == 2, a8, a16)))
        dp_ref[...] = (adj - dyv).astype(bf16)

    return pl.pallas_call(
        body, grid=(B,),
        in_specs=[pl.BlockSpec((S, GW), lambda b: (b, 2)),
                  pl.BlockSpec((GW, GW), lambda b: (0, 0)),
                  pl.BlockSpec((1, GW), lambda b: (0, 0)),
                  pl.BlockSpec((S, GW), lambda b: (b, 0))],
        out_specs=[pl.BlockSpec((S, GW), lambda b: (b, 0)),
                   pl.BlockSpec((GW, GW), lambda b: (0, 0)),
                   pl.BlockSpec((1, GW), lambda b: (0, 0))],
        out_shape=[SDS((B * S, GW), bf16), SDS((GW, GW), f32), SDS((1, GW), f32)],
        name="pool_bwd", compiler_params=_cp("arbitrary"))(proj, wbd, scale, dy)


def _t5_bucket_table():
    dist = (np.arange(BLK)[:, None] + BLK) - np.arange(2 * BLK)[None, :]
    d = np.clip(dist, 0, BLK - 1)
    max_exact = N_BUCKETS // 2
    df = np.maximum(d, 1).astype(np.float32)
    large = max_exact + (np.log(df / max_exact) / np.float32(np.log(MAX_DISTANCE / max_exact))
                         * (N_BUCKETS - max_exact)).astype(np.int32)
    large = np.minimum(large, N_BUCKETS - 1)
    return np.where(d < max_exact, d, large).astype(np.int32)


def _swa_block(qb0, qb1, k2, v2, sinks, biases, n):
    heads = [(p, g) for p in range(2) for g in range(2)]
    ri, ci = _iota((BLK, BLK), 0), _iota((BLK, BLK), 1)
    qi, ki = _iota((BLK, 2 * BLK), 0), _iota((BLK, 2 * BLK), 1)
    dist = qi + BLK - ki
    mask = (dist >= 0) & (dist < BLK) & ((ki >= BLK) | (n > 0))
    qbs, kb, vb = (_mx(qb0), _mx(qb1)), _mx(k2), _mx(v2)
    qs, vs = [], []
    for p, g in heads:
        selq = ((ri - g * HD == ci - p * HD) & (ri >= g * HD) & (ri < (g + 1) * HD)).astype(_MXU)
        selv = ((ci - g * HD == ri - p * HD) & (ci >= g * HD) & (ci < (g + 1) * HD)).astype(_MXU)
        qs.append(_mx(jnp.dot(qbs[p], selq, preferred_element_type=f32)))
        vs.append(_mx(jnp.dot(vb, selv, preferred_element_type=f32)))
    zs = [lax.dot_general(q, kb, NT, preferred_element_type=f32) * (HD ** -0.5) for q in qs]
    prs = []
    for h in range(4):
        z = jnp.where(mask, zs[h] + biases[h], -1e30)
        s = jnp.mean(sinks[h], axis=-1, keepdims=True)
        m = jnp.maximum(jnp.max(z, axis=-1, keepdims=True), s)
        e = jnp.exp(z - m)
        prs.append(_mx(e / (jnp.sum(e, axis=-1, keepdims=True) + jnp.exp(s - m))))
    outs = [jnp.dot(prs[h], vs[h], preferred_element_type=f32) for h in range(4)]
    return outs[0] + outs[1], outs[2] + outs[3]


def _swa_fwd(proj, sinks, bias, B, S):
    def body(q_ref, kv_ref, s_ref, b_ref, y_ref):
        def block(n, c):
            rows = pl.ds(pl.multiple_of(n * BLK, BLK), BLK)
            prev = pl.ds(pl.multiple_of(jnp.maximum(n - 1, 0) * BLK, BLK), BLK)
            k2 = jnp.concatenate([kv_ref[prev, 0:BLK], kv_ref[rows, 0:BLK]], axis=0).astype(f32)
            v2 = jnp.concatenate([kv_ref[prev, BLK:2 * BLK], kv_ref[rows, BLK:2 * BLK]], axis=0).astype(f32)
            o0, o1 = _swa_block(q_ref[rows, 0:BLK].astype(f32), q_ref[rows, BLK:2 * BLK].astype(f32), k2, v2,
                                [s_ref[h] for h in range(4)], [b_ref[h] for h in range(4)], n)
            y_ref[rows, 0:BLK] = o0.astype(bf16)
            y_ref[rows, BLK:2 * BLK] = o1.astype(bf16)
            return c
        lax.fori_loop(0, S // BLK, block, 0)

    return pl.pallas_call(
        body, grid=(B,),
        in_specs=[pl.BlockSpec((S, GW), lambda b: (b, 3)),
                  pl.BlockSpec((S, GW), lambda b: (b, 4)),
                  pl.BlockSpec((4, 1, BLK), lambda b: (0, 0, 0)),
                  pl.BlockSpec((4, BLK, 2 * BLK), lambda b: (0, 0, 0))],
        out_specs=pl.BlockSpec((S, GW), lambda b: (b, 0)),
        out_shape=SDS((B * S, GW), bf16),
        name="swa_fwd", compiler_params=_cp("parallel"))(proj, proj, sinks, bias)


def _swa_bwd(proj, sinks, bias, dy, B, S, comm=None):
    def body(q_ref, kv_ref, s_ref, b_ref, dy_ref, dq_ref, dkv_ref, ds_ref, db_ref, acc_ref):
        @pl.when(pl.program_id(0) == 0)
        def _():
            ds_ref[...] = jnp.zeros_like(ds_ref)
            db_ref[...] = jnp.zeros_like(db_ref)
        acc_ref[...] = jnp.zeros_like(acc_ref)

        def block(n, c):
            rows = pl.ds(pl.multiple_of(n * BLK, BLK), BLK)
            prev = pl.ds(pl.multiple_of(jnp.maximum(n - 1, 0) * BLK, BLK), BLK)
            k2 = jnp.concatenate([kv_ref[prev, 0:BLK], kv_ref[rows, 0:BLK]], axis=0).astype(f32)
            v2 = jnp.concatenate([kv_ref[prev, BLK:2 * BLK], kv_ref[rows, BLK:2 * BLK]], axis=0).astype(f32)
            fn = functools.partial(_swa_block, n=n)
            _, vjp = jax.vjp(fn, q_ref[rows, 0:BLK].astype(f32), q_ref[rows, BLK:2 * BLK].astype(f32), k2, v2,
                             [s_ref[h] for h in range(4)], [b_ref[h] for h in range(4)])
            dq0, dq1, dk2, dv2, dss, dbs = vjp((dy_ref[rows, 0:BLK].astype(f32), dy_ref[rows, BLK:2 * BLK].astype(f32)))
            dq_ref[rows, 0:BLK] = dq0.astype(bf16)
            dq_ref[rows, BLK:2 * BLK] = dq1.astype(bf16)
            for h in range(4):
                ds_ref[h] += dss[h]
                db_ref[h] += dbs[h]
            acc_ref[prev, 0:BLK] += dk2[0:BLK]
            acc_ref[rows, 0:BLK] += dk2[BLK:2 * BLK]
            acc_ref[prev, BLK:2 * BLK] += dv2[0:BLK]
            acc_ref[rows, BLK:2 * BLK] += dv2[BLK:2 * BLK]
            return c
        lax.fori_loop(0, S // BLK, block, 0)
        dkv_ref[...] = acc_ref[...].astype(bf16)

    c_args, c_in, c_out, c_shapes, aliases, c_scratch = _host_specs(comm, 5, 4)
    step = lambda v: (lambda: pl.program_id(0) == v)
    return pl.pallas_call(
        _host(body, 5, 4, 1, comm, step(0), step(B - 1)), grid=(B,),
        in_specs=[pl.BlockSpec((S, GW), lambda b: (b, 3)),
                  pl.BlockSpec((S, GW), lambda b: (b, 4)),
                  pl.BlockSpec((4, 1, BLK), lambda b: (0, 0, 0)),
                  pl.BlockSpec((4, BLK, 2 * BLK), lambda b: (0, 0, 0)),
                  pl.BlockSpec((S, GW), lambda b: (b, 0))] + c_in,
        out_specs=[pl.BlockSpec((S, GW), lambda b: (b, 0)),
                   pl.BlockSpec((S, GW), lambda b: (b, 0)),
                   pl.BlockSpec((4, 1, BLK), lambda b: (0, 0, 0)),
                   pl.BlockSpec((4, BLK, 2 * BLK), lambda b: (0, 0, 0))] + c_out,
        out_shape=[SDS((B * S, GW), bf16), SDS((B * S, GW), bf16), SDS((4, 1, BLK), f32),
                   SDS((4, BLK, 2 * BLK), f32)] + c_shapes,
        input_output_aliases=aliases, scratch_shapes=[pltpu.VMEM((S, GW), f32)] + c_scratch,
        name="swa_bwd" if comm is None else "swa_bwd_exchange",
        compiler_params=_cp("arbitrary"))(proj, proj, sinks, bias, dy, *c_args)


def _log1m_parts(z):
    t = jnp.exp(-jnp.abs(z))
    return jnp.minimum(-z, 0.0) - jnp.log(1.0 + t), t


def _log1m(z):
    return _log1m_parts(z)[0]


def _sigmoid_from(z, t):
    return jnp.where(z >= 0.0, 1.0, t) / (1.0 + t)


def _sb_consts(tri):
    r2, c2 = _iota((2 * BLK, 2 * BLK), 0), _iota((2 * BLK, 2 * BLK), 1)
    tri2 = (tri(r2, c2) & (r2 // BLK == c2 // BLK)).astype(bf16)
    ri, ci = _iota((BLK, 2 * BLK), 0), _iota((BLK, 2 * BLK), 1)
    strict2 = (ci % BLK) < ri
    head0 = _iota((BLK, BLK), 1) < HD
    return tri2, strict2, head0


def _sb_stack_kv(k_ref, v_ref, kst_ref, vst_ref, head0, nb):
    def one(kb, c):
        krows = pl.ds(pl.multiple_of(kb * BLK, BLK), BLK)
        for p in range(2):
            for src, dst in ((k_ref, kst_ref), (v_ref, vst_ref)):
                t = src[krows, p * BLK:(p + 1) * BLK]
                dst[p, kb] = _mx(jnp.concatenate([jnp.where(head0, t, 0.0), jnp.where(head0, 0.0, t)], axis=0))
        return c
    lax.fori_loop(0, nb, one, 0)


def _sb_load_kv(kst_ref, vst_ref, kb):
    return [kst_ref[p, kb] for p in range(2)], [vst_ref[p, kb] for p in range(2)]


def _two_halves(a, b):
    return jnp.concatenate([jnp.broadcast_to(a, (BLK, BLK)), jnp.broadcast_to(b, (BLK, BLK))], axis=1)


def _half_sums(t):
    return jnp.sum(t[:, :BLK], axis=-1, keepdims=True), jnp.sum(t[:, BLK:], axis=-1, keepdims=True)


def _sb_fwd(proj, B, S, comm=None):
    def body(q_ref, k_ref, v_ref, y_ref, lt_ref, kst_ref, vst_ref):
        ci = _iota((BLK, BLK), 1)
        above2, strict2, head0 = _sb_consts(lambda r, c: r > c)
        _sb_stack_kv(k_ref, v_ref, kst_ref, vst_ref, head0, S // BLK)

        def step(qs, kbs, carry, diag):
            ok = [None if diag else kb >= 0 for kb in kbs]
            kv = [_sb_load_kv(kst_ref, vst_ref, jnp.maximum(kb, 0)) for kb in kbs]
            zs = [[lax.dot_general(qs[p], kks[p], NT, preferred_element_type=f32) for p in range(2)] for kks, _ in kv]
            Ls = [[_log1m(z) for z in zu] for zu in zs]
            if diag:
                Ls = [[jnp.where(strict2, L, 0.0) for L in Lu] for Lu in Ls]
            tails = [[_split_dot(L, above2) for L in Lu] for Lu in Ls]
            carry = list(carry)
            for u in range(len(kbs)):
                for p in range(2):
                    R0, R1, acc = carry[3 * p:3 * p + 3]
                    w = jnp.exp(zs[u][p] + Ls[u][p] + tails[u][p] + _two_halves(R0, R1))
                    s0, s1 = _half_sums(Ls[u][p])
                    if diag:
                        w = jnp.where(strict2, w, 0.0)
                    else:
                        w, s0, s1 = (jnp.where(ok[u], t, 0.0) for t in (w, s0, s1))
                    acc = acc + jnp.dot(_mx(w), kv[u][1][p], preferred_element_type=f32)
                    carry[3 * p:3 * p + 3] = [R0 + s0, R1 + s1, acc]
            return tuple(carry)

        def qblock(n, c):
            qrows = pl.ds(pl.multiple_of(n * BLK, BLK), BLK)
            qs = [_mx(q_ref[qrows, p * BLK:(p + 1) * BLK] * (HD ** -0.5)) for p in range(2)]
            z1, z2 = jnp.zeros((BLK, 1), f32), jnp.zeros((BLK, BLK), f32)
            carry = step(qs, [n], (z1, z1, z2, z1, z1, z2), True)
            trips = (n + SB_UNROLL - 1) // SB_UNROLL

            def live(st):
                worst = jnp.maximum(jnp.maximum(st[1], st[2]), jnp.maximum(st[4], st[5]))
                return (st[0] < trips) & (jnp.max(worst) > SB_CUT)

            def trip(st):
                i = st[0]
                return (i + 1,) + step(qs, [n - 1 - SB_UNROLL * i - u for u in range(SB_UNROLL)], st[1:], False)
            done, *res = lax.while_loop(live, trip, (jnp.int32(0),) + carry)
            lt = jnp.where(ci == SB_HEADS, done.astype(f32), 0.0)
            for p in range(2):
                y_ref[qrows, p * BLK:(p + 1) * BLK] = res[3 * p + 2].astype(bf16)
                lt = lt + jnp.where(ci == 2 * p, res[3 * p], 0.0) + jnp.where(ci == 2 * p + 1, res[3 * p + 1], 0.0)
            lt_ref[qrows, :] = lt
            return c
        lax.fori_loop(0, S // BLK, qblock, 0)

    spec = lambda j: pl.BlockSpec((S, GW), lambda b: (b, j))
    c_args, c_in, c_out, c_shapes, aliases, c_scratch = _host_specs(comm, 3, 2)
    step = lambda v: (lambda: pl.program_id(0) == v)
    stacked = pltpu.VMEM((2, S // BLK, 2 * BLK, BLK), _MXU)
    return pl.pallas_call(
        _host(body, 3, 2, 2, comm, step(0), step(B - 1)), grid=(B,),
        in_specs=[spec(5), spec(6), spec(7)] + c_in,
        out_specs=[pl.BlockSpec((S, GW), lambda b: (b, 0)), pl.BlockSpec((S, BLK), lambda b: (b, 0))] + c_out,
        out_shape=[SDS((B * S, GW), bf16), SDS((B * S, BLK), f32)] + c_shapes,
        input_output_aliases=aliases, scratch_shapes=[stacked, stacked] + c_scratch,
        name="sb_fwd" if comm is None else "sb_fwd_gather",
        compiler_params=_cp("arbitrary"))(proj, proj, proj, *c_args)


def _sb_bwd(proj, ltot, dy, B, S, comm=None):
    def body(q_ref, k_ref, v_ref, lt_ref, dy_ref, dq_ref, dk_ref, dv_ref, dka_ref, dva_ref, kst_ref, vst_ref):
        ci = _iota((BLK, BLK), 1)
        upto2, strict2, head0 = _sb_consts(lambda r, c: r <= c)
        below2, _, _ = _sb_consts(lambda r, c: r < c)
        dka_ref[...] = jnp.zeros_like(dka_ref)
        dva_ref[...] = jnp.zeros_like(dva_ref)
        _sb_stack_kv(k_ref, v_ref, kst_ref, vst_ref, head0, S // BLK)

        def step(qs, dos, lts, kbs, last, carry, diag):
            U = range(len(kbs))
            ok = [None if diag else kb <= last for kb in kbs]
            kbs = [jnp.minimum(kb, last) for kb in kbs]
            kv = [_sb_load_kv(kst_ref, vst_ref, kb) for kb in kbs]
            zs = [[lax.dot_general(qs[p], kv[u][0][p], NT, preferred_element_type=f32) for p in range(2)] for u in U]
            dws = [[lax.dot_general(dos[p], kv[u][1][p], NT, preferred_element_type=f32) for p in range(2)] for u in U]
            parts = [[_log1m_parts(z) for z in zu] for zu in zs]
            Ls = [[lt[0] for lt in pu] for pu in parts]
            if diag:
                Ls = [[jnp.where(strict2, L, 0.0) for L in Lu] for Lu in Ls]
            pins = [[_split_dot(L, upto2) for L in Lu] for Lu in Ls]
            carry = list(carry)
            ws, das = [], []
            for u in U:
                wu, dau = [], []
                for p in range(2):
                    PL0, PL1 = carry[5 * p], carry[5 * p + 1]
                    tail = _two_halves(lts[2 * p] - PL0, lts[2 * p + 1] - PL1) - pins[u][p]
                    w = jnp.exp(zs[u][p] + Ls[u][p] + tail)
                    l0, l1 = _half_sums(Ls[u][p])
                    if diag:
                        w = jnp.where(strict2, w, 0.0)
                    else:
                        w, l0, l1 = (jnp.where(ok[u], t, 0.0) for t in (w, l0, l1))
                    carry[5 * p], carry[5 * p + 1] = PL0 + l0, PL1 + l1
                    wu.append(w)
                    dau.append(w * dws[u][p])
                ws.append(wu)
                das.append(dau)
            pexs = [[_split_dot(da, below2) for da in dau] for dau in das]
            dzs = []
            for u in U:
                dzu = []
                for p in range(2):
                    dL = _two_halves(carry[5 * p + 2], carry[5 * p + 3]) + pexs[u][p]
                    sg = _sigmoid_from(zs[u][p], parts[u][p][1])
                    dz = das[u][p] * (1.0 - sg) - dL * sg
                    dz = jnp.where(strict2 if diag else ok[u], dz, 0.0)
                    a0, a1 = _half_sums(das[u][p])
                    carry[5 * p + 2], carry[5 * p + 3] = carry[5 * p + 2] + a0, carry[5 * p + 3] + a1
                    dzu.append(_mx(dz))
                dzs.append(dzu)
            dqs = [[jnp.dot(dzs[u][p], kv[u][0][p], preferred_element_type=f32) for p in range(2)] for u in U]
            dks = [[lax.dot_general(dzs[u][p], qs[p], TN, preferred_element_type=f32) for p in range(2)] for u in U]
            dvs = [[lax.dot_general(_mx(ws[u][p]), dos[p], TN, preferred_element_type=f32) for p in range(2)] for u in U]
            for u in U:
                krows = pl.ds(pl.multiple_of(kbs[u] * BLK, BLK), BLK)
                for p in range(2):
                    lanes = slice(p * BLK, (p + 1) * BLK)
                    dka_ref[krows, lanes] += jnp.where(head0, dks[u][p][:BLK], dks[u][p][BLK:])
                    dva_ref[krows, lanes] += jnp.where(head0, dvs[u][p][:BLK], dvs[u][p][BLK:])
                    carry[5 * p + 4] = carry[5 * p + 4] + dqs[u][p]
            return tuple(carry)

        def qblock(n, c):
            qrows = pl.ds(pl.multiple_of(n * BLK, BLK), BLK)
            ltb = lt_ref[qrows, :]
            lts = [jnp.sum(jnp.where(ci == h, ltb, 0.0), axis=-1, keepdims=True) for h in range(4)]
            qs = [_mx(q_ref[qrows, p * BLK:(p + 1) * BLK] * (HD ** -0.5)) for p in range(2)]
            dos = [_mx(dy_ref[qrows, p * BLK:(p + 1) * BLK]) for p in range(2)]
            z1, z2 = jnp.zeros((BLK, 1), f32), jnp.zeros((BLK, BLK), f32)
            done = jnp.max(jnp.where(ci == SB_HEADS, ltb, 0.0)).astype(jnp.int32)
            first = jnp.maximum(n - SB_UNROLL * done, 0)
            carry = lax.fori_loop(
                0, (n - first + SB_UNROLL - 1) // SB_UNROLL,
                lambda i, cr: step(qs, dos, lts, [first + SB_UNROLL * i + u for u in range(SB_UNROLL)], n - 1, cr, False),
                (z1, z1, z1, z1, z2) * 2)
            res = step(qs, dos, lts, [n], n, carry, True)
            for p in range(2):
                dq_ref[qrows, p * BLK:(p + 1) * BLK] = (res[5 * p + 4] * (HD ** -0.5)).astype(bf16)
            return c
        lax.fori_loop(0, S // BLK, qblock, 0)
        dk_ref[...] = dka_ref[...].astype(bf16)
        dv_ref[...] = dva_ref[...].astype(bf16)

    spec = lambda j: pl.BlockSpec((S, GW), lambda b: (b, j))
    o = pl.BlockSpec((S, GW), lambda b: (b, 0))
    c_args, c_in, c_out, c_shapes, aliases, c_scratch = _host_specs(comm, 5, 3)
    step = lambda v: (lambda: pl.program_id(0) == v)
    stacked = pltpu.VMEM((2, S // BLK, 2 * BLK, BLK), _MXU)
    return pl.pallas_call(
        _host(body, 5, 3, 4, comm, step(0), step(B - 1)), grid=(B,),
        in_specs=[spec(5), spec(6), spec(7), pl.BlockSpec((S, BLK), lambda b: (b, 0)), o] + c_in,
        out_specs=[o, o, o] + c_out,
        out_shape=[SDS((B * S, GW), bf16)] * 3 + c_shapes,
        input_output_aliases=aliases,
        scratch_shapes=[pltpu.VMEM((S, GW), f32), pltpu.VMEM((S, GW), f32), stacked, stacked] + c_scratch,
        name="sb_bwd" if comm is None else "sb_bwd_exchange",
        compiler_params=_cp("arbitrary"))(proj, proj, proj, ltot, dy, *c_args)


def _bias_expand(rel_bias_t, bucket):
    n = bucket.shape[1]

    def body(r_ref, b_ref, o_ref):
        onehot = (_iota((N_BUCKETS, n), 0) == b_ref[...]).astype(f32)
        o_ref[...] = jnp.dot(r_ref[...], onehot, precision=HIGHEST, preferred_element_type=f32)
    return pl.pallas_call(body, out_shape=SDS((rel_bias_t.shape[0], n), f32), name="bias_expand",
                          compiler_params=_cp())(rel_bias_t, bucket)


def _bias_reduce(dbias, bucket):
    n = bucket.shape[1]

    def body(*refs):
        b_ref, g_ref = refs[-2], refs[-1]
        d = refs[0][...]
        for r in refs[1:-2]:
            d = d + r[...]
        onehot = (_iota((N_BUCKETS, n), 0) == b_ref[...]).astype(f32)
        g_ref[...] = lax.dot_general(d, onehot, NT, precision=HIGHEST, preferred_element_type=f32)
    return pl.pallas_call(body, out_shape=SDS((dbias[0].shape[0], N_BUCKETS), f32), name="bias_reduce",
                          compiler_params=_cp())(*dbias, bucket)


def _adamw(w, g, m, v, tr, name, comm=None):
    R, C = w.shape

    def body(w_ref, g_ref, m_ref, v_ref, d_ref, m2_ref, v2_ref):
        gv = g_ref[...]
        m2 = ADAM_B1 * m_ref[...] + (1.0 - ADAM_B1) * gv
        v2 = ADAM_B2 * v_ref[...] + (1.0 - ADAM_B2) * (gv * gv)
        m_hat = m2 / (1.0 - ADAM_B1 ** ADAM_STEP)
        v_hat = v2 / (1.0 - ADAM_B2 ** ADAM_STEP)
        d_ref[...] = -ADAM_LR * (m_hat / (jnp.sqrt(v_hat) + ADAM_EPS) + ADAM_WD * w_ref[...])
        m2_ref[...] = m2
        v2_ref[...] = v2

    spec = pl.BlockSpec((tr, C), lambda i: (i, 0))
    return _hosted_call(body, R // tr, [spec] * 4, [spec] * 3, [SDS((R, C), f32)] * 3, [], (w, g, m, v), name, comm)


ANY = pl.BlockSpec(memory_space=pl.ANY)


def _place():
    x, y, c = lax.axis_index("x"), lax.axis_index("y"), lax.axis_index("c")
    chips = [(1 - x, y), (x, 1 - y), (1 - x, 1 - y)]
    return x, y, c, chips


def _cast_slots(w, kidx):
    L, a, b = w.shape
    ta = a // 2

    def body(k_ref, *refs):
        for l in range(L):
            refs[L + l][0] = refs[l][0].astype(bf16)

    return pl.pallas_call(
        body,
        grid_spec=pltpu.PrefetchScalarGridSpec(
            num_scalar_prefetch=1, grid=(a // ta,),
            in_specs=[pl.BlockSpec((1, ta, b), functools.partial(lambda i, k_ref, l: (l, i, 0), l=l)) for l in range(L)],
            out_specs=[pl.BlockSpec((1, ta, b), lambda i, k_ref: (k_ref[0], i, 0)) for _ in range(L)]),
        out_shape=[SDS((N_CHIPS, a, b), bf16)] * L,
        name="cast_slots", compiler_params=_cp("parallel"))(kidx, *([w] * L))


class _GatherComm:
    def __init__(self, bufs):
        self.inputs = list(bufs)
        self.out_shape = [SDS(b.shape, b.dtype) for b in bufs]
        self.aliased = True
        self.scratch = [pltpu.SemaphoreType.DMA((3 * len(bufs),))] * 4

    def _copies(self, i_refs, o_refs, sems):
        send1, recv1, send2, recv2 = sems
        x, y, c, chips = _place()
        k = 2 * x + y
        first, got1, second, got2 = [], [], [], []
        for i, buf in enumerate(self.inputs):
            h = buf.shape[1] // 2
            mine, theirs = pl.ds(c * h, h), pl.ds((1 - c) * h, h)
            for j, (cx, cy) in enumerate(chips):
                s = 3 * i + j
                first.append(pltpu.make_async_remote_copy(
                    src_ref=i_refs[i].at[k, mine], dst_ref=o_refs[i].at[k, mine], send_sem=send1.at[s],
                    recv_sem=recv1.at[s], device_id=(cx, cy, c), device_id_type=MESH))
                a = o_refs[i].at[2 * cx + cy, mine]
                got1.append(pltpu.make_async_remote_copy(
                    src_ref=a, dst_ref=a, send_sem=send1.at[s], recv_sem=recv1.at[s],
                    device_id=(cx, cy, c), device_id_type=MESH))
                second.append(pltpu.make_async_remote_copy(
                    src_ref=a, dst_ref=a, send_sem=send2.at[s], recv_sem=recv2.at[s],
                    device_id=(x, y, 1 - c), device_id_type=MESH))
                b = o_refs[i].at[2 * cx + cy, theirs]
                got2.append(pltpu.make_async_remote_copy(
                    src_ref=b, dst_ref=b, send_sem=send2.at[s], recv_sem=recv2.at[s],
                    device_id=(x, y, 1 - c), device_id_type=MESH))
        return first, got1, second, got2

    def start(self, i_refs, o_refs, sems):
        for cp in self._copies(i_refs, o_refs, sems)[0]:
            cp.start()

    def finish(self, i_refs, o_refs, sems):
        first, got1, second, got2 = self._copies(i_refs, o_refs, sems)
        for g, cp in zip(got1, second):
            g.wait_recv()
            cp.start()
        for g in got2:
            g.wait_recv()
        for cp in first + second:
            cp.wait_send()


class _PairExchangeComm:
    def __init__(self, gs):
        self.inputs = list(gs)
        self.out_shape = [SDS((g.shape[0], g.shape[1] // 2, g.shape[2]), g.dtype) for g in gs]
        self.aliased = False
        self.scratch = [pltpu.SemaphoreType.DMA((len(gs),))] * 2

    def _copies(self, i_refs, o_refs, sems):
        send, recv = sems
        x, y, c, _ = _place()
        cps = []
        for i, g in enumerate(self.inputs):
            h = g.shape[1] // 2
            cps.append(pltpu.make_async_remote_copy(
                src_ref=i_refs[i].at[:, pl.ds((1 - c) * h, h)], dst_ref=o_refs[i], send_sem=send.at[i], recv_sem=recv.at[i],
                device_id=(x, y, 1 - c), device_id_type=MESH))
        return cps

    def start(self, i_refs, o_refs, sems):
        for cp in self._copies(i_refs, o_refs, sems):
            cp.start()

    def finish(self, i_refs, o_refs, sems):
        for cp in self._copies(i_refs, o_refs, sems):
            cp.wait()


class _ChipExchangeComm:
    def __init__(self, qs):
        self.inputs = list(qs)
        self.out_shape = [SDS(q.shape, q.dtype) for q in qs]
        self.aliased = False
        self.scratch = [pltpu.SemaphoreType.DMA((3 * len(qs),))] * 2

    def _copies(self, i_refs, o_refs, sems):
        send, recv = sems
        x, y, c, chips = _place()
        k = 2 * x + y
        cps, got = [], []
        for i in range(len(self.inputs)):
            for j, (cx, cy) in enumerate(chips):
                s = 3 * i + j
                cps.append(pltpu.make_async_remote_copy(
                    src_ref=i_refs[i].at[2 * cx + cy], dst_ref=o_refs[i].at[k], send_sem=send.at[s],
                    recv_sem=recv.at[s], device_id=(cx, cy, c), device_id_type=MESH))
                a = o_refs[i].at[2 * cx + cy]
                got.append(pltpu.make_async_remote_copy(
                    src_ref=a, dst_ref=a, send_sem=send.at[s], recv_sem=recv.at[s],
                    device_id=(cx, cy, c), device_id_type=MESH))
        return cps, got

    def start(self, i_refs, o_refs, sems):
        for cp in self._copies(i_refs, o_refs, sems)[0]:
            cp.start()

    def finish(self, i_refs, o_refs, sems):
        cps, got = self._copies(i_refs, o_refs, sems)
        for g in got:
            g.wait_recv()
        for cp in cps:
            cp.wait_send()


def _comm_only(comm, name):
    n = len(comm.inputs)

    def body(*refs):
        i_refs, o_refs, sems = refs[:n], refs[n:n + len(comm.out_shape)], refs[n + len(comm.out_shape):]
        comm.start(i_refs, o_refs, sems)
        comm.finish(i_refs, o_refs, sems)

    return pl.pallas_call(
        body, out_shape=comm.out_shape, in_specs=[ANY] * n, out_specs=[ANY] * len(comm.out_shape),
        input_output_aliases={i: i for i in range(n)} if comm.aliased else {},
        scratch_shapes=comm.scratch, name=name,
        compiler_params=pltpu.CompilerParams(has_side_effects=True))(*comm.inputs)


def _host(body, n_in, n_out, n_scratch, comm, first, last):
    if comm is None:
        return body
    ci, co = len(comm.inputs), len(comm.out_shape)

    def wrapped(*refs):
        o = 0
        parts = []
        for n in (n_in, ci, n_out, co, n_scratch):
            parts.append(refs[o:o + n])
            o += n
        hin, cin, hout, cout, hs = parts
        sems = refs[o:]

        @pl.when(first())
        def _():
            comm.start(cin, cout, sems)
        body(*hin, *hout, *hs)

        @pl.when(last())
        def _():
            comm.finish(cin, cout, sems)
    return wrapped


def _host_specs(comm, n_in, n_out):
    if comm is None:
        return [], [], [], [], {}, []
    ci, co = len(comm.inputs), len(comm.out_shape)
    aliases = {n_in + i: n_out + i for i in range(ci)} if comm.aliased else {}
    return comm.inputs, [ANY] * ci, [ANY] * co, comm.out_shape, aliases, comm.scratch


def _pair_add(g, r, cidx, name):
    ns, a, b = g.shape
    h = a // 2
    th = h if h * b * 4 <= 4 * 1024 * 1024 else h // 2

    def body(c_ref, g_ref, r_ref, qb_ref):
        qb_ref[...] = (g_ref[...] + r_ref[...]).astype(bf16)

    nb = h // th
    spec = pl.BlockSpec((1, th, b), lambda s, i, c_ref: (s, i, 0))
    return pl.pallas_call(
        body,
        grid_spec=pltpu.PrefetchScalarGridSpec(
            num_scalar_prefetch=1, grid=(ns, nb),
            in_specs=[pl.BlockSpec((1, th, b), lambda s, i, c_ref: (s, c_ref[0] * nb + i, 0)), spec],
            out_specs=spec),
        out_shape=SDS((ns, h, b), bf16),
        name=name, compiler_params=_cp("parallel", "parallel"))(cidx, g, r)


def _chip_add(g, r1, r2, idx, prev, L, name):
    ns, h, b = r2.shape
    th = h if h * b * 4 <= 4 * 1024 * 1024 else h // 2
    nb = h // th

    def body(s_ref, g_ref, r1_ref, a_ref, b_ref, c_ref, *rest):
        o_ref = rest[-1]
        o_ref[0] = (g_ref[0] + r1_ref[0]) + a_ref[0].astype(f32) + b_ref[0].astype(f32) + c_ref[0].astype(f32)

    other = lambda d: pl.BlockSpec((1, th, b), lambda i, s_ref: ((s_ref[0] + d) % ns, i, 0))
    in_specs = [pl.BlockSpec((1, th, b), lambda i, s_ref: (s_ref[0], s_ref[1] * nb + i, 0)),
                pl.BlockSpec((1, th, b), lambda i, s_ref: (s_ref[0], i, 0)), other(1), other(2), other(3)]
    args = [idx, g, r1, r2, r2, r2]
    aliases = {}
    if prev is not None:
        in_specs.append(ANY)
        args.append(prev)
        aliases = {6: 0}
    return pl.pallas_call(
        body,
        grid_spec=pltpu.PrefetchScalarGridSpec(
            num_scalar_prefetch=1, grid=(nb,), in_specs=in_specs,
            out_specs=pl.BlockSpec((1, th, b), lambda i, s_ref: (s_ref[2], s_ref[1] * nb + i, 0))),
        out_shape=SDS((L, 2 * h, b), f32), input_output_aliases=aliases,
        name=name, compiler_params=_cp("arbitrary"))(*args)


def _pair_share(gs, hs):
    n = len(gs)
    L = gs[0].shape[0]

    def body(*refs):
        i_refs, o_refs = refs[:n], refs[n:2 * n]
        send, recv = refs[2 * n:]
        x, y, c, _ = _place()
        cps = []
        for i in range(n):
            for l in range(L):
                mine = pl.ds(c * hs[i], hs[i])
                cp = pltpu.make_async_remote_copy(
                    src_ref=i_refs[i].at[l, mine], dst_ref=o_refs[i].at[l, mine], send_sem=send.at[i * L + l],
                    recv_sem=recv.at[i * L + l], device_id=(x, y, 1 - c), device_id_type=MESH)
                cp.start()
                cps.append(cp)
        for i in range(n):
            for l in range(L):
                got = o_refs[i].at[l, pl.ds((1 - c) * hs[i], hs[i])]
                pltpu.make_async_remote_copy(
                    src_ref=got, dst_ref=got, send_sem=send.at[i * L + l], recv_sem=recv.at[i * L + l],
                    device_id=(x, y, 1 - c), device_id_type=MESH).wait_recv()
        for cp in cps:
            cp.wait_send()

    return pl.pallas_call(
        body, out_shape=[SDS(g.shape, g.dtype) for g in gs], in_specs=[ANY] * n, out_specs=[ANY] * n,
        input_output_aliases={i: i for i in range(n)},
        scratch_shapes=[pltpu.SemaphoreType.DMA((n * L,))] * 2,
        name="grad_pair_share", compiler_params=pltpu.CompilerParams(has_side_effects=True))(*gs)


class _SwapComm:
    def __init__(self, arrays):
        self.inputs = list(arrays)
        self.out_shape = [SDS(a.shape, a.dtype) for a in arrays]
        self.aliased = False
        self.scratch = [pltpu.SemaphoreType.DMA((len(arrays),))] * 2

    def _copies(self, i_refs, o_refs, sems):
        send, recv = sems
        x, y, c, _ = _place()
        return [pltpu.make_async_remote_copy(
            src_ref=i_refs[i], dst_ref=o_refs[i], send_sem=send.at[i], recv_sem=recv.at[i],
            device_id=(x, y, 1 - c), device_id_type=MESH) for i in range(len(self.inputs))]

    def start(self, i_refs, o_refs, sems):
        for cp in self._copies(i_refs, o_refs, sems):
            cp.start()

    def finish(self, i_refs, o_refs, sems):
        for cp in self._copies(i_refs, o_refs, sems):
            cp.wait()


class _SlotShareComm:
    def __init__(self, bufs):
        self.inputs = list(bufs)
        self.out_shape = [SDS(b.shape, b.dtype) for b in bufs]
        self.aliased = True
        self.scratch = [pltpu.SemaphoreType.DMA((3 * len(bufs),))] * 2

    def _copies(self, i_refs, o_refs, sems):
        send, recv = sems
        x, y, c, chips = _place()
        k = 2 * x + y
        cps, got = [], []
        for i in range(len(self.inputs)):
            for j, (cx, cy) in enumerate(chips):
                s = 3 * i + j
                cps.append(pltpu.make_async_remote_copy(
                    src_ref=i_refs[i].at[k], dst_ref=o_refs[i].at[k], send_sem=send.at[s], recv_sem=recv.at[s],
                    device_id=(cx, cy, c), device_id_type=MESH))
                a = o_refs[i].at[2 * cx + cy]
                got.append(pltpu.make_async_remote_copy(
                    src_ref=a, dst_ref=a, send_sem=send.at[s], recv_sem=recv.at[s],
                    device_id=(cx, cy, c), device_id_type=MESH))
        return cps, got

    def start(self, i_refs, o_refs, sems):
        for cp in self._copies(i_refs, o_refs, sems)[0]:
            cp.start()

    def finish(self, i_refs, o_refs, sems):
        cps, got = self._copies(i_refs, o_refs, sems)
        for g in got:
            g.wait_recv()
        for cp in cps:
            cp.wait_send()


def _pair_sum_slot(mine, theirs, kidx):
    R, C = mine.shape

    def body(k_ref, a_ref, b_ref, o_ref):
        o_ref[0] = a_ref[...] + b_ref[...]

    spec = pl.BlockSpec((R, C), lambda i, k_ref: (0, 0))
    return pl.pallas_call(
        body,
        grid_spec=pltpu.PrefetchScalarGridSpec(
            num_scalar_prefetch=1, grid=(1,), in_specs=[spec, spec],
            out_specs=pl.BlockSpec((1, R, C), lambda i, k_ref: (k_ref[0], 0, 0))),
        out_shape=SDS((N_CHIPS, R, C), f32), name="small_pair_sum", compiler_params=_cp("arbitrary"))(kidx, mine, theirs)


def _small_sum(g):
    n, R, C = g.shape

    def body(g_ref, o_ref):
        acc = g_ref[0]
        for j in range(1, n):
            acc = acc + g_ref[j]
        o_ref[...] = acc
    return pl.pallas_call(body, out_shape=SDS((R, C), f32), name="small_sum", compiler_params=_cp())(g)


PACK_COLS = 1024


def _rows_of(shape):
    n = int(np.prod(shape)) if len(shape) else 1
    return -(-n // (8 * PACK_COLS)) * 8


def _pack(parts):
    blocks = []
    for p in parts:
        flat = p.reshape(-1)
        r = _rows_of(p.shape)
        blocks.append(jnp.pad(flat, (0, r * PACK_COLS - flat.shape[0])).reshape(r, PACK_COLS))
    return jnp.concatenate(blocks, axis=0)


def _unpack(buf, shapes):
    out, off = [], 0
    for s in shapes:
        n = int(np.prod(s)) if len(s) else 1
        r = _rows_of(s)
        out.append(buf[off:off + r].reshape(-1)[:n].reshape(s))
        off += r
    return out


def _block_diag(w):
    g, a, _ = w.shape
    out = jnp.zeros((g * a, g * a), w.dtype)
    for i in range(g):
        out = lax.dynamic_update_slice(out, w[i], (i * a, i * a))
    return out


def kernel(x, w_in, w_out, sgu_w, sgu_b, pool_w, pool_scale, swa_sinks, rel_bias, mix_out_gain, norm_mix, norm_ffn, w_gate_up, w_down, norm_final, loss_target, m_w_in, m_w_out, m_sgu_w, m_sgu_b, m_pool_w, m_pool_scale, m_swa_sinks, m_rel_bias, m_mix_out_gain, m_norm_mix, m_norm_ffn, m_w_gate_up, m_w_down, m_norm_final, v_w_in, v_w_out, v_sgu_w, v_sgu_b, v_pool_w, v_pool_scale, v_swa_sinks, v_rel_bias, v_mix_out_gain, v_norm_mix, v_norm_ffn, v_w_gate_up, v_w_down, v_norm_final):
    B, S, D = x.shape
    T = B * S
    L = w_in.shape[0]
    tm = min(512, T)
    F = w_down.shape[1] * N_CHIPS
    xi, yi, ci = lax.axis_index("x"), lax.axis_index("y"), lax.axis_index("c")
    cidx = jnp.reshape(ci, (1,)).astype(jnp.int32)
    kidx = jnp.reshape(2 * xi + yi, (1,)).astype(jnp.int32)

    big = [w_in, w_out, w_gate_up, w_down]
    slots = [_cast_slots(w, kidx) for w in big]
    gather = lambda pi, l: _GatherComm([slots[pi][l]])
    Win, Wo, Wgu, Wd = ([None] * L for _ in range(4))
    Win[0], = _comm_only(gather(0, 0), "gather_weights")

    bucket = jnp.asarray(_t5_bucket_table().reshape(1, -1))
    bias_tab = _bias_expand(rel_bias.T, bucket).reshape(4, BLK, 2 * BLK)

    row = lambda v: v.reshape(1, -1)
    xc = x.reshape(T, D)
    tgt = loss_target.reshape(T, D)
    saved = []
    for l in range(L):
        h1, proj, wo = _norm_mm(xc, row(norm_mix[l]), Win[l], tm, gather(1, l))
        Wo[l] = wo.reshape(D, D)
        bexp = jnp.repeat(sgu_b[l].T, HD, axis=1)
        wbd = _block_diag(pool_w[l])
        sk = jnp.broadcast_to(swa_sinks[l][:, None, None], (4, 1, BLK))
        ya = _sgu_fwd(proj, sgu_w[l], bexp, B, S)
        yb = _pool_fwd(proj, wbd, row(pool_scale[l]), B, S)
        yc = _swa_fwd(proj, sk, bias_tab, B, S)
        if l == 0:
            yd, lt, wd, Wgu[0] = _sb_fwd(proj, B, S, _GatherComm([slots[3][0], slots[2][0]]))
        else:
            yd, lt, wd = _sb_fwd(proj, B, S, gather(3, l))
        Wd[l] = wd.reshape(F, D)
        ys = (ya, yb, yc, yd)
        ycn, x1 = _gnorm_mm_res(ys, row(mix_out_gain[l]), Wo[l], xc, tm)
        if l + 1 < L:
            h2, gu, act, Wgu[l + 1] = _norm_mm_swiglu(x1, row(norm_ffn[l]), Wgu[l], tm, gather(2, l + 1))
            x2, Win[l + 1] = _mm_res(act, Wd[l], x1, tm, gather(0, l + 1))
        else:
            h2, gu, act = _norm_mm_swiglu(x1, row(norm_ffn[l]), Wgu[l], tm)
            x2, = _mm_res(act, Wd[l], x1, tm)
        saved.append((xc, h1, proj, bexp, wbd, sk, ys, lt, ycn, x1, h2, gu, act))
        xc = x2

    dx, g_final, loss_v = _final_loss(xc, row(norm_final), tgt, tm)

    tk = min(T, 2048)
    gW = [[None] * L for _ in range(4)]
    g_sgu_w, g_sgu_b, g_pool_w, g_pool_scale, g_sinks, g_bias = ([None] * L for _ in range(6))
    g_out_gain, g_mix, g_ffn = ([None] * L for _ in range(3))
    reduced = [None] * 4
    sums = {}

    def pair_comm(keys):
        return _PairExchangeComm([gW[pi][l] for pi, l in keys])

    def after_pair(keys, r1):
        for (pi, l), r in zip(keys, r1):
            sums[pi, l] = (r, _pair_add(gW[pi][l], r, cidx, "grad_pair_add"))

    def chip_comm(keys):
        return _ChipExchangeComm([sums[k][1] for k in keys])

    def after_chip(keys, r2):
        for (pi, l), r in zip(keys, r2):
            idx = jnp.stack([2 * xi + yi, ci, jnp.int32(l)]).astype(jnp.int32)
            reduced[pi] = _chip_add(gW[pi][l], sums.pop((pi, l))[0], r, idx, reduced[pi], L, "grad_chip_add")

    for l in reversed(range(L)):
        x0, h1, proj, bexp, wbd, sk, ys, lt, ycn, x1, h2, gu, act = saved[l]
        if l + 1 < L:
            dgu, *r2 = _dact(dx, Wd[l], gu, tm, chip_comm([(0, l + 1)]))
            after_chip([(0, l + 1)], r2)
        else:
            dgu, = _dact(dx, Wd[l], gu, tm)
        gW[3][l] = _dw(act, dx, lambda t, s: (t, 0), D, 1, F // 2, tk // 2, "dw_down").reshape(N_CHIPS, F // N_CHIPS, D)
        gW[2][l] = _dw(h2, dgu, lambda t, s: (s // 2, t, s % 2), F // 2, N_CHIPS, D, tk, "dw_gate_up")
        keys = [(2, l), (3, l)]
        dx1, g_ffn[l], *r1 = _dx_norm_bwd(dgu, Wgu[l], x1, row(norm_ffn[l]), dx, tm, "dx_ffn_exchange", pair_comm(keys))
        after_pair(keys, r1)
        gW[1][l] = _dw(ycn, dx1, lambda t, s: (t, 0), D, 1, D, tk, "dw_out").reshape(N_CHIPS, D // N_CHIPS, D)
        dya, dyb, dyc, dyd, g_out_gain[l], *r1 = _dycat(dx1, Wo[l], ys, row(mix_out_gain[l]), tm, pair_comm([(1, l)]))
        after_pair([(1, l)], r1)
        dpa, g_sgu_w[l], dbf, *r2 = _sgu_bwd(proj, sgu_w[l], bexp, dya, B, S, chip_comm([(1, l)]))
        after_chip([(1, l)], r2)
        g_sgu_b[l] = dbf[:, ::HD].T
        dpb, dwbd, dsc = _pool_bwd(proj, wbd, row(pool_scale[l]), dyb, B, S)
        npg = len(POOL_WINDOWS)
        g_pool_w[l] = jnp.stack([dwbd[i * HD:(i + 1) * HD, i * HD:(i + 1) * HD] for i in range(npg)])
        g_pool_scale[l] = dsc[0]
        dcq, dckv, dsk, g_bias[l], *r2 = _swa_bwd(proj, sk, bias_tab, dyc, B, S, chip_comm([(3, l)]))
        after_chip([(3, l)], r2)
        g_sinks[l] = dsk[:, 0, 0] * float(BLK)
        ddq, ddk, ddv, *r2 = _sb_bwd(proj, lt, dyd, B, S, chip_comm([(2, l)]))
        after_chip([(2, l)], r2)
        dproj = [dpa, dpb, dcq, dckv, ddq, ddk, ddv]
        gW[0][l] = _dw_pieces(h1, dproj, w_in.shape[2], N_CHIPS, tk, "dw_in")
        dx, g_mix[l], *r1 = _dx_norm_bwd(dproj, Win[l], x0, row(norm_mix[l]), dx1, tm, "dx_mix_exchange", pair_comm([(0, l)]))
        after_pair([(0, l)], r1)
    grad_x = dx.reshape(B, S, D)

    after_chip([(0, 0)], _comm_only(chip_comm([(0, 0)]), "grad_chip_exchange"))
    g_big = _pair_share(reduced, [g.shape[1] // 2 for g in reduced])

    g_rel_bias = _bias_reduce([g.reshape(4, -1) for g in g_bias], bucket).T
    small_g = [jnp.stack(g_sgu_w), jnp.stack(g_sgu_b), jnp.stack(g_pool_w), jnp.stack(g_pool_scale), jnp.stack(g_sinks),
               g_rel_bias, jnp.concatenate(g_out_gain), jnp.concatenate(g_mix), jnp.concatenate(g_ffn), g_final[0]]
    small_w = [sgu_w, sgu_b, pool_w, pool_scale, swa_sinks, rel_bias, mix_out_gain, norm_mix, norm_ffn, norm_final]
    small_m = [m_sgu_w, m_sgu_b, m_pool_w, m_pool_scale, m_swa_sinks, m_rel_bias, m_mix_out_gain, m_norm_mix, m_norm_ffn, m_norm_final]
    small_v = [v_sgu_w, v_sgu_b, v_pool_w, v_pool_scale, v_swa_sinks, v_rel_bias, v_mix_out_gain, v_norm_mix, v_norm_ffn, v_norm_final]
    shapes = [w.shape for w in small_w]
    mine = _pack(small_g + [loss_v[0, 0:1]])
    theirs, = _comm_only(_SwapComm([mine]), "small_pair_swap")
    share = _SlotShareComm([_pair_sum_slot(mine, theirs, kidx)])

    big_m = [m_w_in, m_w_out, m_w_gate_up, m_w_down]
    big_v = [v_w_in, v_w_out, v_w_gate_up, v_w_down]
    d_big, m_big, v_big = [None] * 4, [None] * 4, [None] * 4
    for pi in (2, 0, 1, 3):
        w, g, m, v = big[pi], g_big[pi], big_m[pi], big_v[pi]
        two = lambda a: a.reshape(-1, a.shape[-1])
        rows = two(w).shape[0]
        d2, m2, v2, *got = _adamw(two(w), two(g), two(m), two(v), rows // 8 if rows >= 2048 else rows, "adamw_big",
                                  share if pi == 2 else None)
        if pi == 2:
            shared, = got
        d_big[pi], m_big[pi], v_big[pi] = d2.reshape(w.shape), m2.reshape(w.shape), v2.reshape(w.shape)

    packed = _small_sum(shared)
    *g_small, loss = _unpack(packed, shapes + [()])
    g_small_packed = _pack(g_small)
    ds, ms, vs = _adamw(_pack(small_w), g_small_packed, _pack(small_m), _pack(small_v), g_small_packed.shape[0], "adamw_small")
    d_small, m_small, v_small = _unpack(ds, shapes), _unpack(ms, shapes), _unpack(vs, shapes)

    def order(bigs, smalls):
        return [bigs[0], bigs[1]] + list(smalls[0:9]) + [bigs[2], bigs[3], smalls[9]]

    return (loss, grad_x, *order(g_big, g_small), *order(d_big, d_small), *order(m_big, m_small), *order(v_big, v_small))
```

```python
import functools

import numpy as np
import jax
import jax.numpy as jnp
from jax import lax
from jax.experimental import pallas as pl
from jax.experimental.pallas import tpu as pltpu

f32 = jnp.float32
bf16 = jnp.bfloat16
_MXU = jnp.bfloat16

EPS = 1e-6
HD = 64
GW = 256
BLK = 128
SB_UNROLL = 2
SB_HEADS = 4
SB_CUT = -110.0
POOL_WINDOWS = (2, 4, 8, 16)
N_BUCKETS = 32
MAX_DISTANCE = 128
N_CHIPS = 4
N_DEV = 8
VMEM_LIMIT = 48 * 1024 * 1024

ADAM_LR = 0.001
ADAM_B1 = 0.9
ADAM_B2 = 0.999
ADAM_EPS = 1e-08
ADAM_WD = 0.01
ADAM_STEP = 10

SDS = jax.ShapeDtypeStruct
MESH = pl.DeviceIdType.MESH
HIGHEST = lax.Precision.HIGHEST
RESIDENT = pl.Buffered(1)
NT = (((1,), (1,)), ((), ()))
TN = (((0,), (0,)), ((), ()))


def _cp(*sem):
    return pltpu.CompilerParams(dimension_semantics=sem if sem else None, vmem_limit_bytes=VMEM_LIMIT)


def _mx(v):
    return v.astype(_MXU)


def _iota(shape, dim):
    return lax.broadcasted_iota(jnp.int32, shape, dim)


def _split_dot(a, tri):
    hi = a.astype(bf16)
    lo = (a - hi.astype(f32)).astype(bf16)
    return jnp.dot(hi, tri, preferred_element_type=f32) + jnp.dot(lo, tri, preferred_element_type=f32)


def _rms(xv):
    return lax.rsqrt(jnp.mean(xv * xv, axis=-1, keepdims=True) + EPS)


def _hosted_call(body, steps, in_specs, out_specs, out_shape, scratch, args, name, comm):
    n_in, n_out = len(in_specs), len(out_specs)
    c_args, c_in, c_out, c_shapes, aliases, c_scratch = _host_specs(comm, n_in, n_out)
    step = lambda v: (lambda: pl.program_id(0) == v)
    return pl.pallas_call(
        _host(body, n_in, n_out, len(scratch), comm, step(0), step(steps - 1)), grid=(steps,),
        in_specs=list(in_specs) + c_in, out_specs=list(out_specs) + c_out, out_shape=list(out_shape) + c_shapes,
        input_output_aliases=aliases, scratch_shapes=list(scratch) + c_scratch,
        name=name if comm is None else name + "_comm", compiler_params=_cp("arbitrary"))(*args, *c_args)


def _norm_mm(x, gain, w, tm, comm=None):
    T, D = x.shape
    NS, _, ns = w.shape

    def body(x_ref, g_ref, w_ref, h_ref, o_ref):
        xv = x_ref[...]
        h = (xv * _rms(xv) * g_ref[...]).astype(bf16)
        h_ref[...] = h
        for s in range(NS):
            o_ref[:, s * ns:(s + 1) * ns] = jnp.dot(_mx(h), w_ref[s], preferred_element_type=f32).astype(bf16)

    return _hosted_call(
        body, T // tm,
        [pl.BlockSpec((tm, D), lambda i: (i, 0)),
         pl.BlockSpec((1, D), lambda i: (0, 0)),
         pl.BlockSpec((NS, D, ns), lambda i: (0, 0, 0), pipeline_mode=RESIDENT)],
        [pl.BlockSpec((tm, D), lambda i: (i, 0)), pl.BlockSpec((tm, NS * ns), lambda i: (i, 0))],
        [SDS((T, D), bf16), SDS((T, NS * ns), bf16)], [], (x, gain, w), "norm_mm_in", comm)


def _norm_mm_swiglu(x, gain, w, tm, comm=None):
    T, D = x.shape
    NS, _, ns = w.shape
    half = NS // 2

    def body(x_ref, g_ref, w_ref, h_ref, gu_ref, a_ref):
        xv = x_ref[...]
        hb = (xv * _rms(xv) * g_ref[...]).astype(bf16)
        h_ref[...] = hb
        h = _mx(hb)
        for s in range(half):
            cols = slice(s * ns, (s + 1) * ns)
            g = jnp.dot(h, w_ref[s], preferred_element_type=f32)
            u = jnp.dot(h, w_ref[s + half], preferred_element_type=f32)
            gu_ref[0, :, cols] = g.astype(bf16)
            gu_ref[1, :, cols] = u.astype(bf16)
            a_ref[:, cols] = (jax.nn.silu(g) * u).astype(bf16)

    c_args, c_in, c_out, c_shapes, aliases, c_scratch = _host_specs(comm, 3, 3)
    step = lambda v: (lambda: pl.program_id(0) == v)
    return pl.pallas_call(
        _host(body, 3, 3, 0, comm, step(0), step(T // tm - 1)), grid=(T // tm,),
        in_specs=[pl.BlockSpec((tm, D), lambda i: (i, 0)),
                  pl.BlockSpec((1, D), lambda i: (0, 0)),
                  pl.BlockSpec((NS, D, ns), lambda i: (0, 0, 0), pipeline_mode=RESIDENT)] + c_in,
        out_specs=[pl.BlockSpec((tm, D), lambda i: (i, 0)),
                   pl.BlockSpec((2, tm, half * ns), lambda i: (0, i, 0)),
                   pl.BlockSpec((tm, half * ns), lambda i: (i, 0))] + c_out,
        out_shape=[SDS((T, D), bf16), SDS((2, T, half * ns), bf16), SDS((T, half * ns), bf16)] + c_shapes,
        input_output_aliases=aliases, scratch_shapes=c_scratch,
        name="norm_mm_swiglu" if comm is None else "norm_mm_swiglu_gather",
        compiler_params=_cp("arbitrary"))(x, gain, w, *c_args)


def _gnorm_mm_res(ys, gain, w, x, tm):
    T, D = x.shape

    def body(ya, yb, yc, yd, g_ref, w_ref, x_ref, yn_ref, o_ref):
        parts = []
        for m, r in enumerate((ya, yb, yc, yd)):
            y = r[...].astype(f32)
            parts.append((y * _rms(y) * g_ref[:, m * GW:(m + 1) * GW]).astype(bf16))
        yn = jnp.concatenate(parts, axis=1)
        yn_ref[...] = yn
        o_ref[...] = x_ref[...] + jnp.dot(_mx(yn), w_ref[...], preferred_element_type=f32)

    yspec = pl.BlockSpec((tm, GW), lambda i: (i, 0))
    return pl.pallas_call(
        body, grid=(T // tm,),
        in_specs=[yspec, yspec, yspec, yspec,
                  pl.BlockSpec((1, D), lambda i: (0, 0)),
                  pl.BlockSpec((D, D), lambda i: (0, 0)),
                  pl.BlockSpec((tm, D), lambda i: (i, 0))],
        out_specs=[pl.BlockSpec((tm, D), lambda i: (i, 0)), pl.BlockSpec((tm, D), lambda i: (i, 0))],
        out_shape=[SDS((T, D), bf16), SDS((T, D), f32)],
        name="gnorm_mm_res", compiler_params=_cp("parallel"))(*ys, gain, w, x)


def _mm_res(a, w, x, tm, comm=None):
    T, D = x.shape
    K = a.shape[1]

    def body(a_ref, w_ref, x_ref, o_ref):
        o_ref[...] = x_ref[...] + jnp.dot(_mx(a_ref[...]), w_ref[...], preferred_element_type=f32)

    return _hosted_call(
        body, T // tm,
        [pl.BlockSpec((tm, K), lambda i: (i, 0)),
         pl.BlockSpec((K, D), lambda i: (0, 0), pipeline_mode=RESIDENT),
         pl.BlockSpec((tm, D), lambda i: (i, 0))],
        [pl.BlockSpec((tm, D), lambda i: (i, 0))], [SDS((T, D), f32)], [], (a, w, x), "mm_res_down", comm)


def _final_loss(x, gain, tgt, tm):
    T, D = x.shape

    def body(x_ref, g_ref, t_ref, dx_ref, dg_ref, l_ref):
        @pl.when(pl.program_id(0) == 0)
        def _():
            dg_ref[...] = jnp.zeros_like(dg_ref)
            l_ref[...] = jnp.zeros_like(l_ref)
        xv = x_ref[...]
        g = g_ref[...]
        r = _rms(xv)
        xh = xv * r
        err = xh * g - t_ref[...]
        l_ref[...] += 0.5 * jnp.sum(jnp.mean(err * err, axis=-1, keepdims=True), axis=0, keepdims=True)
        dy = err * (1.0 / D)
        dg_ref[...] += jnp.sum(dy * xh, axis=0, keepdims=True)
        dxh = dy * g
        dx_ref[...] = r * (dxh - xh * jnp.mean(dxh * xh, axis=-1, keepdims=True))

    return pl.pallas_call(
        body, grid=(T // tm,),
        in_specs=[pl.BlockSpec((tm, D), lambda i: (i, 0)),
                  pl.BlockSpec((1, D), lambda i: (0, 0)),
                  pl.BlockSpec((tm, D), lambda i: (i, 0))],
        out_specs=[pl.BlockSpec((tm, D), lambda i: (i, 0)),
                   pl.BlockSpec((1, D), lambda i: (0, 0)),
                   pl.BlockSpec((1, BLK), lambda i: (0, 0))],
        out_shape=[SDS((T, D), f32), SDS((1, D), f32), SDS((1, BLK), f32)],
        name="final_loss", compiler_params=_cp("arbitrary"))(x, gain, tgt)


def _dact(dx, wd, gu, tm, comm=None):
    T, D = dx.shape
    F = wd.shape[0]
    ns = F // 2

    def body(dx_ref, w_ref, gu_ref, o_ref):
        dxb = _mx(dx_ref[...])
        for s in range(2):
            cols = slice(s * ns, (s + 1) * ns)
            da = lax.dot_general(dxb, w_ref[s * ns:(s + 1) * ns, :], NT, preferred_element_type=f32)
            g = gu_ref[0, :, cols].astype(f32)
            u = gu_ref[1, :, cols].astype(f32)
            sg = jax.nn.sigmoid(g)
            o_ref[0, :, cols] = (da * u * (sg * (1.0 + g * (1.0 - sg)))).astype(bf16)
            o_ref[1, :, cols] = (da * (g * sg)).astype(bf16)

    return _hosted_call(
        body, T // tm,
        [pl.BlockSpec((tm, D), lambda i: (i, 0)),
         pl.BlockSpec((F, D), lambda i: (0, 0), pipeline_mode=RESIDENT),
         pl.BlockSpec((2, tm, F), lambda i: (0, i, 0))],
        [pl.BlockSpec((2, tm, F), lambda i: (0, i, 0))], [SDS((2, T, F), bf16)], [], (dx, wd, gu), "dact", comm)


def _dw(a, b, b_map, ns, NS, tka, tk, name):
    T, Ka = a.shape
    b_block = (tk, ns) if b.ndim == 2 else (1, tk, ns)

    def body(a_ref, b_ref, o_ref):
        bv = b_ref[...] if b.ndim == 2 else b_ref[0]
        part = lax.dot_general(_mx(a_ref[...]), _mx(bv), TN, preferred_element_type=f32)

        @pl.when(pl.program_id(2) == 0)
        def _():
            o_ref[0] = part

        @pl.when(pl.program_id(2) > 0)
        def _():
            o_ref[0] += part

    return pl.pallas_call(
        body, grid=(NS, Ka // tka, T // tk),
        in_specs=[pl.BlockSpec((tk, tka), lambda s, k, t: (t, k)),
                  pl.BlockSpec(b_block, lambda s, k, t: b_map(t, s))],
        out_specs=pl.BlockSpec((1, tka, ns), lambda s, k, t: (s, k, 0)),
        out_shape=SDS((NS, Ka, ns), f32),
        name=name, compiler_params=_cp("parallel", "parallel", "arbitrary"))(a, b)


def _dw_pieces(a, pieces, ns, NS, tk, name):
    T, Ka = a.shape
    n = len(pieces)

    def body(*refs):
        a_ref, b_refs, o_ref = refs[0], refs[1:1 + n], refs[1 + n]
        full = jnp.concatenate([r[...] for r in b_refs], axis=1)
        av = _mx(a_ref[...])
        parts = [lax.dot_general(av, _mx(full[:, s * ns:(s + 1) * ns]), TN, preferred_element_type=f32) for s in range(NS)]

        @pl.when(pl.program_id(0) == 0)
        def _():
            for s in range(NS):
                o_ref[s] = parts[s]

        @pl.when(pl.program_id(0) > 0)
        def _():
            for s in range(NS):
                o_ref[s] += parts[s]

    return pl.pallas_call(
        body, grid=(T // tk,),
        in_specs=[pl.BlockSpec((tk, Ka), lambda t: (t, 0))] + [pl.BlockSpec((tk, p.shape[1]), lambda t: (t, 0)) for p in pieces],
        out_specs=pl.BlockSpec((NS, Ka, ns), lambda t: (0, 0, 0)),
        out_shape=SDS((NS, Ka, ns), f32),
        name=name, compiler_params=_cp("arbitrary"))(a, *pieces)


def _dx_norm_bwd(dy, w, x, gain, dxin, tm, name, comm=None):
    T, D = x.shape
    NS, _, ns = w.shape
    half = NS // 2
    pieces = list(dy) if isinstance(dy, (list, tuple)) else None
    n_dy = len(pieces) if pieces else 1

    def body(*refs):
        dy_refs = refs[:n_dy]
        w_ref, x_ref, g_ref, dxin_ref, dx_ref, dg_ref = refs[n_dy:]

        @pl.when(pl.program_id(0) == 0)
        def _():
            dg_ref[...] = jnp.zeros_like(dg_ref)
        if pieces:
            full = jnp.concatenate([r[...] for r in dy_refs], axis=1)
        dh = None
        for s in range(NS):
            if pieces:
                dv = full[:, s * ns:(s + 1) * ns]
            else:
                dv = dy_refs[0][s // half, :, (s % half) * ns:(s % half + 1) * ns]
            part = lax.dot_general(_mx(dv), w_ref[s], NT, preferred_element_type=f32)
            dh = part if dh is None else dh + part
        xv = x_ref[...]
        r = _rms(xv)
        xh = xv * r
        dg_ref[...] += jnp.sum(dh * xh, axis=0, keepdims=True)
        dxh = dh * g_ref[...]
        dx_ref[...] = dxin_ref[...] + r * (dxh - xh * jnp.mean(dxh * xh, axis=-1, keepdims=True))

    if pieces:
        dy_specs = [pl.BlockSpec((tm, p.shape[1]), lambda i: (i, 0)) for p in pieces]
    else:
        dy_specs = [pl.BlockSpec((2, tm, half * ns), lambda i: (0, i, 0))]
    return _hosted_call(
        body, T // tm,
        dy_specs + [pl.BlockSpec((NS, D, ns), lambda i: (0, 0, 0), pipeline_mode=RESIDENT),
                    pl.BlockSpec((tm, D), lambda i: (i, 0)),
                    pl.BlockSpec((1, D), lambda i: (0, 0)),
                    pl.BlockSpec((tm, D), lambda i: (i, 0))],
        [pl.BlockSpec((tm, D), lambda i: (i, 0)), pl.BlockSpec((1, D), lambda i: (0, 0))],
        [SDS((T, D), f32), SDS((1, D), f32)], [], (*(pieces or [dy]), w, x, gain, dxin), name, comm)


def _dycat(dx, w, ys, gain, tm, comm=None):
    T, D = dx.shape

    def body(dx_ref, w_ref, ya, yb, yc, yd, g_ref, da, db, dc, dd, dg_ref):
        @pl.when(pl.program_id(0) == 0)
        def _():
            dg_ref[...] = jnp.zeros_like(dg_ref)
        dyn = lax.dot_general(_mx(dx_ref[...]), w_ref[...], NT, preferred_element_type=f32)
        for m, (r, o) in enumerate(((ya, da), (yb, db), (yc, dc), (yd, dd))):
            cols = slice(m * GW, (m + 1) * GW)
            y = r[...].astype(f32)
            rs = _rms(y)
            yh = y * rs
            d = dyn[:, cols]
            dg_ref[:, cols] += jnp.sum(d * yh, axis=0, keepdims=True)
            dyh = d * g_ref[:, cols]
            o[...] = (rs * (dyh - yh * jnp.mean(dyh * yh, axis=-1, keepdims=True))).astype(bf16)

    yspec = pl.BlockSpec((tm, GW), lambda i: (i, 0))
    return _hosted_call(
        body, T // tm,
        [pl.BlockSpec((tm, D), lambda i: (i, 0)),
         pl.BlockSpec((D, D), lambda i: (0, 0), pipeline_mode=RESIDENT),
         yspec, yspec, yspec, yspec,
         pl.BlockSpec((1, D), lambda i: (0, 0))],
        [yspec, yspec, yspec, yspec, pl.BlockSpec((1, D), lambda i: (0, 0))],
        [SDS((T, GW), bf16)] * 4 + [SDS((1, D), f32)], [], (dx, w, *ys, gain), "dycat", comm)


def _sgu_consts():
    r, c = _iota((GW, GW), 0), _iota((GW, GW), 1)
    seg = (r // HD == c // HD).astype(f32)
    tr, ts = _iota((BLK, BLK), 0), _iota((BLK, BLK), 1)
    causal = ts <= tr
    lane_head = _iota((BLK, GW), 1) // HD
    return seg, causal, lane_head


def _split3_dot(a, ones):
    hi = a.astype(bf16)
    r1 = a - hi.astype(f32)
    mid = r1.astype(bf16)
    lo = (r1 - mid.astype(f32)).astype(bf16)
    dot = functools.partial(jnp.dot, preferred_element_type=f32)
    return dot(hi, ones) + dot(mid, ones) + dot(lo, ones)


def _sgu_chunks(aus, avs, w, bexp, consts):
    seg, causal, lane_head = consts
    segb = seg.astype(bf16)
    nh = GW // HD
    vs = [jax.nn.gelu(av) for av in avs]
    mus = [_split3_dot(v, segb) * (1.0 / HD) for v in vs]
    vcs = [v - mu for v, mu in zip(vs, mus)]
    vars_ = [_split3_dot(vc * vc, segb) * (1.0 / HD) for vc in vcs]
    vns = [_mx(vc * lax.rsqrt(var + EPS)) for vc, var in zip(vcs, vars_)]
    whs = [_mx(jnp.where(causal, w[h], 0.0)) for h in range(nh)]
    mixes = [[jnp.dot(whs[h], vn, preferred_element_type=f32) for h in range(nh)] for vn in vns]
    out = []
    for au, ms in zip(aus, mixes):
        mix = bexp
        for h in range(nh):
            mix = mix + jnp.where(lane_head == h, ms[h], 0.0)
        out.append(jax.nn.gelu(au) * mix)
    return out


def _sgu_group(S):
    nc = S // BLK
    return 4 if nc % 4 == 0 else (2 if nc % 2 == 0 else 1)


def _sgu_fwd(proj, w, bexp, B, S):
    G = _sgu_group(S)

    def body(au_ref, av_ref, w_ref, b_ref, y_ref):
        consts = _sgu_consts()
        wv, bv = w_ref[...], b_ref[...]

        def group(n, c):
            rows = [pl.ds(pl.multiple_of((n * G + j) * BLK, BLK), BLK) for j in range(G)]
            ys = _sgu_chunks([au_ref[r, :].astype(f32) for r in rows], [av_ref[r, :].astype(f32) for r in rows],
                             wv, bv, consts)
            for r, y in zip(rows, ys):
                y_ref[r, :] = y.astype(bf16)
            return c
        lax.fori_loop(0, S // BLK // G, group, 0)

    return pl.pallas_call(
        body, grid=(B,),
        in_specs=[pl.BlockSpec((S, GW), lambda b: (b, 0)),
                  pl.BlockSpec((S, GW), lambda b: (b, 1)),
                  pl.BlockSpec((GW // HD, BLK, BLK), lambda b: (0, 0, 0)),
                  pl.BlockSpec((BLK, GW), lambda b: (0, 0))],
        out_specs=pl.BlockSpec((S, GW), lambda b: (b, 0)),
        out_shape=SDS((B * S, GW), bf16),
        name="sgu_fwd", compiler_params=_cp("parallel"))(proj, proj, w, bexp)


def _sgu_bwd(proj, w, bexp, dy, B, S, comm=None):
    def body(au_ref, av_ref, w_ref, b_ref, dy_ref, dp_ref, dw_ref, db_ref):
        @pl.when(pl.program_id(0) == 0)
        def _():
            dw_ref[...] = jnp.zeros_like(dw_ref)
            db_ref[...] = jnp.zeros_like(db_ref)
        consts = _sgu_consts()
        wv, bv = w_ref[...], b_ref[...]
        fn = lambda aus, avs, ww, bb: _sgu_chunks(aus, avs, ww, bb, consts)
        G = _sgu_group(S)

        def group(n, carry):
            dw_acc, db_acc = carry
            rows = [pl.ds(pl.multiple_of((n * G + j) * BLK, BLK), BLK) for j in range(G)]
            _, vjp = jax.vjp(fn, [au_ref[r, :].astype(f32) for r in rows], [av_ref[r, :].astype(f32) for r in rows], wv, bv)
            daus, davs, dwc, dbc = vjp([dy_ref[r, :].astype(f32) for r in rows])
            for r, dau, dav in zip(rows, daus, davs):
                dp_ref[r, 0:GW] = dau.astype(bf16)
                dp_ref[r, GW:2 * GW] = dav.astype(bf16)
            return dw_acc + dwc, db_acc + dbc
        dw_acc, db_acc = lax.fori_loop(0, S // BLK // G, group, (jnp.zeros(wv.shape, f32), jnp.zeros(bv.shape, f32)))
        dw_ref[...] += dw_acc
        db_ref[...] += jnp.dot(db_acc, consts[0], precision=HIGHEST, preferred_element_type=f32)

    return _hosted_call(
        body, B,
        [pl.BlockSpec((S, GW), lambda b: (b, 0)),
         pl.BlockSpec((S, GW), lambda b: (b, 1)),
         pl.BlockSpec((GW // HD, BLK, BLK), lambda b: (0, 0, 0)),
         pl.BlockSpec((BLK, GW), lambda b: (0, 0)),
         pl.BlockSpec((S, GW), lambda b: (b, 0))],
        [pl.BlockSpec((S, 2 * GW), lambda b: (b, 0)),
         pl.BlockSpec((GW // HD, BLK, BLK), lambda b: (0, 0, 0)),
         pl.BlockSpec((BLK, GW), lambda b: (0, 0))],
        [SDS((B * S, 2 * GW), bf16), SDS((GW // HD, BLK, BLK), f32), SDS((BLK, GW), f32)], [],
        (proj, proj, w, bexp, dy), "sgu_bwd", comm)


def _pool_parts(p):
    n = p.shape[0]
    r = _iota(p.shape, 0)
    lg = _iota(p.shape, 1) // HD

    def sh(v, k):
        return jnp.where(r >= k, pltpu.roll(v, k, 0), 0.0)
    s2 = p + sh(p, 1)
    s4 = s2 + sh(s2, 2)
    s8 = s4 + sh(s4, 4)
    s16 = s8 + sh(s8, 8)
    ws = jnp.where(lg == 0, s2, jnp.where(lg == 1, s4, jnp.where(lg == 2, s8, s16)))
    wlen = jnp.where(lg == 0, 2, jnp.where(lg == 1, 4, jnp.where(lg == 2, 8, 16)))
    cnt = jnp.minimum(r + 1, wlen).astype(f32)
    del n
    return ws / cnt - p, cnt, lg


def _pool_fwd(proj, wbd, scale, B, S):
    def body(p_ref, w_ref, s_ref, y_ref):
        y, _, _ = _pool_parts(p_ref[...].astype(f32))
        y_ref[...] = (jnp.dot(_mx(y), _mx(w_ref[...]), preferred_element_type=f32) * s_ref[...]).astype(bf16)

    return pl.pallas_call(
        body, grid=(B,),
        in_specs=[pl.BlockSpec((S, GW), lambda b: (b, 2)),
                  pl.BlockSpec((GW, GW), lambda b: (0, 0)),
                  pl.BlockSpec((1, GW), lambda b: (0, 0))],
        out_specs=pl.BlockSpec((S, GW), lambda b: (b, 0)),
        out_shape=SDS((B * S, GW), bf16),
        name="pool_fwd", compiler_params=_cp("parallel"))(proj, wbd, scale)


def _pool_bwd(proj, wbd, scale, dy, B, S):
    def body(p_ref, w_ref, s_ref, dy_ref, dp_ref, dw_ref, ds_ref):
        @pl.when(pl.program_id(0) == 0)
        def _():
            dw_ref[...] = jnp.zeros_like(dw_ref)
            ds_ref[...] = jnp.zeros_like(ds_ref)
        y, cnt, lg = _pool_parts(p_ref[...].astype(f32))
        wv = _mx(w_ref[...])
        z = jnp.dot(_mx(y), wv, preferred_element_type=f32)
        dout = dy_ref[...].astype(f32)
        ds_ref[...] += jnp.sum(dout * z, axis=0, keepdims=True)
        dz = _mx(dout * s_ref[...])
        dw_ref[...] += lax.dot_general(_mx(y), dz, TN, preferred_element_type=f32)
        dyv = lax.dot_general(dz, wv, NT, preferred_element_type=f32)
        n = dyv.shape[0]
        r = _iota(dyv.shape, 0)

        def ush(v, k):
            return jnp.where(r < n - k, pltpu.roll(v, n - k, 0), 0.0)
        gq = dyv / cnt
        a2 = gq + ush(gq, 1)
        a4 = a2 + ush(a2, 2)
        a8 = a4 + ush(a4, 4)
        a16 = a8 + ush(a8, 8)
        adj = jnp.where(lg == 0, a2, jnp.where(lg == 1, a4, jnp.where(lg == 2, a8, a16)))
        dp_ref[...] = (adj - dyv).astype(bf16)

    return pl.pallas_call(
        body, grid=(B,),
        in_specs=[pl.BlockSpec((S, GW), lambda b: (b, 2)),
                  pl.BlockSpec((GW, GW), lambda b: (0, 0)),
                  pl.BlockSpec((1, GW), lambda b: (0, 0)),
                  pl.BlockSpec((S, GW), lambda b: (b, 0))],
        out_specs=[pl.BlockSpec((S, GW), lambda b: (b, 0)),
                   pl.BlockSpec((GW, GW), lambda b: (0, 0)),
                   pl.BlockSpec((1, GW), lambda b: (0, 0))],
        out_shape=[SDS((B * S, GW), bf16), SDS((GW, GW), f32), SDS((1, GW), f32)],
        name="pool_bwd", compiler_params=_cp("arbitrary"))(proj, wbd, scale, dy)


def _t5_bucket_table():
    dist = (np.arange(BLK)[:, None] + BLK) - np.arange(2 * BLK)[None, :]
    d = np.clip(dist, 0, BLK - 1)
    max_exact = N_BUCKETS // 2
    df = np.maximum(d, 1).astype(np.float32)
    large = max_exact + (np.log(df / max_exact) / np.float32(np.log(MAX_DISTANCE / max_exact))
                         * (N_BUCKETS - max_exact)).astype(np.int32)
    large = np.minimum(large, N_BUCKETS - 1)
    return np.where(d < max_exact, d, large).astype(np.int32)


def _swa_block(qb0, qb1, k2, v2, sinks, biases, n):
    heads = [(p, g) for p in range(2) for g in range(2)]
    ri, ci = _iota((BLK, BLK), 0), _iota((BLK, BLK), 1)
    qi, ki = _iota((BLK, 2 * BLK), 0), _iota((BLK, 2 * BLK), 1)
    dist = qi + BLK - ki
    mask = (dist >= 0) & (dist < BLK) & ((ki >= BLK) | (n > 0))
    qbs, kb, vb = (_mx(qb0), _mx(qb1)), _mx(k2), _mx(v2)
    qs, vs = [], []
    for p, g in heads:
        selq = ((ri - g * HD == ci - p * HD) & (ri >= g * HD) & (ri < (g + 1) * HD)).astype(_MXU)
        selv = ((ci - g * HD == ri - p * HD) & (ci >= g * HD) & (ci < (g + 1) * HD)).astype(_MXU)
        qs.append(_mx(jnp.dot(qbs[p], selq, preferred_element_type=f32)))
        vs.append(_mx(jnp.dot(vb, selv, preferred_element_type=f32)))
    zs = [lax.dot_general(q, kb, NT, preferred_element_type=f32) * (HD ** -0.5) for q in qs]
    prs = []
    for h in range(4):
        z = jnp.where(mask, zs[h] + biases[h], -1e30)
        s = jnp.mean(sinks[h], axis=-1, keepdims=True)
        m = jnp.maximum(jnp.max(z, axis=-1, keepdims=True), s)
        e = jnp.exp(z - m)
        prs.append(_mx(e / (jnp.sum(e, axis=-1, keepdims=True) + jnp.exp(s - m))))
    outs = [jnp.dot(prs[h], vs[h], preferred_element_type=f32) for h in range(4)]
    return outs[0] + outs[1], outs[2] + outs[3]


def _swa_fwd(proj, sinks, bias, B, S, comm=None):
    def body(q_ref, kv_ref, s_ref, b_ref, y_ref):
        def block(n, c):
            rows = pl.ds(pl.multiple_of(n * BLK, BLK), BLK)
            prev = pl.ds(pl.multiple_of(jnp.maximum(n - 1, 0) * BLK, BLK), BLK)
            k2 = jnp.concatenate([kv_ref[prev, 0:BLK], kv_ref[rows, 0:BLK]], axis=0).astype(f32)
            v2 = jnp.concatenate([kv_ref[prev, BLK:2 * BLK], kv_ref[rows, BLK:2 * BLK]], axis=0).astype(f32)
            o0, o1 = _swa_block(q_ref[rows, 0:BLK].astype(f32), q_ref[rows, BLK:2 * BLK].astype(f32), k2, v2,
                                [s_ref[h] for h in range(4)], [b_ref[h] for h in range(4)], n)
            y_ref[rows, 0:BLK] = o0.astype(bf16)
            y_ref[rows, BLK:2 * BLK] = o1.astype(bf16)
            return c
        lax.fori_loop(0, S // BLK, block, 0)

    return _hosted_call(
        body, B,
        [pl.BlockSpec((S, GW), lambda b: (b, 3)),
         pl.BlockSpec((S, GW), lambda b: (b, 4)),
         pl.BlockSpec((4, 1, BLK), lambda b: (0, 0, 0)),
         pl.BlockSpec((4, BLK, 2 * BLK), lambda b: (0, 0, 0))],
        [pl.BlockSpec((S, GW), lambda b: (b, 0))], [SDS((B * S, GW), bf16)], [],
        (proj, proj, sinks, bias), "swa_fwd", comm)


def _swa_bwd(proj, sinks, bias, dy, B, S, comm=None):
    def body(q_ref, kv_ref, s_ref, b_ref, dy_ref, dq_ref, dkv_ref, ds_ref, db_ref, acc_ref):
        @pl.when(pl.program_id(0) == 0)
        def _():
            ds_ref[...] = jnp.zeros_like(ds_ref)
            db_ref[...] = jnp.zeros_like(db_ref)
        acc_ref[...] = jnp.zeros_like(acc_ref)

        def block(n, c):
            rows = pl.ds(pl.multiple_of(n * BLK, BLK), BLK)
            prev = pl.ds(pl.multiple_of(jnp.maximum(n - 1, 0) * BLK, BLK), BLK)
            k2 = jnp.concatenate([kv_ref[prev, 0:BLK], kv_ref[rows, 0:BLK]], axis=0).astype(f32)
            v2 = jnp.concatenate([kv_ref[prev, BLK:2 * BLK], kv_ref[rows, BLK:2 * BLK]], axis=0).astype(f32)
            fn = functools.partial(_swa_block, n=n)
            _, vjp = jax.vjp(fn, q_ref[rows, 0:BLK].astype(f32), q_ref[rows, BLK:2 * BLK].astype(f32), k2, v2,
                             [s_ref[h] for h in range(4)], [b_ref[h] for h in range(4)])
            dq0, dq1, dk2, dv2, dss, dbs = vjp((dy_ref[rows, 0:BLK].astype(f32), dy_ref[rows, BLK:2 * BLK].astype(f32)))
            dq_ref[rows, 0:BLK] = dq0.astype(bf16)
            dq_ref[rows, BLK:2 * BLK] = dq1.astype(bf16)
            for h in range(4):
                ds_ref[h] += dss[h]
                db_ref[h] += dbs[h]
            acc_ref[prev, 0:BLK] += dk2[0:BLK]
            acc_ref[rows, 0:BLK] += dk2[BLK:2 * BLK]
            acc_ref[prev, BLK:2 * BLK] += dv2[0:BLK]
            acc_ref[rows, BLK:2 * BLK] += dv2[BLK:2 * BLK]
            return c
        lax.fori_loop(0, S // BLK, block, 0)
        dkv_ref[...] = acc_ref[...].astype(bf16)

    c_args, c_in, c_out, c_shapes, aliases, c_scratch = _host_specs(comm, 5, 4)
    step = lambda v: (lambda: pl.program_id(0) == v)
    return pl.pallas_call(
        _host(body, 5, 4, 1, comm, step(0), step(B - 1)), grid=(B,),
        in_specs=[pl.BlockSpec((S, GW), lambda b: (b, 3)),
                  pl.BlockSpec((S, GW), lambda b: (b, 4)),
                  pl.BlockSpec((4, 1, BLK), lambda b: (0, 0, 0)),
                  pl.BlockSpec((4, BLK, 2 * BLK), lambda b: (0, 0, 0)),
                  pl.BlockSpec((S, GW), lambda b: (b, 0))] + c_in,
        out_specs=[pl.BlockSpec((S, GW), lambda b: (b, 0)),
                   pl.BlockSpec((S, GW), lambda b: (b, 0)),
                   pl.BlockSpec((4, 1, BLK), lambda b: (0, 0, 0)),
                   pl.BlockSpec((4, BLK, 2 * BLK), lambda b: (0, 0, 0))] + c_out,
        out_shape=[SDS((B * S, GW), bf16), SDS((B * S, GW), bf16), SDS((4, 1, BLK), f32),
                   SDS((4, BLK, 2 * BLK), f32)] + c_shapes,
        input_output_aliases=aliases, scratch_shapes=[pltpu.VMEM((S, GW), f32)] + c_scratch,
        name="swa_bwd" if comm is None else "swa_bwd_exchange",
        compiler_params=_cp("arbitrary"))(proj, proj, sinks, bias, dy, *c_args)


def _log1m_parts(z):
    t = jnp.exp(-jnp.abs(z))
    return jnp.minimum(-z, 0.0) - jnp.log(1.0 + t), t


def _log1m(z):
    return _log1m_parts(z)[0]


def _sigmoid_from(z, t):
    return jnp.where(z >= 0.0, 1.0, t) / (1.0 + t)


def _sb_consts(tri):
    r2, c2 = _iota((2 * BLK, 2 * BLK), 0), _iota((2 * BLK, 2 * BLK), 1)
    tri2 = (tri(r2, c2) & (r2 // BLK == c2 // BLK)).astype(bf16)
    ri, ci = _iota((BLK, 2 * BLK), 0), _iota((BLK, 2 * BLK), 1)
    strict2 = (ci % BLK) < ri
    head0 = _iota((BLK, BLK), 1) < HD
    return tri2, strict2, head0


def _sb_stack_kv(k_ref, v_ref, kst_ref, vst_ref, head0, nb):
    def one(kb, c):
        krows = pl.ds(pl.multiple_of(kb * BLK, BLK), BLK)
        for p in range(2):
            for src, dst in ((k_ref, kst_ref), (v_ref, vst_ref)):
                t = src[krows, p * BLK:(p + 1) * BLK]
                dst[p, kb] = _mx(jnp.concatenate([jnp.where(head0, t, 0.0), jnp.where(head0, 0.0, t)], axis=0))
        return c
    lax.fori_loop(0, nb, one, 0)


def _sb_load_kv(kst_ref, vst_ref, kb):
    return [kst_ref[p, kb] for p in range(2)], [vst_ref[p, kb] for p in range(2)]


def _two_halves(a, b):
    return jnp.concatenate([jnp.broadcast_to(a, (BLK, BLK)), jnp.broadcast_to(b, (BLK, BLK))], axis=1)


def _half_sums(t):
    return jnp.sum(t[:, :BLK], axis=-1, keepdims=True), jnp.sum(t[:, BLK:], axis=-1, keepdims=True)


def _sb_fwd(proj, B, S, comm=None):
    def body(q_ref, k_ref, v_ref, y_ref, lt_ref, kst_ref, vst_ref):
        ci = _iota((BLK, BLK), 1)
        above2, strict2, head0 = _sb_consts(lambda r, c: r > c)
        _sb_stack_kv(k_ref, v_ref, kst_ref, vst_ref, head0, S // BLK)

        def step(qs, kbs, diags, carry):
            U = range(len(kbs))
            ok = [None if diags[u] else kbs[u] >= 0 for u in U]
            kv = [_sb_load_kv(kst_ref, vst_ref, jnp.maximum(kb, 0)) for kb in kbs]
            zs = [[lax.dot_general(qs[p], kks[p], NT, preferred_element_type=f32) for p in range(2)] for kks, _ in kv]
            Ls = [[jnp.where(strict2, _log1m(z), 0.0) if diags[u] else _log1m(z) for z in zs[u]] for u in U]
            tails = [[_split_dot(L, above2) for L in Lu] for Lu in Ls]
            carry = list(carry)
            for u in U:
                for p in range(2):
                    R0, R1, acc = carry[3 * p:3 * p + 3]
                    w = jnp.exp(zs[u][p] + Ls[u][p] + tails[u][p] + _two_halves(R0, R1))
                    s0, s1 = _half_sums(Ls[u][p])
                    if diags[u]:
                        w = jnp.where(strict2, w, 0.0)
                    else:
                        w, s0, s1 = (jnp.where(ok[u], t, 0.0) for t in (w, s0, s1))
                    acc = acc + jnp.dot(_mx(w), kv[u][1][p], preferred_element_type=f32)
                    carry[3 * p:3 * p + 3] = [R0 + s0, R1 + s1, acc]
            return tuple(carry)

        def qblock(n, c):
            qrows = pl.ds(pl.multiple_of(n * BLK, BLK), BLK)
            qs = [_mx(q_ref[qrows, p * BLK:(p + 1) * BLK] * (HD ** -0.5)) for p in range(2)]
            z1, z2 = jnp.zeros((BLK, 1), f32), jnp.zeros((BLK, BLK), f32)
            near = [n - 1 - u for u in range(SB_UNROLL)]
            carry = step(qs, [n] + near, [True] + [False] * SB_UNROLL, (z1, z1, z2, z1, z1, z2))
            far = jnp.maximum(n - SB_UNROLL, 0)
            trips = (far + SB_UNROLL - 1) // SB_UNROLL

            def live(st):
                worst = jnp.maximum(jnp.maximum(st[1], st[2]), jnp.maximum(st[4], st[5]))
                return (st[0] < trips) & (jnp.max(worst) > SB_CUT)

            def trip(st):
                i = st[0]
                kbs = [far - 1 - SB_UNROLL * i - u for u in range(SB_UNROLL)]
                return (i + 1,) + step(qs, kbs, [False] * SB_UNROLL, st[1:])
            done, *res = lax.while_loop(live, trip, (jnp.int32(0),) + carry)
            lt = jnp.where(ci == SB_HEADS, done.astype(f32), 0.0)
            for p in range(2):
                y_ref[qrows, p * BLK:(p + 1) * BLK] = res[3 * p + 2].astype(bf16)
                lt = lt + jnp.where(ci == 2 * p, res[3 * p], 0.0) + jnp.where(ci == 2 * p + 1, res[3 * p + 1], 0.0)
            lt_ref[qrows, :] = lt
            return c
        lax.fori_loop(0, S // BLK, qblock, 0)

    spec = lambda j: pl.BlockSpec((S, GW), lambda b: (b, j))
    c_args, c_in, c_out, c_shapes, aliases, c_scratch = _host_specs(comm, 3, 2)
    step = lambda v: (lambda: pl.program_id(0) == v)
    stacked = pltpu.VMEM((2, S // BLK, 2 * BLK, BLK), _MXU)
    return pl.pallas_call(
        _host(body, 3, 2, 2, comm, step(0), step(B - 1)), grid=(B,),
        in_specs=[spec(5), spec(6), spec(7)] + c_in,
        out_specs=[pl.BlockSpec((S, GW), lambda b: (b, 0)), pl.BlockSpec((S, BLK), lambda b: (b, 0))] + c_out,
        out_shape=[SDS((B * S, GW), bf16), SDS((B * S, BLK), f32)] + c_shapes,
        input_output_aliases=aliases, scratch_shapes=[stacked, stacked] + c_scratch,
        name="sb_fwd" if comm is None else "sb_fwd_gather",
        compiler_params=_cp("arbitrary"))(proj, proj, proj, *c_args)


def _sb_bwd(proj, ltot, dy, B, S, comm=None):
    def body(q_ref, k_ref, v_ref, lt_ref, dy_ref, dq_ref, dk_ref, dv_ref, dka_ref, dva_ref, kst_ref, vst_ref):
        ci = _iota((BLK, BLK), 1)
        upto2, strict2, head0 = _sb_consts(lambda r, c: r <= c)
        below2, _, _ = _sb_consts(lambda r, c: r < c)
        dka_ref[...] = jnp.zeros_like(dka_ref)
        dva_ref[...] = jnp.zeros_like(dva_ref)
        _sb_stack_kv(k_ref, v_ref, kst_ref, vst_ref, head0, S // BLK)

        def step(qs, dos, lts, kbs, ok, diags, top, carry):
            U = range(len(kbs))
            kbs = [jnp.clip(kb, 0, top) for kb in kbs]
            kv = [_sb_load_kv(kst_ref, vst_ref, kb) for kb in kbs]
            zs = [[lax.dot_general(qs[p], kv[u][0][p], NT, preferred_element_type=f32) for p in range(2)] for u in U]
            dws = [[lax.dot_general(dos[p], kv[u][1][p], NT, preferred_element_type=f32) for p in range(2)] for u in U]
            parts = [[_log1m_parts(z) for z in zu] for zu in zs]
            Ls = [[jnp.where(strict2, lt[0], 0.0) if diags[u] else lt[0] for lt in parts[u]] for u in U]
            pins = [[_split_dot(L, upto2) for L in Lu] for Lu in Ls]
            carry = list(carry)
            ws, das = [], []
            for u in U:
                wu, dau = [], []
                for p in range(2):
                    PL0, PL1 = carry[5 * p], carry[5 * p + 1]
                    tail = _two_halves(lts[2 * p] - PL0, lts[2 * p + 1] - PL1) - pins[u][p]
                    w = jnp.exp(zs[u][p] + Ls[u][p] + tail)
                    l0, l1 = _half_sums(Ls[u][p])
                    if diags[u]:
                        w = jnp.where(strict2, w, 0.0)
                    else:
                        w, l0, l1 = (jnp.where(ok[u], t, 0.0) for t in (w, l0, l1))
                    carry[5 * p], carry[5 * p + 1] = PL0 + l0, PL1 + l1
                    wu.append(w)
                    dau.append(w * dws[u][p])
                ws.append(wu)
                das.append(dau)
            pexs = [[_split_dot(da, below2) for da in dau] for dau in das]
            dzs = []
            for u in U:
                dzu = []
                for p in range(2):
                    dL = _two_halves(carry[5 * p + 2], carry[5 * p + 3]) + pexs[u][p]
                    sg = _sigmoid_from(zs[u][p], parts[u][p][1])
                    dz = das[u][p] * (1.0 - sg) - dL * sg
                    dz = jnp.where(strict2 if diags[u] else ok[u], dz, 0.0)
                    a0, a1 = _half_sums(das[u][p])
                    carry[5 * p + 2], carry[5 * p + 3] = carry[5 * p + 2] + a0, carry[5 * p + 3] + a1
                    dzu.append(_mx(dz))
                dzs.append(dzu)
            dqs = [[jnp.dot(dzs[u][p], kv[u][0][p], preferred_element_type=f32) for p in range(2)] for u in U]
            dks = [[lax.dot_general(dzs[u][p], qs[p], TN, preferred_element_type=f32) for p in range(2)] for u in U]
            dvs = [[lax.dot_general(_mx(ws[u][p]), dos[p], TN, preferred_element_type=f32) for p in range(2)] for u in U]
            for u in U:
                krows = pl.ds(pl.multiple_of(kbs[u] * BLK, BLK), BLK)
                for p in range(2):
                    lanes = slice(p * BLK, (p + 1) * BLK)
                    dka_ref[krows, lanes] += jnp.where(head0, dks[u][p][:BLK], dks[u][p][BLK:])
                    dva_ref[krows, lanes] += jnp.where(head0, dvs[u][p][:BLK], dvs[u][p][BLK:])
                    carry[5 * p + 4] = carry[5 * p + 4] + dqs[u][p]
            return tuple(carry)

        def qblock(n, c):
            qrows = pl.ds(pl.multiple_of(n * BLK, BLK), BLK)
            ltb = lt_ref[qrows, :]
            lts = [jnp.sum(jnp.where(ci == h, ltb, 0.0), axis=-1, keepdims=True) for h in range(4)]
            qs = [_mx(q_ref[qrows, p * BLK:(p + 1) * BLK] * (HD ** -0.5)) for p in range(2)]
            dos = [_mx(dy_ref[qrows, p * BLK:(p + 1) * BLK]) for p in range(2)]
            z1, z2 = jnp.zeros((BLK, 1), f32), jnp.zeros((BLK, BLK), f32)
            done = jnp.max(jnp.where(ci == SB_HEADS, ltb, 0.0)).astype(jnp.int32)
            far = jnp.maximum(n - SB_UNROLL, 0)
            first = jnp.maximum(far - SB_UNROLL * done, 0)

            def trip(i, cr):
                kbs = [first + SB_UNROLL * i + u for u in range(SB_UNROLL)]
                return step(qs, dos, lts, kbs, [kb < far for kb in kbs], [False] * SB_UNROLL, n, cr)
            carry = lax.fori_loop(0, (far - first + SB_UNROLL - 1) // SB_UNROLL, trip, (z1, z1, z1, z1, z2) * 2)
            near = [n - SB_UNROLL + u for u in range(SB_UNROLL)]
            res = step(qs, dos, lts, near + [n], [kb >= 0 for kb in near] + [None], [False] * SB_UNROLL + [True], n, carry)
            for p in range(2):
                dq_ref[qrows, p * BLK:(p + 1) * BLK] = (res[5 * p + 4] * (HD ** -0.5)).astype(bf16)
            return c
        lax.fori_loop(0, S // BLK, qblock, 0)
        dk_ref[...] = dka_ref[...].astype(bf16)
        dv_ref[...] = dva_ref[...].astype(bf16)

    spec = lambda j: pl.BlockSpec((S, GW), lambda b: (b, j))
    o = pl.BlockSpec((S, GW), lambda b: (b, 0))
    c_args, c_in, c_out, c_shapes, aliases, c_scratch = _host_specs(comm, 5, 3)
    step = lambda v: (lambda: pl.program_id(0) == v)
    stacked = pltpu.VMEM((2, S // BLK, 2 * BLK, BLK), _MXU)
    return pl.pallas_call(
        _host(body, 5, 3, 4, comm, step(0), step(B - 1)), grid=(B,),
        in_specs=[spec(5), spec(6), spec(7), pl.BlockSpec((S, BLK), lambda b: (b, 0)), o] + c_in,
        out_specs=[o, o, o] + c_out,
        out_shape=[SDS((B * S, GW), bf16)] * 3 + c_shapes,
        input_output_aliases=aliases,
        scratch_shapes=[pltpu.VMEM((S, GW), f32), pltpu.VMEM((S, GW), f32), stacked, stacked] + c_scratch,
        name="sb_bwd" if comm is None else "sb_bwd_exchange",
        compiler_params=_cp("arbitrary"))(proj, proj, proj, ltot, dy, *c_args)


def _bias_expand(rel_bias_t, bucket):
    n = bucket.shape[1]

    def body(r_ref, b_ref, o_ref):
        onehot = (_iota((N_BUCKETS, n), 0) == b_ref[...]).astype(f32)
        o_ref[...] = jnp.dot(r_ref[...], onehot, precision=HIGHEST, preferred_element_type=f32)
    return pl.pallas_call(body, out_shape=SDS((rel_bias_t.shape[0], n), f32), name="bias_expand",
                          compiler_params=_cp())(rel_bias_t, bucket)


def _bias_reduce(dbias, bucket):
    n = bucket.shape[1]

    def body(*refs):
        b_ref, g_ref = refs[-2], refs[-1]
        d = refs[0][...]
        for r in refs[1:-2]:
            d = d + r[...]
        onehot = (_iota((N_BUCKETS, n), 0) == b_ref[...]).astype(f32)
        g_ref[...] = lax.dot_general(d, onehot, NT, precision=HIGHEST, preferred_element_type=f32)
    return pl.pallas_call(body, out_shape=SDS((dbias[0].shape[0], N_BUCKETS), f32), name="bias_reduce",
                          compiler_params=_cp())(*dbias, bucket)


def _adamw(w, g, m, v, tr, name, comm=None):
    R, C = w.shape

    def body(w_ref, g_ref, m_ref, v_ref, d_ref, m2_ref, v2_ref):
        gv = g_ref[...]
        m2 = ADAM_B1 * m_ref[...] + (1.0 - ADAM_B1) * gv
        v2 = ADAM_B2 * v_ref[...] + (1.0 - ADAM_B2) * (gv * gv)
        m_hat = m2 / (1.0 - ADAM_B1 ** ADAM_STEP)
        v_hat = v2 / (1.0 - ADAM_B2 ** ADAM_STEP)
        d_ref[...] = -ADAM_LR * (m_hat / (jnp.sqrt(v_hat) + ADAM_EPS) + ADAM_WD * w_ref[...])
        m2_ref[...] = m2
        v2_ref[...] = v2

    spec = pl.BlockSpec((tr, C), lambda i: (i, 0))
    return _hosted_call(body, R // tr, [spec] * 4, [spec] * 3, [SDS((R, C), f32)] * 3, [], (w, g, m, v), name, comm)


ANY = pl.BlockSpec(memory_space=pl.ANY)


def _place():
    x, y, c = lax.axis_index("x"), lax.axis_index("y"), lax.axis_index("c")
    chips = [(1 - x, y), (x, 1 - y), (1 - x, 1 - y)]
    return x, y, c, chips


def _cast_slots(w, kidx):
    L, a, b = w.shape
    ta = a // 2

    def body(k_ref, *refs):
        for l in range(L):
            refs[L + l][0] = refs[l][0].astype(bf16)

    return pl.pallas_call(
        body,
        grid_spec=pltpu.PrefetchScalarGridSpec(
            num_scalar_prefetch=1, grid=(a // ta,),
            in_specs=[pl.BlockSpec((1, ta, b), functools.partial(lambda i, k_ref, l: (l, i, 0), l=l)) for l in range(L)],
            out_specs=[pl.BlockSpec((1, ta, b), lambda i, k_ref: (k_ref[0], i, 0)) for _ in range(L)]),
        out_shape=[SDS((N_CHIPS, a, b), bf16)] * L,
        name="cast_slots", compiler_params=_cp("parallel"))(kidx, *([w] * L))


class _GatherComm:
    def __init__(self, bufs):
        self.inputs = list(bufs)
        self.out_shape = [SDS(b.shape, b.dtype) for b in bufs]
        self.aliased = True
        self.scratch = [pltpu.SemaphoreType.DMA((3 * len(bufs),))] * 4

    def _copies(self, i_refs, o_refs, sems):
        send1, recv1, send2, recv2 = sems
        x, y, c, chips = _place()
        k = 2 * x + y
        first, got1, second, got2 = [], [], [], []
        for i, buf in enumerate(self.inputs):
            h = buf.shape[1] // 2
            mine, theirs = pl.ds(c * h, h), pl.ds((1 - c) * h, h)
            for j, (cx, cy) in enumerate(chips):
                s = 3 * i + j
                first.append(pltpu.make_async_remote_copy(
                    src_ref=i_refs[i].at[k, mine], dst_ref=o_refs[i].at[k, mine], send_sem=send1.at[s],
                    recv_sem=recv1.at[s], device_id=(cx, cy, c), device_id_type=MESH))
                a = o_refs[i].at[2 * cx + cy, mine]
                got1.append(pltpu.make_async_remote_copy(
                    src_ref=a, dst_ref=a, send_sem=send1.at[s], recv_sem=recv1.at[s],
                    device_id=(cx, cy, c), device_id_type=MESH))
                second.append(pltpu.make_async_remote_copy(
                    src_ref=a, dst_ref=a, send_sem=send2.at[s], recv_sem=recv2.at[s],
                    device_id=(x, y, 1 - c), device_id_type=MESH))
                b = o_refs[i].at[2 * cx + cy, theirs]
                got2.append(pltpu.make_async_remote_copy(
                    src_ref=b, dst_ref=b, send_sem=send2.at[s], recv_sem=recv2.at[s],
                    device_id=(x, y, 1 - c), device_id_type=MESH))
        return first, got1, second, got2

    def start(self, i_refs, o_refs, sems):
        for cp in self._copies(i_refs, o_refs, sems)[0]:
            cp.start()

    def finish(self, i_refs, o_refs, sems):
        first, got1, second, got2 = self._copies(i_refs, o_refs, sems)
        for g, cp in zip(got1, second):
            g.wait_recv()
            cp.start()
        for g in got2:
            g.wait_recv()
        for cp in first + second:
            cp.wait_send()


class _PairExchangeComm:
    def __init__(self, gs):
        self.inputs = list(gs)
        self.out_shape = [SDS((g.shape[0], g.shape[1] // 2, g.shape[2]), g.dtype) for g in gs]
        self.aliased = False
        self.scratch = [pltpu.SemaphoreType.DMA((len(gs),))] * 2

    def _copies(self, i_refs, o_refs, sems):
        send, recv = sems
        x, y, c, _ = _place()
        cps = []
        for i, g in enumerate(self.inputs):
            h = g.shape[1] // 2
            cps.append(pltpu.make_async_remote_copy(
                src_ref=i_refs[i].at[:, pl.ds((1 - c) * h, h)], dst_ref=o_refs[i], send_sem=send.at[i], recv_sem=recv.at[i],
                device_id=(x, y, 1 - c), device_id_type=MESH))
        return cps

    def start(self, i_refs, o_refs, sems):
        for cp in self._copies(i_refs, o_refs, sems):
            cp.start()

    def finish(self, i_refs, o_refs, sems):
        for cp in self._copies(i_refs, o_refs, sems):
            cp.wait()


class _ChipExchangeComm:
    def __init__(self, qs):
        self.inputs = list(qs)
        self.out_shape = [SDS(q.shape, q.dtype) for q in qs]
        self.aliased = False
        self.scratch = [pltpu.SemaphoreType.DMA((3 * len(qs),))] * 2

    def _copies(self, i_refs, o_refs, sems):
        send, recv = sems
        x, y, c, chips = _place()
        k = 2 * x + y
        cps, got = [], []
        for i in range(len(self.inputs)):
            for j, (cx, cy) in enumerate(chips):
                s = 3 * i + j
                cps.append(pltpu.make_async_remote_copy(
                    src_ref=i_refs[i].at[2 * cx + cy], dst_ref=o_refs[i].at[k], send_sem=send.at[s],
                    recv_sem=recv.at[s], device_id=(cx, cy, c), device_id_type=MESH))
                a = o_refs[i].at[2 * cx + cy]
                got.append(pltpu.make_async_remote_copy(
                    src_ref=a, dst_ref=a, send_sem=send.at[s], recv_sem=recv.at[s],
                    device_id=(cx, cy, c), device_id_type=MESH))
        return cps, got

    def start(self, i_refs, o_refs, sems):
        for cp in self._copies(i_refs, o_refs, sems)[0]:
            cp.start()

    def finish(self, i_refs, o_refs, sems):
        cps, got = self._copies(i_refs, o_refs, sems)
        for g in got:
            g.wait_recv()
        for cp in cps:
            cp.wait_send()


def _comm_only(comm, name):
    n = len(comm.inputs)

    def body(*refs):
        i_refs, o_refs, sems = refs[:n], refs[n:n + len(comm.out_shape)], refs[n + len(comm.out_shape):]
        comm.start(i_refs, o_refs, sems)
        comm.finish(i_refs, o_refs, sems)

    return pl.pallas_call(
        body, out_shape=comm.out_shape, in_specs=[ANY] * n, out_specs=[ANY] * len(comm.out_shape),
        input_output_aliases={i: i for i in range(n)} if comm.aliased else {},
        scratch_shapes=comm.scratch, name=name,
        compiler_params=pltpu.CompilerParams(has_side_effects=True))(*comm.inputs)


def _host(body, n_in, n_out, n_scratch, comm, first, last):
    if comm is None:
        return body
    ci, co = len(comm.inputs), len(comm.out_shape)

    def wrapped(*refs):
        o = 0
        parts = []
        for n in (n_in, ci, n_out, co, n_scratch):
            parts.append(refs[o:o + n])
            o += n
        hin, cin, hout, cout, hs = parts
        sems = refs[o:]

        @pl.when(first())
        def _():
            comm.start(cin, cout, sems)
        body(*hin, *hout, *hs)

        @pl.when(last())
        def _():
            comm.finish(cin, cout, sems)
    return wrapped


def _host_specs(comm, n_in, n_out):
    if comm is None:
        return [], [], [], [], {}, []
    ci, co = len(comm.inputs), len(comm.out_shape)
    aliases = {n_in + i: n_out + i for i in range(ci)} if comm.aliased else {}
    return comm.inputs, [ANY] * ci, [ANY] * co, comm.out_shape, aliases, comm.scratch


def _pair_add(g, r, cidx, name):
    ns, a, b = g.shape
    h = a // 2
    th = h if h * b * 4 <= 4 * 1024 * 1024 else h // 2

    def body(c_ref, g_ref, r_ref, qb_ref):
        qb_ref[...] = (g_ref[...] + r_ref[...]).astype(bf16)

    nb = h // th
    spec = pl.BlockSpec((1, th, b), lambda s, i, c_ref: (s, i, 0))
    return pl.pallas_call(
        body,
        grid_spec=pltpu.PrefetchScalarGridSpec(
            num_scalar_prefetch=1, grid=(ns, nb),
            in_specs=[pl.BlockSpec((1, th, b), lambda s, i, c_ref: (s, c_ref[0] * nb + i, 0)), spec],
            out_specs=spec),
        out_shape=SDS((ns, h, b), bf16),
        name=name, compiler_params=_cp("parallel", "parallel"))(cidx, g, r)


def _chip_add(g, r1, r2, idx, prev, L, name):
    ns, h, b = r2.shape
    th = h if h * b * 4 <= 4 * 1024 * 1024 else h // 2
    nb = h // th

    def body(s_ref, g_ref, r1_ref, a_ref, b_ref, c_ref, *rest):
        o_ref = rest[-1]
        o_ref[0] = (g_ref[0] + r1_ref[0]) + a_ref[0].astype(f32) + b_ref[0].astype(f32) + c_ref[0].astype(f32)

    other = lambda d: pl.BlockSpec((1, th, b), lambda i, s_ref: ((s_ref[0] + d) % ns, i, 0))
    in_specs = [pl.BlockSpec((1, th, b), lambda i, s_ref: (s_ref[0], s_ref[1] * nb + i, 0)),
                pl.BlockSpec((1, th, b), lambda i, s_ref: (s_ref[0], i, 0)), other(1), other(2), other(3)]
    args = [idx, g, r1, r2, r2, r2]
    aliases = {}
    if prev is not None:
        in_specs.append(ANY)
        args.append(prev)
        aliases = {6: 0}
    return pl.pallas_call(
        body,
        grid_spec=pltpu.PrefetchScalarGridSpec(
            num_scalar_prefetch=1, grid=(nb,), in_specs=in_specs,
            out_specs=pl.BlockSpec((1, th, b), lambda i, s_ref: (s_ref[2], s_ref[1] * nb + i, 0))),
        out_shape=SDS((L, 2 * h, b), f32), input_output_aliases=aliases,
        name=name, compiler_params=_cp("arbitrary"))(*args)


def _pair_share(gs, hs):
    n = len(gs)
    L = gs[0].shape[0]

    def body(*refs):
        i_refs, o_refs = refs[:n], refs[n:2 * n]
        send, recv = refs[2 * n:]
        x, y, c, _ = _place()
        cps = []
        for i in range(n):
            for l in range(L):
                mine = pl.ds(c * hs[i], hs[i])
                cp = pltpu.make_async_remote_copy(
                    src_ref=i_refs[i].at[l, mine], dst_ref=o_refs[i].at[l, mine], send_sem=send.at[i * L + l],
                    recv_sem=recv.at[i * L + l], device_id=(x, y, 1 - c), device_id_type=MESH)
                cp.start()
                cps.append(cp)
        for i in range(n):
            for l in range(L):
                got = o_refs[i].at[l, pl.ds((1 - c) * hs[i], hs[i])]
                pltpu.make_async_remote_copy(
                    src_ref=got, dst_ref=got, send_sem=send.at[i * L + l], recv_sem=recv.at[i * L + l],
                    device_id=(x, y, 1 - c), device_id_type=MESH).wait_recv()
        for cp in cps:
            cp.wait_send()

    return pl.pallas_call(
        body, out_shape=[SDS(g.shape, g.dtype) for g in gs], in_specs=[ANY] * n, out_specs=[ANY] * n,
        input_output_aliases={i: i for i in range(n)},
        scratch_shapes=[pltpu.SemaphoreType.DMA((n * L,))] * 2,
        name="grad_pair_share", compiler_params=pltpu.CompilerParams(has_side_effects=True))(*gs)


class _SwapComm:
    def __init__(self, arrays):
        self.inputs = list(arrays)
        self.out_shape = [SDS(a.shape, a.dtype) for a in arrays]
        self.aliased = False
        self.scratch = [pltpu.SemaphoreType.DMA((len(arrays),))] * 2

    def _copies(self, i_refs, o_refs, sems):
        send, recv = sems
        x, y, c, _ = _place()
        return [pltpu.make_async_remote_copy(
            src_ref=i_refs[i], dst_ref=o_refs[i], send_sem=send.at[i], recv_sem=recv.at[i],
            device_id=(x, y, 1 - c), device_id_type=MESH) for i in range(len(self.inputs))]

    def start(self, i_refs, o_refs, sems):
        for cp in self._copies(i_refs, o_refs, sems):
            cp.start()

    def finish(self, i_refs, o_refs, sems):
        for cp in self._copies(i_refs, o_refs, sems):
            cp.wait()


class _SlotShareComm:
    def __init__(self, bufs):
        self.inputs = list(bufs)
        self.out_shape = [SDS(b.shape, b.dtype) for b in bufs]
        self.aliased = True
        self.scratch = [pltpu.SemaphoreType.DMA((3 * len(bufs),))] * 2

    def _copies(self, i_refs, o_refs, sems):
        send, recv = sems
        x, y, c, chips = _place()
        k = 2 * x + y
        cps, got = [], []
        for i in range(len(self.inputs)):
            for j, (cx, cy) in enumerate(chips):
                s = 3 * i + j
                cps.append(pltpu.make_async_remote_copy(
                    src_ref=i_refs[i].at[k], dst_ref=o_refs[i].at[k], send_sem=send.at[s], recv_sem=recv.at[s],
                    device_id=(cx, cy, c), device_id_type=MESH))
                a = o_refs[i].at[2 * cx + cy]
                got.append(pltpu.make_async_remote_copy(
                    src_ref=a, dst_ref=a, send_sem=send.at[s], recv_sem=recv.at[s],
                    device_id=(cx, cy, c), device_id_type=MESH))
        return cps, got

    def start(self, i_refs, o_refs, sems):
        for cp in self._copies(i_refs, o_refs, sems)[0]:
            cp.start()

    def finish(self, i_refs, o_refs, sems):
        cps, got = self._copies(i_refs, o_refs, sems)
        for g in got:
            g.wait_recv()
        for cp in cps:
            cp.wait_send()


def _pair_sum_slot(mine, theirs, kidx):
    R, C = mine.shape

    def body(k_ref, a_ref, b_ref, o_ref):
        o_ref[0] = a_ref[...] + b_ref[...]

    spec = pl.BlockSpec((R, C), lambda i, k_ref: (0, 0))
    return pl.pallas_call(
        body,
        grid_spec=pltpu.PrefetchScalarGridSpec(
            num_scalar_prefetch=1, grid=(1,), in_specs=[spec, spec],
            out_specs=pl.BlockSpec((1, R, C), lambda i, k_ref: (k_ref[0], 0, 0))),
        out_shape=SDS((N_CHIPS, R, C), f32), name="small_pair_sum", compiler_params=_cp("arbitrary"))(kidx, mine, theirs)


def _small_sum(g):
    n, R, C = g.shape

    def body(g_ref, o_ref):
        acc = g_ref[0]
        for j in range(1, n):
            acc = acc + g_ref[j]
        o_ref[...] = acc
    return pl.pallas_call(body, out_shape=SDS((R, C), f32), name="small_sum", compiler_params=_cp())(g)


PACK_COLS = 1024


def _rows_of(shape):
    n = int(np.prod(shape)) if len(shape) else 1
    return -(-n // (8 * PACK_COLS)) * 8


def _pack(parts):
    blocks = []
    for p in parts:
        flat = p.reshape(-1)
        r = _rows_of(p.shape)
        blocks.append(jnp.pad(flat, (0, r * PACK_COLS - flat.shape[0])).reshape(r, PACK_COLS))
    return jnp.concatenate(blocks, axis=0)


def _unpack(buf, shapes):
    out, off = [], 0
    for s in shapes:
        n = int(np.prod(s)) if len(s) else 1
        r = _rows_of(s)
        out.append(buf[off:off + r].reshape(-1)[:n].reshape(s))
        off += r
    return out


def _block_diag(w):
    g, a, _ = w.shape
    out = jnp.zeros((g * a, g * a), w.dtype)
    for i in range(g):
        out = lax.dynamic_update_slice(out, w[i], (i * a, i * a))
    return out


def kernel(x, w_in, w_out, sgu_w, sgu_b, pool_w, pool_scale, swa_sinks, rel_bias, mix_out_gain, norm_mix, norm_ffn, w_gate_up, w_down, norm_final, loss_target, m_w_in, m_w_out, m_sgu_w, m_sgu_b, m_pool_w, m_pool_scale, m_swa_sinks, m_rel_bias, m_mix_out_gain, m_norm_mix, m_norm_ffn, m_w_gate_up, m_w_down, m_norm_final, v_w_in, v_w_out, v_sgu_w, v_sgu_b, v_pool_w, v_pool_scale, v_swa_sinks, v_rel_bias, v_mix_out_gain, v_norm_mix, v_norm_ffn, v_w_gate_up, v_w_down, v_norm_final):
    B, S, D = x.shape
    T = B * S
    L = w_in.shape[0]
    tm = min(512, T)
    F = w_down.shape[1] * N_CHIPS
    xi, yi, ci = lax.axis_index("x"), lax.axis_index("y"), lax.axis_index("c")
    cidx = jnp.reshape(ci, (1,)).astype(jnp.int32)
    kidx = jnp.reshape(2 * xi + yi, (1,)).astype(jnp.int32)

    big = [w_in, w_out, w_gate_up, w_down]
    slots = [_cast_slots(w, kidx) for w in big]
    gather = lambda pi, l: _GatherComm([slots[pi][l]])
    Win, Wo, Wgu, Wd = ([None] * L for _ in range(4))
    Win[0], = _comm_only(gather(0, 0), "gather_weights")

    bucket = jnp.asarray(_t5_bucket_table().reshape(1, -1))
    bias_tab = _bias_expand(rel_bias.T, bucket).reshape(4, BLK, 2 * BLK)

    row = lambda v: v.reshape(1, -1)
    xc = x.reshape(T, D)
    tgt = loss_target.reshape(T, D)
    saved = []
    for l in range(L):
        h1, proj, wo = _norm_mm(xc, row(norm_mix[l]), Win[l], tm, gather(1, l))
        Wo[l] = wo.reshape(D, D)
        bexp = jnp.repeat(sgu_b[l].T, HD, axis=1)
        wbd = _block_diag(pool_w[l])
        sk = jnp.broadcast_to(swa_sinks[l][:, None, None], (4, 1, BLK))
        ya = _sgu_fwd(proj, sgu_w[l], bexp, B, S)
        yb = _pool_fwd(proj, wbd, row(pool_scale[l]), B, S)
        if l == 0:
            yc, wd = _swa_fwd(proj, sk, bias_tab, B, S, gather(3, 0))
            yd, lt, Wgu[0] = _sb_fwd(proj, B, S, gather(2, 0))
        else:
            yc, = _swa_fwd(proj, sk, bias_tab, B, S)
            yd, lt, wd = _sb_fwd(proj, B, S, gather(3, l))
        Wd[l] = wd.reshape(F, D)
        ys = (ya, yb, yc, yd)
        ycn, x1 = _gnorm_mm_res(ys, row(mix_out_gain[l]), Wo[l], xc, tm)
        if l + 1 < L:
            h2, gu, act, Wgu[l + 1] = _norm_mm_swiglu(x1, row(norm_ffn[l]), Wgu[l], tm, gather(2, l + 1))
            x2, Win[l + 1] = _mm_res(act, Wd[l], x1, tm, gather(0, l + 1))
        else:
            h2, gu, act = _norm_mm_swiglu(x1, row(norm_ffn[l]), Wgu[l], tm)
            x2, = _mm_res(act, Wd[l], x1, tm)
        saved.append((xc, h1, proj, bexp, wbd, sk, ys, lt, ycn, x1, h2, gu, act))
        xc = x2

    dx, g_final, loss_v = _final_loss(xc, row(norm_final), tgt, tm)

    tk = min(T, 2048)
    gW = [[None] * L for _ in range(4)]
    g_sgu_w, g_sgu_b, g_pool_w, g_pool_scale, g_sinks, g_bias = ([None] * L for _ in range(6))
    g_out_gain, g_mix, g_ffn = ([None] * L for _ in range(3))
    reduced = [None] * 4
    sums = {}

    def pair_comm(keys):
        return _PairExchangeComm([gW[pi][l] for pi, l in keys])

    def after_pair(keys, r1):
        for (pi, l), r in zip(keys, r1):
            sums[pi, l] = (r, _pair_add(gW[pi][l], r, cidx, "grad_pair_add"))

    def chip_comm(keys):
        return _ChipExchangeComm([sums[k][1] for k in keys])

    def after_chip(keys, r2):
        for (pi, l), r in zip(keys, r2):
            idx = jnp.stack([2 * xi + yi, ci, jnp.int32(l)]).astype(jnp.int32)
            reduced[pi] = _chip_add(gW[pi][l], sums.pop((pi, l))[0], r, idx, reduced[pi], L, "grad_chip_add")

    for l in reversed(range(L)):
        x0, h1, proj, bexp, wbd, sk, ys, lt, ycn, x1, h2, gu, act = saved[l]
        if l + 1 < L:
            dgu, *r2 = _dact(dx, Wd[l], gu, tm, chip_comm([(0, l + 1)]))
            after_chip([(0, l + 1)], r2)
        else:
            dgu, = _dact(dx, Wd[l], gu, tm)
        gW[3][l] = _dw(act, dx, lambda t, s: (t, 0), D, 1, F // 2, tk // 2, "dw_down").reshape(N_CHIPS, F // N_CHIPS, D)
        gW[2][l] = _dw(h2, dgu, lambda t, s: (s // 2, t, s % 2), F // 2, N_CHIPS, D, tk, "dw_gate_up")
        keys = [(2, l), (3, l)]
        dx1, g_ffn[l], *r1 = _dx_norm_bwd(dgu, Wgu[l], x1, row(norm_ffn[l]), dx, tm, "dx_ffn_exchange", pair_comm(keys))
        after_pair(keys, r1)
        gW[1][l] = _dw(ycn, dx1, lambda t, s: (t, 0), D, 1, D, tk, "dw_out").reshape(N_CHIPS, D // N_CHIPS, D)
        dya, dyb, dyc, dyd, g_out_gain[l], *r1 = _dycat(dx1, Wo[l], ys, row(mix_out_gain[l]), tm, pair_comm([(1, l)]))
        after_pair([(1, l)], r1)
        dpa, g_sgu_w[l], dbf, *r2 = _sgu_bwd(proj, sgu_w[l], bexp, dya, B, S, chip_comm([(1, l)]))
        after_chip([(1, l)], r2)
        g_sgu_b[l] = dbf[:, ::HD].T
        dpb, dwbd, dsc = _pool_bwd(proj, wbd, row(pool_scale[l]), dyb, B, S)
        npg = len(POOL_WINDOWS)
        g_pool_w[l] = jnp.stack([dwbd[i * HD:(i + 1) * HD, i * HD:(i + 1) * HD] for i in range(npg)])
        g_pool_scale[l] = dsc[0]
        dcq, dckv, dsk, g_bias[l], *r2 = _swa_bwd(proj, sk, bias_tab, dyc, B, S, chip_comm([(3, l)]))
        after_chip([(3, l)], r2)
        g_sinks[l] = dsk[:, 0, 0] * float(BLK)
        ddq, ddk, ddv, *r2 = _sb_bwd(proj, lt, dyd, B, S, chip_comm([(2, l)]))
        after_chip([(2, l)], r2)
        dproj = [dpa, dpb, dcq, dckv, ddq, ddk, ddv]
        gW[0][l] = _dw_pieces(h1, dproj, w_in.shape[2], N_CHIPS, tk, "dw_in")
        dx, g_mix[l], *r1 = _dx_norm_bwd(dproj, Win[l], x0, row(norm_mix[l]), dx1, tm, "dx_mix_exchange", pair_comm([(0, l)]))
        after_pair([(0, l)], r1)
    grad_x = dx.reshape(B, S, D)

    after_chip([(0, 0)], _comm_only(chip_comm([(0, 0)]), "grad_chip_exchange"))
    g_big = _pair_share(reduced, [g.shape[1] // 2 for g in reduced])

    g_rel_bias = _bias_reduce([g.reshape(4, -1) for g in g_bias], bucket).T
    small_g = [jnp.stack(g_sgu_w), jnp.stack(g_sgu_b), jnp.stack(g_pool_w), jnp.stack(g_pool_scale), jnp.stack(g_sinks),
               g_rel_bias, jnp.concatenate(g_out_gain), jnp.concatenate(g_mix), jnp.concatenate(g_ffn), g_final[0]]
    small_w = [sgu_w, sgu_b, pool_w, pool_scale, swa_sinks, rel_bias, mix_out_gain, norm_mix, norm_ffn, norm_final]
    small_m = [m_sgu_w, m_sgu_b, m_pool_w, m_pool_scale, m_swa_sinks, m_rel_bias, m_mix_out_gain, m_norm_mix, m_norm_ffn, m_norm_final]
    small_v = [v_sgu_w, v_sgu_b, v_pool_w, v_pool_scale, v_swa_sinks, v_rel_bias, v_mix_out_gain, v_norm_mix, v_norm_ffn, v_norm_final]
    shapes = [w.shape for w in small_w]
    mine = _pack(small_g + [loss_v[0, 0:1]])
    theirs, = _comm_only(_SwapComm([mine]), "small_pair_swap")
    share = _SlotShareComm([_pair_sum_slot(mine, theirs, kidx)])

    big_m = [m_w_in, m_w_out, m_w_gate_up, m_w_down]
    big_v = [v_w_in, v_w_out, v_w_gate_up, v_w_down]
    d_big, m_big, v_big = [None] * 4, [None] * 4, [None] * 4
    for pi in (2, 0, 1, 3):
        w, g, m, v = big[pi], g_big[pi], big_m[pi], big_v[pi]
        two = lambda a: a.reshape(-1, a.shape[-1])
        rows = two(w).shape[0]
        d2, m2, v2, *got = _adamw(two(w), two(g), two(m), two(v), rows // 8 if rows >= 2048 else rows, "adamw_big",
                                  share if pi == 2 else None)
        if pi == 2:
            shared, = got
        d_big[pi], m_big[pi], v_big[pi] = d2.reshape(w.shape), m2.reshape(w.shape), v2.reshape(w.shape)

    packed = _small_sum(shared)
    *g_small, loss = _unpack(packed, shapes + [()])
    g_small_packed = _pack(g_small)
    ds, ms, vs = _adamw(_pack(small_w), g_small_packed, _pack(small_m), _pack(small_v), g_small_packed.shape[0], "adamw_small")
    d_small, m_small, v_small = _unpack(ds, shapes), _unpack(ms, shapes), _unpack(vs, shapes)

    def order(bigs, smalls):
        return [bigs[0], bigs[1]] + list(smalls[0:9]) + [bigs[2], bigs[3], smalls[9]]

    return (loss, grad_x, *order(g_big, g_small), *order(d_big, d_small), *order(m_big, m_small), *order(v_big, v_small))
```

```python
import functools

import numpy as np
import jax
import jax.numpy as jnp
from jax import lax
from jax.experimental import pallas as pl
from jax.experimental.pallas import tpu as pltpu

f32 = jnp.float32
bf16 = jnp.bfloat16
_MXU = jnp.bfloat16

EPS = 1e-6
HD = 64
GW = 256
BLK = 128
SB_UNROLL = 2
SB_HEADS = 4
SB_CUT = -110.0
POOL_WINDOWS = (2, 4, 8, 16)
N_BUCKETS = 32
MAX_DISTANCE = 128
N_CHIPS = 4
N_DEV = 8
VMEM_LIMIT = 48 * 1024 * 1024

ADAM_LR = 0.001
ADAM_B1 = 0.9
ADAM_B2 = 0.999
ADAM_EPS = 1e-08
ADAM_WD = 0.01
ADAM_STEP = 10

SDS = jax.ShapeDtypeStruct
MESH = pl.DeviceIdType.MESH
HIGHEST = lax.Precision.HIGHEST
RESIDENT = pl.Buffered(1)
NT = (((1,), (1,)), ((), ()))
TN = (((0,), (0,)), ((), ()))


def _cp(*sem):
    return pltpu.CompilerParams(dimension_semantics=sem if sem else None, vmem_limit_bytes=VMEM_LIMIT)


def _mx(v):
    return v.astype(_MXU)


def _iota(shape, dim):
    return lax.broadcasted_iota(jnp.int32, shape, dim)


def _split_dot(a, tri):
    hi = a.astype(bf16)
    lo = (a - hi.astype(f32)).astype(bf16)
    return jnp.dot(hi, tri, preferred_element_type=f32) + jnp.dot(lo, tri, preferred_element_type=f32)


def _rms(xv):
    return lax.rsqrt(jnp.mean(xv * xv, axis=-1, keepdims=True) + EPS)


def _hosted_call(body, steps, in_specs, out_specs, out_shape, scratch, args, name, comm):
    n_in, n_out = len(in_specs), len(out_specs)
    c_args, c_in, c_out, c_shapes, aliases, c_scratch = _host_specs(comm, n_in, n_out)
    step = lambda v: (lambda: pl.program_id(0) == v)
    return pl.pallas_call(
        _host(body, n_in, n_out, len(scratch), comm, step(0), step(steps - 1)), grid=(steps,),
        in_specs=list(in_specs) + c_in, out_specs=list(out_specs) + c_out, out_shape=list(out_shape) + c_shapes,
        input_output_aliases=aliases, scratch_shapes=list(scratch) + c_scratch,
        name=name if comm is None else name + "_comm", compiler_params=_cp("arbitrary"))(*args, *c_args)


def _norm_mm(x, gain, w, tm, comm=None):
    T, D = x.shape
    NS, _, ns = w.shape

    def body(x_ref, g_ref, w_ref, h_ref, o_ref):
        xv = x_ref[...]
        h = (xv * _rms(xv) * g_ref[...]).astype(bf16)
        h_ref[...] = h
        for s in range(NS):
            o_ref[:, s * ns:(s + 1) * ns] = jnp.dot(_mx(h), w_ref[s], preferred_element_type=f32).astype(bf16)

    return _hosted_call(
        body, T // tm,
        [pl.BlockSpec((tm, D), lambda i: (i, 0)),
         pl.BlockSpec((1, D), lambda i: (0, 0)),
         pl.BlockSpec((NS, D, ns), lambda i: (0, 0, 0), pipeline_mode=RESIDENT)],
        [pl.BlockSpec((tm, D), lambda i: (i, 0)), pl.BlockSpec((tm, NS * ns), lambda i: (i, 0))],
        [SDS((T, D), bf16), SDS((T, NS * ns), bf16)], [], (x, gain, w), "norm_mm_in", comm)


def _norm_mm_swiglu(x, gain, w, tm, comm=None):
    T, D = x.shape
    NS, _, ns = w.shape
    half = NS // 2

    def body(x_ref, g_ref, w_ref, h_ref, gu_ref, a_ref):
        xv = x_ref[...]
        hb = (xv * _rms(xv) * g_ref[...]).astype(bf16)
        h_ref[...] = hb
        h = _mx(hb)
        for s in range(half):
            cols = slice(s * ns, (s + 1) * ns)
            g = jnp.dot(h, w_ref[s], preferred_element_type=f32)
            u = jnp.dot(h, w_ref[s + half], preferred_element_type=f32)
            gu_ref[0, :, cols] = g.astype(bf16)
            gu_ref[1, :, cols] = u.astype(bf16)
            a_ref[:, cols] = (jax.nn.silu(g) * u).astype(bf16)

    c_args, c_in, c_out, c_shapes, aliases, c_scratch = _host_specs(comm, 3, 3)
    step = lambda v: (lambda: pl.program_id(0) == v)
    return pl.pallas_call(
        _host(body, 3, 3, 0, comm, step(0), step(T // tm - 1)), grid=(T // tm,),
        in_specs=[pl.BlockSpec((tm, D), lambda i: (i, 0)),
                  pl.BlockSpec((1, D), lambda i: (0, 0)),
                  pl.BlockSpec((NS, D, ns), lambda i: (0, 0, 0), pipeline_mode=RESIDENT)] + c_in,
        out_specs=[pl.BlockSpec((tm, D), lambda i: (i, 0)),
                   pl.BlockSpec((2, tm, half * ns), lambda i: (0, i, 0)),
                   pl.BlockSpec((tm, half * ns), lambda i: (i, 0))] + c_out,
        out_shape=[SDS((T, D), bf16), SDS((2, T, half * ns), bf16), SDS((T, half * ns), bf16)] + c_shapes,
        input_output_aliases=aliases, scratch_shapes=c_scratch,
        name="norm_mm_swiglu" if comm is None else "norm_mm_swiglu_gather",
        compiler_params=_cp("arbitrary"))(x, gain, w, *c_args)


def _gnorm_mm_res(ys, gain, w, x, tm):
    T, D = x.shape

    def body(ya, yb, yc, yd, g_ref, w_ref, x_ref, yn_ref, o_ref):
        parts = []
        for m, r in enumerate((ya, yb, yc, yd)):
            y = r[...].astype(f32)
            parts.append((y * _rms(y) * g_ref[:, m * GW:(m + 1) * GW]).astype(bf16))
        yn = jnp.concatenate(parts, axis=1)
        yn_ref[...] = yn
        o_ref[...] = x_ref[...] + jnp.dot(_mx(yn), w_ref[...], preferred_element_type=f32)

    yspec = pl.BlockSpec((tm, GW), lambda i: (i, 0))
    return pl.pallas_call(
        body, grid=(T // tm,),
        in_specs=[yspec, yspec, yspec, yspec,
                  pl.BlockSpec((1, D), lambda i: (0, 0)),
                  pl.BlockSpec((D, D), lambda i: (0, 0)),
                  pl.BlockSpec((tm, D), lambda i: (i, 0))],
        out_specs=[pl.BlockSpec((tm, D), lambda i: (i, 0)), pl.BlockSpec((tm, D), lambda i: (i, 0))],
        out_shape=[SDS((T, D), bf16), SDS((T, D), f32)],
        name="gnorm_mm_res", compiler_params=_cp("parallel"))(*ys, gain, w, x)


def _mm_res(a, w, x, tm, comm=None):
    T, D = x.shape
    K = a.shape[1]

    def body(a_ref, w_ref, x_ref, o_ref):
        o_ref[...] = x_ref[...] + jnp.dot(_mx(a_ref[...]), w_ref[...], preferred_element_type=f32)

    return _hosted_call(
        body, T // tm,
        [pl.BlockSpec((tm, K), lambda i: (i, 0)),
         pl.BlockSpec((K, D), lambda i: (0, 0), pipeline_mode=RESIDENT),
         pl.BlockSpec((tm, D), lambda i: (i, 0))],
        [pl.BlockSpec((tm, D), lambda i: (i, 0))], [SDS((T, D), f32)], [], (a, w, x), "mm_res_down", comm)


def _final_loss(x, gain, tgt, tm):
    T, D = x.shape

    def body(x_ref, g_ref, t_ref, dx_ref, dg_ref, l_ref):
        @pl.when(pl.program_id(0) == 0)
        def _():
            dg_ref[...] = jnp.zeros_like(dg_ref)
            l_ref[...] = jnp.zeros_like(l_ref)
        xv = x_ref[...]
        g = g_ref[...]
        r = _rms(xv)
        xh = xv * r
        err = xh * g - t_ref[...]
        l_ref[...] += 0.5 * jnp.sum(jnp.mean(err * err, axis=-1, keepdims=True), axis=0, keepdims=True)
        dy = err * (1.0 / D)
        dg_ref[...] += jnp.sum(dy * xh, axis=0, keepdims=True)
        dxh = dy * g
        dx_ref[...] = r * (dxh - xh * jnp.mean(dxh * xh, axis=-1, keepdims=True))

    return pl.pallas_call(
        body, grid=(T // tm,),
        in_specs=[pl.BlockSpec((tm, D), lambda i: (i, 0)),
                  pl.BlockSpec((1, D), lambda i: (0, 0)),
                  pl.BlockSpec((tm, D), lambda i: (i, 0))],
        out_specs=[pl.BlockSpec((tm, D), lambda i: (i, 0)),
                   pl.BlockSpec((1, D), lambda i: (0, 0)),
                   pl.BlockSpec((1, BLK), lambda i: (0, 0))],
        out_shape=[SDS((T, D), f32), SDS((1, D), f32), SDS((1, BLK), f32)],
        name="final_loss", compiler_params=_cp("arbitrary"))(x, gain, tgt)


def _dact(dx, wd, gu, tm, comm=None):
    T, D = dx.shape
    F = wd.shape[0]
    ns = F // 2

    def body(dx_ref, w_ref, gu_ref, o_ref):
        dxb = _mx(dx_ref[...])
        for s in range(2):
            cols = slice(s * ns, (s + 1) * ns)
            da = lax.dot_general(dxb, w_ref[s * ns:(s + 1) * ns, :], NT, preferred_element_type=f32)
            g = gu_ref[0, :, cols].astype(f32)
            u = gu_ref[1, :, cols].astype(f32)
            sg = jax.nn.sigmoid(g)
            o_ref[0, :, cols] = (da * u * (sg * (1.0 + g * (1.0 - sg)))).astype(bf16)
            o_ref[1, :, cols] = (da * (g * sg)).astype(bf16)

    return _hosted_call(
        body, T // tm,
        [pl.BlockSpec((tm, D), lambda i: (i, 0)),
         pl.BlockSpec((F, D), lambda i: (0, 0), pipeline_mode=RESIDENT),
         pl.BlockSpec((2, tm, F), lambda i: (0, i, 0))],
        [pl.BlockSpec((2, tm, F), lambda i: (0, i, 0))], [SDS((2, T, F), bf16)], [], (dx, wd, gu), "dact", comm)


def _dw(a, b, b_map, ns, NS, tka, tk, name):
    T, Ka = a.shape
    b_block = (tk, ns) if b.ndim == 2 else (1, tk, ns)

    def body(a_ref, b_ref, o_ref):
        bv = b_ref[...] if b.ndim == 2 else b_ref[0]
        part = lax.dot_general(_mx(a_ref[...]), _mx(bv), TN, preferred_element_type=f32)

        @pl.when(pl.program_id(2) == 0)
        def _():
            o_ref[0] = part

        @pl.when(pl.program_id(2) > 0)
        def _():
            o_ref[0] += part

    return pl.pallas_call(
        body, grid=(NS, Ka // tka, T // tk),
        in_specs=[pl.BlockSpec((tk, tka), lambda s, k, t: (t, k)),
                  pl.BlockSpec(b_block, lambda s, k, t: b_map(t, s))],
        out_specs=pl.BlockSpec((1, tka, ns), lambda s, k, t: (s, k, 0)),
        out_shape=SDS((NS, Ka, ns), f32),
        name=name, compiler_params=_cp("parallel", "parallel", "arbitrary"))(a, b)


def _dw_pieces(a, pieces, ns, NS, tk, name):
    T, Ka = a.shape
    n = len(pieces)

    def body(*refs):
        a_ref, b_refs, o_ref = refs[0], refs[1:1 + n], refs[1 + n]
        full = jnp.concatenate([r[...] for r in b_refs], axis=1)
        av = _mx(a_ref[...])
        parts = [lax.dot_general(av, _mx(full[:, s * ns:(s + 1) * ns]), TN, preferred_element_type=f32) for s in range(NS)]

        @pl.when(pl.program_id(0) == 0)
        def _():
            for s in range(NS):
                o_ref[s] = parts[s]

        @pl.when(pl.program_id(0) > 0)
        def _():
            for s in range(NS):
                o_ref[s] += parts[s]

    return pl.pallas_call(
        body, grid=(T // tk,),
        in_specs=[pl.BlockSpec((tk, Ka), lambda t: (t, 0))] + [pl.BlockSpec((tk, p.shape[1]), lambda t: (t, 0)) for p in pieces],
        out_specs=pl.BlockSpec((NS, Ka, ns), lambda t: (0, 0, 0)),
        out_shape=SDS((NS, Ka, ns), f32),
        name=name, compiler_params=_cp("arbitrary"))(a, *pieces)


def _dx_norm_bwd(dy, w, x, gain, dxin, tm, name, comm=None):
    T, D = x.shape
    NS, _, ns = w.shape
    half = NS // 2
    pieces = list(dy) if isinstance(dy, (list, tuple)) else None
    n_dy = len(pieces) if pieces else 1

    def body(*refs):
        dy_refs = refs[:n_dy]
        w_ref, x_ref, g_ref, dxin_ref, dx_ref, dg_ref = refs[n_dy:]

        @pl.when(pl.program_id(0) == 0)
        def _():
            dg_ref[...] = jnp.zeros_like(dg_ref)
        if pieces:
            full = jnp.concatenate([r[...] for r in dy_refs], axis=1)
        dh = None
        for s in range(NS):
            if pieces:
                dv = full[:, s * ns:(s + 1) * ns]
            else:
                dv = dy_refs[0][s // half, :, (s % half) * ns:(s % half + 1) * ns]
            part = lax.dot_general(_mx(dv), w_ref[s], NT, preferred_element_type=f32)
            dh = part if dh is None else dh + part
        xv = x_ref[...]
        r = _rms(xv)
        xh = xv * r
        dg_ref[...] += jnp.sum(dh * xh, axis=0, keepdims=True)
        dxh = dh * g_ref[...]
        dx_ref[...] = dxin_ref[...] + r * (dxh - xh * jnp.mean(dxh * xh, axis=-1, keepdims=True))

    if pieces:
        dy_specs = [pl.BlockSpec((tm, p.shape[1]), lambda i: (i, 0)) for p in pieces]
    else:
        dy_specs = [pl.BlockSpec((2, tm, half * ns), lambda i: (0, i, 0))]
    return _hosted_call(
        body, T // tm,
        dy_specs + [pl.BlockSpec((NS, D, ns), lambda i: (0, 0, 0), pipeline_mode=RESIDENT),
                    pl.BlockSpec((tm, D), lambda i: (i, 0)),
                    pl.BlockSpec((1, D), lambda i: (0, 0)),
                    pl.BlockSpec((tm, D), lambda i: (i, 0))],
        [pl.BlockSpec((tm, D), lambda i: (i, 0)), pl.BlockSpec((1, D), lambda i: (0, 0))],
        [SDS((T, D), f32), SDS((1, D), f32)], [], (*(pieces or [dy]), w, x, gain, dxin), name, comm)


def _dycat(dx, w, ys, gain, tm, comm=None):
    T, D = dx.shape

    def body(dx_ref, w_ref, ya, yb, yc, yd, g_ref, da, db, dc, dd, dg_ref):
        @pl.when(pl.program_id(0) == 0)
        def _():
            dg_ref[...] = jnp.zeros_like(dg_ref)
        dyn = lax.dot_general(_mx(dx_ref[...]), w_ref[...], NT, preferred_element_type=f32)
        for m, (r, o) in enumerate(((ya, da), (yb, db), (yc, dc), (yd, dd))):
            cols = slice(m * GW, (m + 1) * GW)
            y = r[...].astype(f32)
            rs = _rms(y)
            yh = y * rs
            d = dyn[:, cols]
            dg_ref[:, cols] += jnp.sum(d * yh, axis=0, keepdims=True)
            dyh = d * g_ref[:, cols]
            o[...] = (rs * (dyh - yh * jnp.mean(dyh * yh, axis=-1, keepdims=True))).astype(bf16)

    yspec = pl.BlockSpec((tm, GW), lambda i: (i, 0))
    return _hosted_call(
        body, T // tm,
        [pl.BlockSpec((tm, D), lambda i: (i, 0)),
         pl.BlockSpec((D, D), lambda i: (0, 0), pipeline_mode=RESIDENT),
         yspec, yspec, yspec, yspec,
         pl.BlockSpec((1, D), lambda i: (0, 0))],
        [yspec, yspec, yspec, yspec, pl.BlockSpec((1, D), lambda i: (0, 0))],
        [SDS((T, GW), bf16)] * 4 + [SDS((1, D), f32)], [], (dx, w, *ys, gain), "dycat", comm)


def _sgu_consts():
    r, c = _iota((GW, GW), 0), _iota((GW, GW), 1)
    seg = (r // HD == c // HD).astype(f32)
    tr, ts = _iota((BLK, BLK), 0), _iota((BLK, BLK), 1)
    causal = ts <= tr
    lane_head = _iota((BLK, GW), 1) // HD
    return seg, causal, lane_head


def _split3_dot(a, ones):
    hi = a.astype(bf16)
    r1 = a - hi.astype(f32)
    mid = r1.astype(bf16)
    lo = (r1 - mid.astype(f32)).astype(bf16)
    dot = functools.partial(jnp.dot, preferred_element_type=f32)
    return dot(hi, ones) + dot(mid, ones) + dot(lo, ones)


def _sgu_chunks(aus, avs, w, bexp, consts):
    seg, causal, lane_head = consts
    segb = seg.astype(bf16)
    nh = GW // HD
    vs = [jax.nn.gelu(av) for av in avs]
    mus = [_split3_dot(v, segb) * (1.0 / HD) for v in vs]
    vcs = [v - mu for v, mu in zip(vs, mus)]
    vars_ = [_split3_dot(vc * vc, segb) * (1.0 / HD) for vc in vcs]
    vns = [_mx(vc * lax.rsqrt(var + EPS)) for vc, var in zip(vcs, vars_)]
    whs = [_mx(jnp.where(causal, w[h], 0.0)) for h in range(nh)]
    mixes = [[jnp.dot(whs[h], vn, preferred_element_type=f32) for h in range(nh)] for vn in vns]
    out = []
    for au, ms in zip(aus, mixes):
        mix = bexp
        for h in range(nh):
            mix = mix + jnp.where(lane_head == h, ms[h], 0.0)
        out.append(jax.nn.gelu(au) * mix)
    return out


def _sgu_group(S):
    nc = S // BLK
    return 4 if nc % 4 == 0 else (2 if nc % 2 == 0 else 1)


def _sgu_fwd(proj, w, bexp, B, S):
    G = _sgu_group(S)

    def body(au_ref, av_ref, w_ref, b_ref, y_ref):
        consts = _sgu_consts()
        wv, bv = w_ref[...], b_ref[...]

        def group(n, c):
            rows = [pl.ds(pl.multiple_of((n * G + j) * BLK, BLK), BLK) for j in range(G)]
            ys = _sgu_chunks([au_ref[r, :].astype(f32) for r in rows], [av_ref[r, :].astype(f32) for r in rows],
                             wv, bv, consts)
            for r, y in zip(rows, ys):
                y_ref[r, :] = y.astype(bf16)
            return c
        lax.fori_loop(0, S // BLK // G, group, 0)

    return pl.pallas_call(
        body, grid=(B,),
        in_specs=[pl.BlockSpec((S, GW), lambda b: (b, 0)),
                  pl.BlockSpec((S, GW), lambda b: (b, 1)),
                  pl.BlockSpec((GW // HD, BLK, BLK), lambda b: (0, 0, 0)),
                  pl.BlockSpec((BLK, GW), lambda b: (0, 0))],
        out_specs=pl.BlockSpec((S, GW), lambda b: (b, 0)),
        out_shape=SDS((B * S, GW), bf16),
        name="sgu_fwd", compiler_params=_cp("parallel"))(proj, proj, w, bexp)


def _sgu_bwd(proj, w, bexp, dy, B, S, comm=None):
    def body(au_ref, av_ref, w_ref, b_ref, dy_ref, dp_ref, dw_ref, db_ref):
        @pl.when(pl.program_id(0) == 0)
        def _():
            dw_ref[...] = jnp.zeros_like(dw_ref)
            db_ref[...] = jnp.zeros_like(db_ref)
        consts = _sgu_consts()
        wv, bv = w_ref[...], b_ref[...]
        fn = lambda aus, avs, ww, bb: _sgu_chunks(aus, avs, ww, bb, consts)
        G = _sgu_group(S)

        def group(n, carry):
            dw_acc, db_acc = carry
            rows = [pl.ds(pl.multiple_of((n * G + j) * BLK, BLK), BLK) for j in range(G)]
            _, vjp = jax.vjp(fn, [au_ref[r, :].astype(f32) for r in rows], [av_ref[r, :].astype(f32) for r in rows], wv, bv)
            daus, davs, dwc, dbc = vjp([dy_ref[r, :].astype(f32) for r in rows])
            for r, dau, dav in zip(rows, daus, davs):
                dp_ref[r, 0:GW] = dau.astype(bf16)
                dp_ref[r, GW:2 * GW] = dav.astype(bf16)
            return dw_acc + dwc, db_acc + dbc
        dw_acc, db_acc = lax.fori_loop(0, S // BLK // G, group, (jnp.zeros(wv.shape, f32), jnp.zeros(bv.shape, f32)))
        dw_ref[...] += dw_acc
        db_ref[...] += jnp.dot(db_acc, consts[0], precision=HIGHEST, preferred_element_type=f32)

    return _hosted_call(
        body, B,
        [pl.BlockSpec((S, GW), lambda b: (b, 0)),
         pl.BlockSpec((S, GW), lambda b: (b, 1)),
         pl.BlockSpec((GW // HD, BLK, BLK), lambda b: (0, 0, 0)),
         pl.BlockSpec((BLK, GW), lambda b: (0, 0)),
         pl.BlockSpec((S, GW), lambda b: (b, 0))],
        [pl.BlockSpec((S, 2 * GW), lambda b: (b, 0)),
         pl.BlockSpec((GW // HD, BLK, BLK), lambda b: (0, 0, 0)),
         pl.BlockSpec((BLK, GW), lambda b: (0, 0))],
        [SDS((B * S, 2 * GW), bf16), SDS((GW // HD, BLK, BLK), f32), SDS((BLK, GW), f32)], [],
        (proj, proj, w, bexp, dy), "sgu_bwd", comm)


def _pool_parts(p):
    n = p.shape[0]
    r = _iota(p.shape, 0)
    lg = _iota(p.shape, 1) // HD

    def sh(v, k):
        return jnp.where(r >= k, pltpu.roll(v, k, 0), 0.0)
    s2 = p + sh(p, 1)
    s4 = s2 + sh(s2, 2)
    s8 = s4 + sh(s4, 4)
    s16 = s8 + sh(s8, 8)
    ws = jnp.where(lg == 0, s2, jnp.where(lg == 1, s4, jnp.where(lg == 2, s8, s16)))
    wlen = jnp.where(lg == 0, 2, jnp.where(lg == 1, 4, jnp.where(lg == 2, 8, 16)))
    cnt = jnp.minimum(r + 1, wlen).astype(f32)
    del n
    return ws / cnt - p, cnt, lg


def _pool_fwd(proj, wbd, scale, B, S):
    def body(p_ref, w_ref, s_ref, y_ref):
        y, _, _ = _pool_parts(p_ref[...].astype(f32))
        y_ref[...] = (jnp.dot(_mx(y), _mx(w_ref[...]), preferred_element_type=f32) * s_ref[...]).astype(bf16)

    return pl.pallas_call(
        body, grid=(B,),
        in_specs=[pl.BlockSpec((S, GW), lambda b: (b, 2)),
                  pl.BlockSpec((GW, GW), lambda b: (0, 0)),
                  pl.BlockSpec((1, GW), lambda b: (0, 0))],
        out_specs=pl.BlockSpec((S, GW), lambda b: (b, 0)),
        out_shape=SDS((B * S, GW), bf16),
        name="pool_fwd", compiler_params=_cp("parallel"))(proj, wbd, scale)


def _pool_bwd(proj, wbd, scale, dy, B, S):
    def body(p_ref, w_ref, s_ref, dy_ref, dp_ref, dw_ref, ds_ref):
        @pl.when(pl.program_id(0) == 0)
        def _():
            dw_ref[...] = jnp.zeros_like(dw_ref)
            ds_ref[...] = jnp.zeros_like(ds_ref)
        y, cnt, lg = _pool_parts(p_ref[...].astype(f32))
        wv = _mx(w_ref[...])
        z = jnp.dot(_mx(y), wv, preferred_element_type=f32)
        dout = dy_ref[...].astype(f32)
        ds_ref[...] += jnp.sum(dout * z, axis=0, keepdims=True)
        dz = _mx(dout * s_ref[...])
        dw_ref[...] += lax.dot_general(_mx(y), dz, TN, preferred_element_type=f32)
        dyv = lax.dot_general(dz, wv, NT, preferred_element_type=f32)
        n = dyv.shape[0]
        r = _iota(dyv.shape, 0)

        def ush(v, k):
            return jnp.where(r < n - k, pltpu.roll(v, n - k, 0), 0.0)
        gq = dyv / cnt
        a2 = gq + ush(gq, 1)
        a4 = a2 + ush(a2, 2)
        a8 = a4 + ush(a4, 4)
        a16 = a8 + ush(a8, 8)
        adj = jnp.where(lg == 0, a2, jnp.where(lg == 1, a4, jnp.where(lg == 2, a8, a16)))
        dp_ref[...] = (adj - dyv).astype(bf16)

    return pl.pallas_call(
        body, grid=(B,),
        in_specs=[pl.BlockSpec((S, GW), lambda b: (b, 2)),
                  pl.BlockSpec((GW, GW), lambda b: (0, 0)),
                  pl.BlockSpec((1, GW), lambda b: (0, 0)),
                  pl.BlockSpec((S, GW), lambda b: (b, 0))],
        out_specs=[pl.BlockSpec((S, GW), lambda b: (b, 0)),
                   pl.BlockSpec((GW, GW), lambda b: (0, 0)),
                   pl.BlockSpec((1, GW), lambda b: (0, 0))],
        out_shape=[SDS((B * S, GW), bf16), SDS((GW, GW), f32), SDS((1, GW), f32)],
        name="pool_bwd", compiler_params=_cp("arbitrary"))(proj, wbd, scale, dy)


def _t5_bucket_table():
    dist = (np.arange(BLK)[:, None] + BLK) - np.arange(2 * BLK)[None, :]
    d = np.clip(dist, 0, BLK - 1)
    max_exact = N_BUCKETS // 2
    df = np.maximum(d, 1).astype(np.float32)
    large = max_exact + (np.log(df / max_exact) / np.float32(np.log(MAX_DISTANCE / max_exact))
                         * (N_BUCKETS - max_exact)).astype(np.int32)
    large = np.minimum(large, N_BUCKETS - 1)
    return np.where(d < max_exact, d, large).astype(np.int32)


def _swa_blocks(qs, kx, vx, sinks, biases, n):
    G = len(qs)
    heads = [(p, g) for p in range(2) for g in range(2)]
    ri, ci = _iota((BLK, BLK), 0), _iota((BLK, BLK), 1)
    qi, ki = _iota((BLK, 2 * BLK), 0), _iota((BLK, 2 * BLK), 1)
    dist = qi + BLK - ki
    band = (dist >= 0) & (dist < BLK)
    masks = [band & ((ki >= BLK) | (n > 0))] + [band] * (G - 1)
    kb, vb = _mx(kx), _mx(vx)
    qsel = [[None] * 4 for _ in range(G)]
    vs = []
    for h, (p, g) in enumerate(heads):
        selq = ((ri - g * HD == ci - p * HD) & (ri >= g * HD) & (ri < (g + 1) * HD)).astype(_MXU)
        selv = ((ci - g * HD == ri - p * HD) & (ci >= g * HD) & (ci < (g + 1) * HD)).astype(_MXU)
        for b in range(G):
            qsel[b][h] = _mx(jnp.dot(_mx(qs[b][p]), selq, preferred_element_type=f32))
        vs.append(_mx(jnp.dot(vb, selv, preferred_element_type=f32)))
    zs = [[lax.dot_general(qsel[b][h], kb[b * BLK:(b + 2) * BLK], NT, preferred_element_type=f32) * (HD ** -0.5)
           for h in range(4)] for b in range(G)]
    prs = [[None] * 4 for _ in range(G)]
    for b in range(G):
        for h in range(4):
            z = jnp.where(masks[b], zs[b][h] + biases[h], -1e30)
            s = jnp.mean(sinks[h], axis=-1, keepdims=True)
            m = jnp.maximum(jnp.max(z, axis=-1, keepdims=True), s)
            e = jnp.exp(z - m)
            prs[b][h] = _mx(e / (jnp.sum(e, axis=-1, keepdims=True) + jnp.exp(s - m)))
    outs = [[jnp.dot(prs[b][h], vs[h][b * BLK:(b + 2) * BLK], preferred_element_type=f32) for h in range(4)]
            for b in range(G)]
    return [[o[0] + o[1], o[2] + o[3]] for o in outs]


def _swa_group(S):
    return 2 if (S // BLK) % 2 == 0 else 1


def _swa_rows(n, G):
    blk = lambda j: pl.ds(pl.multiple_of(j * BLK, BLK), BLK)
    return [blk(jnp.maximum(n - 1, 0))] + [blk(n + b) for b in range(G)]


def _swa_fwd(proj, sinks, bias, B, S, comm=None):
    G = _swa_group(S)

    def body(q_ref, kv_ref, s_ref, b_ref, y_ref):
        def group(i, c):
            n = i * G
            rows = _swa_rows(n, G)
            kx = jnp.concatenate([kv_ref[r, 0:BLK] for r in rows], axis=0).astype(f32)
            vx = jnp.concatenate([kv_ref[r, BLK:2 * BLK] for r in rows], axis=0).astype(f32)
            qs = [[q_ref[r, 0:BLK].astype(f32), q_ref[r, BLK:2 * BLK].astype(f32)] for r in rows[1:]]
            outs = _swa_blocks(qs, kx, vx, [s_ref[h] for h in range(4)], [b_ref[h] for h in range(4)], n)
            for r, (o0, o1) in zip(rows[1:], outs):
                y_ref[r, 0:BLK] = o0.astype(bf16)
                y_ref[r, BLK:2 * BLK] = o1.astype(bf16)
            return c
        lax.fori_loop(0, S // BLK // G, group, 0)

    return _hosted_call(
        body, B,
        [pl.BlockSpec((S, GW), lambda b: (b, 3)),
         pl.BlockSpec((S, GW), lambda b: (b, 4)),
         pl.BlockSpec((4, 1, BLK), lambda b: (0, 0, 0)),
         pl.BlockSpec((4, BLK, 2 * BLK), lambda b: (0, 0, 0))],
        [pl.BlockSpec((S, GW), lambda b: (b, 0))], [SDS((B * S, GW), bf16)], [],
        (proj, proj, sinks, bias), "swa_fwd", comm)


def _swa_bwd(proj, sinks, bias, dy, B, S, comm=None):
    def body(q_ref, kv_ref, s_ref, b_ref, dy_ref, dq_ref, dkv_ref, ds_ref, db_ref, acc_ref):
        @pl.when(pl.program_id(0) == 0)
        def _():
            ds_ref[...] = jnp.zeros_like(ds_ref)
            db_ref[...] = jnp.zeros_like(db_ref)
        acc_ref[...] = jnp.zeros_like(acc_ref)

        G = _swa_group(S)

        def group(i, c):
            n = i * G
            rows = _swa_rows(n, G)
            kx = jnp.concatenate([kv_ref[r, 0:BLK] for r in rows], axis=0).astype(f32)
            vx = jnp.concatenate([kv_ref[r, BLK:2 * BLK] for r in rows], axis=0).astype(f32)
            qs = [[q_ref[r, 0:BLK].astype(f32), q_ref[r, BLK:2 * BLK].astype(f32)] for r in rows[1:]]
            dos = [[dy_ref[r, 0:BLK].astype(f32), dy_ref[r, BLK:2 * BLK].astype(f32)] for r in rows[1:]]
            fn = functools.partial(_swa_blocks, n=n)
            _, vjp = jax.vjp(fn, qs, kx, vx, [s_ref[h] for h in range(4)], [b_ref[h] for h in range(4)])
            dqs, dkx, dvx, dss, dbs = vjp(dos)
            for r, (dq0, dq1) in zip(rows[1:], dqs):
                dq_ref[r, 0:BLK] = dq0.astype(bf16)
                dq_ref[r, BLK:2 * BLK] = dq1.astype(bf16)
            for h in range(4):
                ds_ref[h] += dss[h]
                db_ref[h] += dbs[h]
            for j, r in enumerate(rows):
                acc_ref[r, 0:BLK] += dkx[j * BLK:(j + 1) * BLK]
                acc_ref[r, BLK:2 * BLK] += dvx[j * BLK:(j + 1) * BLK]
            return c
        lax.fori_loop(0, S // BLK // G, group, 0)
        dkv_ref[...] = acc_ref[...].astype(bf16)

    c_args, c_in, c_out, c_shapes, aliases, c_scratch = _host_specs(comm, 5, 4)
    step = lambda v: (lambda: pl.program_id(0) == v)
    return pl.pallas_call(
        _host(body, 5, 4, 1, comm, step(0), step(B - 1)), grid=(B,),
        in_specs=[pl.BlockSpec((S, GW), lambda b: (b, 3)),
                  pl.BlockSpec((S, GW), lambda b: (b, 4)),
                  pl.BlockSpec((4, 1, BLK), lambda b: (0, 0, 0)),
                  pl.BlockSpec((4, BLK, 2 * BLK), lambda b: (0, 0, 0)),
                  pl.BlockSpec((S, GW), lambda b: (b, 0))] + c_in,
        out_specs=[pl.BlockSpec((S, GW), lambda b: (b, 0)),
                   pl.BlockSpec((S, GW), lambda b: (b, 0)),
                   pl.BlockSpec((4, 1, BLK), lambda b: (0, 0, 0)),
                   pl.BlockSpec((4, BLK, 2 * BLK), lambda b: (0, 0, 0))] + c_out,
        out_shape=[SDS((B * S, GW), bf16), SDS((B * S, GW), bf16), SDS((4, 1, BLK), f32),
                   SDS((4, BLK, 2 * BLK), f32)] + c_shapes,
        input_output_aliases=aliases, scratch_shapes=[pltpu.VMEM((S, GW), f32)] + c_scratch,
        name="swa_bwd" if comm is None else "swa_bwd_exchange",
        compiler_params=_cp("arbitrary"))(proj, proj, sinks, bias, dy, *c_args)


def _log1m_parts(z):
    t = jnp.exp(-jnp.abs(z))
    return jnp.minimum(-z, 0.0) - jnp.log(1.0 + t), t


def _log1m(z):
    return _log1m_parts(z)[0]


def _sigmoid_from(z, t):
    return jnp.where(z >= 0.0, 1.0, t) / (1.0 + t)


def _sb_consts(tri):
    r2, c2 = _iota((2 * BLK, 2 * BLK), 0), _iota((2 * BLK, 2 * BLK), 1)
    tri2 = (tri(r2, c2) & (r2 // BLK == c2 // BLK)).astype(bf16)
    ri, ci = _iota((BLK, 2 * BLK), 0), _iota((BLK, 2 * BLK), 1)
    strict2 = (ci % BLK) < ri
    head0 = _iota((BLK, BLK), 1) < HD
    return tri2, strict2, head0


def _sb_stack_kv(k_ref, v_ref, kst_ref, vst_ref, head0, nb):
    def one(kb, c):
        krows = pl.ds(pl.multiple_of(kb * BLK, BLK), BLK)
        for p in range(2):
            for src, dst in ((k_ref, kst_ref), (v_ref, vst_ref)):
                t = src[krows, p * BLK:(p + 1) * BLK]
                dst[p, kb] = _mx(jnp.concatenate([jnp.where(head0, t, 0.0), jnp.where(head0, 0.0, t)], axis=0))
        return c
    lax.fori_loop(0, nb, one, 0)


def _sb_load_kv(kst_ref, vst_ref, kb):
    return [kst_ref[p, kb] for p in range(2)], [vst_ref[p, kb] for p in range(2)]


def _two_halves(a, b):
    return jnp.concatenate([jnp.broadcast_to(a, (BLK, BLK)), jnp.broadcast_to(b, (BLK, BLK))], axis=1)


def _half_sums(t):
    return jnp.sum(t[:, :BLK], axis=-1, keepdims=True), jnp.sum(t[:, BLK:], axis=-1, keepdims=True)


def _sb_fwd(proj, B, S, comm=None):
    def body(q_ref, k_ref, v_ref, y_ref, lt_ref, kst_ref, vst_ref):
        ci = _iota((BLK, BLK), 1)
        above2, strict2, head0 = _sb_consts(lambda r, c: r > c)
        _sb_stack_kv(k_ref, v_ref, kst_ref, vst_ref, head0, S // BLK)

        def step(qs, kbs, diags, carry):
            U = range(len(kbs))
            ok = [None if diags[u] else kbs[u] >= 0 for u in U]
            kv = [_sb_load_kv(kst_ref, vst_ref, jnp.maximum(kb, 0)) for kb in kbs]
            zs = [[lax.dot_general(qs[p], kks[p], NT, preferred_element_type=f32) for p in range(2)] for kks, _ in kv]
            Ls = [[jnp.where(strict2, _log1m(z), 0.0) if diags[u] else _log1m(z) for z in zs[u]] for u in U]
            tails = [[_split_dot(L, above2) for L in Lu] for Lu in Ls]
            carry = list(carry)
            for u in U:
                for p in range(2):
                    R0, R1, acc = carry[3 * p:3 * p + 3]
                    w = jnp.exp(zs[u][p] + Ls[u][p] + tails[u][p] + _two_halves(R0, R1))
                    s0, s1 = _half_sums(Ls[u][p])
                    if diags[u]:
                        w = jnp.where(strict2, w, 0.0)
                    else:
                        w, s0, s1 = (jnp.where(ok[u], t, 0.0) for t in (w, s0, s1))
                    acc = acc + jnp.dot(_mx(w), kv[u][1][p], preferred_element_type=f32)
                    carry[3 * p:3 * p + 3] = [R0 + s0, R1 + s1, acc]
            return tuple(carry)

        def qblock(n, c):
            qrows = pl.ds(pl.multiple_of(n * BLK, BLK), BLK)
            qs = [_mx(q_ref[qrows, p * BLK:(p + 1) * BLK] * (HD ** -0.5)) for p in range(2)]
            z1, z2 = jnp.zeros((BLK, 1), f32), jnp.zeros((BLK, BLK), f32)
            near = [n - 1 - u for u in range(SB_UNROLL)]
            carry = step(qs, [n] + near, [True] + [False] * SB_UNROLL, (z1, z1, z2, z1, z1, z2))
            far = jnp.maximum(n - SB_UNROLL, 0)
            trips = (far + SB_UNROLL - 1) // SB_UNROLL

            def live(st):
                worst = jnp.maximum(jnp.maximum(st[1], st[2]), jnp.maximum(st[4], st[5]))
                return (st[0] < trips) & (jnp.max(worst) > SB_CUT)

            def trip(st):
                i = st[0]
                kbs = [far - 1 - SB_UNROLL * i - u for u in range(SB_UNROLL)]
                return (i + 1,) + step(qs, kbs, [False] * SB_UNROLL, st[1:])
            done, *res = lax.while_loop(live, trip, (jnp.int32(0),) + carry)
            lt = jnp.where(ci == SB_HEADS, done.astype(f32), 0.0)
            for p in range(2):
                y_ref[qrows, p * BLK:(p + 1) * BLK] = res[3 * p + 2].astype(bf16)
                lt = lt + jnp.where(ci == 2 * p, res[3 * p], 0.0) + jnp.where(ci == 2 * p + 1, res[3 * p + 1], 0.0)
            lt_ref[qrows, :] = lt
            return c
        lax.fori_loop(0, S // BLK, qblock, 0)

    spec = lambda j: pl.BlockSpec((S, GW), lambda b: (b, j))
    c_args, c_in, c_out, c_shapes, aliases, c_scratch = _host_specs(comm, 3, 2)
    step = lambda v: (lambda: pl.program_id(0) == v)
    stacked = pltpu.VMEM((2, S // BLK, 2 * BLK, BLK), _MXU)
    return pl.pallas_call(
        _host(body, 3, 2, 2, comm, step(0), step(B - 1)), grid=(B,),
        in_specs=[spec(5), spec(6), spec(7)] + c_in,
        out_specs=[pl.BlockSpec((S, GW), lambda b: (b, 0)), pl.BlockSpec((S, BLK), lambda b: (b, 0))] + c_out,
        out_shape=[SDS((B * S, GW), bf16), SDS((B * S, BLK), f32)] + c_shapes,
        input_output_aliases=aliases, scratch_shapes=[stacked, stacked] + c_scratch,
        name="sb_fwd" if comm is None else "sb_fwd_gather",
        compiler_params=_cp("arbitrary"))(proj, proj, proj, *c_args)


def _sb_bwd(proj, ltot, dy, B, S, comm=None):
    def body(q_ref, k_ref, v_ref, lt_ref, dy_ref, dq_ref, dk_ref, dv_ref, dka_ref, dva_ref, kst_ref, vst_ref):
        ci = _iota((BLK, BLK), 1)
        upto2, strict2, head0 = _sb_consts(lambda r, c: r <= c)
        below2, _, _ = _sb_consts(lambda r, c: r < c)
        dka_ref[...] = jnp.zeros_like(dka_ref)
        dva_ref[...] = jnp.zeros_like(dva_ref)
        _sb_stack_kv(k_ref, v_ref, kst_ref, vst_ref, head0, S // BLK)

        def step(qs, dos, lts, kbs, ok, diags, top, carry):
            U = range(len(kbs))
            kbs = [jnp.clip(kb, 0, top) for kb in kbs]
            kv = [_sb_load_kv(kst_ref, vst_ref, kb) for kb in kbs]
            zs = [[lax.dot_general(qs[p], kv[u][0][p], NT, preferred_element_type=f32) for p in range(2)] for u in U]
            dws = [[lax.dot_general(dos[p], kv[u][1][p], NT, preferred_element_type=f32) for p in range(2)] for u in U]
            parts = [[_log1m_parts(z) for z in zu] for zu in zs]
            Ls = [[jnp.where(strict2, lt[0], 0.0) if diags[u] else lt[0] for lt in parts[u]] for u in U]
            pins = [[_split_dot(L, upto2) for L in Lu] for Lu in Ls]
            carry = list(carry)
            ws, das = [], []
            for u in U:
                wu, dau = [], []
                for p in range(2):
                    PL0, PL1 = carry[5 * p], carry[5 * p + 1]
                    tail = _two_halves(lts[2 * p] - PL0, lts[2 * p + 1] - PL1) - pins[u][p]
                    w = jnp.exp(zs[u][p] + Ls[u][p] + tail)
                    l0, l1 = _half_sums(Ls[u][p])
                    if diags[u]:
                        w = jnp.where(strict2, w, 0.0)
                    else:
                        w, l0, l1 = (jnp.where(ok[u], t, 0.0) for t in (w, l0, l1))
                    carry[5 * p], carry[5 * p + 1] = PL0 + l0, PL1 + l1
                    wu.append(w)
                    dau.append(w * dws[u][p])
                ws.append(wu)
                das.append(dau)
            pexs = [[_split_dot(da, below2) for da in dau] for dau in das]
            dzs = []
            for u in U:
                dzu = []
                for p in range(2):
                    dL = _two_halves(carry[5 * p + 2], carry[5 * p + 3]) + pexs[u][p]
                    sg = _sigmoid_from(zs[u][p], parts[u][p][1])
                    dz = das[u][p] * (1.0 - sg) - dL * sg
                    dz = jnp.where(strict2 if diags[u] else ok[u], dz, 0.0)
                    a0, a1 = _half_sums(das[u][p])
                    carry[5 * p + 2], carry[5 * p + 3] = carry[5 * p + 2] + a0, carry[5 * p + 3] + a1
                    dzu.append(_mx(dz))
                dzs.append(dzu)
            dqs = [[jnp.dot(dzs[u][p], kv[u][0][p], preferred_element_type=f32) for p in range(2)] for u in U]
            dks = [[lax.dot_general(dzs[u][p], qs[p], TN, preferred_element_type=f32) for p in range(2)] for u in U]
            dvs = [[lax.dot_general(_mx(ws[u][p]), dos[p], TN, preferred_element_type=f32) for p in range(2)] for u in U]
            for u in U:
                krows = pl.ds(pl.multiple_of(kbs[u] * BLK, BLK), BLK)
                for p in range(2):
                    lanes = slice(p * BLK, (p + 1) * BLK)
                    dka_ref[krows, lanes] += jnp.where(head0, dks[u][p][:BLK], dks[u][p][BLK:])
                    dva_ref[krows, lanes] += jnp.where(head0, dvs[u][p][:BLK], dvs[u][p][BLK:])
                    carry[5 * p + 4] = carry[5 * p + 4] + dqs[u][p]
            return tuple(carry)

        def qblock(n, c):
            qrows = pl.ds(pl.multiple_of(n * BLK, BLK), BLK)
            ltb = lt_ref[qrows, :]
            lts = [jnp.sum(jnp.where(ci == h, ltb, 0.0), axis=-1, keepdims=True) for h in range(4)]
            qs = [_mx(q_ref[qrows, p * BLK:(p + 1) * BLK] * (HD ** -0.5)) for p in range(2)]
            dos = [_mx(dy_ref[qrows, p * BLK:(p + 1) * BLK]) for p in range(2)]
            z1, z2 = jnp.zeros((BLK, 1), f32), jnp.zeros((BLK, BLK), f32)
            done = jnp.max(jnp.where(ci == SB_HEADS, ltb, 0.0)).astype(jnp.int32)
            far = jnp.maximum(n - SB_UNROLL, 0)
            first = jnp.maximum(far - SB_UNROLL * done, 0)

            def trip(i, cr):
                kbs = [first + SB_UNROLL * i + u for u in range(SB_UNROLL)]
                return step(qs, dos, lts, kbs, [kb < far for kb in kbs], [False] * SB_UNROLL, n, cr)
            carry = lax.fori_loop(0, (far - first + SB_UNROLL - 1) // SB_UNROLL, trip, (z1, z1, z1, z1, z2) * 2)
            near = [n - SB_UNROLL + u for u in range(SB_UNROLL)]
            res = step(qs, dos, lts, near + [n], [kb >= 0 for kb in near] + [None], [False] * SB_UNROLL + [True], n, carry)
            for p in range(2):
                dq_ref[qrows, p * BLK:(p + 1) * BLK] = (res[5 * p + 4] * (HD ** -0.5)).astype(bf16)
            return c
        lax.fori_loop(0, S // BLK, qblock, 0)
        dk_ref[...] = dka_ref[...].astype(bf16)
        dv_ref[...] = dva_ref[...].astype(bf16)

    spec = lambda j: pl.BlockSpec((S, GW), lambda b: (b, j))
    o = pl.BlockSpec((S, GW), lambda b: (b, 0))
    c_args, c_in, c_out, c_shapes, aliases, c_scratch = _host_specs(comm, 5, 3)
    step = lambda v: (lambda: pl.program_id(0) == v)
    stacked = pltpu.VMEM((2, S // BLK, 2 * BLK, BLK), _MXU)
    return pl.pallas_call(
        _host(body, 5, 3, 4, comm, step(0), step(B - 1)), grid=(B,),
        in_specs=[spec(5), spec(6), spec(7), pl.BlockSpec((S, BLK), lambda b: (b, 0)), o] + c_in,
        out_specs=[o, o, o] + c_out,
        out_shape=[SDS((B * S, GW), bf16)] * 3 + c_shapes,
        input_output_aliases=aliases,
        scratch_shapes=[pltpu.VMEM((S, GW), f32), pltpu.VMEM((S, GW), f32), stacked, stacked] + c_scratch,
        name="sb_bwd" if comm is None else "sb_bwd_exchange",
        compiler_params=_cp("arbitrary"))(proj, proj, proj, ltot, dy, *c_args)


def _bias_expand(rel_bias_t, bucket):
    n = bucket.shape[1]

    def body(r_ref, b_ref, o_ref):
        onehot = (_iota((N_BUCKETS, n), 0) == b_ref[...]).astype(f32)
        o_ref[...] = jnp.dot(r_ref[...], onehot, precision=HIGHEST, preferred_element_type=f32)
    return pl.pallas_call(body, out_shape=SDS((rel_bias_t.shape[0], n), f32), name="bias_expand",
                          compiler_params=_cp())(rel_bias_t, bucket)


def _bias_reduce(dbias, bucket):
    n = bucket.shape[1]

    def body(*refs):
        b_ref, g_ref = refs[-2], refs[-1]
        d = refs[0][...]
        for r in refs[1:-2]:
            d = d + r[...]
        onehot = (_iota((N_BUCKETS, n), 0) == b_ref[...]).astype(f32)
        g_ref[...] = lax.dot_general(d, onehot, NT, precision=HIGHEST, preferred_element_type=f32)
    return pl.pallas_call(body, out_shape=SDS((dbias[0].shape[0], N_BUCKETS), f32), name="bias_reduce",
                          compiler_params=_cp())(*dbias, bucket)


def _adamw(w, g, m, v, tr, name, emit_g=False):
    R, C = w.shape

    def body(w_ref, g_ref, m_ref, v_ref, d_ref, m2_ref, v2_ref, *g_out):
        gv = g_ref[...]
        if emit_g:
            g_out[0][...] = gv
        m2 = ADAM_B1 * m_ref[...] + (1.0 - ADAM_B1) * gv
        v2 = ADAM_B2 * v_ref[...] + (1.0 - ADAM_B2) * (gv * gv)
        m_hat = m2 / (1.0 - ADAM_B1 ** ADAM_STEP)
        v_hat = v2 / (1.0 - ADAM_B2 ** ADAM_STEP)
        d_ref[...] = -ADAM_LR * (m_hat / (jnp.sqrt(v_hat) + ADAM_EPS) + ADAM_WD * w_ref[...])
        m2_ref[...] = m2
        v2_ref[...] = v2

    spec = pl.BlockSpec((tr, C), lambda i: (i, 0))
    n_out = 4 if emit_g else 3
    return pl.pallas_call(
        body, grid=(R // tr,), in_specs=[spec] * 4, out_specs=[spec] * n_out,
        out_shape=[SDS((R, C), f32)] * n_out, name=name, compiler_params=_cp("parallel"))(w, g, m, v)


ANY = pl.BlockSpec(memory_space=pl.ANY)


def _place():
    x, y, c = lax.axis_index("x"), lax.axis_index("y"), lax.axis_index("c")
    chips = [(1 - x, y), (x, 1 - y), (1 - x, 1 - y)]
    return x, y, c, chips


def _cast_slots(w, kidx):
    L, a, b = w.shape
    ta = a // 2

    def body(k_ref, *refs):
        for l in range(L):
            refs[L + l][0] = refs[l][0].astype(bf16)

    return pl.pallas_call(
        body,
        grid_spec=pltpu.PrefetchScalarGridSpec(
            num_scalar_prefetch=1, grid=(a // ta,),
            in_specs=[pl.BlockSpec((1, ta, b), functools.partial(lambda i, k_ref, l: (l, i, 0), l=l)) for l in range(L)],
            out_specs=[pl.BlockSpec((1, ta, b), lambda i, k_ref: (k_ref[0], i, 0)) for _ in range(L)]),
        out_shape=[SDS((N_CHIPS, a, b), bf16)] * L,
        name="cast_slots", compiler_params=_cp("parallel"))(kidx, *([w] * L))


class _GatherComm:
    def __init__(self, bufs):
        self.inputs = list(bufs)
        self.out_shape = [SDS(b.shape, b.dtype) for b in bufs]
        self.aliased = True
        self.scratch = [pltpu.SemaphoreType.DMA((3 * len(bufs),))] * 4

    def _copies(self, i_refs, o_refs, sems):
        send1, recv1, send2, recv2 = sems
        x, y, c, chips = _place()
        k = 2 * x + y
        first, got1, second, got2 = [], [], [], []
        for i, buf in enumerate(self.inputs):
            h = buf.shape[1] // 2
            mine, theirs = pl.ds(c * h, h), pl.ds((1 - c) * h, h)
            for j, (cx, cy) in enumerate(chips):
                s = 3 * i + j
                first.append(pltpu.make_async_remote_copy(
                    src_ref=i_refs[i].at[k, mine], dst_ref=o_refs[i].at[k, mine], send_sem=send1.at[s],
                    recv_sem=recv1.at[s], device_id=(cx, cy, c), device_id_type=MESH))
                a = o_refs[i].at[2 * cx + cy, mine]
                got1.append(pltpu.make_async_remote_copy(
                    src_ref=a, dst_ref=a, send_sem=send1.at[s], recv_sem=recv1.at[s],
                    device_id=(cx, cy, c), device_id_type=MESH))
                second.append(pltpu.make_async_remote_copy(
                    src_ref=a, dst_ref=a, send_sem=send2.at[s], recv_sem=recv2.at[s],
                    device_id=(x, y, 1 - c), device_id_type=MESH))
                b = o_refs[i].at[2 * cx + cy, theirs]
                got2.append(pltpu.make_async_remote_copy(
                    src_ref=b, dst_ref=b, send_sem=send2.at[s], recv_sem=recv2.at[s],
                    device_id=(x, y, 1 - c), device_id_type=MESH))
        return first, got1, second, got2

    def start(self, i_refs, o_refs, sems):
        for cp in self._copies(i_refs, o_refs, sems)[0]:
            cp.start()

    def finish(self, i_refs, o_refs, sems):
        first, got1, second, got2 = self._copies(i_refs, o_refs, sems)
        for g, cp in zip(got1, second):
            g.wait_recv()
            cp.start()
        for g in got2:
            g.wait_recv()
        for cp in first + second:
            cp.wait_send()


class _PairExchangeComm:
    def __init__(self, gs):
        self.inputs = list(gs)
        self.out_shape = [SDS((g.shape[0], g.shape[1] // 2, g.shape[2]), g.dtype) for g in gs]
        self.aliased = False
        self.scratch = [pltpu.SemaphoreType.DMA((len(gs),))] * 2

    def _copies(self, i_refs, o_refs, sems):
        send, recv = sems
        x, y, c, _ = _place()
        cps = []
        for i, g in enumerate(self.inputs):
            h = g.shape[1] // 2
            cps.append(pltpu.make_async_remote_copy(
                src_ref=i_refs[i].at[:, pl.ds((1 - c) * h, h)], dst_ref=o_refs[i], send_sem=send.at[i], recv_sem=recv.at[i],
                device_id=(x, y, 1 - c), device_id_type=MESH))
        return cps

    def start(self, i_refs, o_refs, sems):
        for cp in self._copies(i_refs, o_refs, sems):
            cp.start()

    def finish(self, i_refs, o_refs, sems):
        for cp in self._copies(i_refs, o_refs, sems):
            cp.wait()


class _ChipExchangeComm:
    def __init__(self, qs):
        self.inputs = list(qs)
        self.out_shape = [SDS(q.shape, q.dtype) for q in qs]
        self.aliased = False
        self.scratch = [pltpu.SemaphoreType.DMA((3 * len(qs),))] * 2

    def _copies(self, i_refs, o_refs, sems):
        send, recv = sems
        x, y, c, chips = _place()
        k = 2 * x + y
        cps, got = [], []
        for i in range(len(self.inputs)):
            for j, (cx, cy) in enumerate(chips):
                s = 3 * i + j
                cps.append(pltpu.make_async_remote_copy(
                    src_ref=i_refs[i].at[2 * cx + cy], dst_ref=o_refs[i].at[k], send_sem=send.at[s],
                    recv_sem=recv.at[s], device_id=(cx, cy, c), device_id_type=MESH))
                a = o_refs[i].at[2 * cx + cy]
                got.append(pltpu.make_async_remote_copy(
                    src_ref=a, dst_ref=a, send_sem=send.at[s], recv_sem=recv.at[s],
                    device_id=(cx, cy, c), device_id_type=MESH))
        return cps, got

    def start(self, i_refs, o_refs, sems):
        for cp in self._copies(i_refs, o_refs, sems)[0]:
            cp.start()

    def finish(self, i_refs, o_refs, sems):
        cps, got = self._copies(i_refs, o_refs, sems)
        for g in got:
            g.wait_recv()
        for cp in cps:
            cp.wait_send()


def _comm_only(comm, name):
    n = len(comm.inputs)

    def body(*refs):
        i_refs, o_refs, sems = refs[:n], refs[n:n + len(comm.out_shape)], refs[n + len(comm.out_shape):]
        comm.start(i_refs, o_refs, sems)
        comm.finish(i_refs, o_refs, sems)

    return pl.pallas_call(
        body, out_shape=comm.out_shape, in_specs=[ANY] * n, out_specs=[ANY] * len(comm.out_shape),
        input_output_aliases={i: i for i in range(n)} if comm.aliased else {},
        scratch_shapes=comm.scratch, name=name,
        compiler_params=pltpu.CompilerParams(has_side_effects=True))(*comm.inputs)


def _host(body, n_in, n_out, n_scratch, comm, first, last):
    if comm is None:
        return body
    ci, co = len(comm.inputs), len(comm.out_shape)

    def wrapped(*refs):
        o = 0
        parts = []
        for n in (n_in, ci, n_out, co, n_scratch):
            parts.append(refs[o:o + n])
            o += n
        hin, cin, hout, cout, hs = parts
        sems = refs[o:]

        @pl.when(first())
        def _():
            comm.start(cin, cout, sems)
        body(*hin, *hout, *hs)

        @pl.when(last())
        def _():
            comm.finish(cin, cout, sems)
    return wrapped


def _host_specs(comm, n_in, n_out):
    if comm is None:
        return [], [], [], [], {}, []
    ci, co = len(comm.inputs), len(comm.out_shape)
    aliases = {n_in + i: n_out + i for i in range(ci)} if comm.aliased else {}
    return comm.inputs, [ANY] * ci, [ANY] * co, comm.out_shape, aliases, comm.scratch


def _pair_add(g, r, cidx, name):
    ns, a, b = g.shape
    h = a // 2
    th = h if h * b * 4 <= 4 * 1024 * 1024 else h // 2

    def body(c_ref, g_ref, r_ref, qb_ref):
        qb_ref[...] = (g_ref[...] + r_ref[...]).astype(bf16)

    nb = h // th
    spec = pl.BlockSpec((1, th, b), lambda s, i, c_ref: (s, i, 0))
    return pl.pallas_call(
        body,
        grid_spec=pltpu.PrefetchScalarGridSpec(
            num_scalar_prefetch=1, grid=(ns, nb),
            in_specs=[pl.BlockSpec((1, th, b), lambda s, i, c_ref: (s, c_ref[0] * nb + i, 0)), spec],
            out_specs=spec),
        out_shape=SDS((ns, h, b), bf16),
        name=name, compiler_params=_cp("parallel", "parallel"))(cidx, g, r)


def _chip_add(g, r1, r2, idx, prev, L, name):
    ns, h, b = r2.shape
    th = h if h * b * 4 <= 4 * 1024 * 1024 else h // 2
    nb = h // th

    def body(s_ref, g_ref, r1_ref, a_ref, b_ref, c_ref, *rest):
        o_ref = rest[-1]
        o_ref[0] = (g_ref[0] + r1_ref[0]) + a_ref[0].astype(f32) + b_ref[0].astype(f32) + c_ref[0].astype(f32)

    other = lambda d: pl.BlockSpec((1, th, b), lambda i, s_ref: ((s_ref[0] + d) % ns, i, 0))
    in_specs = [pl.BlockSpec((1, th, b), lambda i, s_ref: (s_ref[0], s_ref[1] * nb + i, 0)),
                pl.BlockSpec((1, th, b), lambda i, s_ref: (s_ref[0], i, 0)), other(1), other(2), other(3)]
    args = [idx, g, r1, r2, r2, r2]
    aliases = {}
    if prev is not None:
        in_specs.append(ANY)
        args.append(prev)
        aliases = {6: 0}
    return pl.pallas_call(
        body,
        grid_spec=pltpu.PrefetchScalarGridSpec(
            num_scalar_prefetch=1, grid=(nb,), in_specs=in_specs,
            out_specs=pl.BlockSpec((1, th, b), lambda i, s_ref: (s_ref[2], s_ref[1] * nb + i, 0))),
        out_shape=SDS((L, 2 * h, b), f32), input_output_aliases=aliases,
        name=name, compiler_params=_cp("arbitrary"))(*args)


def _pair_share(gs, hs):
    n = len(gs)
    L = gs[0].shape[0]

    def body(*refs):
        i_refs, o_refs = refs[:n], refs[n:2 * n]
        send, recv = refs[2 * n:]
        x, y, c, _ = _place()
        cps = []
        for i in range(n):
            for l in range(L):
                mine = pl.ds(c * hs[i], hs[i])
                cp = pltpu.make_async_remote_copy(
                    src_ref=i_refs[i].at[l, mine], dst_ref=o_refs[i].at[l, mine], send_sem=send.at[i * L + l],
                    recv_sem=recv.at[i * L + l], device_id=(x, y, 1 - c), device_id_type=MESH)
                cp.start()
                cps.append(cp)
        for i in range(n):
            for l in range(L):
                got = o_refs[i].at[l, pl.ds((1 - c) * hs[i], hs[i])]
                pltpu.make_async_remote_copy(
                    src_ref=got, dst_ref=got, send_sem=send.at[i * L + l], recv_sem=recv.at[i * L + l],
                    device_id=(x, y, 1 - c), device_id_type=MESH).wait_recv()
        for cp in cps:
            cp.wait_send()

    return pl.pallas_call(
        body, out_shape=[SDS(g.shape, g.dtype) for g in gs], in_specs=[ANY] * n, out_specs=[ANY] * n,
        input_output_aliases={i: i for i in range(n)},
        scratch_shapes=[pltpu.SemaphoreType.DMA((n * L,))] * 2,
        name="grad_pair_share", compiler_params=pltpu.CompilerParams(has_side_effects=True))(*gs)


class _SwapComm:
    def __init__(self, arrays):
        self.inputs = list(arrays)
        self.out_shape = [SDS(a.shape, a.dtype) for a in arrays]
        self.aliased = False
        self.scratch = [pltpu.SemaphoreType.DMA((len(arrays),))] * 2

    def _copies(self, i_refs, o_refs, sems):
        send, recv = sems
        x, y, c, _ = _place()
        return [pltpu.make_async_remote_copy(
            src_ref=i_refs[i], dst_ref=o_refs[i], send_sem=send.at[i], recv_sem=recv.at[i],
            device_id=(x, y, 1 - c), device_id_type=MESH) for i in range(len(self.inputs))]

    def start(self, i_refs, o_refs, sems):
        for cp in self._copies(i_refs, o_refs, sems):
            cp.start()

    def finish(self, i_refs, o_refs, sems):
        for cp in self._copies(i_refs, o_refs, sems):
            cp.wait()


class _SlotShareComm:
    def __init__(self, bufs):
        self.inputs = list(bufs)
        self.out_shape = [SDS(b.shape, b.dtype) for b in bufs]
        self.aliased = True
        self.scratch = [pltpu.SemaphoreType.DMA((3 * len(bufs),))] * 2

    def _copies(self, i_refs, o_refs, sems):
        send, recv = sems
        x, y, c, chips = _place()
        k = 2 * x + y
        cps, got = [], []
        for i in range(len(self.inputs)):
            for j, (cx, cy) in enumerate(chips):
                s = 3 * i + j
                cps.append(pltpu.make_async_remote_copy(
                    src_ref=i_refs[i].at[k], dst_ref=o_refs[i].at[k], send_sem=send.at[s], recv_sem=recv.at[s],
                    device_id=(cx, cy, c), device_id_type=MESH))
                a = o_refs[i].at[2 * cx + cy]
                got.append(pltpu.make_async_remote_copy(
                    src_ref=a, dst_ref=a, send_sem=send.at[s], recv_sem=recv.at[s],
                    device_id=(cx, cy, c), device_id_type=MESH))
        return cps, got

    def start(self, i_refs, o_refs, sems):
        for cp in self._copies(i_refs, o_refs, sems)[0]:
            cp.start()

    def finish(self, i_refs, o_refs, sems):
        cps, got = self._copies(i_refs, o_refs, sems)
        for g in got:
            g.wait_recv()
        for cp in cps:
            cp.wait_send()


def _pair_sum_slot(mine, theirs, kidx, dtype):
    R, C = mine.shape

    def body(k_ref, a_ref, b_ref, o_ref):
        o_ref[0] = (a_ref[...] + b_ref[...]).astype(dtype)

    spec = pl.BlockSpec((R, C), lambda i, k_ref: (0, 0))
    return pl.pallas_call(
        body,
        grid_spec=pltpu.PrefetchScalarGridSpec(
            num_scalar_prefetch=1, grid=(1,), in_specs=[spec, spec],
            out_specs=pl.BlockSpec((1, R, C), lambda i, k_ref: (k_ref[0], 0, 0))),
        out_shape=SDS((N_CHIPS, R, C), dtype), name="small_pair_sum", compiler_params=_cp("arbitrary"))(kidx, mine, theirs)


def _small_sum(g):
    n, R, C = g.shape

    def body(g_ref, o_ref):
        acc = g_ref[0].astype(f32)
        for j in range(1, n):
            acc = acc + g_ref[j].astype(f32)
        o_ref[...] = acc
    return pl.pallas_call(body, out_shape=SDS((R, C), f32), name="small_sum", compiler_params=_cp())(g)


PACK_COLS = 1024


def _rows_of(shape):
    n = int(np.prod(shape)) if len(shape) else 1
    return -(-n // (8 * PACK_COLS)) * 8


def _pack(parts):
    blocks = []
    for p in parts:
        flat = p.reshape(-1)
        r = _rows_of(p.shape)
        blocks.append(jnp.pad(flat, (0, r * PACK_COLS - flat.shape[0])).reshape(r, PACK_COLS))
    return jnp.concatenate(blocks, axis=0)


def _unpack(buf, shapes):
    out, off = [], 0
    for s in shapes:
        n = int(np.prod(s)) if len(s) else 1
        r = _rows_of(s)
        out.append(buf[off:off + r].reshape(-1)[:n].reshape(s))
        off += r
    return out


def _block_diag(w):
    g, a, _ = w.shape
    out = jnp.zeros((g * a, g * a), w.dtype)
    for i in range(g):
        out = lax.dynamic_update_slice(out, w[i], (i * a, i * a))
    return out


def kernel(x, w_in, w_out, sgu_w, sgu_b, pool_w, pool_scale, swa_sinks, rel_bias, mix_out_gain, norm_mix, norm_ffn, w_gate_up, w_down, norm_final, loss_target, m_w_in, m_w_out, m_sgu_w, m_sgu_b, m_pool_w, m_pool_scale, m_swa_sinks, m_rel_bias, m_mix_out_gain, m_norm_mix, m_norm_ffn, m_w_gate_up, m_w_down, m_norm_final, v_w_in, v_w_out, v_sgu_w, v_sgu_b, v_pool_w, v_pool_scale, v_swa_sinks, v_rel_bias, v_mix_out_gain, v_norm_mix, v_norm_ffn, v_w_gate_up, v_w_down, v_norm_final):
    B, S, D = x.shape
    T = B * S
    L = w_in.shape[0]
    tm = min(512, T)
    F = w_down.shape[1] * N_CHIPS
    xi, yi, ci = lax.axis_index("x"), lax.axis_index("y"), lax.axis_index("c")
    cidx = jnp.reshape(ci, (1,)).astype(jnp.int32)
    kidx = jnp.reshape(2 * xi + yi, (1,)).astype(jnp.int32)

    big = [w_in, w_out, w_gate_up, w_down]
    slots = [_cast_slots(w, kidx) for w in big]
    gather = lambda pi, l: _GatherComm([slots[pi][l]])
    Win, Wo, Wgu, Wd = ([None] * L for _ in range(4))
    Win[0], = _comm_only(gather(0, 0), "gather_weights")

    bucket = jnp.asarray(_t5_bucket_table().reshape(1, -1))
    bias_tab = _bias_expand(rel_bias.T, bucket).reshape(4, BLK, 2 * BLK)

    row = lambda v: v.reshape(1, -1)
    xc = x.reshape(T, D)
    tgt = loss_target.reshape(T, D)
    saved = []
    for l in range(L):
        h1, proj, wo = _norm_mm(xc, row(norm_mix[l]), Win[l], tm, gather(1, l))
        Wo[l] = wo.reshape(D, D)
        bexp = jnp.repeat(sgu_b[l].T, HD, axis=1)
        wbd = _block_diag(pool_w[l])
        sk = jnp.broadcast_to(swa_sinks[l][:, None, None], (4, 1, BLK))
        ya = _sgu_fwd(proj, sgu_w[l], bexp, B, S)
        yb = _pool_fwd(proj, wbd, row(pool_scale[l]), B, S)
        if l == 0:
            yc, wd = _swa_fwd(proj, sk, bias_tab, B, S, gather(3, 0))
            yd, lt, Wgu[0] = _sb_fwd(proj, B, S, gather(2, 0))
        else:
            yc, = _swa_fwd(proj, sk, bias_tab, B, S)
            yd, lt, wd = _sb_fwd(proj, B, S, gather(3, l))
        Wd[l] = wd.reshape(F, D)
        ys = (ya, yb, yc, yd)
        ycn, x1 = _gnorm_mm_res(ys, row(mix_out_gain[l]), Wo[l], xc, tm)
        if l + 1 < L:
            h2, gu, act, Wgu[l + 1] = _norm_mm_swiglu(x1, row(norm_ffn[l]), Wgu[l], tm, gather(2, l + 1))
            x2, Win[l + 1] = _mm_res(act, Wd[l], x1, tm, gather(0, l + 1))
        else:
            h2, gu, act = _norm_mm_swiglu(x1, row(norm_ffn[l]), Wgu[l], tm)
            x2, = _mm_res(act, Wd[l], x1, tm)
        saved.append((xc, h1, proj, bexp, wbd, sk, ys, lt, ycn, x1, h2, gu, act))
        xc = x2

    dx, g_final, loss_v = _final_loss(xc, row(norm_final), tgt, tm)

    tk = min(T, 2048)
    gW = [[None] * L for _ in range(4)]
    g_sgu_w, g_sgu_b, g_pool_w, g_pool_scale, g_sinks, g_bias = ([None] * L for _ in range(6))
    g_out_gain, g_mix, g_ffn = ([None] * L for _ in range(3))
    reduced = [None] * 4
    sums = {}

    def pair_comm(keys):
        return _PairExchangeComm([gW[pi][l] for pi, l in keys])

    def after_pair(keys, r1):
        for (pi, l), r in zip(keys, r1):
            sums[pi, l] = (r, _pair_add(gW[pi][l], r, cidx, "grad_pair_add"))

    def chip_comm(keys):
        return _ChipExchangeComm([sums[k][1] for k in keys])

    def after_chip(keys, r2):
        for (pi, l), r in zip(keys, r2):
            idx = jnp.stack([2 * xi + yi, ci, jnp.int32(l)]).astype(jnp.int32)
            reduced[pi] = _chip_add(gW[pi][l], sums.pop((pi, l))[0], r, idx, reduced[pi], L, "grad_chip_add")

    for l in reversed(range(L)):
        x0, h1, proj, bexp, wbd, sk, ys, lt, ycn, x1, h2, gu, act = saved[l]
        if l + 1 < L:
            keys = [(0, l + 1), (1, l + 1)]
            dgu, *r2 = _dact(dx, Wd[l], gu, tm, chip_comm(keys))
            after_chip(keys, r2)
        else:
            dgu, = _dact(dx, Wd[l], gu, tm)
        gW[3][l] = _dw(act, dx, lambda t, s: (t, 0), D, 1, F // 2, tk // 2, "dw_down").reshape(N_CHIPS, F // N_CHIPS, D)
        gW[2][l] = _dw(h2, dgu, lambda t, s: (s // 2, t, s % 2), F // 2, N_CHIPS, D, tk, "dw_gate_up")
        keys = [(2, l), (3, l)]
        dx1, g_ffn[l], *r1 = _dx_norm_bwd(dgu, Wgu[l], x1, row(norm_ffn[l]), dx, tm, "dx_ffn_exchange", pair_comm(keys))
        after_pair(keys, r1)
        gW[1][l] = _dw(ycn, dx1, lambda t, s: (t, 0), D, 1, D, tk, "dw_out").reshape(N_CHIPS, D // N_CHIPS, D)
        dya, dyb, dyc, dyd, g_out_gain[l] = _dycat(dx1, Wo[l], ys, row(mix_out_gain[l]), tm)
        dpa, g_sgu_w[l], dbf = _sgu_bwd(proj, sgu_w[l], bexp, dya, B, S)
        g_sgu_b[l] = dbf[:, ::HD].T
        dpb, dwbd, dsc = _pool_bwd(proj, wbd, row(pool_scale[l]), dyb, B, S)
        npg = len(POOL_WINDOWS)
        g_pool_w[l] = jnp.stack([dwbd[i * HD:(i + 1) * HD, i * HD:(i + 1) * HD] for i in range(npg)])
        g_pool_scale[l] = dsc[0]
        dcq, dckv, dsk, g_bias[l], *r2 = _swa_bwd(proj, sk, bias_tab, dyc, B, S, chip_comm([(3, l)]))
        after_chip([(3, l)], r2)
        g_sinks[l] = dsk[:, 0, 0] * float(BLK)
        ddq, ddk, ddv, *r2 = _sb_bwd(proj, lt, dyd, B, S, chip_comm([(2, l)]))
        after_chip([(2, l)], r2)
        dproj = [dpa, dpb, dcq, dckv, ddq, ddk, ddv]
        gW[0][l] = _dw_pieces(h1, dproj, w_in.shape[2], N_CHIPS, tk, "dw_in")
        keys = [(0, l), (1, l)]
        dx, g_mix[l], *r1 = _dx_norm_bwd(dproj, Win[l], x0, row(norm_mix[l]), dx1, tm, "dx_mix_exchange", pair_comm(keys))
        after_pair(keys, r1)
    grad_x = dx.reshape(B, S, D)

    after_chip(keys, _comm_only(chip_comm(keys), "grad_chip_exchange"))
    g_big = _pair_share(reduced, [g.shape[1] // 2 for g in reduced])

    g_rel_bias = _bias_reduce([g.reshape(4, -1) for g in g_bias], bucket).T
    small_g = [jnp.stack(g_sgu_w), jnp.stack(g_sgu_b), jnp.stack(g_pool_w), jnp.stack(g_pool_scale), jnp.stack(g_sinks),
               g_rel_bias, jnp.concatenate(g_out_gain), jnp.concatenate(g_mix), jnp.concatenate(g_ffn), g_final[0]]
    small_w = [sgu_w, sgu_b, pool_w, pool_scale, swa_sinks, rel_bias, mix_out_gain, norm_mix, norm_ffn, norm_final]
    small_m = [m_sgu_w, m_sgu_b, m_pool_w, m_pool_scale, m_swa_sinks, m_rel_bias, m_mix_out_gain, m_norm_mix, m_norm_ffn, m_norm_final]
    small_v = [v_sgu_w, v_sgu_b, v_pool_w, v_pool_scale, v_swa_sinks, v_rel_bias, v_mix_out_gain, v_norm_mix, v_norm_ffn, v_norm_final]
    shapes = [w.shape for w in small_w]
    bulk, fine = [small_g[0], small_g[2]], [small_g[i] for i in (1, 3, 4, 5, 6, 7, 8, 9)] + [loss_v[0, 0:1]]
    mine = [_pack(bulk), _pack(fine)]
    theirs = _comm_only(_SwapComm(mine), "small_pair_swap")
    slots_s = [_pair_sum_slot(a, b, kidx, dt) for a, b, dt in zip(mine, theirs, (bf16, f32))]
    shared = _comm_only(_SlotShareComm(slots_s), "small_chip_share")
    g_bulk = _unpack(_small_sum(shared[0]), [shapes[0], shapes[2]])
    *g_fine, loss = _unpack(_small_sum(shared[1]), [shapes[i] for i in (1, 3, 4, 5, 6, 7, 8, 9)] + [()])
    g_small = [g_bulk[0], g_fine[0], g_bulk[1]] + g_fine[1:]

    big_m = [m_w_in, m_w_out, m_w_gate_up, m_w_down]
    big_v = [v_w_in, v_w_out, v_w_gate_up, v_w_down]
    g_out, d_big, m_big, v_big = [], [], [], []
    for w, g, m, v in zip(big, g_big, big_m, big_v):
        two = lambda a: a.reshape(-1, a.shape[-1])
        rows = two(w).shape[0]
        cap = max(8, (1 << 20) // (4 * w.shape[-1]))
        tr = max(t for t in range(8, min(rows, cap) + 1, 8) if rows % t == 0)
        d2, m2, v2, g2 = _adamw(two(w), two(g), two(m), two(v), tr, "adamw_big", True)
        for lst, val in ((d_big, d2), (m_big, m2), (v_big, v2), (g_out, g2)):
            lst.append(val.reshape(w.shape))
    g_big = g_out

    g_small_packed = _pack(g_small)
    ds, ms, vs = _adamw(_pack(small_w), g_small_packed, _pack(small_m), _pack(small_v), g_small_packed.shape[0], "adamw_small")
    d_small, m_small, v_small = _unpack(ds, shapes), _unpack(ms, shapes), _unpack(vs, shapes)

    def order(bigs, smalls):
        return [bigs[0], bigs[1]] + list(smalls[0:9]) + [bigs[2], bigs[3], smalls[9]]

    return (loss, grad_x, *order(g_big, g_small), *order(d_big, d_small), *order(m_big, m_small), *order(v_big, v_small))
```

```python
import functools

import numpy as np
import jax
import jax.numpy as jnp
from jax import lax
from jax.experimental import pallas as pl
from jax.experimental.pallas import tpu as pltpu

f32 = jnp.float32
bf16 = jnp.bfloat16
_MXU = jnp.bfloat16

EPS = 1e-6
HD = 64
GW = 256
BLK = 128
SB_UNROLL = 2
SB_HEADS = 4
SB_CUT = -110.0
POOL_WINDOWS = (2, 4, 8, 16)
N_BUCKETS = 32
MAX_DISTANCE = 128
N_CHIPS = 4
N_DEV = 8
VMEM_LIMIT = 48 * 1024 * 1024

ADAM_LR = 0.001
ADAM_B1 = 0.9
ADAM_B2 = 0.999
ADAM_EPS = 1e-08
ADAM_WD = 0.01
ADAM_STEP = 10

SDS = jax.ShapeDtypeStruct
MESH = pl.DeviceIdType.MESH
HIGHEST = lax.Precision.HIGHEST
RESIDENT = pl.Buffered(1)
NT = (((1,), (1,)), ((), ()))
TN = (((0,), (0,)), ((), ()))


def _cp(*sem):
    return pltpu.CompilerParams(dimension_semantics=sem if sem else None, vmem_limit_bytes=VMEM_LIMIT)


def _mx(v):
    return v.astype(_MXU)


def _iota(shape, dim):
    return lax.broadcasted_iota(jnp.int32, shape, dim)


def _split_dot(a, tri):
    hi = a.astype(bf16)
    lo = (a - hi.astype(f32)).astype(bf16)
    return jnp.dot(hi, tri, preferred_element_type=f32) + jnp.dot(lo, tri, preferred_element_type=f32)


def _rms(xv):
    return lax.rsqrt(jnp.mean(xv * xv, axis=-1, keepdims=True) + EPS)


def _hosted_call(body, steps, in_specs, out_specs, out_shape, scratch, args, name, comm):
    n_in, n_out = len(in_specs), len(out_specs)
    c_args, c_in, c_out, c_shapes, aliases, c_scratch = _host_specs(comm, n_in, n_out)
    step = lambda v: (lambda: pl.program_id(0) == v)
    return pl.pallas_call(
        _host(body, n_in, n_out, len(scratch), comm, step(0), step(steps - 1)), grid=(steps,),
        in_specs=list(in_specs) + c_in, out_specs=list(out_specs) + c_out, out_shape=list(out_shape) + c_shapes,
        input_output_aliases=aliases, scratch_shapes=list(scratch) + c_scratch,
        name=name if comm is None else name + "_comm", compiler_params=_cp("arbitrary"))(*args, *c_args)


def _norm_mm(x, gain, w, tm, comm=None):
    T, D = x.shape
    NS, _, ns = w.shape

    def body(x_ref, g_ref, w_ref, h_ref, o_ref):
        xv = x_ref[...]
        h = (xv * _rms(xv) * g_ref[...]).astype(bf16)
        h_ref[...] = h
        for s in range(NS):
            o_ref[:, s * ns:(s + 1) * ns] = jnp.dot(_mx(h), w_ref[s], preferred_element_type=f32).astype(bf16)

    return _hosted_call(
        body, T // tm,
        [pl.BlockSpec((tm, D), lambda i: (i, 0)),
         pl.BlockSpec((1, D), lambda i: (0, 0)),
         pl.BlockSpec((NS, D, ns), lambda i: (0, 0, 0), pipeline_mode=RESIDENT)],
        [pl.BlockSpec((tm, D), lambda i: (i, 0)), pl.BlockSpec((tm, NS * ns), lambda i: (i, 0))],
        [SDS((T, D), bf16), SDS((T, NS * ns), bf16)], [], (x, gain, w), "norm_mm_in", comm)


def _norm_mm_swiglu(x, gain, w, tm, comm=None):
    T, D = x.shape
    NS, _, ns = w.shape
    half = NS // 2

    def body(x_ref, g_ref, w_ref, h_ref, gu_ref, a_ref):
        xv = x_ref[...]
        hb = (xv * _rms(xv) * g_ref[...]).astype(bf16)
        h_ref[...] = hb
        h = _mx(hb)
        for s in range(half):
            cols = slice(s * ns, (s + 1) * ns)
            g = jnp.dot(h, w_ref[s], preferred_element_type=f32)
            u = jnp.dot(h, w_ref[s + half], preferred_element_type=f32)
            gu_ref[0, :, cols] = g.astype(bf16)
            gu_ref[1, :, cols] = u.astype(bf16)
            a_ref[:, cols] = (jax.nn.silu(g) * u).astype(bf16)

    c_args, c_in, c_out, c_shapes, aliases, c_scratch = _host_specs(comm, 3, 3)
    step = lambda v: (lambda: pl.program_id(0) == v)
    return pl.pallas_call(
        _host(body, 3, 3, 0, comm, step(0), step(T // tm - 1)), grid=(T // tm,),
        in_specs=[pl.BlockSpec((tm, D), lambda i: (i, 0)),
                  pl.BlockSpec((1, D), lambda i: (0, 0)),
                  pl.BlockSpec((NS, D, ns), lambda i: (0, 0, 0), pipeline_mode=RESIDENT)] + c_in,
        out_specs=[pl.BlockSpec((tm, D), lambda i: (i, 0)),
                   pl.BlockSpec((2, tm, half * ns), lambda i: (0, i, 0)),
                   pl.BlockSpec((tm, half * ns), lambda i: (i, 0))] + c_out,
        out_shape=[SDS((T, D), bf16), SDS((2, T, half * ns), bf16), SDS((T, half * ns), bf16)] + c_shapes,
        input_output_aliases=aliases, scratch_shapes=c_scratch,
        name="norm_mm_swiglu" if comm is None else "norm_mm_swiglu_gather",
        compiler_params=_cp("arbitrary"))(x, gain, w, *c_args)


def _gnorm_mm_res(ys, gain, w, x, tm, comm=None):
    T, D = x.shape

    def body(ya, yb, yc, yd, g_ref, w_ref, x_ref, yn_ref, o_ref):
        parts = []
        for m, r in enumerate((ya, yb, yc, yd)):
            y = r[...].astype(f32)
            parts.append((y * _rms(y) * g_ref[:, m * GW:(m + 1) * GW]).astype(bf16))
        yn = jnp.concatenate(parts, axis=1)
        yn_ref[...] = yn
        o_ref[...] = x_ref[...] + jnp.dot(_mx(yn), w_ref[...], preferred_element_type=f32)

    yspec = pl.BlockSpec((tm, GW), lambda i: (i, 0))
    return _hosted_call(
        body, T // tm,
        [yspec, yspec, yspec, yspec,
         pl.BlockSpec((1, D), lambda i: (0, 0)),
         pl.BlockSpec((D, D), lambda i: (0, 0), pipeline_mode=RESIDENT),
         pl.BlockSpec((tm, D), lambda i: (i, 0))],
        [pl.BlockSpec((tm, D), lambda i: (i, 0)), pl.BlockSpec((tm, D), lambda i: (i, 0))],
        [SDS((T, D), bf16), SDS((T, D), f32)], [], (*ys, gain, w, x), "gnorm_mm_res", comm)


def _mm_res(a, w, x, tm, comm=None):
    T, D = x.shape
    K = a.shape[1]

    def body(a_ref, w_ref, x_ref, o_ref):
        o_ref[...] = x_ref[...] + jnp.dot(_mx(a_ref[...]), w_ref[...], preferred_element_type=f32)

    return _hosted_call(
        body, T // tm,
        [pl.BlockSpec((tm, K), lambda i: (i, 0)),
         pl.BlockSpec((K, D), lambda i: (0, 0), pipeline_mode=RESIDENT),
         pl.BlockSpec((tm, D), lambda i: (i, 0))],
        [pl.BlockSpec((tm, D), lambda i: (i, 0))], [SDS((T, D), f32)], [], (a, w, x), "mm_res_down", comm)


def _final_loss(x, gain, tgt, tm):
    T, D = x.shape

    def body(x_ref, g_ref, t_ref, dx_ref, dg_ref, l_ref):
        @pl.when(pl.program_id(0) == 0)
        def _():
            dg_ref[...] = jnp.zeros_like(dg_ref)
            l_ref[...] = jnp.zeros_like(l_ref)
        xv = x_ref[...]
        g = g_ref[...]
        r = _rms(xv)
        xh = xv * r
        err = xh * g - t_ref[...]
        l_ref[...] += 0.5 * jnp.sum(jnp.mean(err * err, axis=-1, keepdims=True), axis=0, keepdims=True)
        dy = err * (1.0 / D)
        dg_ref[...] += jnp.sum(dy * xh, axis=0, keepdims=True)
        dxh = dy * g
        dx_ref[...] = r * (dxh - xh * jnp.mean(dxh * xh, axis=-1, keepdims=True))

    return pl.pallas_call(
        body, grid=(T // tm,),
        in_specs=[pl.BlockSpec((tm, D), lambda i: (i, 0)),
                  pl.BlockSpec((1, D), lambda i: (0, 0)),
                  pl.BlockSpec((tm, D), lambda i: (i, 0))],
        out_specs=[pl.BlockSpec((tm, D), lambda i: (i, 0)),
                   pl.BlockSpec((1, D), lambda i: (0, 0)),
                   pl.BlockSpec((1, BLK), lambda i: (0, 0))],
        out_shape=[SDS((T, D), f32), SDS((1, D), f32), SDS((1, BLK), f32)],
        name="final_loss", compiler_params=_cp("arbitrary"))(x, gain, tgt)


def _dact(dx, wd, gu, tm, comm=None):
    T, D = dx.shape
    F = wd.shape[0]
    ns = F // 2

    def body(dx_ref, w_ref, gu_ref, o_ref):
        dxb = _mx(dx_ref[...])
        for s in range(2):
            cols = slice(s * ns, (s + 1) * ns)
            da = lax.dot_general(dxb, w_ref[s * ns:(s + 1) * ns, :], NT, preferred_element_type=f32)
            g = gu_ref[0, :, cols].astype(f32)
            u = gu_ref[1, :, cols].astype(f32)
            sg = jax.nn.sigmoid(g)
            o_ref[0, :, cols] = (da * u * (sg * (1.0 + g * (1.0 - sg)))).astype(bf16)
            o_ref[1, :, cols] = (da * (g * sg)).astype(bf16)

    return _hosted_call(
        body, T // tm,
        [pl.BlockSpec((tm, D), lambda i: (i, 0)),
         pl.BlockSpec((F, D), lambda i: (0, 0), pipeline_mode=RESIDENT),
         pl.BlockSpec((2, tm, F), lambda i: (0, i, 0))],
        [pl.BlockSpec((2, tm, F), lambda i: (0, i, 0))], [SDS((2, T, F), bf16)], [], (dx, wd, gu), "dact", comm)


def _dw(a, b, b_map, ns, NS, tka, tk, name):
    T, Ka = a.shape
    b_block = (tk, ns) if b.ndim == 2 else (1, tk, ns)

    def body(a_ref, b_ref, o_ref):
        bv = b_ref[...] if b.ndim == 2 else b_ref[0]
        part = lax.dot_general(_mx(a_ref[...]), _mx(bv), TN, preferred_element_type=f32)

        @pl.when(pl.program_id(2) == 0)
        def _():
            o_ref[0] = part

        @pl.when(pl.program_id(2) > 0)
        def _():
            o_ref[0] += part

    return pl.pallas_call(
        body, grid=(NS, Ka // tka, T // tk),
        in_specs=[pl.BlockSpec((tk, tka), lambda s, k, t: (t, k)),
                  pl.BlockSpec(b_block, lambda s, k, t: b_map(t, s))],
        out_specs=pl.BlockSpec((1, tka, ns), lambda s, k, t: (s, k, 0)),
        out_shape=SDS((NS, Ka, ns), f32),
        name=name, compiler_params=_cp("parallel", "parallel", "arbitrary"))(a, b)


def _dw_pieces(a, pieces, ns, NS, tk, name):
    T, Ka = a.shape
    n = len(pieces)

    def body(*refs):
        a_ref, b_refs, o_ref = refs[0], refs[1:1 + n], refs[1 + n]
        full = jnp.concatenate([r[...] for r in b_refs], axis=1)
        av = _mx(a_ref[...])
        parts = [lax.dot_general(av, _mx(full[:, s * ns:(s + 1) * ns]), TN, preferred_element_type=f32) for s in range(NS)]

        @pl.when(pl.program_id(0) == 0)
        def _():
            for s in range(NS):
                o_ref[s] = parts[s]

        @pl.when(pl.program_id(0) > 0)
        def _():
            for s in range(NS):
                o_ref[s] += parts[s]

    return pl.pallas_call(
        body, grid=(T // tk,),
        in_specs=[pl.BlockSpec((tk, Ka), lambda t: (t, 0))] + [pl.BlockSpec((tk, p.shape[1]), lambda t: (t, 0)) for p in pieces],
        out_specs=pl.BlockSpec((NS, Ka, ns), lambda t: (0, 0, 0)),
        out_shape=SDS((NS, Ka, ns), f32),
        name=name, compiler_params=_cp("arbitrary"))(a, *pieces)


def _dx_norm_bwd(dy, w, x, gain, dxin, tm, name, comm=None):
    T, D = x.shape
    NS, _, ns = w.shape
    half = NS // 2
    pieces = list(dy) if isinstance(dy, (list, tuple)) else None
    n_dy = len(pieces) if pieces else 1

    def body(*refs):
        dy_refs = refs[:n_dy]
        w_ref, x_ref, g_ref, dxin_ref, dx_ref, dg_ref = refs[n_dy:]

        @pl.when(pl.program_id(0) == 0)
        def _():
            dg_ref[...] = jnp.zeros_like(dg_ref)
        if pieces:
            full = jnp.concatenate([r[...] for r in dy_refs], axis=1)
        dh = None
        for s in range(NS):
            if pieces:
                dv = full[:, s * ns:(s + 1) * ns]
            else:
                dv = dy_refs[0][s // half, :, (s % half) * ns:(s % half + 1) * ns]
            part = lax.dot_general(_mx(dv), w_ref[s], NT, preferred_element_type=f32)
            dh = part if dh is None else dh + part
        xv = x_ref[...]
        r = _rms(xv)
        xh = xv * r
        dg_ref[...] += jnp.sum(dh * xh, axis=0, keepdims=True)
        dxh = dh * g_ref[...]
        dx_ref[...] = dxin_ref[...] + r * (dxh - xh * jnp.mean(dxh * xh, axis=-1, keepdims=True))

    if pieces:
        dy_specs = [pl.BlockSpec((tm, p.shape[1]), lambda i: (i, 0)) for p in pieces]
    else:
        dy_specs = [pl.BlockSpec((2, tm, half * ns), lambda i: (0, i, 0))]
    return _hosted_call(
        body, T // tm,
        dy_specs + [pl.BlockSpec((NS, D, ns), lambda i: (0, 0, 0), pipeline_mode=RESIDENT),
                    pl.BlockSpec((tm, D), lambda i: (i, 0)),
                    pl.BlockSpec((1, D), lambda i: (0, 0)),
                    pl.BlockSpec((tm, D), lambda i: (i, 0))],
        [pl.BlockSpec((tm, D), lambda i: (i, 0)), pl.BlockSpec((1, D), lambda i: (0, 0))],
        [SDS((T, D), f32), SDS((1, D), f32)], [], (*(pieces or [dy]), w, x, gain, dxin), name, comm)


def _dycat(dx, w, ys, gain, tm, comm=None):
    T, D = dx.shape

    def body(dx_ref, w_ref, ya, yb, yc, yd, g_ref, da, db, dc, dd, dg_ref):
        @pl.when(pl.program_id(0) == 0)
        def _():
            dg_ref[...] = jnp.zeros_like(dg_ref)
        dyn = lax.dot_general(_mx(dx_ref[...]), w_ref[...], NT, preferred_element_type=f32)
        for m, (r, o) in enumerate(((ya, da), (yb, db), (yc, dc), (yd, dd))):
            cols = slice(m * GW, (m + 1) * GW)
            y = r[...].astype(f32)
            rs = _rms(y)
            yh = y * rs
            d = dyn[:, cols]
            dg_ref[:, cols] += jnp.sum(d * yh, axis=0, keepdims=True)
            dyh = d * g_ref[:, cols]
            o[...] = (rs * (dyh - yh * jnp.mean(dyh * yh, axis=-1, keepdims=True))).astype(bf16)

    yspec = pl.BlockSpec((tm, GW), lambda i: (i, 0))
    return _hosted_call(
        body, T // tm,
        [pl.BlockSpec((tm, D), lambda i: (i, 0)),
         pl.BlockSpec((D, D), lambda i: (0, 0), pipeline_mode=RESIDENT),
         yspec, yspec, yspec, yspec,
         pl.BlockSpec((1, D), lambda i: (0, 0))],
        [yspec, yspec, yspec, yspec, pl.BlockSpec((1, D), lambda i: (0, 0))],
        [SDS((T, GW), bf16)] * 4 + [SDS((1, D), f32)], [], (dx, w, *ys, gain), "dycat", comm)


def _sgu_consts():
    r, c = _iota((GW, GW), 0), _iota((GW, GW), 1)
    seg = (r // HD == c // HD).astype(f32)
    tr, ts = _iota((BLK, BLK), 0), _iota((BLK, BLK), 1)
    causal = ts <= tr
    lane_head = _iota((BLK, GW), 1) // HD
    return seg, causal, lane_head


def _split3_dot(a, ones):
    hi = a.astype(bf16)
    r1 = a - hi.astype(f32)
    mid = r1.astype(bf16)
    lo = (r1 - mid.astype(f32)).astype(bf16)
    dot = functools.partial(jnp.dot, preferred_element_type=f32)
    return dot(hi, ones) + dot(mid, ones) + dot(lo, ones)


def _sgu_chunks(aus, avs, w, bexp, consts):
    seg, causal, lane_head = consts
    segb = seg.astype(bf16)
    nh = GW // HD
    vs = [jax.nn.gelu(av) for av in avs]
    mus = [_split3_dot(v, segb) * (1.0 / HD) for v in vs]
    vcs = [v - mu for v, mu in zip(vs, mus)]
    vars_ = [_split3_dot(vc * vc, segb) * (1.0 / HD) for vc in vcs]
    vns = [_mx(vc * lax.rsqrt(var + EPS)) for vc, var in zip(vcs, vars_)]
    whs = [_mx(jnp.where(causal, w[h], 0.0)) for h in range(nh)]
    mixes = [[jnp.dot(whs[h], vn, preferred_element_type=f32) for h in range(nh)] for vn in vns]
    out = []
    for au, ms in zip(aus, mixes):
        mix = bexp
        for h in range(nh):
            mix = mix + jnp.where(lane_head == h, ms[h], 0.0)
        out.append(jax.nn.gelu(au) * mix)
    return out


def _sgu_group(S):
    nc = S // BLK
    return 4 if nc % 4 == 0 else (2 if nc % 2 == 0 else 1)


def _sgu_fwd(proj, w, bexp, B, S):
    G = _sgu_group(S)

    def body(au_ref, av_ref, w_ref, b_ref, y_ref):
        consts = _sgu_consts()
        wv, bv = w_ref[...], b_ref[...]

        def group(n, c):
            rows = [pl.ds(pl.multiple_of((n * G + j) * BLK, BLK), BLK) for j in range(G)]
            ys = _sgu_chunks([au_ref[r, :].astype(f32) for r in rows], [av_ref[r, :].astype(f32) for r in rows],
                             wv, bv, consts)
            for r, y in zip(rows, ys):
                y_ref[r, :] = y.astype(bf16)
            return c
        lax.fori_loop(0, S // BLK // G, group, 0)

    return pl.pallas_call(
        body, grid=(B,),
        in_specs=[pl.BlockSpec((S, GW), lambda b: (b, 0)),
                  pl.BlockSpec((S, GW), lambda b: (b, 1)),
                  pl.BlockSpec((GW // HD, BLK, BLK), lambda b: (0, 0, 0)),
                  pl.BlockSpec((BLK, GW), lambda b: (0, 0))],
        out_specs=pl.BlockSpec((S, GW), lambda b: (b, 0)),
        out_shape=SDS((B * S, GW), bf16),
        name="sgu_fwd", compiler_params=_cp("parallel"))(proj, proj, w, bexp)


def _sgu_bwd(proj, w, bexp, dy, B, S, comm=None):
    def body(au_ref, av_ref, w_ref, b_ref, dy_ref, dp_ref, dw_ref, db_ref):
        @pl.when(pl.program_id(0) == 0)
        def _():
            dw_ref[...] = jnp.zeros_like(dw_ref)
            db_ref[...] = jnp.zeros_like(db_ref)
        consts = _sgu_consts()
        wv, bv = w_ref[...], b_ref[...]
        fn = lambda aus, avs, ww, bb: _sgu_chunks(aus, avs, ww, bb, consts)
        G = _sgu_group(S)

        def group(n, carry):
            dw_acc, db_acc = carry
            rows = [pl.ds(pl.multiple_of((n * G + j) * BLK, BLK), BLK) for j in range(G)]
            _, vjp = jax.vjp(fn, [au_ref[r, :].astype(f32) for r in rows], [av_ref[r, :].astype(f32) for r in rows], wv, bv)
            daus, davs, dwc, dbc = vjp([dy_ref[r, :].astype(f32) for r in rows])
            for r, dau, dav in zip(rows, daus, davs):
                dp_ref[r, 0:GW] = dau.astype(bf16)
                dp_ref[r, GW:2 * GW] = dav.astype(bf16)
            return dw_acc + dwc, db_acc + dbc
        dw_acc, db_acc = lax.fori_loop(0, S // BLK // G, group, (jnp.zeros(wv.shape, f32), jnp.zeros(bv.shape, f32)))
        dw_ref[...] += dw_acc
        db_ref[...] += jnp.dot(db_acc, consts[0], precision=HIGHEST, preferred_element_type=f32)

    return _hosted_call(
        body, B,
        [pl.BlockSpec((S, GW), lambda b: (b, 0)),
         pl.BlockSpec((S, GW), lambda b: (b, 1)),
         pl.BlockSpec((GW // HD, BLK, BLK), lambda b: (0, 0, 0)),
         pl.BlockSpec((BLK, GW), lambda b: (0, 0)),
         pl.BlockSpec((S, GW), lambda b: (b, 0))],
        [pl.BlockSpec((S, 2 * GW), lambda b: (b, 0)),
         pl.BlockSpec((GW // HD, BLK, BLK), lambda b: (0, 0, 0)),
         pl.BlockSpec((BLK, GW), lambda b: (0, 0))],
        [SDS((B * S, 2 * GW), bf16), SDS((GW // HD, BLK, BLK), f32), SDS((BLK, GW), f32)], [],
        (proj, proj, w, bexp, dy), "sgu_bwd", comm)


def _pool_parts(p):
    n = p.shape[0]
    r = _iota(p.shape, 0)
    lg = _iota(p.shape, 1) // HD

    def sh(v, k):
        return jnp.where(r >= k, pltpu.roll(v, k, 0), 0.0)
    s2 = p + sh(p, 1)
    s4 = s2 + sh(s2, 2)
    s8 = s4 + sh(s4, 4)
    s16 = s8 + sh(s8, 8)
    ws = jnp.where(lg == 0, s2, jnp.where(lg == 1, s4, jnp.where(lg == 2, s8, s16)))
    wlen = jnp.where(lg == 0, 2, jnp.where(lg == 1, 4, jnp.where(lg == 2, 8, 16)))
    cnt = jnp.minimum(r + 1, wlen).astype(f32)
    del n
    return ws / cnt - p, cnt, lg


def _pool_fwd(proj, wbd, scale, B, S):
    def body(p_ref, w_ref, s_ref, y_ref):
        y, _, _ = _pool_parts(p_ref[...].astype(f32))
        y_ref[...] = (jnp.dot(_mx(y), _mx(w_ref[...]), preferred_element_type=f32) * s_ref[...]).astype(bf16)

    return pl.pallas_call(
        body, grid=(B,),
        in_specs=[pl.BlockSpec((S, GW), lambda b: (b, 2)),
                  pl.BlockSpec((GW, GW), lambda b: (0, 0)),
                  pl.BlockSpec((1, GW), lambda b: (0, 0))],
        out_specs=pl.BlockSpec((S, GW), lambda b: (b, 0)),
        out_shape=SDS((B * S, GW), bf16),
        name="pool_fwd", compiler_params=_cp("parallel"))(proj, wbd, scale)


def _pool_bwd(proj, wbd, scale, dy, B, S):
    def body(p_ref, w_ref, s_ref, dy_ref, dp_ref, dw_ref, ds_ref):
        @pl.when(pl.program_id(0) == 0)
        def _():
            dw_ref[...] = jnp.zeros_like(dw_ref)
            ds_ref[...] = jnp.zeros_like(ds_ref)
        y, cnt, lg = _pool_parts(p_ref[...].astype(f32))
        wv = _mx(w_ref[...])
        z = jnp.dot(_mx(y), wv, preferred_element_type=f32)
        dout = dy_ref[...].astype(f32)
        ds_ref[...] += jnp.sum(dout * z, axis=0, keepdims=True)
        dz = _mx(dout * s_ref[...])
        dw_ref[...] += lax.dot_general(_mx(y), dz, TN, preferred_element_type=f32)
        dyv = lax.dot_general(dz, wv, NT, preferred_element_type=f32)
        n = dyv.shape[0]
        r = _iota(dyv.shape, 0)

        def ush(v, k):
            return jnp.where(r < n - k, pltpu.roll(v, n - k, 0), 0.0)
        gq = dyv / cnt
        a2 = gq + ush(gq, 1)
        a4 = a2 + ush(a2, 2)
        a8 = a4 + ush(a4, 4)
        a16 = a8 + ush(a8, 8)
        adj = jnp.where(lg == 0, a2, jnp.where(lg == 1, a4, jnp.where(lg == 2, a8, a16)))
        dp_ref[...] = (adj - dyv).astype(bf16)

    return pl.pallas_call(
        body, grid=(B,),
        in_specs=[pl.BlockSpec((S, GW), lambda b: (b, 2)),
                  pl.BlockSpec((GW, GW), lambda b: (0, 0)),
                  pl.BlockSpec((1, GW), lambda b: (0, 0)),
                  pl.BlockSpec((S, GW), lambda b: (b, 0))],
        out_specs=[pl.BlockSpec((S, GW), lambda b: (b, 0)),
                   pl.BlockSpec((GW, GW), lambda b: (0, 0)),
                   pl.BlockSpec((1, GW), lambda b: (0, 0))],
        out_shape=[SDS((B * S, GW), bf16), SDS((GW, GW), f32), SDS((1, GW), f32)],
        name="pool_bwd", compiler_params=_cp("arbitrary"))(proj, wbd, scale, dy)


def _t5_bucket_table():
    dist = (np.arange(BLK)[:, None] + BLK) - np.arange(2 * BLK)[None, :]
    d = np.clip(dist, 0, BLK - 1)
    max_exact = N_BUCKETS // 2
    df = np.maximum(d, 1).astype(np.float32)
    large = max_exact + (np.log(df / max_exact) / np.float32(np.log(MAX_DISTANCE / max_exact))
                         * (N_BUCKETS - max_exact)).astype(np.int32)
    large = np.minimum(large, N_BUCKETS - 1)
    return np.where(d < max_exact, d, large).astype(np.int32)


def _swa_blocks(qs, kx, vx, sinks, biases, n):
    G = len(qs)
    heads = [(p, g) for p in range(2) for g in range(2)]
    ri, ci = _iota((BLK, BLK), 0), _iota((BLK, BLK), 1)
    qi, ki = _iota((BLK, 2 * BLK), 0), _iota((BLK, 2 * BLK), 1)
    dist = qi + BLK - ki
    band = (dist >= 0) & (dist < BLK)
    masks = [band & ((ki >= BLK) | (n > 0))] + [band] * (G - 1)
    kb, vb = _mx(kx), _mx(vx)
    qsel = [[None] * 4 for _ in range(G)]
    vs = []
    for h, (p, g) in enumerate(heads):
        selq = ((ri - g * HD == ci - p * HD) & (ri >= g * HD) & (ri < (g + 1) * HD)).astype(_MXU)
        selv = ((ci - g * HD == ri - p * HD) & (ci >= g * HD) & (ci < (g + 1) * HD)).astype(_MXU)
        for b in range(G):
            qsel[b][h] = _mx(jnp.dot(_mx(qs[b][p]), selq, preferred_element_type=f32))
        vs.append(_mx(jnp.dot(vb, selv, preferred_element_type=f32)))
    zs = [[lax.dot_general(qsel[b][h], kb[b * BLK:(b + 2) * BLK], NT, preferred_element_type=f32) * (HD ** -0.5)
           for h in range(4)] for b in range(G)]
    prs = [[None] * 4 for _ in range(G)]
    for b in range(G):
        for h in range(4):
            z = jnp.where(masks[b], zs[b][h] + biases[h], -1e30)
            s = jnp.mean(sinks[h], axis=-1, keepdims=True)
            m = jnp.maximum(jnp.max(z, axis=-1, keepdims=True), s)
            e = jnp.exp(z - m)
            prs[b][h] = _mx(e / (jnp.sum(e, axis=-1, keepdims=True) + jnp.exp(s - m)))
    outs = [[jnp.dot(prs[b][h], vs[h][b * BLK:(b + 2) * BLK], preferred_element_type=f32) for h in range(4)]
            for b in range(G)]
    return [[o[0] + o[1], o[2] + o[3]] for o in outs]


def _swa_group(S):
    return 2 if (S // BLK) % 2 == 0 else 1


def _swa_rows(n, G):
    blk = lambda j: pl.ds(pl.multiple_of(j * BLK, BLK), BLK)
    return [blk(jnp.maximum(n - 1, 0))] + [blk(n + b) for b in range(G)]


def _swa_fwd(proj, sinks, bias, B, S, comm=None):
    G = _swa_group(S)

    def body(q_ref, kv_ref, s_ref, b_ref, y_ref):
        def group(i, c):
            n = i * G
            rows = _swa_rows(n, G)
            kx = jnp.concatenate([kv_ref[r, 0:BLK] for r in rows], axis=0).astype(f32)
            vx = jnp.concatenate([kv_ref[r, BLK:2 * BLK] for r in rows], axis=0).astype(f32)
            qs = [[q_ref[r, 0:BLK].astype(f32), q_ref[r, BLK:2 * BLK].astype(f32)] for r in rows[1:]]
            outs = _swa_blocks(qs, kx, vx, [s_ref[h] for h in range(4)], [b_ref[h] for h in range(4)], n)
            for r, (o0, o1) in zip(rows[1:], outs):
                y_ref[r, 0:BLK] = o0.astype(bf16)
                y_ref[r, BLK:2 * BLK] = o1.astype(bf16)
            return c
        lax.fori_loop(0, S // BLK // G, group, 0)

    return _hosted_call(
        body, B,
        [pl.BlockSpec((S, GW), lambda b: (b, 3)),
         pl.BlockSpec((S, GW), lambda b: (b, 4)),
         pl.BlockSpec((4, 1, BLK), lambda b: (0, 0, 0)),
         pl.BlockSpec((4, BLK, 2 * BLK), lambda b: (0, 0, 0))],
        [pl.BlockSpec((S, GW), lambda b: (b, 0))], [SDS((B * S, GW), bf16)], [],
        (proj, proj, sinks, bias), "swa_fwd", comm)


def _swa_bwd(proj, sinks, bias, dy, B, S, comm=None):
    def body(q_ref, kv_ref, s_ref, b_ref, dy_ref, dq_ref, dkv_ref, ds_ref, db_ref, acc_ref):
        @pl.when(pl.program_id(0) == 0)
        def _():
            ds_ref[...] = jnp.zeros_like(ds_ref)
            db_ref[...] = jnp.zeros_like(db_ref)
        acc_ref[...] = jnp.zeros_like(acc_ref)

        G = _swa_group(S)

        def group(i, c):
            n = i * G
            rows = _swa_rows(n, G)
            kx = jnp.concatenate([kv_ref[r, 0:BLK] for r in rows], axis=0).astype(f32)
            vx = jnp.concatenate([kv_ref[r, BLK:2 * BLK] for r in rows], axis=0).astype(f32)
            qs = [[q_ref[r, 0:BLK].astype(f32), q_ref[r, BLK:2 * BLK].astype(f32)] for r in rows[1:]]
            dos = [[dy_ref[r, 0:BLK].astype(f32), dy_ref[r, BLK:2 * BLK].astype(f32)] for r in rows[1:]]
            fn = functools.partial(_swa_blocks, n=n)
            _, vjp = jax.vjp(fn, qs, kx, vx, [s_ref[h] for h in range(4)], [b_ref[h] for h in range(4)])
            dqs, dkx, dvx, dss, dbs = vjp(dos)
            for r, (dq0, dq1) in zip(rows[1:], dqs):
                dq_ref[r, 0:BLK] = dq0.astype(bf16)
                dq_ref[r, BLK:2 * BLK] = dq1.astype(bf16)
            for h in range(4):
                ds_ref[h] += dss[h]
                db_ref[h] += dbs[h]
            for j, r in enumerate(rows):
                acc_ref[r, 0:BLK] += dkx[j * BLK:(j + 1) * BLK]
                acc_ref[r, BLK:2 * BLK] += dvx[j * BLK:(j + 1) * BLK]
            return c
        lax.fori_loop(0, S // BLK // G, group, 0)
        dkv_ref[...] = acc_ref[...].astype(bf16)

    c_args, c_in, c_out, c_shapes, aliases, c_scratch = _host_specs(comm, 5, 4)
    step = lambda v: (lambda: pl.program_id(0) == v)
    return pl.pallas_call(
        _host(body, 5, 4, 1, comm, step(0), step(B - 1)), grid=(B,),
        in_specs=[pl.BlockSpec((S, GW), lambda b: (b, 3)),
                  pl.BlockSpec((S, GW), lambda b: (b, 4)),
                  pl.BlockSpec((4, 1, BLK), lambda b: (0, 0, 0)),
                  pl.BlockSpec((4, BLK, 2 * BLK), lambda b: (0, 0, 0)),
                  pl.BlockSpec((S, GW), lambda b: (b, 0))] + c_in,
        out_specs=[pl.BlockSpec((S, GW), lambda b: (b, 0)),
                   pl.BlockSpec((S, GW), lambda b: (b, 0)),
                   pl.BlockSpec((4, 1, BLK), lambda b: (0, 0, 0)),
                   pl.BlockSpec((4, BLK, 2 * BLK), lambda b: (0, 0, 0))] + c_out,
        out_shape=[SDS((B * S, GW), bf16), SDS((B * S, GW), bf16), SDS((4, 1, BLK), f32),
                   SDS((4, BLK, 2 * BLK), f32)] + c_shapes,
        input_output_aliases=aliases, scratch_shapes=[pltpu.VMEM((S, GW), f32)] + c_scratch,
        name="swa_bwd" if comm is None else "swa_bwd_exchange",
        compiler_params=_cp("arbitrary"))(proj, proj, sinks, bias, dy, *c_args)


def _log1m_parts(z):
    t = jnp.exp(-jnp.abs(z))
    return jnp.minimum(-z, 0.0) - jnp.log(1.0 + t), t


def _log1m(z):
    return _log1m_parts(z)[0]


def _sigmoid_from(z, t):
    return jnp.where(z >= 0.0, 1.0, t) / (1.0 + t)


def _sb_consts(tri):
    r2, c2 = _iota((2 * BLK, 2 * BLK), 0), _iota((2 * BLK, 2 * BLK), 1)
    tri2 = (tri(r2, c2) & (r2 // BLK == c2 // BLK)).astype(bf16)
    ri, ci = _iota((BLK, 2 * BLK), 0), _iota((BLK, 2 * BLK), 1)
    strict2 = (ci % BLK) < ri
    head0 = _iota((BLK, BLK), 1) < HD
    return tri2, strict2, head0


def _sb_stack_kv(k_ref, v_ref, kst_ref, vst_ref, head0, nb):
    def one(kb, c):
        krows = pl.ds(pl.multiple_of(kb * BLK, BLK), BLK)
        for p in range(2):
            for src, dst in ((k_ref, kst_ref), (v_ref, vst_ref)):
                t = src[krows, p * BLK:(p + 1) * BLK]
                dst[p, kb] = _mx(jnp.concatenate([jnp.where(head0, t, 0.0), jnp.where(head0, 0.0, t)], axis=0))
        return c
    lax.fori_loop(0, nb, one, 0)


def _sb_load_kv(kst_ref, vst_ref, kb):
    return [kst_ref[p, kb] for p in range(2)], [vst_ref[p, kb] for p in range(2)]


def _two_halves(a, b):
    return jnp.concatenate([jnp.broadcast_to(a, (BLK, BLK)), jnp.broadcast_to(b, (BLK, BLK))], axis=1)


def _half_sums(t):
    return jnp.sum(t[:, :BLK], axis=-1, keepdims=True), jnp.sum(t[:, BLK:], axis=-1, keepdims=True)


def _sb_fwd(proj, B, S, comm=None):
    def body(q_ref, k_ref, v_ref, y_ref, lt_ref, kst_ref, vst_ref):
        ci = _iota((BLK, BLK), 1)
        above2, strict2, head0 = _sb_consts(lambda r, c: r > c)
        _sb_stack_kv(k_ref, v_ref, kst_ref, vst_ref, head0, S // BLK)

        def step(qs, kbs, diags, carry):
            U = range(len(kbs))
            ok = [None if diags[u] else kbs[u] >= 0 for u in U]
            kv = [_sb_load_kv(kst_ref, vst_ref, jnp.maximum(kb, 0)) for kb in kbs]
            zs = [[lax.dot_general(qs[p], kks[p], NT, preferred_element_type=f32) for p in range(2)] for kks, _ in kv]
            Ls = [[jnp.where(strict2, _log1m(z), 0.0) if diags[u] else _log1m(z) for z in zs[u]] for u in U]
            tails = [[_split_dot(L, above2) for L in Lu] for Lu in Ls]
            carry = list(carry)
            for u in U:
                for p in range(2):
                    R0, R1, acc = carry[3 * p:3 * p + 3]
                    w = jnp.exp(zs[u][p] + Ls[u][p] + tails[u][p] + _two_halves(R0, R1))
                    s0, s1 = _half_sums(Ls[u][p])
                    if diags[u]:
                        w = jnp.where(strict2, w, 0.0)
                    else:
                        w, s0, s1 = (jnp.where(ok[u], t, 0.0) for t in (w, s0, s1))
                    acc = acc + jnp.dot(_mx(w), kv[u][1][p], preferred_element_type=f32)
                    carry[3 * p:3 * p + 3] = [R0 + s0, R1 + s1, acc]
            return tuple(carry)

        def qblock(n, c):
            qrows = pl.ds(pl.multiple_of(n * BLK, BLK), BLK)
            qs = [_mx(q_ref[qrows, p * BLK:(p + 1) * BLK] * (HD ** -0.5)) for p in range(2)]
            z1, z2 = jnp.zeros((BLK, 1), f32), jnp.zeros((BLK, BLK), f32)
            near = [n - 1 - u for u in range(SB_UNROLL)]
            carry = step(qs, [n] + near, [True] + [False] * SB_UNROLL, (z1, z1, z2, z1, z1, z2))
            far = jnp.maximum(n - SB_UNROLL, 0)
            trips = (far + SB_UNROLL - 1) // SB_UNROLL

            def live(st):
                worst = jnp.maximum(jnp.maximum(st[1], st[2]), jnp.maximum(st[4], st[5]))
                return (st[0] < trips) & (jnp.max(worst) > SB_CUT)

            def trip(st):
                i = st[0]
                kbs = [far - 1 - SB_UNROLL * i - u for u in range(SB_UNROLL)]
                return (i + 1,) + step(qs, kbs, [False] * SB_UNROLL, st[1:])
            done, *res = lax.while_loop(live, trip, (jnp.int32(0),) + carry)
            lt = jnp.where(ci == SB_HEADS, done.astype(f32), 0.0)
            for p in range(2):
                y_ref[qrows, p * BLK:(p + 1) * BLK] = res[3 * p + 2].astype(bf16)
                lt = lt + jnp.where(ci == 2 * p, res[3 * p], 0.0) + jnp.where(ci == 2 * p + 1, res[3 * p + 1], 0.0)
            lt_ref[qrows, :] = lt
            return c
        lax.fori_loop(0, S // BLK, qblock, 0)

    spec = lambda j: pl.BlockSpec((S, GW), lambda b: (b, j))
    c_args, c_in, c_out, c_shapes, aliases, c_scratch = _host_specs(comm, 3, 2)
    step = lambda v: (lambda: pl.program_id(0) == v)
    stacked = pltpu.VMEM((2, S // BLK, 2 * BLK, BLK), _MXU)
    return pl.pallas_call(
        _host(body, 3, 2, 2, comm, step(0), step(B - 1)), grid=(B,),
        in_specs=[spec(5), spec(6), spec(7)] + c_in,
        out_specs=[pl.BlockSpec((S, GW), lambda b: (b, 0)), pl.BlockSpec((S, BLK), lambda b: (b, 0))] + c_out,
        out_shape=[SDS((B * S, GW), bf16), SDS((B * S, BLK), f32)] + c_shapes,
        input_output_aliases=aliases, scratch_shapes=[stacked, stacked] + c_scratch,
        name="sb_fwd" if comm is None else "sb_fwd_gather",
        compiler_params=_cp("arbitrary"))(proj, proj, proj, *c_args)


def _sb_bwd(proj, ltot, dy, B, S, comm=None):
    def body(q_ref, k_ref, v_ref, lt_ref, dy_ref, dq_ref, dk_ref, dv_ref, dka_ref, dva_ref, kst_ref, vst_ref):
        ci = _iota((BLK, BLK), 1)
        upto2, strict2, head0 = _sb_consts(lambda r, c: r <= c)
        below2, _, _ = _sb_consts(lambda r, c: r < c)
        dka_ref[...] = jnp.zeros_like(dka_ref)
        dva_ref[...] = jnp.zeros_like(dva_ref)
        _sb_stack_kv(k_ref, v_ref, kst_ref, vst_ref, head0, S // BLK)

        def step(qs, dos, lts, kbs, ok, diags, top, carry):
            U = range(len(kbs))
            kbs = [jnp.clip(kb, 0, top) for kb in kbs]
            kv = [_sb_load_kv(kst_ref, vst_ref, kb) for kb in kbs]
            zs = [[lax.dot_general(qs[p], kv[u][0][p], NT, preferred_element_type=f32) for p in range(2)] for u in U]
            dws = [[lax.dot_general(dos[p], kv[u][1][p], NT, preferred_element_type=f32) for p in range(2)] for u in U]
            parts = [[_log1m_parts(z) for z in zu] for zu in zs]
            Ls = [[jnp.where(strict2, lt[0], 0.0) if diags[u] else lt[0] for lt in parts[u]] for u in U]
            pins = [[_split_dot(L, upto2) for L in Lu] for Lu in Ls]
            carry = list(carry)
            ws, das = [], []
            for u in U:
                wu, dau = [], []
                for p in range(2):
                    PL0, PL1 = carry[5 * p], carry[5 * p + 1]
                    tail = _two_halves(lts[2 * p] - PL0, lts[2 * p + 1] - PL1) - pins[u][p]
                    w = jnp.exp(zs[u][p] + Ls[u][p] + tail)
                    l0, l1 = _half_sums(Ls[u][p])
                    if diags[u]:
                        w = jnp.where(strict2, w, 0.0)
                    else:
                        w, l0, l1 = (jnp.where(ok[u], t, 0.0) for t in (w, l0, l1))
                    carry[5 * p], carry[5 * p + 1] = PL0 + l0, PL1 + l1
                    wu.append(w)
                    dau.append(w * dws[u][p])
                ws.append(wu)
                das.append(dau)
            pexs = [[_split_dot(da, below2) for da in dau] for dau in das]
            dzs = []
            for u in U:
                dzu = []
                for p in range(2):
                    dL = _two_halves(carry[5 * p + 2], carry[5 * p + 3]) + pexs[u][p]
                    sg = _sigmoid_from(zs[u][p], parts[u][p][1])
                    dz = das[u][p] * (1.0 - sg) - dL * sg
                    dz = jnp.where(strict2 if diags[u] else ok[u], dz, 0.0)
                    a0, a1 = _half_sums(das[u][p])
                    carry[5 * p + 2], carry[5 * p + 3] = carry[5 * p + 2] + a0, carry[5 * p + 3] + a1
                    dzu.append(_mx(dz))
                dzs.append(dzu)
            dqs = [[jnp.dot(dzs[u][p], kv[u][0][p], preferred_element_type=f32) for p in range(2)] for u in U]
            dks = [[lax.dot_general(dzs[u][p], qs[p], TN, preferred_element_type=f32) for p in range(2)] for u in U]
            dvs = [[lax.dot_general(_mx(ws[u][p]), dos[p], TN, preferred_element_type=f32) for p in range(2)] for u in U]
            for u in U:
                krows = pl.ds(pl.multiple_of(kbs[u] * BLK, BLK), BLK)
                for p in range(2):
                    lanes = slice(p * BLK, (p + 1) * BLK)
                    dka_ref[krows, lanes] += jnp.where(head0, dks[u][p][:BLK], dks[u][p][BLK:])
                    dva_ref[krows, lanes] += jnp.where(head0, dvs[u][p][:BLK], dvs[u][p][BLK:])
                    carry[5 * p + 4] = carry[5 * p + 4] + dqs[u][p]
            return tuple(carry)

        def qblock(n, c):
            qrows = pl.ds(pl.multiple_of(n * BLK, BLK), BLK)
            ltb = lt_ref[qrows, :]
            lts = [jnp.sum(jnp.where(ci == h, ltb, 0.0), axis=-1, keepdims=True) for h in range(4)]
            qs = [_mx(q_ref[qrows, p * BLK:(p + 1) * BLK] * (HD ** -0.5)) for p in range(2)]
            dos = [_mx(dy_ref[qrows, p * BLK:(p + 1) * BLK]) for p in range(2)]
            z1, z2 = jnp.zeros((BLK, 1), f32), jnp.zeros((BLK, BLK), f32)
            done = jnp.max(jnp.where(ci == SB_HEADS, ltb, 0.0)).astype(jnp.int32)
            far = jnp.maximum(n - SB_UNROLL, 0)
            first = jnp.maximum(far - SB_UNROLL * done, 0)

            def trip(i, cr):
                kbs = [first + SB_UNROLL * i + u for u in range(SB_UNROLL)]
                return step(qs, dos, lts, kbs, [kb < far for kb in kbs], [False] * SB_UNROLL, n, cr)
            carry = lax.fori_loop(0, (far - first + SB_UNROLL - 1) // SB_UNROLL, trip, (z1, z1, z1, z1, z2) * 2)
            near = [n - SB_UNROLL + u for u in range(SB_UNROLL)]
            res = step(qs, dos, lts, near + [n], [kb >= 0 for kb in near] + [None], [False] * SB_UNROLL + [True], n, carry)
            for p in range(2):
                dq_ref[qrows, p * BLK:(p + 1) * BLK] = (res[5 * p + 4] * (HD ** -0.5)).astype(bf16)
            return c
        lax.fori_loop(0, S // BLK, qblock, 0)
        dk_ref[...] = dka_ref[...].astype(bf16)
        dv_ref[...] = dva_ref[...].astype(bf16)

    spec = lambda j: pl.BlockSpec((S, GW), lambda b: (b, j))
    o = pl.BlockSpec((S, GW), lambda b: (b, 0))
    c_args, c_in, c_out, c_shapes, aliases, c_scratch = _host_specs(comm, 5, 3)
    step = lambda v: (lambda: pl.program_id(0) == v)
    stacked = pltpu.VMEM((2, S // BLK, 2 * BLK, BLK), _MXU)
    return pl.pallas_call(
        _host(body, 5, 3, 4, comm, step(0), step(B - 1)), grid=(B,),
        in_specs=[spec(5), spec(6), spec(7), pl.BlockSpec((S, BLK), lambda b: (b, 0)), o] + c_in,
        out_specs=[o, o, o] + c_out,
        out_shape=[SDS((B * S, GW), bf16)] * 3 + c_shapes,
        input_output_aliases=aliases,
        scratch_shapes=[pltpu.VMEM((S, GW), f32), pltpu.VMEM((S, GW), f32), stacked, stacked] + c_scratch,
        name="sb_bwd" if comm is None else "sb_bwd_exchange",
        compiler_params=_cp("arbitrary"))(proj, proj, proj, ltot, dy, *c_args)


def _bias_expand(rel_bias_t, bucket):
    n = bucket.shape[1]

    def body(r_ref, b_ref, o_ref):
        onehot = (_iota((N_BUCKETS, n), 0) == b_ref[...]).astype(f32)
        o_ref[...] = jnp.dot(r_ref[...], onehot, precision=HIGHEST, preferred_element_type=f32)
    return pl.pallas_call(body, out_shape=SDS((rel_bias_t.shape[0], n), f32), name="bias_expand",
                          compiler_params=_cp())(rel_bias_t, bucket)


def _bias_reduce(dbias, bucket):
    n = bucket.shape[1]

    def body(*refs):
        b_ref, g_ref = refs[-2], refs[-1]
        d = refs[0][...]
        for r in refs[1:-2]:
            d = d + r[...]
        onehot = (_iota((N_BUCKETS, n), 0) == b_ref[...]).astype(f32)
        g_ref[...] = lax.dot_general(d, onehot, NT, precision=HIGHEST, preferred_element_type=f32)
    return pl.pallas_call(body, out_shape=SDS((dbias[0].shape[0], N_BUCKETS), f32), name="bias_reduce",
                          compiler_params=_cp())(*dbias, bucket)


def _adamw(w, g, m, v, tr, name, emit_g=False):
    R, C = w.shape

    def body(w_ref, g_ref, m_ref, v_ref, d_ref, m2_ref, v2_ref, *g_out):
        gv = g_ref[...]
        if emit_g:
            g_out[0][...] = gv
        m2 = ADAM_B1 * m_ref[...] + (1.0 - ADAM_B1) * gv
        v2 = ADAM_B2 * v_ref[...] + (1.0 - ADAM_B2) * (gv * gv)
        m_hat = m2 / (1.0 - ADAM_B1 ** ADAM_STEP)
        v_hat = v2 / (1.0 - ADAM_B2 ** ADAM_STEP)
        d_ref[...] = -ADAM_LR * (m_hat / (jnp.sqrt(v_hat) + ADAM_EPS) + ADAM_WD * w_ref[...])
        m2_ref[...] = m2
        v2_ref[...] = v2

    spec = pl.BlockSpec((tr, C), lambda i: (i, 0))
    n_out = 4 if emit_g else 3
    return pl.pallas_call(
        body, grid=(R // tr,), in_specs=[spec] * 4, out_specs=[spec] * n_out,
        out_shape=[SDS((R, C), f32)] * n_out, name=name, compiler_params=_cp("parallel"))(w, g, m, v)


ANY = pl.BlockSpec(memory_space=pl.ANY)


def _place():
    x, y, c = lax.axis_index("x"), lax.axis_index("y"), lax.axis_index("c")
    chips = [(1 - x, y), (x, 1 - y), (1 - x, 1 - y)]
    return x, y, c, chips


def _cast_slots(w, kidx):
    L, a, b = w.shape
    ta = a // 2

    def body(k_ref, *refs):
        for l in range(L):
            refs[L + l][0] = refs[l][0].astype(bf16)

    return pl.pallas_call(
        body,
        grid_spec=pltpu.PrefetchScalarGridSpec(
            num_scalar_prefetch=1, grid=(a // ta,),
            in_specs=[pl.BlockSpec((1, ta, b), functools.partial(lambda i, k_ref, l: (l, i, 0), l=l)) for l in range(L)],
            out_specs=[pl.BlockSpec((1, ta, b), lambda i, k_ref: (k_ref[0], i, 0)) for _ in range(L)]),
        out_shape=[SDS((N_CHIPS, a, b), bf16)] * L,
        name="cast_slots", compiler_params=_cp("parallel"))(kidx, *([w] * L))


class _GatherComm:
    def __init__(self, bufs, part=0, parts=1):
        self.inputs = list(bufs)
        self.out_shape = [SDS(b.shape, b.dtype) for b in bufs]
        self.aliased = True
        self.scratch = [pltpu.SemaphoreType.DMA((3 * len(bufs),))] * 4
        self.part, self.parts = part, parts

    def _copies(self, i_refs, o_refs, sems):
        send1, recv1, send2, recv2 = sems
        x, y, c, chips = _place()
        k = 2 * x + y
        first, got1, second, got2 = [], [], [], []
        for i, buf in enumerate(self.inputs):
            h = buf.shape[1] // 2
            n = h // self.parts
            mine, theirs = pl.ds(c * h + self.part * n, n), pl.ds((1 - c) * h + self.part * n, n)
            for j, (cx, cy) in enumerate(chips):
                s = 3 * i + j
                first.append(pltpu.make_async_remote_copy(
                    src_ref=i_refs[i].at[k, mine], dst_ref=o_refs[i].at[k, mine], send_sem=send1.at[s],
                    recv_sem=recv1.at[s], device_id=(cx, cy, c), device_id_type=MESH))
                a = o_refs[i].at[2 * cx + cy, mine]
                got1.append(pltpu.make_async_remote_copy(
                    src_ref=a, dst_ref=a, send_sem=send1.at[s], recv_sem=recv1.at[s],
                    device_id=(cx, cy, c), device_id_type=MESH))
                second.append(pltpu.make_async_remote_copy(
                    src_ref=a, dst_ref=a, send_sem=send2.at[s], recv_sem=recv2.at[s],
                    device_id=(x, y, 1 - c), device_id_type=MESH))
                b = o_refs[i].at[2 * cx + cy, theirs]
                got2.append(pltpu.make_async_remote_copy(
                    src_ref=b, dst_ref=b, send_sem=send2.at[s], recv_sem=recv2.at[s],
                    device_id=(x, y, 1 - c), device_id_type=MESH))
        return first, got1, second, got2

    def start(self, i_refs, o_refs, sems):
        for cp in self._copies(i_refs, o_refs, sems)[0]:
            cp.start()

    def finish(self, i_refs, o_refs, sems):
        first, got1, second, got2 = self._copies(i_refs, o_refs, sems)
        for g, cp in zip(got1, second):
            g.wait_recv()
            cp.start()
        for g in got2:
            g.wait_recv()
        for cp in first + second:
            cp.wait_send()


class _MultiComm:
    def __init__(self, comms):
        self.comms = comms
        self.inputs = [a for c in comms for a in c.inputs]
        self.out_shape = [s for c in comms for s in c.out_shape]
        self.aliased = comms[0].aliased
        assert all(c.aliased == self.aliased for c in comms)
        self.scratch = [s for c in comms for s in c.scratch]

    def _split(self, i_refs, o_refs, sems):
        i = o = s = 0
        for c in self.comms:
            ni, no, ns = len(c.inputs), len(c.out_shape), len(c.scratch)
            yield c, i_refs[i:i + ni], o_refs[o:o + no], sems[s:s + ns]
            i, o, s = i + ni, o + no, s + ns

    def start(self, i_refs, o_refs, sems):
        for c, i, o, s in self._split(i_refs, o_refs, sems):
            c.start(i, o, s)

    def finish(self, i_refs, o_refs, sems):
        for c, i, o, s in self._split(i_refs, o_refs, sems):
            c.finish(i, o, s)


class _PairExchangeComm:
    def __init__(self, gs):
        self.inputs = list(gs)
        self.out_shape = [SDS((g.shape[0], g.shape[1] // 2, g.shape[2]), g.dtype) for g in gs]
        self.aliased = False
        self.scratch = [pltpu.SemaphoreType.DMA((len(gs),))] * 2

    def _copies(self, i_refs, o_refs, sems):
        send, recv = sems
        x, y, c, _ = _place()
        cps = []
        for i, g in enumerate(self.inputs):
            h = g.shape[1] // 2
            cps.append(pltpu.make_async_remote_copy(
                src_ref=i_refs[i].at[:, pl.ds((1 - c) * h, h)], dst_ref=o_refs[i], send_sem=send.at[i], recv_sem=recv.at[i],
                device_id=(x, y, 1 - c), device_id_type=MESH))
        return cps

    def start(self, i_refs, o_refs, sems):
        for cp in self._copies(i_refs, o_refs, sems):
            cp.start()

    def finish(self, i_refs, o_refs, sems):
        for cp in self._copies(i_refs, o_refs, sems):
            cp.wait()


class _ChipExchangeComm:
    def __init__(self, qs):
        self.inputs = list(qs)
        self.out_shape = [SDS(q.shape, q.dtype) for q in qs]
        self.aliased = False
        self.scratch = [pltpu.SemaphoreType.DMA((3 * len(qs),))] * 2

    def _copies(self, i_refs, o_refs, sems):
        send, recv = sems
        x, y, c, chips = _place()
        k = 2 * x + y
        cps, got = [], []
        for i in range(len(self.inputs)):
            for j, (cx, cy) in enumerate(chips):
                s = 3 * i + j
                cps.append(pltpu.make_async_remote_copy(
                    src_ref=i_refs[i].at[2 * cx + cy], dst_ref=o_refs[i].at[k], send_sem=send.at[s],
                    recv_sem=recv.at[s], device_id=(cx, cy, c), device_id_type=MESH))
                a = o_refs[i].at[2 * cx + cy]
                got.append(pltpu.make_async_remote_copy(
                    src_ref=a, dst_ref=a, send_sem=send.at[s], recv_sem=recv.at[s],
                    device_id=(cx, cy, c), device_id_type=MESH))
        return cps, got

    def start(self, i_refs, o_refs, sems):
        for cp in self._copies(i_refs, o_refs, sems)[0]:
            cp.start()

    def finish(self, i_refs, o_refs, sems):
        cps, got = self._copies(i_refs, o_refs, sems)
        for g in got:
            g.wait_recv()
        for cp in cps:
            cp.wait_send()


def _comm_only(comm, name):
    n = len(comm.inputs)

    def body(*refs):
        i_refs, o_refs, sems = refs[:n], refs[n:n + len(comm.out_shape)], refs[n + len(comm.out_shape):]
        comm.start(i_refs, o_refs, sems)
        comm.finish(i_refs, o_refs, sems)

    return pl.pallas_call(
        body, out_shape=comm.out_shape, in_specs=[ANY] * n, out_specs=[ANY] * len(comm.out_shape),
        input_output_aliases={i: i for i in range(n)} if comm.aliased else {},
        scratch_shapes=comm.scratch, name=name,
        compiler_params=pltpu.CompilerParams(has_side_effects=True))(*comm.inputs)


def _host(body, n_in, n_out, n_scratch, comm, first, last):
    if comm is None:
        return body
    ci, co = len(comm.inputs), len(comm.out_shape)

    def wrapped(*refs):
        o = 0
        parts = []
        for n in (n_in, ci, n_out, co, n_scratch):
            parts.append(refs[o:o + n])
            o += n
        hin, cin, hout, cout, hs = parts
        sems = refs[o:]

        @pl.when(first())
        def _():
            comm.start(cin, cout, sems)
        body(*hin, *hout, *hs)

        @pl.when(last())
        def _():
            comm.finish(cin, cout, sems)
    return wrapped


def _host_specs(comm, n_in, n_out):
    if comm is None:
        return [], [], [], [], {}, []
    ci, co = len(comm.inputs), len(comm.out_shape)
    aliases = {n_in + i: n_out + i for i in range(ci)} if comm.aliased else {}
    return comm.inputs, [ANY] * ci, [ANY] * co, comm.out_shape, aliases, comm.scratch


def _pair_add(g, r, cidx, name):
    ns, a, b = g.shape
    h = a // 2
    th = h if h * b * 4 <= 4 * 1024 * 1024 else h // 2

    def body(c_ref, g_ref, r_ref, qb_ref):
        qb_ref[...] = (g_ref[...] + r_ref[...]).astype(bf16)

    nb = h // th
    spec = pl.BlockSpec((1, th, b), lambda s, i, c_ref: (s, i, 0))
    return pl.pallas_call(
        body,
        grid_spec=pltpu.PrefetchScalarGridSpec(
            num_scalar_prefetch=1, grid=(ns, nb),
            in_specs=[pl.BlockSpec((1, th, b), lambda s, i, c_ref: (s, c_ref[0] * nb + i, 0)), spec],
            out_specs=spec),
        out_shape=SDS((ns, h, b), bf16),
        name=name, compiler_params=_cp("parallel", "parallel"))(cidx, g, r)


def _chip_add(g, r1, r2, idx, prev, L, name):
    ns, h, b = r2.shape
    th = h if h * b * 4 <= 4 * 1024 * 1024 else h // 2
    nb = h // th

    def body(s_ref, g_ref, r1_ref, a_ref, b_ref, c_ref, *rest):
        o_ref = rest[-1]
        o_ref[0] = (g_ref[0] + r1_ref[0]) + a_ref[0].astype(f32) + b_ref[0].astype(f32) + c_ref[0].astype(f32)

    other = lambda d: pl.BlockSpec((1, th, b), lambda i, s_ref: ((s_ref[0] + d) % ns, i, 0))
    in_specs = [pl.BlockSpec((1, th, b), lambda i, s_ref: (s_ref[0], s_ref[1] * nb + i, 0)),
                pl.BlockSpec((1, th, b), lambda i, s_ref: (s_ref[0], i, 0)), other(1), other(2), other(3)]
    args = [idx, g, r1, r2, r2, r2]
    aliases = {}
    if prev is not None:
        in_specs.append(ANY)
        args.append(prev)
        aliases = {6: 0}
    return pl.pallas_call(
        body,
        grid_spec=pltpu.PrefetchScalarGridSpec(
            num_scalar_prefetch=1, grid=(nb,), in_specs=in_specs,
            out_specs=pl.BlockSpec((1, th, b), lambda i, s_ref: (s_ref[2], s_ref[1] * nb + i, 0))),
        out_shape=SDS((L, 2 * h, b), f32), input_output_aliases=aliases,
        name=name, compiler_params=_cp("arbitrary"))(*args)


def _pair_share(gs, hs):
    n = len(gs)
    L = gs[0].shape[0]

    def body(*refs):
        i_refs, o_refs = refs[:n], refs[n:2 * n]
        send, recv = refs[2 * n:]
        x, y, c, _ = _place()
        cps = []
        for i in range(n):
            for l in range(L):
                mine = pl.ds(c * hs[i], hs[i])
                cp = pltpu.make_async_remote_copy(
                    src_ref=i_refs[i].at[l, mine], dst_ref=o_refs[i].at[l, mine], send_sem=send.at[i * L + l],
                    recv_sem=recv.at[i * L + l], device_id=(x, y, 1 - c), device_id_type=MESH)
                cp.start()
                cps.append(cp)
        for i in range(n):
            for l in range(L):
                got = o_refs[i].at[l, pl.ds((1 - c) * hs[i], hs[i])]
                pltpu.make_async_remote_copy(
                    src_ref=got, dst_ref=got, send_sem=send.at[i * L + l], recv_sem=recv.at[i * L + l],
                    device_id=(x, y, 1 - c), device_id_type=MESH).wait_recv()
        for cp in cps:
            cp.wait_send()

    return pl.pallas_call(
        body, out_shape=[SDS(g.shape, g.dtype) for g in gs], in_specs=[ANY] * n, out_specs=[ANY] * n,
        input_output_aliases={i: i for i in range(n)},
        scratch_shapes=[pltpu.SemaphoreType.DMA((n * L,))] * 2,
        name="grad_pair_share", compiler_params=pltpu.CompilerParams(has_side_effects=True))(*gs)


class _SwapComm:
    def __init__(self, arrays):
        self.inputs = list(arrays)
        self.out_shape = [SDS(a.shape, a.dtype) for a in arrays]
        self.aliased = False
        self.scratch = [pltpu.SemaphoreType.DMA((len(arrays),))] * 2

    def _copies(self, i_refs, o_refs, sems):
        send, recv = sems
        x, y, c, _ = _place()
        return [pltpu.make_async_remote_copy(
            src_ref=i_refs[i], dst_ref=o_refs[i], send_sem=send.at[i], recv_sem=recv.at[i],
            device_id=(x, y, 1 - c), device_id_type=MESH) for i in range(len(self.inputs))]

    def start(self, i_refs, o_refs, sems):
        for cp in self._copies(i_refs, o_refs, sems):
            cp.start()

    def finish(self, i_refs, o_refs, sems):
        for cp in self._copies(i_refs, o_refs, sems):
            cp.wait()


class _SlotShareComm:
    def __init__(self, bufs):
        self.inputs = list(bufs)
        self.out_shape = [SDS(b.shape, b.dtype) for b in bufs]
        self.aliased = True
        self.scratch = [pltpu.SemaphoreType.DMA((3 * len(bufs),))] * 2

    def _copies(self, i_refs, o_refs, sems):
        send, recv = sems
        x, y, c, chips = _place()
        k = 2 * x + y
        cps, got = [], []
        for i in range(len(self.inputs)):
            for j, (cx, cy) in enumerate(chips):
                s = 3 * i + j
                cps.append(pltpu.make_async_remote_copy(
                    src_ref=i_refs[i].at[k], dst_ref=o_refs[i].at[k], send_sem=send.at[s], recv_sem=recv.at[s],
                    device_id=(cx, cy, c), device_id_type=MESH))
                a = o_refs[i].at[2 * cx + cy]
                got.append(pltpu.make_async_remote_copy(
                    src_ref=a, dst_ref=a, send_sem=send.at[s], recv_sem=recv.at[s],
                    device_id=(cx, cy, c), device_id_type=MESH))
        return cps, got

    def start(self, i_refs, o_refs, sems):
        for cp in self._copies(i_refs, o_refs, sems)[0]:
            cp.start()

    def finish(self, i_refs, o_refs, sems):
        cps, got = self._copies(i_refs, o_refs, sems)
        for g in got:
            g.wait_recv()
        for cp in cps:
            cp.wait_send()


def _pair_sum_slot(mine, theirs, kidx, dtype):
    R, C = mine.shape

    def body(k_ref, a_ref, b_ref, o_ref):
        o_ref[0] = (a_ref[...] + b_ref[...]).astype(dtype)

    spec = pl.BlockSpec((R, C), lambda i, k_ref: (0, 0))
    return pl.pallas_call(
        body,
        grid_spec=pltpu.PrefetchScalarGridSpec(
            num_scalar_prefetch=1, grid=(1,), in_specs=[spec, spec],
            out_specs=pl.BlockSpec((1, R, C), lambda i, k_ref: (k_ref[0], 0, 0))),
        out_shape=SDS((N_CHIPS, R, C), dtype), name="small_pair_sum", compiler_params=_cp("arbitrary"))(kidx, mine, theirs)


def _small_sum(g):
    n, R, C = g.shape

    def body(g_ref, o_ref):
        acc = g_ref[0].astype(f32)
        for j in range(1, n):
            acc = acc + g_ref[j].astype(f32)
        o_ref[...] = acc
    return pl.pallas_call(body, out_shape=SDS((R, C), f32), name="small_sum", compiler_params=_cp())(g)


PACK_COLS = 1024


def _rows_of(shape):
    n = int(np.prod(shape)) if len(shape) else 1
    return -(-n // (8 * PACK_COLS)) * 8


def _pack(parts):
    blocks = []
    for p in parts:
        flat = p.reshape(-1)
        r = _rows_of(p.shape)
        blocks.append(jnp.pad(flat, (0, r * PACK_COLS - flat.shape[0])).reshape(r, PACK_COLS))
    return jnp.concatenate(blocks, axis=0)


def _unpack(buf, shapes):
    out, off = [], 0
    for s in shapes:
        n = int(np.prod(s)) if len(s) else 1
        r = _rows_of(s)
        out.append(buf[off:off + r].reshape(-1)[:n].reshape(s))
        off += r
    return out


def _block_diag(w):
    g, a, _ = w.shape
    out = jnp.zeros((g * a, g * a), w.dtype)
    for i in range(g):
        out = lax.dynamic_update_slice(out, w[i], (i * a, i * a))
    return out


def kernel(x, w_in, w_out, sgu_w, sgu_b, pool_w, pool_scale, swa_sinks, rel_bias, mix_out_gain, norm_mix, norm_ffn, w_gate_up, w_down, norm_final, loss_target, m_w_in, m_w_out, m_sgu_w, m_sgu_b, m_pool_w, m_pool_scale, m_swa_sinks, m_rel_bias, m_mix_out_gain, m_norm_mix, m_norm_ffn, m_w_gate_up, m_w_down, m_norm_final, v_w_in, v_w_out, v_sgu_w, v_sgu_b, v_pool_w, v_pool_scale, v_swa_sinks, v_rel_bias, v_mix_out_gain, v_norm_mix, v_norm_ffn, v_w_gate_up, v_w_down, v_norm_final):
    B, S, D = x.shape
    T = B * S
    L = w_in.shape[0]
    tm = min(512, T)
    F = w_down.shape[1] * N_CHIPS
    xi, yi, ci = lax.axis_index("x"), lax.axis_index("y"), lax.axis_index("c")
    cidx = jnp.reshape(ci, (1,)).astype(jnp.int32)
    kidx = jnp.reshape(2 * xi + yi, (1,)).astype(jnp.int32)

    big = [w_in, w_out, w_gate_up, w_down]
    slots = [_cast_slots(w, kidx) for w in big]
    gather = lambda pi, l, part=0, parts=1: _GatherComm([slots[pi][l]], part, parts)
    Win, Wo, Wgu, Wd = ([None] * L for _ in range(4))
    Win[0], = _comm_only(gather(0, 0), "gather_weights")

    bucket = jnp.asarray(_t5_bucket_table().reshape(1, -1))
    bias_tab = _bias_expand(rel_bias.T, bucket).reshape(4, BLK, 2 * BLK)

    row = lambda v: v.reshape(1, -1)
    xc = x.reshape(T, D)
    tgt = loss_target.reshape(T, D)
    saved = []
    for l in range(L):
        bexp = jnp.repeat(sgu_b[l].T, HD, axis=1)
        wbd = _block_diag(pool_w[l])
        sk = jnp.broadcast_to(swa_sinks[l][:, None, None], (4, 1, BLK))
        h1, proj, wo = _norm_mm(xc, row(norm_mix[l]), Win[l], tm, gather(1, l))
        ya = _sgu_fwd(proj, sgu_w[l], bexp, B, S)
        yb = _pool_fwd(proj, wbd, row(pool_scale[l]), B, S)
        yc, wd = _swa_fwd(proj, sk, bias_tab, B, S, gather(3, l, 0, 2))
        if l == 0:
            yd, lt, wd, wg = _sb_fwd(proj, B, S, _MultiComm([_GatherComm([wd], 1, 2), gather(2, 0, 0, 2)]))
        else:
            yd, lt, wd = _sb_fwd(proj, B, S, _GatherComm([wd], 1, 2))
        Wo[l], Wd[l] = wo.reshape(D, D), wd.reshape(F, D)
        ys = (ya, yb, yc, yd)
        if l == 0:
            ycn, x1, Wgu[0] = _gnorm_mm_res(ys, row(mix_out_gain[0]), Wo[0], xc, tm, _GatherComm([wg], 1, 2))
        else:
            ycn, x1 = _gnorm_mm_res(ys, row(mix_out_gain[l]), Wo[l], xc, tm)
        if l + 1 < L:
            h2, gu, act, Wgu[l + 1] = _norm_mm_swiglu(x1, row(norm_ffn[l]), Wgu[l], tm, gather(2, l + 1))
            x2, Win[l + 1] = _mm_res(act, Wd[l], x1, tm, gather(0, l + 1))
        else:
            h2, gu, act = _norm_mm_swiglu(x1, row(norm_ffn[l]), Wgu[l], tm)
            x2, = _mm_res(act, Wd[l], x1, tm)
        saved.append((xc, h1, proj, bexp, wbd, sk, ys, lt, ycn, x1, h2, gu, act))
        xc = x2

    dx, g_final, loss_v = _final_loss(xc, row(norm_final), tgt, tm)

    tk = min(T, 2048)
    gW = [[None] * L for _ in range(4)]
    g_sgu_w, g_sgu_b, g_pool_w, g_pool_scale, g_sinks, g_bias = ([None] * L for _ in range(6))
    g_out_gain, g_mix, g_ffn = ([None] * L for _ in range(3))
    reduced = [None] * 4
    sums = {}

    def pair_comm(keys):
        return _PairExchangeComm([gW[pi][l] for pi, l in keys])

    def after_pair(keys, r1):
        for (pi, l), r in zip(keys, r1):
            sums[pi, l] = (r, _pair_add(gW[pi][l], r, cidx, "grad_pair_add"))

    def chip_comm(keys):
        return _ChipExchangeComm([sums[k][1] for k in keys])

    def after_chip(keys, r2):
        for (pi, l), r in zip(keys, r2):
            idx = jnp.stack([2 * xi + yi, ci, jnp.int32(l)]).astype(jnp.int32)
            reduced[pi] = _chip_add(gW[pi][l], sums.pop((pi, l))[0], r, idx, reduced[pi], L, "grad_chip_add")

    for l in reversed(range(L)):
        x0, h1, proj, bexp, wbd, sk, ys, lt, ycn, x1, h2, gu, act = saved[l]
        if l + 1 < L:
            keys = [(0, l + 1), (1, l + 1)]
            dgu, *r2 = _dact(dx, Wd[l], gu, tm, chip_comm(keys))
            after_chip(keys, r2)
        else:
            dgu, = _dact(dx, Wd[l], gu, tm)
        gW[3][l] = _dw(act, dx, lambda t, s: (t, 0), D, 1, F // 2, tk // 2, "dw_down").reshape(N_CHIPS, F // N_CHIPS, D)
        gW[2][l] = _dw(h2, dgu, lambda t, s: (s // 2, t, s % 2), F // 2, N_CHIPS, D, tk, "dw_gate_up")
        keys = [(2, l), (3, l)]
        dx1, g_ffn[l], *r1 = _dx_norm_bwd(dgu, Wgu[l], x1, row(norm_ffn[l]), dx, tm, "dx_ffn_exchange", pair_comm(keys))
        after_pair(keys, r1)
        gW[1][l] = _dw(ycn, dx1, lambda t, s: (t, 0), D, 1, D, tk, "dw_out").reshape(N_CHIPS, D // N_CHIPS, D)
        dya, dyb, dyc, dyd, g_out_gain[l] = _dycat(dx1, Wo[l], ys, row(mix_out_gain[l]), tm)
        dpa, g_sgu_w[l], dbf = _sgu_bwd(proj, sgu_w[l], bexp, dya, B, S)
        g_sgu_b[l] = dbf[:, ::HD].T
        dpb, dwbd, dsc = _pool_bwd(proj, wbd, row(pool_scale[l]), dyb, B, S)
        npg = len(POOL_WINDOWS)
        g_pool_w[l] = jnp.stack([dwbd[i * HD:(i + 1) * HD, i * HD:(i + 1) * HD] for i in range(npg)])
        g_pool_scale[l] = dsc[0]
        dcq, dckv, dsk, g_bias[l], *r2 = _swa_bwd(proj, sk, bias_tab, dyc, B, S, chip_comm([(3, l)]))
        after_chip([(3, l)], r2)
        g_sinks[l] = dsk[:, 0, 0] * float(BLK)
        ddq, ddk, ddv, *r2 = _sb_bwd(proj, lt, dyd, B, S, chip_comm([(2, l)]))
        after_chip([(2, l)], r2)
        dproj = [dpa, dpb, dcq, dckv, ddq, ddk, ddv]
        gW[0][l] = _dw_pieces(h1, dproj, w_in.shape[2], N_CHIPS, tk, "dw_in")
        keys = [(0, l), (1, l)]
        dx, g_mix[l], *r1 = _dx_norm_bwd(dproj, Win[l], x0, row(norm_mix[l]), dx1, tm, "dx_mix_exchange", pair_comm(keys))
        after_pair(keys, r1)
    grad_x = dx.reshape(B, S, D)

    after_chip(keys, _comm_only(chip_comm(keys), "grad_chip_exchange"))
    g_big = _pair_share(reduced, [g.shape[1] // 2 for g in reduced])

    g_rel_bias = _bias_reduce([g.reshape(4, -1) for g in g_bias], bucket).T
    small_g = [jnp.stack(g_sgu_w), jnp.stack(g_sgu_b), jnp.stack(g_pool_w), jnp.stack(g_pool_scale), jnp.stack(g_sinks),
               g_rel_bias, jnp.concatenate(g_out_gain), jnp.concatenate(g_mix), jnp.concatenate(g_ffn), g_final[0]]
    small_w = [sgu_w, sgu_b, pool_w, pool_scale, swa_sinks, rel_bias, mix_out_gain, norm_mix, norm_ffn, norm_final]
    small_m = [m_sgu_w, m_sgu_b, m_pool_w, m_pool_scale, m_swa_sinks, m_rel_bias, m_mix_out_gain, m_norm_mix, m_norm_ffn, m_norm_final]
    small_v = [v_sgu_w, v_sgu_b, v_pool_w, v_pool_scale, v_swa_sinks, v_rel_bias, v_mix_out_gain, v_norm_mix, v_norm_ffn, v_norm_final]
    shapes = [w.shape for w in small_w]
    bulk, fine = [small_g[0], small_g[2]], [small_g[i] for i in (1, 3, 4, 5, 6, 7, 8, 9)] + [loss_v[0, 0:1]]
    mine = [_pack(bulk), _pack(fine)]
    theirs = _comm_only(_SwapComm(mine), "small_pair_swap")
    slots_s = [_pair_sum_slot(a, b, kidx, dt) for a, b, dt in zip(mine, theirs, (bf16, f32))]
    shared = _comm_only(_SlotShareComm(slots_s), "small_chip_share")
    g_bulk = _unpack(_small_sum(shared[0]), [shapes[0], shapes[2]])
    *g_fine, loss = _unpack(_small_sum(shared[1]), [shapes[i] for i in (1, 3, 4, 5, 6, 7, 8, 9)] + [()])
    g_small = [g_bulk[0], g_fine[0], g_bulk[1]] + g_fine[1:]

    big_m = [m_w_in, m_w_out, m_w_gate_up, m_w_down]
    big_v = [v_w_in, v_w_out, v_w_gate_up, v_w_down]
    g_out, d_big, m_big, v_big = [], [], [], []
    for w, g, m, v in zip(big, g_big, big_m, big_v):
        two = lambda a: a.reshape(-1, a.shape[-1])
        rows = two(w).shape[0]
        cap = max(8, (1 << 20) // (4 * w.shape[-1]))
        tr = max(t for t in range(8, min(rows, cap) + 1, 8) if rows % t == 0)
        d2, m2, v2, g2 = _adamw(two(w), two(g), two(m), two(v), tr, "adamw_big", True)
        for lst, val in ((d_big, d2), (m_big, m2), (v_big, v2), (g_out, g2)):
            lst.append(val.reshape(w.shape))
    g_big = g_out

    g_small_packed = _pack(g_small)
    ds, ms, vs = _adamw(_pack(small_w), g_small_packed, _pack(small_m), _pack(small_v), g_small_packed.shape[0], "adamw_small")
    d_small, m_small, v_small = _unpack(ds, shapes), _unpack(ms, shapes), _unpack(vs, shapes)

    def order(bigs, smalls):
        return [bigs[0], bigs[1]] + list(smalls[0:9]) + [bigs[2], bigs[3], smalls[9]]

    return (loss, grad_x, *order(g_big, g_small), *order(d_big, d_small), *order(m_big, m_small), *order(v_big, v_small))
```

```python
import functools

import numpy as np
import jax
import jax.numpy as jnp
from jax import lax
from jax.experimental import pallas as pl
from jax.experimental.pallas import tpu as pltpu

f32 = jnp.float32
bf16 = jnp.bfloat16
_MXU = jnp.bfloat16

EPS = 1e-6
HD = 64
GW = 256
BLK = 128
SB_UNROLL = 2
SB_HEADS = 4
SB_CUT = -110.0
POOL_WINDOWS = (2, 4, 8, 16)
N_BUCKETS = 32
MAX_DISTANCE = 128
N_CHIPS = 4
N_DEV = 8
VMEM_LIMIT = 48 * 1024 * 1024

ADAM_LR = 0.001
ADAM_B1 = 0.9
ADAM_B2 = 0.999
ADAM_EPS = 1e-08
ADAM_WD = 0.01
ADAM_STEP = 10

SDS = jax.ShapeDtypeStruct
MESH = pl.DeviceIdType.MESH
HIGHEST = lax.Precision.HIGHEST
RESIDENT = pl.Buffered(1)
NT = (((1,), (1,)), ((), ()))
TN = (((0,), (0,)), ((), ()))


def _cp(*sem):
    return pltpu.CompilerParams(dimension_semantics=sem if sem else None, vmem_limit_bytes=VMEM_LIMIT)


def _mx(v):
    return v.astype(_MXU)


def _iota(shape, dim):
    return lax.broadcasted_iota(jnp.int32, shape, dim)


def _split_dot(a, tri):
    hi = a.astype(bf16)
    lo = (a - hi.astype(f32)).astype(bf16)
    return jnp.dot(hi, tri, preferred_element_type=f32) + jnp.dot(lo, tri, preferred_element_type=f32)


def _rms(xv):
    return lax.rsqrt(jnp.mean(xv * xv, axis=-1, keepdims=True) + EPS)


def _hosted_call(body, steps, in_specs, out_specs, out_shape, scratch, args, name, comm):
    n_in, n_out = len(in_specs), len(out_specs)
    c_args, c_in, c_out, c_shapes, aliases, c_scratch = _host_specs(comm, n_in, n_out)
    step = lambda v: (lambda: pl.program_id(0) == v)
    return pl.pallas_call(
        _host(body, n_in, n_out, len(scratch), comm, step(0), step(steps - 1)), grid=(steps,),
        in_specs=list(in_specs) + c_in, out_specs=list(out_specs) + c_out, out_shape=list(out_shape) + c_shapes,
        input_output_aliases=aliases, scratch_shapes=list(scratch) + c_scratch,
        name=name if comm is None else name + "_comm", compiler_params=_cp("arbitrary"))(*args, *c_args)


def _norm_mm(x, gain, w, tm, comm=None):
    T, D = x.shape
    NS, _, ns = w.shape

    def body(x_ref, g_ref, w_ref, h_ref, o_ref):
        xv = x_ref[...]
        h = (xv * _rms(xv) * g_ref[...]).astype(bf16)
        h_ref[...] = h
        for s in range(NS):
            o_ref[:, s * ns:(s + 1) * ns] = jnp.dot(_mx(h), w_ref[s], preferred_element_type=f32).astype(bf16)

    return _hosted_call(
        body, T // tm,
        [pl.BlockSpec((tm, D), lambda i: (i, 0)),
         pl.BlockSpec((1, D), lambda i: (0, 0)),
         pl.BlockSpec((NS, D, ns), lambda i: (0, 0, 0), pipeline_mode=RESIDENT)],
        [pl.BlockSpec((tm, D), lambda i: (i, 0)), pl.BlockSpec((tm, NS * ns), lambda i: (i, 0))],
        [SDS((T, D), bf16), SDS((T, NS * ns), bf16)], [], (x, gain, w), "norm_mm_in", comm)


def _norm_mm_swiglu(x, gain, w, tm, comm=None):
    T, D = x.shape
    NS, _, ns = w.shape
    half = NS // 2

    def body(x_ref, g_ref, w_ref, h_ref, gu_ref, a_ref):
        xv = x_ref[...]
        hb = (xv * _rms(xv) * g_ref[...]).astype(bf16)
        h_ref[...] = hb
        h = _mx(hb)
        for s in range(half):
            cols = slice(s * ns, (s + 1) * ns)
            g = jnp.dot(h, w_ref[s], preferred_element_type=f32)
            u = jnp.dot(h, w_ref[s + half], preferred_element_type=f32)
            gu_ref[0, :, cols] = g.astype(bf16)
            gu_ref[1, :, cols] = u.astype(bf16)
            a_ref[:, cols] = (jax.nn.silu(g) * u).astype(bf16)

    c_args, c_in, c_out, c_shapes, aliases, c_scratch = _host_specs(comm, 3, 3)
    step = lambda v: (lambda: pl.program_id(0) == v)
    return pl.pallas_call(
        _host(body, 3, 3, 0, comm, step(0), step(T // tm - 1)), grid=(T // tm,),
        in_specs=[pl.BlockSpec((tm, D), lambda i: (i, 0)),
                  pl.BlockSpec((1, D), lambda i: (0, 0)),
                  pl.BlockSpec((NS, D, ns), lambda i: (0, 0, 0), pipeline_mode=RESIDENT)] + c_in,
        out_specs=[pl.BlockSpec((tm, D), lambda i: (i, 0)),
                   pl.BlockSpec((2, tm, half * ns), lambda i: (0, i, 0)),
                   pl.BlockSpec((tm, half * ns), lambda i: (i, 0))] + c_out,
        out_shape=[SDS((T, D), bf16), SDS((2, T, half * ns), bf16), SDS((T, half * ns), bf16)] + c_shapes,
        input_output_aliases=aliases, scratch_shapes=c_scratch,
        name="norm_mm_swiglu" if comm is None else "norm_mm_swiglu_gather",
        compiler_params=_cp("arbitrary"))(x, gain, w, *c_args)


def _gnorm_mm_res(ys, gain, w, x, tm, comm=None):
    T, D = x.shape

    def body(ya, yb, yc, yd, g_ref, w_ref, x_ref, yn_ref, o_ref):
        parts = []
        for m, r in enumerate((ya, yb, yc, yd)):
            y = r[...].astype(f32)
            parts.append((y * _rms(y) * g_ref[:, m * GW:(m + 1) * GW]).astype(bf16))
        yn = jnp.concatenate(parts, axis=1)
        yn_ref[...] = yn
        o_ref[...] = x_ref[...] + jnp.dot(_mx(yn), w_ref[...], preferred_element_type=f32)

    yspec = pl.BlockSpec((tm, GW), lambda i: (i, 0))
    return _hosted_call(
        body, T // tm,
        [yspec, yspec, yspec, yspec,
         pl.BlockSpec((1, D), lambda i: (0, 0)),
         pl.BlockSpec((D, D), lambda i: (0, 0), pipeline_mode=RESIDENT),
         pl.BlockSpec((tm, D), lambda i: (i, 0))],
        [pl.BlockSpec((tm, D), lambda i: (i, 0)), pl.BlockSpec((tm, D), lambda i: (i, 0))],
        [SDS((T, D), bf16), SDS((T, D), f32)], [], (*ys, gain, w, x), "gnorm_mm_res", comm)


def _mm_res(a, w, x, tm, comm=None):
    T, D = x.shape
    K = a.shape[1]

    def body(a_ref, w_ref, x_ref, o_ref):
        o_ref[...] = x_ref[...] + jnp.dot(_mx(a_ref[...]), w_ref[...], preferred_element_type=f32)

    return _hosted_call(
        body, T // tm,
        [pl.BlockSpec((tm, K), lambda i: (i, 0)),
         pl.BlockSpec((K, D), lambda i: (0, 0), pipeline_mode=RESIDENT),
         pl.BlockSpec((tm, D), lambda i: (i, 0))],
        [pl.BlockSpec((tm, D), lambda i: (i, 0))], [SDS((T, D), f32)], [], (a, w, x), "mm_res_down", comm)


def _final_loss(x, gain, tgt, tm):
    T, D = x.shape

    def body(x_ref, g_ref, t_ref, dx_ref, dg_ref, l_ref):
        @pl.when(pl.program_id(0) == 0)
        def _():
            dg_ref[...] = jnp.zeros_like(dg_ref)
            l_ref[...] = jnp.zeros_like(l_ref)
        xv = x_ref[...]
        g = g_ref[...]
        r = _rms(xv)
        xh = xv * r
        err = xh * g - t_ref[...]
        l_ref[...] += 0.5 * jnp.sum(jnp.mean(err * err, axis=-1, keepdims=True), axis=0, keepdims=True)
        dy = err * (1.0 / D)
        dg_ref[...] += jnp.sum(dy * xh, axis=0, keepdims=True)
        dxh = dy * g
        dx_ref[...] = r * (dxh - xh * jnp.mean(dxh * xh, axis=-1, keepdims=True))

    return pl.pallas_call(
        body, grid=(T // tm,),
        in_specs=[pl.BlockSpec((tm, D), lambda i: (i, 0)),
                  pl.BlockSpec((1, D), lambda i: (0, 0)),
                  pl.BlockSpec((tm, D), lambda i: (i, 0))],
        out_specs=[pl.BlockSpec((tm, D), lambda i: (i, 0)),
                   pl.BlockSpec((1, D), lambda i: (0, 0)),
                   pl.BlockSpec((1, BLK), lambda i: (0, 0))],
        out_shape=[SDS((T, D), f32), SDS((1, D), f32), SDS((1, BLK), f32)],
        name="final_loss", compiler_params=_cp("arbitrary"))(x, gain, tgt)


def _dact_dx(dx, wd, gu, w, x, gain, tm, comm=None):
    T, D = dx.shape
    F = wd.shape[0]
    NS, _, ns = w.shape
    half = NS // 2

    def body(dx_ref, wd_ref, gu_ref, w_ref, x_ref, g_ref, dgu_ref, dx1_ref, dg_ref):
        @pl.when(pl.program_id(0) == 0)
        def _():
            dg_ref[...] = jnp.zeros_like(dg_ref)
        dxv = dx_ref[...]
        dxb = _mx(dxv)
        dh = None
        for s in range(half):
            cols = slice(s * ns, (s + 1) * ns)
            da = lax.dot_general(dxb, wd_ref[s * ns:(s + 1) * ns, :], NT, preferred_element_type=f32)
            g = gu_ref[0, :, cols].astype(f32)
            u = gu_ref[1, :, cols].astype(f32)
            sg = jax.nn.sigmoid(g)
            dgs = (da * u * (sg * (1.0 + g * (1.0 - sg)))).astype(bf16)
            dus = (da * (g * sg)).astype(bf16)
            dgu_ref[0, :, cols] = dgs
            dgu_ref[1, :, cols] = dus
            part = (lax.dot_general(_mx(dgs), w_ref[s], NT, preferred_element_type=f32)
                    + lax.dot_general(_mx(dus), w_ref[s + half], NT, preferred_element_type=f32))
            dh = part if dh is None else dh + part
        xv = x_ref[...]
        r = _rms(xv)
        xh = xv * r
        dg_ref[...] += jnp.sum(dh * xh, axis=0, keepdims=True)
        dxh = dh * g_ref[...]
        dx1_ref[...] = dxv + r * (dxh - xh * jnp.mean(dxh * xh, axis=-1, keepdims=True))

    return _hosted_call(
        body, T // tm,
        [pl.BlockSpec((tm, D), lambda i: (i, 0)),
         pl.BlockSpec((F, D), lambda i: (0, 0), pipeline_mode=RESIDENT),
         pl.BlockSpec((2, tm, F), lambda i: (0, i, 0)),
         pl.BlockSpec((NS, D, ns), lambda i: (0, 0, 0), pipeline_mode=RESIDENT),
         pl.BlockSpec((tm, D), lambda i: (i, 0)),
         pl.BlockSpec((1, D), lambda i: (0, 0))],
        [pl.BlockSpec((2, tm, F), lambda i: (0, i, 0)), pl.BlockSpec((tm, D), lambda i: (i, 0)),
         pl.BlockSpec((1, D), lambda i: (0, 0))],
        [SDS((2, T, F), bf16), SDS((T, D), f32), SDS((1, D), f32)], [], (dx, wd, gu, w, x, gain), "dact_dx", comm)


def _dw(a, b, b_map, ns, NS, tka, tk, name):
    T, Ka = a.shape
    b_block = (tk, ns) if b.ndim == 2 else (1, tk, ns)

    def body(a_ref, b_ref, o_ref):
        bv = b_ref[...] if b.ndim == 2 else b_ref[0]
        part = lax.dot_general(_mx(a_ref[...]), _mx(bv), TN, preferred_element_type=f32)

        @pl.when(pl.program_id(2) == 0)
        def _():
            o_ref[0] = part

        @pl.when(pl.program_id(2) > 0)
        def _():
            o_ref[0] += part

    return pl.pallas_call(
        body, grid=(NS, Ka // tka, T // tk),
        in_specs=[pl.BlockSpec((tk, tka), lambda s, k, t: (t, k)),
                  pl.BlockSpec(b_block, lambda s, k, t: b_map(t, s))],
        out_specs=pl.BlockSpec((1, tka, ns), lambda s, k, t: (s, k, 0)),
        out_shape=SDS((NS, Ka, ns), f32),
        name=name, compiler_params=_cp("parallel", "parallel", "arbitrary"))(a, b)


def _dw_pieces(a, pieces, ns, NS, tk, name):
    T, Ka = a.shape
    n = len(pieces)

    def body(*refs):
        a_ref, b_refs, o_ref = refs[0], refs[1:1 + n], refs[1 + n]
        full = jnp.concatenate([r[...] for r in b_refs], axis=1)
        av = _mx(a_ref[...])
        parts = [lax.dot_general(av, _mx(full[:, s * ns:(s + 1) * ns]), TN, preferred_element_type=f32) for s in range(NS)]

        @pl.when(pl.program_id(0) == 0)
        def _():
            for s in range(NS):
                o_ref[s] = parts[s]

        @pl.when(pl.program_id(0) > 0)
        def _():
            for s in range(NS):
                o_ref[s] += parts[s]

    return pl.pallas_call(
        body, grid=(T // tk,),
        in_specs=[pl.BlockSpec((tk, Ka), lambda t: (t, 0))] + [pl.BlockSpec((tk, p.shape[1]), lambda t: (t, 0)) for p in pieces],
        out_specs=pl.BlockSpec((NS, Ka, ns), lambda t: (0, 0, 0)),
        out_shape=SDS((NS, Ka, ns), f32),
        name=name, compiler_params=_cp("arbitrary"))(a, *pieces)


def _dx_norm_bwd(pieces, w, x, gain, dxin, tm, name, comm=None):
    T, D = x.shape
    NS, _, ns = w.shape
    n_dy = len(pieces)

    def body(*refs):
        dy_refs = refs[:n_dy]
        w_ref, x_ref, g_ref, dxin_ref, dx_ref, dg_ref = refs[n_dy:]

        @pl.when(pl.program_id(0) == 0)
        def _():
            dg_ref[...] = jnp.zeros_like(dg_ref)
        full = jnp.concatenate([r[...] for r in dy_refs], axis=1)
        dh = None
        for s in range(NS):
            part = lax.dot_general(_mx(full[:, s * ns:(s + 1) * ns]), w_ref[s], NT, preferred_element_type=f32)
            dh = part if dh is None else dh + part
        xv = x_ref[...]
        r = _rms(xv)
        xh = xv * r
        dg_ref[...] += jnp.sum(dh * xh, axis=0, keepdims=True)
        dxh = dh * g_ref[...]
        dx_ref[...] = dxin_ref[...] + r * (dxh - xh * jnp.mean(dxh * xh, axis=-1, keepdims=True))

    dy_specs = [pl.BlockSpec((tm, p.shape[1]), lambda i: (i, 0)) for p in pieces]
    return _hosted_call(
        body, T // tm,
        dy_specs + [pl.BlockSpec((NS, D, ns), lambda i: (0, 0, 0), pipeline_mode=RESIDENT),
                    pl.BlockSpec((tm, D), lambda i: (i, 0)),
                    pl.BlockSpec((1, D), lambda i: (0, 0)),
                    pl.BlockSpec((tm, D), lambda i: (i, 0))],
        [pl.BlockSpec((tm, D), lambda i: (i, 0)), pl.BlockSpec((1, D), lambda i: (0, 0))],
        [SDS((T, D), f32), SDS((1, D), f32)], [], (*pieces, w, x, gain, dxin), name, comm)


def _dycat(dx, w, ys, gain, tm, comm=None):
    T, D = dx.shape

    def body(dx_ref, w_ref, ya, yb, yc, yd, g_ref, da, db, dc, dd, dg_ref):
        @pl.when(pl.program_id(0) == 0)
        def _():
            dg_ref[...] = jnp.zeros_like(dg_ref)
        dyn = lax.dot_general(_mx(dx_ref[...]), w_ref[...], NT, preferred_element_type=f32)
        for m, (r, o) in enumerate(((ya, da), (yb, db), (yc, dc), (yd, dd))):
            cols = slice(m * GW, (m + 1) * GW)
            y = r[...].astype(f32)
            rs = _rms(y)
            yh = y * rs
            d = dyn[:, cols]
            dg_ref[:, cols] += jnp.sum(d * yh, axis=0, keepdims=True)
            dyh = d * g_ref[:, cols]
            o[...] = (rs * (dyh - yh * jnp.mean(dyh * yh, axis=-1, keepdims=True))).astype(bf16)

    yspec = pl.BlockSpec((tm, GW), lambda i: (i, 0))
    return _hosted_call(
        body, T // tm,
        [pl.BlockSpec((tm, D), lambda i: (i, 0)),
         pl.BlockSpec((D, D), lambda i: (0, 0), pipeline_mode=RESIDENT),
         yspec, yspec, yspec, yspec,
         pl.BlockSpec((1, D), lambda i: (0, 0))],
        [yspec, yspec, yspec, yspec, pl.BlockSpec((1, D), lambda i: (0, 0))],
        [SDS((T, GW), bf16)] * 4 + [SDS((1, D), f32)], [], (dx, w, *ys, gain), "dycat", comm)


def _sgu_consts():
    r, c = _iota((GW, GW), 0), _iota((GW, GW), 1)
    seg = (r // HD == c // HD).astype(f32)
    tr, ts = _iota((BLK, BLK), 0), _iota((BLK, BLK), 1)
    causal = ts <= tr
    lane_head = _iota((BLK, GW), 1) // HD
    return seg, causal, lane_head


def _split3_dot(a, ones):
    hi = a.astype(bf16)
    r1 = a - hi.astype(f32)
    mid = r1.astype(bf16)
    lo = (r1 - mid.astype(f32)).astype(bf16)
    dot = functools.partial(jnp.dot, preferred_element_type=f32)
    return dot(hi, ones) + dot(mid, ones) + dot(lo, ones)


def _sgu_chunks(aus, avs, w, bexp, consts):
    seg, causal, lane_head = consts
    segb = seg.astype(bf16)
    nh = GW // HD
    vs = [jax.nn.gelu(av) for av in avs]
    mus = [_split3_dot(v, segb) * (1.0 / HD) for v in vs]
    vcs = [v - mu for v, mu in zip(vs, mus)]
    vars_ = [_split3_dot(vc * vc, segb) * (1.0 / HD) for vc in vcs]
    vns = [_mx(vc * lax.rsqrt(var + EPS)) for vc, var in zip(vcs, vars_)]
    whs = [_mx(jnp.where(causal, w[h], 0.0)) for h in range(nh)]
    mixes = [[jnp.dot(whs[h], vn, preferred_element_type=f32) for h in range(nh)] for vn in vns]
    out = []
    for au, ms in zip(aus, mixes):
        mix = bexp
        for h in range(nh):
            mix = mix + jnp.where(lane_head == h, ms[h], 0.0)
        out.append(jax.nn.gelu(au) * mix)
    return out


def _sgu_group(S):
    nc = S // BLK
    return 4 if nc % 4 == 0 else (2 if nc % 2 == 0 else 1)


def _sgu_fwd(proj, w, bexp, B, S):
    G = _sgu_group(S)

    def body(au_ref, av_ref, w_ref, b_ref, y_ref):
        consts = _sgu_consts()
        wv, bv = w_ref[...], b_ref[...]

        def group(n, c):
            rows = [pl.ds(pl.multiple_of((n * G + j) * BLK, BLK), BLK) for j in range(G)]
            ys = _sgu_chunks([au_ref[r, :].astype(f32) for r in rows], [av_ref[r, :].astype(f32) for r in rows],
                             wv, bv, consts)
            for r, y in zip(rows, ys):
                y_ref[r, :] = y.astype(bf16)
            return c
        lax.fori_loop(0, S // BLK // G, group, 0)

    return pl.pallas_call(
        body, grid=(B,),
        in_specs=[pl.BlockSpec((S, GW), lambda b: (b, 0)),
                  pl.BlockSpec((S, GW), lambda b: (b, 1)),
                  pl.BlockSpec((GW // HD, BLK, BLK), lambda b: (0, 0, 0)),
                  pl.BlockSpec((BLK, GW), lambda b: (0, 0))],
        out_specs=pl.BlockSpec((S, GW), lambda b: (b, 0)),
        out_shape=SDS((B * S, GW), bf16),
        name="sgu_fwd", compiler_params=_cp("parallel"))(proj, proj, w, bexp)


def _sgu_bwd(proj, w, bexp, dy, B, S, comm=None):
    def body(au_ref, av_ref, w_ref, b_ref, dy_ref, dp_ref, dw_ref, db_ref):
        @pl.when(pl.program_id(0) == 0)
        def _():
            dw_ref[...] = jnp.zeros_like(dw_ref)
            db_ref[...] = jnp.zeros_like(db_ref)
        consts = _sgu_consts()
        wv, bv = w_ref[...], b_ref[...]
        fn = lambda aus, avs, ww, bb: _sgu_chunks(aus, avs, ww, bb, consts)
        G = _sgu_group(S)

        def group(n, carry):
            dw_acc, db_acc = carry
            rows = [pl.ds(pl.multiple_of((n * G + j) * BLK, BLK), BLK) for j in range(G)]
            _, vjp = jax.vjp(fn, [au_ref[r, :].astype(f32) for r in rows], [av_ref[r, :].astype(f32) for r in rows], wv, bv)
            daus, davs, dwc, dbc = vjp([dy_ref[r, :].astype(f32) for r in rows])
            for r, dau, dav in zip(rows, daus, davs):
                dp_ref[r, 0:GW] = dau.astype(bf16)
                dp_ref[r, GW:2 * GW] = dav.astype(bf16)
            return dw_acc + dwc, db_acc + dbc
        dw_acc, db_acc = lax.fori_loop(0, S // BLK // G, group, (jnp.zeros(wv.shape, f32), jnp.zeros(bv.shape, f32)))
        dw_ref[...] += dw_acc
        db_ref[...] += jnp.dot(db_acc, consts[0], precision=HIGHEST, preferred_element_type=f32)

    return _hosted_call(
        body, B,
        [pl.BlockSpec((S, GW), lambda b: (b, 0)),
         pl.BlockSpec((S, GW), lambda b: (b, 1)),
         pl.BlockSpec((GW // HD, BLK, BLK), lambda b: (0, 0, 0)),
         pl.BlockSpec((BLK, GW), lambda b: (0, 0)),
         pl.BlockSpec((S, GW), lambda b: (b, 0))],
        [pl.BlockSpec((S, 2 * GW), lambda b: (b, 0)),
         pl.BlockSpec((GW // HD, BLK, BLK), lambda b: (0, 0, 0)),
         pl.BlockSpec((BLK, GW), lambda b: (0, 0))],
        [SDS((B * S, 2 * GW), bf16), SDS((GW // HD, BLK, BLK), f32), SDS((BLK, GW), f32)], [],
        (proj, proj, w, bexp, dy), "sgu_bwd", comm)


def _pool_parts(p):
    n = p.shape[0]
    r = _iota(p.shape, 0)
    lg = _iota(p.shape, 1) // HD

    def sh(v, k):
        return jnp.where(r >= k, pltpu.roll(v, k, 0), 0.0)
    s2 = p + sh(p, 1)
    s4 = s2 + sh(s2, 2)
    s8 = s4 + sh(s4, 4)
    s16 = s8 + sh(s8, 8)
    ws = jnp.where(lg == 0, s2, jnp.where(lg == 1, s4, jnp.where(lg == 2, s8, s16)))
    wlen = jnp.where(lg == 0, 2, jnp.where(lg == 1, 4, jnp.where(lg == 2, 8, 16)))
    cnt = jnp.minimum(r + 1, wlen).astype(f32)
    del n
    return ws / cnt - p, cnt, lg


def _pool_fwd(proj, wbd, scale, B, S):
    def body(p_ref, w_ref, s_ref, y_ref):
        y, _, _ = _pool_parts(p_ref[...].astype(f32))
        y_ref[...] = (jnp.dot(_mx(y), _mx(w_ref[...]), preferred_element_type=f32) * s_ref[...]).astype(bf16)

    return pl.pallas_call(
        body, grid=(B,),
        in_specs=[pl.BlockSpec((S, GW), lambda b: (b, 2)),
                  pl.BlockSpec((GW, GW), lambda b: (0, 0)),
                  pl.BlockSpec((1, GW), lambda b: (0, 0))],
        out_specs=pl.BlockSpec((S, GW), lambda b: (b, 0)),
        out_shape=SDS((B * S, GW), bf16),
        name="pool_fwd", compiler_params=_cp("parallel"))(proj, wbd, scale)


def _pool_bwd(proj, wbd, scale, dy, B, S):
    def body(p_ref, w_ref, s_ref, dy_ref, dp_ref, dw_ref, ds_ref):
        @pl.when(pl.program_id(0) == 0)
        def _():
            dw_ref[...] = jnp.zeros_like(dw_ref)
            ds_ref[...] = jnp.zeros_like(ds_ref)
        y, cnt, lg = _pool_parts(p_ref[...].astype(f32))
        wv = _mx(w_ref[...])
        z = jnp.dot(_mx(y), wv, preferred_element_type=f32)
        dout = dy_ref[...].astype(f32)
        ds_ref[...] += jnp.sum(dout * z, axis=0, keepdims=True)
        dz = _mx(dout * s_ref[...])
        dw_ref[...] += lax.dot_general(_mx(y), dz, TN, preferred_element_type=f32)
        dyv = lax.dot_general(dz, wv, NT, preferred_element_type=f32)
        n = dyv.shape[0]
        r = _iota(dyv.shape, 0)

        def ush(v, k):
            return jnp.where(r < n - k, pltpu.roll(v, n - k, 0), 0.0)
        gq = dyv / cnt
        a2 = gq + ush(gq, 1)
        a4 = a2 + ush(a2, 2)
        a8 = a4 + ush(a4, 4)
        a16 = a8 + ush(a8, 8)
        adj = jnp.where(lg == 0, a2, jnp.where(lg == 1, a4, jnp.where(lg == 2, a8, a16)))
        dp_ref[...] = (adj - dyv).astype(bf16)

    return pl.pallas_call(
        body, grid=(B,),
        in_specs=[pl.BlockSpec((S, GW), lambda b: (b, 2)),
                  pl.BlockSpec((GW, GW), lambda b: (0, 0)),
                  pl.BlockSpec((1, GW), lambda b: (0, 0)),
                  pl.BlockSpec((S, GW), lambda b: (b, 0))],
        out_specs=[pl.BlockSpec((S, GW), lambda b: (b, 0)),
                   pl.BlockSpec((GW, GW), lambda b: (0, 0)),
                   pl.BlockSpec((1, GW), lambda b: (0, 0))],
        out_shape=[SDS((B * S, GW), bf16), SDS((GW, GW), f32), SDS((1, GW), f32)],
        name="pool_bwd", compiler_params=_cp("arbitrary"))(proj, wbd, scale, dy)


def _t5_bucket_table():
    dist = (np.arange(BLK)[:, None] + BLK) - np.arange(2 * BLK)[None, :]
    d = np.clip(dist, 0, BLK - 1)
    max_exact = N_BUCKETS // 2
    df = np.maximum(d, 1).astype(np.float32)
    large = max_exact + (np.log(df / max_exact) / np.float32(np.log(MAX_DISTANCE / max_exact))
                         * (N_BUCKETS - max_exact)).astype(np.int32)
    large = np.minimum(large, N_BUCKETS - 1)
    return np.where(d < max_exact, d, large).astype(np.int32)


def _swa_blocks(qs, kx, vx, sinks, biases, n):
    G = len(qs)
    heads = [(p, g) for p in range(2) for g in range(2)]
    ri, ci = _iota((BLK, BLK), 0), _iota((BLK, BLK), 1)
    qi, ki = _iota((BLK, 2 * BLK), 0), _iota((BLK, 2 * BLK), 1)
    dist = qi + BLK - ki
    band = (dist >= 0) & (dist < BLK)
    masks = [band & ((ki >= BLK) | (n > 0))] + [band] * (G - 1)
    kb, vb = _mx(kx), _mx(vx)
    qsel = [[None] * 4 for _ in range(G)]
    vs = []
    for h, (p, g) in enumerate(heads):
        selq = ((ri - g * HD == ci - p * HD) & (ri >= g * HD) & (ri < (g + 1) * HD)).astype(_MXU)
        selv = ((ci - g * HD == ri - p * HD) & (ci >= g * HD) & (ci < (g + 1) * HD)).astype(_MXU)
        for b in range(G):
            qsel[b][h] = _mx(jnp.dot(_mx(qs[b][p]), selq, preferred_element_type=f32))
        vs.append(_mx(jnp.dot(vb, selv, preferred_element_type=f32)))
    zs = [[lax.dot_general(qsel[b][h], kb[b * BLK:(b + 2) * BLK], NT, preferred_element_type=f32) * (HD ** -0.5)
           for h in range(4)] for b in range(G)]
    prs = [[None] * 4 for _ in range(G)]
    for b in range(G):
        for h in range(4):
            z = jnp.where(masks[b], zs[b][h] + biases[h], -1e30)
            s = jnp.mean(sinks[h], axis=-1, keepdims=True)
            m = jnp.maximum(jnp.max(z, axis=-1, keepdims=True), s)
            e = jnp.exp(z - m)
            prs[b][h] = _mx(e / (jnp.sum(e, axis=-1, keepdims=True) + jnp.exp(s - m)))
    outs = [[jnp.dot(prs[b][h], vs[h][b * BLK:(b + 2) * BLK], preferred_element_type=f32) for h in range(4)]
            for b in range(G)]
    return [[o[0] + o[1], o[2] + o[3]] for o in outs]


def _swa_group(S):
    return 2 if (S // BLK) % 2 == 0 else 1


def _swa_rows(n, G):
    blk = lambda j: pl.ds(pl.multiple_of(j * BLK, BLK), BLK)
    return [blk(jnp.maximum(n - 1, 0))] + [blk(n + b) for b in range(G)]


def _swa_fwd(proj, sinks, bias, B, S, comm=None):
    G = _swa_group(S)

    def body(q_ref, kv_ref, s_ref, b_ref, y_ref):
        def group(i, c):
            n = i * G
            rows = _swa_rows(n, G)
            kx = jnp.concatenate([kv_ref[r, 0:BLK] for r in rows], axis=0).astype(f32)
            vx = jnp.concatenate([kv_ref[r, BLK:2 * BLK] for r in rows], axis=0).astype(f32)
            qs = [[q_ref[r, 0:BLK].astype(f32), q_ref[r, BLK:2 * BLK].astype(f32)] for r in rows[1:]]
            outs = _swa_blocks(qs, kx, vx, [s_ref[h] for h in range(4)], [b_ref[h] for h in range(4)], n)
            for r, (o0, o1) in zip(rows[1:], outs):
                y_ref[r, 0:BLK] = o0.astype(bf16)
                y_ref[r, BLK:2 * BLK] = o1.astype(bf16)
            return c
        lax.fori_loop(0, S // BLK // G, group, 0)

    return _hosted_call(
        body, B,
        [pl.BlockSpec((S, GW), lambda b: (b, 3)),
         pl.BlockSpec((S, GW), lambda b: (b, 4)),
         pl.BlockSpec((4, 1, BLK), lambda b: (0, 0, 0)),
         pl.BlockSpec((4, BLK, 2 * BLK), lambda b: (0, 0, 0))],
        [pl.BlockSpec((S, GW), lambda b: (b, 0))], [SDS((B * S, GW), bf16)], [],
        (proj, proj, sinks, bias), "swa_fwd", comm)


def _swa_bwd(proj, sinks, bias, dy, B, S, comm=None):
    def body(q_ref, kv_ref, s_ref, b_ref, dy_ref, dq_ref, dkv_ref, ds_ref, db_ref, acc_ref):
        @pl.when(pl.program_id(0) == 0)
        def _():
            ds_ref[...] = jnp.zeros_like(ds_ref)
            db_ref[...] = jnp.zeros_like(db_ref)
        acc_ref[...] = jnp.zeros_like(acc_ref)

        G = _swa_group(S)

        def group(i, c):
            n = i * G
            rows = _swa_rows(n, G)
            kx = jnp.concatenate([kv_ref[r, 0:BLK] for r in rows], axis=0).astype(f32)
            vx = jnp.concatenate([kv_ref[r, BLK:2 * BLK] for r in rows], axis=0).astype(f32)
            qs = [[q_ref[r, 0:BLK].astype(f32), q_ref[r, BLK:2 * BLK].astype(f32)] for r in rows[1:]]
            dos = [[dy_ref[r, 0:BLK].astype(f32), dy_ref[r, BLK:2 * BLK].astype(f32)] for r in rows[1:]]
            fn = functools.partial(_swa_blocks, n=n)
            _, vjp = jax.vjp(fn, qs, kx, vx, [s_ref[h] for h in range(4)], [b_ref[h] for h in range(4)])
            dqs, dkx, dvx, dss, dbs = vjp(dos)
            for r, (dq0, dq1) in zip(rows[1:], dqs):
                dq_ref[r, 0:BLK] = dq0.astype(bf16)
                dq_ref[r, BLK:2 * BLK] = dq1.astype(bf16)
            for h in range(4):
                ds_ref[h] += dss[h]
                db_ref[h] += dbs[h]
            for j, r in enumerate(rows):
                acc_ref[r, 0:BLK] += dkx[j * BLK:(j + 1) * BLK]
                acc_ref[r, BLK:2 * BLK] += dvx[j * BLK:(j + 1) * BLK]
            return c
        lax.fori_loop(0, S // BLK // G, group, 0)
        dkv_ref[...] = acc_ref[...].astype(bf16)

    c_args, c_in, c_out, c_shapes, aliases, c_scratch = _host_specs(comm, 5, 4)
    step = lambda v: (lambda: pl.program_id(0) == v)
    return pl.pallas_call(
        _host(body, 5, 4, 1, comm, step(0), step(B - 1)), grid=(B,),
        in_specs=[pl.BlockSpec((S, GW), lambda b: (b, 3)),
                  pl.BlockSpec((S, GW), lambda b: (b, 4)),
                  pl.BlockSpec((4, 1, BLK), lambda b: (0, 0, 0)),
                  pl.BlockSpec((4, BLK, 2 * BLK), lambda b: (0, 0, 0)),
                  pl.BlockSpec((S, GW), lambda b: (b, 0))] + c_in,
        out_specs=[pl.BlockSpec((S, GW), lambda b: (b, 0)),
                   pl.BlockSpec((S, GW), lambda b: (b, 0)),
                   pl.BlockSpec((4, 1, BLK), lambda b: (0, 0, 0)),
                   pl.BlockSpec((4, BLK, 2 * BLK), lambda b: (0, 0, 0))] + c_out,
        out_shape=[SDS((B * S, GW), bf16), SDS((B * S, GW), bf16), SDS((4, 1, BLK), f32),
                   SDS((4, BLK, 2 * BLK), f32)] + c_shapes,
        input_output_aliases=aliases, scratch_shapes=[pltpu.VMEM((S, GW), f32)] + c_scratch,
        name="swa_bwd" if comm is None else "swa_bwd_exchange",
        compiler_params=_cp("arbitrary"))(proj, proj, sinks, bias, dy, *c_args)


def _log1m_parts(z):
    t = jnp.exp(-jnp.abs(z))
    return jnp.minimum(-z, 0.0) - jnp.log(1.0 + t), t


def _log1m(z):
    return _log1m_parts(z)[0]


def _sigmoid_from(z, t):
    return jnp.where(z >= 0.0, 1.0, t) / (1.0 + t)


def _sb_consts(tri):
    r2, c2 = _iota((2 * BLK, 2 * BLK), 0), _iota((2 * BLK, 2 * BLK), 1)
    tri2 = (tri(r2, c2) & (r2 // BLK == c2 // BLK)).astype(bf16)
    ri, ci = _iota((BLK, 2 * BLK), 0), _iota((BLK, 2 * BLK), 1)
    strict2 = (ci % BLK) < ri
    head0 = _iota((BLK, BLK), 1) < HD
    return tri2, strict2, head0


def _sb_stack_kv(k_ref, v_ref, kst_ref, vst_ref, head0, nb):
    def one(kb, c):
        krows = pl.ds(pl.multiple_of(kb * BLK, BLK), BLK)
        for p in range(2):
            for src, dst in ((k_ref, kst_ref), (v_ref, vst_ref)):
                t = src[krows, p * BLK:(p + 1) * BLK]
                dst[p, kb] = _mx(jnp.concatenate([jnp.where(head0, t, 0.0), jnp.where(head0, 0.0, t)], axis=0))
        return c
    lax.fori_loop(0, nb, one, 0)


def _sb_load_kv(kst_ref, vst_ref, kb):
    return [kst_ref[p, kb] for p in range(2)], [vst_ref[p, kb] for p in range(2)]


def _two_halves(a, b):
    return jnp.concatenate([jnp.broadcast_to(a, (BLK, BLK)), jnp.broadcast_to(b, (BLK, BLK))], axis=1)


def _half_sums(t):
    return jnp.sum(t[:, :BLK], axis=-1, keepdims=True), jnp.sum(t[:, BLK:], axis=-1, keepdims=True)


def _sb_fwd(proj, B, S, comm=None):
    def body(q_ref, k_ref, v_ref, y_ref, lt_ref, kst_ref, vst_ref):
        ci = _iota((BLK, BLK), 1)
        above2, strict2, head0 = _sb_consts(lambda r, c: r > c)
        _sb_stack_kv(k_ref, v_ref, kst_ref, vst_ref, head0, S // BLK)

        def step(qs, kbs, diags, carry):
            U = range(len(kbs))
            ok = [None if diags[u] else kbs[u] >= 0 for u in U]
            kv = [_sb_load_kv(kst_ref, vst_ref, jnp.maximum(kb, 0)) for kb in kbs]
            zs = [[lax.dot_general(qs[p], kks[p], NT, preferred_element_type=f32) for p in range(2)] for kks, _ in kv]
            Ls = [[jnp.where(strict2, _log1m(z), 0.0) if diags[u] else _log1m(z) for z in zs[u]] for u in U]
            tails = [[_split_dot(L, above2) for L in Lu] for Lu in Ls]
            carry = list(carry)
            for u in U:
                for p in range(2):
                    R0, R1, acc = carry[3 * p:3 * p + 3]
                    w = jnp.exp(zs[u][p] + Ls[u][p] + tails[u][p] + _two_halves(R0, R1))
                    s0, s1 = _half_sums(Ls[u][p])
                    if diags[u]:
                        w = jnp.where(strict2, w, 0.0)
                    else:
                        w, s0, s1 = (jnp.where(ok[u], t, 0.0) for t in (w, s0, s1))
                    acc = acc + jnp.dot(_mx(w), kv[u][1][p], preferred_element_type=f32)
                    carry[3 * p:3 * p + 3] = [R0 + s0, R1 + s1, acc]
            return tuple(carry)

        def qblock(n, c):
            qrows = pl.ds(pl.multiple_of(n * BLK, BLK), BLK)
            qs = [_mx(q_ref[qrows, p * BLK:(p + 1) * BLK] * (HD ** -0.5)) for p in range(2)]
            z1, z2 = jnp.zeros((BLK, 1), f32), jnp.zeros((BLK, BLK), f32)
            near = [n - 1 - u for u in range(SB_UNROLL)]
            carry = step(qs, [n] + near, [True] + [False] * SB_UNROLL, (z1, z1, z2, z1, z1, z2))
            far = jnp.maximum(n - SB_UNROLL, 0)
            trips = (far + SB_UNROLL - 1) // SB_UNROLL

            def live(st):
                worst = jnp.maximum(jnp.maximum(st[1], st[2]), jnp.maximum(st[4], st[5]))
                return (st[0] < trips) & (jnp.max(worst) > SB_CUT)

            def trip(st):
                i = st[0]
                kbs = [far - 1 - SB_UNROLL * i - u for u in range(SB_UNROLL)]
                return (i + 1,) + step(qs, kbs, [False] * SB_UNROLL, st[1:])
            done, *res = lax.while_loop(live, trip, (jnp.int32(0),) + carry)
            lt = jnp.where(ci == SB_HEADS, done.astype(f32), 0.0)
            for p in range(2):
                y_ref[qrows, p * BLK:(p + 1) * BLK] = res[3 * p + 2].astype(bf16)
                lt = lt + jnp.where(ci == 2 * p, res[3 * p], 0.0) + jnp.where(ci == 2 * p + 1, res[3 * p + 1], 0.0)
            lt_ref[qrows, :] = lt
            return c
        lax.fori_loop(0, S // BLK, qblock, 0)

    spec = lambda j: pl.BlockSpec((S, GW), lambda b: (b, j))
    c_args, c_in, c_out, c_shapes, aliases, c_scratch = _host_specs(comm, 3, 2)
    step = lambda v: (lambda: pl.program_id(0) == v)
    stacked = pltpu.VMEM((2, S // BLK, 2 * BLK, BLK), _MXU)
    return pl.pallas_call(
        _host(body, 3, 2, 2, comm, step(0), step(B - 1)), grid=(B,),
        in_specs=[spec(5), spec(6), spec(7)] + c_in,
        out_specs=[pl.BlockSpec((S, GW), lambda b: (b, 0)), pl.BlockSpec((S, BLK), lambda b: (b, 0))] + c_out,
        out_shape=[SDS((B * S, GW), bf16), SDS((B * S, BLK), f32)] + c_shapes,
        input_output_aliases=aliases, scratch_shapes=[stacked, stacked] + c_scratch,
        name="sb_fwd" if comm is None else "sb_fwd_gather",
        compiler_params=_cp("arbitrary"))(proj, proj, proj, *c_args)


def _sb_bwd(proj, ltot, dy, B, S, comm=None):
    def body(q_ref, k_ref, v_ref, lt_ref, dy_ref, dq_ref, dk_ref, dv_ref, dka_ref, dva_ref, kst_ref, vst_ref):
        ci = _iota((BLK, BLK), 1)
        upto2, strict2, head0 = _sb_consts(lambda r, c: r <= c)
        below2, _, _ = _sb_consts(lambda r, c: r < c)
        dka_ref[...] = jnp.zeros_like(dka_ref)
        dva_ref[...] = jnp.zeros_like(dva_ref)
        _sb_stack_kv(k_ref, v_ref, kst_ref, vst_ref, head0, S // BLK)

        def step(qs, dos, lts, kbs, ok, diags, top, carry):
            U = range(len(kbs))
            kbs = [jnp.clip(kb, 0, top) for kb in kbs]
            kv = [_sb_load_kv(kst_ref, vst_ref, kb) for kb in kbs]
            zs = [[lax.dot_general(qs[p], kv[u][0][p], NT, preferred_element_type=f32) for p in range(2)] for u in U]
            dws = [[lax.dot_general(dos[p], kv[u][1][p], NT, preferred_element_type=f32) for p in range(2)] for u in U]
            parts = [[_log1m_parts(z) for z in zu] for zu in zs]
            Ls = [[jnp.where(strict2, lt[0], 0.0) if diags[u] else lt[0] for lt in parts[u]] for u in U]
            pins = [[_split_dot(L, upto2) for L in Lu] for Lu in Ls]
            carry = list(carry)
            ws, das = [], []
            for u in U:
                wu, dau = [], []
                for p in range(2):
                    PL0, PL1 = carry[5 * p], carry[5 * p + 1]
                    tail = _two_halves(lts[2 * p] - PL0, lts[2 * p + 1] - PL1) - pins[u][p]
                    w = jnp.exp(zs[u][p] + Ls[u][p] + tail)
                    l0, l1 = _half_sums(Ls[u][p])
                    if diags[u]:
                        w = jnp.where(strict2, w, 0.0)
                    else:
                        w, l0, l1 = (jnp.where(ok[u], t, 0.0) for t in (w, l0, l1))
                    carry[5 * p], carry[5 * p + 1] = PL0 + l0, PL1 + l1
                    wu.append(w)
                    dau.append(w * dws[u][p])
                ws.append(wu)
                das.append(dau)
            pexs = [[_split_dot(da, below2) for da in dau] for dau in das]
            dzs = []
            for u in U:
                dzu = []
                for p in range(2):
                    dL = _two_halves(carry[5 * p + 2], carry[5 * p + 3]) + pexs[u][p]
                    sg = _sigmoid_from(zs[u][p], parts[u][p][1])
                    dz = das[u][p] * (1.0 - sg) - dL * sg
                    dz = jnp.where(strict2 if diags[u] else ok[u], dz, 0.0)
                    a0, a1 = _half_sums(das[u][p])
                    carry[5 * p + 2], carry[5 * p + 3] = carry[5 * p + 2] + a0, carry[5 * p + 3] + a1
                    dzu.append(_mx(dz))
                dzs.append(dzu)
            dqs = [[jnp.dot(dzs[u][p], kv[u][0][p], preferred_element_type=f32) for p in range(2)] for u in U]
            dks = [[lax.dot_general(dzs[u][p], qs[p], TN, preferred_element_type=f32) for p in range(2)] for u in U]
            dvs = [[lax.dot_general(_mx(ws[u][p]), dos[p], TN, preferred_element_type=f32) for p in range(2)] for u in U]
            for u in U:
                krows = pl.ds(pl.multiple_of(kbs[u] * BLK, BLK), BLK)
                for p in range(2):
                    lanes = slice(p * BLK, (p + 1) * BLK)
                    dka_ref[krows, lanes] += jnp.where(head0, dks[u][p][:BLK], dks[u][p][BLK:])
                    dva_ref[krows, lanes] += jnp.where(head0, dvs[u][p][:BLK], dvs[u][p][BLK:])
                    carry[5 * p + 4] = carry[5 * p + 4] + dqs[u][p]
            return tuple(carry)

        def qblock(n, c):
            qrows = pl.ds(pl.multiple_of(n * BLK, BLK), BLK)
            ltb = lt_ref[qrows, :]
            lts = [jnp.sum(jnp.where(ci == h, ltb, 0.0), axis=-1, keepdims=True) for h in range(4)]
            qs = [_mx(q_ref[qrows, p * BLK:(p + 1) * BLK] * (HD ** -0.5)) for p in range(2)]
            dos = [_mx(dy_ref[qrows, p * BLK:(p + 1) * BLK]) for p in range(2)]
            z1, z2 = jnp.zeros((BLK, 1), f32), jnp.zeros((BLK, BLK), f32)
            done = jnp.max(jnp.where(ci == SB_HEADS, ltb, 0.0)).astype(jnp.int32)
            far = jnp.maximum(n - SB_UNROLL, 0)
            first = jnp.maximum(far - SB_UNROLL * done, 0)

            def trip(i, cr):
                kbs = [first + SB_UNROLL * i + u for u in range(SB_UNROLL)]
                return step(qs, dos, lts, kbs, [kb < far for kb in kbs], [False] * SB_UNROLL, n, cr)
            carry = lax.fori_loop(0, (far - first + SB_UNROLL - 1) // SB_UNROLL, trip, (z1, z1, z1, z1, z2) * 2)
            near = [n - SB_UNROLL + u for u in range(SB_UNROLL)]
            res = step(qs, dos, lts, near + [n], [kb >= 0 for kb in near] + [None], [False] * SB_UNROLL + [True], n, carry)
            for p in range(2):
                dq_ref[qrows, p * BLK:(p + 1) * BLK] = (res[5 * p + 4] * (HD ** -0.5)).astype(bf16)
            return c
        lax.fori_loop(0, S // BLK, qblock, 0)
        dk_ref[...] = dka_ref[...].astype(bf16)
        dv_ref[...] = dva_ref[...].astype(bf16)

    spec = lambda j: pl.BlockSpec((S, GW), lambda b: (b, j))
    o = pl.BlockSpec((S, GW), lambda b: (b, 0))
    c_args, c_in, c_out, c_shapes, aliases, c_scratch = _host_specs(comm, 5, 3)
    step = lambda v: (lambda: pl.program_id(0) == v)
    stacked = pltpu.VMEM((2, S // BLK, 2 * BLK, BLK), _MXU)
    return pl.pallas_call(
        _host(body, 5, 3, 4, comm, step(0), step(B - 1)), grid=(B,),
        in_specs=[spec(5), spec(6), spec(7), pl.BlockSpec((S, BLK), lambda b: (b, 0)), o] + c_in,
        out_specs=[o, o, o] + c_out,
        out_shape=[SDS((B * S, GW), bf16)] * 3 + c_shapes,
        input_output_aliases=aliases,
        scratch_shapes=[pltpu.VMEM((S, GW), f32), pltpu.VMEM((S, GW), f32), stacked, stacked] + c_scratch,
        name="sb_bwd" if comm is None else "sb_bwd_exchange",
        compiler_params=_cp("arbitrary"))(proj, proj, proj, ltot, dy, *c_args)


def _bias_expand(rel_bias_t, bucket):
    n = bucket.shape[1]

    def body(r_ref, b_ref, o_ref):
        onehot = (_iota((N_BUCKETS, n), 0) == b_ref[...]).astype(f32)
        o_ref[...] = jnp.dot(r_ref[...], onehot, precision=HIGHEST, preferred_element_type=f32)
    return pl.pallas_call(body, out_shape=SDS((rel_bias_t.shape[0], n), f32), name="bias_expand",
                          compiler_params=_cp())(rel_bias_t, bucket)


def _bias_reduce(dbias, bucket):
    n = bucket.shape[1]

    def body(*refs):
        b_ref, g_ref = refs[-2], refs[-1]
        d = refs[0][...]
        for r in refs[1:-2]:
            d = d + r[...]
        onehot = (_iota((N_BUCKETS, n), 0) == b_ref[...]).astype(f32)
        g_ref[...] = lax.dot_general(d, onehot, NT, precision=HIGHEST, preferred_element_type=f32)
    return pl.pallas_call(body, out_shape=SDS((dbias[0].shape[0], N_BUCKETS), f32), name="bias_reduce",
                          compiler_params=_cp())(*dbias, bucket)


def _adamw(w, g, m, v, tr, name, emit_g=False):
    R, C = w.shape

    def body(w_ref, g_ref, m_ref, v_ref, d_ref, m2_ref, v2_ref, *g_out):
        gv = g_ref[...]
        if emit_g:
            g_out[0][...] = gv
        m2 = ADAM_B1 * m_ref[...] + (1.0 - ADAM_B1) * gv
        v2 = ADAM_B2 * v_ref[...] + (1.0 - ADAM_B2) * (gv * gv)
        m_hat = m2 / (1.0 - ADAM_B1 ** ADAM_STEP)
        v_hat = v2 / (1.0 - ADAM_B2 ** ADAM_STEP)
        d_ref[...] = -ADAM_LR * (m_hat / (jnp.sqrt(v_hat) + ADAM_EPS) + ADAM_WD * w_ref[...])
        m2_ref[...] = m2
        v2_ref[...] = v2

    spec = pl.BlockSpec((tr, C), lambda i: (i, 0))
    n_out = 4 if emit_g else 3
    return pl.pallas_call(
        body, grid=(R // tr,), in_specs=[spec] * 4, out_specs=[spec] * n_out,
        out_shape=[SDS((R, C), f32)] * n_out, name=name, compiler_params=_cp("parallel"))(w, g, m, v)


ANY = pl.BlockSpec(memory_space=pl.ANY)


def _place():
    x, y, c = lax.axis_index("x"), lax.axis_index("y"), lax.axis_index("c")
    chips = [(1 - x, y), (x, 1 - y), (1 - x, 1 - y)]
    return x, y, c, chips


def _cast_slots(w, kidx):
    L, a, b = w.shape
    ta = a // 2

    def body(k_ref, *refs):
        for l in range(L):
            refs[L + l][0] = refs[l][0].astype(bf16)

    return pl.pallas_call(
        body,
        grid_spec=pltpu.PrefetchScalarGridSpec(
            num_scalar_prefetch=1, grid=(a // ta,),
            in_specs=[pl.BlockSpec((1, ta, b), functools.partial(lambda i, k_ref, l: (l, i, 0), l=l)) for l in range(L)],
            out_specs=[pl.BlockSpec((1, ta, b), lambda i, k_ref: (k_ref[0], i, 0)) for _ in range(L)]),
        out_shape=[SDS((N_CHIPS, a, b), bf16)] * L,
        name="cast_slots", compiler_params=_cp("parallel"))(kidx, *([w] * L))


class _GatherComm:
    def __init__(self, bufs, part=0, parts=1):
        self.inputs = list(bufs)
        self.out_shape = [SDS(b.shape, b.dtype) for b in bufs]
        self.aliased = True
        self.scratch = [pltpu.SemaphoreType.DMA((3 * len(bufs),))] * 4
        self.part, self.parts = part, parts

    def _copies(self, i_refs, o_refs, sems):
        send1, recv1, send2, recv2 = sems
        x, y, c, chips = _place()
        k = 2 * x + y
        first, got1, second, got2 = [], [], [], []
        for i, buf in enumerate(self.inputs):
            h = buf.shape[1] // 2
            n = h // self.parts
            mine, theirs = pl.ds(c * h + self.part * n, n), pl.ds((1 - c) * h + self.part * n, n)
            for j, (cx, cy) in enumerate(chips):
                s = 3 * i + j
                first.append(pltpu.make_async_remote_copy(
                    src_ref=i_refs[i].at[k, mine], dst_ref=o_refs[i].at[k, mine], send_sem=send1.at[s],
                    recv_sem=recv1.at[s], device_id=(cx, cy, c), device_id_type=MESH))
                a = o_refs[i].at[2 * cx + cy, mine]
                got1.append(pltpu.make_async_remote_copy(
                    src_ref=a, dst_ref=a, send_sem=send1.at[s], recv_sem=recv1.at[s],
                    device_id=(cx, cy, c), device_id_type=MESH))
                second.append(pltpu.make_async_remote_copy(
                    src_ref=a, dst_ref=a, send_sem=send2.at[s], recv_sem=recv2.at[s],
                    device_id=(x, y, 1 - c), device_id_type=MESH))
                b = o_refs[i].at[2 * cx + cy, theirs]
                got2.append(pltpu.make_async_remote_copy(
                    src_ref=b, dst_ref=b, send_sem=send2.at[s], recv_sem=recv2.at[s],
                    device_id=(x, y, 1 - c), device_id_type=MESH))
        return first, got1, second, got2

    def start(self, i_refs, o_refs, sems):
        for cp in self._copies(i_refs, o_refs, sems)[0]:
            cp.start()

    def finish(self, i_refs, o_refs, sems):
        first, got1, second, got2 = self._copies(i_refs, o_refs, sems)
        for g, cp in zip(got1, second):
            g.wait_recv()
            cp.start()
        for g in got2:
            g.wait_recv()
        for cp in first + second:
            cp.wait_send()


class _MultiComm:
    def __init__(self, comms):
        self.comms = comms
        self.inputs = [a for c in comms for a in c.inputs]
        self.out_shape = [s for c in comms for s in c.out_shape]
        self.aliased = comms[0].aliased
        assert all(c.aliased == self.aliased for c in comms)
        self.scratch = [s for c in comms for s in c.scratch]

    def _split(self, i_refs, o_refs, sems):
        i = o = s = 0
        for c in self.comms:
            ni, no, ns = len(c.inputs), len(c.out_shape), len(c.scratch)
            yield c, i_refs[i:i + ni], o_refs[o:o + no], sems[s:s + ns]
            i, o, s = i + ni, o + no, s + ns

    def start(self, i_refs, o_refs, sems):
        for c, i, o, s in self._split(i_refs, o_refs, sems):
            c.start(i, o, s)

    def finish(self, i_refs, o_refs, sems):
        for c, i, o, s in self._split(i_refs, o_refs, sems):
            c.finish(i, o, s)


class _PairExchangeComm:
    def __init__(self, gs):
        self.inputs = list(gs)
        self.out_shape = [SDS((g.shape[0], g.shape[1] // 2, g.shape[2]), g.dtype) for g in gs]
        self.aliased = False
        self.scratch = [pltpu.SemaphoreType.DMA((len(gs),))] * 2

    def _copies(self, i_refs, o_refs, sems):
        send, recv = sems
        x, y, c, _ = _place()
        cps = []
        for i, g in enumerate(self.inputs):
            h = g.shape[1] // 2
            cps.append(pltpu.make_async_remote_copy(
                src_ref=i_refs[i].at[:, pl.ds((1 - c) * h, h)], dst_ref=o_refs[i], send_sem=send.at[i], recv_sem=recv.at[i],
                device_id=(x, y, 1 - c), device_id_type=MESH))
        return cps

    def start(self, i_refs, o_refs, sems):
        for cp in self._copies(i_refs, o_refs, sems):
            cp.start()

    def finish(self, i_refs, o_refs, sems):
        for cp in self._copies(i_refs, o_refs, sems):
            cp.wait()


class _ChipExchangeComm:
    def __init__(self, qs):
        self.inputs = list(qs)
        self.out_shape = [SDS(q.shape, q.dtype) for q in qs]
        self.aliased = False
        self.scratch = [pltpu.SemaphoreType.DMA((3 * len(qs),))] * 2

    def _copies(self, i_refs, o_refs, sems):
        send, recv = sems
        x, y, c, chips = _place()
        k = 2 * x + y
        cps, got = [], []
        for i in range(len(self.inputs)):
            for j, (cx, cy) in enumerate(chips):
                s = 3 * i + j
                cps.append(pltpu.make_async_remote_copy(
                    src_ref=i_refs[i].at[2 * cx + cy], dst_ref=o_refs[i].at[k], send_sem=send.at[s],
                    recv_sem=recv.at[s], device_id=(cx, cy, c), device_id_type=MESH))
                a = o_refs[i].at[2 * cx + cy]
                got.append(pltpu.make_async_remote_copy(
                    src_ref=a, dst_ref=a, send_sem=send.at[s], recv_sem=recv.at[s],
                    device_id=(cx, cy, c), device_id_type=MESH))
        return cps, got

    def start(self, i_refs, o_refs, sems):
        for cp in self._copies(i_refs, o_refs, sems)[0]:
            cp.start()

    def finish(self, i_refs, o_refs, sems):
        cps, got = self._copies(i_refs, o_refs, sems)
        for g in got:
            g.wait_recv()
        for cp in cps:
            cp.wait_send()


def _comm_only(comm, name):
    n = len(comm.inputs)

    def body(*refs):
        i_refs, o_refs, sems = refs[:n], refs[n:n + len(comm.out_shape)], refs[n + len(comm.out_shape):]
        comm.start(i_refs, o_refs, sems)
        comm.finish(i_refs, o_refs, sems)

    return pl.pallas_call(
        body, out_shape=comm.out_shape, in_specs=[ANY] * n, out_specs=[ANY] * len(comm.out_shape),
        input_output_aliases={i: i for i in range(n)} if comm.aliased else {},
        scratch_shapes=comm.scratch, name=name,
        compiler_params=pltpu.CompilerParams(has_side_effects=True))(*comm.inputs)


def _host(body, n_in, n_out, n_scratch, comm, first, last):
    if comm is None:
        return body
    ci, co = len(comm.inputs), len(comm.out_shape)

    def wrapped(*refs):
        o = 0
        parts = []
        for n in (n_in, ci, n_out, co, n_scratch):
            parts.append(refs[o:o + n])
            o += n
        hin, cin, hout, cout, hs = parts
        sems = refs[o:]

        @pl.when(first())
        def _():
            comm.start(cin, cout, sems)
        body(*hin, *hout, *hs)

        @pl.when(last())
        def _():
            comm.finish(cin, cout, sems)
    return wrapped


def _host_specs(comm, n_in, n_out):
    if comm is None:
        return [], [], [], [], {}, []
    ci, co = len(comm.inputs), len(comm.out_shape)
    aliases = {n_in + i: n_out + i for i in range(ci)} if comm.aliased else {}
    return comm.inputs, [ANY] * ci, [ANY] * co, comm.out_shape, aliases, comm.scratch


def _pair_add(g, r, cidx, name):
    ns, a, b = g.shape
    h = a // 2
    th = h if h * b * 4 <= 4 * 1024 * 1024 else h // 2

    def body(c_ref, g_ref, r_ref, qb_ref):
        qb_ref[...] = (g_ref[...] + r_ref[...]).astype(bf16)

    nb = h // th
    spec = pl.BlockSpec((1, th, b), lambda s, i, c_ref: (s, i, 0))
    return pl.pallas_call(
        body,
        grid_spec=pltpu.PrefetchScalarGridSpec(
            num_scalar_prefetch=1, grid=(ns, nb),
            in_specs=[pl.BlockSpec((1, th, b), lambda s, i, c_ref: (s, c_ref[0] * nb + i, 0)), spec],
            out_specs=spec),
        out_shape=SDS((ns, h, b), bf16),
        name=name, compiler_params=_cp("parallel", "parallel"))(cidx, g, r)


def _chip_add(g, r1, r2, idx, prev, L, name):
    ns, h, b = r2.shape
    th = h if h * b * 4 <= 4 * 1024 * 1024 else h // 2
    nb = h // th

    def body(s_ref, g_ref, r1_ref, a_ref, b_ref, c_ref, *rest):
        o_ref = rest[-1]
        o_ref[0] = (g_ref[0] + r1_ref[0]) + a_ref[0].astype(f32) + b_ref[0].astype(f32) + c_ref[0].astype(f32)

    other = lambda d: pl.BlockSpec((1, th, b), lambda i, s_ref: ((s_ref[0] + d) % ns, i, 0))
    in_specs = [pl.BlockSpec((1, th, b), lambda i, s_ref: (s_ref[0], s_ref[1] * nb + i, 0)),
                pl.BlockSpec((1, th, b), lambda i, s_ref: (s_ref[0], i, 0)), other(1), other(2), other(3)]
    args = [idx, g, r1, r2, r2, r2]
    aliases = {}
    if prev is not None:
        in_specs.append(ANY)
        args.append(prev)
        aliases = {6: 0}
    return pl.pallas_call(
        body,
        grid_spec=pltpu.PrefetchScalarGridSpec(
            num_scalar_prefetch=1, grid=(nb,), in_specs=in_specs,
            out_specs=pl.BlockSpec((1, th, b), lambda i, s_ref: (s_ref[2], s_ref[1] * nb + i, 0))),
        out_shape=SDS((L, 2 * h, b), f32), input_output_aliases=aliases,
        name=name, compiler_params=_cp("arbitrary"))(*args)


def _pair_share(gs, hs):
    n = len(gs)
    L = gs[0].shape[0]

    def body(*refs):
        i_refs, o_refs = refs[:n], refs[n:2 * n]
        send, recv = refs[2 * n:]
        x, y, c, _ = _place()
        cps = []
        for i in range(n):
            for l in range(L):
                mine = pl.ds(c * hs[i], hs[i])
                cp = pltpu.make_async_remote_copy(
                    src_ref=i_refs[i].at[l, mine], dst_ref=o_refs[i].at[l, mine], send_sem=send.at[i * L + l],
                    recv_sem=recv.at[i * L + l], device_id=(x, y, 1 - c), device_id_type=MESH)
                cp.start()
                cps.append(cp)
        for i in range(n):
            for l in range(L):
                got = o_refs[i].at[l, pl.ds((1 - c) * hs[i], hs[i])]
                pltpu.make_async_remote_copy(
                    src_ref=got, dst_ref=got, send_sem=send.at[i * L + l], recv_sem=recv.at[i * L + l],
                    device_id=(x, y, 1 - c), device_id_type=MESH).wait_recv()
        for cp in cps:
            cp.wait_send()

    return pl.pallas_call(
        body, out_shape=[SDS(g.shape, g.dtype) for g in gs], in_specs=[ANY] * n, out_specs=[ANY] * n,
        input_output_aliases={i: i for i in range(n)},
        scratch_shapes=[pltpu.SemaphoreType.DMA((n * L,))] * 2,
        name="grad_pair_share", compiler_params=pltpu.CompilerParams(has_side_effects=True))(*gs)


class _SwapComm:
    def __init__(self, arrays):
        self.inputs = list(arrays)
        self.out_shape = [SDS(a.shape, a.dtype) for a in arrays]
        self.aliased = False
        self.scratch = [pltpu.SemaphoreType.DMA((len(arrays),))] * 2

    def _copies(self, i_refs, o_refs, sems):
        send, recv = sems
        x, y, c, _ = _place()
        return [pltpu.make_async_remote_copy(
            src_ref=i_refs[i], dst_ref=o_refs[i], send_sem=send.at[i], recv_sem=recv.at[i],
            device_id=(x, y, 1 - c), device_id_type=MESH) for i in range(len(self.inputs))]

    def start(self, i_refs, o_refs, sems):
        for cp in self._copies(i_refs, o_refs, sems):
            cp.start()

    def finish(self, i_refs, o_refs, sems):
        for cp in self._copies(i_refs, o_refs, sems):
            cp.wait()


class _SlotShareComm:
    def __init__(self, bufs):
        self.inputs = list(bufs)
        self.out_shape = [SDS(b.shape, b.dtype) for b in bufs]
        self.aliased = True
        self.scratch = [pltpu.SemaphoreType.DMA((3 * len(bufs),))] * 2

    def _copies(self, i_refs, o_refs, sems):
        send, recv = sems
        x, y, c, chips = _place()
        k = 2 * x + y
        cps, got = [], []
        for i in range(len(self.inputs)):
            for j, (cx, cy) in enumerate(chips):
                s = 3 * i + j
                cps.append(pltpu.make_async_remote_copy(
                    src_ref=i_refs[i].at[k], dst_ref=o_refs[i].at[k], send_sem=send.at[s], recv_sem=recv.at[s],
                    device_id=(cx, cy, c), device_id_type=MESH))
                a = o_refs[i].at[2 * cx + cy]
                got.append(pltpu.make_async_remote_copy(
                    src_ref=a, dst_ref=a, send_sem=send.at[s], recv_sem=recv.at[s],
                    device_id=(cx, cy, c), device_id_type=MESH))
        return cps, got

    def start(self, i_refs, o_refs, sems):
        for cp in self._copies(i_refs, o_refs, sems)[0]:
            cp.start()

    def finish(self, i_refs, o_refs, sems):
        cps, got = self._copies(i_refs, o_refs, sems)
        for g in got:
            g.wait_recv()
        for cp in cps:
            cp.wait_send()


def _pair_sum_slot(mine, theirs, kidx, dtype):
    R, C = mine.shape

    def body(k_ref, a_ref, b_ref, o_ref):
        o_ref[0] = (a_ref[...] + b_ref[...]).astype(dtype)

    spec = pl.BlockSpec((R, C), lambda i, k_ref: (0, 0))
    return pl.pallas_call(
        body,
        grid_spec=pltpu.PrefetchScalarGridSpec(
            num_scalar_prefetch=1, grid=(1,), in_specs=[spec, spec],
            out_specs=pl.BlockSpec((1, R, C), lambda i, k_ref: (k_ref[0], 0, 0))),
        out_shape=SDS((N_CHIPS, R, C), dtype), name="small_pair_sum", compiler_params=_cp("arbitrary"))(kidx, mine, theirs)


def _small_sum(g):
    n, R, C = g.shape

    def body(g_ref, o_ref):
        acc = g_ref[0].astype(f32)
        for j in range(1, n):
            acc = acc + g_ref[j].astype(f32)
        o_ref[...] = acc
    return pl.pallas_call(body, out_shape=SDS((R, C), f32), name="small_sum", compiler_params=_cp())(g)


PACK_COLS = 1024


def _rows_of(shape):
    n = int(np.prod(shape)) if len(shape) else 1
    return -(-n // (8 * PACK_COLS)) * 8


def _pack(parts):
    blocks = []
    for p in parts:
        flat = p.reshape(-1)
        r = _rows_of(p.shape)
        blocks.append(jnp.pad(flat, (0, r * PACK_COLS - flat.shape[0])).reshape(r, PACK_COLS))
    return jnp.concatenate(blocks, axis=0)


def _unpack(buf, shapes):
    out, off = [], 0
    for s in shapes:
        n = int(np.prod(s)) if len(s) else 1
        r = _rows_of(s)
        out.append(buf[off:off + r].reshape(-1)[:n].reshape(s))
        off += r
    return out


def _block_diag(w):
    g, a, _ = w.shape
    out = jnp.zeros((g * a, g * a), w.dtype)
    for i in range(g):
        out = lax.dynamic_update_slice(out, w[i], (i * a, i * a))
    return out


def kernel(x, w_in, w_out, sgu_w, sgu_b, pool_w, pool_scale, swa_sinks, rel_bias, mix_out_gain, norm_mix, norm_ffn, w_gate_up, w_down, norm_final, loss_target, m_w_in, m_w_out, m_sgu_w, m_sgu_b, m_pool_w, m_pool_scale, m_swa_sinks, m_rel_bias, m_mix_out_gain, m_norm_mix, m_norm_ffn, m_w_gate_up, m_w_down, m_norm_final, v_w_in, v_w_out, v_sgu_w, v_sgu_b, v_pool_w, v_pool_scale, v_swa_sinks, v_rel_bias, v_mix_out_gain, v_norm_mix, v_norm_ffn, v_w_gate_up, v_w_down, v_norm_final):
    B, S, D = x.shape
    T = B * S
    L = w_in.shape[0]
    tm = min(512, T)
    F = w_down.shape[1] * N_CHIPS
    xi, yi, ci = lax.axis_index("x"), lax.axis_index("y"), lax.axis_index("c")
    cidx = jnp.reshape(ci, (1,)).astype(jnp.int32)
    kidx = jnp.reshape(2 * xi + yi, (1,)).astype(jnp.int32)

    big = [w_in, w_out, w_gate_up, w_down]
    slots = [_cast_slots(w, kidx) for w in big]
    gather = lambda pi, l, part=0, parts=1: _GatherComm([slots[pi][l]], part, parts)
    Win, Wo, Wgu, Wd = ([None] * L for _ in range(4))
    Win[0], = _comm_only(gather(0, 0), "gather_weights")

    bucket = jnp.asarray(_t5_bucket_table().reshape(1, -1))
    bias_tab = _bias_expand(rel_bias.T, bucket).reshape(4, BLK, 2 * BLK)

    row = lambda v: v.reshape(1, -1)
    xc = x.reshape(T, D)
    tgt = loss_target.reshape(T, D)
    saved = []
    for l in range(L):
        bexp = jnp.repeat(sgu_b[l].T, HD, axis=1)
        wbd = _block_diag(pool_w[l])
        sk = jnp.broadcast_to(swa_sinks[l][:, None, None], (4, 1, BLK))
        h1, proj, wo = _norm_mm(xc, row(norm_mix[l]), Win[l], tm, gather(1, l))
        ya = _sgu_fwd(proj, sgu_w[l], bexp, B, S)
        yb = _pool_fwd(proj, wbd, row(pool_scale[l]), B, S)
        yc, wd = _swa_fwd(proj, sk, bias_tab, B, S, gather(3, l, 0, 2))
        if l == 0:
            yd, lt, wd, wg = _sb_fwd(proj, B, S, _MultiComm([_GatherComm([wd], 1, 2), gather(2, 0, 0, 2)]))
        else:
            yd, lt, wd = _sb_fwd(proj, B, S, _GatherComm([wd], 1, 2))
        Wo[l], Wd[l] = wo.reshape(D, D), wd.reshape(F, D)
        ys = (ya, yb, yc, yd)
        if l == 0:
            ycn, x1, Wgu[0] = _gnorm_mm_res(ys, row(mix_out_gain[0]), Wo[0], xc, tm, _GatherComm([wg], 1, 2))
        else:
            ycn, x1 = _gnorm_mm_res(ys, row(mix_out_gain[l]), Wo[l], xc, tm)
        if l + 1 < L:
            h2, gu, act, Wgu[l + 1] = _norm_mm_swiglu(x1, row(norm_ffn[l]), Wgu[l], tm, gather(2, l + 1))
            x2, Win[l + 1] = _mm_res(act, Wd[l], x1, tm, gather(0, l + 1))
        else:
            h2, gu, act = _norm_mm_swiglu(x1, row(norm_ffn[l]), Wgu[l], tm)
            x2, = _mm_res(act, Wd[l], x1, tm)
        saved.append((xc, h1, proj, bexp, wbd, sk, ys, lt, ycn, x1, h2, gu, act))
        xc = x2

    dx, g_final, loss_v = _final_loss(xc, row(norm_final), tgt, tm)

    tk = min(T, 2048)
    gW = [[None] * L for _ in range(4)]
    g_sgu_w, g_sgu_b, g_pool_w, g_pool_scale, g_sinks, g_bias = ([None] * L for _ in range(6))
    g_out_gain, g_mix, g_ffn = ([None] * L for _ in range(3))
    reduced = [None] * 4
    sums = {}

    def pair_comm(keys):
        return _PairExchangeComm([gW[pi][l] for pi, l in keys])

    def after_pair(keys, r1):
        for (pi, l), r in zip(keys, r1):
            sums[pi, l] = (r, _pair_add(gW[pi][l], r, cidx, "grad_pair_add"))

    def chip_comm(keys):
        return _ChipExchangeComm([sums[k][1] for k in keys])

    def after_chip(keys, r2):
        for (pi, l), r in zip(keys, r2):
            idx = jnp.stack([2 * xi + yi, ci, jnp.int32(l)]).astype(jnp.int32)
            reduced[pi] = _chip_add(gW[pi][l], sums.pop((pi, l))[0], r, idx, reduced[pi], L, "grad_chip_add")

    for l in reversed(range(L)):
        x0, h1, proj, bexp, wbd, sk, ys, lt, ycn, x1, h2, gu, act = saved[l]
        keys = [(0, l + 1), (1, l + 1)]
        comm = chip_comm(keys) if l + 1 < L else None
        dgu, dx1, g_ffn[l], *r2 = _dact_dx(dx, Wd[l], gu, Wgu[l], x1, row(norm_ffn[l]), tm // 2, comm)
        if comm is not None:
            after_chip(keys, r2)
        gW[3][l] = _dw(act, dx, lambda t, s: (t, 0), D, 1, F // 2, tk // 2, "dw_down").reshape(N_CHIPS, F // N_CHIPS, D)
        gW[2][l] = _dw(h2, dgu, lambda t, s: (s // 2, t, s % 2), F // 2, N_CHIPS, D, tk, "dw_gate_up")
        gW[1][l] = _dw(ycn, dx1, lambda t, s: (t, 0), D, 1, D, tk, "dw_out").reshape(N_CHIPS, D // N_CHIPS, D)
        dya, dyb, dyc, dyd, g_out_gain[l] = _dycat(dx1, Wo[l], ys, row(mix_out_gain[l]), tm)
        keys = [(2, l), (3, l)]
        dpa, g_sgu_w[l], dbf, *r1 = _sgu_bwd(proj, sgu_w[l], bexp, dya, B, S, pair_comm(keys))
        after_pair(keys, r1)
        g_sgu_b[l] = dbf[:, ::HD].T
        dpb, dwbd, dsc = _pool_bwd(proj, wbd, row(pool_scale[l]), dyb, B, S)
        npg = len(POOL_WINDOWS)
        g_pool_w[l] = jnp.stack([dwbd[i * HD:(i + 1) * HD, i * HD:(i + 1) * HD] for i in range(npg)])
        g_pool_scale[l] = dsc[0]
        dcq, dckv, dsk, g_bias[l], *r2 = _swa_bwd(proj, sk, bias_tab, dyc, B, S, chip_comm([(3, l)]))
        after_chip([(3, l)], r2)
        g_sinks[l] = dsk[:, 0, 0] * float(BLK)
        ddq, ddk, ddv, *r2 = _sb_bwd(proj, lt, dyd, B, S, chip_comm([(2, l)]))
        after_chip([(2, l)], r2)
        dproj = [dpa, dpb, dcq, dckv, ddq, ddk, ddv]
        gW[0][l] = _dw_pieces(h1, dproj, w_in.shape[2], N_CHIPS, tk, "dw_in")
        keys = [(0, l), (1, l)]
        dx, g_mix[l], *r1 = _dx_norm_bwd(dproj, Win[l], x0, row(norm_mix[l]), dx1, tm, "dx_mix_exchange", pair_comm(keys))
        after_pair(keys, r1)
    grad_x = dx.reshape(B, S, D)

    after_chip(keys, _comm_only(chip_comm(keys), "grad_chip_exchange"))
    g_big = _pair_share(reduced, [g.shape[1] // 2 for g in reduced])

    g_rel_bias = _bias_reduce([g.reshape(4, -1) for g in g_bias], bucket).T
    small_g = [jnp.stack(g_sgu_w), jnp.stack(g_sgu_b), jnp.stack(g_pool_w), jnp.stack(g_pool_scale), jnp.stack(g_sinks),
               g_rel_bias, jnp.concatenate(g_out_gain), jnp.concatenate(g_mix), jnp.concatenate(g_ffn), g_final[0]]
    small_w = [sgu_w, sgu_b, pool_w, pool_scale, swa_sinks, rel_bias, mix_out_gain, norm_mix, norm_ffn, norm_final]
    small_m = [m_sgu_w, m_sgu_b, m_pool_w, m_pool_scale, m_swa_sinks, m_rel_bias, m_mix_out_gain, m_norm_mix, m_norm_ffn, m_norm_final]
    small_v = [v_sgu_w, v_sgu_b, v_pool_w, v_pool_scale, v_swa_sinks, v_rel_bias, v_mix_out_gain, v_norm_mix, v_norm_ffn, v_norm_final]
    shapes = [w.shape for w in small_w]
    bulk, fine = [small_g[0], small_g[2]], [small_g[i] for i in (1, 3, 4, 5, 6, 7, 8, 9)] + [loss_v[0, 0:1]]
    mine = [_pack(bulk), _pack(fine)]
    theirs = _comm_only(_SwapComm(mine), "small_pair_swap")
    slots_s = [_pair_sum_slot(a, b, kidx, dt) for a, b, dt in zip(mine, theirs, (bf16, f32))]
    shared = _comm_only(_SlotShareComm(slots_s), "small_chip_share")
    g_bulk = _unpack(_small_sum(shared[0]), [shapes[0], shapes[2]])
    *g_fine, loss = _unpack(_small_sum(shared[1]), [shapes[i] for i in (1, 3, 4, 5, 6, 7, 8, 9)] + [()])
    g_small = [g_bulk[0], g_fine[0], g_bulk[1]] + g_fine[1:]

    big_m = [m_w_in, m_w_out, m_w_gate_up, m_w_down]
    big_v = [v_w_in, v_w_out, v_w_gate_up, v_w_down]
    g_out, d_big, m_big, v_big = [], [], [], []
    for w, g, m, v in zip(big, g_big, big_m, big_v):
        two = lambda a: a.reshape(-1, a.shape[-1])
        rows = two(w).shape[0]
        cap = max(8, (1 << 20) // (4 * w.shape[-1]))
        tr = max(t for t in range(8, min(rows, cap) + 1, 8) if rows % t == 0)
        d2, m2, v2, g2 = _adamw(two(w), two(g), two(m), two(v), tr, "adamw_big", True)
        for lst, val in ((d_big, d2), (m_big, m2), (v_big, v2), (g_out, g2)):
            lst.append(val.reshape(w.shape))
    g_big = g_out

    g_small_packed = _pack(g_small)
    ds, ms, vs = _adamw(_pack(small_w), g_small_packed, _pack(small_m), _pack(small_v), g_small_packed.shape[0], "adamw_small")
    d_small, m_small, v_small = _unpack(ds, shapes), _unpack(ms, shapes), _unpack(vs, shapes)

    def order(bigs, smalls):
        return [bigs[0], bigs[1]] + list(smalls[0:9]) + [bigs[2], bigs[3], smalls[9]]

    return (loss, grad_x, *order(g_big, g_small), *order(d_big, d_small), *order(m_big, m_small), *order(v_big, v_small))
```

```python
import functools

import numpy as np
import jax
import jax.numpy as jnp
from jax import lax
from jax.experimental import pallas as pl
from jax.experimental.pallas import tpu as pltpu

f32 = jnp.float32
bf16 = jnp.bfloat16
_MXU = jnp.bfloat16

EPS = 1e-6
HD = 64
GW = 256
BLK = 128
SB_UNROLL = 2
SB_HEADS = 4
SB_CUT = -110.0
POOL_WINDOWS = (2, 4, 8, 16)
N_BUCKETS = 32
MAX_DISTANCE = 128
N_CHIPS = 4
N_DEV = 8
VMEM_LIMIT = 48 * 1024 * 1024

ADAM_LR = 0.001
ADAM_B1 = 0.9
ADAM_B2 = 0.999
ADAM_EPS = 1e-08
ADAM_WD = 0.01
ADAM_STEP = 10

SDS = jax.ShapeDtypeStruct
MESH = pl.DeviceIdType.MESH
HIGHEST = lax.Precision.HIGHEST
RESIDENT = pl.Buffered(1)
NT = (((1,), (1,)), ((), ()))
TN = (((0,), (0,)), ((), ()))


def _cp(*sem):
    return pltpu.CompilerParams(dimension_semantics=sem if sem else None, vmem_limit_bytes=VMEM_LIMIT)


def _mx(v):
    return v.astype(_MXU)


def _iota(shape, dim):
    return lax.broadcasted_iota(jnp.int32, shape, dim)


def _split_dot(a, tri):
    hi = a.astype(bf16)
    lo = (a - hi.astype(f32)).astype(bf16)
    return jnp.dot(hi, tri, preferred_element_type=f32) + jnp.dot(lo, tri, preferred_element_type=f32)


def _rms(xv):
    return lax.rsqrt(jnp.mean(xv * xv, axis=-1, keepdims=True) + EPS)


def _hosted_call(body, steps, in_specs, out_specs, out_shape, scratch, args, name, comm):
    n_in, n_out = len(in_specs), len(out_specs)
    c_args, c_in, c_out, c_shapes, aliases, c_scratch = _host_specs(comm, n_in, n_out)
    step = lambda v: (lambda: pl.program_id(0) == v)
    return pl.pallas_call(
        _host(body, n_in, n_out, len(scratch), comm, step(0), step(steps - 1)), grid=(steps,),
        in_specs=list(in_specs) + c_in, out_specs=list(out_specs) + c_out, out_shape=list(out_shape) + c_shapes,
        input_output_aliases=aliases, scratch_shapes=list(scratch) + c_scratch,
        name=name if comm is None else name + "_comm", compiler_params=_cp("arbitrary"))(*args, *c_args)


def _norm_mm(x, gain, w, tm, comm=None):
    T, D = x.shape
    NS, _, ns = w.shape

    def body(x_ref, g_ref, w_ref, h_ref, o_ref):
        xv = x_ref[...]
        h = (xv * _rms(xv) * g_ref[...]).astype(bf16)
        h_ref[...] = h
        for s in range(NS):
            o_ref[:, s * ns:(s + 1) * ns] = jnp.dot(_mx(h), w_ref[s], preferred_element_type=f32).astype(bf16)

    return _hosted_call(
        body, T // tm,
        [pl.BlockSpec((tm, D), lambda i: (i, 0)),
         pl.BlockSpec((1, D), lambda i: (0, 0)),
         pl.BlockSpec((NS, D, ns), lambda i: (0, 0, 0), pipeline_mode=RESIDENT)],
        [pl.BlockSpec((tm, D), lambda i: (i, 0)), pl.BlockSpec((tm, NS * ns), lambda i: (i, 0))],
        [SDS((T, D), bf16), SDS((T, NS * ns), bf16)], [], (x, gain, w), "norm_mm_in", comm)


def _mix_out_swiglu(ys, gain_mix, wo, x, gain_ffn, w, tm, comm=None):
    T, D = x.shape
    NS, _, ns = w.shape
    half = NS // 2

    def body(ya, yb, yc, yd, gm_ref, wo_ref, x_ref, gf_ref, w_ref, yn_ref, x1_ref, h_ref, gu_ref, a_ref):
        parts = []
        for m, r in enumerate((ya, yb, yc, yd)):
            y = r[...].astype(f32)
            parts.append((y * _rms(y) * gm_ref[:, m * GW:(m + 1) * GW]).astype(bf16))
        yn = jnp.concatenate(parts, axis=1)
        yn_ref[...] = yn
        x1 = x_ref[...] + jnp.dot(_mx(yn), wo_ref[...], preferred_element_type=f32)
        x1_ref[...] = x1
        hb = (x1 * _rms(x1) * gf_ref[...]).astype(bf16)
        h_ref[...] = hb
        h = _mx(hb)
        for s in range(half):
            cols = slice(s * ns, (s + 1) * ns)
            g = jnp.dot(h, w_ref[s], preferred_element_type=f32)
            u = jnp.dot(h, w_ref[s + half], preferred_element_type=f32)
            gu_ref[0, :, cols] = g.astype(bf16)
            gu_ref[1, :, cols] = u.astype(bf16)
            a_ref[:, cols] = (jax.nn.silu(g) * u).astype(bf16)

    yspec = pl.BlockSpec((tm, GW), lambda i: (i, 0))
    row, tile = pl.BlockSpec((1, D), lambda i: (0, 0)), pl.BlockSpec((tm, D), lambda i: (i, 0))
    return _hosted_call(
        body, T // tm,
        [yspec, yspec, yspec, yspec, row,
         pl.BlockSpec((D, D), lambda i: (0, 0), pipeline_mode=RESIDENT), tile, row,
         pl.BlockSpec((NS, D, ns), lambda i: (0, 0, 0), pipeline_mode=RESIDENT)],
        [tile, tile, tile, pl.BlockSpec((2, tm, half * ns), lambda i: (0, i, 0)),
         pl.BlockSpec((tm, half * ns), lambda i: (i, 0))],
        [SDS((T, D), bf16), SDS((T, D), f32), SDS((T, D), bf16), SDS((2, T, half * ns), bf16), SDS((T, half * ns), bf16)],
        [], (*ys, gain_mix, wo, x, gain_ffn, w), "mix_out_swiglu", comm)


def _mm_res(a, w, x, tm, comm=None):
    T, D = x.shape
    K = a.shape[1]

    def body(a_ref, w_ref, x_ref, o_ref):
        o_ref[...] = x_ref[...] + jnp.dot(_mx(a_ref[...]), w_ref[...], preferred_element_type=f32)

    return _hosted_call(
        body, T // tm,
        [pl.BlockSpec((tm, K), lambda i: (i, 0)),
         pl.BlockSpec((K, D), lambda i: (0, 0), pipeline_mode=RESIDENT),
         pl.BlockSpec((tm, D), lambda i: (i, 0))],
        [pl.BlockSpec((tm, D), lambda i: (i, 0))], [SDS((T, D), f32)], [], (a, w, x), "mm_res_down", comm)


def _final_loss(x, gain, tgt, tm):
    T, D = x.shape

    def body(x_ref, g_ref, t_ref, dx_ref, dg_ref, l_ref):
        @pl.when(pl.program_id(0) == 0)
        def _():
            dg_ref[...] = jnp.zeros_like(dg_ref)
            l_ref[...] = jnp.zeros_like(l_ref)
        xv = x_ref[...]
        g = g_ref[...]
        r = _rms(xv)
        xh = xv * r
        err = xh * g - t_ref[...]
        l_ref[...] += 0.5 * jnp.sum(jnp.mean(err * err, axis=-1, keepdims=True), axis=0, keepdims=True)
        dy = err * (1.0 / D)
        dg_ref[...] += jnp.sum(dy * xh, axis=0, keepdims=True)
        dxh = dy * g
        dx_ref[...] = r * (dxh - xh * jnp.mean(dxh * xh, axis=-1, keepdims=True))

    return pl.pallas_call(
        body, grid=(T // tm,),
        in_specs=[pl.BlockSpec((tm, D), lambda i: (i, 0)),
                  pl.BlockSpec((1, D), lambda i: (0, 0)),
                  pl.BlockSpec((tm, D), lambda i: (i, 0))],
        out_specs=[pl.BlockSpec((tm, D), lambda i: (i, 0)),
                   pl.BlockSpec((1, D), lambda i: (0, 0)),
                   pl.BlockSpec((1, BLK), lambda i: (0, 0))],
        out_shape=[SDS((T, D), f32), SDS((1, D), f32), SDS((1, BLK), f32)],
        name="final_loss", compiler_params=_cp("arbitrary"))(x, gain, tgt)


def _dact_dx(dx, wd, gu, w, x, gain, tm, comm=None):
    T, D = dx.shape
    F = wd.shape[0]
    NS, _, ns = w.shape
    half = NS // 2

    def body(dx_ref, wd_ref, gu_ref, w_ref, x_ref, g_ref, dgu_ref, dx1_ref, dg_ref):
        @pl.when(pl.program_id(0) == 0)
        def _():
            dg_ref[...] = jnp.zeros_like(dg_ref)
        dxv = dx_ref[...]
        dxb = _mx(dxv)
        dh = None
        for s in range(half):
            cols = slice(s * ns, (s + 1) * ns)
            da = lax.dot_general(dxb, wd_ref[s * ns:(s + 1) * ns, :], NT, preferred_element_type=f32)
            g = gu_ref[0, :, cols].astype(f32)
            u = gu_ref[1, :, cols].astype(f32)
            sg = jax.nn.sigmoid(g)
            dgs = (da * u * (sg * (1.0 + g * (1.0 - sg)))).astype(bf16)
            dus = (da * (g * sg)).astype(bf16)
            dgu_ref[0, :, cols] = dgs
            dgu_ref[1, :, cols] = dus
            part = (lax.dot_general(_mx(dgs), w_ref[s], NT, preferred_element_type=f32)
                    + lax.dot_general(_mx(dus), w_ref[s + half], NT, preferred_element_type=f32))
            dh = part if dh is None else dh + part
        xv = x_ref[...]
        r = _rms(xv)
        xh = xv * r
        dg_ref[...] += jnp.sum(dh * xh, axis=0, keepdims=True)
        dxh = dh * g_ref[...]
        dx1_ref[...] = dxv + r * (dxh - xh * jnp.mean(dxh * xh, axis=-1, keepdims=True))

    return _hosted_call(
        body, T // tm,
        [pl.BlockSpec((tm, D), lambda i: (i, 0)),
         pl.BlockSpec((F, D), lambda i: (0, 0), pipeline_mode=RESIDENT),
         pl.BlockSpec((2, tm, F), lambda i: (0, i, 0)),
         pl.BlockSpec((NS, D, ns), lambda i: (0, 0, 0), pipeline_mode=RESIDENT),
         pl.BlockSpec((tm, D), lambda i: (i, 0)),
         pl.BlockSpec((1, D), lambda i: (0, 0))],
        [pl.BlockSpec((2, tm, F), lambda i: (0, i, 0)), pl.BlockSpec((tm, D), lambda i: (i, 0)),
         pl.BlockSpec((1, D), lambda i: (0, 0))],
        [SDS((2, T, F), bf16), SDS((T, D), f32), SDS((1, D), f32)], [], (dx, wd, gu, w, x, gain), "dact_dx", comm)


def _dw(a, b, b_map, ns, NS, tka, tk, name):
    T, Ka = a.shape
    b_block = (tk, ns) if b.ndim == 2 else (1, tk, ns)

    def body(a_ref, b_ref, o_ref):
        bv = b_ref[...] if b.ndim == 2 else b_ref[0]
        part = lax.dot_general(_mx(a_ref[...]), _mx(bv), TN, preferred_element_type=f32)

        @pl.when(pl.program_id(2) == 0)
        def _():
            o_ref[0] = part

        @pl.when(pl.program_id(2) > 0)
        def _():
            o_ref[0] += part

    return pl.pallas_call(
        body, grid=(NS, Ka // tka, T // tk),
        in_specs=[pl.BlockSpec((tk, tka), lambda s, k, t: (t, k)),
                  pl.BlockSpec(b_block, lambda s, k, t: b_map(t, s))],
        out_specs=pl.BlockSpec((1, tka, ns), lambda s, k, t: (s, k, 0)),
        out_shape=SDS((NS, Ka, ns), f32),
        name=name, compiler_params=_cp("parallel", "parallel", "arbitrary"))(a, b)


def _dw_pieces(a, pieces, ns, NS, tk, name):
    T, Ka = a.shape
    n = len(pieces)

    def body(*refs):
        a_ref, b_refs, o_ref = refs[0], refs[1:1 + n], refs[1 + n]
        full = jnp.concatenate([r[...] for r in b_refs], axis=1)
        av = _mx(a_ref[...])
        parts = [lax.dot_general(av, _mx(full[:, s * ns:(s + 1) * ns]), TN, preferred_element_type=f32) for s in range(NS)]

        @pl.when(pl.program_id(0) == 0)
        def _():
            for s in range(NS):
                o_ref[s] = parts[s]

        @pl.when(pl.program_id(0) > 0)
        def _():
            for s in range(NS):
                o_ref[s] += parts[s]

    return pl.pallas_call(
        body, grid=(T // tk,),
        in_specs=[pl.BlockSpec((tk, Ka), lambda t: (t, 0))] + [pl.BlockSpec((tk, p.shape[1]), lambda t: (t, 0)) for p in pieces],
        out_specs=pl.BlockSpec((NS, Ka, ns), lambda t: (0, 0, 0)),
        out_shape=SDS((NS, Ka, ns), f32),
        name=name, compiler_params=_cp("arbitrary"))(a, *pieces)


def _dx_norm_bwd(pieces, w, x, gain, dxin, tm, name, comm=None):
    T, D = x.shape
    NS, _, ns = w.shape
    n_dy = len(pieces)

    def body(*refs):
        dy_refs = refs[:n_dy]
        w_ref, x_ref, g_ref, dxin_ref, dx_ref, dg_ref = refs[n_dy:]

        @pl.when(pl.program_id(0) == 0)
        def _():
            dg_ref[...] = jnp.zeros_like(dg_ref)
        full = jnp.concatenate([r[...] for r in dy_refs], axis=1)
        dh = None
        for s in range(NS):
            part = lax.dot_general(_mx(full[:, s * ns:(s + 1) * ns]), w_ref[s], NT, preferred_element_type=f32)
            dh = part if dh is None else dh + part
        xv = x_ref[...]
        r = _rms(xv)
        xh = xv * r
        dg_ref[...] += jnp.sum(dh * xh, axis=0, keepdims=True)
        dxh = dh * g_ref[...]
        dx_ref[...] = dxin_ref[...] + r * (dxh - xh * jnp.mean(dxh * xh, axis=-1, keepdims=True))

    dy_specs = [pl.BlockSpec((tm, p.shape[1]), lambda i: (i, 0)) for p in pieces]
    return _hosted_call(
        body, T // tm,
        dy_specs + [pl.BlockSpec((NS, D, ns), lambda i: (0, 0, 0), pipeline_mode=RESIDENT),
                    pl.BlockSpec((tm, D), lambda i: (i, 0)),
                    pl.BlockSpec((1, D), lambda i: (0, 0)),
                    pl.BlockSpec((tm, D), lambda i: (i, 0))],
        [pl.BlockSpec((tm, D), lambda i: (i, 0)), pl.BlockSpec((1, D), lambda i: (0, 0))],
        [SDS((T, D), f32), SDS((1, D), f32)], [], (*pieces, w, x, gain, dxin), name, comm)


def _dycat(dx, w, ys, gain, tm, comm=None):
    T, D = dx.shape

    def body(dx_ref, w_ref, ya, yb, yc, yd, g_ref, da, db, dc, dd, dg_ref):
        @pl.when(pl.program_id(0) == 0)
        def _():
            dg_ref[...] = jnp.zeros_like(dg_ref)
        dyn = lax.dot_general(_mx(dx_ref[...]), w_ref[...], NT, preferred_element_type=f32)
        for m, (r, o) in enumerate(((ya, da), (yb, db), (yc, dc), (yd, dd))):
            cols = slice(m * GW, (m + 1) * GW)
            y = r[...].astype(f32)
            rs = _rms(y)
            yh = y * rs
            d = dyn[:, cols]
            dg_ref[:, cols] += jnp.sum(d * yh, axis=0, keepdims=True)
            dyh = d * g_ref[:, cols]
            o[...] = (rs * (dyh - yh * jnp.mean(dyh * yh, axis=-1, keepdims=True))).astype(bf16)

    yspec = pl.BlockSpec((tm, GW), lambda i: (i, 0))
    return _hosted_call(
        body, T // tm,
        [pl.BlockSpec((tm, D), lambda i: (i, 0)),
         pl.BlockSpec((D, D), lambda i: (0, 0), pipeline_mode=RESIDENT),
         yspec, yspec, yspec, yspec,
         pl.BlockSpec((1, D), lambda i: (0, 0))],
        [yspec, yspec, yspec, yspec, pl.BlockSpec((1, D), lambda i: (0, 0))],
        [SDS((T, GW), bf16)] * 4 + [SDS((1, D), f32)], [], (dx, w, *ys, gain), "dycat", comm)


def _sgu_consts():
    r, c = _iota((GW, GW), 0), _iota((GW, GW), 1)
    seg = (r // HD == c // HD).astype(f32)
    tr, ts = _iota((BLK, BLK), 0), _iota((BLK, BLK), 1)
    causal = ts <= tr
    lane_head = _iota((BLK, GW), 1) // HD
    return seg, causal, lane_head


def _split3_dot(a, ones):
    hi = a.astype(bf16)
    r1 = a - hi.astype(f32)
    mid = r1.astype(bf16)
    lo = (r1 - mid.astype(f32)).astype(bf16)
    dot = functools.partial(jnp.dot, preferred_element_type=f32)
    return dot(hi, ones) + dot(mid, ones) + dot(lo, ones)


def _sgu_chunks(aus, avs, w, bexp, consts):
    seg, causal, lane_head = consts
    segb = seg.astype(bf16)
    nh = GW // HD
    vs = [jax.nn.gelu(av) for av in avs]
    mus = [_split3_dot(v, segb) * (1.0 / HD) for v in vs]
    vcs = [v - mu for v, mu in zip(vs, mus)]
    vars_ = [_split3_dot(vc * vc, segb) * (1.0 / HD) for vc in vcs]
    vns = [_mx(vc * lax.rsqrt(var + EPS)) for vc, var in zip(vcs, vars_)]
    whs = [_mx(jnp.where(causal, w[h], 0.0)) for h in range(nh)]
    mixes = [[jnp.dot(whs[h], vn, preferred_element_type=f32) for h in range(nh)] for vn in vns]
    out = []
    for au, ms in zip(aus, mixes):
        mix = bexp
        for h in range(nh):
            mix = mix + jnp.where(lane_head == h, ms[h], 0.0)
        out.append(jax.nn.gelu(au) * mix)
    return out


def _sgu_group(S):
    nc = S // BLK
    return 4 if nc % 4 == 0 else (2 if nc % 2 == 0 else 1)


def _sgu_fwd(proj, w, bexp, B, S):
    G = _sgu_group(S)

    def body(au_ref, av_ref, w_ref, b_ref, y_ref):
        consts = _sgu_consts()
        wv, bv = w_ref[...], b_ref[...]

        def group(n, c):
            rows = [pl.ds(pl.multiple_of((n * G + j) * BLK, BLK), BLK) for j in range(G)]
            ys = _sgu_chunks([au_ref[r, :].astype(f32) for r in rows], [av_ref[r, :].astype(f32) for r in rows],
                             wv, bv, consts)
            for r, y in zip(rows, ys):
                y_ref[r, :] = y.astype(bf16)
            return c
        lax.fori_loop(0, S // BLK // G, group, 0)

    return pl.pallas_call(
        body, grid=(B,),
        in_specs=[pl.BlockSpec((S, GW), lambda b: (b, 0)),
                  pl.BlockSpec((S, GW), lambda b: (b, 1)),
                  pl.BlockSpec((GW // HD, BLK, BLK), lambda b: (0, 0, 0)),
                  pl.BlockSpec((BLK, GW), lambda b: (0, 0))],
        out_specs=pl.BlockSpec((S, GW), lambda b: (b, 0)),
        out_shape=SDS((B * S, GW), bf16),
        name="sgu_fwd", compiler_params=_cp("parallel"))(proj, proj, w, bexp)


def _sgu_bwd(proj, w, bexp, dy, B, S, comm=None):
    def body(au_ref, av_ref, w_ref, b_ref, dy_ref, dp_ref, dw_ref, db_ref):
        @pl.when(pl.program_id(0) == 0)
        def _():
            dw_ref[...] = jnp.zeros_like(dw_ref)
            db_ref[...] = jnp.zeros_like(db_ref)
        consts = _sgu_consts()
        wv, bv = w_ref[...], b_ref[...]
        fn = lambda aus, avs, ww, bb: _sgu_chunks(aus, avs, ww, bb, consts)
        G = _sgu_group(S)

        def group(n, carry):
            dw_acc, db_acc = carry
            rows = [pl.ds(pl.multiple_of((n * G + j) * BLK, BLK), BLK) for j in range(G)]
            _, vjp = jax.vjp(fn, [au_ref[r, :].astype(f32) for r in rows], [av_ref[r, :].astype(f32) for r in rows], wv, bv)
            daus, davs, dwc, dbc = vjp([dy_ref[r, :].astype(f32) for r in rows])
            for r, dau, dav in zip(rows, daus, davs):
                dp_ref[r, 0:GW] = dau.astype(bf16)
                dp_ref[r, GW:2 * GW] = dav.astype(bf16)
            return dw_acc + dwc, db_acc + dbc
        dw_acc, db_acc = lax.fori_loop(0, S // BLK // G, group, (jnp.zeros(wv.shape, f32), jnp.zeros(bv.shape, f32)))
        dw_ref[...] += dw_acc
        db_ref[...] += jnp.dot(db_acc, consts[0], precision=HIGHEST, preferred_element_type=f32)

    return _hosted_call(
        body, B,
        [pl.BlockSpec((S, GW), lambda b: (b, 0)),
         pl.BlockSpec((S, GW), lambda b: (b, 1)),
         pl.BlockSpec((GW // HD, BLK, BLK), lambda b: (0, 0, 0)),
         pl.BlockSpec((BLK, GW), lambda b: (0, 0)),
         pl.BlockSpec((S, GW), lambda b: (b, 0))],
        [pl.BlockSpec((S, 2 * GW), lambda b: (b, 0)),
         pl.BlockSpec((GW // HD, BLK, BLK), lambda b: (0, 0, 0)),
         pl.BlockSpec((BLK, GW), lambda b: (0, 0))],
        [SDS((B * S, 2 * GW), bf16), SDS((GW // HD, BLK, BLK), f32), SDS((BLK, GW), f32)], [],
        (proj, proj, w, bexp, dy), "sgu_bwd", comm)


def _pool_parts(p):
    n = p.shape[0]
    r = _iota(p.shape, 0)
    lg = _iota(p.shape, 1) // HD

    def sh(v, k):
        return jnp.where(r >= k, pltpu.roll(v, k, 0), 0.0)
    s2 = p + sh(p, 1)
    s4 = s2 + sh(s2, 2)
    s8 = s4 + sh(s4, 4)
    s16 = s8 + sh(s8, 8)
    ws = jnp.where(lg == 0, s2, jnp.where(lg == 1, s4, jnp.where(lg == 2, s8, s16)))
    wlen = jnp.where(lg == 0, 2, jnp.where(lg == 1, 4, jnp.where(lg == 2, 8, 16)))
    cnt = jnp.minimum(r + 1, wlen).astype(f32)
    del n
    return ws / cnt - p, cnt, lg


def _pool_fwd(proj, wbd, scale, B, S):
    def body(p_ref, w_ref, s_ref, y_ref):
        y, _, _ = _pool_parts(p_ref[...].astype(f32))
        y_ref[...] = (jnp.dot(_mx(y), _mx(w_ref[...]), preferred_element_type=f32) * s_ref[...]).astype(bf16)

    return pl.pallas_call(
        body, grid=(B,),
        in_specs=[pl.BlockSpec((S, GW), lambda b: (b, 2)),
                  pl.BlockSpec((GW, GW), lambda b: (0, 0)),
                  pl.BlockSpec((1, GW), lambda b: (0, 0))],
        out_specs=pl.BlockSpec((S, GW), lambda b: (b, 0)),
        out_shape=SDS((B * S, GW), bf16),
        name="pool_fwd", compiler_params=_cp("parallel"))(proj, wbd, scale)


def _pool_bwd(proj, wbd, scale, dy, B, S):
    def body(p_ref, w_ref, s_ref, dy_ref, dp_ref, dw_ref, ds_ref):
        @pl.when(pl.program_id(0) == 0)
        def _():
            dw_ref[...] = jnp.zeros_like(dw_ref)
            ds_ref[...] = jnp.zeros_like(ds_ref)
        y, cnt, lg = _pool_parts(p_ref[...].astype(f32))
        wv = _mx(w_ref[...])
        z = jnp.dot(_mx(y), wv, preferred_element_type=f32)
        dout = dy_ref[...].astype(f32)
        ds_ref[...] += jnp.sum(dout * z, axis=0, keepdims=True)
        dz = _mx(dout * s_ref[...])
        dw_ref[...] += lax.dot_general(_mx(y), dz, TN, preferred_element_type=f32)
        dyv = lax.dot_general(dz, wv, NT, preferred_element_type=f32)
        n = dyv.shape[0]
        r = _iota(dyv.shape, 0)

        def ush(v, k):
            return jnp.where(r < n - k, pltpu.roll(v, n - k, 0), 0.0)
        gq = dyv / cnt
        a2 = gq + ush(gq, 1)
        a4 = a2 + ush(a2, 2)
        a8 = a4 + ush(a4, 4)
        a16 = a8 + ush(a8, 8)
        adj = jnp.where(lg == 0, a2, jnp.where(lg == 1, a4, jnp.where(lg == 2, a8, a16)))
        dp_ref[...] = (adj - dyv).astype(bf16)

    return pl.pallas_call(
        body, grid=(B,),
        in_specs=[pl.BlockSpec((S, GW), lambda b: (b, 2)),
                  pl.BlockSpec((GW, GW), lambda b: (0, 0)),
                  pl.BlockSpec((1, GW), lambda b: (0, 0)),
                  pl.BlockSpec((S, GW), lambda b: (b, 0))],
        out_specs=[pl.BlockSpec((S, GW), lambda b: (b, 0)),
                   pl.BlockSpec((GW, GW), lambda b: (0, 0)),
                   pl.BlockSpec((1, GW), lambda b: (0, 0))],
        out_shape=[SDS((B * S, GW), bf16), SDS((GW, GW), f32), SDS((1, GW), f32)],
        name="pool_bwd", compiler_params=_cp("arbitrary"))(proj, wbd, scale, dy)


def _t5_bucket_table():
    dist = (np.arange(BLK)[:, None] + BLK) - np.arange(2 * BLK)[None, :]
    d = np.clip(dist, 0, BLK - 1)
    max_exact = N_BUCKETS // 2
    df = np.maximum(d, 1).astype(np.float32)
    large = max_exact + (np.log(df / max_exact) / np.float32(np.log(MAX_DISTANCE / max_exact))
                         * (N_BUCKETS - max_exact)).astype(np.int32)
    large = np.minimum(large, N_BUCKETS - 1)
    return np.where(d < max_exact, d, large).astype(np.int32)


def _swa_blocks(qs, kx, vx, sinks, biases, n):
    G = len(qs)
    heads = [(p, g) for p in range(2) for g in range(2)]
    ri, ci = _iota((BLK, BLK), 0), _iota((BLK, BLK), 1)
    qi, ki = _iota((BLK, 2 * BLK), 0), _iota((BLK, 2 * BLK), 1)
    dist = qi + BLK - ki
    band = (dist >= 0) & (dist < BLK)
    masks = [band & ((ki >= BLK) | (n > 0))] + [band] * (G - 1)
    kb, vb = _mx(kx), _mx(vx)
    qsel = [[None] * 4 for _ in range(G)]
    vs = []
    for h, (p, g) in enumerate(heads):
        selq = ((ri - g * HD == ci - p * HD) & (ri >= g * HD) & (ri < (g + 1) * HD)).astype(_MXU)
        selv = ((ci - g * HD == ri - p * HD) & (ci >= g * HD) & (ci < (g + 1) * HD)).astype(_MXU)
        for b in range(G):
            qsel[b][h] = _mx(jnp.dot(_mx(qs[b][p]), selq, preferred_element_type=f32))
        vs.append(_mx(jnp.dot(vb, selv, preferred_element_type=f32)))
    zs = [[lax.dot_general(qsel[b][h], kb[b * BLK:(b + 2) * BLK], NT, preferred_element_type=f32) * (HD ** -0.5)
           for h in range(4)] for b in range(G)]
    prs = [[None] * 4 for _ in range(G)]
    for b in range(G):
        for h in range(4):
            z = jnp.where(masks[b], zs[b][h] + biases[h], -1e30)
            s = jnp.mean(sinks[h], axis=-1, keepdims=True)
            m = jnp.maximum(jnp.max(z, axis=-1, keepdims=True), s)
            e = jnp.exp(z - m)
            prs[b][h] = _mx(e / (jnp.sum(e, axis=-1, keepdims=True) + jnp.exp(s - m)))
    outs = [[jnp.dot(prs[b][h], vs[h][b * BLK:(b + 2) * BLK], preferred_element_type=f32) for h in range(4)]
            for b in range(G)]
    return [[o[0] + o[1], o[2] + o[3]] for o in outs]


def _swa_group(S):
    return 2 if (S // BLK) % 2 == 0 else 1


def _swa_rows(n, G):
    blk = lambda j: pl.ds(pl.multiple_of(j * BLK, BLK), BLK)
    return [blk(jnp.maximum(n - 1, 0))] + [blk(n + b) for b in range(G)]


def _swa_fwd(proj, sinks, bias, B, S, comm=None):
    G = _swa_group(S)

    def body(q_ref, kv_ref, s_ref, b_ref, y_ref):
        def group(i, c):
            n = i * G
            rows = _swa_rows(n, G)
            kx = jnp.concatenate([kv_ref[r, 0:BLK] for r in rows], axis=0).astype(f32)
            vx = jnp.concatenate([kv_ref[r, BLK:2 * BLK] for r in rows], axis=0).astype(f32)
            qs = [[q_ref[r, 0:BLK].astype(f32), q_ref[r, BLK:2 * BLK].astype(f32)] for r in rows[1:]]
            outs = _swa_blocks(qs, kx, vx, [s_ref[h] for h in range(4)], [b_ref[h] for h in range(4)], n)
            for r, (o0, o1) in zip(rows[1:], outs):
                y_ref[r, 0:BLK] = o0.astype(bf16)
                y_ref[r, BLK:2 * BLK] = o1.astype(bf16)
            return c
        lax.fori_loop(0, S // BLK // G, group, 0)

    return _hosted_call(
        body, B,
        [pl.BlockSpec((S, GW), lambda b: (b, 3)),
         pl.BlockSpec((S, GW), lambda b: (b, 4)),
         pl.BlockSpec((4, 1, BLK), lambda b: (0, 0, 0)),
         pl.BlockSpec((4, BLK, 2 * BLK), lambda b: (0, 0, 0))],
        [pl.BlockSpec((S, GW), lambda b: (b, 0))], [SDS((B * S, GW), bf16)], [],
        (proj, proj, sinks, bias), "swa_fwd", comm)


def _swa_bwd(proj, sinks, bias, dy, B, S, comm=None):
    def body(q_ref, kv_ref, s_ref, b_ref, dy_ref, dq_ref, dkv_ref, ds_ref, db_ref, acc_ref):
        @pl.when(pl.program_id(0) == 0)
        def _():
            ds_ref[...] = jnp.zeros_like(ds_ref)
            db_ref[...] = jnp.zeros_like(db_ref)
        acc_ref[...] = jnp.zeros_like(acc_ref)

        G = _swa_group(S)

        def group(i, c):
            n = i * G
            rows = _swa_rows(n, G)
            kx = jnp.concatenate([kv_ref[r, 0:BLK] for r in rows], axis=0).astype(f32)
            vx = jnp.concatenate([kv_ref[r, BLK:2 * BLK] for r in rows], axis=0).astype(f32)
            qs = [[q_ref[r, 0:BLK].astype(f32), q_ref[r, BLK:2 * BLK].astype(f32)] for r in rows[1:]]
            dos = [[dy_ref[r, 0:BLK].astype(f32), dy_ref[r, BLK:2 * BLK].astype(f32)] for r in rows[1:]]
            fn = functools.partial(_swa_blocks, n=n)
            _, vjp = jax.vjp(fn, qs, kx, vx, [s_ref[h] for h in range(4)], [b_ref[h] for h in range(4)])
            dqs, dkx, dvx, dss, dbs = vjp(dos)
            for r, (dq0, dq1) in zip(rows[1:], dqs):
                dq_ref[r, 0:BLK] = dq0.astype(bf16)
                dq_ref[r, BLK:2 * BLK] = dq1.astype(bf16)
            for h in range(4):
                ds_ref[h] += dss[h]
                db_ref[h] += dbs[h]
            for j, r in enumerate(rows):
                acc_ref[r, 0:BLK] += dkx[j * BLK:(j + 1) * BLK]
                acc_ref[r, BLK:2 * BLK] += dvx[j * BLK:(j + 1) * BLK]
            return c
        lax.fori_loop(0, S // BLK // G, group, 0)
        dkv_ref[...] = acc_ref[...].astype(bf16)

    c_args, c_in, c_out, c_shapes, aliases, c_scratch = _host_specs(comm, 5, 4)
    step = lambda v: (lambda: pl.program_id(0) == v)
    return pl.pallas_call(
        _host(body, 5, 4, 1, comm, step(0), step(B - 1)), grid=(B,),
        in_specs=[pl.BlockSpec((S, GW), lambda b: (b, 3)),
                  pl.BlockSpec((S, GW), lambda b: (b, 4)),
                  pl.BlockSpec((4, 1, BLK), lambda b: (0, 0, 0)),
                  pl.BlockSpec((4, BLK, 2 * BLK), lambda b: (0, 0, 0)),
                  pl.BlockSpec((S, GW), lambda b: (b, 0))] + c_in,
        out_specs=[pl.BlockSpec((S, GW), lambda b: (b, 0)),
                   pl.BlockSpec((S, GW), lambda b: (b, 0)),
                   pl.BlockSpec((4, 1, BLK), lambda b: (0, 0, 0)),
                   pl.BlockSpec((4, BLK, 2 * BLK), lambda b: (0, 0, 0))] + c_out,
        out_shape=[SDS((B * S, GW), bf16), SDS((B * S, GW), bf16), SDS((4, 1, BLK), f32),
                   SDS((4, BLK, 2 * BLK), f32)] + c_shapes,
        input_output_aliases=aliases, scratch_shapes=[pltpu.VMEM((S, GW), f32)] + c_scratch,
        name="swa_bwd" if comm is None else "swa_bwd_exchange",
        compiler_params=_cp("arbitrary"))(proj, proj, sinks, bias, dy, *c_args)


def _log1m_parts(z):
    t = jnp.exp(-jnp.abs(z))
    return jnp.minimum(-z, 0.0) - jnp.log(1.0 + t), t


def _log1m(z):
    return _log1m_parts(z)[0]


def _sigmoid_from(z, t):
    return jnp.where(z >= 0.0, 1.0, t) / (1.0 + t)


def _sb_consts(tri):
    r2, c2 = _iota((2 * BLK, 2 * BLK), 0), _iota((2 * BLK, 2 * BLK), 1)
    tri2 = (tri(r2, c2) & (r2 // BLK == c2 // BLK)).astype(bf16)
    ri, ci = _iota((BLK, 2 * BLK), 0), _iota((BLK, 2 * BLK), 1)
    strict2 = (ci % BLK) < ri
    head0 = _iota((BLK, BLK), 1) < HD
    return tri2, strict2, head0


def _sb_stack_kv(k_ref, v_ref, kst_ref, vst_ref, head0, nb):
    def one(kb, c):
        krows = pl.ds(pl.multiple_of(kb * BLK, BLK), BLK)
        for p in range(2):
            for src, dst in ((k_ref, kst_ref), (v_ref, vst_ref)):
                t = src[krows, p * BLK:(p + 1) * BLK]
                dst[p, kb] = _mx(jnp.concatenate([jnp.where(head0, t, 0.0), jnp.where(head0, 0.0, t)], axis=0))
        return c
    lax.fori_loop(0, nb, one, 0)


def _sb_load_kv(kst_ref, vst_ref, kb):
    return [kst_ref[p, kb] for p in range(2)], [vst_ref[p, kb] for p in range(2)]


def _two_halves(a, b):
    return jnp.concatenate([jnp.broadcast_to(a, (BLK, BLK)), jnp.broadcast_to(b, (BLK, BLK))], axis=1)


def _half_sums(t):
    return jnp.sum(t[:, :BLK], axis=-1, keepdims=True), jnp.sum(t[:, BLK:], axis=-1, keepdims=True)


def _sb_fwd(proj, B, S, comm=None):
    def body(q_ref, k_ref, v_ref, y_ref, lt_ref, kst_ref, vst_ref):
        ci = _iota((BLK, BLK), 1)
        above2, strict2, head0 = _sb_consts(lambda r, c: r > c)
        _sb_stack_kv(k_ref, v_ref, kst_ref, vst_ref, head0, S // BLK)

        def step(qs, kbs, diags, carry):
            U = range(len(kbs))
            ok = [None if diags[u] else kbs[u] >= 0 for u in U]
            kv = [_sb_load_kv(kst_ref, vst_ref, jnp.maximum(kb, 0)) for kb in kbs]
            zs = [[lax.dot_general(qs[p], kks[p], NT, preferred_element_type=f32) for p in range(2)] for kks, _ in kv]
            Ls = [[jnp.where(strict2, _log1m(z), 0.0) if diags[u] else _log1m(z) for z in zs[u]] for u in U]
            tails = [[_split_dot(L, above2) for L in Lu] for Lu in Ls]
            carry = list(carry)
            for u in U:
                for p in range(2):
                    R0, R1, acc = carry[3 * p:3 * p + 3]
                    w = jnp.exp(zs[u][p] + Ls[u][p] + tails[u][p] + _two_halves(R0, R1))
                    s0, s1 = _half_sums(Ls[u][p])
                    if diags[u]:
                        w = jnp.where(strict2, w, 0.0)
                    else:
                        w, s0, s1 = (jnp.where(ok[u], t, 0.0) for t in (w, s0, s1))
                    acc = acc + jnp.dot(_mx(w), kv[u][1][p], preferred_element_type=f32)
                    carry[3 * p:3 * p + 3] = [R0 + s0, R1 + s1, acc]
            return tuple(carry)

        def qblock(n, c):
            qrows = pl.ds(pl.multiple_of(n * BLK, BLK), BLK)
            qs = [_mx(q_ref[qrows, p * BLK:(p + 1) * BLK] * (HD ** -0.5)) for p in range(2)]
            z1, z2 = jnp.zeros((BLK, 1), f32), jnp.zeros((BLK, BLK), f32)
            near = [n - 1 - u for u in range(SB_UNROLL)]
            carry = step(qs, [n] + near, [True] + [False] * SB_UNROLL, (z1, z1, z2, z1, z1, z2))
            far = jnp.maximum(n - SB_UNROLL, 0)
            trips = (far + SB_UNROLL - 1) // SB_UNROLL

            def live(st):
                worst = jnp.maximum(jnp.maximum(st[1], st[2]), jnp.maximum(st[4], st[5]))
                return (st[0] < trips) & (jnp.max(worst) > SB_CUT)

            def trip(st):
                i = st[0]
                kbs = [far - 1 - SB_UNROLL * i - u for u in range(SB_UNROLL)]
                return (i + 1,) + step(qs, kbs, [False] * SB_UNROLL, st[1:])
            done, *res = lax.while_loop(live, trip, (jnp.int32(0),) + carry)
            lt = jnp.where(ci == SB_HEADS, done.astype(f32), 0.0)
            for p in range(2):
                y_ref[qrows, p * BLK:(p + 1) * BLK] = res[3 * p + 2].astype(bf16)
                lt = lt + jnp.where(ci == 2 * p, res[3 * p], 0.0) + jnp.where(ci == 2 * p + 1, res[3 * p + 1], 0.0)
            lt_ref[qrows, :] = lt
            return c
        lax.fori_loop(0, S // BLK, qblock, 0)

    spec = lambda j: pl.BlockSpec((S, GW), lambda b: (b, j))
    c_args, c_in, c_out, c_shapes, aliases, c_scratch = _host_specs(comm, 3, 2)
    step = lambda v: (lambda: pl.program_id(0) == v)
    stacked = pltpu.VMEM((2, S // BLK, 2 * BLK, BLK), _MXU)
    return pl.pallas_call(
        _host(body, 3, 2, 2, comm, step(0), step(B - 1)), grid=(B,),
        in_specs=[spec(5), spec(6), spec(7)] + c_in,
        out_specs=[pl.BlockSpec((S, GW), lambda b: (b, 0)), pl.BlockSpec((S, BLK), lambda b: (b, 0))] + c_out,
        out_shape=[SDS((B * S, GW), bf16), SDS((B * S, BLK), f32)] + c_shapes,
        input_output_aliases=aliases, scratch_shapes=[stacked, stacked] + c_scratch,
        name="sb_fwd" if comm is None else "sb_fwd_gather",
        compiler_params=_cp("arbitrary"))(proj, proj, proj, *c_args)


def _sb_bwd(proj, ltot, dy, B, S, comm=None):
    def body(q_ref, k_ref, v_ref, lt_ref, dy_ref, dq_ref, dk_ref, dv_ref, dka_ref, dva_ref, kst_ref, vst_ref):
        ci = _iota((BLK, BLK), 1)
        upto2, strict2, head0 = _sb_consts(lambda r, c: r <= c)
        below2, _, _ = _sb_consts(lambda r, c: r < c)
        dka_ref[...] = jnp.zeros_like(dka_ref)
        dva_ref[...] = jnp.zeros_like(dva_ref)
        _sb_stack_kv(k_ref, v_ref, kst_ref, vst_ref, head0, S // BLK)

        def step(qs, dos, lts, kbs, ok, diags, top, carry):
            U = range(len(kbs))
            kbs = [jnp.clip(kb, 0, top) for kb in kbs]
            kv = [_sb_load_kv(kst_ref, vst_ref, kb) for kb in kbs]
            zs = [[lax.dot_general(qs[p], kv[u][0][p], NT, preferred_element_type=f32) for p in range(2)] for u in U]
            dws = [[lax.dot_general(dos[p], kv[u][1][p], NT, preferred_element_type=f32) for p in range(2)] for u in U]
            parts = [[_log1m_parts(z) for z in zu] for zu in zs]
            Ls = [[jnp.where(strict2, lt[0], 0.0) if diags[u] else lt[0] for lt in parts[u]] for u in U]
            pins = [[_split_dot(L, upto2) for L in Lu] for Lu in Ls]
            carry = list(carry)
            ws, das = [], []
            for u in U:
                wu, dau = [], []
                for p in range(2):
                    PL0, PL1 = carry[5 * p], carry[5 * p + 1]
                    tail = _two_halves(lts[2 * p] - PL0, lts[2 * p + 1] - PL1) - pins[u][p]
                    w = jnp.exp(zs[u][p] + Ls[u][p] + tail)
                    l0, l1 = _half_sums(Ls[u][p])
                    if diags[u]:
                        w = jnp.where(strict2, w, 0.0)
                    else:
                        w, l0, l1 = (jnp.where(ok[u], t, 0.0) for t in (w, l0, l1))
                    carry[5 * p], carry[5 * p + 1] = PL0 + l0, PL1 + l1
                    wu.append(w)
                    dau.append(w * dws[u][p])
                ws.append(wu)
                das.append(dau)
            pexs = [[_split_dot(da, below2) for da in dau] for dau in das]
            dzs = []
            for u in U:
                dzu = []
                for p in range(2):
                    dL = _two_halves(carry[5 * p + 2], carry[5 * p + 3]) + pexs[u][p]
                    sg = _sigmoid_from(zs[u][p], parts[u][p][1])
                    dz = das[u][p] * (1.0 - sg) - dL * sg
                    dz = jnp.where(strict2 if diags[u] else ok[u], dz, 0.0)
                    a0, a1 = _half_sums(das[u][p])
                    carry[5 * p + 2], carry[5 * p + 3] = carry[5 * p + 2] + a0, carry[5 * p + 3] + a1
                    dzu.append(_mx(dz))
                dzs.append(dzu)
            dqs = [[jnp.dot(dzs[u][p], kv[u][0][p], preferred_element_type=f32) for p in range(2)] for u in U]
            dks = [[lax.dot_general(dzs[u][p], qs[p], TN, preferred_element_type=f32) for p in range(2)] for u in U]
            dvs = [[lax.dot_general(_mx(ws[u][p]), dos[p], TN, preferred_element_type=f32) for p in range(2)] for u in U]
            for u in U:
                krows = pl.ds(pl.multiple_of(kbs[u] * BLK, BLK), BLK)
                for p in range(2):
                    lanes = slice(p * BLK, (p + 1) * BLK)
                    dka_ref[krows, lanes] += jnp.where(head0, dks[u][p][:BLK], dks[u][p][BLK:])
                    dva_ref[krows, lanes] += jnp.where(head0, dvs[u][p][:BLK], dvs[u][p][BLK:])
                    carry[5 * p + 4] = carry[5 * p + 4] + dqs[u][p]
            return tuple(carry)

        def qblock(n, c):
            qrows = pl.ds(pl.multiple_of(n * BLK, BLK), BLK)
            ltb = lt_ref[qrows, :]
            lts = [jnp.sum(jnp.where(ci == h, ltb, 0.0), axis=-1, keepdims=True) for h in range(4)]
            qs = [_mx(q_ref[qrows, p * BLK:(p + 1) * BLK] * (HD ** -0.5)) for p in range(2)]
            dos = [_mx(dy_ref[qrows, p * BLK:(p + 1) * BLK]) for p in range(2)]
            z1, z2 = jnp.zeros((BLK, 1), f32), jnp.zeros((BLK, BLK), f32)
            done = jnp.max(jnp.where(ci == SB_HEADS, ltb, 0.0)).astype(jnp.int32)
            far = jnp.maximum(n - SB_UNROLL, 0)
            first = jnp.maximum(far - SB_UNROLL * done, 0)

            def trip(i, cr):
                kbs = [first + SB_UNROLL * i + u for u in range(SB_UNROLL)]
                return step(qs, dos, lts, kbs, [kb < far for kb in kbs], [False] * SB_UNROLL, n, cr)
            carry = lax.fori_loop(0, (far - first + SB_UNROLL - 1) // SB_UNROLL, trip, (z1, z1, z1, z1, z2) * 2)
            near = [n - SB_UNROLL + u for u in range(SB_UNROLL)]
            res = step(qs, dos, lts, near + [n], [kb >= 0 for kb in near] + [None], [False] * SB_UNROLL + [True], n, carry)
            for p in range(2):
                dq_ref[qrows, p * BLK:(p + 1) * BLK] = (res[5 * p + 4] * (HD ** -0.5)).astype(bf16)
            return c
        lax.fori_loop(0, S // BLK, qblock, 0)
        dk_ref[...] = dka_ref[...].astype(bf16)
        dv_ref[...] = dva_ref[...].astype(bf16)

    spec = lambda j: pl.BlockSpec((S, GW), lambda b: (b, j))
    o = pl.BlockSpec((S, GW), lambda b: (b, 0))
    c_args, c_in, c_out, c_shapes, aliases, c_scratch = _host_specs(comm, 5, 3)
    step = lambda v: (lambda: pl.program_id(0) == v)
    stacked = pltpu.VMEM((2, S // BLK, 2 * BLK, BLK), _MXU)
    return pl.pallas_call(
        _host(body, 5, 3, 4, comm, step(0), step(B - 1)), grid=(B,),
        in_specs=[spec(5), spec(6), spec(7), pl.BlockSpec((S, BLK), lambda b: (b, 0)), o] + c_in,
        out_specs=[o, o, o] + c_out,
        out_shape=[SDS((B * S, GW), bf16)] * 3 + c_shapes,
        input_output_aliases=aliases,
        scratch_shapes=[pltpu.VMEM((S, GW), f32), pltpu.VMEM((S, GW), f32), stacked, stacked] + c_scratch,
        name="sb_bwd" if comm is None else "sb_bwd_exchange",
        compiler_params=_cp("arbitrary"))(proj, proj, proj, ltot, dy, *c_args)


def _bias_expand(rel_bias_t, bucket):
    n = bucket.shape[1]

    def body(r_ref, b_ref, o_ref):
        onehot = (_iota((N_BUCKETS, n), 0) == b_ref[...]).astype(f32)
        o_ref[...] = jnp.dot(r_ref[...], onehot, precision=HIGHEST, preferred_element_type=f32)
    return pl.pallas_call(body, out_shape=SDS((rel_bias_t.shape[0], n), f32), name="bias_expand",
                          compiler_params=_cp())(rel_bias_t, bucket)


def _bias_reduce(dbias, bucket):
    n = bucket.shape[1]

    def body(*refs):
        b_ref, g_ref = refs[-2], refs[-1]
        d = refs[0][...]
        for r in refs[1:-2]:
            d = d + r[...]
        onehot = (_iota((N_BUCKETS, n), 0) == b_ref[...]).astype(f32)
        g_ref[...] = lax.dot_general(d, onehot, NT, precision=HIGHEST, preferred_element_type=f32)
    return pl.pallas_call(body, out_shape=SDS((dbias[0].shape[0], N_BUCKETS), f32), name="bias_reduce",
                          compiler_params=_cp())(*dbias, bucket)


def _adamw(w, g, m, v, tr, name, emit_g=False):
    R, C = w.shape

    def body(w_ref, g_ref, m_ref, v_ref, d_ref, m2_ref, v2_ref, *g_out):
        gv = g_ref[...]
        if emit_g:
            g_out[0][...] = gv
        m2 = ADAM_B1 * m_ref[...] + (1.0 - ADAM_B1) * gv
        v2 = ADAM_B2 * v_ref[...] + (1.0 - ADAM_B2) * (gv * gv)
        m_hat = m2 / (1.0 - ADAM_B1 ** ADAM_STEP)
        v_hat = v2 / (1.0 - ADAM_B2 ** ADAM_STEP)
        d_ref[...] = -ADAM_LR * (m_hat / (jnp.sqrt(v_hat) + ADAM_EPS) + ADAM_WD * w_ref[...])
        m2_ref[...] = m2
        v2_ref[...] = v2

    spec = pl.BlockSpec((tr, C), lambda i: (i, 0))
    n_out = 4 if emit_g else 3
    return pl.pallas_call(
        body, grid=(R // tr,), in_specs=[spec] * 4, out_specs=[spec] * n_out,
        out_shape=[SDS((R, C), f32)] * n_out, name=name, compiler_params=_cp("parallel"))(w, g, m, v)


ANY = pl.BlockSpec(memory_space=pl.ANY)


def _place():
    x, y, c = lax.axis_index("x"), lax.axis_index("y"), lax.axis_index("c")
    chips = [(1 - x, y), (x, 1 - y), (1 - x, 1 - y)]
    return x, y, c, chips


def _cast_slots(w, kidx):
    L, a, b = w.shape
    ta = a // 2

    def body(k_ref, *refs):
        for l in range(L):
            refs[L + l][0] = refs[l][0].astype(bf16)

    return pl.pallas_call(
        body,
        grid_spec=pltpu.PrefetchScalarGridSpec(
            num_scalar_prefetch=1, grid=(a // ta,),
            in_specs=[pl.BlockSpec((1, ta, b), functools.partial(lambda i, k_ref, l: (l, i, 0), l=l)) for l in range(L)],
            out_specs=[pl.BlockSpec((1, ta, b), lambda i, k_ref: (k_ref[0], i, 0)) for _ in range(L)]),
        out_shape=[SDS((N_CHIPS, a, b), bf16)] * L,
        name="cast_slots", compiler_params=_cp("parallel"))(kidx, *([w] * L))


class _GatherComm:
    def __init__(self, bufs, part=0, parts=1):
        self.inputs = list(bufs)
        self.out_shape = [SDS(b.shape, b.dtype) for b in bufs]
        self.aliased = True
        self.scratch = [pltpu.SemaphoreType.DMA((3 * len(bufs),))] * 4
        self.part, self.parts = part, parts

    def _copies(self, i_refs, o_refs, sems):
        send1, recv1, send2, recv2 = sems
        x, y, c, chips = _place()
        k = 2 * x + y
        first, got1, second, got2 = [], [], [], []
        for i, buf in enumerate(self.inputs):
            h = buf.shape[1] // 2
            n = h // self.parts
            mine, theirs = pl.ds(c * h + self.part * n, n), pl.ds((1 - c) * h + self.part * n, n)
            for j, (cx, cy) in enumerate(chips):
                s = 3 * i + j
                first.append(pltpu.make_async_remote_copy(
                    src_ref=i_refs[i].at[k, mine], dst_ref=o_refs[i].at[k, mine], send_sem=send1.at[s],
                    recv_sem=recv1.at[s], device_id=(cx, cy, c), device_id_type=MESH))
                a = o_refs[i].at[2 * cx + cy, mine]
                got1.append(pltpu.make_async_remote_copy(
                    src_ref=a, dst_ref=a, send_sem=send1.at[s], recv_sem=recv1.at[s],
                    device_id=(cx, cy, c), device_id_type=MESH))
                second.append(pltpu.make_async_remote_copy(
                    src_ref=a, dst_ref=a, send_sem=send2.at[s], recv_sem=recv2.at[s],
                    device_id=(x, y, 1 - c), device_id_type=MESH))
                b = o_refs[i].at[2 * cx + cy, theirs]
                got2.append(pltpu.make_async_remote_copy(
                    src_ref=b, dst_ref=b, send_sem=send2.at[s], recv_sem=recv2.at[s],
                    device_id=(x, y, 1 - c), device_id_type=MESH))
        return first, got1, second, got2

    def start(self, i_refs, o_refs, sems):
        for cp in self._copies(i_refs, o_refs, sems)[0]:
            cp.start()

    def finish(self, i_refs, o_refs, sems):
        first, got1, second, got2 = self._copies(i_refs, o_refs, sems)
        for g, cp in zip(got1, second):
            g.wait_recv()
            cp.start()
        for g in got2:
            g.wait_recv()
        for cp in first + second:
            cp.wait_send()


class _MultiComm:
    def __init__(self, comms):
        self.comms = comms
        self.inputs = [a for c in comms for a in c.inputs]
        self.out_shape = [s for c in comms for s in c.out_shape]
        self.aliased = comms[0].aliased
        assert all(c.aliased == self.aliased for c in comms)
        self.scratch = [s for c in comms for s in c.scratch]

    def _split(self, i_refs, o_refs, sems):
        i = o = s = 0
        for c in self.comms:
            ni, no, ns = len(c.inputs), len(c.out_shape), len(c.scratch)
            yield c, i_refs[i:i + ni], o_refs[o:o + no], sems[s:s + ns]
            i, o, s = i + ni, o + no, s + ns

    def start(self, i_refs, o_refs, sems):
        for c, i, o, s in self._split(i_refs, o_refs, sems):
            c.start(i, o, s)

    def finish(self, i_refs, o_refs, sems):
        for c, i, o, s in self._split(i_refs, o_refs, sems):
            c.finish(i, o, s)


class _PairExchangeComm:
    def __init__(self, gs):
        self.inputs = list(gs)
        self.out_shape = [SDS((g.shape[0], g.shape[1] // 2, g.shape[2]), g.dtype) for g in gs]
        self.aliased = False
        self.scratch = [pltpu.SemaphoreType.DMA((len(gs),))] * 2

    def _copies(self, i_refs, o_refs, sems):
        send, recv = sems
        x, y, c, _ = _place()
        cps = []
        for i, g in enumerate(self.inputs):
            h = g.shape[1] // 2
            cps.append(pltpu.make_async_remote_copy(
                src_ref=i_refs[i].at[:, pl.ds((1 - c) * h, h)], dst_ref=o_refs[i], send_sem=send.at[i], recv_sem=recv.at[i],
                device_id=(x, y, 1 - c), device_id_type=MESH))
        return cps

    def start(self, i_refs, o_refs, sems):
        for cp in self._copies(i_refs, o_refs, sems):
            cp.start()

    def finish(self, i_refs, o_refs, sems):
        for cp in self._copies(i_refs, o_refs, sems):
            cp.wait()


class _ChipExchangeComm:
    def __init__(self, qs):
        self.inputs = list(qs)
        self.out_shape = [SDS(q.shape, q.dtype) for q in qs]
        self.aliased = False
        self.scratch = [pltpu.SemaphoreType.DMA((3 * len(qs),))] * 2

    def _copies(self, i_refs, o_refs, sems):
        send, recv = sems
        x, y, c, chips = _place()
        k = 2 * x + y
        cps, got = [], []
        for i in range(len(self.inputs)):
            for j, (cx, cy) in enumerate(chips):
                s = 3 * i + j
                cps.append(pltpu.make_async_remote_copy(
                    src_ref=i_refs[i].at[2 * cx + cy], dst_ref=o_refs[i].at[k], send_sem=send.at[s],
                    recv_sem=recv.at[s], device_id=(cx, cy, c), device_id_type=MESH))
                a = o_refs[i].at[2 * cx + cy]
                got.append(pltpu.make_async_remote_copy(
                    src_ref=a, dst_ref=a, send_sem=send.at[s], recv_sem=recv.at[s],
                    device_id=(cx, cy, c), device_id_type=MESH))
        return cps, got

    def start(self, i_refs, o_refs, sems):
        for cp in self._copies(i_refs, o_refs, sems)[0]:
            cp.start()

    def finish(self, i_refs, o_refs, sems):
        cps, got = self._copies(i_refs, o_refs, sems)
        for g in got:
            g.wait_recv()
        for cp in cps:
            cp.wait_send()


def _comm_only(comm, name):
    n = len(comm.inputs)

    def body(*refs):
        i_refs, o_refs, sems = refs[:n], refs[n:n + len(comm.out_shape)], refs[n + len(comm.out_shape):]
        comm.start(i_refs, o_refs, sems)
        comm.finish(i_refs, o_refs, sems)

    return pl.pallas_call(
        body, out_shape=comm.out_shape, in_specs=[ANY] * n, out_specs=[ANY] * len(comm.out_shape),
        input_output_aliases={i: i for i in range(n)} if comm.aliased else {},
        scratch_shapes=comm.scratch, name=name,
        compiler_params=pltpu.CompilerParams(has_side_effects=True))(*comm.inputs)


def _host(body, n_in, n_out, n_scratch, comm, first, last):
    if comm is None:
        return body
    ci, co = len(comm.inputs), len(comm.out_shape)

    def wrapped(*refs):
        o = 0
        parts = []
        for n in (n_in, ci, n_out, co, n_scratch):
            parts.append(refs[o:o + n])
            o += n
        hin, cin, hout, cout, hs = parts
        sems = refs[o:]

        @pl.when(first())
        def _():
            comm.start(cin, cout, sems)
        body(*hin, *hout, *hs)

        @pl.when(last())
        def _():
            comm.finish(cin, cout, sems)
    return wrapped


def _host_specs(comm, n_in, n_out):
    if comm is None:
        return [], [], [], [], {}, []
    ci, co = len(comm.inputs), len(comm.out_shape)
    aliases = {n_in + i: n_out + i for i in range(ci)} if comm.aliased else {}
    return comm.inputs, [ANY] * ci, [ANY] * co, comm.out_shape, aliases, comm.scratch


def _pair_add(g, r, cidx, name):
    ns, a, b = g.shape
    h = a // 2
    th = h if h * b * 4 <= 4 * 1024 * 1024 else h // 2

    def body(c_ref, g_ref, r_ref, qb_ref):
        qb_ref[...] = (g_ref[...] + r_ref[...]).astype(bf16)

    nb = h // th
    spec = pl.BlockSpec((1, th, b), lambda s, i, c_ref: (s, i, 0))
    return pl.pallas_call(
        body,
        grid_spec=pltpu.PrefetchScalarGridSpec(
            num_scalar_prefetch=1, grid=(ns, nb),
            in_specs=[pl.BlockSpec((1, th, b), lambda s, i, c_ref: (s, c_ref[0] * nb + i, 0)), spec],
            out_specs=spec),
        out_shape=SDS((ns, h, b), bf16),
        name=name, compiler_params=_cp("parallel", "parallel"))(cidx, g, r)


def _chip_add(g, r1, r2, idx, prev, L, name):
    ns, h, b = r2.shape
    th = h if h * b * 4 <= 4 * 1024 * 1024 else h // 2
    nb = h // th

    def body(s_ref, g_ref, r1_ref, a_ref, b_ref, c_ref, *rest):
        o_ref = rest[-1]
        o_ref[0] = (g_ref[0] + r1_ref[0]) + a_ref[0].astype(f32) + b_ref[0].astype(f32) + c_ref[0].astype(f32)

    other = lambda d: pl.BlockSpec((1, th, b), lambda i, s_ref: ((s_ref[0] + d) % ns, i, 0))
    in_specs = [pl.BlockSpec((1, th, b), lambda i, s_ref: (s_ref[0], s_ref[1] * nb + i, 0)),
                pl.BlockSpec((1, th, b), lambda i, s_ref: (s_ref[0], i, 0)), other(1), other(2), other(3)]
    args = [idx, g, r1, r2, r2, r2]
    aliases = {}
    if prev is not None:
        in_specs.append(ANY)
        args.append(prev)
        aliases = {6: 0}
    return pl.pallas_call(
        body,
        grid_spec=pltpu.PrefetchScalarGridSpec(
            num_scalar_prefetch=1, grid=(nb,), in_specs=in_specs,
            out_specs=pl.BlockSpec((1, th, b), lambda i, s_ref: (s_ref[2], s_ref[1] * nb + i, 0))),
        out_shape=SDS((L, 2 * h, b), f32), input_output_aliases=aliases,
        name=name, compiler_params=_cp("arbitrary"))(*args)


def _pair_share(gs, hs):
    n = len(gs)
    L = gs[0].shape[0]

    def body(*refs):
        i_refs, o_refs = refs[:n], refs[n:2 * n]
        send, recv = refs[2 * n:]
        x, y, c, _ = _place()
        cps = []
        for i in range(n):
            for l in range(L):
                mine = pl.ds(c * hs[i], hs[i])
                cp = pltpu.make_async_remote_copy(
                    src_ref=i_refs[i].at[l, mine], dst_ref=o_refs[i].at[l, mine], send_sem=send.at[i * L + l],
                    recv_sem=recv.at[i * L + l], device_id=(x, y, 1 - c), device_id_type=MESH)
                cp.start()
                cps.append(cp)
        for i in range(n):
            for l in range(L):
                got = o_refs[i].at[l, pl.ds((1 - c) * hs[i], hs[i])]
                pltpu.make_async_remote_copy(
                    src_ref=got, dst_ref=got, send_sem=send.at[i * L + l], recv_sem=recv.at[i * L + l],
                    device_id=(x, y, 1 - c), device_id_type=MESH).wait_recv()
        for cp in cps:
            cp.wait_send()

    return pl.pallas_call(
        body, out_shape=[SDS(g.shape, g.dtype) for g in gs], in_specs=[ANY] * n, out_specs=[ANY] * n,
        input_output_aliases={i: i for i in range(n)},
        scratch_shapes=[pltpu.SemaphoreType.DMA((n * L,))] * 2,
        name="grad_pair_share", compiler_params=pltpu.CompilerParams(has_side_effects=True))(*gs)


class _SwapComm:
    def __init__(self, arrays):
        self.inputs = list(arrays)
        self.out_shape = [SDS(a.shape, a.dtype) for a in arrays]
        self.aliased = False
        self.scratch = [pltpu.SemaphoreType.DMA((len(arrays),))] * 2

    def _copies(self, i_refs, o_refs, sems):
        send, recv = sems
        x, y, c, _ = _place()
        return [pltpu.make_async_remote_copy(
            src_ref=i_refs[i], dst_ref=o_refs[i], send_sem=send.at[i], recv_sem=recv.at[i],
            device_id=(x, y, 1 - c), device_id_type=MESH) for i in range(len(self.inputs))]

    def start(self, i_refs, o_refs, sems):
        for cp in self._copies(i_refs, o_refs, sems):
            cp.start()

    def finish(self, i_refs, o_refs, sems):
        for cp in self._copies(i_refs, o_refs, sems):
            cp.wait()


class _SlotShareComm:
    def __init__(self, bufs):
        self.inputs = list(bufs)
        self.out_shape = [SDS(b.shape, b.dtype) for b in bufs]
        self.aliased = True
        self.scratch = [pltpu.SemaphoreType.DMA((3 * len(bufs),))] * 2

    def _copies(self, i_refs, o_refs, sems):
        send, recv = sems
        x, y, c, chips = _place()
        k = 2 * x + y
        cps, got = [], []
        for i in range(len(self.inputs)):
            for j, (cx, cy) in enumerate(chips):
                s = 3 * i + j
                cps.append(pltpu.make_async_remote_copy(
                    src_ref=i_refs[i].at[k], dst_ref=o_refs[i].at[k], send_sem=send.at[s], recv_sem=recv.at[s],
                    device_id=(cx, cy, c), device_id_type=MESH))
                a = o_refs[i].at[2 * cx + cy]
                got.append(pltpu.make_async_remote_copy(
                    src_ref=a, dst_ref=a, send_sem=send.at[s], recv_sem=recv.at[s],
                    device_id=(cx, cy, c), device_id_type=MESH))
        return cps, got

    def start(self, i_refs, o_refs, sems):
        for cp in self._copies(i_refs, o_refs, sems)[0]:
            cp.start()

    def finish(self, i_refs, o_refs, sems):
        cps, got = self._copies(i_refs, o_refs, sems)
        for g in got:
            g.wait_recv()
        for cp in cps:
            cp.wait_send()


def _pair_sum_slot(mine, theirs, kidx, dtype):
    R, C = mine.shape

    def body(k_ref, a_ref, b_ref, o_ref):
        o_ref[0] = (a_ref[...] + b_ref[...]).astype(dtype)

    spec = pl.BlockSpec((R, C), lambda i, k_ref: (0, 0))
    return pl.pallas_call(
        body,
        grid_spec=pltpu.PrefetchScalarGridSpec(
            num_scalar_prefetch=1, grid=(1,), in_specs=[spec, spec],
            out_specs=pl.BlockSpec((1, R, C), lambda i, k_ref: (k_ref[0], 0, 0))),
        out_shape=SDS((N_CHIPS, R, C), dtype), name="small_pair_sum", compiler_params=_cp("arbitrary"))(kidx, mine, theirs)


def _small_sum(g):
    n, R, C = g.shape

    def body(g_ref, o_ref):
        acc = g_ref[0].astype(f32)
        for j in range(1, n):
            acc = acc + g_ref[j].astype(f32)
        o_ref[...] = acc
    return pl.pallas_call(body, out_shape=SDS((R, C), f32), name="small_sum", compiler_params=_cp())(g)


PACK_COLS = 1024


def _rows_of(shape):
    n = int(np.prod(shape)) if len(shape) else 1
    return -(-n // (8 * PACK_COLS)) * 8


def _pack(parts):
    blocks = []
    for p in parts:
        flat = p.reshape(-1)
        r = _rows_of(p.shape)
        blocks.append(jnp.pad(flat, (0, r * PACK_COLS - flat.shape[0])).reshape(r, PACK_COLS))
    return jnp.concatenate(blocks, axis=0)


def _unpack(buf, shapes):
    out, off = [], 0
    for s in shapes:
        n = int(np.prod(s)) if len(s) else 1
        r = _rows_of(s)
        out.append(buf[off:off + r].reshape(-1)[:n].reshape(s))
        off += r
    return out


def _block_diag(w):
    g, a, _ = w.shape
    out = jnp.zeros((g * a, g * a), w.dtype)
    for i in range(g):
        out = lax.dynamic_update_slice(out, w[i], (i * a, i * a))
    return out


def kernel(x, w_in, w_out, sgu_w, sgu_b, pool_w, pool_scale, swa_sinks, rel_bias, mix_out_gain, norm_mix, norm_ffn, w_gate_up, w_down, norm_final, loss_target, m_w_in, m_w_out, m_sgu_w, m_sgu_b, m_pool_w, m_pool_scale, m_swa_sinks, m_rel_bias, m_mix_out_gain, m_norm_mix, m_norm_ffn, m_w_gate_up, m_w_down, m_norm_final, v_w_in, v_w_out, v_sgu_w, v_sgu_b, v_pool_w, v_pool_scale, v_swa_sinks, v_rel_bias, v_mix_out_gain, v_norm_mix, v_norm_ffn, v_w_gate_up, v_w_down, v_norm_final):
    B, S, D = x.shape
    T = B * S
    L = w_in.shape[0]
    tm = min(512, T)
    F = w_down.shape[1] * N_CHIPS
    xi, yi, ci = lax.axis_index("x"), lax.axis_index("y"), lax.axis_index("c")
    cidx = jnp.reshape(ci, (1,)).astype(jnp.int32)
    kidx = jnp.reshape(2 * xi + yi, (1,)).astype(jnp.int32)

    big = [w_in, w_out, w_gate_up, w_down]
    slots = [_cast_slots(w, kidx) for w in big]
    gather = lambda pi, l, part=0, parts=1: _GatherComm([slots[pi][l]], part, parts)
    Win, Wo, Wgu, Wd = ([None] * L for _ in range(4))
    Win[0], = _comm_only(gather(0, 0), "gather_weights")

    bucket = jnp.asarray(_t5_bucket_table().reshape(1, -1))
    bias_tab = _bias_expand(rel_bias.T, bucket).reshape(4, BLK, 2 * BLK)

    row = lambda v: v.reshape(1, -1)
    xc = x.reshape(T, D)
    tgt = loss_target.reshape(T, D)
    saved = []
    for l in range(L):
        bexp = jnp.repeat(sgu_b[l].T, HD, axis=1)
        wbd = _block_diag(pool_w[l])
        sk = jnp.broadcast_to(swa_sinks[l][:, None, None], (4, 1, BLK))
        h1, proj, wo = _norm_mm(xc, row(norm_mix[l]), Win[l], tm, gather(1, l))
        ya = _sgu_fwd(proj, sgu_w[l], bexp, B, S)
        yb = _pool_fwd(proj, wbd, row(pool_scale[l]), B, S)
        yc, wd = _swa_fwd(proj, sk, bias_tab, B, S, gather(3, l, 0, 2))
        if l == 0:
            yd, lt, wd, Wgu[0] = _sb_fwd(proj, B, S, _MultiComm([_GatherComm([wd], 1, 2), gather(2, 0)]))
        else:
            yd, lt, wd = _sb_fwd(proj, B, S, _GatherComm([wd], 1, 2))
        Wo[l], Wd[l] = wo.reshape(D, D), wd.reshape(F, D)
        ys = (ya, yb, yc, yd)
        nxt = l + 1 < L
        ycn, x1, h2, gu, act, *got = _mix_out_swiglu(ys, row(mix_out_gain[l]), Wo[l], xc, row(norm_ffn[l]), Wgu[l], tm,
                                                     gather(2, l + 1) if nxt else None)
        x2, *got2 = _mm_res(act, Wd[l], x1, tm, gather(0, l + 1) if nxt else None)
        if nxt:
            Wgu[l + 1], Win[l + 1] = got[0], got2[0]
        saved.append((xc, h1, proj, bexp, wbd, sk, ys, lt, ycn, x1, h2, gu, act))
        xc = x2

    dx, g_final, loss_v = _final_loss(xc, row(norm_final), tgt, tm)

    tk = min(T, 2048)
    gW = [[None] * L for _ in range(4)]
    g_sgu_w, g_sgu_b, g_pool_w, g_pool_scale, g_sinks, g_bias = ([None] * L for _ in range(6))
    g_out_gain, g_mix, g_ffn = ([None] * L for _ in range(3))
    reduced = [None] * 4
    sums = {}

    def pair_comm(keys):
        return _PairExchangeComm([gW[pi][l] for pi, l in keys])

    def after_pair(keys, r1):
        for (pi, l), r in zip(keys, r1):
            sums[pi, l] = (r, _pair_add(gW[pi][l], r, cidx, "grad_pair_add"))

    def chip_comm(keys):
        return _ChipExchangeComm([sums[k][1] for k in keys])

    def after_chip(keys, r2):
        for (pi, l), r in zip(keys, r2):
            idx = jnp.stack([2 * xi + yi, ci, jnp.int32(l)]).astype(jnp.int32)
            reduced[pi] = _chip_add(gW[pi][l], sums.pop((pi, l))[0], r, idx, reduced[pi], L, "grad_chip_add")

    for l in reversed(range(L)):
        x0, h1, proj, bexp, wbd, sk, ys, lt, ycn, x1, h2, gu, act = saved[l]
        keys = [(0, l + 1), (1, l + 1)]
        comm = chip_comm(keys) if l + 1 < L else None
        dgu, dx1, g_ffn[l], *r2 = _dact_dx(dx, Wd[l], gu, Wgu[l], x1, row(norm_ffn[l]), tm // 2, comm)
        if comm is not None:
            after_chip(keys, r2)
        gW[3][l] = _dw(act, dx, lambda t, s: (t, 0), D, 1, F // 2, tk // 2, "dw_down").reshape(N_CHIPS, F // N_CHIPS, D)
        gW[2][l] = _dw(h2, dgu, lambda t, s: (s // 2, t, s % 2), F // 2, N_CHIPS, D, tk, "dw_gate_up")
        gW[1][l] = _dw(ycn, dx1, lambda t, s: (t, 0), D, 1, D, tk, "dw_out").reshape(N_CHIPS, D // N_CHIPS, D)
        dya, dyb, dyc, dyd, g_out_gain[l] = _dycat(dx1, Wo[l], ys, row(mix_out_gain[l]), tm)
        keys = [(2, l), (3, l)]
        dpa, g_sgu_w[l], dbf, *r1 = _sgu_bwd(proj, sgu_w[l], bexp, dya, B, S, pair_comm(keys))
        after_pair(keys, r1)
        g_sgu_b[l] = dbf[:, ::HD].T
        dpb, dwbd, dsc = _pool_bwd(proj, wbd, row(pool_scale[l]), dyb, B, S)
        npg = len(POOL_WINDOWS)
        g_pool_w[l] = jnp.stack([dwbd[i * HD:(i + 1) * HD, i * HD:(i + 1) * HD] for i in range(npg)])
        g_pool_scale[l] = dsc[0]
        dcq, dckv, dsk, g_bias[l], *r2 = _swa_bwd(proj, sk, bias_tab, dyc, B, S, chip_comm([(3, l)]))
        after_chip([(3, l)], r2)
        g_sinks[l] = dsk[:, 0, 0] * float(BLK)
        ddq, ddk, ddv, *r2 = _sb_bwd(proj, lt, dyd, B, S, chip_comm([(2, l)]))
        after_chip([(2, l)], r2)
        dproj = [dpa, dpb, dcq, dckv, ddq, ddk, ddv]
        gW[0][l] = _dw_pieces(h1, dproj, w_in.shape[2], N_CHIPS, tk, "dw_in")
        keys = [(0, l), (1, l)]
        dx, g_mix[l], *r1 = _dx_norm_bwd(dproj, Win[l], x0, row(norm_mix[l]), dx1, tm, "dx_mix_exchange", pair_comm(keys))
        after_pair(keys, r1)
    grad_x = dx.reshape(B, S, D)

    after_chip(keys, _comm_only(chip_comm(keys), "grad_chip_exchange"))
    g_big = _pair_share(reduced, [g.shape[1] // 2 for g in reduced])

    g_rel_bias = _bias_reduce([g.reshape(4, -1) for g in g_bias], bucket).T
    small_g = [jnp.stack(g_sgu_w), jnp.stack(g_sgu_b), jnp.stack(g_pool_w), jnp.stack(g_pool_scale), jnp.stack(g_sinks),
               g_rel_bias, jnp.concatenate(g_out_gain), jnp.concatenate(g_mix), jnp.concatenate(g_ffn), g_final[0]]
    small_w = [sgu_w, sgu_b, pool_w, pool_scale, swa_sinks, rel_bias, mix_out_gain, norm_mix, norm_ffn, norm_final]
    small_m = [m_sgu_w, m_sgu_b, m_pool_w, m_pool_scale, m_swa_sinks, m_rel_bias, m_mix_out_gain, m_norm_mix, m_norm_ffn, m_norm_final]
    small_v = [v_sgu_w, v_sgu_b, v_pool_w, v_pool_scale, v_swa_sinks, v_rel_bias, v_mix_out_gain, v_norm_mix, v_norm_ffn, v_norm_final]
    shapes = [w.shape for w in small_w]
    bulk, fine = [small_g[0], small_g[2]], [small_g[i] for i in (1, 3, 4, 5, 6, 7, 8, 9)] + [loss_v[0, 0:1]]
    mine = [_pack(bulk), _pack(fine)]
    theirs = _comm_only(_SwapComm(mine), "small_pair_swap")
    slots_s = [_pair_sum_slot(a, b, kidx, dt) for a, b, dt in zip(mine, theirs, (bf16, f32))]
    shared = _comm_only(_SlotShareComm(slots_s), "small_chip_share")
    g_bulk = _unpack(_small_sum(shared[0]), [shapes[0], shapes[2]])
    *g_fine, loss = _unpack(_small_sum(shared[1]), [shapes[i] for i in (1, 3, 4, 5, 6, 7, 8, 9)] + [()])
    g_small = [g_bulk[0], g_fine[0], g_bulk[1]] + g_fine[1:]

    big_m = [m_w_in, m_w_out, m_w_gate_up, m_w_down]
    big_v = [v_w_in, v_w_out, v_w_gate_up, v_w_down]
    g_out, d_big, m_big, v_big = [], [], [], []
    for w, g, m, v in zip(big, g_big, big_m, big_v):
        two = lambda a: a.reshape(-1, a.shape[-1])
        rows = two(w).shape[0]
        cap = max(8, (1 << 20) // (4 * w.shape[-1]))
        tr = max(t for t in range(8, min(rows, cap) + 1, 8) if rows % t == 0)
        d2, m2, v2, g2 = _adamw(two(w), two(g), two(m), two(v), tr, "adamw_big", True)
        for lst, val in ((d_big, d2), (m_big, m2), (v_big, v2), (g_out, g2)):
            lst.append(val.reshape(w.shape))
    g_big = g_out

    g_small_packed = _pack(g_small)
    ds, ms, vs = _adamw(_pack(small_w), g_small_packed, _pack(small_m), _pack(small_v), g_small_packed.shape[0], "adamw_small")
    d_small, m_small, v_small = _unpack(ds, shapes), _unpack(ms, shapes), _unpack(vs, shapes)

    def order(bigs, smalls):
        return [bigs[0], bigs[1]] + list(smalls[0:9]) + [bigs[2], bigs[3], smalls[9]]

    return (loss, grad_x, *order(g_big, g_small), *order(d_big, d_small), *order(m_big, m_small), *order(v_big, v_small))
```

```python
import functools

import numpy as np
import jax
import jax.numpy as jnp
from jax import lax
from jax.experimental import pallas as pl
from jax.experimental.pallas import tpu as pltpu

f32 = jnp.float32
bf16 = jnp.bfloat16
_MXU = jnp.bfloat16

EPS = 1e-6
HD = 64
GW = 256
BLK = 128
SB_UNROLL = 2
SB_HEADS = 4
SB_CUT = -110.0
POOL_WINDOWS = (2, 4, 8, 16)
N_BUCKETS = 32
MAX_DISTANCE = 128
N_CHIPS = 4
N_DEV = 8
VMEM_LIMIT = 48 * 1024 * 1024

ADAM_LR = 0.001
ADAM_B1 = 0.9
ADAM_B2 = 0.999
ADAM_EPS = 1e-08
ADAM_WD = 0.01
ADAM_STEP = 10

SDS = jax.ShapeDtypeStruct
MESH = pl.DeviceIdType.MESH
HIGHEST = lax.Precision.HIGHEST
RESIDENT = pl.Buffered(1)
NT = (((1,), (1,)), ((), ()))
TN = (((0,), (0,)), ((), ()))


def _cp(*sem):
    return pltpu.CompilerParams(dimension_semantics=sem if sem else None, vmem_limit_bytes=VMEM_LIMIT)


def _mx(v):
    return v.astype(_MXU)


def _iota(shape, dim):
    return lax.broadcasted_iota(jnp.int32, shape, dim)


def _split_dot(a, tri):
    hi = a.astype(bf16)
    lo = (a - hi.astype(f32)).astype(bf16)
    return jnp.dot(hi, tri, preferred_element_type=f32) + jnp.dot(lo, tri, preferred_element_type=f32)


def _rms(xv):
    return lax.rsqrt(jnp.mean(xv * xv, axis=-1, keepdims=True) + EPS)


def _hosted_call(body, steps, in_specs, out_specs, out_shape, scratch, args, name, comm):
    n_in, n_out = len(in_specs), len(out_specs)
    c_args, c_in, c_out, c_shapes, aliases, c_scratch = _host_specs(comm, n_in, n_out)
    step = lambda v: (lambda: pl.program_id(0) == v)
    return pl.pallas_call(
        _host(body, n_in, n_out, len(scratch), comm, step(0), step(steps - 1)), grid=(steps,),
        in_specs=list(in_specs) + c_in, out_specs=list(out_specs) + c_out, out_shape=list(out_shape) + c_shapes,
        input_output_aliases=aliases, scratch_shapes=list(scratch) + c_scratch,
        name=name if comm is None else name + "_comm", compiler_params=_cp("arbitrary"))(*args, *c_args)


def _norm_mm(x, gain, w, tm, comm=None):
    T, D = x.shape
    NS, _, ns = w.shape

    def body(x_ref, g_ref, w_ref, h_ref, o_ref):
        xv = x_ref[...]
        h = (xv * _rms(xv) * g_ref[...]).astype(bf16)
        h_ref[...] = h
        for s in range(NS):
            o_ref[:, s * ns:(s + 1) * ns] = jnp.dot(_mx(h), w_ref[s], preferred_element_type=f32).astype(bf16)

    return _hosted_call(
        body, T // tm,
        [pl.BlockSpec((tm, D), lambda i: (i, 0)),
         pl.BlockSpec((1, D), lambda i: (0, 0)),
         pl.BlockSpec((NS, D, ns), lambda i: (0, 0, 0), pipeline_mode=RESIDENT)],
        [pl.BlockSpec((tm, D), lambda i: (i, 0)), pl.BlockSpec((tm, NS * ns), lambda i: (i, 0))],
        [SDS((T, D), bf16), SDS((T, NS * ns), bf16)], [], (x, gain, w), "norm_mm_in", comm)


def _mix_out_swiglu(ys, gain_mix, wo, x, gain_ffn, w, tm, comm=None):
    T, D = x.shape
    NS, _, ns = w.shape
    half = NS // 2

    def body(ya, yb, yc, yd, gm_ref, wo_ref, x_ref, gf_ref, w_ref, yn_ref, x1_ref, h_ref, gu_ref, a_ref):
        parts = []
        for m, r in enumerate((ya, yb, yc, yd)):
            y = r[...].astype(f32)
            parts.append((y * _rms(y) * gm_ref[:, m * GW:(m + 1) * GW]).astype(bf16))
        yn = jnp.concatenate(parts, axis=1)
        yn_ref[...] = yn
        x1 = x_ref[...] + jnp.dot(_mx(yn), wo_ref[...], preferred_element_type=f32)
        x1_ref[...] = x1
        hb = (x1 * _rms(x1) * gf_ref[...]).astype(bf16)
        h_ref[...] = hb
        h = _mx(hb)
        for s in range(half):
            cols = slice(s * ns, (s + 1) * ns)
            g = jnp.dot(h, w_ref[s], preferred_element_type=f32)
            u = jnp.dot(h, w_ref[s + half], preferred_element_type=f32)
            gu_ref[0, :, cols] = g.astype(bf16)
            gu_ref[1, :, cols] = u.astype(bf16)
            a_ref[:, cols] = (jax.nn.silu(g) * u).astype(bf16)

    yspec = pl.BlockSpec((tm, GW), lambda i: (i, 0))
    row, tile = pl.BlockSpec((1, D), lambda i: (0, 0)), pl.BlockSpec((tm, D), lambda i: (i, 0))
    return _hosted_call(
        body, T // tm,
        [yspec, yspec, yspec, yspec, row,
         pl.BlockSpec((D, D), lambda i: (0, 0), pipeline_mode=RESIDENT), tile, row,
         pl.BlockSpec((NS, D, ns), lambda i: (0, 0, 0), pipeline_mode=RESIDENT)],
        [tile, tile, tile, pl.BlockSpec((2, tm, half * ns), lambda i: (0, i, 0)),
         pl.BlockSpec((tm, half * ns), lambda i: (i, 0))],
        [SDS((T, D), bf16), SDS((T, D), f32), SDS((T, D), bf16), SDS((2, T, half * ns), bf16), SDS((T, half * ns), bf16)],
        [], (*ys, gain_mix, wo, x, gain_ffn, w), "mix_out_swiglu", comm)


def _mm_res(a, w, x, tm, comm=None):
    T, D = x.shape
    K = a.shape[1]

    def body(a_ref, w_ref, x_ref, o_ref):
        o_ref[...] = x_ref[...] + jnp.dot(_mx(a_ref[...]), w_ref[...], preferred_element_type=f32)

    return _hosted_call(
        body, T // tm,
        [pl.BlockSpec((tm, K), lambda i: (i, 0)),
         pl.BlockSpec((K, D), lambda i: (0, 0), pipeline_mode=RESIDENT),
         pl.BlockSpec((tm, D), lambda i: (i, 0))],
        [pl.BlockSpec((tm, D), lambda i: (i, 0))], [SDS((T, D), f32)], [], (a, w, x), "mm_res_down", comm)


def _final_loss(x, gain, tgt, tm):
    T, D = x.shape

    def body(x_ref, g_ref, t_ref, dx_ref, dg_ref, l_ref):
        @pl.when(pl.program_id(0) == 0)
        def _():
            dg_ref[...] = jnp.zeros_like(dg_ref)
            l_ref[...] = jnp.zeros_like(l_ref)
        xv = x_ref[...]
        g = g_ref[...]
        r = _rms(xv)
        xh = xv * r
        err = xh * g - t_ref[...]
        l_ref[...] += 0.5 * jnp.sum(jnp.mean(err * err, axis=-1, keepdims=True), axis=0, keepdims=True)
        dy = err * (1.0 / D)
        dg_ref[...] += jnp.sum(dy * xh, axis=0, keepdims=True)
        dxh = dy * g
        dx_ref[...] = r * (dxh - xh * jnp.mean(dxh * xh, axis=-1, keepdims=True))

    return pl.pallas_call(
        body, grid=(T // tm,),
        in_specs=[pl.BlockSpec((tm, D), lambda i: (i, 0)),
                  pl.BlockSpec((1, D), lambda i: (0, 0)),
                  pl.BlockSpec((tm, D), lambda i: (i, 0))],
        out_specs=[pl.BlockSpec((tm, D), lambda i: (i, 0)),
                   pl.BlockSpec((1, D), lambda i: (0, 0)),
                   pl.BlockSpec((1, BLK), lambda i: (0, 0))],
        out_shape=[SDS((T, D), f32), SDS((1, D), f32), SDS((1, BLK), f32)],
        name="final_loss", compiler_params=_cp("arbitrary"))(x, gain, tgt)


def _dact_dx(dx, wd, gu, w, x, gain, tm, comm=None):
    T, D = dx.shape
    F = wd.shape[0]
    NS, _, ns = w.shape
    half = NS // 2

    def body(dx_ref, wd_ref, gu_ref, w_ref, x_ref, g_ref, dgu_ref, dx1_ref, dg_ref):
        @pl.when(pl.program_id(0) == 0)
        def _():
            dg_ref[...] = jnp.zeros_like(dg_ref)
        dxv = dx_ref[...]
        dxb = _mx(dxv)
        dh = None
        for s in range(half):
            cols = slice(s * ns, (s + 1) * ns)
            da = lax.dot_general(dxb, wd_ref[s * ns:(s + 1) * ns, :], NT, preferred_element_type=f32)
            g = gu_ref[0, :, cols].astype(f32)
            u = gu_ref[1, :, cols].astype(f32)
            sg = jax.nn.sigmoid(g)
            dgs = (da * u * (sg * (1.0 + g * (1.0 - sg)))).astype(bf16)
            dus = (da * (g * sg)).astype(bf16)
            dgu_ref[0, :, cols] = dgs
            dgu_ref[1, :, cols] = dus
            part = (lax.dot_general(_mx(dgs), w_ref[s], NT, preferred_element_type=f32)
                    + lax.dot_general(_mx(dus), w_ref[s + half], NT, preferred_element_type=f32))
            dh = part if dh is None else dh + part
        xv = x_ref[...]
        r = _rms(xv)
        xh = xv * r
        dg_ref[...] += jnp.sum(dh * xh, axis=0, keepdims=True)
        dxh = dh * g_ref[...]
        dx1_ref[...] = dxv + r * (dxh - xh * jnp.mean(dxh * xh, axis=-1, keepdims=True))

    return _hosted_call(
        body, T // tm,
        [pl.BlockSpec((tm, D), lambda i: (i, 0)),
         pl.BlockSpec((F, D), lambda i: (0, 0), pipeline_mode=RESIDENT),
         pl.BlockSpec((2, tm, F), lambda i: (0, i, 0)),
         pl.BlockSpec((NS, D, ns), lambda i: (0, 0, 0), pipeline_mode=RESIDENT),
         pl.BlockSpec((tm, D), lambda i: (i, 0)),
         pl.BlockSpec((1, D), lambda i: (0, 0))],
        [pl.BlockSpec((2, tm, F), lambda i: (0, i, 0)), pl.BlockSpec((tm, D), lambda i: (i, 0)),
         pl.BlockSpec((1, D), lambda i: (0, 0))],
        [SDS((2, T, F), bf16), SDS((T, D), f32), SDS((1, D), f32)], [], (dx, wd, gu, w, x, gain), "dact_dx", comm)


def _dw(a, b, b_map, ns, NS, tka, tk, name):
    T, Ka = a.shape
    b_block = (tk, ns) if b.ndim == 2 else (1, tk, ns)

    def body(a_ref, b_ref, o_ref):
        bv = b_ref[...] if b.ndim == 2 else b_ref[0]
        part = lax.dot_general(_mx(a_ref[...]), _mx(bv), TN, preferred_element_type=f32)

        @pl.when(pl.program_id(2) == 0)
        def _():
            o_ref[0] = part

        @pl.when(pl.program_id(2) > 0)
        def _():
            o_ref[0] += part

    return pl.pallas_call(
        body, grid=(NS, Ka // tka, T // tk),
        in_specs=[pl.BlockSpec((tk, tka), lambda s, k, t: (t, k)),
                  pl.BlockSpec(b_block, lambda s, k, t: b_map(t, s))],
        out_specs=pl.BlockSpec((1, tka, ns), lambda s, k, t: (s, k, 0)),
        out_shape=SDS((NS, Ka, ns), f32),
        name=name, compiler_params=_cp("parallel", "parallel", "arbitrary"))(a, b)


def _dw_pieces(a, pieces, ns, NS, tk, name):
    T, Ka = a.shape
    n = len(pieces)

    def body(*refs):
        a_ref, b_refs, o_ref = refs[0], refs[1:1 + n], refs[1 + n]
        full = jnp.concatenate([r[...] for r in b_refs], axis=1)
        av = _mx(a_ref[...])
        parts = [lax.dot_general(av, _mx(full[:, s * ns:(s + 1) * ns]), TN, preferred_element_type=f32) for s in range(NS)]

        @pl.when(pl.program_id(0) == 0)
        def _():
            for s in range(NS):
                o_ref[s] = parts[s]

        @pl.when(pl.program_id(0) > 0)
        def _():
            for s in range(NS):
                o_ref[s] += parts[s]

    return pl.pallas_call(
        body, grid=(T // tk,),
        in_specs=[pl.BlockSpec((tk, Ka), lambda t: (t, 0))] + [pl.BlockSpec((tk, p.shape[1]), lambda t: (t, 0)) for p in pieces],
        out_specs=pl.BlockSpec((NS, Ka, ns), lambda t: (0, 0, 0)),
        out_shape=SDS((NS, Ka, ns), f32),
        name=name, compiler_params=_cp("arbitrary"))(a, *pieces)


def _dx_norm_bwd(pieces, w, x, gain, dxin, tm, name, comm=None):
    T, D = x.shape
    NS, _, ns = w.shape
    n_dy = len(pieces)

    def body(*refs):
        dy_refs = refs[:n_dy]
        w_ref, x_ref, g_ref, dxin_ref, dx_ref, dg_ref = refs[n_dy:]

        @pl.when(pl.program_id(0) == 0)
        def _():
            dg_ref[...] = jnp.zeros_like(dg_ref)
        full = jnp.concatenate([r[...] for r in dy_refs], axis=1)
        dh = None
        for s in range(NS):
            part = lax.dot_general(_mx(full[:, s * ns:(s + 1) * ns]), w_ref[s], NT, preferred_element_type=f32)
            dh = part if dh is None else dh + part
        xv = x_ref[...]
        r = _rms(xv)
        xh = xv * r
        dg_ref[...] += jnp.sum(dh * xh, axis=0, keepdims=True)
        dxh = dh * g_ref[...]
        dx_ref[...] = dxin_ref[...] + r * (dxh - xh * jnp.mean(dxh * xh, axis=-1, keepdims=True))

    dy_specs = [pl.BlockSpec((tm, p.shape[1]), lambda i: (i, 0)) for p in pieces]
    return _hosted_call(
        body, T // tm,
        dy_specs + [pl.BlockSpec((NS, D, ns), lambda i: (0, 0, 0), pipeline_mode=RESIDENT),
                    pl.BlockSpec((tm, D), lambda i: (i, 0)),
                    pl.BlockSpec((1, D), lambda i: (0, 0)),
                    pl.BlockSpec((tm, D), lambda i: (i, 0))],
        [pl.BlockSpec((tm, D), lambda i: (i, 0)), pl.BlockSpec((1, D), lambda i: (0, 0))],
        [SDS((T, D), f32), SDS((1, D), f32)], [], (*pieces, w, x, gain, dxin), name, comm)


def _dycat(dx, w, ys, gain, tm, comm=None):
    T, D = dx.shape

    def body(dx_ref, w_ref, ya, yb, yc, yd, g_ref, da, db, dc, dd, dg_ref):
        @pl.when(pl.program_id(0) == 0)
        def _():
            dg_ref[...] = jnp.zeros_like(dg_ref)
        dyn = lax.dot_general(_mx(dx_ref[...]), w_ref[...], NT, preferred_element_type=f32)
        for m, (r, o) in enumerate(((ya, da), (yb, db), (yc, dc), (yd, dd))):
            cols = slice(m * GW, (m + 1) * GW)
            y = r[...].astype(f32)
            rs = _rms(y)
            yh = y * rs
            d = dyn[:, cols]
            dg_ref[:, cols] += jnp.sum(d * yh, axis=0, keepdims=True)
            dyh = d * g_ref[:, cols]
            o[...] = (rs * (dyh - yh * jnp.mean(dyh * yh, axis=-1, keepdims=True))).astype(bf16)

    yspec = pl.BlockSpec((tm, GW), lambda i: (i, 0))
    return _hosted_call(
        body, T // tm,
        [pl.BlockSpec((tm, D), lambda i: (i, 0)),
         pl.BlockSpec((D, D), lambda i: (0, 0), pipeline_mode=RESIDENT),
         yspec, yspec, yspec, yspec,
         pl.BlockSpec((1, D), lambda i: (0, 0))],
        [yspec, yspec, yspec, yspec, pl.BlockSpec((1, D), lambda i: (0, 0))],
        [SDS((T, GW), bf16)] * 4 + [SDS((1, D), f32)], [], (dx, w, *ys, gain), "dycat", comm)


def _sgu_consts():
    r, c = _iota((GW, GW), 0), _iota((GW, GW), 1)
    seg = (r // HD == c // HD).astype(f32)
    tr, ts = _iota((BLK, BLK), 0), _iota((BLK, BLK), 1)
    causal = ts <= tr
    lane_head = _iota((BLK, GW), 1) // HD
    return seg, causal, lane_head


def _split3_dot(a, ones):
    hi = a.astype(bf16)
    r1 = a - hi.astype(f32)
    mid = r1.astype(bf16)
    lo = (r1 - mid.astype(f32)).astype(bf16)
    dot = functools.partial(jnp.dot, preferred_element_type=f32)
    return dot(hi, ones) + dot(mid, ones) + dot(lo, ones)


def _sgu_chunks(aus, avs, w, bexp, consts):
    seg, causal, lane_head = consts
    segb = seg.astype(bf16)
    nh = GW // HD
    vs = [jax.nn.gelu(av) for av in avs]
    mus = [_split3_dot(v, segb) * (1.0 / HD) for v in vs]
    vcs = [v - mu for v, mu in zip(vs, mus)]
    vars_ = [_split3_dot(vc * vc, segb) * (1.0 / HD) for vc in vcs]
    vns = [_mx(vc * lax.rsqrt(var + EPS)) for vc, var in zip(vcs, vars_)]
    whs = [_mx(jnp.where(causal, w[h], 0.0)) for h in range(nh)]
    mixes = [[jnp.dot(whs[h], vn, preferred_element_type=f32) for h in range(nh)] for vn in vns]
    out = []
    for au, ms in zip(aus, mixes):
        mix = bexp
        for h in range(nh):
            mix = mix + jnp.where(lane_head == h, ms[h], 0.0)
        out.append(jax.nn.gelu(au) * mix)
    return out


def _sgu_group(S):
    nc = S // BLK
    return 4 if nc % 4 == 0 else (2 if nc % 2 == 0 else 1)


def _sgu_fwd(proj, w, bexp, B, S):
    G = _sgu_group(S)

    def body(au_ref, av_ref, w_ref, b_ref, y_ref):
        consts = _sgu_consts()
        wv, bv = w_ref[...], b_ref[...]

        def group(n, c):
            rows = [pl.ds(pl.multiple_of((n * G + j) * BLK, BLK), BLK) for j in range(G)]
            ys = _sgu_chunks([au_ref[r, :].astype(f32) for r in rows], [av_ref[r, :].astype(f32) for r in rows],
                             wv, bv, consts)
            for r, y in zip(rows, ys):
                y_ref[r, :] = y.astype(bf16)
            return c
        lax.fori_loop(0, S // BLK // G, group, 0)

    return pl.pallas_call(
        body, grid=(B,),
        in_specs=[pl.BlockSpec((S, GW), lambda b: (b, 0)),
                  pl.BlockSpec((S, GW), lambda b: (b, 1)),
                  pl.BlockSpec((GW // HD, BLK, BLK), lambda b: (0, 0, 0)),
                  pl.BlockSpec((BLK, GW), lambda b: (0, 0))],
        out_specs=pl.BlockSpec((S, GW), lambda b: (b, 0)),
        out_shape=SDS((B * S, GW), bf16),
        name="sgu_fwd", compiler_params=_cp("parallel"))(proj, proj, w, bexp)


def _sgu_bwd(proj, w, bexp, dy, B, S, comm=None):
    def body(au_ref, av_ref, w_ref, b_ref, dy_ref, dp_ref, dw_ref, db_ref):
        @pl.when(pl.program_id(0) == 0)
        def _():
            dw_ref[...] = jnp.zeros_like(dw_ref)
            db_ref[...] = jnp.zeros_like(db_ref)
        consts = _sgu_consts()
        wv, bv = w_ref[...], b_ref[...]
        fn = lambda aus, avs, ww, bb: _sgu_chunks(aus, avs, ww, bb, consts)
        G = _sgu_group(S)

        def group(n, carry):
            dw_acc, db_acc = carry
            rows = [pl.ds(pl.multiple_of((n * G + j) * BLK, BLK), BLK) for j in range(G)]
            _, vjp = jax.vjp(fn, [au_ref[r, :].astype(f32) for r in rows], [av_ref[r, :].astype(f32) for r in rows], wv, bv)
            daus, davs, dwc, dbc = vjp([dy_ref[r, :].astype(f32) for r in rows])
            for r, dau, dav in zip(rows, daus, davs):
                dp_ref[r, 0:GW] = dau.astype(bf16)
                dp_ref[r, GW:2 * GW] = dav.astype(bf16)
            return dw_acc + dwc, db_acc + dbc
        dw_acc, db_acc = lax.fori_loop(0, S // BLK // G, group, (jnp.zeros(wv.shape, f32), jnp.zeros(bv.shape, f32)))
        dw_ref[...] += dw_acc
        db_ref[...] += jnp.dot(db_acc, consts[0], precision=HIGHEST, preferred_element_type=f32)

    return _hosted_call(
        body, B,
        [pl.BlockSpec((S, GW), lambda b: (b, 0)),
         pl.BlockSpec((S, GW), lambda b: (b, 1)),
         pl.BlockSpec((GW // HD, BLK, BLK), lambda b: (0, 0, 0)),
         pl.BlockSpec((BLK, GW), lambda b: (0, 0)),
         pl.BlockSpec((S, GW), lambda b: (b, 0))],
        [pl.BlockSpec((S, 2 * GW), lambda b: (b, 0)),
         pl.BlockSpec((GW // HD, BLK, BLK), lambda b: (0, 0, 0)),
         pl.BlockSpec((BLK, GW), lambda b: (0, 0))],
        [SDS((B * S, 2 * GW), bf16), SDS((GW // HD, BLK, BLK), f32), SDS((BLK, GW), f32)], [],
        (proj, proj, w, bexp, dy), "sgu_bwd", comm)


def _pool_parts(p):
    n = p.shape[0]
    r = _iota(p.shape, 0)
    lg = _iota(p.shape, 1) // HD

    def sh(v, k):
        return jnp.where(r >= k, pltpu.roll(v, k, 0), 0.0)
    s2 = p + sh(p, 1)
    s4 = s2 + sh(s2, 2)
    s8 = s4 + sh(s4, 4)
    s16 = s8 + sh(s8, 8)
    ws = jnp.where(lg == 0, s2, jnp.where(lg == 1, s4, jnp.where(lg == 2, s8, s16)))
    wlen = jnp.where(lg == 0, 2, jnp.where(lg == 1, 4, jnp.where(lg == 2, 8, 16)))
    cnt = jnp.minimum(r + 1, wlen).astype(f32)
    del n
    return ws / cnt - p, cnt, lg


def _pool_fwd(proj, wbd, scale, B, S):
    def body(p_ref, w_ref, s_ref, y_ref):
        y, _, _ = _pool_parts(p_ref[...].astype(f32))
        y_ref[...] = (jnp.dot(_mx(y), _mx(w_ref[...]), preferred_element_type=f32) * s_ref[...]).astype(bf16)

    return pl.pallas_call(
        body, grid=(B,),
        in_specs=[pl.BlockSpec((S, GW), lambda b: (b, 2)),
                  pl.BlockSpec((GW, GW), lambda b: (0, 0)),
                  pl.BlockSpec((1, GW), lambda b: (0, 0))],
        out_specs=pl.BlockSpec((S, GW), lambda b: (b, 0)),
        out_shape=SDS((B * S, GW), bf16),
        name="pool_fwd", compiler_params=_cp("parallel"))(proj, wbd, scale)


def _pool_bwd(proj, wbd, scale, dy, B, S):
    def body(p_ref, w_ref, s_ref, dy_ref, dp_ref, dw_ref, ds_ref):
        @pl.when(pl.program_id(0) == 0)
        def _():
            dw_ref[...] = jnp.zeros_like(dw_ref)
            ds_ref[...] = jnp.zeros_like(ds_ref)
        y, cnt, lg = _pool_parts(p_ref[...].astype(f32))
        wv = _mx(w_ref[...])
        z = jnp.dot(_mx(y), wv, preferred_element_type=f32)
        dout = dy_ref[...].astype(f32)
        ds_ref[...] += jnp.sum(dout * z, axis=0, keepdims=True)
        dz = _mx(dout * s_ref[...])
        dw_ref[...] += lax.dot_general(_mx(y), dz, TN, preferred_element_type=f32)
        dyv = lax.dot_general(dz, wv, NT, preferred_element_type=f32)
        n = dyv.shape[0]
        r = _iota(dyv.shape, 0)

        def ush(v, k):
            return jnp.where(r < n - k, pltpu.roll(v, n - k, 0), 0.0)
        gq = dyv / cnt
        a2 = gq + ush(gq, 1)
        a4 = a2 + ush(a2, 2)
        a8 = a4 + ush(a4, 4)
        a16 = a8 + ush(a8, 8)
        adj = jnp.where(lg == 0, a2, jnp.where(lg == 1, a4, jnp.where(lg == 2, a8, a16)))
        dp_ref[...] = (adj - dyv).astype(bf16)

    return pl.pallas_call(
        body, grid=(B,),
        in_specs=[pl.BlockSpec((S, GW), lambda b: (b, 2)),
                  pl.BlockSpec((GW, GW), lambda b: (0, 0)),
                  pl.BlockSpec((1, GW), lambda b: (0, 0)),
                  pl.BlockSpec((S, GW), lambda b: (b, 0))],
        out_specs=[pl.BlockSpec((S, GW), lambda b: (b, 0)),
                   pl.BlockSpec((GW, GW), lambda b: (0, 0)),
                   pl.BlockSpec((1, GW), lambda b: (0, 0))],
        out_shape=[SDS((B * S, GW), bf16), SDS((GW, GW), f32), SDS((1, GW), f32)],
        name="pool_bwd", compiler_params=_cp("arbitrary"))(proj, wbd, scale, dy)


def _t5_bucket_table():
    dist = (np.arange(BLK)[:, None] + BLK) - np.arange(2 * BLK)[None, :]
    d = np.clip(dist, 0, BLK - 1)
    max_exact = N_BUCKETS // 2
    df = np.maximum(d, 1).astype(np.float32)
    large = max_exact + (np.log(df / max_exact) / np.float32(np.log(MAX_DISTANCE / max_exact))
                         * (N_BUCKETS - max_exact)).astype(np.int32)
    large = np.minimum(large, N_BUCKETS - 1)
    return np.where(d < max_exact, d, large).astype(np.int32)


def _swa_blocks(qs, kx, vx, sinks, biases, n):
    G = len(qs)
    heads = [(p, g) for p in range(2) for g in range(2)]
    ri, ci = _iota((BLK, BLK), 0), _iota((BLK, BLK), 1)
    qi, ki = _iota((BLK, 2 * BLK), 0), _iota((BLK, 2 * BLK), 1)
    dist = qi + BLK - ki
    band = (dist >= 0) & (dist < BLK)
    masks = [band & ((ki >= BLK) | (n > 0))] + [band] * (G - 1)
    kb, vb = _mx(kx), _mx(vx)
    qsel = [[None] * 4 for _ in range(G)]
    vs = []
    for h, (p, g) in enumerate(heads):
        selq = ((ri - g * HD == ci - p * HD) & (ri >= g * HD) & (ri < (g + 1) * HD)).astype(_MXU)
        selv = ((ci - g * HD == ri - p * HD) & (ci >= g * HD) & (ci < (g + 1) * HD)).astype(_MXU)
        for b in range(G):
            qsel[b][h] = _mx(jnp.dot(_mx(qs[b][p]), selq, preferred_element_type=f32))
        vs.append(_mx(jnp.dot(vb, selv, preferred_element_type=f32)))
    zs = [[lax.dot_general(qsel[b][h], kb[b * BLK:(b + 2) * BLK], NT, preferred_element_type=f32) * (HD ** -0.5)
           for h in range(4)] for b in range(G)]
    prs = [[None] * 4 for _ in range(G)]
    for b in range(G):
        for h in range(4):
            z = jnp.where(masks[b], zs[b][h] + biases[h], -1e30)
            s = jnp.mean(sinks[h], axis=-1, keepdims=True)
            m = jnp.maximum(jnp.max(z, axis=-1, keepdims=True), s)
            e = jnp.exp(z - m)
            prs[b][h] = _mx(e / (jnp.sum(e, axis=-1, keepdims=True) + jnp.exp(s - m)))
    outs = [[jnp.dot(prs[b][h], vs[h][b * BLK:(b + 2) * BLK], preferred_element_type=f32) for h in range(4)]
            for b in range(G)]
    return [[o[0] + o[1], o[2] + o[3]] for o in outs]


def _swa_group(S):
    return 2 if (S // BLK) % 2 == 0 else 1


def _swa_rows(n, G):
    blk = lambda j: pl.ds(pl.multiple_of(j * BLK, BLK), BLK)
    return [blk(jnp.maximum(n - 1, 0))] + [blk(n + b) for b in range(G)]


def _swa_fwd(proj, sinks, bias, B, S, comm=None):
    G = _swa_group(S)

    def body(q_ref, kv_ref, s_ref, b_ref, y_ref):
        def group(i, c):
            n = i * G
            rows = _swa_rows(n, G)
            kx = jnp.concatenate([kv_ref[r, 0:BLK] for r in rows], axis=0).astype(f32)
            vx = jnp.concatenate([kv_ref[r, BLK:2 * BLK] for r in rows], axis=0).astype(f32)
            qs = [[q_ref[r, 0:BLK].astype(f32), q_ref[r, BLK:2 * BLK].astype(f32)] for r in rows[1:]]
            outs = _swa_blocks(qs, kx, vx, [s_ref[h] for h in range(4)], [b_ref[h] for h in range(4)], n)
            for r, (o0, o1) in zip(rows[1:], outs):
                y_ref[r, 0:BLK] = o0.astype(bf16)
                y_ref[r, BLK:2 * BLK] = o1.astype(bf16)
            return c
        lax.fori_loop(0, S // BLK // G, group, 0)

    return _hosted_call(
        body, B,
        [pl.BlockSpec((S, GW), lambda b: (b, 3)),
         pl.BlockSpec((S, GW), lambda b: (b, 4)),
         pl.BlockSpec((4, 1, BLK), lambda b: (0, 0, 0)),
         pl.BlockSpec((4, BLK, 2 * BLK), lambda b: (0, 0, 0))],
        [pl.BlockSpec((S, GW), lambda b: (b, 0))], [SDS((B * S, GW), bf16)], [],
        (proj, proj, sinks, bias), "swa_fwd", comm)


def _swa_bwd(proj, sinks, bias, dy, B, S, comm=None):
    def body(q_ref, kv_ref, s_ref, b_ref, dy_ref, dq_ref, dkv_ref, ds_ref, db_ref, acc_ref):
        @pl.when(pl.program_id(0) == 0)
        def _():
            ds_ref[...] = jnp.zeros_like(ds_ref)
            db_ref[...] = jnp.zeros_like(db_ref)
        acc_ref[...] = jnp.zeros_like(acc_ref)

        G = _swa_group(S)

        def group(i, c):
            n = i * G
            rows = _swa_rows(n, G)
            kx = jnp.concatenate([kv_ref[r, 0:BLK] for r in rows], axis=0).astype(f32)
            vx = jnp.concatenate([kv_ref[r, BLK:2 * BLK] for r in rows], axis=0).astype(f32)
            qs = [[q_ref[r, 0:BLK].astype(f32), q_ref[r, BLK:2 * BLK].astype(f32)] for r in rows[1:]]
            dos = [[dy_ref[r, 0:BLK].astype(f32), dy_ref[r, BLK:2 * BLK].astype(f32)] for r in rows[1:]]
            fn = functools.partial(_swa_blocks, n=n)
            _, vjp = jax.vjp(fn, qs, kx, vx, [s_ref[h] for h in range(4)], [b_ref[h] for h in range(4)])
            dqs, dkx, dvx, dss, dbs = vjp(dos)
            for r, (dq0, dq1) in zip(rows[1:], dqs):
                dq_ref[r, 0:BLK] = dq0.astype(bf16)
                dq_ref[r, BLK:2 * BLK] = dq1.astype(bf16)
            for h in range(4):
                ds_ref[h] += dss[h]
                db_ref[h] += dbs[h]
            for j, r in enumerate(rows):
                acc_ref[r, 0:BLK] += dkx[j * BLK:(j + 1) * BLK]
                acc_ref[r, BLK:2 * BLK] += dvx[j * BLK:(j + 1) * BLK]
            return c
        lax.fori_loop(0, S // BLK // G, group, 0)
        dkv_ref[...] = acc_ref[...].astype(bf16)

    c_args, c_in, c_out, c_shapes, aliases, c_scratch = _host_specs(comm, 5, 4)
    step = lambda v: (lambda: pl.program_id(0) == v)
    return pl.pallas_call(
        _host(body, 5, 4, 1, comm, step(0), step(B - 1)), grid=(B,),
        in_specs=[pl.BlockSpec((S, GW), lambda b: (b, 3)),
                  pl.BlockSpec((S, GW), lambda b: (b, 4)),
                  pl.BlockSpec((4, 1, BLK), lambda b: (0, 0, 0)),
                  pl.BlockSpec((4, BLK, 2 * BLK), lambda b: (0, 0, 0)),
                  pl.BlockSpec((S, GW), lambda b: (b, 0))] + c_in,
        out_specs=[pl.BlockSpec((S, GW), lambda b: (b, 0)),
                   pl.BlockSpec((S, GW), lambda b: (b, 0)),
                   pl.BlockSpec((4, 1, BLK), lambda b: (0, 0, 0)),
                   pl.BlockSpec((4, BLK, 2 * BLK), lambda b: (0, 0, 0))] + c_out,
        out_shape=[SDS((B * S, GW), bf16), SDS((B * S, GW), bf16), SDS((4, 1, BLK), f32),
                   SDS((4, BLK, 2 * BLK), f32)] + c_shapes,
        input_output_aliases=aliases, scratch_shapes=[pltpu.VMEM((S, GW), f32)] + c_scratch,
        name="swa_bwd" if comm is None else "swa_bwd_exchange",
        compiler_params=_cp("arbitrary"))(proj, proj, sinks, bias, dy, *c_args)


def _log1m_parts(z):
    t = jnp.exp(-jnp.abs(z))
    return jnp.minimum(-z, 0.0) - jnp.log(1.0 + t), t


def _log1m(z):
    return _log1m_parts(z)[0]


def _sigmoid_from(z, t):
    return jnp.where(z >= 0.0, 1.0, t) / (1.0 + t)


def _sb_consts(tri):
    r2, c2 = _iota((2 * BLK, 2 * BLK), 0), _iota((2 * BLK, 2 * BLK), 1)
    tri2 = (tri(r2, c2) & (r2 // BLK == c2 // BLK)).astype(bf16)
    ri, ci = _iota((BLK, 2 * BLK), 0), _iota((BLK, 2 * BLK), 1)
    strict2 = (ci % BLK) < ri
    head0 = _iota((BLK, BLK), 1) < HD
    return tri2, strict2, head0


def _sb_stack_kv(k_ref, v_ref, kst_ref, vst_ref, head0, nb):
    def one(kb, c):
        krows = pl.ds(pl.multiple_of(kb * BLK, BLK), BLK)
        for p in range(2):
            for src, dst in ((k_ref, kst_ref), (v_ref, vst_ref)):
                t = src[krows, p * BLK:(p + 1) * BLK]
                dst[p, kb] = _mx(jnp.concatenate([jnp.where(head0, t, 0.0), jnp.where(head0, 0.0, t)], axis=0))
        return c
    lax.fori_loop(0, nb, one, 0)


def _sb_load_kv(kst_ref, vst_ref, kb):
    return [kst_ref[p, kb] for p in range(2)], [vst_ref[p, kb] for p in range(2)]


def _two_halves(a, b):
    return jnp.concatenate([jnp.broadcast_to(a, (BLK, BLK)), jnp.broadcast_to(b, (BLK, BLK))], axis=1)


def _half_sums(t):
    return jnp.sum(t[:, :BLK], axis=-1, keepdims=True), jnp.sum(t[:, BLK:], axis=-1, keepdims=True)


def _sb_fwd(proj, B, S, comm=None):
    def body(q_ref, k_ref, v_ref, y_ref, lt_ref, kst_ref, vst_ref):
        ci = _iota((BLK, BLK), 1)
        above2, strict2, head0 = _sb_consts(lambda r, c: r > c)
        _sb_stack_kv(k_ref, v_ref, kst_ref, vst_ref, head0, S // BLK)

        def step(qs, kbs, diags, carry):
            U = range(len(kbs))
            ok = [None if diags[u] else kbs[u] >= 0 for u in U]
            kv = [_sb_load_kv(kst_ref, vst_ref, jnp.maximum(kb, 0)) for kb in kbs]
            zs = [[lax.dot_general(qs[p], kks[p], NT, preferred_element_type=f32) for p in range(2)] for kks, _ in kv]
            Ls = [[jnp.where(strict2, _log1m(z), 0.0) if diags[u] else _log1m(z) for z in zs[u]] for u in U]
            tails = [[_split_dot(L, above2) for L in Lu] for Lu in Ls]
            carry = list(carry)
            for u in U:
                for p in range(2):
                    R0, R1, acc = carry[3 * p:3 * p + 3]
                    w = jnp.exp(zs[u][p] + Ls[u][p] + tails[u][p] + _two_halves(R0, R1))
                    s0, s1 = _half_sums(Ls[u][p])
                    if diags[u]:
                        w = jnp.where(strict2, w, 0.0)
                    else:
                        w, s0, s1 = (jnp.where(ok[u], t, 0.0) for t in (w, s0, s1))
                    acc = acc + jnp.dot(_mx(w), kv[u][1][p], preferred_element_type=f32)
                    carry[3 * p:3 * p + 3] = [R0 + s0, R1 + s1, acc]
            return tuple(carry)

        def qblock(n, c):
            qrows = pl.ds(pl.multiple_of(n * BLK, BLK), BLK)
            qs = [_mx(q_ref[qrows, p * BLK:(p + 1) * BLK] * (HD ** -0.5)) for p in range(2)]
            z1, z2 = jnp.zeros((BLK, 1), f32), jnp.zeros((BLK, BLK), f32)
            near = [n - 1 - u for u in range(SB_UNROLL)]
            carry = step(qs, [n] + near, [True] + [False] * SB_UNROLL, (z1, z1, z2, z1, z1, z2))
            far = jnp.maximum(n - SB_UNROLL, 0)
            trips = (far + SB_UNROLL - 1) // SB_UNROLL

            def live(st):
                worst = jnp.maximum(jnp.maximum(st[1], st[2]), jnp.maximum(st[4], st[5]))
                return st[0] < trips

            def trip(st):
                i = st[0]
                kbs = [far - 1 - SB_UNROLL * i - u for u in range(SB_UNROLL)]
                return (i + 1,) + step(qs, kbs, [False] * SB_UNROLL, st[1:])
            done, *res = lax.while_loop(live, trip, (jnp.int32(0),) + carry)
            lt = jnp.where(ci == SB_HEADS, done.astype(f32), 0.0)
            for p in range(2):
                y_ref[qrows, p * BLK:(p + 1) * BLK] = res[3 * p + 2].astype(bf16)
                lt = lt + jnp.where(ci == 2 * p, res[3 * p], 0.0) + jnp.where(ci == 2 * p + 1, res[3 * p + 1], 0.0)
            lt_ref[qrows, :] = lt
            return c
        lax.fori_loop(0, S // BLK, qblock, 0)

    spec = lambda j: pl.BlockSpec((S, GW), lambda b: (b, j))
    c_args, c_in, c_out, c_shapes, aliases, c_scratch = _host_specs(comm, 3, 2)
    step = lambda v: (lambda: pl.program_id(0) == v)
    stacked = pltpu.VMEM((2, S // BLK, 2 * BLK, BLK), _MXU)
    return pl.pallas_call(
        _host(body, 3, 2, 2, comm, step(0), step(B - 1)), grid=(B,),
        in_specs=[spec(5), spec(6), spec(7)] + c_in,
        out_specs=[pl.BlockSpec((S, GW), lambda b: (b, 0)), pl.BlockSpec((S, BLK), lambda b: (b, 0))] + c_out,
        out_shape=[SDS((B * S, GW), bf16), SDS((B * S, BLK), f32)] + c_shapes,
        input_output_aliases=aliases, scratch_shapes=[stacked, stacked] + c_scratch,
        name="sb_fwd" if comm is None else "sb_fwd_gather",
        compiler_params=_cp("arbitrary"))(proj, proj, proj, *c_args)


def _sb_bwd(proj, ltot, dy, B, S, comm=None):
    def body(q_ref, k_ref, v_ref, lt_ref, dy_ref, dq_ref, dk_ref, dv_ref, dka_ref, dva_ref, kst_ref, vst_ref):
        ci = _iota((BLK, BLK), 1)
        upto2, strict2, head0 = _sb_consts(lambda r, c: r <= c)
        below2, _, _ = _sb_consts(lambda r, c: r < c)
        dka_ref[...] = jnp.zeros_like(dka_ref)
        dva_ref[...] = jnp.zeros_like(dva_ref)
        _sb_stack_kv(k_ref, v_ref, kst_ref, vst_ref, head0, S // BLK)

        def step(qs, dos, lts, kbs, ok, diags, top, carry):
            U = range(len(kbs))
            kbs = [jnp.clip(kb, 0, top) for kb in kbs]
            kv = [_sb_load_kv(kst_ref, vst_ref, kb) for kb in kbs]
            zs = [[lax.dot_general(qs[p], kv[u][0][p], NT, preferred_element_type=f32) for p in range(2)] for u in U]
            dws = [[lax.dot_general(dos[p], kv[u][1][p], NT, preferred_element_type=f32) for p in range(2)] for u in U]
            parts = [[_log1m_parts(z) for z in zu] for zu in zs]
            Ls = [[jnp.where(strict2, lt[0], 0.0) if diags[u] else lt[0] for lt in parts[u]] for u in U]
            pins = [[_split_dot(L, upto2) for L in Lu] for Lu in Ls]
            carry = list(carry)
            ws, das = [], []
            for u in U:
                wu, dau = [], []
                for p in range(2):
                    PL0, PL1 = carry[5 * p], carry[5 * p + 1]
                    tail = _two_halves(lts[2 * p] - PL0, lts[2 * p + 1] - PL1) - pins[u][p]
                    w = jnp.exp(zs[u][p] + Ls[u][p] + tail)
                    l0, l1 = _half_sums(Ls[u][p])
                    if diags[u]:
                        w = jnp.where(strict2, w, 0.0)
                    else:
                        w, l0, l1 = (jnp.where(ok[u], t, 0.0) for t in (w, l0, l1))
                    carry[5 * p], carry[5 * p + 1] = PL0 + l0, PL1 + l1
                    wu.append(w)
                    dau.append(w * dws[u][p])
                ws.append(wu)
                das.append(dau)
            pexs = [[_split_dot(da, below2) for da in dau] for dau in das]
            dzs = []
            for u in U:
                dzu = []
                for p in range(2):
                    dL = _two_halves(carry[5 * p + 2], carry[5 * p + 3]) + pexs[u][p]
                    sg = _sigmoid_from(zs[u][p], parts[u][p][1])
                    dz = das[u][p] * (1.0 - sg) - dL * sg
                    dz = jnp.where(strict2 if diags[u] else ok[u], dz, 0.0)
                    a0, a1 = _half_sums(das[u][p])
                    carry[5 * p + 2], carry[5 * p + 3] = carry[5 * p + 2] + a0, carry[5 * p + 3] + a1
                    dzu.append(_mx(dz))
                dzs.append(dzu)
            dqs = [[jnp.dot(dzs[u][p], kv[u][0][p], preferred_element_type=f32) for p in range(2)] for u in U]
            dks = [[lax.dot_general(dzs[u][p], qs[p], TN, preferred_element_type=f32) for p in range(2)] for u in U]
            dvs = [[lax.dot_general(_mx(ws[u][p]), dos[p], TN, preferred_element_type=f32) for p in range(2)] for u in U]
            for u in U:
                krows = pl.ds(pl.multiple_of(kbs[u] * BLK, BLK), BLK)
                for p in range(2):
                    lanes = slice(p * BLK, (p + 1) * BLK)
                    dka_ref[krows, lanes] += jnp.where(head0, dks[u][p][:BLK], dks[u][p][BLK:])
                    dva_ref[krows, lanes] += jnp.where(head0, dvs[u][p][:BLK], dvs[u][p][BLK:])
                    carry[5 * p + 4] = carry[5 * p + 4] + dqs[u][p]
            return tuple(carry)

        def qblock(n, c):
            qrows = pl.ds(pl.multiple_of(n * BLK, BLK), BLK)
            ltb = lt_ref[qrows, :]
            lts = [jnp.sum(jnp.where(ci == h, ltb, 0.0), axis=-1, keepdims=True) for h in range(4)]
            qs = [_mx(q_ref[qrows, p * BLK:(p + 1) * BLK] * (HD ** -0.5)) for p in range(2)]
            dos = [_mx(dy_ref[qrows, p * BLK:(p + 1) * BLK]) for p in range(2)]
            z1, z2 = jnp.zeros((BLK, 1), f32), jnp.zeros((BLK, BLK), f32)
            done = jnp.max(jnp.where(ci == SB_HEADS, ltb, 0.0)).astype(jnp.int32)
            far = jnp.maximum(n - SB_UNROLL, 0)
            first = 0 * n

            def trip(i, cr):
                kbs = [first + SB_UNROLL * i + u for u in range(SB_UNROLL)]
                return step(qs, dos, lts, kbs, [kb < far for kb in kbs], [False] * SB_UNROLL, n, cr)
            carry = lax.fori_loop(0, (far - first + SB_UNROLL - 1) // SB_UNROLL, trip, (z1, z1, z1, z1, z2) * 2)
            near = [n - SB_UNROLL + u for u in range(SB_UNROLL)]
            res = step(qs, dos, lts, near + [n], [kb >= 0 for kb in near] + [None], [False] * SB_UNROLL + [True], n, carry)
            for p in range(2):
                dq_ref[qrows, p * BLK:(p + 1) * BLK] = (res[5 * p + 4] * (HD ** -0.5)).astype(bf16)
            return c
        lax.fori_loop(0, S // BLK, qblock, 0)
        dk_ref[...] = dka_ref[...].astype(bf16)
        dv_ref[...] = dva_ref[...].astype(bf16)

    spec = lambda j: pl.BlockSpec((S, GW), lambda b: (b, j))
    o = pl.BlockSpec((S, GW), lambda b: (b, 0))
    c_args, c_in, c_out, c_shapes, aliases, c_scratch = _host_specs(comm, 5, 3)
    step = lambda v: (lambda: pl.program_id(0) == v)
    stacked = pltpu.VMEM((2, S // BLK, 2 * BLK, BLK), _MXU)
    return pl.pallas_call(
        _host(body, 5, 3, 4, comm, step(0), step(B - 1)), grid=(B,),
        in_specs=[spec(5), spec(6), spec(7), pl.BlockSpec((S, BLK), lambda b: (b, 0)), o] + c_in,
        out_specs=[o, o, o] + c_out,
        out_shape=[SDS((B * S, GW), bf16)] * 3 + c_shapes,
        input_output_aliases=aliases,
        scratch_shapes=[pltpu.VMEM((S, GW), f32), pltpu.VMEM((S, GW), f32), stacked, stacked] + c_scratch,
        name="sb_bwd" if comm is None else "sb_bwd_exchange",
        compiler_params=_cp("arbitrary"))(proj, proj, proj, ltot, dy, *c_args)


def _bias_expand(rel_bias_t, bucket):
    n = bucket.shape[1]

    def body(r_ref, b_ref, o_ref):
        onehot = (_iota((N_BUCKETS, n), 0) == b_ref[...]).astype(f32)
        o_ref[...] = jnp.dot(r_ref[...], onehot, precision=HIGHEST, preferred_element_type=f32)
    return pl.pallas_call(body, out_shape=SDS((rel_bias_t.shape[0], n), f32), name="bias_expand",
                          compiler_params=_cp())(rel_bias_t, bucket)


def _bias_reduce(dbias, bucket):
    n = bucket.shape[1]

    def body(*refs):
        b_ref, g_ref = refs[-2], refs[-1]
        d = refs[0][...]
        for r in refs[1:-2]:
            d = d + r[...]
        onehot = (_iota((N_BUCKETS, n), 0) == b_ref[...]).astype(f32)
        g_ref[...] = lax.dot_general(d, onehot, NT, precision=HIGHEST, preferred_element_type=f32)
    return pl.pallas_call(body, out_shape=SDS((dbias[0].shape[0], N_BUCKETS), f32), name="bias_reduce",
                          compiler_params=_cp())(*dbias, bucket)


def _adamw(w, g, m, v, tr, name, emit_g=False):
    R, C = w.shape

    def body(w_ref, g_ref, m_ref, v_ref, d_ref, m2_ref, v2_ref, *g_out):
        gv = g_ref[...]
        if emit_g:
            g_out[0][...] = gv
        m2 = ADAM_B1 * m_ref[...] + (1.0 - ADAM_B1) * gv
        v2 = ADAM_B2 * v_ref[...] + (1.0 - ADAM_B2) * (gv * gv)
        m_hat = m2 / (1.0 - ADAM_B1 ** ADAM_STEP)
        v_hat = v2 / (1.0 - ADAM_B2 ** ADAM_STEP)
        d_ref[...] = -ADAM_LR * (m_hat / (jnp.sqrt(v_hat) + ADAM_EPS) + ADAM_WD * w_ref[...])
        m2_ref[...] = m2
        v2_ref[...] = v2

    spec = pl.BlockSpec((tr, C), lambda i: (i, 0))
    n_out = 4 if emit_g else 3
    return pl.pallas_call(
        body, grid=(R // tr,), in_specs=[spec] * 4, out_specs=[spec] * n_out,
        out_shape=[SDS((R, C), f32)] * n_out, name=name, compiler_params=_cp("parallel"))(w, g, m, v)


ANY = pl.BlockSpec(memory_space=pl.ANY)


def _place():
    x, y, c = lax.axis_index("x"), lax.axis_index("y"), lax.axis_index("c")
    chips = [(1 - x, y), (x, 1 - y), (1 - x, 1 - y)]
    return x, y, c, chips


def _cast_slots(w, kidx):
    L, a, b = w.shape
    ta = a // 2

    def body(k_ref, *refs):
        for l in range(L):
            refs[L + l][0] = refs[l][0].astype(bf16)

    return pl.pallas_call(
        body,
        grid_spec=pltpu.PrefetchScalarGridSpec(
            num_scalar_prefetch=1, grid=(a // ta,),
            in_specs=[pl.BlockSpec((1, ta, b), functools.partial(lambda i, k_ref, l: (l, i, 0), l=l)) for l in range(L)],
            out_specs=[pl.BlockSpec((1, ta, b), lambda i, k_ref: (k_ref[0], i, 0)) for _ in range(L)]),
        out_shape=[SDS((N_CHIPS, a, b), bf16)] * L,
        name="cast_slots", compiler_params=_cp("parallel"))(kidx, *([w] * L))


class _GatherComm:
    def __init__(self, bufs, part=0, parts=1):
        self.inputs = list(bufs)
        self.out_shape = [SDS(b.shape, b.dtype) for b in bufs]
        self.aliased = True
        self.scratch = [pltpu.SemaphoreType.DMA((3 * len(bufs),))] * 4
        self.part, self.parts = part, parts

    def _copies(self, i_refs, o_refs, sems):
        send1, recv1, send2, recv2 = sems
        x, y, c, chips = _place()
        k = 2 * x + y
        first, got1, second, got2 = [], [], [], []
        for i, buf in enumerate(self.inputs):
            h = buf.shape[1] // 2
            n = h // self.parts
            mine, theirs = pl.ds(c * h + self.part * n, n), pl.ds((1 - c) * h + self.part * n, n)
            for j, (cx, cy) in enumerate(chips):
                s = 3 * i + j
                first.append(pltpu.make_async_remote_copy(
                    src_ref=i_refs[i].at[k, mine], dst_ref=o_refs[i].at[k, mine], send_sem=send1.at[s],
                    recv_sem=recv1.at[s], device_id=(cx, cy, c), device_id_type=MESH))
                a = o_refs[i].at[2 * cx + cy, mine]
                got1.append(pltpu.make_async_remote_copy(
                    src_ref=a, dst_ref=a, send_sem=send1.at[s], recv_sem=recv1.at[s],
                    device_id=(cx, cy, c), device_id_type=MESH))
                second.append(pltpu.make_async_remote_copy(
                    src_ref=a, dst_ref=a, send_sem=send2.at[s], recv_sem=recv2.at[s],
                    device_id=(x, y, 1 - c), device_id_type=MESH))
                b = o_refs[i].at[2 * cx + cy, theirs]
                got2.append(pltpu.make_async_remote_copy(
                    src_ref=b, dst_ref=b, send_sem=send2.at[s], recv_sem=recv2.at[s],
                    device_id=(x, y, 1 - c), device_id_type=MESH))
        return first, got1, second, got2

    def start(self, i_refs, o_refs, sems):
        for cp in self._copies(i_refs, o_refs, sems)[0]:
            cp.start()

    def finish(self, i_refs, o_refs, sems):
        first, got1, second, got2 = self._copies(i_refs, o_refs, sems)
        for g, cp in zip(got1, second):
            g.wait_recv()
            cp.start()
        for g in got2:
            g.wait_recv()
        for cp in first + second:
            cp.wait_send()


class _MultiComm:
    def __init__(self, comms):
        self.comms = comms
        self.inputs = [a for c in comms for a in c.inputs]
        self.out_shape = [s for c in comms for s in c.out_shape]
        self.aliased = comms[0].aliased
        assert all(c.aliased == self.aliased for c in comms)
        self.scratch = [s for c in comms for s in c.scratch]

    def _split(self, i_refs, o_refs, sems):
        i = o = s = 0
        for c in self.comms:
            ni, no, ns = len(c.inputs), len(c.out_shape), len(c.scratch)
            yield c, i_refs[i:i + ni], o_refs[o:o + no], sems[s:s + ns]
            i, o, s = i + ni, o + no, s + ns

    def start(self, i_refs, o_refs, sems):
        for c, i, o, s in self._split(i_refs, o_refs, sems):
            c.start(i, o, s)

    def finish(self, i_refs, o_refs, sems):
        for c, i, o, s in self._split(i_refs, o_refs, sems):
            c.finish(i, o, s)


class _PairExchangeComm:
    def __init__(self, gs):
        self.inputs = list(gs)
        self.out_shape = [SDS((g.shape[0], g.shape[1] // 2, g.shape[2]), g.dtype) for g in gs]
        self.aliased = False
        self.scratch = [pltpu.SemaphoreType.DMA((len(gs),))] * 2

    def _copies(self, i_refs, o_refs, sems):
        send, recv = sems
        x, y, c, _ = _place()
        cps = []
        for i, g in enumerate(self.inputs):
            h = g.shape[1] // 2
            cps.append(pltpu.make_async_remote_copy(
                src_ref=i_refs[i].at[:, pl.ds((1 - c) * h, h)], dst_ref=o_refs[i], send_sem=send.at[i], recv_sem=recv.at[i],
                device_id=(x, y, 1 - c), device_id_type=MESH))
        return cps

    def start(self, i_refs, o_refs, sems):
        for cp in self._copies(i_refs, o_refs, sems):
            cp.start()

    def finish(self, i_refs, o_refs, sems):
        for cp in self._copies(i_refs, o_refs, sems):
            cp.wait()


class _ChipExchangeComm:
    def __init__(self, qs):
        self.inputs = list(qs)
        self.out_shape = [SDS(q.shape, q.dtype) for q in qs]
        self.aliased = False
        self.scratch = [pltpu.SemaphoreType.DMA((3 * len(qs),))] * 2

    def _copies(self, i_refs, o_refs, sems):
        send, recv = sems
        x, y, c, chips = _place()
        k = 2 * x + y
        cps, got = [], []
        for i in range(len(self.inputs)):
            for j, (cx, cy) in enumerate(chips):
                s = 3 * i + j
                cps.append(pltpu.make_async_remote_copy(
                    src_ref=i_refs[i].at[2 * cx + cy], dst_ref=o_refs[i].at[k], send_sem=send.at[s],
                    recv_sem=recv.at[s], device_id=(cx, cy, c), device_id_type=MESH))
                a = o_refs[i].at[2 * cx + cy]
                got.append(pltpu.make_async_remote_copy(
                    src_ref=a, dst_ref=a, send_sem=send.at[s], recv_sem=recv.at[s],
                    device_id=(cx, cy, c), device_id_type=MESH))
        return cps, got

    def start(self, i_refs, o_refs, sems):
        for cp in self._copies(i_refs, o_refs, sems)[0]:
            cp.start()

    def finish(self, i_refs, o_refs, sems):
        cps, got = self._copies(i_refs, o_refs, sems)
        for g in got:
            g.wait_recv()
        for cp in cps:
            cp.wait_send()


def _comm_only(comm, name):
    n = len(comm.inputs)

    def body(*refs):
        i_refs, o_refs, sems = refs[:n], refs[n:n + len(comm.out_shape)], refs[n + len(comm.out_shape):]
        comm.start(i_refs, o_refs, sems)
        comm.finish(i_refs, o_refs, sems)

    return pl.pallas_call(
        body, out_shape=comm.out_shape, in_specs=[ANY] * n, out_specs=[ANY] * len(comm.out_shape),
        input_output_aliases={i: i for i in range(n)} if comm.aliased else {},
        scratch_shapes=comm.scratch, name=name,
        compiler_params=pltpu.CompilerParams(has_side_effects=True))(*comm.inputs)


def _host(body, n_in, n_out, n_scratch, comm, first, last):
    if comm is None:
        return body
    ci, co = len(comm.inputs), len(comm.out_shape)

    def wrapped(*refs):
        o = 0
        parts = []
        for n in (n_in, ci, n_out, co, n_scratch):
            parts.append(refs[o:o + n])
            o += n
        hin, cin, hout, cout, hs = parts
        sems = refs[o:]

        @pl.when(first())
        def _():
            comm.start(cin, cout, sems)
        body(*hin, *hout, *hs)

        @pl.when(last())
        def _():
            comm.finish(cin, cout, sems)
    return wrapped


def _host_specs(comm, n_in, n_out):
    if comm is None:
        return [], [], [], [], {}, []
    ci, co = len(comm.inputs), len(comm.out_shape)
    aliases = {n_in + i: n_out + i for i in range(ci)} if comm.aliased else {}
    return comm.inputs, [ANY] * ci, [ANY] * co, comm.out_shape, aliases, comm.scratch


def _pair_add(g, r, cidx, name):
    ns, a, b = g.shape
    h = a // 2
    th = h if h * b * 4 <= 4 * 1024 * 1024 else h // 2

    def body(c_ref, g_ref, r_ref, qb_ref):
        qb_ref[...] = (g_ref[...] + r_ref[...]).astype(bf16)

    nb = h // th
    spec = pl.BlockSpec((1, th, b), lambda s, i, c_ref: (s, i, 0))
    return pl.pallas_call(
        body,
        grid_spec=pltpu.PrefetchScalarGridSpec(
            num_scalar_prefetch=1, grid=(ns, nb),
            in_specs=[pl.BlockSpec((1, th, b), lambda s, i, c_ref: (s, c_ref[0] * nb + i, 0)), spec],
            out_specs=spec),
        out_shape=SDS((ns, h, b), bf16),
        name=name, compiler_params=_cp("parallel", "parallel"))(cidx, g, r)


def _chip_add(g, r1, r2, idx, prev, L, name):
    ns, h, b = r2.shape
    th = h if h * b * 4 <= 4 * 1024 * 1024 else h // 2
    nb = h // th

    def body(s_ref, g_ref, r1_ref, a_ref, b_ref, c_ref, *rest):
        o_ref = rest[-1]
        o_ref[0] = (g_ref[0] + r1_ref[0]) + a_ref[0].astype(f32) + b_ref[0].astype(f32) + c_ref[0].astype(f32)

    other = lambda d: pl.BlockSpec((1, th, b), lambda i, s_ref: ((s_ref[0] + d) % ns, i, 0))
    in_specs = [pl.BlockSpec((1, th, b), lambda i, s_ref: (s_ref[0], s_ref[1] * nb + i, 0)),
                pl.BlockSpec((1, th, b), lambda i, s_ref: (s_ref[0], i, 0)), other(1), other(2), other(3)]
    args = [idx, g, r1, r2, r2, r2]
    aliases = {}
    if prev is not None:
        in_specs.append(ANY)
        args.append(prev)
        aliases = {6: 0}
    return pl.pallas_call(
        body,
        grid_spec=pltpu.PrefetchScalarGridSpec(
            num_scalar_prefetch=1, grid=(nb,), in_specs=in_specs,
            out_specs=pl.BlockSpec((1, th, b), lambda i, s_ref: (s_ref[2], s_ref[1] * nb + i, 0))),
        out_shape=SDS((L, 2 * h, b), f32), input_output_aliases=aliases,
        name=name, compiler_params=_cp("arbitrary"))(*args)


def _pair_share(gs, hs):
    n = len(gs)
    L = gs[0].shape[0]

    def body(*refs):
        i_refs, o_refs = refs[:n], refs[n:2 * n]
        send, recv = refs[2 * n:]
        x, y, c, _ = _place()
        cps = []
        for i in range(n):
            for l in range(L):
                mine = pl.ds(c * hs[i], hs[i])
                cp = pltpu.make_async_remote_copy(
                    src_ref=i_refs[i].at[l, mine], dst_ref=o_refs[i].at[l, mine], send_sem=send.at[i * L + l],
                    recv_sem=recv.at[i * L + l], device_id=(x, y, 1 - c), device_id_type=MESH)
                cp.start()
                cps.append(cp)
        for i in range(n):
            for l in range(L):
                got = o_refs[i].at[l, pl.ds((1 - c) * hs[i], hs[i])]
                pltpu.make_async_remote_copy(
                    src_ref=got, dst_ref=got, send_sem=send.at[i * L + l], recv_sem=recv.at[i * L + l],
                    device_id=(x, y, 1 - c), device_id_type=MESH).wait_recv()
        for cp in cps:
            cp.wait_send()

    return pl.pallas_call(
        body, out_shape=[SDS(g.shape, g.dtype) for g in gs], in_specs=[ANY] * n, out_specs=[ANY] * n,
        input_output_aliases={i: i for i in range(n)},
        scratch_shapes=[pltpu.SemaphoreType.DMA((n * L,))] * 2,
        name="grad_pair_share", compiler_params=pltpu.CompilerParams(has_side_effects=True))(*gs)


class _SwapComm:
    def __init__(self, arrays):
        self.inputs = list(arrays)
        self.out_shape = [SDS(a.shape, a.dtype) for a in arrays]
        self.aliased = False
        self.scratch = [pltpu.SemaphoreType.DMA((len(arrays),))] * 2

    def _copies(self, i_refs, o_refs, sems):
        send, recv = sems
        x, y, c, _ = _place()
        return [pltpu.make_async_remote_copy(
            src_ref=i_refs[i], dst_ref=o_refs[i], send_sem=send.at[i], recv_sem=recv.at[i],
            device_id=(x, y, 1 - c), device_id_type=MESH) for i in range(len(self.inputs))]

    def start(self, i_refs, o_refs, sems):
        for cp in self._copies(i_refs, o_refs, sems):
            cp.start()

    def finish(self, i_refs, o_refs, sems):
        for cp in self._copies(i_refs, o_refs, sems):
            cp.wait()


class _SlotShareComm:
    def __init__(self, bufs):
        self.inputs = list(bufs)
        self.out_shape = [SDS(b.shape, b.dtype) for b in bufs]
        self.aliased = True
        self.scratch = [pltpu.SemaphoreType.DMA((3 * len(bufs),))] * 2

    def _copies(self, i_refs, o_refs, sems):
        send, recv = sems
        x, y, c, chips = _place()
        k = 2 * x + y
        cps, got = [], []
        for i in range(len(self.inputs)):
            for j, (cx, cy) in enumerate(chips):
                s = 3 * i + j
                cps.append(pltpu.make_async_remote_copy(
                    src_ref=i_refs[i].at[k], dst_ref=o_refs[i].at[k], send_sem=send.at[s], recv_sem=recv.at[s],
                    device_id=(cx, cy, c), device_id_type=MESH))
                a = o_refs[i].at[2 * cx + cy]
                got.append(pltpu.make_async_remote_copy(
                    src_ref=a, dst_ref=a, send_sem=send.at[s], recv_sem=recv.at[s],
                    device_id=(cx, cy, c), device_id_type=MESH))
        return cps, got

    def start(self, i_refs, o_refs, sems):
        for cp in self._copies(i_refs, o_refs, sems)[0]:
            cp.start()

    def finish(self, i_refs, o_refs, sems):
        cps, got = self._copies(i_refs, o_refs, sems)
        for g in got:
            g.wait_recv()
        for cp in cps:
            cp.wait_send()


def _pair_sum_slot(mine, theirs, kidx, dtype):
    R, C = mine.shape

    def body(k_ref, a_ref, b_ref, o_ref):
        o_ref[0] = (a_ref[...] + b_ref[...]).astype(dtype)

    spec = pl.BlockSpec((R, C), lambda i, k_ref: (0, 0))
    return pl.pallas_call(
        body,
        grid_spec=pltpu.PrefetchScalarGridSpec(
            num_scalar_prefetch=1, grid=(1,), in_specs=[spec, spec],
            out_specs=pl.BlockSpec((1, R, C), lambda i, k_ref: (k_ref[0], 0, 0))),
        out_shape=SDS((N_CHIPS, R, C), dtype), name="small_pair_sum", compiler_params=_cp("arbitrary"))(kidx, mine, theirs)


def _small_sum(g):
    n, R, C = g.shape

    def body(g_ref, o_ref):
        acc = g_ref[0].astype(f32)
        for j in range(1, n):
            acc = acc + g_ref[j].astype(f32)
        o_ref[...] = acc
    return pl.pallas_call(body, out_shape=SDS((R, C), f32), name="small_sum", compiler_params=_cp())(g)


PACK_COLS = 1024


def _rows_of(shape):
    n = int(np.prod(shape)) if len(shape) else 1
    return -(-n // (8 * PACK_COLS)) * 8


def _pack(parts):
    blocks = []
    for p in parts:
        flat = p.reshape(-1)
        r = _rows_of(p.shape)
        blocks.append(jnp.pad(flat, (0, r * PACK_COLS - flat.shape[0])).reshape(r, PACK_COLS))
    return jnp.concatenate(blocks, axis=0)


def _unpack(buf, shapes):
    out, off = [], 0
    for s in shapes:
        n = int(np.prod(s)) if len(s) else 1
        r = _rows_of(s)
        out.append(buf[off:off + r].reshape(-1)[:n].reshape(s))
        off += r
    return out


def _block_diag(w):
    g, a, _ = w.shape
    out = jnp.zeros((g * a, g * a), w.dtype)
    for i in range(g):
        out = lax.dynamic_update_slice(out, w[i], (i * a, i * a))
    return out


def kernel(x, w_in, w_out, sgu_w, sgu_b, pool_w, pool_scale, swa_sinks, rel_bias, mix_out_gain, norm_mix, norm_ffn, w_gate_up, w_down, norm_final, loss_target, m_w_in, m_w_out, m_sgu_w, m_sgu_b, m_pool_w, m_pool_scale, m_swa_sinks, m_rel_bias, m_mix_out_gain, m_norm_mix, m_norm_ffn, m_w_gate_up, m_w_down, m_norm_final, v_w_in, v_w_out, v_sgu_w, v_sgu_b, v_pool_w, v_pool_scale, v_swa_sinks, v_rel_bias, v_mix_out_gain, v_norm_mix, v_norm_ffn, v_w_gate_up, v_w_down, v_norm_final):
    B, S, D = x.shape
    T = B * S
    L = w_in.shape[0]
    tm = min(512, T)
    F = w_down.shape[1] * N_CHIPS
    xi, yi, ci = lax.axis_index("x"), lax.axis_index("y"), lax.axis_index("c")
    cidx = jnp.reshape(ci, (1,)).astype(jnp.int32)
    kidx = jnp.reshape(2 * xi + yi, (1,)).astype(jnp.int32)

    big = [w_in, w_out, w_gate_up, w_down]
    slots = [_cast_slots(w, kidx) for w in big]
    gather = lambda pi, l, part=0, parts=1: _GatherComm([slots[pi][l]], part, parts)
    Win, Wo, Wgu, Wd = ([None] * L for _ in range(4))
    Win[0], = _comm_only(gather(0, 0), "gather_weights")

    bucket = jnp.asarray(_t5_bucket_table().reshape(1, -1))
    bias_tab = _bias_expand(rel_bias.T, bucket).reshape(4, BLK, 2 * BLK)

    row = lambda v: v.reshape(1, -1)
    xc = x.reshape(T, D)
    tgt = loss_target.reshape(T, D)
    saved = []
    for l in range(L):
        bexp = jnp.repeat(sgu_b[l].T, HD, axis=1)
        wbd = _block_diag(pool_w[l])
        sk = jnp.broadcast_to(swa_sinks[l][:, None, None], (4, 1, BLK))
        h1, proj, wo = _norm_mm(xc, row(norm_mix[l]), Win[l], tm, gather(1, l))
        ya = _sgu_fwd(proj, sgu_w[l], bexp, B, S)
        yb = _pool_fwd(proj, wbd, row(pool_scale[l]), B, S)
        yc, wd = _swa_fwd(proj, sk, bias_tab, B, S, gather(3, l, 0, 2))
        if l == 0:
            yd, lt, wd, Wgu[0] = _sb_fwd(proj, B, S, _MultiComm([_GatherComm([wd], 1, 2), gather(2, 0)]))
        else:
            yd, lt, wd = _sb_fwd(proj, B, S, _GatherComm([wd], 1, 2))
        Wo[l], Wd[l] = wo.reshape(D, D), wd.reshape(F, D)
        ys = (ya, yb, yc, yd)
        nxt = l + 1 < L
        ycn, x1, h2, gu, act, *got = _mix_out_swiglu(ys, row(mix_out_gain[l]), Wo[l], xc, row(norm_ffn[l]), Wgu[l], tm,
                                                     gather(2, l + 1) if nxt else None)
        x2, *got2 = _mm_res(act, Wd[l], x1, tm, gather(0, l + 1) if nxt else None)
        if nxt:
            Wgu[l + 1], Win[l + 1] = got[0], got2[0]
        saved.append((xc, h1, proj, bexp, wbd, sk, ys, lt, ycn, x1, h2, gu, act))
        xc = x2

    dx, g_final, loss_v = _final_loss(xc, row(norm_final), tgt, tm)

    tk = min(T, 2048)
    gW = [[None] * L for _ in range(4)]
    g_sgu_w, g_sgu_b, g_pool_w, g_pool_scale, g_sinks, g_bias = ([None] * L for _ in range(6))
    g_out_gain, g_mix, g_ffn = ([None] * L for _ in range(3))
    reduced = [None] * 4
    sums = {}

    def pair_comm(keys):
        return _PairExchangeComm([gW[pi][l] for pi, l in keys])

    def after_pair(keys, r1):
        for (pi, l), r in zip(keys, r1):
            sums[pi, l] = (r, _pair_add(gW[pi][l], r, cidx, "grad_pair_add"))

    def chip_comm(keys):
        return _ChipExchangeComm([sums[k][1] for k in keys])

    def after_chip(keys, r2):
        for (pi, l), r in zip(keys, r2):
            idx = jnp.stack([2 * xi + yi, ci, jnp.int32(l)]).astype(jnp.int32)
            reduced[pi] = _chip_add(gW[pi][l], sums.pop((pi, l))[0], r, idx, reduced[pi], L, "grad_chip_add")

    for l in reversed(range(L)):
        x0, h1, proj, bexp, wbd, sk, ys, lt, ycn, x1, h2, gu, act = saved[l]
        keys = [(0, l + 1), (1, l + 1)]
        comm = chip_comm(keys) if l + 1 < L else None
        dgu, dx1, g_ffn[l], *r2 = _dact_dx(dx, Wd[l], gu, Wgu[l], x1, row(norm_ffn[l]), tm // 2, comm)
        if comm is not None:
            after_chip(keys, r2)
        gW[3][l] = _dw(act, dx, lambda t, s: (t, 0), D, 1, F // 2, tk // 2, "dw_down").reshape(N_CHIPS, F // N_CHIPS, D)
        gW[2][l] = _dw(h2, dgu, lambda t, s: (s // 2, t, s % 2), F // 2, N_CHIPS, D, tk, "dw_gate_up")
        gW[1][l] = _dw(ycn, dx1, lambda t, s: (t, 0), D, 1, D, tk, "dw_out").reshape(N_CHIPS, D // N_CHIPS, D)
        dya, dyb, dyc, dyd, g_out_gain[l] = _dycat(dx1, Wo[l], ys, row(mix_out_gain[l]), tm)
        keys = [(2, l), (3, l)]
        dpa, g_sgu_w[l], dbf, *r1 = _sgu_bwd(proj, sgu_w[l], bexp, dya, B, S, pair_comm(keys))
        after_pair(keys, r1)
        g_sgu_b[l] = dbf[:, ::HD].T
        dpb, dwbd, dsc = _pool_bwd(proj, wbd, row(pool_scale[l]), dyb, B, S)
        npg = len(POOL_WINDOWS)
        g_pool_w[l] = jnp.stack([dwbd[i * HD:(i + 1) * HD, i * HD:(i + 1) * HD] for i in range(npg)])
        g_pool_scale[l] = dsc[0]
        dcq, dckv, dsk, g_bias[l], *r2 = _swa_bwd(proj, sk, bias_tab, dyc, B, S, chip_comm([(3, l)]))
        after_chip([(3, l)], r2)
        g_sinks[l] = dsk[:, 0, 0] * float(BLK)
        ddq, ddk, ddv, *r2 = _sb_bwd(proj, lt, dyd, B, S, chip_comm([(2, l)]))
        after_chip([(2, l)], r2)
        dproj = [dpa, dpb, dcq, dckv, ddq, ddk, ddv]
        gW[0][l] = _dw_pieces(h1, dproj, w_in.shape[2], N_CHIPS, tk, "dw_in")
        keys = [(0, l), (1, l)]
        dx, g_mix[l], *r1 = _dx_norm_bwd(dproj, Win[l], x0, row(norm_mix[l]), dx1, tm, "dx_mix_exchange", pair_comm(keys))
        after_pair(keys, r1)
    grad_x = dx.reshape(B, S, D)

    after_chip(keys, _comm_only(chip_comm(keys), "grad_chip_exchange"))
    g_big = _pair_share(reduced, [g.shape[1] // 2 for g in reduced])

    g_rel_bias = _bias_reduce([g.reshape(4, -1) for g in g_bias], bucket).T
    small_g = [jnp.stack(g_sgu_w), jnp.stack(g_sgu_b), jnp.stack(g_pool_w), jnp.stack(g_pool_scale), jnp.stack(g_sinks),
               g_rel_bias, jnp.concatenate(g_out_gain), jnp.concatenate(g_mix), jnp.concatenate(g_ffn), g_final[0]]
    small_w = [sgu_w, sgu_b, pool_w, pool_scale, swa_sinks, rel_bias, mix_out_gain, norm_mix, norm_ffn, norm_final]
    small_m = [m_sgu_w, m_sgu_b, m_pool_w, m_pool_scale, m_swa_sinks, m_rel_bias, m_mix_out_gain, m_norm_mix, m_norm_ffn, m_norm_final]
    small_v = [v_sgu_w, v_sgu_b, v_pool_w, v_pool_scale, v_swa_sinks, v_rel_bias, v_mix_out_gain, v_norm_mix, v_norm_ffn, v_norm_final]
    shapes = [w.shape for w in small_w]
    bulk, fine = [small_g[0], small_g[2]], [small_g[i] for i in (1, 3, 4, 5, 6, 7, 8, 9)] + [loss_v[0, 0:1]]
    mine = [_pack(bulk), _pack(fine)]
    theirs = _comm_only(_SwapComm(mine), "small_pair_swap")
    slots_s = [_pair_sum_slot(a, b, kidx, dt) for a, b, dt in zip(mine, theirs, (bf16, f32))]
    shared = _comm_only(_SlotShareComm(slots_s), "small_chip_share")
    g_bulk = _unpack(_small_sum(shared[0]), [shapes[0], shapes[2]])
    *g_fine, loss = _unpack(_small_sum(shared[1]), [shapes[i] for i in (1, 3, 4, 5, 6, 7, 8, 9)] + [()])
    g_small = [g_bulk[0], g_fine[0], g_bulk[1]] + g_fine[1:]

    big_m = [m_w_in, m_w_out, m_w_gate_up, m_w_down]
    big_v = [v_w_in, v_w_out, v_w_gate_up, v_w_down]
    g_out, d_big, m_big, v_big = [], [], [], []
    for w, g, m, v in zip(big, g_big, big_m, big_v):
        two = lambda a: a.reshape(-1, a.shape[-1])
        rows = two(w).shape[0]
        cap = max(8, (1 << 20) // (4 * w.shape[-1]))
        tr = max(t for t in range(8, min(rows, cap) + 1, 8) if rows % t == 0)
        d2, m2, v2, g2 = _adamw(two(w), two(g), two(m), two(v), tr, "adamw_big", True)
        for lst, val in ((d_big, d2), (m_big, m2), (v_big, v2), (g_out, g2)):
            lst.append(val.reshape(w.shape))
    g_big = g_out

    g_small_packed = _pack(g_small)
    ds, ms, vs = _adamw(_pack(small_w), g_small_packed, _pack(small_m), _pack(small_v), g_small_packed.shape[0], "adamw_small")
    d_small, m_small, v_small = _unpack(ds, shapes), _unpack(ms, shapes), _unpack(vs, shapes)

    def order(bigs, smalls):
        return [bigs[0], bigs[1]] + list(smalls[0:9]) + [bigs[2], bigs[3], smalls[9]]

    return (loss, grad_x, *order(g_big, g_small), *order(d_big, d_small), *order(m_big, m_small), *order(v_big, v_small))
```

```python
import functools

import numpy as np
import jax
import jax.numpy as jnp
from jax import lax
from jax.experimental import pallas as pl
from jax.experimental.pallas import tpu as pltpu

f32 = jnp.float32
bf16 = jnp.bfloat16
_MXU = jnp.bfloat16

EPS = 1e-6
HD = 64
GW = 256
BLK = 128
SB_UNROLL = 2
SB_HEADS = 4
SB_CUT = -110.0
POOL_WINDOWS = (2, 4, 8, 16)
N_BUCKETS = 32
MAX_DISTANCE = 128
N_CHIPS = 4
VMEM_LIMIT = 48 * 1024 * 1024

ADAM_LR = 0.001
ADAM_B1 = 0.9
ADAM_B2 = 0.999
ADAM_EPS = 1e-08
ADAM_WD = 0.01
ADAM_STEP = 10

SDS = jax.ShapeDtypeStruct
MESH = pl.DeviceIdType.MESH
HIGHEST = lax.Precision.HIGHEST
RESIDENT = pl.Buffered(1)
NT = (((1,), (1,)), ((), ()))
TN = (((0,), (0,)), ((), ()))


def _cp(*sem):
    return pltpu.CompilerParams(dimension_semantics=sem if sem else None, vmem_limit_bytes=VMEM_LIMIT)


def _mx(v):
    return v.astype(_MXU)


def _iota(shape, dim):
    return lax.broadcasted_iota(jnp.int32, shape, dim)


def _split_dot(a, tri):
    hi = a.astype(bf16)
    lo = (a - hi.astype(f32)).astype(bf16)
    return jnp.dot(hi, tri, preferred_element_type=f32) + jnp.dot(lo, tri, preferred_element_type=f32)


def _rms(xv):
    return lax.rsqrt(jnp.mean(xv * xv, axis=-1, keepdims=True) + EPS)


def _hosted_call(body, steps, in_specs, out_specs, out_shape, scratch, args, name, comm):
    n_in, n_out = len(in_specs), len(out_specs)
    c_args, c_in, c_out, c_shapes, aliases, c_scratch = _host_specs(comm, n_in, n_out)
    step = lambda v: (lambda: pl.program_id(0) == v)
    return pl.pallas_call(
        _host(body, n_in, n_out, len(scratch), comm, step(0), step(steps - 1)), grid=(steps,),
        in_specs=list(in_specs) + c_in, out_specs=list(out_specs) + c_out, out_shape=list(out_shape) + c_shapes,
        input_output_aliases=aliases, scratch_shapes=list(scratch) + c_scratch,
        name=name if comm is None else name + "_comm", compiler_params=_cp("arbitrary"))(*args, *c_args)


def _norm_mm(x, gain, w, tm, comm=None):
    T, D = x.shape
    NS, _, ns = w.shape

    def body(x_ref, g_ref, w_ref, h_ref, o_ref):
        xv = x_ref[...]
        h = (xv * _rms(xv) * g_ref[...]).astype(bf16)
        h_ref[...] = h
        for s in range(NS):
            o_ref[:, s * ns:(s + 1) * ns] = jnp.dot(_mx(h), w_ref[s], preferred_element_type=f32).astype(bf16)

    return _hosted_call(
        body, T // tm,
        [pl.BlockSpec((tm, D), lambda i: (i, 0)),
         pl.BlockSpec((1, D), lambda i: (0, 0)),
         pl.BlockSpec((NS, D, ns), lambda i: (0, 0, 0), pipeline_mode=RESIDENT)],
        [pl.BlockSpec((tm, D), lambda i: (i, 0)), pl.BlockSpec((tm, NS * ns), lambda i: (i, 0))],
        [SDS((T, D), bf16), SDS((T, NS * ns), bf16)], [], (x, gain, w), "norm_mm_in", comm)


def _mix_out_swiglu(ys, gain_mix, wo, x, gain_ffn, w, tm, comm=None):
    T, D = x.shape
    NS, _, ns = w.shape
    half = NS // 2

    def body(ya, yb, yc, yd, gm_ref, wo_ref, x_ref, gf_ref, w_ref, yn_ref, x1_ref, h_ref, gu_ref, a_ref):
        parts = []
        for m, r in enumerate((ya, yb, yc, yd)):
            y = r[...].astype(f32)
            parts.append((y * _rms(y) * gm_ref[:, m * GW:(m + 1) * GW]).astype(bf16))
        yn = jnp.concatenate(parts, axis=1)
        yn_ref[...] = yn
        x1 = x_ref[...] + jnp.dot(_mx(yn), wo_ref[...], preferred_element_type=f32)
        x1_ref[...] = x1
        hb = (x1 * _rms(x1) * gf_ref[...]).astype(bf16)
        h_ref[...] = hb
        h = _mx(hb)
        for s in range(half):
            cols = slice(s * ns, (s + 1) * ns)
            g = jnp.dot(h, w_ref[s], preferred_element_type=f32)
            u = jnp.dot(h, w_ref[s + half], preferred_element_type=f32)
            gu_ref[0, :, cols] = g.astype(bf16)
            gu_ref[1, :, cols] = u.astype(bf16)
            a_ref[:, cols] = (jax.nn.silu(g) * u).astype(bf16)

    yspec = pl.BlockSpec((tm, GW), lambda i: (i, 0))
    row, tile = pl.BlockSpec((1, D), lambda i: (0, 0)), pl.BlockSpec((tm, D), lambda i: (i, 0))
    return _hosted_call(
        body, T // tm,
        [yspec, yspec, yspec, yspec, row,
         pl.BlockSpec((D, D), lambda i: (0, 0), pipeline_mode=RESIDENT), tile, row,
         pl.BlockSpec((NS, D, ns), lambda i: (0, 0, 0), pipeline_mode=RESIDENT)],
        [tile, tile, tile, pl.BlockSpec((2, tm, half * ns), lambda i: (0, i, 0)),
         pl.BlockSpec((tm, half * ns), lambda i: (i, 0))],
        [SDS((T, D), bf16), SDS((T, D), f32), SDS((T, D), bf16), SDS((2, T, half * ns), bf16), SDS((T, half * ns), bf16)],
        [], (*ys, gain_mix, wo, x, gain_ffn, w), "mix_out_swiglu", comm)


def _mm_res(a, w, x, tm, comm=None):
    T, D = x.shape
    K = a.shape[1]

    def body(a_ref, w_ref, x_ref, o_ref):
        o_ref[...] = x_ref[...] + jnp.dot(_mx(a_ref[...]), w_ref[...], preferred_element_type=f32)

    return _hosted_call(
        body, T // tm,
        [pl.BlockSpec((tm, K), lambda i: (i, 0)),
         pl.BlockSpec((K, D), lambda i: (0, 0), pipeline_mode=RESIDENT),
         pl.BlockSpec((tm, D), lambda i: (i, 0))],
        [pl.BlockSpec((tm, D), lambda i: (i, 0))], [SDS((T, D), f32)], [], (a, w, x), "mm_res_down", comm)


def _final_loss(x, gain, tgt, tm):
    T, D = x.shape

    def body(x_ref, g_ref, t_ref, dx_ref, dg_ref, l_ref):
        @pl.when(pl.program_id(0) == 0)
        def _():
            dg_ref[...] = jnp.zeros_like(dg_ref)
            l_ref[...] = jnp.zeros_like(l_ref)
        xv = x_ref[...]
        g = g_ref[...]
        r = _rms(xv)
        xh = xv * r
        err = xh * g - t_ref[...]
        l_ref[...] += 0.5 * jnp.sum(jnp.mean(err * err, axis=-1, keepdims=True), axis=0, keepdims=True)
        dy = err * (1.0 / D)
        dg_ref[...] += jnp.sum(dy * xh, axis=0, keepdims=True)
        dxh = dy * g
        dx_ref[...] = r * (dxh - xh * jnp.mean(dxh * xh, axis=-1, keepdims=True))

    return pl.pallas_call(
        body, grid=(T // tm,),
        in_specs=[pl.BlockSpec((tm, D), lambda i: (i, 0)),
                  pl.BlockSpec((1, D), lambda i: (0, 0)),
                  pl.BlockSpec((tm, D), lambda i: (i, 0))],
        out_specs=[pl.BlockSpec((tm, D), lambda i: (i, 0)),
                   pl.BlockSpec((1, D), lambda i: (0, 0)),
                   pl.BlockSpec((1, BLK), lambda i: (0, 0))],
        out_shape=[SDS((T, D), f32), SDS((1, D), f32), SDS((1, BLK), f32)],
        name="final_loss", compiler_params=_cp("arbitrary"))(x, gain, tgt)


def _dact_dx(dx, wd, gu, w, x, gain, tm, comm=None):
    T, D = dx.shape
    F = wd.shape[0]
    NS, _, ns = w.shape
    half = NS // 2

    def body(dx_ref, wd_ref, gu_ref, w_ref, x_ref, g_ref, dgu_ref, dx1_ref, dg_ref):
        @pl.when(pl.program_id(0) == 0)
        def _():
            dg_ref[...] = jnp.zeros_like(dg_ref)
        dxv = dx_ref[...]
        dxb = _mx(dxv)
        dh = None
        for s in range(half):
            cols = slice(s * ns, (s + 1) * ns)
            da = lax.dot_general(dxb, wd_ref[s * ns:(s + 1) * ns, :], NT, preferred_element_type=f32)
            g = gu_ref[0, :, cols].astype(f32)
            u = gu_ref[1, :, cols].astype(f32)
            sg = jax.nn.sigmoid(g)
            dgs = (da * u * (sg * (1.0 + g * (1.0 - sg)))).astype(bf16)
            dus = (da * (g * sg)).astype(bf16)
            dgu_ref[0, :, cols] = dgs
            dgu_ref[1, :, cols] = dus
            part = (lax.dot_general(_mx(dgs), w_ref[s], NT, preferred_element_type=f32)
                    + lax.dot_general(_mx(dus), w_ref[s + half], NT, preferred_element_type=f32))
            dh = part if dh is None else dh + part
        xv = x_ref[...]
        r = _rms(xv)
        xh = xv * r
        dg_ref[...] += jnp.sum(dh * xh, axis=0, keepdims=True)
        dxh = dh * g_ref[...]
        dx1_ref[...] = dxv + r * (dxh - xh * jnp.mean(dxh * xh, axis=-1, keepdims=True))

    return _hosted_call(
        body, T // tm,
        [pl.BlockSpec((tm, D), lambda i: (i, 0)),
         pl.BlockSpec((F, D), lambda i: (0, 0), pipeline_mode=RESIDENT),
         pl.BlockSpec((2, tm, F), lambda i: (0, i, 0)),
         pl.BlockSpec((NS, D, ns), lambda i: (0, 0, 0), pipeline_mode=RESIDENT),
         pl.BlockSpec((tm, D), lambda i: (i, 0)),
         pl.BlockSpec((1, D), lambda i: (0, 0))],
        [pl.BlockSpec((2, tm, F), lambda i: (0, i, 0)), pl.BlockSpec((tm, D), lambda i: (i, 0)),
         pl.BlockSpec((1, D), lambda i: (0, 0))],
        [SDS((2, T, F), bf16), SDS((T, D), f32), SDS((1, D), f32)], [], (dx, wd, gu, w, x, gain), "dact_dx", comm)


def _dw(a, b, b_map, ns, NS, tka, tk, name):
    T, Ka = a.shape
    b_block = (tk, ns) if b.ndim == 2 else (1, tk, ns)

    def body(a_ref, b_ref, o_ref):
        bv = b_ref[...] if b.ndim == 2 else b_ref[0]
        part = lax.dot_general(_mx(a_ref[...]), _mx(bv), TN, preferred_element_type=f32)

        @pl.when(pl.program_id(2) == 0)
        def _():
            o_ref[0] = part

        @pl.when(pl.program_id(2) > 0)
        def _():
            o_ref[0] += part

    return pl.pallas_call(
        body, grid=(NS, Ka // tka, T // tk),
        in_specs=[pl.BlockSpec((tk, tka), lambda s, k, t: (t, k)),
                  pl.BlockSpec(b_block, lambda s, k, t: b_map(t, s))],
        out_specs=pl.BlockSpec((1, tka, ns), lambda s, k, t: (s, k, 0)),
        out_shape=SDS((NS, Ka, ns), f32),
        name=name, compiler_params=_cp("parallel", "parallel", "arbitrary"))(a, b)


def _dw_pieces(a, pieces, ns, NS, tk, name):
    T, Ka = a.shape
    n = len(pieces)

    def body(*refs):
        a_ref, b_refs, o_ref = refs[0], refs[1:1 + n], refs[1 + n]
        full = jnp.concatenate([r[...] for r in b_refs], axis=1)
        av = _mx(a_ref[...])
        parts = [lax.dot_general(av, _mx(full[:, s * ns:(s + 1) * ns]), TN, preferred_element_type=f32) for s in range(NS)]

        @pl.when(pl.program_id(0) == 0)
        def _():
            for s in range(NS):
                o_ref[s] = parts[s]

        @pl.when(pl.program_id(0) > 0)
        def _():
            for s in range(NS):
                o_ref[s] += parts[s]

    return pl.pallas_call(
        body, grid=(T // tk,),
        in_specs=[pl.BlockSpec((tk, Ka), lambda t: (t, 0))] + [pl.BlockSpec((tk, p.shape[1]), lambda t: (t, 0)) for p in pieces],
        out_specs=pl.BlockSpec((NS, Ka, ns), lambda t: (0, 0, 0)),
        out_shape=SDS((NS, Ka, ns), f32),
        name=name, compiler_params=_cp("arbitrary"))(a, *pieces)


def _dx_norm_bwd(pieces, w, x, gain, dxin, tm, name, comm=None):
    T, D = x.shape
    NS, _, ns = w.shape
    n_dy = len(pieces)

    def body(*refs):
        dy_refs = refs[:n_dy]
        w_ref, x_ref, g_ref, dxin_ref, dx_ref, dg_ref = refs[n_dy:]

        @pl.when(pl.program_id(0) == 0)
        def _():
            dg_ref[...] = jnp.zeros_like(dg_ref)
        full = jnp.concatenate([r[...] for r in dy_refs], axis=1)
        dh = None
        for s in range(NS):
            part = lax.dot_general(_mx(full[:, s * ns:(s + 1) * ns]), w_ref[s], NT, preferred_element_type=f32)
            dh = part if dh is None else dh + part
        xv = x_ref[...]
        r = _rms(xv)
        xh = xv * r
        dg_ref[...] += jnp.sum(dh * xh, axis=0, keepdims=True)
        dxh = dh * g_ref[...]
        dx_ref[...] = dxin_ref[...] + r * (dxh - xh * jnp.mean(dxh * xh, axis=-1, keepdims=True))

    dy_specs = [pl.BlockSpec((tm, p.shape[1]), lambda i: (i, 0)) for p in pieces]
    return _hosted_call(
        body, T // tm,
        dy_specs + [pl.BlockSpec((NS, D, ns), lambda i: (0, 0, 0), pipeline_mode=RESIDENT),
                    pl.BlockSpec((tm, D), lambda i: (i, 0)),
                    pl.BlockSpec((1, D), lambda i: (0, 0)),
                    pl.BlockSpec((tm, D), lambda i: (i, 0))],
        [pl.BlockSpec((tm, D), lambda i: (i, 0)), pl.BlockSpec((1, D), lambda i: (0, 0))],
        [SDS((T, D), f32), SDS((1, D), f32)], [], (*pieces, w, x, gain, dxin), name, comm)


def _dycat(dx, w, ys, gain, tm, comm=None):
    T, D = dx.shape

    def body(dx_ref, w_ref, ya, yb, yc, yd, g_ref, da, db, dc, dd, dg_ref):
        @pl.when(pl.program_id(0) == 0)
        def _():
            dg_ref[...] = jnp.zeros_like(dg_ref)
        dyn = lax.dot_general(_mx(dx_ref[...]), w_ref[...], NT, preferred_element_type=f32)
        for m, (r, o) in enumerate(((ya, da), (yb, db), (yc, dc), (yd, dd))):
            cols = slice(m * GW, (m + 1) * GW)
            y = r[...].astype(f32)
            rs = _rms(y)
            yh = y * rs
            d = dyn[:, cols]
            dg_ref[:, cols] += jnp.sum(d * yh, axis=0, keepdims=True)
            dyh = d * g_ref[:, cols]
            o[...] = (rs * (dyh - yh * jnp.mean(dyh * yh, axis=-1, keepdims=True))).astype(bf16)

    yspec = pl.BlockSpec((tm, GW), lambda i: (i, 0))
    return _hosted_call(
        body, T // tm,
        [pl.BlockSpec((tm, D), lambda i: (i, 0)),
         pl.BlockSpec((D, D), lambda i: (0, 0), pipeline_mode=RESIDENT),
         yspec, yspec, yspec, yspec,
         pl.BlockSpec((1, D), lambda i: (0, 0))],
        [yspec, yspec, yspec, yspec, pl.BlockSpec((1, D), lambda i: (0, 0))],
        [SDS((T, GW), bf16)] * 4 + [SDS((1, D), f32)], [], (dx, w, *ys, gain), "dycat", comm)


def _sgu_consts():
    r, c = _iota((GW, GW), 0), _iota((GW, GW), 1)
    seg = (r // HD == c // HD).astype(f32)
    tr, ts = _iota((BLK, BLK), 0), _iota((BLK, BLK), 1)
    causal = ts <= tr
    lane_head = _iota((BLK, GW), 1) // HD
    return seg, causal, lane_head


def _split3_dot(a, ones):
    hi = a.astype(bf16)
    r1 = a - hi.astype(f32)
    mid = r1.astype(bf16)
    lo = (r1 - mid.astype(f32)).astype(bf16)
    dot = functools.partial(jnp.dot, preferred_element_type=f32)
    return dot(hi, ones) + dot(mid, ones) + dot(lo, ones)


def _sgu_chunks(aus, avs, w, bexp, consts):
    seg, causal, lane_head = consts
    segb = seg.astype(bf16)
    nh = GW // HD
    vs = [jax.nn.gelu(av) for av in avs]
    mus = [_split3_dot(v, segb) * (1.0 / HD) for v in vs]
    vcs = [v - mu for v, mu in zip(vs, mus)]
    vars_ = [_split3_dot(vc * vc, segb) * (1.0 / HD) for vc in vcs]
    vns = [_mx(vc * lax.rsqrt(var + EPS)) for vc, var in zip(vcs, vars_)]
    whs = [_mx(jnp.where(causal, w[h], 0.0)) for h in range(nh)]
    mixes = [[jnp.dot(whs[h], vn, preferred_element_type=f32) for h in range(nh)] for vn in vns]
    out = []
    for au, ms in zip(aus, mixes):
        mix = bexp
        for h in range(nh):
            mix = mix + jnp.where(lane_head == h, ms[h], 0.0)
        out.append(jax.nn.gelu(au) * mix)
    return out


def _sgu_group(S):
    nc = S // BLK
    return 4 if nc % 4 == 0 else (2 if nc % 2 == 0 else 1)


def _sgu_fwd(proj, w, bexp, B, S):
    G = _sgu_group(S)

    def body(au_ref, av_ref, w_ref, b_ref, y_ref):
        consts = _sgu_consts()
        wv, bv = w_ref[...], b_ref[...]

        def group(n, c):
            rows = [pl.ds(pl.multiple_of((n * G + j) * BLK, BLK), BLK) for j in range(G)]
            ys = _sgu_chunks([au_ref[r, :].astype(f32) for r in rows], [av_ref[r, :].astype(f32) for r in rows],
                             wv, bv, consts)
            for r, y in zip(rows, ys):
                y_ref[r, :] = y.astype(bf16)
            return c
        lax.fori_loop(0, S // BLK // G, group, 0)

    return pl.pallas_call(
        body, grid=(B,),
        in_specs=[pl.BlockSpec((S, GW), lambda b: (b, 0)),
                  pl.BlockSpec((S, GW), lambda b: (b, 1)),
                  pl.BlockSpec((GW // HD, BLK, BLK), lambda b: (0, 0, 0)),
                  pl.BlockSpec((BLK, GW), lambda b: (0, 0))],
        out_specs=pl.BlockSpec((S, GW), lambda b: (b, 0)),
        out_shape=SDS((B * S, GW), bf16),
        name="sgu_fwd", compiler_params=_cp("parallel"))(proj, proj, w, bexp)


def _sgu_bwd(proj, w, bexp, dy, B, S, comm=None):
    def body(au_ref, av_ref, w_ref, b_ref, dy_ref, dp_ref, dw_ref, db_ref):
        @pl.when(pl.program_id(0) == 0)
        def _():
            dw_ref[...] = jnp.zeros_like(dw_ref)
            db_ref[...] = jnp.zeros_like(db_ref)
        consts = _sgu_consts()
        wv, bv = w_ref[...], b_ref[...]
        fn = lambda aus, avs, ww, bb: _sgu_chunks(aus, avs, ww, bb, consts)
        G = _sgu_group(S)

        def group(n, carry):
            dw_acc, db_acc = carry
            rows = [pl.ds(pl.multiple_of((n * G + j) * BLK, BLK), BLK) for j in range(G)]
            _, vjp = jax.vjp(fn, [au_ref[r, :].astype(f32) for r in rows], [av_ref[r, :].astype(f32) for r in rows], wv, bv)
            daus, davs, dwc, dbc = vjp([dy_ref[r, :].astype(f32) for r in rows])
            for r, dau, dav in zip(rows, daus, davs):
                dp_ref[r, 0:GW] = dau.astype(bf16)
                dp_ref[r, GW:2 * GW] = dav.astype(bf16)
            return dw_acc + dwc, db_acc + dbc
        dw_acc, db_acc = lax.fori_loop(0, S // BLK // G, group, (jnp.zeros(wv.shape, f32), jnp.zeros(bv.shape, f32)))
        dw_ref[...] += dw_acc
        db_ref[...] += jnp.dot(db_acc, consts[0], precision=HIGHEST, preferred_element_type=f32)

    return _hosted_call(
        body, B,
        [pl.BlockSpec((S, GW), lambda b: (b, 0)),
         pl.BlockSpec((S, GW), lambda b: (b, 1)),
         pl.BlockSpec((GW // HD, BLK, BLK), lambda b: (0, 0, 0)),
         pl.BlockSpec((BLK, GW), lambda b: (0, 0)),
         pl.BlockSpec((S, GW), lambda b: (b, 0))],
        [pl.BlockSpec((S, 2 * GW), lambda b: (b, 0)),
         pl.BlockSpec((GW // HD, BLK, BLK), lambda b: (0, 0, 0)),
         pl.BlockSpec((BLK, GW), lambda b: (0, 0))],
        [SDS((B * S, 2 * GW), bf16), SDS((GW // HD, BLK, BLK), f32), SDS((BLK, GW), f32)], [],
        (proj, proj, w, bexp, dy), "sgu_bwd", comm)


def _pool_parts(p):
    n = p.shape[0]
    r = _iota(p.shape, 0)
    lg = _iota(p.shape, 1) // HD

    def sh(v, k):
        return jnp.where(r >= k, pltpu.roll(v, k, 0), 0.0)
    s2 = p + sh(p, 1)
    s4 = s2 + sh(s2, 2)
    s8 = s4 + sh(s4, 4)
    s16 = s8 + sh(s8, 8)
    ws = jnp.where(lg == 0, s2, jnp.where(lg == 1, s4, jnp.where(lg == 2, s8, s16)))
    wlen = jnp.where(lg == 0, 2, jnp.where(lg == 1, 4, jnp.where(lg == 2, 8, 16)))
    cnt = jnp.minimum(r + 1, wlen).astype(f32)
    del n
    return ws / cnt - p, cnt, lg


def _pool_fwd(proj, wbd, scale, B, S):
    def body(p_ref, w_ref, s_ref, y_ref):
        y, _, _ = _pool_parts(p_ref[...].astype(f32))
        y_ref[...] = (jnp.dot(_mx(y), _mx(w_ref[...]), preferred_element_type=f32) * s_ref[...]).astype(bf16)

    return pl.pallas_call(
        body, grid=(B,),
        in_specs=[pl.BlockSpec((S, GW), lambda b: (b, 2)),
                  pl.BlockSpec((GW, GW), lambda b: (0, 0)),
                  pl.BlockSpec((1, GW), lambda b: (0, 0))],
        out_specs=pl.BlockSpec((S, GW), lambda b: (b, 0)),
        out_shape=SDS((B * S, GW), bf16),
        name="pool_fwd", compiler_params=_cp("parallel"))(proj, wbd, scale)


def _pool_bwd(proj, wbd, scale, dy, B, S):
    def body(p_ref, w_ref, s_ref, dy_ref, dp_ref, dw_ref, ds_ref):
        @pl.when(pl.program_id(0) == 0)
        def _():
            dw_ref[...] = jnp.zeros_like(dw_ref)
            ds_ref[...] = jnp.zeros_like(ds_ref)
        y, cnt, lg = _pool_parts(p_ref[...].astype(f32))
        wv = _mx(w_ref[...])
        z = jnp.dot(_mx(y), wv, preferred_element_type=f32)
        dout = dy_ref[...].astype(f32)
        ds_ref[...] += jnp.sum(dout * z, axis=0, keepdims=True)
        dz = _mx(dout * s_ref[...])
        dw_ref[...] += lax.dot_general(_mx(y), dz, TN, preferred_element_type=f32)
        dyv = lax.dot_general(dz, wv, NT, preferred_element_type=f32)
        n = dyv.shape[0]
        r = _iota(dyv.shape, 0)

        def ush(v, k):
            return jnp.where(r < n - k, pltpu.roll(v, n - k, 0), 0.0)
        gq = dyv / cnt
        a2 = gq + ush(gq, 1)
        a4 = a2 + ush(a2, 2)
        a8 = a4 + ush(a4, 4)
        a16 = a8 + ush(a8, 8)
        adj = jnp.where(lg == 0, a2, jnp.where(lg == 1, a4, jnp.where(lg == 2, a8, a16)))
        dp_ref[...] = (adj - dyv).astype(bf16)

    return pl.pallas_call(
        body, grid=(B,),
        in_specs=[pl.BlockSpec((S, GW), lambda b: (b, 2)),
                  pl.BlockSpec((GW, GW), lambda b: (0, 0)),
                  pl.BlockSpec((1, GW), lambda b: (0, 0)),
                  pl.BlockSpec((S, GW), lambda b: (b, 0))],
        out_specs=[pl.BlockSpec((S, GW), lambda b: (b, 0)),
                   pl.BlockSpec((GW, GW), lambda b: (0, 0)),
                   pl.BlockSpec((1, GW), lambda b: (0, 0))],
        out_shape=[SDS((B * S, GW), bf16), SDS((GW, GW), f32), SDS((1, GW), f32)],
        name="pool_bwd", compiler_params=_cp("arbitrary"))(proj, wbd, scale, dy)


def _t5_bucket_table():
    dist = (np.arange(BLK)[:, None] + BLK) - np.arange(2 * BLK)[None, :]
    d = np.clip(dist, 0, BLK - 1)
    max_exact = N_BUCKETS // 2
    df = np.maximum(d, 1).astype(np.float32)
    large = max_exact + (np.log(df / max_exact) / np.float32(np.log(MAX_DISTANCE / max_exact))
                         * (N_BUCKETS - max_exact)).astype(np.int32)
    large = np.minimum(large, N_BUCKETS - 1)
    return np.where(d < max_exact, d, large).astype(np.int32)


def _swa_blocks(qs, kx, vx, sinks, biases, n):
    G = len(qs)
    heads = [(p, g) for p in range(2) for g in range(2)]
    ri, ci = _iota((BLK, BLK), 0), _iota((BLK, BLK), 1)
    qi, ki = _iota((BLK, 2 * BLK), 0), _iota((BLK, 2 * BLK), 1)
    dist = qi + BLK - ki
    band = (dist >= 0) & (dist < BLK)
    masks = [band & ((ki >= BLK) | (n > 0))] + [band] * (G - 1)
    kb, vb = _mx(kx), _mx(vx)
    qsel = [[None] * 4 for _ in range(G)]
    vs = []
    for h, (p, g) in enumerate(heads):
        selq = ((ri - g * HD == ci - p * HD) & (ri >= g * HD) & (ri < (g + 1) * HD)).astype(_MXU)
        selv = ((ci - g * HD == ri - p * HD) & (ci >= g * HD) & (ci < (g + 1) * HD)).astype(_MXU)
        for b in range(G):
            qsel[b][h] = _mx(jnp.dot(_mx(qs[b][p]), selq, preferred_element_type=f32))
        vs.append(_mx(jnp.dot(vb, selv, preferred_element_type=f32)))
    zs = [[lax.dot_general(qsel[b][h], kb[b * BLK:(b + 2) * BLK], NT, preferred_element_type=f32) * (HD ** -0.5)
           for h in range(4)] for b in range(G)]
    prs = [[None] * 4 for _ in range(G)]
    for b in range(G):
        for h in range(4):
            z = jnp.where(masks[b], zs[b][h] + biases[h], -1e30)
            s = jnp.mean(sinks[h], axis=-1, keepdims=True)
            m = jnp.maximum(jnp.max(z, axis=-1, keepdims=True), s)
            e = jnp.exp(z - m)
            prs[b][h] = _mx(e / (jnp.sum(e, axis=-1, keepdims=True) + jnp.exp(s - m)))
    outs = [[jnp.dot(prs[b][h], vs[h][b * BLK:(b + 2) * BLK], preferred_element_type=f32) for h in range(4)]
            for b in range(G)]
    return [[o[0] + o[1], o[2] + o[3]] for o in outs]


def _swa_group(S):
    return 2 if (S // BLK) % 2 == 0 else 1


def _swa_rows(n, G):
    blk = lambda j: pl.ds(pl.multiple_of(j * BLK, BLK), BLK)
    return [blk(jnp.maximum(n - 1, 0))] + [blk(n + b) for b in range(G)]


def _swa_fwd(proj, sinks, bias, B, S, comm=None):
    G = _swa_group(S)

    def body(q_ref, kv_ref, s_ref, b_ref, y_ref):
        def group(i, c):
            n = i * G
            rows = _swa_rows(n, G)
            kx = jnp.concatenate([kv_ref[r, 0:BLK] for r in rows], axis=0).astype(f32)
            vx = jnp.concatenate([kv_ref[r, BLK:2 * BLK] for r in rows], axis=0).astype(f32)
            qs = [[q_ref[r, 0:BLK].astype(f32), q_ref[r, BLK:2 * BLK].astype(f32)] for r in rows[1:]]
            outs = _swa_blocks(qs, kx, vx, [s_ref[h] for h in range(4)], [b_ref[h] for h in range(4)], n)
            for r, (o0, o1) in zip(rows[1:], outs):
                y_ref[r, 0:BLK] = o0.astype(bf16)
                y_ref[r, BLK:2 * BLK] = o1.astype(bf16)
            return c
        lax.fori_loop(0, S // BLK // G, group, 0)

    return _hosted_call(
        body, B,
        [pl.BlockSpec((S, GW), lambda b: (b, 3)),
         pl.BlockSpec((S, GW), lambda b: (b, 4)),
         pl.BlockSpec((4, 1, BLK), lambda b: (0, 0, 0)),
         pl.BlockSpec((4, BLK, 2 * BLK), lambda b: (0, 0, 0))],
        [pl.BlockSpec((S, GW), lambda b: (b, 0))], [SDS((B * S, GW), bf16)], [],
        (proj, proj, sinks, bias), "swa_fwd", comm)


def _swa_bwd(proj, sinks, bias, dy, B, S, comm=None):
    def body(q_ref, kv_ref, s_ref, b_ref, dy_ref, dq_ref, dkv_ref, ds_ref, db_ref, acc_ref):
        @pl.when(pl.program_id(0) == 0)
        def _():
            ds_ref[...] = jnp.zeros_like(ds_ref)
            db_ref[...] = jnp.zeros_like(db_ref)
        acc_ref[...] = jnp.zeros_like(acc_ref)

        G = _swa_group(S)

        def group(i, c):
            n = i * G
            rows = _swa_rows(n, G)
            kx = jnp.concatenate([kv_ref[r, 0:BLK] for r in rows], axis=0).astype(f32)
            vx = jnp.concatenate([kv_ref[r, BLK:2 * BLK] for r in rows], axis=0).astype(f32)
            qs = [[q_ref[r, 0:BLK].astype(f32), q_ref[r, BLK:2 * BLK].astype(f32)] for r in rows[1:]]
            dos = [[dy_ref[r, 0:BLK].astype(f32), dy_ref[r, BLK:2 * BLK].astype(f32)] for r in rows[1:]]
            fn = functools.partial(_swa_blocks, n=n)
            _, vjp = jax.vjp(fn, qs, kx, vx, [s_ref[h] for h in range(4)], [b_ref[h] for h in range(4)])
            dqs, dkx, dvx, dss, dbs = vjp(dos)
            for r, (dq0, dq1) in zip(rows[1:], dqs):
                dq_ref[r, 0:BLK] = dq0.astype(bf16)
                dq_ref[r, BLK:2 * BLK] = dq1.astype(bf16)
            for h in range(4):
                ds_ref[h] += dss[h]
                db_ref[h] += dbs[h]
            for j, r in enumerate(rows):
                acc_ref[r, 0:BLK] += dkx[j * BLK:(j + 1) * BLK]
                acc_ref[r, BLK:2 * BLK] += dvx[j * BLK:(j + 1) * BLK]
            return c
        lax.fori_loop(0, S // BLK // G, group, 0)
        dkv_ref[...] = acc_ref[...].astype(bf16)

    c_args, c_in, c_out, c_shapes, aliases, c_scratch = _host_specs(comm, 5, 4)
    step = lambda v: (lambda: pl.program_id(0) == v)
    return pl.pallas_call(
        _host(body, 5, 4, 1, comm, step(0), step(B - 1)), grid=(B,),
        in_specs=[pl.BlockSpec((S, GW), lambda b: (b, 3)),
                  pl.BlockSpec((S, GW), lambda b: (b, 4)),
                  pl.BlockSpec((4, 1, BLK), lambda b: (0, 0, 0)),
                  pl.BlockSpec((4, BLK, 2 * BLK), lambda b: (0, 0, 0)),
                  pl.BlockSpec((S, GW), lambda b: (b, 0))] + c_in,
        out_specs=[pl.BlockSpec((S, GW), lambda b: (b, 0)),
                   pl.BlockSpec((S, GW), lambda b: (b, 0)),
                   pl.BlockSpec((4, 1, BLK), lambda b: (0, 0, 0)),
                   pl.BlockSpec((4, BLK, 2 * BLK), lambda b: (0, 0, 0))] + c_out,
        out_shape=[SDS((B * S, GW), bf16), SDS((B * S, GW), bf16), SDS((4, 1, BLK), f32),
                   SDS((4, BLK, 2 * BLK), f32)] + c_shapes,
        input_output_aliases=aliases, scratch_shapes=[pltpu.VMEM((S, GW), f32)] + c_scratch,
        name="swa_bwd" if comm is None else "swa_bwd_exchange",
        compiler_params=_cp("arbitrary"))(proj, proj, sinks, bias, dy, *c_args)


def _log1m_parts(z):
    t = jnp.exp(-jnp.abs(z))
    return jnp.minimum(-z, 0.0) - jnp.log(1.0 + t), t


def _log1m(z):
    return _log1m_parts(z)[0]


def _sigmoid_from(z, t):
    return jnp.where(z >= 0.0, 1.0, t) / (1.0 + t)


def _sb_consts(tri):
    r2, c2 = _iota((2 * BLK, 2 * BLK), 0), _iota((2 * BLK, 2 * BLK), 1)
    tri2 = (tri(r2, c2) & (r2 // BLK == c2 // BLK)).astype(bf16)
    ri, ci = _iota((BLK, 2 * BLK), 0), _iota((BLK, 2 * BLK), 1)
    strict2 = (ci % BLK) < ri
    head0 = _iota((BLK, BLK), 1) < HD
    return tri2, strict2, head0


def _sb_stack_kv(k_ref, v_ref, kst_ref, vst_ref, head0, nb):
    def one(kb, c):
        krows = pl.ds(pl.multiple_of(kb * BLK, BLK), BLK)
        for p in range(2):
            for src, dst in ((k_ref, kst_ref), (v_ref, vst_ref)):
                t = src[krows, p * BLK:(p + 1) * BLK]
                dst[p, kb] = _mx(jnp.concatenate([jnp.where(head0, t, 0.0), jnp.where(head0, 0.0, t)], axis=0))
        return c
    lax.fori_loop(0, nb, one, 0)


def _sb_load_kv(kst_ref, vst_ref, kb):
    return [kst_ref[p, kb] for p in range(2)], [vst_ref[p, kb] for p in range(2)]


def _two_halves(a, b):
    return jnp.concatenate([jnp.broadcast_to(a, (BLK, BLK)), jnp.broadcast_to(b, (BLK, BLK))], axis=1)


def _half_sums(t):
    return jnp.sum(t[:, :BLK], axis=-1, keepdims=True), jnp.sum(t[:, BLK:], axis=-1, keepdims=True)


def _sb_fwd(proj, B, S, comm=None):
    def body(q_ref, k_ref, v_ref, y_ref, lt_ref, kst_ref, vst_ref):
        ci = _iota((BLK, BLK), 1)
        above2, strict2, head0 = _sb_consts(lambda r, c: r > c)
        _sb_stack_kv(k_ref, v_ref, kst_ref, vst_ref, head0, S // BLK)

        def step(qs, kbs, diags, carry):
            U = range(len(kbs))
            ok = [None if diags[u] else kbs[u] >= 0 for u in U]
            kv = [_sb_load_kv(kst_ref, vst_ref, jnp.maximum(kb, 0)) for kb in kbs]
            zs = [[lax.dot_general(qs[p], kks[p], NT, preferred_element_type=f32) for p in range(2)] for kks, _ in kv]
            Ls = [[jnp.where(strict2, _log1m(z), 0.0) if diags[u] else _log1m(z) for z in zs[u]] for u in U]
            tails = [[_split_dot(L, above2) for L in Lu] for Lu in Ls]
            carry = list(carry)
            for u in U:
                for p in range(2):
                    R0, R1, acc = carry[3 * p:3 * p + 3]
                    w = jnp.exp(zs[u][p] + Ls[u][p] + tails[u][p] + _two_halves(R0, R1))
                    s0, s1 = _half_sums(Ls[u][p])
                    if diags[u]:
                        w = jnp.where(strict2, w, 0.0)
                    else:
                        w, s0, s1 = (jnp.where(ok[u], t, 0.0) for t in (w, s0, s1))
                    acc = acc + jnp.dot(_mx(w), kv[u][1][p], preferred_element_type=f32)
                    carry[3 * p:3 * p + 3] = [R0 + s0, R1 + s1, acc]
            return tuple(carry)

        def qblock(n, c):
            qrows = pl.ds(pl.multiple_of(n * BLK, BLK), BLK)
            qs = [_mx(q_ref[qrows, p * BLK:(p + 1) * BLK] * (HD ** -0.5)) for p in range(2)]
            z1, z2 = jnp.zeros((BLK, 1), f32), jnp.zeros((BLK, BLK), f32)
            near = [n - 1 - u for u in range(SB_UNROLL)]
            carry = step(qs, [n] + near, [True] + [False] * SB_UNROLL, (z1, z1, z2, z1, z1, z2))
            far = jnp.maximum(n - SB_UNROLL, 0)
            trips = (far + SB_UNROLL - 1) // SB_UNROLL

            def live(st):
                worst = jnp.maximum(jnp.maximum(st[1], st[2]), jnp.maximum(st[4], st[5]))
                return (st[0] < trips) & (jnp.max(worst) > SB_CUT)

            def trip(st):
                i = st[0]
                kbs = [far - 1 - SB_UNROLL * i - u for u in range(SB_UNROLL)]
                return (i + 1,) + step(qs, kbs, [False] * SB_UNROLL, st[1:])
            done, *res = lax.while_loop(live, trip, (jnp.int32(0),) + carry)
            lt = jnp.where(ci == SB_HEADS, done.astype(f32), 0.0)
            for p in range(2):
                y_ref[qrows, p * BLK:(p + 1) * BLK] = res[3 * p + 2].astype(bf16)
                lt = lt + jnp.where(ci == 2 * p, res[3 * p], 0.0) + jnp.where(ci == 2 * p + 1, res[3 * p + 1], 0.0)
            lt_ref[qrows, :] = lt
            return c
        lax.fori_loop(0, S // BLK, qblock, 0)

    spec = lambda j: pl.BlockSpec((S, GW), lambda b: (b, j))
    c_args, c_in, c_out, c_shapes, aliases, c_scratch = _host_specs(comm, 3, 2)
    step = lambda v: (lambda: pl.program_id(0) == v)
    stacked = pltpu.VMEM((2, S // BLK, 2 * BLK, BLK), _MXU)
    return pl.pallas_call(
        _host(body, 3, 2, 2, comm, step(0), step(B - 1)), grid=(B,),
        in_specs=[spec(5), spec(6), spec(7)] + c_in,
        out_specs=[pl.BlockSpec((S, GW), lambda b: (b, 0)), pl.BlockSpec((S, BLK), lambda b: (b, 0))] + c_out,
        out_shape=[SDS((B * S, GW), bf16), SDS((B * S, BLK), f32)] + c_shapes,
        input_output_aliases=aliases, scratch_shapes=[stacked, stacked] + c_scratch,
        name="sb_fwd" if comm is None else "sb_fwd_gather",
        compiler_params=_cp("arbitrary"))(proj, proj, proj, *c_args)


def _sb_bwd(proj, ltot, dy, B, S, comm=None):
    def body(q_ref, k_ref, v_ref, lt_ref, dy_ref, dq_ref, dk_ref, dv_ref, dka_ref, dva_ref, kst_ref, vst_ref):
        ci = _iota((BLK, BLK), 1)
        upto2, strict2, head0 = _sb_consts(lambda r, c: r <= c)
        below2, _, _ = _sb_consts(lambda r, c: r < c)
        dka_ref[...] = jnp.zeros_like(dka_ref)
        dva_ref[...] = jnp.zeros_like(dva_ref)
        _sb_stack_kv(k_ref, v_ref, kst_ref, vst_ref, head0, S // BLK)

        def step(qs, dos, lts, kbs, ok, diags, top, carry):
            U = range(len(kbs))
            kbs = [jnp.clip(kb, 0, top) for kb in kbs]
            kv = [_sb_load_kv(kst_ref, vst_ref, kb) for kb in kbs]
            zs = [[lax.dot_general(qs[p], kv[u][0][p], NT, preferred_element_type=f32) for p in range(2)] for u in U]
            dws = [[lax.dot_general(dos[p], kv[u][1][p], NT, preferred_element_type=f32) for p in range(2)] for u in U]
            parts = [[_log1m_parts(z) for z in zu] for zu in zs]
            Ls = [[jnp.where(strict2, lt[0], 0.0) if diags[u] else lt[0] for lt in parts[u]] for u in U]
            pins = [[_split_dot(L, upto2) for L in Lu] for Lu in Ls]
            carry = list(carry)
            ws, das = [], []
            for u in U:
                wu, dau = [], []
                for p in range(2):
                    PL0, PL1 = carry[5 * p], carry[5 * p + 1]
                    tail = _two_halves(lts[2 * p] - PL0, lts[2 * p + 1] - PL1) - pins[u][p]
                    w = jnp.exp(zs[u][p] + Ls[u][p] + tail)
                    l0, l1 = _half_sums(Ls[u][p])
                    if diags[u]:
                        w = jnp.where(strict2, w, 0.0)
                    else:
                        w, l0, l1 = (jnp.where(ok[u], t, 0.0) for t in (w, l0, l1))
                    carry[5 * p], carry[5 * p + 1] = PL0 + l0, PL1 + l1
                    wu.append(w)
                    dau.append(w * dws[u][p])
                ws.append(wu)
                das.append(dau)
            pexs = [[_split_dot(da, below2) for da in dau] for dau in das]
            dzs = []
            for u in U:
                dzu = []
                for p in range(2):
                    dL = _two_halves(carry[5 * p + 2], carry[5 * p + 3]) + pexs[u][p]
                    sg = _sigmoid_from(zs[u][p], parts[u][p][1])
                    dz = das[u][p] * (1.0 - sg) - dL * sg
                    dz = jnp.where(strict2 if diags[u] else ok[u], dz, 0.0)
                    a0, a1 = _half_sums(das[u][p])
                    carry[5 * p + 2], carry[5 * p + 3] = carry[5 * p + 2] + a0, carry[5 * p + 3] + a1
                    dzu.append(_mx(dz))
                dzs.append(dzu)
            dqs = [[jnp.dot(dzs[u][p], kv[u][0][p], preferred_element_type=f32) for p in range(2)] for u in U]
            dks = [[lax.dot_general(dzs[u][p], qs[p], TN, preferred_element_type=f32) for p in range(2)] for u in U]
            dvs = [[lax.dot_general(_mx(ws[u][p]), dos[p], TN, preferred_element_type=f32) for p in range(2)] for u in U]
            for u in U:
                krows = pl.ds(pl.multiple_of(kbs[u] * BLK, BLK), BLK)
                for p in range(2):
                    lanes = slice(p * BLK, (p + 1) * BLK)
                    dka_ref[krows, lanes] += jnp.where(head0, dks[u][p][:BLK], dks[u][p][BLK:])
                    dva_ref[krows, lanes] += jnp.where(head0, dvs[u][p][:BLK], dvs[u][p][BLK:])
                    carry[5 * p + 4] = carry[5 * p + 4] + dqs[u][p]
            return tuple(carry)

        def qblock(n, c):
            qrows = pl.ds(pl.multiple_of(n * BLK, BLK), BLK)
            ltb = lt_ref[qrows, :]
            lts = [jnp.sum(jnp.where(ci == h, ltb, 0.0), axis=-1, keepdims=True) for h in range(4)]
            qs = [_mx(q_ref[qrows, p * BLK:(p + 1) * BLK] * (HD ** -0.5)) for p in range(2)]
            dos = [_mx(dy_ref[qrows, p * BLK:(p + 1) * BLK]) for p in range(2)]
            z1, z2 = jnp.zeros((BLK, 1), f32), jnp.zeros((BLK, BLK), f32)
            done = jnp.max(jnp.where(ci == SB_HEADS, ltb, 0.0)).astype(jnp.int32)
            far = jnp.maximum(n - SB_UNROLL, 0)
            first = jnp.maximum(far - SB_UNROLL * done, 0)

            def trip(i, cr):
                kbs = [first + SB_UNROLL * i + u for u in range(SB_UNROLL)]
                return step(qs, dos, lts, kbs, [kb < far for kb in kbs], [False] * SB_UNROLL, n, cr)
            carry = lax.fori_loop(0, (far - first + SB_UNROLL - 1) // SB_UNROLL, trip, (z1, z1, z1, z1, z2) * 2)
            near = [n - SB_UNROLL + u for u in range(SB_UNROLL)]
            res = step(qs, dos, lts, near + [n], [kb >= 0 for kb in near] + [None], [False] * SB_UNROLL + [True], n, carry)
            for p in range(2):
                dq_ref[qrows, p * BLK:(p + 1) * BLK] = (res[5 * p + 4] * (HD ** -0.5)).astype(bf16)
            return c
        lax.fori_loop(0, S // BLK, qblock, 0)
        dk_ref[...] = dka_ref[...].astype(bf16)
        dv_ref[...] = dva_ref[...].astype(bf16)

    spec = lambda j: pl.BlockSpec((S, GW), lambda b: (b, j))
    o = pl.BlockSpec((S, GW), lambda b: (b, 0))
    c_args, c_in, c_out, c_shapes, aliases, c_scratch = _host_specs(comm, 5, 3)
    step = lambda v: (lambda: pl.program_id(0) == v)
    stacked = pltpu.VMEM((2, S // BLK, 2 * BLK, BLK), _MXU)
    return pl.pallas_call(
        _host(body, 5, 3, 4, comm, step(0), step(B - 1)), grid=(B,),
        in_specs=[spec(5), spec(6), spec(7), pl.BlockSpec((S, BLK), lambda b: (b, 0)), o] + c_in,
        out_specs=[o, o, o] + c_out,
        out_shape=[SDS((B * S, GW), bf16)] * 3 + c_shapes,
        input_output_aliases=aliases,
        scratch_shapes=[pltpu.VMEM((S, GW), f32), pltpu.VMEM((S, GW), f32), stacked, stacked] + c_scratch,
        name="sb_bwd" if comm is None else "sb_bwd_exchange",
        compiler_params=_cp("arbitrary"))(proj, proj, proj, ltot, dy, *c_args)


def _bias_expand(rel_bias_t, bucket):
    n = bucket.shape[1]

    def body(r_ref, b_ref, o_ref):
        onehot = (_iota((N_BUCKETS, n), 0) == b_ref[...]).astype(f32)
        o_ref[...] = jnp.dot(r_ref[...], onehot, precision=HIGHEST, preferred_element_type=f32)
    return pl.pallas_call(body, out_shape=SDS((rel_bias_t.shape[0], n), f32), name="bias_expand",
                          compiler_params=_cp())(rel_bias_t, bucket)


def _bias_reduce(dbias, bucket):
    n = bucket.shape[1]

    def body(*refs):
        b_ref, g_ref = refs[-2], refs[-1]
        d = refs[0][...]
        for r in refs[1:-2]:
            d = d + r[...]
        onehot = (_iota((N_BUCKETS, n), 0) == b_ref[...]).astype(f32)
        g_ref[...] = lax.dot_general(d, onehot, NT, precision=HIGHEST, preferred_element_type=f32)
    return pl.pallas_call(body, out_shape=SDS((dbias[0].shape[0], N_BUCKETS), f32), name="bias_reduce",
                          compiler_params=_cp())(*dbias, bucket)


def _adamw(w, g, m, v, tr, name, emit_g=False):
    R, C = w.shape

    def body(w_ref, g_ref, m_ref, v_ref, d_ref, m2_ref, v2_ref, *g_out):
        gv = g_ref[...]
        if emit_g:
            g_out[0][...] = gv
        m2 = ADAM_B1 * m_ref[...] + (1.0 - ADAM_B1) * gv
        v2 = ADAM_B2 * v_ref[...] + (1.0 - ADAM_B2) * (gv * gv)
        m_hat = m2 / (1.0 - ADAM_B1 ** ADAM_STEP)
        v_hat = v2 / (1.0 - ADAM_B2 ** ADAM_STEP)
        d_ref[...] = -ADAM_LR * (m_hat / (jnp.sqrt(v_hat) + ADAM_EPS) + ADAM_WD * w_ref[...])
        m2_ref[...] = m2
        v2_ref[...] = v2

    spec = pl.BlockSpec((tr, C), lambda i: (i, 0))
    n_out = 4 if emit_g else 3
    return pl.pallas_call(
        body, grid=(R // tr,), in_specs=[spec] * 4, out_specs=[spec] * n_out,
        out_shape=[SDS((R, C), f32)] * n_out, name=name, compiler_params=_cp("parallel"))(w, g, m, v)


ANY = pl.BlockSpec(memory_space=pl.ANY)


def _place():
    x, y, c = lax.axis_index("x"), lax.axis_index("y"), lax.axis_index("c")
    chips = [(1 - x, y), (x, 1 - y), (1 - x, 1 - y)]
    return x, y, c, chips


def _cast_slots(w, kidx):
    L, a, b = w.shape
    ta = a // 2

    def body(k_ref, *refs):
        for l in range(L):
            refs[L + l][0] = refs[l][0].astype(bf16)

    return pl.pallas_call(
        body,
        grid_spec=pltpu.PrefetchScalarGridSpec(
            num_scalar_prefetch=1, grid=(a // ta,),
            in_specs=[pl.BlockSpec((1, ta, b), functools.partial(lambda i, k_ref, l: (l, i, 0), l=l)) for l in range(L)],
            out_specs=[pl.BlockSpec((1, ta, b), lambda i, k_ref: (k_ref[0], i, 0)) for _ in range(L)]),
        out_shape=[SDS((N_CHIPS, a, b), bf16)] * L,
        name="cast_slots", compiler_params=_cp("parallel"))(kidx, *([w] * L))


class _GatherComm:
    def __init__(self, bufs, part=0, parts=1):
        self.inputs = list(bufs)
        self.out_shape = [SDS(b.shape, b.dtype) for b in bufs]
        self.aliased = True
        self.scratch = [pltpu.SemaphoreType.DMA((3 * len(bufs),))] * 4
        self.part, self.parts = part, parts

    def _copies(self, i_refs, o_refs, sems):
        send1, recv1, send2, recv2 = sems
        x, y, c, chips = _place()
        k = 2 * x + y
        first, got1, second, got2 = [], [], [], []
        for i, buf in enumerate(self.inputs):
            h = buf.shape[1] // 2
            n = h // self.parts
            mine, theirs = pl.ds(c * h + self.part * n, n), pl.ds((1 - c) * h + self.part * n, n)
            for j, (cx, cy) in enumerate(chips):
                s = 3 * i + j
                first.append(pltpu.make_async_remote_copy(
                    src_ref=i_refs[i].at[k, mine], dst_ref=o_refs[i].at[k, mine], send_sem=send1.at[s],
                    recv_sem=recv1.at[s], device_id=(cx, cy, c), device_id_type=MESH))
                a = o_refs[i].at[2 * cx + cy, mine]
                got1.append(pltpu.make_async_remote_copy(
                    src_ref=a, dst_ref=a, send_sem=send1.at[s], recv_sem=recv1.at[s],
                    device_id=(cx, cy, c), device_id_type=MESH))
                second.append(pltpu.make_async_remote_copy(
                    src_ref=a, dst_ref=a, send_sem=send2.at[s], recv_sem=recv2.at[s],
                    device_id=(x, y, 1 - c), device_id_type=MESH))
                b = o_refs[i].at[2 * cx + cy, theirs]
                got2.append(pltpu.make_async_remote_copy(
                    src_ref=b, dst_ref=b, send_sem=send2.at[s], recv_sem=recv2.at[s],
                    device_id=(x, y, 1 - c), device_id_type=MESH))
        return first, got1, second, got2

    def start(self, i_refs, o_refs, sems):
        for cp in self._copies(i_refs, o_refs, sems)[0]:
            cp.start()

    def finish(self, i_refs, o_refs, sems):
        first, got1, second, got2 = self._copies(i_refs, o_refs, sems)
        for g, cp in zip(got1, second):
            g.wait_recv()
            cp.start()
        for g in got2:
            g.wait_recv()
        for cp in first + second:
            cp.wait_send()


class _MultiComm:
    def __init__(self, comms):
        self.comms = comms
        self.inputs = [a for c in comms for a in c.inputs]
        self.out_shape = [s for c in comms for s in c.out_shape]
        self.aliased = comms[0].aliased
        assert all(c.aliased == self.aliased for c in comms)
        self.scratch = [s for c in comms for s in c.scratch]

    def _split(self, i_refs, o_refs, sems):
        i = o = s = 0
        for c in self.comms:
            ni, no, ns = len(c.inputs), len(c.out_shape), len(c.scratch)
            yield c, i_refs[i:i + ni], o_refs[o:o + no], sems[s:s + ns]
            i, o, s = i + ni, o + no, s + ns

    def start(self, i_refs, o_refs, sems):
        for c, i, o, s in self._split(i_refs, o_refs, sems):
            c.start(i, o, s)

    def finish(self, i_refs, o_refs, sems):
        for c, i, o, s in self._split(i_refs, o_refs, sems):
            c.finish(i, o, s)


class _PairExchangeComm:
    def __init__(self, gs):
        self.inputs = list(gs)
        self.out_shape = [SDS((g.shape[0], g.shape[1] // 2, g.shape[2]), g.dtype) for g in gs]
        self.aliased = False
        self.scratch = [pltpu.SemaphoreType.DMA((len(gs),))] * 2

    def _copies(self, i_refs, o_refs, sems):
        send, recv = sems
        x, y, c, _ = _place()
        cps = []
        for i, g in enumerate(self.inputs):
            h = g.shape[1] // 2
            cps.append(pltpu.make_async_remote_copy(
                src_ref=i_refs[i].at[:, pl.ds((1 - c) * h, h)], dst_ref=o_refs[i], send_sem=send.at[i], recv_sem=recv.at[i],
                device_id=(x, y, 1 - c), device_id_type=MESH))
        return cps

    def start(self, i_refs, o_refs, sems):
        for cp in self._copies(i_refs, o_refs, sems):
            cp.start()

    def finish(self, i_refs, o_refs, sems):
        for cp in self._copies(i_refs, o_refs, sems):
            cp.wait()


class _ChipExchangeComm:
    def __init__(self, qs):
        self.inputs = list(qs)
        self.out_shape = [SDS(q.shape, q.dtype) for q in qs]
        self.aliased = False
        self.scratch = [pltpu.SemaphoreType.DMA((3 * len(qs),))] * 2

    def _copies(self, i_refs, o_refs, sems):
        send, recv = sems
        x, y, c, chips = _place()
        k = 2 * x + y
        cps, got = [], []
        for i in range(len(self.inputs)):
            for j, (cx, cy) in enumerate(chips):
                s = 3 * i + j
                cps.append(pltpu.make_async_remote_copy(
                    src_ref=i_refs[i].at[2 * cx + cy], dst_ref=o_refs[i].at[k], send_sem=send.at[s],
                    recv_sem=recv.at[s], device_id=(cx, cy, c), device_id_type=MESH))
                a = o_refs[i].at[2 * cx + cy]
                got.append(pltpu.make_async_remote_copy(
                    src_ref=a, dst_ref=a, send_sem=send.at[s], recv_sem=recv.at[s],
                    device_id=(cx, cy, c), device_id_type=MESH))
        return cps, got

    def start(self, i_refs, o_refs, sems):
        for cp in self._copies(i_refs, o_refs, sems)[0]:
            cp.start()

    def finish(self, i_refs, o_refs, sems):
        cps, got = self._copies(i_refs, o_refs, sems)
        for g in got:
            g.wait_recv()
        for cp in cps:
            cp.wait_send()


def _comm_only(comm, name):
    n = len(comm.inputs)

    def body(*refs):
        i_refs, o_refs, sems = refs[:n], refs[n:n + len(comm.out_shape)], refs[n + len(comm.out_shape):]
        comm.start(i_refs, o_refs, sems)
        comm.finish(i_refs, o_refs, sems)

    return pl.pallas_call(
        body, out_shape=comm.out_shape, in_specs=[ANY] * n, out_specs=[ANY] * len(comm.out_shape),
        input_output_aliases={i: i for i in range(n)} if comm.aliased else {},
        scratch_shapes=comm.scratch, name=name,
        compiler_params=pltpu.CompilerParams(has_side_effects=True))(*comm.inputs)


def _host(body, n_in, n_out, n_scratch, comm, first, last):
    if comm is None:
        return body
    ci, co = len(comm.inputs), len(comm.out_shape)

    def wrapped(*refs):
        o = 0
        parts = []
        for n in (n_in, ci, n_out, co, n_scratch):
            parts.append(refs[o:o + n])
            o += n
        hin, cin, hout, cout, hs = parts
        sems = refs[o:]

        @pl.when(first())
        def _():
            comm.start(cin, cout, sems)
        body(*hin, *hout, *hs)

        @pl.when(last())
        def _():
            comm.finish(cin, cout, sems)
    return wrapped


def _host_specs(comm, n_in, n_out):
    if comm is None:
        return [], [], [], [], {}, []
    ci, co = len(comm.inputs), len(comm.out_shape)
    aliases = {n_in + i: n_out + i for i in range(ci)} if comm.aliased else {}
    return comm.inputs, [ANY] * ci, [ANY] * co, comm.out_shape, aliases, comm.scratch


def _pair_add(g, r, kc, name):
    ns, a, b = g.shape
    h = a // 2
    th = h if h * b * 4 <= 4 * 1024 * 1024 else h // 2

    def body(kc_ref, g_ref, r_ref, qb_ref):
        qb_ref[...] = (g_ref[...] + r_ref[...]).astype(bf16)

    nb = h // th
    slot = lambda j, kc_ref: (kc_ref[0] + 1 + j) % ns
    spec = pl.BlockSpec((1, th, b), lambda j, i, kc_ref: (slot(j, kc_ref), i, 0))
    return pl.pallas_call(
        body,
        grid_spec=pltpu.PrefetchScalarGridSpec(
            num_scalar_prefetch=1, grid=(ns - 1, nb),
            in_specs=[pl.BlockSpec((1, th, b), lambda j, i, kc_ref: (slot(j, kc_ref), kc_ref[1] * nb + i, 0)), spec],
            out_specs=spec),
        out_shape=SDS((ns, h, b), bf16),
        name=name, compiler_params=_cp("parallel", "parallel"))(kc, g, r)


def _chip_add(g, r1, r2, idx, prev, L, name):
    ns, h, b = r2.shape
    th = h if h * b * 4 <= 4 * 1024 * 1024 else h // 2
    nb = h // th

    def body(s_ref, g_ref, r1_ref, a_ref, b_ref, c_ref, *rest):
        o_ref = rest[-1]
        o_ref[0] = (g_ref[0] + r1_ref[0]) + a_ref[0].astype(f32) + b_ref[0].astype(f32) + c_ref[0].astype(f32)

    other = lambda d: pl.BlockSpec((1, th, b), lambda i, s_ref: ((s_ref[0] + d) % ns, i, 0))
    in_specs = [pl.BlockSpec((1, th, b), lambda i, s_ref: (s_ref[0], s_ref[1] * nb + i, 0)),
                pl.BlockSpec((1, th, b), lambda i, s_ref: (s_ref[0], i, 0)), other(1), other(2), other(3)]
    args = [idx, g, r1, r2, r2, r2]
    aliases = {}
    if prev is not None:
        in_specs.append(ANY)
        args.append(prev)
        aliases = {6: 0}
    return pl.pallas_call(
        body,
        grid_spec=pltpu.PrefetchScalarGridSpec(
            num_scalar_prefetch=1, grid=(nb,), in_specs=in_specs,
            out_specs=pl.BlockSpec((1, th, b), lambda i, s_ref: (s_ref[2], s_ref[1] * nb + i, 0))),
        out_shape=SDS((L, 2 * h, b), f32), input_output_aliases=aliases,
        name=name, compiler_params=_cp("arbitrary"))(*args)


def _pair_share(gs, hs):
    n = len(gs)
    L = gs[0].shape[0]

    def body(*refs):
        i_refs, o_refs = refs[:n], refs[n:2 * n]
        send, recv = refs[2 * n:]
        x, y, c, _ = _place()
        cps = []
        for i in range(n):
            for l in range(L):
                mine = pl.ds(c * hs[i], hs[i])
                cp = pltpu.make_async_remote_copy(
                    src_ref=i_refs[i].at[l, mine], dst_ref=o_refs[i].at[l, mine], send_sem=send.at[i * L + l],
                    recv_sem=recv.at[i * L + l], device_id=(x, y, 1 - c), device_id_type=MESH)
                cp.start()
                cps.append(cp)
        for i in range(n):
            for l in range(L):
                got = o_refs[i].at[l, pl.ds((1 - c) * hs[i], hs[i])]
                pltpu.make_async_remote_copy(
                    src_ref=got, dst_ref=got, send_sem=send.at[i * L + l], recv_sem=recv.at[i * L + l],
                    device_id=(x, y, 1 - c), device_id_type=MESH).wait_recv()
        for cp in cps:
            cp.wait_send()

    return pl.pallas_call(
        body, out_shape=[SDS(g.shape, g.dtype) for g in gs], in_specs=[ANY] * n, out_specs=[ANY] * n,
        input_output_aliases={i: i for i in range(n)},
        scratch_shapes=[pltpu.SemaphoreType.DMA((n * L,))] * 2,
        name="grad_pair_share", compiler_params=pltpu.CompilerParams(has_side_effects=True))(*gs)


class _SwapComm:
    def __init__(self, arrays):
        self.inputs = list(arrays)
        self.out_shape = [SDS(a.shape, a.dtype) for a in arrays]
        self.aliased = False
        self.scratch = [pltpu.SemaphoreType.DMA((len(arrays),))] * 2

    def _copies(self, i_refs, o_refs, sems):
        send, recv = sems
        x, y, c, _ = _place()
        return [pltpu.make_async_remote_copy(
            src_ref=i_refs[i], dst_ref=o_refs[i], send_sem=send.at[i], recv_sem=recv.at[i],
            device_id=(x, y, 1 - c), device_id_type=MESH) for i in range(len(self.inputs))]

    def start(self, i_refs, o_refs, sems):
        for cp in self._copies(i_refs, o_refs, sems):
            cp.start()

    def finish(self, i_refs, o_refs, sems):
        for cp in self._copies(i_refs, o_refs, sems):
            cp.wait()


class _SlotShareComm:
    def __init__(self, bufs):
        self.inputs = list(bufs)
        self.out_shape = [SDS(b.shape, b.dtype) for b in bufs]
        self.aliased = True
        self.scratch = [pltpu.SemaphoreType.DMA((3 * len(bufs),))] * 2

    def _copies(self, i_refs, o_refs, sems):
        send, recv = sems
        x, y, c, chips = _place()
        k = 2 * x + y
        cps, got = [], []
        for i in range(len(self.inputs)):
            for j, (cx, cy) in enumerate(chips):
                s = 3 * i + j
                cps.append(pltpu.make_async_remote_copy(
                    src_ref=i_refs[i].at[k], dst_ref=o_refs[i].at[k], send_sem=send.at[s], recv_sem=recv.at[s],
                    device_id=(cx, cy, c), device_id_type=MESH))
                a = o_refs[i].at[2 * cx + cy]
                got.append(pltpu.make_async_remote_copy(
                    src_ref=a, dst_ref=a, send_sem=send.at[s], recv_sem=recv.at[s],
                    device_id=(cx, cy, c), device_id_type=MESH))
        return cps, got

    def start(self, i_refs, o_refs, sems):
        for cp in self._copies(i_refs, o_refs, sems)[0]:
            cp.start()

    def finish(self, i_refs, o_refs, sems):
        cps, got = self._copies(i_refs, o_refs, sems)
        for g in got:
            g.wait_recv()
        for cp in cps:
            cp.wait_send()


def _pair_sum_slot(mine, theirs, kidx, dtype):
    R, C = mine.shape

    def body(k_ref, a_ref, b_ref, o_ref):
        o_ref[0] = (a_ref[...] + b_ref[...]).astype(dtype)

    spec = pl.BlockSpec((R, C), lambda i, k_ref: (0, 0))
    return pl.pallas_call(
        body,
        grid_spec=pltpu.PrefetchScalarGridSpec(
            num_scalar_prefetch=1, grid=(1,), in_specs=[spec, spec],
            out_specs=pl.BlockSpec((1, R, C), lambda i, k_ref: (k_ref[0], 0, 0))),
        out_shape=SDS((N_CHIPS, R, C), dtype), name="small_pair_sum", compiler_params=_cp("arbitrary"))(kidx, mine, theirs)


def _small_sum(g):
    n, R, C = g.shape

    def body(g_ref, o_ref):
        acc = g_ref[0].astype(f32)
        for j in range(1, n):
            acc = acc + g_ref[j].astype(f32)
        o_ref[...] = acc
    return pl.pallas_call(body, out_shape=SDS((R, C), f32), name="small_sum", compiler_params=_cp())(g)


PACK_COLS = 1024


def _rows_of(shape):
    n = int(np.prod(shape)) if len(shape) else 1
    return -(-n // (8 * PACK_COLS)) * 8


def _pack(parts):
    blocks = []
    for p in parts:
        flat = p.reshape(-1)
        r = _rows_of(p.shape)
        blocks.append(jnp.pad(flat, (0, r * PACK_COLS - flat.shape[0])).reshape(r, PACK_COLS))
    return jnp.concatenate(blocks, axis=0)


def _unpack(buf, shapes):
    out, off = [], 0
    for s in shapes:
        n = int(np.prod(s)) if len(s) else 1
        r = _rows_of(s)
        out.append(buf[off:off + r].reshape(-1)[:n].reshape(s))
        off += r
    return out


def _block_diag(w):
    g, a, _ = w.shape
    out = jnp.zeros((g * a, g * a), w.dtype)
    for i in range(g):
        out = lax.dynamic_update_slice(out, w[i], (i * a, i * a))
    return out


def kernel(x, w_in, w_out, sgu_w, sgu_b, pool_w, pool_scale, swa_sinks, rel_bias, mix_out_gain, norm_mix, norm_ffn, w_gate_up, w_down, norm_final, loss_target, m_w_in, m_w_out, m_sgu_w, m_sgu_b, m_pool_w, m_pool_scale, m_swa_sinks, m_rel_bias, m_mix_out_gain, m_norm_mix, m_norm_ffn, m_w_gate_up, m_w_down, m_norm_final, v_w_in, v_w_out, v_sgu_w, v_sgu_b, v_pool_w, v_pool_scale, v_swa_sinks, v_rel_bias, v_mix_out_gain, v_norm_mix, v_norm_ffn, v_w_gate_up, v_w_down, v_norm_final):
    B, S, D = x.shape
    T = B * S
    L = w_in.shape[0]
    tm = min(512, T)
    F = w_down.shape[1] * N_CHIPS
    xi, yi, ci = lax.axis_index("x"), lax.axis_index("y"), lax.axis_index("c")
    kidx = jnp.reshape(2 * xi + yi, (1,)).astype(jnp.int32)
    kcidx = jnp.stack([2 * xi + yi, ci]).astype(jnp.int32)

    big = [w_in, w_out, w_gate_up, w_down]
    slots = [_cast_slots(w, kidx) for w in big]
    gather = lambda pi, l, part=0, parts=1: _GatherComm([slots[pi][l]], part, parts)
    Win, Wo, Wgu, Wd = ([None] * L for _ in range(4))
    Win[0], = _comm_only(gather(0, 0), "gather_weights")

    bucket = jnp.asarray(_t5_bucket_table().reshape(1, -1))
    bias_tab = _bias_expand(rel_bias.T, bucket).reshape(4, BLK, 2 * BLK)

    row = lambda v: v.reshape(1, -1)
    xc = x.reshape(T, D)
    tgt = loss_target.reshape(T, D)
    saved = []
    for l in range(L):
        bexp = jnp.repeat(sgu_b[l].T, HD, axis=1)
        wbd = _block_diag(pool_w[l])
        sk = jnp.broadcast_to(swa_sinks[l][:, None, None], (4, 1, BLK))
        h1, proj, wo = _norm_mm(xc, row(norm_mix[l]), Win[l], tm, gather(1, l))
        ya = _sgu_fwd(proj, sgu_w[l], bexp, B, S)
        yb = _pool_fwd(proj, wbd, row(pool_scale[l]), B, S)
        yc, wd = _swa_fwd(proj, sk, bias_tab, B, S, gather(3, l, 0, 2))
        if l == 0:
            yd, lt, wd, Wgu[0] = _sb_fwd(proj, B, S, _MultiComm([_GatherComm([wd], 1, 2), gather(2, 0)]))
        else:
            yd, lt, wd = _sb_fwd(proj, B, S, _GatherComm([wd], 1, 2))
        Wo[l], Wd[l] = wo.reshape(D, D), wd.reshape(F, D)
        ys = (ya, yb, yc, yd)
        nxt = l + 1 < L
        ycn, x1, h2, gu, act, *got = _mix_out_swiglu(ys, row(mix_out_gain[l]), Wo[l], xc, row(norm_ffn[l]), Wgu[l], tm,
                                                     gather(2, l + 1) if nxt else None)
        x2, *got2 = _mm_res(act, Wd[l], x1, tm, gather(0, l + 1) if nxt else None)
        if nxt:
            Wgu[l + 1], Win[l + 1] = got[0], got2[0]
        saved.append((xc, h1, proj, bexp, wbd, sk, ys, lt, ycn, x1, h2, gu, act))
        xc = x2

    dx, g_final, loss_v = _final_loss(xc, row(norm_final), tgt, tm)

    tk = min(T, 2048)
    gW = [[None] * L for _ in range(4)]
    g_sgu_w, g_sgu_b, g_pool_w, g_pool_scale, g_sinks, g_bias = ([None] * L for _ in range(6))
    g_out_gain, g_mix, g_ffn = ([None] * L for _ in range(3))
    reduced = [None] * 4
    sums = {}

    def pair_comm(keys):
        return _PairExchangeComm([gW[pi][l] for pi, l in keys])

    def after_pair(keys, r1):
        for (pi, l), r in zip(keys, r1):
            sums[pi, l] = (r, _pair_add(gW[pi][l], r, kcidx, "grad_pair_add"))

    def chip_comm(keys):
        return _ChipExchangeComm([sums[k][1] for k in keys])

    def after_chip(keys, r2):
        for (pi, l), r in zip(keys, r2):
            idx = jnp.stack([2 * xi + yi, ci, jnp.int32(l)]).astype(jnp.int32)
            reduced[pi] = _chip_add(gW[pi][l], sums.pop((pi, l))[0], r, idx, reduced[pi], L, "grad_chip_add")

    for l in reversed(range(L)):
        x0, h1, proj, bexp, wbd, sk, ys, lt, ycn, x1, h2, gu, act = saved[l]
        keys = [(0, l + 1), (1, l + 1)]
        comm = chip_comm(keys) if l + 1 < L else None
        dgu, dx1, g_ffn[l], *r2 = _dact_dx(dx, Wd[l], gu, Wgu[l], x1, row(norm_ffn[l]), tm // 2, comm)
        if comm is not None:
            after_chip(keys, r2)
        gW[3][l] = _dw(act, dx, lambda t, s: (t, 0), D, 1, F // 2, tk // 2, "dw_down").reshape(N_CHIPS, F // N_CHIPS, D)
        gW[2][l] = _dw(h2, dgu, lambda t, s: (s // 2, t, s % 2), F // 2, N_CHIPS, D, tk, "dw_gate_up")
        gW[1][l] = _dw(ycn, dx1, lambda t, s: (t, 0), D, 1, D, tk, "dw_out").reshape(N_CHIPS, D // N_CHIPS, D)
        dya, dyb, dyc, dyd, g_out_gain[l] = _dycat(dx1, Wo[l], ys, row(mix_out_gain[l]), tm)
        keys = [(2, l), (3, l)]
        dpa, g_sgu_w[l], dbf, *r1 = _sgu_bwd(proj, sgu_w[l], bexp, dya, B, S, pair_comm(keys))
        after_pair(keys, r1)
        g_sgu_b[l] = dbf[:, ::HD].T
        dpb, dwbd, dsc = _pool_bwd(proj, wbd, row(pool_scale[l]), dyb, B, S)
        npg = len(POOL_WINDOWS)
        g_pool_w[l] = jnp.stack([dwbd[i * HD:(i + 1) * HD, i * HD:(i + 1) * HD] for i in range(npg)])
        g_pool_scale[l] = dsc[0]
        dcq, dckv, dsk, g_bias[l], *r2 = _swa_bwd(proj, sk, bias_tab, dyc, B, S, chip_comm([(3, l)]))
        after_chip([(3, l)], r2)
        g_sinks[l] = dsk[:, 0, 0] * float(BLK)
        ddq, ddk, ddv, *r2 = _sb_bwd(proj, lt, dyd, B, S, chip_comm([(2, l)]))
        after_chip([(2, l)], r2)
        dproj = [dpa, dpb, dcq, dckv, ddq, ddk, ddv]
        gW[0][l] = _dw_pieces(h1, dproj, w_in.shape[2], N_CHIPS, tk, "dw_in")
        keys = [(0, l), (1, l)]
        dx, g_mix[l], *r1 = _dx_norm_bwd(dproj, Win[l], x0, row(norm_mix[l]), dx1, tm, "dx_mix_exchange", pair_comm(keys))
        after_pair(keys, r1)
    grad_x = dx.reshape(B, S, D)

    after_chip(keys, _comm_only(chip_comm(keys), "grad_chip_exchange"))
    g_big = _pair_share(reduced, [g.shape[1] // 2 for g in reduced])

    g_rel_bias = _bias_reduce([g.reshape(4, -1) for g in g_bias], bucket).T
    small_g = [jnp.stack(g_sgu_w), jnp.stack(g_sgu_b), jnp.stack(g_pool_w), jnp.stack(g_pool_scale), jnp.stack(g_sinks),
               g_rel_bias, jnp.concatenate(g_out_gain), jnp.concatenate(g_mix), jnp.concatenate(g_ffn), g_final[0]]
    small_w = [sgu_w, sgu_b, pool_w, pool_scale, swa_sinks, rel_bias, mix_out_gain, norm_mix, norm_ffn, norm_final]
    small_m = [m_sgu_w, m_sgu_b, m_pool_w, m_pool_scale, m_swa_sinks, m_rel_bias, m_mix_out_gain, m_norm_mix, m_norm_ffn, m_norm_final]
    small_v = [v_sgu_w, v_sgu_b, v_pool_w, v_pool_scale, v_swa_sinks, v_rel_bias, v_mix_out_gain, v_norm_mix, v_norm_ffn, v_norm_final]
    shapes = [w.shape for w in small_w]
    bulk, fine = [small_g[0], small_g[2]], [small_g[i] for i in (1, 3, 4, 5, 6, 7, 8, 9)] + [loss_v[0, 0:1]]
    mine = [_pack(bulk), _pack(fine)]
    theirs = _comm_only(_SwapComm(mine), "small_pair_swap")
    slots_s = [_pair_sum_slot(a, b, kidx, dt) for a, b, dt in zip(mine, theirs, (bf16, f32))]
    shared = _comm_only(_SlotShareComm(slots_s), "small_chip_share")
    g_bulk = _unpack(_small_sum(shared[0]), [shapes[0], shapes[2]])
    *g_fine, loss = _unpack(_small_sum(shared[1]), [shapes[i] for i in (1, 3, 4, 5, 6, 7, 8, 9)] + [()])
    g_small = [g_bulk[0], g_fine[0], g_bulk[1]] + g_fine[1:]

    big_m = [m_w_in, m_w_out, m_w_gate_up, m_w_down]
    big_v = [v_w_in, v_w_out, v_w_gate_up, v_w_down]
    g_out, d_big, m_big, v_big = [], [], [], []
    for w, g, m, v in zip(big, g_big, big_m, big_v):
        two = lambda a: a.reshape(-1, a.shape[-1])
        rows = two(w).shape[0]
        cap = max(8, (1 << 20) // (4 * w.shape[-1]))
        tr = max(t for t in range(8, min(rows, cap) + 1, 8) if rows % t == 0)
        d2, m2, v2, g2 = _adamw(two(w), two(g), two(m), two(v), tr, "adamw_big", True)
        for lst, val in ((d_big, d2), (m_big, m2), (v_big, v2), (g_out, g2)):
            lst.append(val.reshape(w.shape))
    g_big = g_out

    g_small_packed = _pack(g_small)
    ds, ms, vs = _adamw(_pack(small_w), g_small_packed, _pack(small_m), _pack(small_v), g_small_packed.shape[0], "adamw_small")
    d_small, m_small, v_small = _unpack(ds, shapes), _unpack(ms, shapes), _unpack(vs, shapes)

    def order(bigs, smalls):
        return [bigs[0], bigs[1]] + list(smalls[0:9]) + [bigs[2], bigs[3], smalls[9]]

    return (loss, grad_x, *order(g_big, g_small), *order(d_big, d_small), *order(m_big, m_small), *order(v_big, v_small))
```

```python
import functools

import numpy as np
import jax
import jax.numpy as jnp
from jax import lax
from jax.experimental import pallas as pl
from jax.experimental.pallas import tpu as pltpu

f32 = jnp.float32
bf16 = jnp.bfloat16
_MXU = jnp.bfloat16

EPS = 1e-6
HD = 64
GW = 256
BLK = 128
SB_UNROLL = 2
SB_HEADS = 4
SB_CUT = -110.0
POOL_WINDOWS = (2, 4, 8, 16)
N_BUCKETS = 32
MAX_DISTANCE = 128
N_CHIPS = 4
VMEM_LIMIT = 48 * 1024 * 1024

ADAM_LR = 0.001
ADAM_B1 = 0.9
ADAM_B2 = 0.999
ADAM_EPS = 1e-08
ADAM_WD = 0.01
ADAM_STEP = 10

SDS = jax.ShapeDtypeStruct
MESH = pl.DeviceIdType.MESH
HIGHEST = lax.Precision.HIGHEST
RESIDENT = pl.Buffered(1)
NT = (((1,), (1,)), ((), ()))
TN = (((0,), (0,)), ((), ()))


def _cp(*sem):
    return pltpu.CompilerParams(dimension_semantics=sem if sem else None, vmem_limit_bytes=VMEM_LIMIT)


def _mx(v):
    return v.astype(_MXU)


def _iota(shape, dim):
    return lax.broadcasted_iota(jnp.int32, shape, dim)


def _split_dot(a, tri):
    hi = a.astype(bf16)
    lo = (a - hi.astype(f32)).astype(bf16)
    return jnp.dot(hi, tri, preferred_element_type=f32) + jnp.dot(lo, tri, preferred_element_type=f32)


def _rms(xv):
    return lax.rsqrt(jnp.mean(xv * xv, axis=-1, keepdims=True) + EPS)


def _hosted_call(body, steps, in_specs, out_specs, out_shape, scratch, args, name, comm):
    n_in, n_out = len(in_specs), len(out_specs)
    c_args, c_in, c_out, c_shapes, aliases, c_scratch = _host_specs(comm, n_in, n_out)
    step = lambda v: (lambda: pl.program_id(0) == v)
    return pl.pallas_call(
        _host(body, n_in, n_out, len(scratch), comm, step(0), step(steps - 1)), grid=(steps,),
        in_specs=list(in_specs) + c_in, out_specs=list(out_specs) + c_out, out_shape=list(out_shape) + c_shapes,
        input_output_aliases=aliases, scratch_shapes=list(scratch) + c_scratch,
        name=name if comm is None else name + "_comm", compiler_params=_cp("arbitrary"))(*args, *c_args)


def _norm_mm(x, gain, w, tm, comm=None):
    T, D = x.shape
    NS, _, ns = w.shape

    def body(x_ref, g_ref, w_ref, h_ref, o_ref):
        xv = x_ref[...]
        h = (xv * _rms(xv) * g_ref[...]).astype(bf16)
        h_ref[...] = h
        for s in range(NS):
            o_ref[:, s * ns:(s + 1) * ns] = jnp.dot(_mx(h), w_ref[s], preferred_element_type=f32).astype(bf16)

    return _hosted_call(
        body, T // tm,
        [pl.BlockSpec((tm, D), lambda i: (i, 0)),
         pl.BlockSpec((1, D), lambda i: (0, 0)),
         pl.BlockSpec((NS, D, ns), lambda i: (0, 0, 0), pipeline_mode=RESIDENT)],
        [pl.BlockSpec((tm, D), lambda i: (i, 0)), pl.BlockSpec((tm, NS * ns), lambda i: (i, 0))],
        [SDS((T, D), bf16), SDS((T, NS * ns), bf16)], [], (x, gain, w), "norm_mm_in", comm)


def _mix_out_swiglu(ys, gain_mix, wo, x, gain_ffn, w, tm, comm=None):
    T, D = x.shape
    NS, _, ns = w.shape
    half = NS // 2

    def body(ya, yb, yc, yd, gm_ref, wo_ref, x_ref, gf_ref, w_ref, yn_ref, x1_ref, h_ref, gu_ref, a_ref):
        parts = []
        for m, r in enumerate((ya, yb, yc, yd)):
            y = r[...].astype(f32)
            parts.append((y * _rms(y) * gm_ref[:, m * GW:(m + 1) * GW]).astype(bf16))
        yn = jnp.concatenate(parts, axis=1)
        yn_ref[...] = yn
        x1 = x_ref[...] + jnp.dot(_mx(yn), wo_ref[...], preferred_element_type=f32)
        x1_ref[...] = x1
        hb = (x1 * _rms(x1) * gf_ref[...]).astype(bf16)
        h_ref[...] = hb
        h = _mx(hb)
        for s in range(half):
            cols = slice(s * ns, (s + 1) * ns)
            g = jnp.dot(h, w_ref[s], preferred_element_type=f32)
            u = jnp.dot(h, w_ref[s + half], preferred_element_type=f32)
            gu_ref[0, :, cols] = g.astype(bf16)
            gu_ref[1, :, cols] = u.astype(bf16)
            a_ref[:, cols] = (jax.nn.silu(g) * u).astype(bf16)

    yspec = pl.BlockSpec((tm, GW), lambda i: (i, 0))
    row, tile = pl.BlockSpec((1, D), lambda i: (0, 0)), pl.BlockSpec((tm, D), lambda i: (i, 0))
    return _hosted_call(
        body, T // tm,
        [yspec, yspec, yspec, yspec, row,
         pl.BlockSpec((D, D), lambda i: (0, 0), pipeline_mode=RESIDENT), tile, row,
         pl.BlockSpec((NS, D, ns), lambda i: (0, 0, 0), pipeline_mode=RESIDENT)],
        [tile, tile, tile, pl.BlockSpec((2, tm, half * ns), lambda i: (0, i, 0)),
         pl.BlockSpec((tm, half * ns), lambda i: (i, 0))],
        [SDS((T, D), bf16), SDS((T, D), f32), SDS((T, D), bf16), SDS((2, T, half * ns), bf16), SDS((T, half * ns), bf16)],
        [], (*ys, gain_mix, wo, x, gain_ffn, w), "mix_out_swiglu", comm)


def _mm_res(a, w, x, tm, comm=None):
    T, D = x.shape
    K = a.shape[1]

    def body(a_ref, w_ref, x_ref, o_ref):
        o_ref[...] = x_ref[...] + jnp.dot(_mx(a_ref[...]), w_ref[...], preferred_element_type=f32)

    return _hosted_call(
        body, T // tm,
        [pl.BlockSpec((tm, K), lambda i: (i, 0)),
         pl.BlockSpec((K, D), lambda i: (0, 0), pipeline_mode=RESIDENT),
         pl.BlockSpec((tm, D), lambda i: (i, 0))],
        [pl.BlockSpec((tm, D), lambda i: (i, 0))], [SDS((T, D), f32)], [], (a, w, x), "mm_res_down", comm)


def _final_loss(x, gain, tgt, tm):
    T, D = x.shape

    def body(x_ref, g_ref, t_ref, dx_ref, dg_ref, l_ref):
        @pl.when(pl.program_id(0) == 0)
        def _():
            dg_ref[...] = jnp.zeros_like(dg_ref)
            l_ref[...] = jnp.zeros_like(l_ref)
        xv = x_ref[...]
        g = g_ref[...]
        r = _rms(xv)
        xh = xv * r
        err = xh * g - t_ref[...]
        l_ref[...] += 0.5 * jnp.sum(jnp.mean(err * err, axis=-1, keepdims=True), axis=0, keepdims=True)
        dy = err * (1.0 / D)
        dg_ref[...] += jnp.sum(dy * xh, axis=0, keepdims=True)
        dxh = dy * g
        dx_ref[...] = r * (dxh - xh * jnp.mean(dxh * xh, axis=-1, keepdims=True))

    return pl.pallas_call(
        body, grid=(T // tm,),
        in_specs=[pl.BlockSpec((tm, D), lambda i: (i, 0)),
                  pl.BlockSpec((1, D), lambda i: (0, 0)),
                  pl.BlockSpec((tm, D), lambda i: (i, 0))],
        out_specs=[pl.BlockSpec((tm, D), lambda i: (i, 0)),
                   pl.BlockSpec((1, D), lambda i: (0, 0)),
                   pl.BlockSpec((1, BLK), lambda i: (0, 0))],
        out_shape=[SDS((T, D), f32), SDS((1, D), f32), SDS((1, BLK), f32)],
        name="final_loss", compiler_params=_cp("arbitrary"))(x, gain, tgt)


def _dact_dx(dx, wd, gu, w, x, gain, tm, comm=None):
    T, D = dx.shape
    F = wd.shape[0]
    NS, _, ns = w.shape
    half = NS // 2

    def body(dx_ref, wd_ref, gu_ref, w_ref, x_ref, g_ref, dgu_ref, dx1_ref, dg_ref):
        @pl.when(pl.program_id(0) == 0)
        def _():
            dg_ref[...] = jnp.zeros_like(dg_ref)
        dxv = dx_ref[...]
        dxb = _mx(dxv)
        dh = None
        for s in range(half):
            cols = slice(s * ns, (s + 1) * ns)
            da = lax.dot_general(dxb, wd_ref[s * ns:(s + 1) * ns, :], NT, preferred_element_type=f32)
            g = gu_ref[0, :, cols].astype(f32)
            u = gu_ref[1, :, cols].astype(f32)
            sg = jax.nn.sigmoid(g)
            dgs = (da * u * (sg * (1.0 + g * (1.0 - sg)))).astype(bf16)
            dus = (da * (g * sg)).astype(bf16)
            dgu_ref[0, :, cols] = dgs
            dgu_ref[1, :, cols] = dus
            part = (lax.dot_general(_mx(dgs), w_ref[s], NT, preferred_element_type=f32)
                    + lax.dot_general(_mx(dus), w_ref[s + half], NT, preferred_element_type=f32))
            dh = part if dh is None else dh + part
        xv = x_ref[...]
        r = _rms(xv)
        xh = xv * r
        dg_ref[...] += jnp.sum(dh * xh, axis=0, keepdims=True)
        dxh = dh * g_ref[...]
        dx1_ref[...] = dxv + r * (dxh - xh * jnp.mean(dxh * xh, axis=-1, keepdims=True))

    return _hosted_call(
        body, T // tm,
        [pl.BlockSpec((tm, D), lambda i: (i, 0)),
         pl.BlockSpec((F, D), lambda i: (0, 0), pipeline_mode=RESIDENT),
         pl.BlockSpec((2, tm, F), lambda i: (0, i, 0)),
         pl.BlockSpec((NS, D, ns), lambda i: (0, 0, 0), pipeline_mode=RESIDENT),
         pl.BlockSpec((tm, D), lambda i: (i, 0)),
         pl.BlockSpec((1, D), lambda i: (0, 0))],
        [pl.BlockSpec((2, tm, F), lambda i: (0, i, 0)), pl.BlockSpec((tm, D), lambda i: (i, 0)),
         pl.BlockSpec((1, D), lambda i: (0, 0))],
        [SDS((2, T, F), bf16), SDS((T, D), f32), SDS((1, D), f32)], [], (dx, wd, gu, w, x, gain), "dact_dx", comm)


def _dw(a, b, b_map, ns, NS, tka, tk, name):
    T, Ka = a.shape
    b_block = (tk, ns) if b.ndim == 2 else (1, tk, ns)

    def body(a_ref, b_ref, o_ref):
        bv = b_ref[...] if b.ndim == 2 else b_ref[0]
        part = lax.dot_general(_mx(a_ref[...]), _mx(bv), TN, preferred_element_type=f32)

        @pl.when(pl.program_id(2) == 0)
        def _():
            o_ref[0] = part

        @pl.when(pl.program_id(2) > 0)
        def _():
            o_ref[0] += part

    return pl.pallas_call(
        body, grid=(NS, Ka // tka, T // tk),
        in_specs=[pl.BlockSpec((tk, tka), lambda s, k, t: (t, k)),
                  pl.BlockSpec(b_block, lambda s, k, t: b_map(t, s))],
        out_specs=pl.BlockSpec((1, tka, ns), lambda s, k, t: (s, k, 0)),
        out_shape=SDS((NS, Ka, ns), f32),
        name=name, compiler_params=_cp("parallel", "parallel", "arbitrary"))(a, b)


def _dw_pieces(a, pieces, ns, NS, tk, name):
    T, Ka = a.shape
    n = len(pieces)

    def body(*refs):
        a_ref, b_refs, o_ref = refs[0], refs[1:1 + n], refs[1 + n]
        full = jnp.concatenate([r[...] for r in b_refs], axis=1)
        av = _mx(a_ref[...])
        parts = [lax.dot_general(av, _mx(full[:, s * ns:(s + 1) * ns]), TN, preferred_element_type=f32) for s in range(NS)]

        @pl.when(pl.program_id(0) == 0)
        def _():
            for s in range(NS):
                o_ref[s] = parts[s]

        @pl.when(pl.program_id(0) > 0)
        def _():
            for s in range(NS):
                o_ref[s] += parts[s]

    return pl.pallas_call(
        body, grid=(T // tk,),
        in_specs=[pl.BlockSpec((tk, Ka), lambda t: (t, 0))] + [pl.BlockSpec((tk, p.shape[1]), lambda t: (t, 0)) for p in pieces],
        out_specs=pl.BlockSpec((NS, Ka, ns), lambda t: (0, 0, 0)),
        out_shape=SDS((NS, Ka, ns), f32),
        name=name, compiler_params=_cp("arbitrary"))(a, *pieces)


def _dx_norm_bwd(pieces, w, x, gain, dxin, tm, name, comm=None):
    T, D = x.shape
    NS, _, ns = w.shape
    n_dy = len(pieces)

    def body(*refs):
        dy_refs = refs[:n_dy]
        w_ref, x_ref, g_ref, dxin_ref, dx_ref, dg_ref = refs[n_dy:]

        @pl.when(pl.program_id(0) == 0)
        def _():
            dg_ref[...] = jnp.zeros_like(dg_ref)
        full = jnp.concatenate([r[...] for r in dy_refs], axis=1)
        dh = None
        for s in range(NS):
            part = lax.dot_general(_mx(full[:, s * ns:(s + 1) * ns]), w_ref[s], NT, preferred_element_type=f32)
            dh = part if dh is None else dh + part
        xv = x_ref[...]
        r = _rms(xv)
        xh = xv * r
        dg_ref[...] += jnp.sum(dh * xh, axis=0, keepdims=True)
        dxh = dh * g_ref[...]
        dx_ref[...] = dxin_ref[...] + r * (dxh - xh * jnp.mean(dxh * xh, axis=-1, keepdims=True))

    dy_specs = [pl.BlockSpec((tm, p.shape[1]), lambda i: (i, 0)) for p in pieces]
    return _hosted_call(
        body, T // tm,
        dy_specs + [pl.BlockSpec((NS, D, ns), lambda i: (0, 0, 0), pipeline_mode=RESIDENT),
                    pl.BlockSpec((tm, D), lambda i: (i, 0)),
                    pl.BlockSpec((1, D), lambda i: (0, 0)),
                    pl.BlockSpec((tm, D), lambda i: (i, 0))],
        [pl.BlockSpec((tm, D), lambda i: (i, 0)), pl.BlockSpec((1, D), lambda i: (0, 0))],
        [SDS((T, D), f32), SDS((1, D), f32)], [], (*pieces, w, x, gain, dxin), name, comm)


def _dycat(dx, w, ys, gain, tm, comm=None):
    T, D = dx.shape

    def body(dx_ref, w_ref, ya, yb, yc, yd, g_ref, da, db, dc, dd, dg_ref):
        @pl.when(pl.program_id(0) == 0)
        def _():
            dg_ref[...] = jnp.zeros_like(dg_ref)
        dyn = lax.dot_general(_mx(dx_ref[...]), w_ref[...], NT, preferred_element_type=f32)
        for m, (r, o) in enumerate(((ya, da), (yb, db), (yc, dc), (yd, dd))):
            cols = slice(m * GW, (m + 1) * GW)
            y = r[...].astype(f32)
            rs = _rms(y)
            yh = y * rs
            d = dyn[:, cols]
            dg_ref[:, cols] += jnp.sum(d * yh, axis=0, keepdims=True)
            dyh = d * g_ref[:, cols]
            o[...] = (rs * (dyh - yh * jnp.mean(dyh * yh, axis=-1, keepdims=True))).astype(bf16)

    yspec = pl.BlockSpec((tm, GW), lambda i: (i, 0))
    return _hosted_call(
        body, T // tm,
        [pl.BlockSpec((tm, D), lambda i: (i, 0)),
         pl.BlockSpec((D, D), lambda i: (0, 0), pipeline_mode=RESIDENT),
         yspec, yspec, yspec, yspec,
         pl.BlockSpec((1, D), lambda i: (0, 0))],
        [yspec, yspec, yspec, yspec, pl.BlockSpec((1, D), lambda i: (0, 0))],
        [SDS((T, GW), bf16)] * 4 + [SDS((1, D), f32)], [], (dx, w, *ys, gain), "dycat", comm)


def _sgu_consts():
    r, c = _iota((GW, GW), 0), _iota((GW, GW), 1)
    seg = (r // HD == c // HD).astype(f32)
    tr, ts = _iota((BLK, BLK), 0), _iota((BLK, BLK), 1)
    causal = ts <= tr
    lane_head = _iota((BLK, GW), 1) // HD
    return seg, causal, lane_head


def _split3_dot(a, ones):
    hi = a.astype(bf16)
    r1 = a - hi.astype(f32)
    mid = r1.astype(bf16)
    lo = (r1 - mid.astype(f32)).astype(bf16)
    dot = functools.partial(jnp.dot, preferred_element_type=f32)
    return dot(hi, ones) + dot(mid, ones) + dot(lo, ones)


def _sgu_chunks(aus, avs, w, bexp, consts):
    seg, causal, lane_head = consts
    segb = seg.astype(bf16)
    nh = GW // HD
    vs = [jax.nn.gelu(av) for av in avs]
    mus = [_split3_dot(v, segb) * (1.0 / HD) for v in vs]
    vcs = [v - mu for v, mu in zip(vs, mus)]
    vars_ = [_split3_dot(vc * vc, segb) * (1.0 / HD) for vc in vcs]
    vns = [_mx(vc * lax.rsqrt(var + EPS)) for vc, var in zip(vcs, vars_)]
    whs = [_mx(jnp.where(causal, w[h], 0.0)) for h in range(nh)]
    mixes = [[jnp.dot(whs[h], vn, preferred_element_type=f32) for h in range(nh)] for vn in vns]
    out = []
    for au, ms in zip(aus, mixes):
        mix = bexp
        for h in range(nh):
            mix = mix + jnp.where(lane_head == h, ms[h], 0.0)
        out.append(jax.nn.gelu(au) * mix)
    return out


def _sgu_group(S):
    nc = S // BLK
    return 4 if nc % 4 == 0 else (2 if nc % 2 == 0 else 1)


def _sgu_fwd(proj, w, bexp, B, S):
    G = _sgu_group(S)

    def body(au_ref, av_ref, w_ref, b_ref, y_ref):
        consts = _sgu_consts()
        wv, bv = w_ref[...], b_ref[...]

        def group(n, c):
            rows = [pl.ds(pl.multiple_of((n * G + j) * BLK, BLK), BLK) for j in range(G)]
            ys = _sgu_chunks([au_ref[r, :].astype(f32) for r in rows], [av_ref[r, :].astype(f32) for r in rows],
                             wv, bv, consts)
            for r, y in zip(rows, ys):
                y_ref[r, :] = y.astype(bf16)
            return c
        lax.fori_loop(0, S // BLK // G, group, 0)

    return pl.pallas_call(
        body, grid=(B,),
        in_specs=[pl.BlockSpec((S, GW), lambda b: (b, 0)),
                  pl.BlockSpec((S, GW), lambda b: (b, 1)),
                  pl.BlockSpec((GW // HD, BLK, BLK), lambda b: (0, 0, 0)),
                  pl.BlockSpec((BLK, GW), lambda b: (0, 0))],
        out_specs=pl.BlockSpec((S, GW), lambda b: (b, 0)),
        out_shape=SDS((B * S, GW), bf16),
        name="sgu_fwd", compiler_params=_cp("parallel"))(proj, proj, w, bexp)


def _sgu_bwd(proj, w, bexp, dy, B, S, comm=None):
    def body(au_ref, av_ref, w_ref, b_ref, dy_ref, dp_ref, dw_ref, db_ref):
        @pl.when(pl.program_id(0) == 0)
        def _():
            dw_ref[...] = jnp.zeros_like(dw_ref)
            db_ref[...] = jnp.zeros_like(db_ref)
        consts = _sgu_consts()
        wv, bv = w_ref[...], b_ref[...]
        fn = lambda aus, avs, ww, bb: _sgu_chunks(aus, avs, ww, bb, consts)
        G = _sgu_group(S)

        def group(n, carry):
            dw_acc, db_acc = carry
            rows = [pl.ds(pl.multiple_of((n * G + j) * BLK, BLK), BLK) for j in range(G)]
            _, vjp = jax.vjp(fn, [au_ref[r, :].astype(f32) for r in rows], [av_ref[r, :].astype(f32) for r in rows], wv, bv)
            daus, davs, dwc, dbc = vjp([dy_ref[r, :].astype(f32) for r in rows])
            for r, dau, dav in zip(rows, daus, davs):
                dp_ref[r, 0:GW] = dau.astype(bf16)
                dp_ref[r, GW:2 * GW] = dav.astype(bf16)
            return dw_acc + dwc, db_acc + dbc
        dw_acc, db_acc = lax.fori_loop(0, S // BLK // G, group, (jnp.zeros(wv.shape, f32), jnp.zeros(bv.shape, f32)))
        dw_ref[...] += dw_acc
        db_ref[...] += jnp.dot(db_acc, consts[0], precision=HIGHEST, preferred_element_type=f32)

    return _hosted_call(
        body, B,
        [pl.BlockSpec((S, GW), lambda b: (b, 0)),
         pl.BlockSpec((S, GW), lambda b: (b, 1)),
         pl.BlockSpec((GW // HD, BLK, BLK), lambda b: (0, 0, 0)),
         pl.BlockSpec((BLK, GW), lambda b: (0, 0)),
         pl.BlockSpec((S, GW), lambda b: (b, 0))],
        [pl.BlockSpec((S, 2 * GW), lambda b: (b, 0)),
         pl.BlockSpec((GW // HD, BLK, BLK), lambda b: (0, 0, 0)),
         pl.BlockSpec((BLK, GW), lambda b: (0, 0))],
        [SDS((B * S, 2 * GW), bf16), SDS((GW // HD, BLK, BLK), f32), SDS((BLK, GW), f32)], [],
        (proj, proj, w, bexp, dy), "sgu_bwd", comm)


def _pool_parts(p):
    n = p.shape[0]
    r = _iota(p.shape, 0)
    lg = _iota(p.shape, 1) // HD

    def sh(v, k):
        return jnp.where(r >= k, pltpu.roll(v, k, 0), 0.0)
    s2 = p + sh(p, 1)
    s4 = s2 + sh(s2, 2)
    s8 = s4 + sh(s4, 4)
    s16 = s8 + sh(s8, 8)
    ws = jnp.where(lg == 0, s2, jnp.where(lg == 1, s4, jnp.where(lg == 2, s8, s16)))
    wlen = jnp.where(lg == 0, 2, jnp.where(lg == 1, 4, jnp.where(lg == 2, 8, 16)))
    cnt = jnp.minimum(r + 1, wlen).astype(f32)
    del n
    return ws / cnt - p, cnt, lg


def _pool_fwd(proj, wbd, scale, B, S):
    def body(p_ref, w_ref, s_ref, y_ref):
        y, _, _ = _pool_parts(p_ref[...].astype(f32))
        y_ref[...] = (jnp.dot(_mx(y), _mx(w_ref[...]), preferred_element_type=f32) * s_ref[...]).astype(bf16)

    return pl.pallas_call(
        body, grid=(B,),
        in_specs=[pl.BlockSpec((S, GW), lambda b: (b, 2)),
                  pl.BlockSpec((GW, GW), lambda b: (0, 0)),
                  pl.BlockSpec((1, GW), lambda b: (0, 0))],
        out_specs=pl.BlockSpec((S, GW), lambda b: (b, 0)),
        out_shape=SDS((B * S, GW), bf16),
        name="pool_fwd", compiler_params=_cp("parallel"))(proj, wbd, scale)


def _pool_bwd(proj, wbd, scale, dy, B, S):
    def body(p_ref, w_ref, s_ref, dy_ref, dp_ref, dw_ref, ds_ref):
        @pl.when(pl.program_id(0) == 0)
        def _():
            dw_ref[...] = jnp.zeros_like(dw_ref)
            ds_ref[...] = jnp.zeros_like(ds_ref)
        y, cnt, lg = _pool_parts(p_ref[...].astype(f32))
        wv = _mx(w_ref[...])
        z = jnp.dot(_mx(y), wv, preferred_element_type=f32)
        dout = dy_ref[...].astype(f32)
        ds_ref[...] += jnp.sum(dout * z, axis=0, keepdims=True)
        dz = _mx(dout * s_ref[...])
        dw_ref[...] += lax.dot_general(_mx(y), dz, TN, preferred_element_type=f32)
        dyv = lax.dot_general(dz, wv, NT, preferred_element_type=f32)
        n = dyv.shape[0]
        r = _iota(dyv.shape, 0)

        def ush(v, k):
            return jnp.where(r < n - k, pltpu.roll(v, n - k, 0), 0.0)
        gq = dyv / cnt
        a2 = gq + ush(gq, 1)
        a4 = a2 + ush(a2, 2)
        a8 = a4 + ush(a4, 4)
        a16 = a8 + ush(a8, 8)
        adj = jnp.where(lg == 0, a2, jnp.where(lg == 1, a4, jnp.where(lg == 2, a8, a16)))
        dp_ref[...] = (adj - dyv).astype(bf16)

    return pl.pallas_call(
        body, grid=(B,),
        in_specs=[pl.BlockSpec((S, GW), lambda b: (b, 2)),
                  pl.BlockSpec((GW, GW), lambda b: (0, 0)),
                  pl.BlockSpec((1, GW), lambda b: (0, 0)),
                  pl.BlockSpec((S, GW), lambda b: (b, 0))],
        out_specs=[pl.BlockSpec((S, GW), lambda b: (b, 0)),
                   pl.BlockSpec((GW, GW), lambda b: (0, 0)),
                   pl.BlockSpec((1, GW), lambda b: (0, 0))],
        out_shape=[SDS((B * S, GW), bf16), SDS((GW, GW), f32), SDS((1, GW), f32)],
        name="pool_bwd", compiler_params=_cp("arbitrary"))(proj, wbd, scale, dy)


def _t5_bucket_table():
    dist = (np.arange(BLK)[:, None] + BLK) - np.arange(2 * BLK)[None, :]
    d = np.clip(dist, 0, BLK - 1)
    max_exact = N_BUCKETS // 2
    df = np.maximum(d, 1).astype(np.float32)
    large = max_exact + (np.log(df / max_exact) / np.float32(np.log(MAX_DISTANCE / max_exact))
                         * (N_BUCKETS - max_exact)).astype(np.int32)
    large = np.minimum(large, N_BUCKETS - 1)
    return np.where(d < max_exact, d, large).astype(np.int32)


def _swa_blocks(qs, kx, vx, sinks, biases, n):
    G = len(qs)
    heads = [(p, g) for p in range(2) for g in range(2)]
    ri, ci = _iota((BLK, BLK), 0), _iota((BLK, BLK), 1)
    qi, ki = _iota((BLK, 2 * BLK), 0), _iota((BLK, 2 * BLK), 1)
    dist = qi + BLK - ki
    band = (dist >= 0) & (dist < BLK)
    masks = [band & ((ki >= BLK) | (n > 0))] + [band] * (G - 1)
    kb, vb = _mx(kx), _mx(vx)
    qsel = [[None] * 4 for _ in range(G)]
    vs = []
    for h, (p, g) in enumerate(heads):
        selq = ((ri - g * HD == ci - p * HD) & (ri >= g * HD) & (ri < (g + 1) * HD)).astype(_MXU)
        selv = ((ci - g * HD == ri - p * HD) & (ci >= g * HD) & (ci < (g + 1) * HD)).astype(_MXU)
        for b in range(G):
            qsel[b][h] = _mx(jnp.dot(_mx(qs[b][p]), selq, preferred_element_type=f32))
        vs.append(_mx(jnp.dot(vb, selv, preferred_element_type=f32)))
    zs = [[lax.dot_general(qsel[b][h], kb[b * BLK:(b + 2) * BLK], NT, preferred_element_type=f32) * (HD ** -0.5)
           for h in range(4)] for b in range(G)]
    prs = [[None] * 4 for _ in range(G)]
    for b in range(G):
        for h in range(4):
            z = jnp.where(masks[b], zs[b][h] + biases[h], -1e30)
            s = jnp.mean(sinks[h], axis=-1, keepdims=True)
            m = jnp.maximum(jnp.max(z, axis=-1, keepdims=True), s)
            e = jnp.exp(z - m)
            prs[b][h] = _mx(e / (jnp.sum(e, axis=-1, keepdims=True) + jnp.exp(s - m)))
    outs = [[jnp.dot(prs[b][h], vs[h][b * BLK:(b + 2) * BLK], preferred_element_type=f32) for h in range(4)]
            for b in range(G)]
    return [[o[0] + o[1], o[2] + o[3]] for o in outs]


def _swa_group(S, most=2):
    g = most
    while (S // BLK) % g:
        g //= 2
    return g


def _swa_rows(n, G):
    blk = lambda j: pl.ds(pl.multiple_of(j * BLK, BLK), BLK)
    return [blk(jnp.maximum(n - 1, 0))] + [blk(n + b) for b in range(G)]


def _swa_fwd(proj, sinks, bias, B, S, comm=None):
    G = _swa_group(S, 4)

    def body(q_ref, kv_ref, s_ref, b_ref, y_ref):
        def group(i, c):
            n = i * G
            rows = _swa_rows(n, G)
            kx = jnp.concatenate([kv_ref[r, 0:BLK] for r in rows], axis=0).astype(f32)
            vx = jnp.concatenate([kv_ref[r, BLK:2 * BLK] for r in rows], axis=0).astype(f32)
            qs = [[q_ref[r, 0:BLK].astype(f32), q_ref[r, BLK:2 * BLK].astype(f32)] for r in rows[1:]]
            outs = _swa_blocks(qs, kx, vx, [s_ref[h] for h in range(4)], [b_ref[h] for h in range(4)], n)
            for r, (o0, o1) in zip(rows[1:], outs):
                y_ref[r, 0:BLK] = o0.astype(bf16)
                y_ref[r, BLK:2 * BLK] = o1.astype(bf16)
            return c
        lax.fori_loop(0, S // BLK // G, group, 0)

    return _hosted_call(
        body, B,
        [pl.BlockSpec((S, GW), lambda b: (b, 3)),
         pl.BlockSpec((S, GW), lambda b: (b, 4)),
         pl.BlockSpec((4, 1, BLK), lambda b: (0, 0, 0)),
         pl.BlockSpec((4, BLK, 2 * BLK), lambda b: (0, 0, 0))],
        [pl.BlockSpec((S, GW), lambda b: (b, 0))], [SDS((B * S, GW), bf16)], [],
        (proj, proj, sinks, bias), "swa_fwd", comm)


def _swa_bwd(proj, sinks, bias, dy, B, S, comm=None):
    def body(q_ref, kv_ref, s_ref, b_ref, dy_ref, dq_ref, dkv_ref, ds_ref, db_ref, acc_ref):
        @pl.when(pl.program_id(0) == 0)
        def _():
            ds_ref[...] = jnp.zeros_like(ds_ref)
            db_ref[...] = jnp.zeros_like(db_ref)
        acc_ref[...] = jnp.zeros_like(acc_ref)

        G = _swa_group(S)

        def group(i, c):
            n = i * G
            rows = _swa_rows(n, G)
            kx = jnp.concatenate([kv_ref[r, 0:BLK] for r in rows], axis=0).astype(f32)
            vx = jnp.concatenate([kv_ref[r, BLK:2 * BLK] for r in rows], axis=0).astype(f32)
            qs = [[q_ref[r, 0:BLK].astype(f32), q_ref[r, BLK:2 * BLK].astype(f32)] for r in rows[1:]]
            dos = [[dy_ref[r, 0:BLK].astype(f32), dy_ref[r, BLK:2 * BLK].astype(f32)] for r in rows[1:]]
            fn = functools.partial(_swa_blocks, n=n)
            _, vjp = jax.vjp(fn, qs, kx, vx, [s_ref[h] for h in range(4)], [b_ref[h] for h in range(4)])
            dqs, dkx, dvx, dss, dbs = vjp(dos)
            for r, (dq0, dq1) in zip(rows[1:], dqs):
                dq_ref[r, 0:BLK] = dq0.astype(bf16)
                dq_ref[r, BLK:2 * BLK] = dq1.astype(bf16)
            for h in range(4):
                ds_ref[h] += dss[h]
                db_ref[h] += dbs[h]
            for j, r in enumerate(rows):
                acc_ref[r, 0:BLK] += dkx[j * BLK:(j + 1) * BLK]
                acc_ref[r, BLK:2 * BLK] += dvx[j * BLK:(j + 1) * BLK]
            return c
        lax.fori_loop(0, S // BLK // G, group, 0)
        dkv_ref[...] = acc_ref[...].astype(bf16)

    c_args, c_in, c_out, c_shapes, aliases, c_scratch = _host_specs(comm, 5, 4)
    step = lambda v: (lambda: pl.program_id(0) == v)
    return pl.pallas_call(
        _host(body, 5, 4, 1, comm, step(0), step(B - 1)), grid=(B,),
        in_specs=[pl.BlockSpec((S, GW), lambda b: (b, 3)),
                  pl.BlockSpec((S, GW), lambda b: (b, 4)),
                  pl.BlockSpec((4, 1, BLK), lambda b: (0, 0, 0)),
                  pl.BlockSpec((4, BLK, 2 * BLK), lambda b: (0, 0, 0)),
                  pl.BlockSpec((S, GW), lambda b: (b, 0))] + c_in,
        out_specs=[pl.BlockSpec((S, GW), lambda b: (b, 0)),
                   pl.BlockSpec((S, GW), lambda b: (b, 0)),
                   pl.BlockSpec((4, 1, BLK), lambda b: (0, 0, 0)),
                   pl.BlockSpec((4, BLK, 2 * BLK), lambda b: (0, 0, 0))] + c_out,
        out_shape=[SDS((B * S, GW), bf16), SDS((B * S, GW), bf16), SDS((4, 1, BLK), f32),
                   SDS((4, BLK, 2 * BLK), f32)] + c_shapes,
        input_output_aliases=aliases, scratch_shapes=[pltpu.VMEM((S, GW), f32)] + c_scratch,
        name="swa_bwd" if comm is None else "swa_bwd_exchange",
        compiler_params=_cp("arbitrary"))(proj, proj, sinks, bias, dy, *c_args)


def _log1m_parts(z):
    t = jnp.exp(-jnp.abs(z))
    return jnp.minimum(-z, 0.0) - jnp.log(1.0 + t), t


def _log1m(z):
    return _log1m_parts(z)[0]


def _sigmoid_from(z, t):
    return jnp.where(z >= 0.0, 1.0, t) / (1.0 + t)


def _sb_consts(tri):
    r2, c2 = _iota((2 * BLK, 2 * BLK), 0), _iota((2 * BLK, 2 * BLK), 1)
    tri2 = (tri(r2, c2) & (r2 // BLK == c2 // BLK)).astype(bf16)
    ri, ci = _iota((BLK, 2 * BLK), 0), _iota((BLK, 2 * BLK), 1)
    strict2 = (ci % BLK) < ri
    head0 = _iota((BLK, BLK), 1) < HD
    return tri2, strict2, head0


def _sb_stack_kv(k_ref, v_ref, kst_ref, vst_ref, head0, nb):
    def one(kb, c):
        krows = pl.ds(pl.multiple_of(kb * BLK, BLK), BLK)
        for p in range(2):
            for src, dst in ((k_ref, kst_ref), (v_ref, vst_ref)):
                t = src[krows, p * BLK:(p + 1) * BLK]
                dst[p, kb] = _mx(jnp.concatenate([jnp.where(head0, t, 0.0), jnp.where(head0, 0.0, t)], axis=0))
        return c
    lax.fori_loop(0, nb, one, 0)


def _sb_load_kv(kst_ref, vst_ref, kb):
    return [kst_ref[p, kb] for p in range(2)], [vst_ref[p, kb] for p in range(2)]


def _two_halves(a, b):
    return jnp.concatenate([jnp.broadcast_to(a, (BLK, BLK)), jnp.broadcast_to(b, (BLK, BLK))], axis=1)


def _half_sums(t):
    return jnp.sum(t[:, :BLK], axis=-1, keepdims=True), jnp.sum(t[:, BLK:], axis=-1, keepdims=True)


def _sb_fwd(proj, B, S, comm=None):
    def body(q_ref, k_ref, v_ref, y_ref, lt_ref, kst_ref, vst_ref):
        ci = _iota((BLK, BLK), 1)
        above2, strict2, head0 = _sb_consts(lambda r, c: r > c)
        _sb_stack_kv(k_ref, v_ref, kst_ref, vst_ref, head0, S // BLK)

        def step(qs, kbs, diags, carry):
            U = range(len(kbs))
            ok = [None if diags[u] else kbs[u] >= 0 for u in U]
            kv = [_sb_load_kv(kst_ref, vst_ref, jnp.maximum(kb, 0)) for kb in kbs]
            zs = [[lax.dot_general(qs[p], kks[p], NT, preferred_element_type=f32) for p in range(2)] for kks, _ in kv]
            Ls = [[jnp.where(strict2, _log1m(z), 0.0) if diags[u] else _log1m(z) for z in zs[u]] for u in U]
            tails = [[_split_dot(L, above2) for L in Lu] for Lu in Ls]
            carry = list(carry)
            for u in U:
                for p in range(2):
                    R0, R1, acc = carry[3 * p:3 * p + 3]
                    w = jnp.exp(zs[u][p] + Ls[u][p] + tails[u][p] + _two_halves(R0, R1))
                    s0, s1 = _half_sums(Ls[u][p])
                    if diags[u]:
                        w = jnp.where(strict2, w, 0.0)
                    else:
                        w, s0, s1 = (jnp.where(ok[u], t, 0.0) for t in (w, s0, s1))
                    acc = acc + jnp.dot(_mx(w), kv[u][1][p], preferred_element_type=f32)
                    carry[3 * p:3 * p + 3] = [R0 + s0, R1 + s1, acc]
            return tuple(carry)

        def qblock(n, c):
            qrows = pl.ds(pl.multiple_of(n * BLK, BLK), BLK)
            qs = [_mx(q_ref[qrows, p * BLK:(p + 1) * BLK] * (HD ** -0.5)) for p in range(2)]
            z1, z2 = jnp.zeros((BLK, 1), f32), jnp.zeros((BLK, BLK), f32)
            near = [n - 1 - u for u in range(SB_UNROLL)]
            carry = step(qs, [n] + near, [True] + [False] * SB_UNROLL, (z1, z1, z2, z1, z1, z2))
            far = jnp.maximum(n - SB_UNROLL, 0)
            trips = (far + SB_UNROLL - 1) // SB_UNROLL

            def live(st):
                worst = jnp.maximum(jnp.maximum(st[1], st[2]), jnp.maximum(st[4], st[5]))
                return (st[0] < trips) & (jnp.max(worst) > SB_CUT)

            def trip(st):
                i = st[0]
                kbs = [far - 1 - SB_UNROLL * i - u for u in range(SB_UNROLL)]
                return (i + 1,) + step(qs, kbs, [False] * SB_UNROLL, st[1:])
            done, *res = lax.while_loop(live, trip, (jnp.int32(0),) + carry)
            lt = jnp.where(ci == SB_HEADS, done.astype(f32), 0.0)
            for p in range(2):
                y_ref[qrows, p * BLK:(p + 1) * BLK] = res[3 * p + 2].astype(bf16)
                lt = lt + jnp.where(ci == 2 * p, res[3 * p], 0.0) + jnp.where(ci == 2 * p + 1, res[3 * p + 1], 0.0)
            lt_ref[qrows, :] = lt
            return c
        lax.fori_loop(0, S // BLK, qblock, 0)

    spec = lambda j: pl.BlockSpec((S, GW), lambda b: (b, j))
    c_args, c_in, c_out, c_shapes, aliases, c_scratch = _host_specs(comm, 3, 2)
    step = lambda v: (lambda: pl.program_id(0) == v)
    stacked = pltpu.VMEM((2, S // BLK, 2 * BLK, BLK), _MXU)
    return pl.pallas_call(
        _host(body, 3, 2, 2, comm, step(0), step(B - 1)), grid=(B,),
        in_specs=[spec(5), spec(6), spec(7)] + c_in,
        out_specs=[pl.BlockSpec((S, GW), lambda b: (b, 0)), pl.BlockSpec((S, BLK), lambda b: (b, 0))] + c_out,
        out_shape=[SDS((B * S, GW), bf16), SDS((B * S, BLK), f32)] + c_shapes,
        input_output_aliases=aliases, scratch_shapes=[stacked, stacked] + c_scratch,
        name="sb_fwd" if comm is None else "sb_fwd_gather",
        compiler_params=_cp("arbitrary"))(proj, proj, proj, *c_args)


def _sb_bwd(proj, ltot, dy, B, S, comm=None):
    def body(q_ref, k_ref, v_ref, lt_ref, dy_ref, dq_ref, dk_ref, dv_ref, dka_ref, dva_ref, kst_ref, vst_ref):
        ci = _iota((BLK, BLK), 1)
        upto2, strict2, head0 = _sb_consts(lambda r, c: r <= c)
        below2, _, _ = _sb_consts(lambda r, c: r < c)
        dka_ref[...] = jnp.zeros_like(dka_ref)
        dva_ref[...] = jnp.zeros_like(dva_ref)
        _sb_stack_kv(k_ref, v_ref, kst_ref, vst_ref, head0, S // BLK)

        def step(qs, dos, lts, kbs, ok, diags, top, carry):
            U = range(len(kbs))
            kbs = [jnp.clip(kb, 0, top) for kb in kbs]
            kv = [_sb_load_kv(kst_ref, vst_ref, kb) for kb in kbs]
            zs = [[lax.dot_general(qs[p], kv[u][0][p], NT, preferred_element_type=f32) for p in range(2)] for u in U]
            dws = [[lax.dot_general(dos[p], kv[u][1][p], NT, preferred_element_type=f32) for p in range(2)] for u in U]
            parts = [[_log1m_parts(z) for z in zu] for zu in zs]
            Ls = [[jnp.where(strict2, lt[0], 0.0) if diags[u] else lt[0] for lt in parts[u]] for u in U]
            pins = [[_split_dot(L, upto2) for L in Lu] for Lu in Ls]
            carry = list(carry)
            ws, das = [], []
            for u in U:
                wu, dau = [], []
                for p in range(2):
                    PL0, PL1 = carry[5 * p], carry[5 * p + 1]
                    tail = _two_halves(lts[2 * p] - PL0, lts[2 * p + 1] - PL1) - pins[u][p]
                    w = jnp.exp(zs[u][p] + Ls[u][p] + tail)
                    l0, l1 = _half_sums(Ls[u][p])
                    if diags[u]:
                        w = jnp.where(strict2, w, 0.0)
                    else:
                        w, l0, l1 = (jnp.where(ok[u], t, 0.0) for t in (w, l0, l1))
                    carry[5 * p], carry[5 * p + 1] = PL0 + l0, PL1 + l1
                    wu.append(w)
                    dau.append(w * dws[u][p])
                ws.append(wu)
                das.append(dau)
            pexs = [[_split_dot(da, below2) for da in dau] for dau in das]
            dzs = []
            for u in U:
                dzu = []
                for p in range(2):
                    dL = _two_halves(carry[5 * p + 2], carry[5 * p + 3]) + pexs[u][p]
                    sg = _sigmoid_from(zs[u][p], parts[u][p][1])
                    dz = das[u][p] * (1.0 - sg) - dL * sg
                    dz = jnp.where(strict2 if diags[u] else ok[u], dz, 0.0)
                    a0, a1 = _half_sums(das[u][p])
                    carry[5 * p + 2], carry[5 * p + 3] = carry[5 * p + 2] + a0, carry[5 * p + 3] + a1
                    dzu.append(_mx(dz))
                dzs.append(dzu)
            dqs = [[jnp.dot(dzs[u][p], kv[u][0][p], preferred_element_type=f32) for p in range(2)] for u in U]
            dks = [[lax.dot_general(dzs[u][p], qs[p], TN, preferred_element_type=f32) for p in range(2)] for u in U]
            dvs = [[lax.dot_general(_mx(ws[u][p]), dos[p], TN, preferred_element_type=f32) for p in range(2)] for u in U]
            for u in U:
                krows = pl.ds(pl.multiple_of(kbs[u] * BLK, BLK), BLK)
                for p in range(2):
                    lanes = slice(p * BLK, (p + 1) * BLK)
                    dka_ref[krows, lanes] += jnp.where(head0, dks[u][p][:BLK], dks[u][p][BLK:])
                    dva_ref[krows, lanes] += jnp.where(head0, dvs[u][p][:BLK], dvs[u][p][BLK:])
                    carry[5 * p + 4] = carry[5 * p + 4] + dqs[u][p]
            return tuple(carry)

        def qblock(n, c):
            qrows = pl.ds(pl.multiple_of(n * BLK, BLK), BLK)
            ltb = lt_ref[qrows, :]
            lts = [jnp.sum(jnp.where(ci == h, ltb, 0.0), axis=-1, keepdims=True) for h in range(4)]
            qs = [_mx(q_ref[qrows, p * BLK:(p + 1) * BLK] * (HD ** -0.5)) for p in range(2)]
            dos = [_mx(dy_ref[qrows, p * BLK:(p + 1) * BLK]) for p in range(2)]
            z1, z2 = jnp.zeros((BLK, 1), f32), jnp.zeros((BLK, BLK), f32)
            done = jnp.max(jnp.where(ci == SB_HEADS, ltb, 0.0)).astype(jnp.int32)
            far = jnp.maximum(n - SB_UNROLL, 0)
            first = jnp.maximum(far - SB_UNROLL * done, 0)

            def trip(i, cr):
                kbs = [first + SB_UNROLL * i + u for u in range(SB_UNROLL)]
                return step(qs, dos, lts, kbs, [kb < far for kb in kbs], [False] * SB_UNROLL, n, cr)
            carry = lax.fori_loop(0, (far - first + SB_UNROLL - 1) // SB_UNROLL, trip, (z1, z1, z1, z1, z2) * 2)
            near = [n - SB_UNROLL + u for u in range(SB_UNROLL)]
            res = step(qs, dos, lts, near + [n], [kb >= 0 for kb in near] + [None], [False] * SB_UNROLL + [True], n, carry)
            for p in range(2):
                dq_ref[qrows, p * BLK:(p + 1) * BLK] = (res[5 * p + 4] * (HD ** -0.5)).astype(bf16)
            return c
        lax.fori_loop(0, S // BLK, qblock, 0)
        dk_ref[...] = dka_ref[...].astype(bf16)
        dv_ref[...] = dva_ref[...].astype(bf16)

    spec = lambda j: pl.BlockSpec((S, GW), lambda b: (b, j))
    o = pl.BlockSpec((S, GW), lambda b: (b, 0))
    c_args, c_in, c_out, c_shapes, aliases, c_scratch = _host_specs(comm, 5, 3)
    step = lambda v: (lambda: pl.program_id(0) == v)
    stacked = pltpu.VMEM((2, S // BLK, 2 * BLK, BLK), _MXU)
    return pl.pallas_call(
        _host(body, 5, 3, 4, comm, step(0), step(B - 1)), grid=(B,),
        in_specs=[spec(5), spec(6), spec(7), pl.BlockSpec((S, BLK), lambda b: (b, 0)), o] + c_in,
        out_specs=[o, o, o] + c_out,
        out_shape=[SDS((B * S, GW), bf16)] * 3 + c_shapes,
        input_output_aliases=aliases,
        scratch_shapes=[pltpu.VMEM((S, GW), f32), pltpu.VMEM((S, GW), f32), stacked, stacked] + c_scratch,
        name="sb_bwd" if comm is None else "sb_bwd_exchange",
        compiler_params=_cp("arbitrary"))(proj, proj, proj, ltot, dy, *c_args)


def _bias_expand(rel_bias_t, bucket):
    n = bucket.shape[1]

    def body(r_ref, b_ref, o_ref):
        onehot = (_iota((N_BUCKETS, n), 0) == b_ref[...]).astype(f32)
        o_ref[...] = jnp.dot(r_ref[...], onehot, precision=HIGHEST, preferred_element_type=f32)
    return pl.pallas_call(body, out_shape=SDS((rel_bias_t.shape[0], n), f32), name="bias_expand",
                          compiler_params=_cp())(rel_bias_t, bucket)


def _bias_reduce(dbias, bucket):
    n = bucket.shape[1]

    def body(*refs):
        b_ref, g_ref = refs[-2], refs[-1]
        d = refs[0][...]
        for r in refs[1:-2]:
            d = d + r[...]
        onehot = (_iota((N_BUCKETS, n), 0) == b_ref[...]).astype(f32)
        g_ref[...] = lax.dot_general(d, onehot, NT, precision=HIGHEST, preferred_element_type=f32)
    return pl.pallas_call(body, out_shape=SDS((dbias[0].shape[0], N_BUCKETS), f32), name="bias_reduce",
                          compiler_params=_cp())(*dbias, bucket)


def _adamw(w, g, m, v, tr, name, emit_g=False):
    R, C = w.shape

    def body(w_ref, g_ref, m_ref, v_ref, d_ref, m2_ref, v2_ref, *g_out):
        gv = g_ref[...]
        if emit_g:
            g_out[0][...] = gv
        m2 = ADAM_B1 * m_ref[...] + (1.0 - ADAM_B1) * gv
        v2 = ADAM_B2 * v_ref[...] + (1.0 - ADAM_B2) * (gv * gv)
        m_hat = m2 / (1.0 - ADAM_B1 ** ADAM_STEP)
        v_hat = v2 / (1.0 - ADAM_B2 ** ADAM_STEP)
        d_ref[...] = -ADAM_LR * (m_hat / (jnp.sqrt(v_hat) + ADAM_EPS) + ADAM_WD * w_ref[...])
        m2_ref[...] = m2
        v2_ref[...] = v2

    spec = pl.BlockSpec((tr, C), lambda i: (i, 0))
    n_out = 4 if emit_g else 3
    return pl.pallas_call(
        body, grid=(R // tr,), in_specs=[spec] * 4, out_specs=[spec] * n_out,
        out_shape=[SDS((R, C), f32)] * n_out, name=name, compiler_params=_cp("parallel"))(w, g, m, v)


ANY = pl.BlockSpec(memory_space=pl.ANY)


def _place():
    x, y, c = lax.axis_index("x"), lax.axis_index("y"), lax.axis_index("c")
    chips = [(1 - x, y), (x, 1 - y), (1 - x, 1 - y)]
    return x, y, c, chips


def _cast_slots(w, kidx):
    L, a, b = w.shape
    ta = a // 2

    def body(k_ref, *refs):
        for l in range(L):
            refs[L + l][0] = refs[l][0].astype(bf16)

    return pl.pallas_call(
        body,
        grid_spec=pltpu.PrefetchScalarGridSpec(
            num_scalar_prefetch=1, grid=(a // ta,),
            in_specs=[pl.BlockSpec((1, ta, b), functools.partial(lambda i, k_ref, l: (l, i, 0), l=l)) for l in range(L)],
            out_specs=[pl.BlockSpec((1, ta, b), lambda i, k_ref: (k_ref[0], i, 0)) for _ in range(L)]),
        out_shape=[SDS((N_CHIPS, a, b), bf16)] * L,
        name="cast_slots", compiler_params=_cp("parallel"))(kidx, *([w] * L))


class _GatherComm:
    def __init__(self, bufs, part=0, parts=1):
        self.inputs = list(bufs)
        self.out_shape = [SDS(b.shape, b.dtype) for b in bufs]
        self.aliased = True
        self.scratch = [pltpu.SemaphoreType.DMA((3 * len(bufs),))] * 4
        self.part, self.parts = part, parts

    def _copies(self, i_refs, o_refs, sems):
        send1, recv1, send2, recv2 = sems
        x, y, c, chips = _place()
        k = 2 * x + y
        first, got1, second, got2 = [], [], [], []
        for i, buf in enumerate(self.inputs):
            h = buf.shape[1] // 2
            n = h // self.parts
            mine, theirs = pl.ds(c * h + self.part * n, n), pl.ds((1 - c) * h + self.part * n, n)
            for j, (cx, cy) in enumerate(chips):
                s = 3 * i + j
                first.append(pltpu.make_async_remote_copy(
                    src_ref=i_refs[i].at[k, mine], dst_ref=o_refs[i].at[k, mine], send_sem=send1.at[s],
                    recv_sem=recv1.at[s], device_id=(cx, cy, c), device_id_type=MESH))
                a = o_refs[i].at[2 * cx + cy, mine]
                got1.append(pltpu.make_async_remote_copy(
                    src_ref=a, dst_ref=a, send_sem=send1.at[s], recv_sem=recv1.at[s],
                    device_id=(cx, cy, c), device_id_type=MESH))
                second.append(pltpu.make_async_remote_copy(
                    src_ref=a, dst_ref=a, send_sem=send2.at[s], recv_sem=recv2.at[s],
                    device_id=(x, y, 1 - c), device_id_type=MESH))
                b = o_refs[i].at[2 * cx + cy, theirs]
                got2.append(pltpu.make_async_remote_copy(
                    src_ref=b, dst_ref=b, send_sem=send2.at[s], recv_sem=recv2.at[s],
                    device_id=(x, y, 1 - c), device_id_type=MESH))
        return first, got1, second, got2

    def start(self, i_refs, o_refs, sems):
        for cp in self._copies(i_refs, o_refs, sems)[0]:
            cp.start()

    def finish(self, i_refs, o_refs, sems):
        first, got1, second, got2 = self._copies(i_refs, o_refs, sems)
        for g, cp in zip(got1, second):
            g.wait_recv()
            cp.start()
        for g in got2:
            g.wait_recv()
        for cp in first + second:
            cp.wait_send()


class _MultiComm:
    def __init__(self, comms):
        self.comms = comms
        self.inputs = [a for c in comms for a in c.inputs]
        self.out_shape = [s for c in comms for s in c.out_shape]
        self.aliased = comms[0].aliased
        assert all(c.aliased == self.aliased for c in comms)
        self.scratch = [s for c in comms for s in c.scratch]

    def _split(self, i_refs, o_refs, sems):
        i = o = s = 0
        for c in self.comms:
            ni, no, ns = len(c.inputs), len(c.out_shape), len(c.scratch)
            yield c, i_refs[i:i + ni], o_refs[o:o + no], sems[s:s + ns]
            i, o, s = i + ni, o + no, s + ns

    def start(self, i_refs, o_refs, sems):
        for c, i, o, s in self._split(i_refs, o_refs, sems):
            c.start(i, o, s)

    def finish(self, i_refs, o_refs, sems):
        for c, i, o, s in self._split(i_refs, o_refs, sems):
            c.finish(i, o, s)


class _PairExchangeComm:
    def __init__(self, gs):
        self.inputs = list(gs)
        self.out_shape = [SDS((g.shape[0], g.shape[1] // 2, g.shape[2]), g.dtype) for g in gs]
        self.aliased = False
        self.scratch = [pltpu.SemaphoreType.DMA((len(gs),))] * 2

    def _copies(self, i_refs, o_refs, sems):
        send, recv = sems
        x, y, c, _ = _place()
        cps = []
        for i, g in enumerate(self.inputs):
            h = g.shape[1] // 2
            cps.append(pltpu.make_async_remote_copy(
                src_ref=i_refs[i].at[:, pl.ds((1 - c) * h, h)], dst_ref=o_refs[i], send_sem=send.at[i], recv_sem=recv.at[i],
                device_id=(x, y, 1 - c), device_id_type=MESH))
        return cps

    def start(self, i_refs, o_refs, sems):
        for cp in self._copies(i_refs, o_refs, sems):
            cp.start()

    def finish(self, i_refs, o_refs, sems):
        for cp in self._copies(i_refs, o_refs, sems):
            cp.wait()


class _ChipExchangeComm:
    def __init__(self, qs):
        self.inputs = list(qs)
        self.out_shape = [SDS(q.shape, q.dtype) for q in qs]
        self.aliased = False
        self.scratch = [pltpu.SemaphoreType.DMA((3 * len(qs),))] * 2

    def _copies(self, i_refs, o_refs, sems):
        send, recv = sems
        x, y, c, chips = _place()
        k = 2 * x + y
        cps, got = [], []
        for i in range(len(self.inputs)):
            for j, (cx, cy) in enumerate(chips):
                s = 3 * i + j
                cps.append(pltpu.make_async_remote_copy(
                    src_ref=i_refs[i].at[2 * cx + cy], dst_ref=o_refs[i].at[k], send_sem=send.at[s],
                    recv_sem=recv.at[s], device_id=(cx, cy, c), device_id_type=MESH))
                a = o_refs[i].at[2 * cx + cy]
                got.append(pltpu.make_async_remote_copy(
                    src_ref=a, dst_ref=a, send_sem=send.at[s], recv_sem=recv.at[s],
                    device_id=(cx, cy, c), device_id_type=MESH))
        return cps, got

    def start(self, i_refs, o_refs, sems):
        for cp in self._copies(i_refs, o_refs, sems)[0]:
            cp.start()

    def finish(self, i_refs, o_refs, sems):
        cps, got = self._copies(i_refs, o_refs, sems)
        for g in got:
            g.wait_recv()
        for cp in cps:
            cp.wait_send()


def _comm_only(comm, name):
    n = len(comm.inputs)

    def body(*refs):
        i_refs, o_refs, sems = refs[:n], refs[n:n + len(comm.out_shape)], refs[n + len(comm.out_shape):]
        comm.start(i_refs, o_refs, sems)
        comm.finish(i_refs, o_refs, sems)

    return pl.pallas_call(
        body, out_shape=comm.out_shape, in_specs=[ANY] * n, out_specs=[ANY] * len(comm.out_shape),
        input_output_aliases={i: i for i in range(n)} if comm.aliased else {},
        scratch_shapes=comm.scratch, name=name,
        compiler_params=pltpu.CompilerParams(has_side_effects=True))(*comm.inputs)


def _host(body, n_in, n_out, n_scratch, comm, first, last):
    if comm is None:
        return body
    ci, co = len(comm.inputs), len(comm.out_shape)

    def wrapped(*refs):
        o = 0
        parts = []
        for n in (n_in, ci, n_out, co, n_scratch):
            parts.append(refs[o:o + n])
            o += n
        hin, cin, hout, cout, hs = parts
        sems = refs[o:]

        @pl.when(first())
        def _():
            comm.start(cin, cout, sems)
        body(*hin, *hout, *hs)

        @pl.when(last())
        def _():
            comm.finish(cin, cout, sems)
    return wrapped


def _host_specs(comm, n_in, n_out):
    if comm is None:
        return [], [], [], [], {}, []
    ci, co = len(comm.inputs), len(comm.out_shape)
    aliases = {n_in + i: n_out + i for i in range(ci)} if comm.aliased else {}
    return comm.inputs, [ANY] * ci, [ANY] * co, comm.out_shape, aliases, comm.scratch


def _pair_add(g, r, kc, name):
    ns, a, b = g.shape
    h = a // 2
    th = h if h * b * 4 <= 4 * 1024 * 1024 else h // 2

    def body(kc_ref, g_ref, r_ref, qb_ref):
        qb_ref[...] = (g_ref[...] + r_ref[...]).astype(bf16)

    nb = h // th
    slot = lambda j, kc_ref: (kc_ref[0] + 1 + j) % ns
    spec = pl.BlockSpec((1, th, b), lambda j, i, kc_ref: (slot(j, kc_ref), i, 0))
    return pl.pallas_call(
        body,
        grid_spec=pltpu.PrefetchScalarGridSpec(
            num_scalar_prefetch=1, grid=(ns - 1, nb),
            in_specs=[pl.BlockSpec((1, th, b), lambda j, i, kc_ref: (slot(j, kc_ref), kc_ref[1] * nb + i, 0)), spec],
            out_specs=spec),
        out_shape=SDS((ns, h, b), bf16),
        name=name, compiler_params=_cp("parallel", "parallel"))(kc, g, r)


def _chip_add(g, r1, r2, idx, prev, L, name):
    ns, h, b = r2.shape
    th = h if h * b * 4 <= 4 * 1024 * 1024 else h // 2
    nb = h // th

    def body(s_ref, g_ref, r1_ref, a_ref, b_ref, c_ref, *rest):
        o_ref = rest[-1]
        o_ref[0] = (g_ref[0] + r1_ref[0]) + a_ref[0].astype(f32) + b_ref[0].astype(f32) + c_ref[0].astype(f32)

    other = lambda d: pl.BlockSpec((1, th, b), lambda i, s_ref: ((s_ref[0] + d) % ns, i, 0))
    in_specs = [pl.BlockSpec((1, th, b), lambda i, s_ref: (s_ref[0], s_ref[1] * nb + i, 0)),
                pl.BlockSpec((1, th, b), lambda i, s_ref: (s_ref[0], i, 0)), other(1), other(2), other(3)]
    args = [idx, g, r1, r2, r2, r2]
    aliases = {}
    if prev is not None:
        in_specs.append(ANY)
        args.append(prev)
        aliases = {6: 0}
    return pl.pallas_call(
        body,
        grid_spec=pltpu.PrefetchScalarGridSpec(
            num_scalar_prefetch=1, grid=(nb,), in_specs=in_specs,
            out_specs=pl.BlockSpec((1, th, b), lambda i, s_ref: (s_ref[2], s_ref[1] * nb + i, 0))),
        out_shape=SDS((L, 2 * h, b), f32), input_output_aliases=aliases,
        name=name, compiler_params=_cp("arbitrary"))(*args)


def _pair_share(gs, hs):
    n = len(gs)
    L = gs[0].shape[0]

    def body(*refs):
        i_refs, o_refs = refs[:n], refs[n:2 * n]
        send, recv = refs[2 * n:]
        x, y, c, _ = _place()
        cps = []
        for i in range(n):
            for l in range(L):
                mine = pl.ds(c * hs[i], hs[i])
                cp = pltpu.make_async_remote_copy(
                    src_ref=i_refs[i].at[l, mine], dst_ref=o_refs[i].at[l, mine], send_sem=send.at[i * L + l],
                    recv_sem=recv.at[i * L + l], device_id=(x, y, 1 - c), device_id_type=MESH)
                cp.start()
                cps.append(cp)
        for i in range(n):
            for l in range(L):
                got = o_refs[i].at[l, pl.ds((1 - c) * hs[i], hs[i])]
                pltpu.make_async_remote_copy(
                    src_ref=got, dst_ref=got, send_sem=send.at[i * L + l], recv_sem=recv.at[i * L + l],
                    device_id=(x, y, 1 - c), device_id_type=MESH).wait_recv()
        for cp in cps:
            cp.wait_send()

    return pl.pallas_call(
        body, out_shape=[SDS(g.shape, g.dtype) for g in gs], in_specs=[ANY] * n, out_specs=[ANY] * n,
        input_output_aliases={i: i for i in range(n)},
        scratch_shapes=[pltpu.SemaphoreType.DMA((n * L,))] * 2,
        name="grad_pair_share", compiler_params=pltpu.CompilerParams(has_side_effects=True))(*gs)


class _SwapComm:
    def __init__(self, arrays):
        self.inputs = list(arrays)
        self.out_shape = [SDS(a.shape, a.dtype) for a in arrays]
        self.aliased = False
        self.scratch = [pltpu.SemaphoreType.DMA((len(arrays),))] * 2

    def _copies(self, i_refs, o_refs, sems):
        send, recv = sems
        x, y, c, _ = _place()
        return [pltpu.make_async_remote_copy(
            src_ref=i_refs[i], dst_ref=o_refs[i], send_sem=send.at[i], recv_sem=recv.at[i],
            device_id=(x, y, 1 - c), device_id_type=MESH) for i in range(len(self.inputs))]

    def start(self, i_refs, o_refs, sems):
        for cp in self._copies(i_refs, o_refs, sems):
            cp.start()

    def finish(self, i_refs, o_refs, sems):
        for cp in self._copies(i_refs, o_refs, sems):
            cp.wait()


class _SlotShareComm:
    def __init__(self, bufs):
        self.inputs = list(bufs)
        self.out_shape = [SDS(b.shape, b.dtype) for b in bufs]
        self.aliased = True
        self.scratch = [pltpu.SemaphoreType.DMA((3 * len(bufs),))] * 2

    def _copies(self, i_refs, o_refs, sems):
        send, recv = sems
        x, y, c, chips = _place()
        k = 2 * x + y
        cps, got = [], []
        for i in range(len(self.inputs)):
            for j, (cx, cy) in enumerate(chips):
                s = 3 * i + j
                cps.append(pltpu.make_async_remote_copy(
                    src_ref=i_refs[i].at[k], dst_ref=o_refs[i].at[k], send_sem=send.at[s], recv_sem=recv.at[s],
                    device_id=(cx, cy, c), device_id_type=MESH))
                a = o_refs[i].at[2 * cx + cy]
                got.append(pltpu.make_async_remote_copy(
                    src_ref=a, dst_ref=a, send_sem=send.at[s], recv_sem=recv.at[s],
                    device_id=(cx, cy, c), device_id_type=MESH))
        return cps, got

    def start(self, i_refs, o_refs, sems):
        for cp in self._copies(i_refs, o_refs, sems)[0]:
            cp.start()

    def finish(self, i_refs, o_refs, sems):
        cps, got = self._copies(i_refs, o_refs, sems)
        for g in got:
            g.wait_recv()
        for cp in cps:
            cp.wait_send()


def _pair_sum_slot(mine, theirs, kidx, dtype):
    R, C = mine.shape

    def body(k_ref, a_ref, b_ref, o_ref):
        o_ref[0] = (a_ref[...] + b_ref[...]).astype(dtype)

    spec = pl.BlockSpec((R, C), lambda i, k_ref: (0, 0))
    return pl.pallas_call(
        body,
        grid_spec=pltpu.PrefetchScalarGridSpec(
            num_scalar_prefetch=1, grid=(1,), in_specs=[spec, spec],
            out_specs=pl.BlockSpec((1, R, C), lambda i, k_ref: (k_ref[0], 0, 0))),
        out_shape=SDS((N_CHIPS, R, C), dtype), name="small_pair_sum", compiler_params=_cp("arbitrary"))(kidx, mine, theirs)


def _small_sum(g):
    n, R, C = g.shape

    def body(g_ref, o_ref):
        acc = g_ref[0].astype(f32)
        for j in range(1, n):
            acc = acc + g_ref[j].astype(f32)
        o_ref[...] = acc
    return pl.pallas_call(body, out_shape=SDS((R, C), f32), name="small_sum", compiler_params=_cp())(g)


PACK_COLS = 1024


def _rows_of(shape):
    n = int(np.prod(shape)) if len(shape) else 1
    return -(-n // (8 * PACK_COLS)) * 8


def _pack(parts):
    blocks = []
    for p in parts:
        flat = p.reshape(-1)
        r = _rows_of(p.shape)
        blocks.append(jnp.pad(flat, (0, r * PACK_COLS - flat.shape[0])).reshape(r, PACK_COLS))
    return jnp.concatenate(blocks, axis=0)


def _unpack(buf, shapes):
    out, off = [], 0
    for s in shapes:
        n = int(np.prod(s)) if len(s) else 1
        r = _rows_of(s)
        out.append(buf[off:off + r].reshape(-1)[:n].reshape(s))
        off += r
    return out


def _block_diag(w):
    g, a, _ = w.shape
    out = jnp.zeros((g * a, g * a), w.dtype)
    for i in range(g):
        out = lax.dynamic_update_slice(out, w[i], (i * a, i * a))
    return out


def kernel(x, w_in, w_out, sgu_w, sgu_b, pool_w, pool_scale, swa_sinks, rel_bias, mix_out_gain, norm_mix, norm_ffn, w_gate_up, w_down, norm_final, loss_target, m_w_in, m_w_out, m_sgu_w, m_sgu_b, m_pool_w, m_pool_scale, m_swa_sinks, m_rel_bias, m_mix_out_gain, m_norm_mix, m_norm_ffn, m_w_gate_up, m_w_down, m_norm_final, v_w_in, v_w_out, v_sgu_w, v_sgu_b, v_pool_w, v_pool_scale, v_swa_sinks, v_rel_bias, v_mix_out_gain, v_norm_mix, v_norm_ffn, v_w_gate_up, v_w_down, v_norm_final):
    B, S, D = x.shape
    T = B * S
    L = w_in.shape[0]
    tm = min(512, T)
    tm2 = min(1024, T)
    F = w_down.shape[1] * N_CHIPS
    xi, yi, ci = lax.axis_index("x"), lax.axis_index("y"), lax.axis_index("c")
    kidx = jnp.reshape(2 * xi + yi, (1,)).astype(jnp.int32)
    kcidx = jnp.stack([2 * xi + yi, ci]).astype(jnp.int32)

    big = [w_in, w_out, w_gate_up, w_down]
    slots = [_cast_slots(w, kidx) for w in big]
    gather = lambda pi, l, part=0, parts=1: _GatherComm([slots[pi][l]], part, parts)
    Win, Wo, Wgu, Wd = ([None] * L for _ in range(4))
    Win[0], = _comm_only(gather(0, 0), "gather_weights")

    bucket = jnp.asarray(_t5_bucket_table().reshape(1, -1))
    bias_tab = _bias_expand(rel_bias.T, bucket).reshape(4, BLK, 2 * BLK)

    row = lambda v: v.reshape(1, -1)
    xc = x.reshape(T, D)
    tgt = loss_target.reshape(T, D)
    saved = []
    for l in range(L):
        bexp = jnp.repeat(sgu_b[l].T, HD, axis=1)
        wbd = _block_diag(pool_w[l])
        sk = jnp.broadcast_to(swa_sinks[l][:, None, None], (4, 1, BLK))
        h1, proj, wo = _norm_mm(xc, row(norm_mix[l]), Win[l], tm2, gather(1, l))
        ya = _sgu_fwd(proj, sgu_w[l], bexp, B, S)
        yb = _pool_fwd(proj, wbd, row(pool_scale[l]), B, S)
        yc, wd = _swa_fwd(proj, sk, bias_tab, B, S, gather(3, l, 0, 2))
        if l == 0:
            yd, lt, wd, Wgu[0] = _sb_fwd(proj, B, S, _MultiComm([_GatherComm([wd], 1, 2), gather(2, 0)]))
        else:
            yd, lt, wd = _sb_fwd(proj, B, S, _GatherComm([wd], 1, 2))
        Wo[l], Wd[l] = wo.reshape(D, D), wd.reshape(F, D)
        ys = (ya, yb, yc, yd)
        nxt = l + 1 < L
        ycn, x1, h2, gu, act, *got = _mix_out_swiglu(ys, row(mix_out_gain[l]), Wo[l], xc, row(norm_ffn[l]), Wgu[l], tm,
                                                     gather(2, l + 1) if nxt else None)
        x2, *got2 = _mm_res(act, Wd[l], x1, tm2, gather(0, l + 1) if nxt else None)
        if nxt:
            Wgu[l + 1], Win[l + 1] = got[0], got2[0]
        saved.append((xc, h1, proj, bexp, wbd, sk, ys, lt, ycn, x1, h2, gu, act))
        xc = x2

    dx, g_final, loss_v = _final_loss(xc, row(norm_final), tgt, tm)

    tk = min(T, 2048)
    gW = [[None] * L for _ in range(4)]
    g_sgu_w, g_sgu_b, g_pool_w, g_pool_scale, g_sinks, g_bias = ([None] * L for _ in range(6))
    g_out_gain, g_mix, g_ffn = ([None] * L for _ in range(3))
    reduced = [None] * 4
    sums = {}

    def pair_comm(keys):
        return _PairExchangeComm([gW[pi][l] for pi, l in keys])

    def after_pair(keys, r1):
        for (pi, l), r in zip(keys, r1):
            sums[pi, l] = (r, _pair_add(gW[pi][l], r, kcidx, "grad_pair_add"))

    def chip_comm(keys):
        return _ChipExchangeComm([sums[k][1] for k in keys])

    def after_chip(keys, r2):
        for (pi, l), r in zip(keys, r2):
            idx = jnp.stack([2 * xi + yi, ci, jnp.int32(l)]).astype(jnp.int32)
            reduced[pi] = _chip_add(gW[pi][l], sums.pop((pi, l))[0], r, idx, reduced[pi], L, "grad_chip_add")

    for l in reversed(range(L)):
        x0, h1, proj, bexp, wbd, sk, ys, lt, ycn, x1, h2, gu, act = saved[l]
        keys = [(0, l + 1), (1, l + 1)]
        comm = chip_comm(keys) if l + 1 < L else None
        dgu, dx1, g_ffn[l], *r2 = _dact_dx(dx, Wd[l], gu, Wgu[l], x1, row(norm_ffn[l]), tm // 2, comm)
        if comm is not None:
            after_chip(keys, r2)
        gW[3][l] = _dw(act, dx, lambda t, s: (t, 0), D, 1, F // 2, tk // 2, "dw_down").reshape(N_CHIPS, F // N_CHIPS, D)
        gW[2][l] = _dw(h2, dgu, lambda t, s: (s // 2, t, s % 2), F // 2, N_CHIPS, D, tk, "dw_gate_up")
        gW[1][l] = _dw(ycn, dx1, lambda t, s: (t, 0), D, 1, D, tk, "dw_out").reshape(N_CHIPS, D // N_CHIPS, D)
        dya, dyb, dyc, dyd, g_out_gain[l] = _dycat(dx1, Wo[l], ys, row(mix_out_gain[l]), tm2)
        keys = [(2, l), (3, l)]
        dpa, g_sgu_w[l], dbf, *r1 = _sgu_bwd(proj, sgu_w[l], bexp, dya, B, S, pair_comm(keys))
        after_pair(keys, r1)
        g_sgu_b[l] = dbf[:, ::HD].T
        dpb, dwbd, dsc = _pool_bwd(proj, wbd, row(pool_scale[l]), dyb, B, S)
        npg = len(POOL_WINDOWS)
        g_pool_w[l] = jnp.stack([dwbd[i * HD:(i + 1) * HD, i * HD:(i + 1) * HD] for i in range(npg)])
        g_pool_scale[l] = dsc[0]
        dcq, dckv, dsk, g_bias[l], *r2 = _swa_bwd(proj, sk, bias_tab, dyc, B, S, chip_comm([(3, l)]))
        after_chip([(3, l)], r2)
        g_sinks[l] = dsk[:, 0, 0] * float(BLK)
        ddq, ddk, ddv, *r2 = _sb_bwd(proj, lt, dyd, B, S, chip_comm([(2, l)]))
        after_chip([(2, l)], r2)
        dproj = [dpa, dpb, dcq, dckv, ddq, ddk, ddv]
        gW[0][l] = _dw_pieces(h1, dproj, w_in.shape[2], N_CHIPS, tk, "dw_in")
        keys = [(0, l), (1, l)]
        dx, g_mix[l], *r1 = _dx_norm_bwd(dproj, Win[l], x0, row(norm_mix[l]), dx1, tm, "dx_mix_exchange", pair_comm(keys))
        after_pair(keys, r1)
    grad_x = dx.reshape(B, S, D)

    after_chip(keys, _comm_only(chip_comm(keys), "grad_chip_exchange"))
    g_big = _pair_share(reduced, [g.shape[1] // 2 for g in reduced])

    g_rel_bias = _bias_reduce([g.reshape(4, -1) for g in g_bias], bucket).T
    small_g = [jnp.stack(g_sgu_w), jnp.stack(g_sgu_b), jnp.stack(g_pool_w), jnp.stack(g_pool_scale), jnp.stack(g_sinks),
               g_rel_bias, jnp.concatenate(g_out_gain), jnp.concatenate(g_mix), jnp.concatenate(g_ffn), g_final[0]]
    small_w = [sgu_w, sgu_b, pool_w, pool_scale, swa_sinks, rel_bias, mix_out_gain, norm_mix, norm_ffn, norm_final]
    small_m = [m_sgu_w, m_sgu_b, m_pool_w, m_pool_scale, m_swa_sinks, m_rel_bias, m_mix_out_gain, m_norm_mix, m_norm_ffn, m_norm_final]
    small_v = [v_sgu_w, v_sgu_b, v_pool_w, v_pool_scale, v_swa_sinks, v_rel_bias, v_mix_out_gain, v_norm_mix, v_norm_ffn, v_norm_final]
    shapes = [w.shape for w in small_w]
    bulk, fine = [small_g[0], small_g[2]], [small_g[i] for i in (1, 3, 4, 5, 6, 7, 8, 9)] + [loss_v[0, 0:1]]
    mine = [_pack(bulk), _pack(fine)]
    theirs = _comm_only(_SwapComm(mine), "small_pair_swap")
    slots_s = [_pair_sum_slot(a, b, kidx, dt) for a, b, dt in zip(mine, theirs, (bf16, f32))]
    shared = _comm_only(_SlotShareComm(slots_s), "small_chip_share")
    g_bulk = _unpack(_small_sum(shared[0]), [shapes[0], shapes[2]])
    *g_fine, loss = _unpack(_small_sum(shared[1]), [shapes[i] for i in (1, 3, 4, 5, 6, 7, 8, 9)] + [()])
    g_small = [g_bulk[0], g_fine[0], g_bulk[1]] + g_fine[1:]

    big_m = [m_w_in, m_w_out, m_w_gate_up, m_w_down]
    big_v = [v_w_in, v_w_out, v_w_gate_up, v_w_down]
    g_out, d_big, m_big, v_big = [], [], [], []
    for w, g, m, v in zip(big, g_big, big_m, big_v):
        two = lambda a: a.reshape(-1, a.shape[-1])
        rows = two(w).shape[0]
        cap = max(8, (1 << 20) // (4 * w.shape[-1]))
        tr = max(t for t in range(8, min(rows, cap) + 1, 8) if rows % t == 0)
        d2, m2, v2, g2 = _adamw(two(w), two(g), two(m), two(v), tr, "adamw_big", True)
        for lst, val in ((d_big, d2), (m_big, m2), (v_big, v2), (g_out, g2)):
            lst.append(val.reshape(w.shape))
    g_big = g_out

    g_small_packed = _pack(g_small)
    ds, ms, vs = _adamw(_pack(small_w), g_small_packed, _pack(small_m), _pack(small_v), g_small_packed.shape[0], "adamw_small")
    d_small, m_small, v_small = _unpack(ds, shapes), _unpack(ms, shapes), _unpack(vs, shapes)

    def order(bigs, smalls):
        return [bigs[0], bigs[1]] + list(smalls[0:9]) + [bigs[2], bigs[3], smalls[9]]

    return (loss, grad_x, *order(g_big, g_small), *order(d_big, d_small), *order(m_big, m_small), *order(v_big, v_small))
```

```python
import functools

import numpy as np
import jax
import jax.numpy as jnp
from jax import lax
from jax.experimental import pallas as pl
from jax.experimental.pallas import tpu as pltpu

f32 = jnp.float32
bf16 = jnp.bfloat16
_MXU = jnp.bfloat16

EPS = 1e-6
HD = 64
GW = 256
BLK = 128
SB_UNROLL = 2
SB_HEADS = 4
SB_CUT = -110.0
POOL_WINDOWS = (2, 4, 8, 16)
N_BUCKETS = 32
MAX_DISTANCE = 128
N_CHIPS = 4
VMEM_LIMIT = 48 * 1024 * 1024

ADAM_LR = 0.001
ADAM_B1 = 0.9
ADAM_B2 = 0.999
ADAM_EPS = 1e-08
ADAM_WD = 0.01
ADAM_STEP = 10

SDS = jax.ShapeDtypeStruct
MESH = pl.DeviceIdType.MESH
HIGHEST = lax.Precision.HIGHEST
RESIDENT = pl.Buffered(1)
NT = (((1,), (1,)), ((), ()))
TN = (((0,), (0,)), ((), ()))


def _cp(*sem):
    return pltpu.CompilerParams(dimension_semantics=sem if sem else None, vmem_limit_bytes=VMEM_LIMIT)


def _mx(v):
    return v.astype(_MXU)


def _iota(shape, dim):
    return lax.broadcasted_iota(jnp.int32, shape, dim)


def _split_dot(a, tri):
    hi = a.astype(bf16)
    lo = (a - hi.astype(f32)).astype(bf16)
    return jnp.dot(hi, tri, preferred_element_type=f32) + jnp.dot(lo, tri, preferred_element_type=f32)


def _rms(xv):
    return lax.rsqrt(jnp.mean(xv * xv, axis=-1, keepdims=True) + EPS)


def _hosted_call(body, steps, in_specs, out_specs, out_shape, scratch, args, name, comm):
    n_in, n_out = len(in_specs), len(out_specs)
    c_args, c_in, c_out, c_shapes, aliases, c_scratch = _host_specs(comm, n_in, n_out)
    step = lambda v: (lambda: pl.program_id(0) == v)
    return pl.pallas_call(
        _host(body, n_in, n_out, len(scratch), comm, step(0), step(steps - 1)), grid=(steps,),
        in_specs=list(in_specs) + c_in, out_specs=list(out_specs) + c_out, out_shape=list(out_shape) + c_shapes,
        input_output_aliases=aliases, scratch_shapes=list(scratch) + c_scratch,
        name=name if comm is None else name + "_comm", compiler_params=_cp("arbitrary"))(*args, *c_args)


def _norm_mm(x, gain, w, tm, comm=None):
    T, D = x.shape
    NS, _, ns = w.shape

    def body(x_ref, g_ref, w_ref, h_ref, o_ref):
        xv = x_ref[...]
        h = (xv * _rms(xv) * g_ref[...]).astype(bf16)
        h_ref[...] = h
        for s in range(NS):
            o_ref[:, s * ns:(s + 1) * ns] = jnp.dot(_mx(h), w_ref[s], preferred_element_type=f32).astype(bf16)

    return _hosted_call(
        body, T // tm,
        [pl.BlockSpec((tm, D), lambda i: (i, 0)),
         pl.BlockSpec((1, D), lambda i: (0, 0)),
         pl.BlockSpec((NS, D, ns), lambda i: (0, 0, 0), pipeline_mode=RESIDENT)],
        [pl.BlockSpec((tm, D), lambda i: (i, 0)), pl.BlockSpec((tm, NS * ns), lambda i: (i, 0))],
        [SDS((T, D), bf16), SDS((T, NS * ns), bf16)], [], (x, gain, w), "norm_mm_in", comm)


def _mix_out_swiglu(ys, gain_mix, wo, x, gain_ffn, w, tm, comm=None):
    T, D = x.shape
    NS, _, ns = w.shape
    half = NS // 2

    def body(ya, yb, yc, yd, gm_ref, wo_ref, x_ref, gf_ref, w_ref, yn_ref, x1_ref, h_ref, gu_ref, a_ref):
        parts = []
        for m, r in enumerate((ya, yb, yc, yd)):
            y = r[...].astype(f32)
            parts.append((y * _rms(y) * gm_ref[:, m * GW:(m + 1) * GW]).astype(bf16))
        yn = jnp.concatenate(parts, axis=1)
        yn_ref[...] = yn
        x1 = x_ref[...] + jnp.dot(_mx(yn), wo_ref[...], preferred_element_type=f32)
        x1_ref[...] = x1
        hb = (x1 * _rms(x1) * gf_ref[...]).astype(bf16)
        h_ref[...] = hb
        h = _mx(hb)
        for s in range(half):
            cols = slice(s * ns, (s + 1) * ns)
            g = jnp.dot(h, w_ref[s], preferred_element_type=f32)
            u = jnp.dot(h, w_ref[s + half], preferred_element_type=f32)
            gu_ref[0, :, cols] = g.astype(bf16)
            gu_ref[1, :, cols] = u.astype(bf16)
            a_ref[:, cols] = (jax.nn.silu(g) * u).astype(bf16)

    yspec = pl.BlockSpec((tm, GW), lambda i: (i, 0))
    row, tile = pl.BlockSpec((1, D), lambda i: (0, 0)), pl.BlockSpec((tm, D), lambda i: (i, 0))
    return _hosted_call(
        body, T // tm,
        [yspec, yspec, yspec, yspec, row,
         pl.BlockSpec((D, D), lambda i: (0, 0), pipeline_mode=RESIDENT), tile, row,
         pl.BlockSpec((NS, D, ns), lambda i: (0, 0, 0), pipeline_mode=RESIDENT)],
        [tile, tile, tile, pl.BlockSpec((2, tm, half * ns), lambda i: (0, i, 0)),
         pl.BlockSpec((tm, half * ns), lambda i: (i, 0))],
        [SDS((T, D), bf16), SDS((T, D), f32), SDS((T, D), bf16), SDS((2, T, half * ns), bf16), SDS((T, half * ns), bf16)],
        [], (*ys, gain_mix, wo, x, gain_ffn, w), "mix_out_swiglu", comm)


def _mm_res(a, w, x, tm, comm=None):
    T, D = x.shape
    K = a.shape[1]

    def body(a_ref, w_ref, x_ref, o_ref):
        o_ref[...] = x_ref[...] + jnp.dot(_mx(a_ref[...]), w_ref[...], preferred_element_type=f32)

    return _hosted_call(
        body, T // tm,
        [pl.BlockSpec((tm, K), lambda i: (i, 0)),
         pl.BlockSpec((K, D), lambda i: (0, 0), pipeline_mode=RESIDENT),
         pl.BlockSpec((tm, D), lambda i: (i, 0))],
        [pl.BlockSpec((tm, D), lambda i: (i, 0))], [SDS((T, D), f32)], [], (a, w, x), "mm_res_down", comm)


def _final_loss(x, gain, tgt, tm):
    T, D = x.shape

    def body(x_ref, g_ref, t_ref, dx_ref, dg_ref, l_ref):
        @pl.when(pl.program_id(0) == 0)
        def _():
            dg_ref[...] = jnp.zeros_like(dg_ref)
            l_ref[...] = jnp.zeros_like(l_ref)
        xv = x_ref[...]
        g = g_ref[...]
        r = _rms(xv)
        xh = xv * r
        err = xh * g - t_ref[...]
        l_ref[...] += 0.5 * jnp.sum(jnp.mean(err * err, axis=-1, keepdims=True), axis=0, keepdims=True)
        dy = err * (1.0 / D)
        dg_ref[...] += jnp.sum(dy * xh, axis=0, keepdims=True)
        dxh = dy * g
        dx_ref[...] = r * (dxh - xh * jnp.mean(dxh * xh, axis=-1, keepdims=True))

    return pl.pallas_call(
        body, grid=(T // tm,),
        in_specs=[pl.BlockSpec((tm, D), lambda i: (i, 0)),
                  pl.BlockSpec((1, D), lambda i: (0, 0)),
                  pl.BlockSpec((tm, D), lambda i: (i, 0))],
        out_specs=[pl.BlockSpec((tm, D), lambda i: (i, 0)),
                   pl.BlockSpec((1, D), lambda i: (0, 0)),
                   pl.BlockSpec((1, BLK), lambda i: (0, 0))],
        out_shape=[SDS((T, D), f32), SDS((1, D), f32), SDS((1, BLK), f32)],
        name="final_loss", compiler_params=_cp("arbitrary"))(x, gain, tgt)


def _dact_dx(dx, wd, gu, w, x, gain, tm, comm=None):
    T, D = dx.shape
    F = wd.shape[0]
    NS, _, ns = w.shape
    half = NS // 2

    def body(dx_ref, wd_ref, gu_ref, w_ref, x_ref, g_ref, dgu_ref, dx1_ref, dg_ref):
        @pl.when(pl.program_id(0) == 0)
        def _():
            dg_ref[...] = jnp.zeros_like(dg_ref)
        dxv = dx_ref[...]
        dxb = _mx(dxv)
        dh = None
        for s in range(half):
            cols = slice(s * ns, (s + 1) * ns)
            da = lax.dot_general(dxb, wd_ref[s * ns:(s + 1) * ns, :], NT, preferred_element_type=f32)
            g = gu_ref[0, :, cols].astype(f32)
            u = gu_ref[1, :, cols].astype(f32)
            sg = jax.nn.sigmoid(g)
            dgs = (da * u * (sg * (1.0 + g * (1.0 - sg)))).astype(bf16)
            dus = (da * (g * sg)).astype(bf16)
            dgu_ref[0, :, cols] = dgs
            dgu_ref[1, :, cols] = dus
            part = (lax.dot_general(_mx(dgs), w_ref[s], NT, preferred_element_type=f32)
                    + lax.dot_general(_mx(dus), w_ref[s + half], NT, preferred_element_type=f32))
            dh = part if dh is None else dh + part
        xv = x_ref[...]
        r = _rms(xv)
        xh = xv * r
        dg_ref[...] += jnp.sum(dh * xh, axis=0, keepdims=True)
        dxh = dh * g_ref[...]
        dx1_ref[...] = dxv + r * (dxh - xh * jnp.mean(dxh * xh, axis=-1, keepdims=True))

    return _hosted_call(
        body, T // tm,
        [pl.BlockSpec((tm, D), lambda i: (i, 0)),
         pl.BlockSpec((F, D), lambda i: (0, 0), pipeline_mode=RESIDENT),
         pl.BlockSpec((2, tm, F), lambda i: (0, i, 0)),
         pl.BlockSpec((NS, D, ns), lambda i: (0, 0, 0), pipeline_mode=RESIDENT),
         pl.BlockSpec((tm, D), lambda i: (i, 0)),
         pl.BlockSpec((1, D), lambda i: (0, 0))],
        [pl.BlockSpec((2, tm, F), lambda i: (0, i, 0)), pl.BlockSpec((tm, D), lambda i: (i, 0)),
         pl.BlockSpec((1, D), lambda i: (0, 0))],
        [SDS((2, T, F), bf16), SDS((T, D), f32), SDS((1, D), f32)], [], (dx, wd, gu, w, x, gain), "dact_dx", comm)


def _dw(a, b, b_map, ns, NS, tka, tk, name):
    T, Ka = a.shape
    b_block = (tk, ns) if b.ndim == 2 else (1, tk, ns)

    def body(a_ref, b_ref, o_ref):
        bv = b_ref[...] if b.ndim == 2 else b_ref[0]
        part = lax.dot_general(_mx(a_ref[...]), _mx(bv), TN, preferred_element_type=f32)

        @pl.when(pl.program_id(2) == 0)
        def _():
            o_ref[0] = part

        @pl.when(pl.program_id(2) > 0)
        def _():
            o_ref[0] += part

    return pl.pallas_call(
        body, grid=(NS, Ka // tka, T // tk),
        in_specs=[pl.BlockSpec((tk, tka), lambda s, k, t: (t, k)),
                  pl.BlockSpec(b_block, lambda s, k, t: b_map(t, s))],
        out_specs=pl.BlockSpec((1, tka, ns), lambda s, k, t: (s, k, 0)),
        out_shape=SDS((NS, Ka, ns), f32),
        name=name, compiler_params=_cp("parallel", "parallel", "arbitrary"))(a, b)


def _dw_pieces(a, pieces, ns, NS, tk, name):
    T, Ka = a.shape
    n = len(pieces)

    def body(*refs):
        a_ref, b_refs, o_ref = refs[0], refs[1:1 + n], refs[1 + n]
        full = jnp.concatenate([r[...] for r in b_refs], axis=1)
        av = _mx(a_ref[...])
        parts = [lax.dot_general(av, _mx(full[:, s * ns:(s + 1) * ns]), TN, preferred_element_type=f32) for s in range(NS)]

        @pl.when(pl.program_id(0) == 0)
        def _():
            for s in range(NS):
                o_ref[s] = parts[s]

        @pl.when(pl.program_id(0) > 0)
        def _():
            for s in range(NS):
                o_ref[s] += parts[s]

    return pl.pallas_call(
        body, grid=(T // tk,),
        in_specs=[pl.BlockSpec((tk, Ka), lambda t: (t, 0))] + [pl.BlockSpec((tk, p.shape[1]), lambda t: (t, 0)) for p in pieces],
        out_specs=pl.BlockSpec((NS, Ka, ns), lambda t: (0, 0, 0)),
        out_shape=SDS((NS, Ka, ns), f32),
        name=name, compiler_params=_cp("arbitrary"))(a, *pieces)


def _dx_norm_bwd(pieces, w, x, gain, dxin, tm, name, comm=None):
    T, D = x.shape
    NS, _, ns = w.shape
    n_dy = len(pieces)

    def body(*refs):
        dy_refs = refs[:n_dy]
        w_ref, x_ref, g_ref, dxin_ref, dx_ref, dg_ref = refs[n_dy:]

        @pl.when(pl.program_id(0) == 0)
        def _():
            dg_ref[...] = jnp.zeros_like(dg_ref)
        full = jnp.concatenate([r[...] for r in dy_refs], axis=1)
        dh = None
        for s in range(NS):
            part = lax.dot_general(_mx(full[:, s * ns:(s + 1) * ns]), w_ref[s], NT, preferred_element_type=f32)
            dh = part if dh is None else dh + part
        xv = x_ref[...]
        r = _rms(xv)
        xh = xv * r
        dg_ref[...] += jnp.sum(dh * xh, axis=0, keepdims=True)
        dxh = dh * g_ref[...]
        dx_ref[...] = dxin_ref[...] + r * (dxh - xh * jnp.mean(dxh * xh, axis=-1, keepdims=True))

    dy_specs = [pl.BlockSpec((tm, p.shape[1]), lambda i: (i, 0)) for p in pieces]
    return _hosted_call(
        body, T // tm,
        dy_specs + [pl.BlockSpec((NS, D, ns), lambda i: (0, 0, 0), pipeline_mode=RESIDENT),
                    pl.BlockSpec((tm, D), lambda i: (i, 0)),
                    pl.BlockSpec((1, D), lambda i: (0, 0)),
                    pl.BlockSpec((tm, D), lambda i: (i, 0))],
        [pl.BlockSpec((tm, D), lambda i: (i, 0)), pl.BlockSpec((1, D), lambda i: (0, 0))],
        [SDS((T, D), f32), SDS((1, D), f32)], [], (*pieces, w, x, gain, dxin), name, comm)


def _dycat(dx, w, ys, gain, tm, comm=None):
    T, D = dx.shape

    def body(dx_ref, w_ref, ya, yb, yc, yd, g_ref, da, db, dc, dd, dg_ref):
        @pl.when(pl.program_id(0) == 0)
        def _():
            dg_ref[...] = jnp.zeros_like(dg_ref)
        dyn = lax.dot_general(_mx(dx_ref[...]), w_ref[...], NT, preferred_element_type=f32)
        for m, (r, o) in enumerate(((ya, da), (yb, db), (yc, dc), (yd, dd))):
            cols = slice(m * GW, (m + 1) * GW)
            y = r[...].astype(f32)
            rs = _rms(y)
            yh = y * rs
            d = dyn[:, cols]
            dg_ref[:, cols] += jnp.sum(d * yh, axis=0, keepdims=True)
            dyh = d * g_ref[:, cols]
            o[...] = (rs * (dyh - yh * jnp.mean(dyh * yh, axis=-1, keepdims=True))).astype(bf16)

    yspec = pl.BlockSpec((tm, GW), lambda i: (i, 0))
    return _hosted_call(
        body, T // tm,
        [pl.BlockSpec((tm, D), lambda i: (i, 0)),
         pl.BlockSpec((D, D), lambda i: (0, 0), pipeline_mode=RESIDENT),
         yspec, yspec, yspec, yspec,
         pl.BlockSpec((1, D), lambda i: (0, 0))],
        [yspec, yspec, yspec, yspec, pl.BlockSpec((1, D), lambda i: (0, 0))],
        [SDS((T, GW), bf16)] * 4 + [SDS((1, D), f32)], [], (dx, w, *ys, gain), "dycat", comm)


def _sgu_consts():
    r, c = _iota((GW, GW), 0), _iota((GW, GW), 1)
    seg = (r // HD == c // HD).astype(f32)
    tr, ts = _iota((BLK, BLK), 0), _iota((BLK, BLK), 1)
    causal = ts <= tr
    lane_head = _iota((BLK, GW), 1) // HD
    return seg, causal, lane_head


def _split3_dot(a, ones):
    hi = a.astype(bf16)
    r1 = a - hi.astype(f32)
    mid = r1.astype(bf16)
    lo = (r1 - mid.astype(f32)).astype(bf16)
    dot = functools.partial(jnp.dot, preferred_element_type=f32)
    return dot(hi, ones) + dot(mid, ones) + dot(lo, ones)


def _sgu_chunks(aus, avs, w, bexp, consts):
    seg, causal, lane_head = consts
    segb = seg.astype(bf16)
    nh = GW // HD
    vs = [jax.nn.gelu(av) for av in avs]
    mus = [_split3_dot(v, segb) * (1.0 / HD) for v in vs]
    vcs = [v - mu for v, mu in zip(vs, mus)]
    vars_ = [_split3_dot(vc * vc, segb) * (1.0 / HD) for vc in vcs]
    vns = [_mx(vc * lax.rsqrt(var + EPS)) for vc, var in zip(vcs, vars_)]
    whs = [_mx(jnp.where(causal, w[h], 0.0)) for h in range(nh)]
    mixes = [[jnp.dot(whs[h], vn, preferred_element_type=f32) for h in range(nh)] for vn in vns]
    out = []
    for au, ms in zip(aus, mixes):
        mix = bexp
        for h in range(nh):
            mix = mix + jnp.where(lane_head == h, ms[h], 0.0)
        out.append(jax.nn.gelu(au) * mix)
    return out


def _sgu_group(S):
    nc = S // BLK
    return 4 if nc % 4 == 0 else (2 if nc % 2 == 0 else 1)


def _sgu_fwd(proj, w, bexp, B, S):
    G = _sgu_group(S)

    def body(au_ref, av_ref, w_ref, b_ref, y_ref):
        consts = _sgu_consts()
        wv, bv = w_ref[...], b_ref[...]

        def group(n, c):
            rows = [pl.ds(pl.multiple_of((n * G + j) * BLK, BLK), BLK) for j in range(G)]
            ys = _sgu_chunks([au_ref[r, :].astype(f32) for r in rows], [av_ref[r, :].astype(f32) for r in rows],
                             wv, bv, consts)
            for r, y in zip(rows, ys):
                y_ref[r, :] = y.astype(bf16)
            return c
        lax.fori_loop(0, S // BLK // G, group, 0)

    return pl.pallas_call(
        body, grid=(B,),
        in_specs=[pl.BlockSpec((S, GW), lambda b: (b, 0)),
                  pl.BlockSpec((S, GW), lambda b: (b, 1)),
                  pl.BlockSpec((GW // HD, BLK, BLK), lambda b: (0, 0, 0)),
                  pl.BlockSpec((BLK, GW), lambda b: (0, 0))],
        out_specs=pl.BlockSpec((S, GW), lambda b: (b, 0)),
        out_shape=SDS((B * S, GW), bf16),
        name="sgu_fwd", compiler_params=_cp("parallel"))(proj, proj, w, bexp)


def _sgu_bwd(proj, w, bexp, dy, B, S, comm=None):
    def body(au_ref, av_ref, w_ref, b_ref, dy_ref, dp_ref, dw_ref, db_ref):
        @pl.when(pl.program_id(0) == 0)
        def _():
            dw_ref[...] = jnp.zeros_like(dw_ref)
            db_ref[...] = jnp.zeros_like(db_ref)
        consts = _sgu_consts()
        wv, bv = w_ref[...], b_ref[...]
        fn = lambda aus, avs, ww, bb: _sgu_chunks(aus, avs, ww, bb, consts)
        G = _sgu_group(S)

        def group(n, carry):
            dw_acc, db_acc = carry
            rows = [pl.ds(pl.multiple_of((n * G + j) * BLK, BLK), BLK) for j in range(G)]
            _, vjp = jax.vjp(fn, [au_ref[r, :].astype(f32) for r in rows], [av_ref[r, :].astype(f32) for r in rows], wv, bv)
            daus, davs, dwc, dbc = vjp([dy_ref[r, :].astype(f32) for r in rows])
            for r, dau, dav in zip(rows, daus, davs):
                dp_ref[r, 0:GW] = dau.astype(bf16)
                dp_ref[r, GW:2 * GW] = dav.astype(bf16)
            return dw_acc + dwc, db_acc + dbc
        dw_acc, db_acc = lax.fori_loop(0, S // BLK // G, group, (jnp.zeros(wv.shape, f32), jnp.zeros(bv.shape, f32)))
        dw_ref[...] += dw_acc
        db_ref[...] += jnp.dot(db_acc, consts[0], precision=HIGHEST, preferred_element_type=f32)

    return _hosted_call(
        body, B,
        [pl.BlockSpec((S, GW), lambda b: (b, 0)),
         pl.BlockSpec((S, GW), lambda b: (b, 1)),
         pl.BlockSpec((GW // HD, BLK, BLK), lambda b: (0, 0, 0)),
         pl.BlockSpec((BLK, GW), lambda b: (0, 0)),
         pl.BlockSpec((S, GW), lambda b: (b, 0))],
        [pl.BlockSpec((S, 2 * GW), lambda b: (b, 0)),
         pl.BlockSpec((GW // HD, BLK, BLK), lambda b: (0, 0, 0)),
         pl.BlockSpec((BLK, GW), lambda b: (0, 0))],
        [SDS((B * S, 2 * GW), bf16), SDS((GW // HD, BLK, BLK), f32), SDS((BLK, GW), f32)], [],
        (proj, proj, w, bexp, dy), "sgu_bwd", comm)


def _pool_parts(p):
    n = p.shape[0]
    r = _iota(p.shape, 0)
    lg = _iota(p.shape, 1) // HD

    def sh(v, k):
        return jnp.where(r >= k, pltpu.roll(v, k, 0), 0.0)
    s2 = p + sh(p, 1)
    s4 = s2 + sh(s2, 2)
    s8 = s4 + sh(s4, 4)
    s16 = s8 + sh(s8, 8)
    ws = jnp.where(lg == 0, s2, jnp.where(lg == 1, s4, jnp.where(lg == 2, s8, s16)))
    wlen = jnp.where(lg == 0, 2, jnp.where(lg == 1, 4, jnp.where(lg == 2, 8, 16)))
    cnt = jnp.minimum(r + 1, wlen).astype(f32)
    del n
    return ws / cnt - p, cnt, lg


def _pool_fwd(proj, wbd, scale, B, S):
    def body(p_ref, w_ref, s_ref, y_ref):
        y, _, _ = _pool_parts(p_ref[...].astype(f32))
        y_ref[...] = (jnp.dot(_mx(y), _mx(w_ref[...]), preferred_element_type=f32) * s_ref[...]).astype(bf16)

    return pl.pallas_call(
        body, grid=(B,),
        in_specs=[pl.BlockSpec((S, GW), lambda b: (b, 2)),
                  pl.BlockSpec((GW, GW), lambda b: (0, 0)),
                  pl.BlockSpec((1, GW), lambda b: (0, 0))],
        out_specs=pl.BlockSpec((S, GW), lambda b: (b, 0)),
        out_shape=SDS((B * S, GW), bf16),
        name="pool_fwd", compiler_params=_cp("parallel"))(proj, wbd, scale)


def _pool_bwd(proj, wbd, scale, dy, B, S):
    def body(p_ref, w_ref, s_ref, dy_ref, dp_ref, dw_ref, ds_ref):
        @pl.when(pl.program_id(0) == 0)
        def _():
            dw_ref[...] = jnp.zeros_like(dw_ref)
            ds_ref[...] = jnp.zeros_like(ds_ref)
        y, cnt, lg = _pool_parts(p_ref[...].astype(f32))
        wv = _mx(w_ref[...])
        z = jnp.dot(_mx(y), wv, preferred_element_type=f32)
        dout = dy_ref[...].astype(f32)
        ds_ref[...] += jnp.sum(dout * z, axis=0, keepdims=True)
        dz = _mx(dout * s_ref[...])
        dw_ref[...] += lax.dot_general(_mx(y), dz, TN, preferred_element_type=f32)
        dyv = lax.dot_general(dz, wv, NT, preferred_element_type=f32)
        n = dyv.shape[0]
        r = _iota(dyv.shape, 0)

        def ush(v, k):
            return jnp.where(r < n - k, pltpu.roll(v, n - k, 0), 0.0)
        gq = dyv / cnt
        a2 = gq + ush(gq, 1)
        a4 = a2 + ush(a2, 2)
        a8 = a4 + ush(a4, 4)
        a16 = a8 + ush(a8, 8)
        adj = jnp.where(lg == 0, a2, jnp.where(lg == 1, a4, jnp.where(lg == 2, a8, a16)))
        dp_ref[...] = (adj - dyv).astype(bf16)

    return pl.pallas_call(
        body, grid=(B,),
        in_specs=[pl.BlockSpec((S, GW), lambda b: (b, 2)),
                  pl.BlockSpec((GW, GW), lambda b: (0, 0)),
                  pl.BlockSpec((1, GW), lambda b: (0, 0)),
                  pl.BlockSpec((S, GW), lambda b: (b, 0))],
        out_specs=[pl.BlockSpec((S, GW), lambda b: (b, 0)),
                   pl.BlockSpec((GW, GW), lambda b: (0, 0)),
                   pl.BlockSpec((1, GW), lambda b: (0, 0))],
        out_shape=[SDS((B * S, GW), bf16), SDS((GW, GW), f32), SDS((1, GW), f32)],
        name="pool_bwd", compiler_params=_cp("arbitrary"))(proj, wbd, scale, dy)


def _t5_bucket_table():
    dist = (np.arange(BLK)[:, None] + BLK) - np.arange(2 * BLK)[None, :]
    d = np.clip(dist, 0, BLK - 1)
    max_exact = N_BUCKETS // 2
    df = np.maximum(d, 1).astype(np.float32)
    large = max_exact + (np.log(df / max_exact) / np.float32(np.log(MAX_DISTANCE / max_exact))
                         * (N_BUCKETS - max_exact)).astype(np.int32)
    large = np.minimum(large, N_BUCKETS - 1)
    return np.where(d < max_exact, d, large).astype(np.int32)


def _swa_blocks(qs, kx, vx, sinks, biases, n):
    G = len(qs)
    heads = [(p, g) for p in range(2) for g in range(2)]
    ri, ci = _iota((BLK, BLK), 0), _iota((BLK, BLK), 1)
    qi, ki = _iota((BLK, 2 * BLK), 0), _iota((BLK, 2 * BLK), 1)
    dist = qi + BLK - ki
    band = (dist >= 0) & (dist < BLK)
    masks = [band & ((ki >= BLK) | (n > 0))] + [band] * (G - 1)
    kb, vb = _mx(kx), _mx(vx)
    qsel = [[None] * 4 for _ in range(G)]
    vs = []
    for h, (p, g) in enumerate(heads):
        selq = ((ri - g * HD == ci - p * HD) & (ri >= g * HD) & (ri < (g + 1) * HD)).astype(_MXU)
        selv = ((ci - g * HD == ri - p * HD) & (ci >= g * HD) & (ci < (g + 1) * HD)).astype(_MXU)
        for b in range(G):
            qsel[b][h] = _mx(jnp.dot(_mx(qs[b][p]), selq, preferred_element_type=f32))
        vs.append(_mx(jnp.dot(vb, selv, preferred_element_type=f32)))
    zs = [[lax.dot_general(qsel[b][h], kb[b * BLK:(b + 2) * BLK], NT, preferred_element_type=f32) * (HD ** -0.5)
           for h in range(4)] for b in range(G)]
    prs = [[None] * 4 for _ in range(G)]
    for b in range(G):
        for h in range(4):
            z = jnp.where(masks[b], zs[b][h] + biases[h], -1e30)
            s = jnp.mean(sinks[h], axis=-1, keepdims=True)
            m = jnp.maximum(jnp.max(z, axis=-1, keepdims=True), s)
            e = jnp.exp(z - m)
            prs[b][h] = _mx(e / (jnp.sum(e, axis=-1, keepdims=True) + jnp.exp(s - m)))
    outs = [[jnp.dot(prs[b][h], vs[h][b * BLK:(b + 2) * BLK], preferred_element_type=f32) for h in range(4)]
            for b in range(G)]
    return [[o[0] + o[1], o[2] + o[3]] for o in outs]


def _swa_group(S):
    return 2 if (S // BLK) % 2 == 0 else 1


def _swa_rows(n, G):
    blk = lambda j: pl.ds(pl.multiple_of(j * BLK, BLK), BLK)
    return [blk(jnp.maximum(n - 1, 0))] + [blk(n + b) for b in range(G)]


def _swa_fwd(proj, sinks, bias, B, S, comm=None):
    G = _swa_group(S)

    def body(q_ref, kv_ref, s_ref, b_ref, y_ref):
        def group(i, c):
            n = i * G
            rows = _swa_rows(n, G)
            kx = jnp.concatenate([kv_ref[r, 0:BLK] for r in rows], axis=0).astype(f32)
            vx = jnp.concatenate([kv_ref[r, BLK:2 * BLK] for r in rows], axis=0).astype(f32)
            qs = [[q_ref[r, 0:BLK].astype(f32), q_ref[r, BLK:2 * BLK].astype(f32)] for r in rows[1:]]
            outs = _swa_blocks(qs, kx, vx, [s_ref[h] for h in range(4)], [b_ref[h] for h in range(4)], n)
            for r, (o0, o1) in zip(rows[1:], outs):
                y_ref[r, 0:BLK] = o0.astype(bf16)
                y_ref[r, BLK:2 * BLK] = o1.astype(bf16)
            return c
        lax.fori_loop(0, S // BLK // G, group, 0)

    return _hosted_call(
        body, B,
        [pl.BlockSpec((S, GW), lambda b: (b, 3)),
         pl.BlockSpec((S, GW), lambda b: (b, 4)),
         pl.BlockSpec((4, 1, BLK), lambda b: (0, 0, 0)),
         pl.BlockSpec((4, BLK, 2 * BLK), lambda b: (0, 0, 0))],
        [pl.BlockSpec((S, GW), lambda b: (b, 0))], [SDS((B * S, GW), bf16)], [],
        (proj, proj, sinks, bias), "swa_fwd", comm)


def _swa_bwd(proj, sinks, bias, dy, B, S, comm=None):
    def body(q_ref, kv_ref, s_ref, b_ref, dy_ref, dq_ref, dkv_ref, ds_ref, db_ref, acc_ref):
        @pl.when(pl.program_id(0) == 0)
        def _():
            ds_ref[...] = jnp.zeros_like(ds_ref)
            db_ref[...] = jnp.zeros_like(db_ref)
        acc_ref[...] = jnp.zeros_like(acc_ref)

        G = _swa_group(S)

        def group(i, c):
            n = i * G
            rows = _swa_rows(n, G)
            kx = jnp.concatenate([kv_ref[r, 0:BLK] for r in rows], axis=0).astype(f32)
            vx = jnp.concatenate([kv_ref[r, BLK:2 * BLK] for r in rows], axis=0).astype(f32)
            qs = [[q_ref[r, 0:BLK].astype(f32), q_ref[r, BLK:2 * BLK].astype(f32)] for r in rows[1:]]
            dos = [[dy_ref[r, 0:BLK].astype(f32), dy_ref[r, BLK:2 * BLK].astype(f32)] for r in rows[1:]]
            fn = functools.partial(_swa_blocks, n=n)
            _, vjp = jax.vjp(fn, qs, kx, vx, [s_ref[h] for h in range(4)], [b_ref[h] for h in range(4)])
            dqs, dkx, dvx, dss, dbs = vjp(dos)
            for r, (dq0, dq1) in zip(rows[1:], dqs):
                dq_ref[r, 0:BLK] = dq0.astype(bf16)
                dq_ref[r, BLK:2 * BLK] = dq1.astype(bf16)
            for h in range(4):
                ds_ref[h] += dss[h]
                db_ref[h] += dbs[h]
            for j, r in enumerate(rows):
                acc_ref[r, 0:BLK] += dkx[j * BLK:(j + 1) * BLK]
                acc_ref[r, BLK:2 * BLK] += dvx[j * BLK:(j + 1) * BLK]
            return c
        lax.fori_loop(0, S // BLK // G, group, 0)
        dkv_ref[...] = acc_ref[...].astype(bf16)

    c_args, c_in, c_out, c_shapes, aliases, c_scratch = _host_specs(comm, 5, 4)
    step = lambda v: (lambda: pl.program_id(0) == v)
    return pl.pallas_call(
        _host(body, 5, 4, 1, comm, step(0), step(B - 1)), grid=(B,),
        in_specs=[pl.BlockSpec((S, GW), lambda b: (b, 3)),
                  pl.BlockSpec((S, GW), lambda b: (b, 4)),
                  pl.BlockSpec((4, 1, BLK), lambda b: (0, 0, 0)),
                  pl.BlockSpec((4, BLK, 2 * BLK), lambda b: (0, 0, 0)),
                  pl.BlockSpec((S, GW), lambda b: (b, 0))] + c_in,
        out_specs=[pl.BlockSpec((S, GW), lambda b: (b, 0)),
                   pl.BlockSpec((S, GW), lambda b: (b, 0)),
                   pl.BlockSpec((4, 1, BLK), lambda b: (0, 0, 0)),
                   pl.BlockSpec((4, BLK, 2 * BLK), lambda b: (0, 0, 0))] + c_out,
        out_shape=[SDS((B * S, GW), bf16), SDS((B * S, GW), bf16), SDS((4, 1, BLK), f32),
                   SDS((4, BLK, 2 * BLK), f32)] + c_shapes,
        input_output_aliases=aliases, scratch_shapes=[pltpu.VMEM((S, GW), f32)] + c_scratch,
        name="swa_bwd" if comm is None else "swa_bwd_exchange",
        compiler_params=_cp("arbitrary"))(proj, proj, sinks, bias, dy, *c_args)


def _log1m_parts(z):
    t = jnp.exp(-jnp.abs(z))
    return jnp.minimum(-z, 0.0) - jnp.log(1.0 + t), t


def _log1m(z):
    return _log1m_parts(z)[0]


def _sigmoid_from(z, t):
    return jnp.where(z >= 0.0, 1.0, t) / (1.0 + t)


def _sb_consts(tri):
    r2, c2 = _iota((2 * BLK, 2 * BLK), 0), _iota((2 * BLK, 2 * BLK), 1)
    tri2 = (tri(r2, c2) & (r2 // BLK == c2 // BLK)).astype(bf16)
    ri, ci = _iota((BLK, 2 * BLK), 0), _iota((BLK, 2 * BLK), 1)
    strict2 = (ci % BLK) < ri
    head0 = _iota((BLK, BLK), 1) < HD
    return tri2, strict2, head0


def _sb_stack_kv(k_ref, v_ref, kst_ref, vst_ref, head0, nb):
    def one(kb, c):
        krows = pl.ds(pl.multiple_of(kb * BLK, BLK), BLK)
        for p in range(2):
            for src, dst in ((k_ref, kst_ref), (v_ref, vst_ref)):
                t = src[krows, p * BLK:(p + 1) * BLK]
                dst[p, kb] = _mx(jnp.concatenate([jnp.where(head0, t, 0.0), jnp.where(head0, 0.0, t)], axis=0))
        return c
    lax.fori_loop(0, nb, one, 0)


def _sb_load_kv(kst_ref, vst_ref, kb):
    return [kst_ref[p, kb] for p in range(2)], [vst_ref[p, kb] for p in range(2)]


def _two_halves(a, b):
    return jnp.concatenate([jnp.broadcast_to(a, (BLK, BLK)), jnp.broadcast_to(b, (BLK, BLK))], axis=1)


def _half_sums(t):
    return jnp.sum(t[:, :BLK], axis=-1, keepdims=True), jnp.sum(t[:, BLK:], axis=-1, keepdims=True)


def _sb_fwd(proj, B, S, comm=None):
    def body(q_ref, k_ref, v_ref, y_ref, lt_ref, kst_ref, vst_ref):
        ci = _iota((BLK, BLK), 1)
        above2, strict2, head0 = _sb_consts(lambda r, c: r > c)
        _sb_stack_kv(k_ref, v_ref, kst_ref, vst_ref, head0, S // BLK)

        def step(qs, kbs, diags, carry):
            U = range(len(kbs))
            ok = [None if diags[u] else kbs[u] >= 0 for u in U]
            kv = [_sb_load_kv(kst_ref, vst_ref, jnp.maximum(kb, 0)) for kb in kbs]
            zs = [[lax.dot_general(qs[p], kks[p], NT, preferred_element_type=f32) for p in range(2)] for kks, _ in kv]
            Ls = [[jnp.where(strict2, _log1m(z), 0.0) if diags[u] else _log1m(z) for z in zs[u]] for u in U]
            tails = [[_split_dot(L, above2) for L in Lu] for Lu in Ls]
            carry = list(carry)
            for u in U:
                for p in range(2):
                    R0, R1, acc = carry[3 * p:3 * p + 3]
                    w = jnp.exp(zs[u][p] + Ls[u][p] + tails[u][p] + _two_halves(R0, R1))
                    s0, s1 = _half_sums(Ls[u][p])
                    if diags[u]:
                        w = jnp.where(strict2, w, 0.0)
                    else:
                        w, s0, s1 = (jnp.where(ok[u], t, 0.0) for t in (w, s0, s1))
                    acc = acc + jnp.dot(_mx(w), kv[u][1][p], preferred_element_type=f32)
                    carry[3 * p:3 * p + 3] = [R0 + s0, R1 + s1, acc]
            return tuple(carry)

        def qblock(n, c):
            qrows = pl.ds(pl.multiple_of(n * BLK, BLK), BLK)
            qs = [_mx(q_ref[qrows, p * BLK:(p + 1) * BLK] * (HD ** -0.5)) for p in range(2)]
            z1, z2 = jnp.zeros((BLK, 1), f32), jnp.zeros((BLK, BLK), f32)
            near = [n - 1 - u for u in range(SB_UNROLL)]
            carry = step(qs, [n] + near, [True] + [False] * SB_UNROLL, (z1, z1, z2, z1, z1, z2))
            far = jnp.maximum(n - SB_UNROLL, 0)
            trips = (far + SB_UNROLL - 1) // SB_UNROLL

            def live(st):
                worst = jnp.maximum(jnp.maximum(st[1], st[2]), jnp.maximum(st[4], st[5]))
                return (st[0] < trips) & (jnp.max(worst) > SB_CUT)

            def trip(st):
                i = st[0]
                kbs = [far - 1 - SB_UNROLL * i - u for u in range(SB_UNROLL)]
                return (i + 1,) + step(qs, kbs, [False] * SB_UNROLL, st[1:])
            done, *res = lax.while_loop(live, trip, (jnp.int32(0),) + carry)
            lt = jnp.where(ci == SB_HEADS, done.astype(f32), 0.0)
            for p in range(2):
                y_ref[qrows, p * BLK:(p + 1) * BLK] = res[3 * p + 2].astype(bf16)
                lt = lt + jnp.where(ci == 2 * p, res[3 * p], 0.0) + jnp.where(ci == 2 * p + 1, res[3 * p + 1], 0.0)
            lt_ref[qrows, :] = lt
            return c
        lax.fori_loop(0, S // BLK, qblock, 0)

    spec = lambda j: pl.BlockSpec((S, GW), lambda b: (b, j))
    c_args, c_in, c_out, c_shapes, aliases, c_scratch = _host_specs(comm, 3, 2)
    step = lambda v: (lambda: pl.program_id(0) == v)
    stacked = pltpu.VMEM((2, S // BLK, 2 * BLK, BLK), _MXU)
    return pl.pallas_call(
        _host(body, 3, 2, 2, comm, step(0), step(B - 1)), grid=(B,),
        in_specs=[spec(5), spec(6), spec(7)] + c_in,
        out_specs=[pl.BlockSpec((S, GW), lambda b: (b, 0)), pl.BlockSpec((S, BLK), lambda b: (b, 0))] + c_out,
        out_shape=[SDS((B * S, GW), bf16), SDS((B * S, BLK), f32)] + c_shapes,
        input_output_aliases=aliases, scratch_shapes=[stacked, stacked] + c_scratch,
        name="sb_fwd" if comm is None else "sb_fwd_gather",
        compiler_params=_cp("arbitrary"))(proj, proj, proj, *c_args)


def _sb_bwd(proj, ltot, dy, B, S, comm=None):
    def body(q_ref, k_ref, v_ref, lt_ref, dy_ref, dq_ref, dk_ref, dv_ref, dka_ref, dva_ref, kst_ref, vst_ref):
        ci = _iota((BLK, BLK), 1)
        upto2, strict2, head0 = _sb_consts(lambda r, c: r <= c)
        below2, _, _ = _sb_consts(lambda r, c: r < c)
        dka_ref[...] = jnp.zeros_like(dka_ref)
        dva_ref[...] = jnp.zeros_like(dva_ref)
        _sb_stack_kv(k_ref, v_ref, kst_ref, vst_ref, head0, S // BLK)

        def step(qs, dos, lts, kbs, ok, diags, top, carry):
            U = range(len(kbs))
            kbs = [jnp.clip(kb, 0, top) for kb in kbs]
            kv = [_sb_load_kv(kst_ref, vst_ref, kb) for kb in kbs]
            zs = [[lax.dot_general(qs[p], kv[u][0][p], NT, preferred_element_type=f32) for p in range(2)] for u in U]
            dws = [[lax.dot_general(dos[p], kv[u][1][p], NT, preferred_element_type=f32) for p in range(2)] for u in U]
            parts = [[_log1m_parts(z) for z in zu] for zu in zs]
            Ls = [[jnp.where(strict2, lt[0], 0.0) if diags[u] else lt[0] for lt in parts[u]] for u in U]
            pins = [[_split_dot(L, upto2) for L in Lu] for Lu in Ls]
            carry = list(carry)
            ws, das = [], []
            for u in U:
                wu, dau = [], []
                for p in range(2):
                    PL0, PL1 = carry[5 * p], carry[5 * p + 1]
                    tail = _two_halves(lts[2 * p] - PL0, lts[2 * p + 1] - PL1) - pins[u][p]
                    w = jnp.exp(zs[u][p] + Ls[u][p] + tail)
                    l0, l1 = _half_sums(Ls[u][p])
                    if diags[u]:
                        w = jnp.where(strict2, w, 0.0)
                    else:
                        w, l0, l1 = (jnp.where(ok[u], t, 0.0) for t in (w, l0, l1))
                    carry[5 * p], carry[5 * p + 1] = PL0 + l0, PL1 + l1
                    wu.append(w)
                    dau.append(w * dws[u][p])
                ws.append(wu)
                das.append(dau)
            pexs = [[_split_dot(da, below2) for da in dau] for dau in das]
            dzs = []
            for u in U:
                dzu = []
                for p in range(2):
                    dL = _two_halves(carry[5 * p + 2], carry[5 * p + 3]) + pexs[u][p]
                    sg = _sigmoid_from(zs[u][p], parts[u][p][1])
                    dz = das[u][p] * (1.0 - sg) - dL * sg
                    dz = jnp.where(strict2 if diags[u] else ok[u], dz, 0.0)
                    a0, a1 = _half_sums(das[u][p])
                    carry[5 * p + 2], carry[5 * p + 3] = carry[5 * p + 2] + a0, carry[5 * p + 3] + a1
                    dzu.append(_mx(dz))
                dzs.append(dzu)
            dqs = [[jnp.dot(dzs[u][p], kv[u][0][p], preferred_element_type=f32) for p in range(2)] for u in U]
            dks = [[lax.dot_general(dzs[u][p], qs[p], TN, preferred_element_type=f32) for p in range(2)] for u in U]
            dvs = [[lax.dot_general(_mx(ws[u][p]), dos[p], TN, preferred_element_type=f32) for p in range(2)] for u in U]
            for u in U:
                krows = pl.ds(pl.multiple_of(kbs[u] * BLK, BLK), BLK)
                for p in range(2):
                    lanes = slice(p * BLK, (p + 1) * BLK)
                    dka_ref[krows, lanes] += jnp.where(head0, dks[u][p][:BLK], dks[u][p][BLK:])
                    dva_ref[krows, lanes] += jnp.where(head0, dvs[u][p][:BLK], dvs[u][p][BLK:])
                    carry[5 * p + 4] = carry[5 * p + 4] + dqs[u][p]
            return tuple(carry)

        def qblock(n, c):
            qrows = pl.ds(pl.multiple_of(n * BLK, BLK), BLK)
            ltb = lt_ref[qrows, :]
            lts = [jnp.sum(jnp.where(ci == h, ltb, 0.0), axis=-1, keepdims=True) for h in range(4)]
            qs = [_mx(q_ref[qrows, p * BLK:(p + 1) * BLK] * (HD ** -0.5)) for p in range(2)]
            dos = [_mx(dy_ref[qrows, p * BLK:(p + 1) * BLK]) for p in range(2)]
            z1, z2 = jnp.zeros((BLK, 1), f32), jnp.zeros((BLK, BLK), f32)
            done = jnp.max(jnp.where(ci == SB_HEADS, ltb, 0.0)).astype(jnp.int32)
            far = jnp.maximum(n - SB_UNROLL, 0)
            first = jnp.maximum(far - SB_UNROLL * done, 0)

            def trip(i, cr):
                kbs = [first + SB_UNROLL * i + u for u in range(SB_UNROLL)]
                return step(qs, dos, lts, kbs, [kb < far for kb in kbs], [False] * SB_UNROLL, n, cr)
            carry = lax.fori_loop(0, (far - first + SB_UNROLL - 1) // SB_UNROLL, trip, (z1, z1, z1, z1, z2) * 2)
            near = [n - SB_UNROLL + u for u in range(SB_UNROLL)]
            res = step(qs, dos, lts, near + [n], [kb >= 0 for kb in near] + [None], [False] * SB_UNROLL + [True], n, carry)
            for p in range(2):
                dq_ref[qrows, p * BLK:(p + 1) * BLK] = (res[5 * p + 4] * (HD ** -0.5)).astype(bf16)
            return c
        lax.fori_loop(0, S // BLK, qblock, 0)
        dk_ref[...] = dka_ref[...].astype(bf16)
        dv_ref[...] = dva_ref[...].astype(bf16)

    spec = lambda j: pl.BlockSpec((S, GW), lambda b: (b, j))
    o = pl.BlockSpec((S, GW), lambda b: (b, 0))
    c_args, c_in, c_out, c_shapes, aliases, c_scratch = _host_specs(comm, 5, 3)
    step = lambda v: (lambda: pl.program_id(0) == v)
    stacked = pltpu.VMEM((2, S // BLK, 2 * BLK, BLK), _MXU)
    return pl.pallas_call(
        _host(body, 5, 3, 4, comm, step(0), step(B - 1)), grid=(B,),
        in_specs=[spec(5), spec(6), spec(7), pl.BlockSpec((S, BLK), lambda b: (b, 0)), o] + c_in,
        out_specs=[o, o, o] + c_out,
        out_shape=[SDS((B * S, GW), bf16)] * 3 + c_shapes,
        input_output_aliases=aliases,
        scratch_shapes=[pltpu.VMEM((S, GW), f32), pltpu.VMEM((S, GW), f32), stacked, stacked] + c_scratch,
        name="sb_bwd" if comm is None else "sb_bwd_exchange",
        compiler_params=_cp("arbitrary"))(proj, proj, proj, ltot, dy, *c_args)


def _bias_expand(rel_bias_t, bucket):
    n = bucket.shape[1]

    def body(r_ref, b_ref, o_ref):
        onehot = (_iota((N_BUCKETS, n), 0) == b_ref[...]).astype(f32)
        o_ref[...] = jnp.dot(r_ref[...], onehot, precision=HIGHEST, preferred_element_type=f32)
    return pl.pallas_call(body, out_shape=SDS((rel_bias_t.shape[0], n), f32), name="bias_expand",
                          compiler_params=_cp())(rel_bias_t, bucket)


def _bias_reduce(dbias, bucket):
    n = bucket.shape[1]

    def body(*refs):
        b_ref, g_ref = refs[-2], refs[-1]
        d = refs[0][...]
        for r in refs[1:-2]:
            d = d + r[...]
        onehot = (_iota((N_BUCKETS, n), 0) == b_ref[...]).astype(f32)
        g_ref[...] = lax.dot_general(d, onehot, NT, precision=HIGHEST, preferred_element_type=f32)
    return pl.pallas_call(body, out_shape=SDS((dbias[0].shape[0], N_BUCKETS), f32), name="bias_reduce",
                          compiler_params=_cp())(*dbias, bucket)


def _adamw(w, g, m, v, tr, name, emit_g=False):
    R, C = w.shape

    def body(w_ref, g_ref, m_ref, v_ref, d_ref, m2_ref, v2_ref, *g_out):
        gv = g_ref[...]
        if emit_g:
            g_out[0][...] = gv
        m2 = ADAM_B1 * m_ref[...] + (1.0 - ADAM_B1) * gv
        v2 = ADAM_B2 * v_ref[...] + (1.0 - ADAM_B2) * (gv * gv)
        m_hat = m2 / (1.0 - ADAM_B1 ** ADAM_STEP)
        v_hat = v2 / (1.0 - ADAM_B2 ** ADAM_STEP)
        d_ref[...] = -ADAM_LR * (m_hat / (jnp.sqrt(v_hat) + ADAM_EPS) + ADAM_WD * w_ref[...])
        m2_ref[...] = m2
        v2_ref[...] = v2

    spec = pl.BlockSpec((tr, C), lambda i: (i, 0))
    n_out = 4 if emit_g else 3
    return pl.pallas_call(
        body, grid=(R // tr,), in_specs=[spec] * 4, out_specs=[spec] * n_out,
        out_shape=[SDS((R, C), f32)] * n_out, name=name, compiler_params=_cp("parallel"))(w, g, m, v)


ANY = pl.BlockSpec(memory_space=pl.ANY)


def _place():
    x, y, c = lax.axis_index("x"), lax.axis_index("y"), lax.axis_index("c")
    chips = [(1 - x, y), (x, 1 - y), (1 - x, 1 - y)]
    return x, y, c, chips


def _cast_slots(w, kidx):
    L, a, b = w.shape
    ta = a // 2

    def body(k_ref, *refs):
        for l in range(L):
            refs[L + l][0] = refs[l][0].astype(bf16)

    return pl.pallas_call(
        body,
        grid_spec=pltpu.PrefetchScalarGridSpec(
            num_scalar_prefetch=1, grid=(a // ta,),
            in_specs=[pl.BlockSpec((1, ta, b), functools.partial(lambda i, k_ref, l: (l, i, 0), l=l)) for l in range(L)],
            out_specs=[pl.BlockSpec((1, ta, b), lambda i, k_ref: (k_ref[0], i, 0)) for _ in range(L)]),
        out_shape=[SDS((N_CHIPS, a, b), bf16)] * L,
        name="cast_slots", compiler_params=_cp("parallel"))(kidx, *([w] * L))


class _GatherComm:
    def __init__(self, bufs, part=0, parts=1):
        self.inputs = list(bufs)
        self.out_shape = [SDS(b.shape, b.dtype) for b in bufs]
        self.aliased = True
        self.scratch = [pltpu.SemaphoreType.DMA((3 * len(bufs),))] * 4
        self.part, self.parts = part, parts

    def _copies(self, i_refs, o_refs, sems):
        send1, recv1, send2, recv2 = sems
        x, y, c, chips = _place()
        k = 2 * x + y
        first, got1, second, got2 = [], [], [], []
        for i, buf in enumerate(self.inputs):
            h = buf.shape[1] // 2
            n = h // self.parts
            mine, theirs = pl.ds(c * h + self.part * n, n), pl.ds((1 - c) * h + self.part * n, n)
            for j, (cx, cy) in enumerate(chips):
                s = 3 * i + j
                first.append(pltpu.make_async_remote_copy(
                    src_ref=i_refs[i].at[k, mine], dst_ref=o_refs[i].at[k, mine], send_sem=send1.at[s],
                    recv_sem=recv1.at[s], device_id=(cx, cy, c), device_id_type=MESH))
                a = o_refs[i].at[2 * cx + cy, mine]
                got1.append(pltpu.make_async_remote_copy(
                    src_ref=a, dst_ref=a, send_sem=send1.at[s], recv_sem=recv1.at[s],
                    device_id=(cx, cy, c), device_id_type=MESH))
                second.append(pltpu.make_async_remote_copy(
                    src_ref=a, dst_ref=a, send_sem=send2.at[s], recv_sem=recv2.at[s],
                    device_id=(x, y, 1 - c), device_id_type=MESH))
                b = o_refs[i].at[2 * cx + cy, theirs]
                got2.append(pltpu.make_async_remote_copy(
                    src_ref=b, dst_ref=b, send_sem=send2.at[s], recv_sem=recv2.at[s],
                    device_id=(x, y, 1 - c), device_id_type=MESH))
        return first, got1, second, got2

    def start(self, i_refs, o_refs, sems):
        for cp in self._copies(i_refs, o_refs, sems)[0]:
            cp.start()

    def finish(self, i_refs, o_refs, sems):
        first, got1, second, got2 = self._copies(i_refs, o_refs, sems)
        for g, cp in zip(got1, second):
            g.wait_recv()
            cp.start()
        for g in got2:
            g.wait_recv()
        for cp in first + second:
            cp.wait_send()


class _MultiComm:
    def __init__(self, comms):
        self.comms = comms
        self.inputs = [a for c in comms for a in c.inputs]
        self.out_shape = [s for c in comms for s in c.out_shape]
        self.aliased = comms[0].aliased
        assert all(c.aliased == self.aliased for c in comms)
        self.scratch = [s for c in comms for s in c.scratch]

    def _split(self, i_refs, o_refs, sems):
        i = o = s = 0
        for c in self.comms:
            ni, no, ns = len(c.inputs), len(c.out_shape), len(c.scratch)
            yield c, i_refs[i:i + ni], o_refs[o:o + no], sems[s:s + ns]
            i, o, s = i + ni, o + no, s + ns

    def start(self, i_refs, o_refs, sems):
        for c, i, o, s in self._split(i_refs, o_refs, sems):
            c.start(i, o, s)

    def finish(self, i_refs, o_refs, sems):
        for c, i, o, s in self._split(i_refs, o_refs, sems):
            c.finish(i, o, s)


class _PairExchangeComm:
    def __init__(self, gs):
        self.inputs = list(gs)
        self.out_shape = [SDS((g.shape[0], g.shape[1] // 2, g.shape[2]), g.dtype) for g in gs]
        self.aliased = False
        self.scratch = [pltpu.SemaphoreType.DMA((len(gs),))] * 2

    def _copies(self, i_refs, o_refs, sems):
        send, recv = sems
        x, y, c, _ = _place()
        cps = []
        for i, g in enumerate(self.inputs):
            h = g.shape[1] // 2
            cps.append(pltpu.make_async_remote_copy(
                src_ref=i_refs[i].at[:, pl.ds((1 - c) * h, h)], dst_ref=o_refs[i], send_sem=send.at[i], recv_sem=recv.at[i],
                device_id=(x, y, 1 - c), device_id_type=MESH))
        return cps

    def start(self, i_refs, o_refs, sems):
        for cp in self._copies(i_refs, o_refs, sems):
            cp.start()

    def finish(self, i_refs, o_refs, sems):
        for cp in self._copies(i_refs, o_refs, sems):
            cp.wait()


class _ChipExchangeComm:
    def __init__(self, qs):
        self.inputs = list(qs)
        self.out_shape = [SDS(q.shape, q.dtype) for q in qs]
        self.aliased = False
        self.scratch = [pltpu.SemaphoreType.DMA((3 * len(qs),))] * 2

    def _copies(self, i_refs, o_refs, sems):
        send, recv = sems
        x, y, c, chips = _place()
        k = 2 * x + y
        cps, got = [], []
        for i in range(len(self.inputs)):
            for j, (cx, cy) in enumerate(chips):
                s = 3 * i + j
                cps.append(pltpu.make_async_remote_copy(
                    src_ref=i_refs[i].at[2 * cx + cy], dst_ref=o_refs[i].at[k], send_sem=send.at[s],
                    recv_sem=recv.at[s], device_id=(cx, cy, c), device_id_type=MESH))
                a = o_refs[i].at[2 * cx + cy]
                got.append(pltpu.make_async_remote_copy(
                    src_ref=a, dst_ref=a, send_sem=send.at[s], recv_sem=recv.at[s],
                    device_id=(cx, cy, c), device_id_type=MESH))
        return cps, got

    def start(self, i_refs, o_refs, sems):
        for cp in self._copies(i_refs, o_refs, sems)[0]:
            cp.start()

    def finish(self, i_refs, o_refs, sems):
        cps, got = self._copies(i_refs, o_refs, sems)
        for g in got:
            g.wait_recv()
        for cp in cps:
            cp.wait_send()


def _comm_only(comm, name):
    n = len(comm.inputs)

    def body(*refs):
        i_refs, o_refs, sems = refs[:n], refs[n:n + len(comm.out_shape)], refs[n + len(comm.out_shape):]
        comm.start(i_refs, o_refs, sems)
        comm.finish(i_refs, o_refs, sems)

    return pl.pallas_call(
        body, out_shape=comm.out_shape, in_specs=[ANY] * n, out_specs=[ANY] * len(comm.out_shape),
        input_output_aliases={i: i for i in range(n)} if comm.aliased else {},
        scratch_shapes=comm.scratch, name=name,
        compiler_params=pltpu.CompilerParams(has_side_effects=True))(*comm.inputs)


def _host(body, n_in, n_out, n_scratch, comm, first, last):
    if comm is None:
        return body
    ci, co = len(comm.inputs), len(comm.out_shape)

    def wrapped(*refs):
        o = 0
        parts = []
        for n in (n_in, ci, n_out, co, n_scratch):
            parts.append(refs[o:o + n])
            o += n
        hin, cin, hout, cout, hs = parts
        sems = refs[o:]

        @pl.when(first())
        def _():
            comm.start(cin, cout, sems)
        body(*hin, *hout, *hs)

        @pl.when(last())
        def _():
            comm.finish(cin, cout, sems)
    return wrapped


def _host_specs(comm, n_in, n_out):
    if comm is None:
        return [], [], [], [], {}, []
    ci, co = len(comm.inputs), len(comm.out_shape)
    aliases = {n_in + i: n_out + i for i in range(ci)} if comm.aliased else {}
    return comm.inputs, [ANY] * ci, [ANY] * co, comm.out_shape, aliases, comm.scratch


def _pair_add(g, r, kc, name):
    ns, a, b = g.shape
    h = a // 2
    th = h if h * b * 4 <= 4 * 1024 * 1024 else h // 2

    def body(kc_ref, g_ref, r_ref, qb_ref):
        qb_ref[...] = (g_ref[...] + r_ref[...]).astype(bf16)

    nb = h // th
    slot = lambda j, kc_ref: (kc_ref[0] + 1 + j) % ns
    spec = pl.BlockSpec((1, th, b), lambda j, i, kc_ref: (slot(j, kc_ref), i, 0))
    return pl.pallas_call(
        body,
        grid_spec=pltpu.PrefetchScalarGridSpec(
            num_scalar_prefetch=1, grid=(ns - 1, nb),
            in_specs=[pl.BlockSpec((1, th, b), lambda j, i, kc_ref: (slot(j, kc_ref), kc_ref[1] * nb + i, 0)), spec],
            out_specs=spec),
        out_shape=SDS((ns, h, b), bf16),
        name=name, compiler_params=_cp("parallel", "parallel"))(kc, g, r)


def _chip_add(g, r1, r2, idx, prev, L, name):
    ns, h, b = r2.shape
    th = h if h * b * 4 <= 4 * 1024 * 1024 else h // 2
    nb = h // th

    def body(s_ref, g_ref, r1_ref, a_ref, b_ref, c_ref, *rest):
        o_ref = rest[-1]
        o_ref[0] = (g_ref[0] + r1_ref[0]) + a_ref[0].astype(f32) + b_ref[0].astype(f32) + c_ref[0].astype(f32)

    other = lambda d: pl.BlockSpec((1, th, b), lambda i, s_ref: ((s_ref[0] + d) % ns, i, 0))
    in_specs = [pl.BlockSpec((1, th, b), lambda i, s_ref: (s_ref[0], s_ref[1] * nb + i, 0)),
                pl.BlockSpec((1, th, b), lambda i, s_ref: (s_ref[0], i, 0)), other(1), other(2), other(3)]
    args = [idx, g, r1, r2, r2, r2]
    aliases = {}
    if prev is not None:
        in_specs.append(ANY)
        args.append(prev)
        aliases = {6: 0}
    return pl.pallas_call(
        body,
        grid_spec=pltpu.PrefetchScalarGridSpec(
            num_scalar_prefetch=1, grid=(nb,), in_specs=in_specs,
            out_specs=pl.BlockSpec((1, th, b), lambda i, s_ref: (s_ref[2], s_ref[1] * nb + i, 0))),
        out_shape=SDS((L, 2 * h, b), f32), input_output_aliases=aliases,
        name=name, compiler_params=_cp("arbitrary"))(*args)


def _pair_share(gs, hs):
    n = len(gs)
    L = gs[0].shape[0]

    def body(*refs):
        i_refs, o_refs = refs[:n], refs[n:2 * n]
        send, recv = refs[2 * n:]
        x, y, c, _ = _place()
        cps = []
        for i in range(n):
            for l in range(L):
                mine = pl.ds(c * hs[i], hs[i])
                cp = pltpu.make_async_remote_copy(
                    src_ref=i_refs[i].at[l, mine], dst_ref=o_refs[i].at[l, mine], send_sem=send.at[i * L + l],
                    recv_sem=recv.at[i * L + l], device_id=(x, y, 1 - c), device_id_type=MESH)
                cp.start()
                cps.append(cp)
        for i in range(n):
            for l in range(L):
                got = o_refs[i].at[l, pl.ds((1 - c) * hs[i], hs[i])]
                pltpu.make_async_remote_copy(
                    src_ref=got, dst_ref=got, send_sem=send.at[i * L + l], recv_sem=recv.at[i * L + l],
                    device_id=(x, y, 1 - c), device_id_type=MESH).wait_recv()
        for cp in cps:
            cp.wait_send()

    return pl.pallas_call(
        body, out_shape=[SDS(g.shape, g.dtype) for g in gs], in_specs=[ANY] * n, out_specs=[ANY] * n,
        input_output_aliases={i: i for i in range(n)},
        scratch_shapes=[pltpu.SemaphoreType.DMA((n * L,))] * 2,
        name="grad_pair_share", compiler_params=pltpu.CompilerParams(has_side_effects=True))(*gs)


class _SwapComm:
    def __init__(self, arrays):
        self.inputs = list(arrays)
        self.out_shape = [SDS(a.shape, a.dtype) for a in arrays]
        self.aliased = False
        self.scratch = [pltpu.SemaphoreType.DMA((len(arrays),))] * 2

    def _copies(self, i_refs, o_refs, sems):
        send, recv = sems
        x, y, c, _ = _place()
        return [pltpu.make_async_remote_copy(
            src_ref=i_refs[i], dst_ref=o_refs[i], send_sem=send.at[i], recv_sem=recv.at[i],
            device_id=(x, y, 1 - c), device_id_type=MESH) for i in range(len(self.inputs))]

    def start(self, i_refs, o_refs, sems):
        for cp in self._copies(i_refs, o_refs, sems):
            cp.start()

    def finish(self, i_refs, o_refs, sems):
        for cp in self._copies(i_refs, o_refs, sems):
            cp.wait()


class _SlotShareComm:
    def __init__(self, bufs):
        self.inputs = list(bufs)
        self.out_shape = [SDS(b.shape, b.dtype) for b in bufs]
        self.aliased = True
        self.scratch = [pltpu.SemaphoreType.DMA((3 * len(bufs),))] * 2

    def _copies(self, i_refs, o_refs, sems):
        send, recv = sems
        x, y, c, chips = _place()
        k = 2 * x + y
        cps, got = [], []
        for i in range(len(self.inputs)):
            for j, (cx, cy) in enumerate(chips):
                s = 3 * i + j
                cps.append(pltpu.make_async_remote_copy(
                    src_ref=i_refs[i].at[k], dst_ref=o_refs[i].at[k], send_sem=send.at[s], recv_sem=recv.at[s],
                    device_id=(cx, cy, c), device_id_type=MESH))
                a = o_refs[i].at[2 * cx + cy]
                got.append(pltpu.make_async_remote_copy(
                    src_ref=a, dst_ref=a, send_sem=send.at[s], recv_sem=recv.at[s],
                    device_id=(cx, cy, c), device_id_type=MESH))
        return cps, got

    def start(self, i_refs, o_refs, sems):
        for cp in self._copies(i_refs, o_refs, sems)[0]:
            cp.start()

    def finish(self, i_refs, o_refs, sems):
        cps, got = self._copies(i_refs, o_refs, sems)
        for g in got:
            g.wait_recv()
        for cp in cps:
            cp.wait_send()


def _pair_sum_slot(mine, theirs, kidx, dtype):
    R, C = mine.shape

    def body(k_ref, a_ref, b_ref, o_ref):
        o_ref[0] = (a_ref[...] + b_ref[...]).astype(dtype)

    spec = pl.BlockSpec((R, C), lambda i, k_ref: (0, 0))
    return pl.pallas_call(
        body,
        grid_spec=pltpu.PrefetchScalarGridSpec(
            num_scalar_prefetch=1, grid=(1,), in_specs=[spec, spec],
            out_specs=pl.BlockSpec((1, R, C), lambda i, k_ref: (k_ref[0], 0, 0))),
        out_shape=SDS((N_CHIPS, R, C), dtype), name="small_pair_sum", compiler_params=_cp("arbitrary"))(kidx, mine, theirs)


def _small_sum(g):
    n, R, C = g.shape

    def body(g_ref, o_ref):
        acc = g_ref[0].astype(f32)
        for j in range(1, n):
            acc = acc + g_ref[j].astype(f32)
        o_ref[...] = acc
    return pl.pallas_call(body, out_shape=SDS((R, C), f32), name="small_sum", compiler_params=_cp())(g)


PACK_COLS = 1024


def _rows_of(shape):
    n = int(np.prod(shape)) if len(shape) else 1
    return -(-n // (8 * PACK_COLS)) * 8


def _pack(parts):
    blocks = []
    for p in parts:
        flat = p.reshape(-1)
        r = _rows_of(p.shape)
        blocks.append(jnp.pad(flat, (0, r * PACK_COLS - flat.shape[0])).reshape(r, PACK_COLS))
    return jnp.concatenate(blocks, axis=0)


def _unpack(buf, shapes):
    out, off = [], 0
    for s in shapes:
        n = int(np.prod(s)) if len(s) else 1
        r = _rows_of(s)
        out.append(buf[off:off + r].reshape(-1)[:n].reshape(s))
        off += r
    return out


def _block_diag(w):
    g, a, _ = w.shape
    out = jnp.zeros((g * a, g * a), w.dtype)
    for i in range(g):
        out = lax.dynamic_update_slice(out, w[i], (i * a, i * a))
    return out


def kernel(x, w_in, w_out, sgu_w, sgu_b, pool_w, pool_scale, swa_sinks, rel_bias, mix_out_gain, norm_mix, norm_ffn, w_gate_up, w_down, norm_final, loss_target, m_w_in, m_w_out, m_sgu_w, m_sgu_b, m_pool_w, m_pool_scale, m_swa_sinks, m_rel_bias, m_mix_out_gain, m_norm_mix, m_norm_ffn, m_w_gate_up, m_w_down, m_norm_final, v_w_in, v_w_out, v_sgu_w, v_sgu_b, v_pool_w, v_pool_scale, v_swa_sinks, v_rel_bias, v_mix_out_gain, v_norm_mix, v_norm_ffn, v_w_gate_up, v_w_down, v_norm_final):
    B, S, D = x.shape
    T = B * S
    L = w_in.shape[0]
    tm = min(512, T)
    F = w_down.shape[1] * N_CHIPS
    xi, yi, ci = lax.axis_index("x"), lax.axis_index("y"), lax.axis_index("c")
    kidx = jnp.reshape(2 * xi + yi, (1,)).astype(jnp.int32)
    kcidx = jnp.stack([2 * xi + yi, ci]).astype(jnp.int32)

    big = [w_in, w_out, w_gate_up, w_down]
    slots = [_cast_slots(w, kidx) for w in big]
    gather = lambda pi, l, part=0, parts=1: _GatherComm([slots[pi][l]], part, parts)
    Win, Wo, Wgu, Wd = ([None] * L for _ in range(4))
    Win[0], = _comm_only(gather(0, 0), "gather_weights")

    bucket = jnp.asarray(_t5_bucket_table().reshape(1, -1))
    bias_tab = _bias_expand(rel_bias.T, bucket).reshape(4, BLK, 2 * BLK)

    row = lambda v: v.reshape(1, -1)
    xc = x.reshape(T, D)
    tgt = loss_target.reshape(T, D)
    saved = []
    for l in range(L):
        bexp = jnp.repeat(sgu_b[l].T, HD, axis=1)
        wbd = _block_diag(pool_w[l])
        sk = jnp.broadcast_to(swa_sinks[l][:, None, None], (4, 1, BLK))
        h1, proj, wo = _norm_mm(xc, row(norm_mix[l]), Win[l], tm, gather(1, l))
        ya = _sgu_fwd(proj, sgu_w[l], bexp, B, S)
        yb = _pool_fwd(proj, wbd, row(pool_scale[l]), B, S)
        yc, wd = _swa_fwd(proj, sk, bias_tab, B, S, gather(3, l, 0, 2))
        if l == 0:
            yd, lt, wd, Wgu[0] = _sb_fwd(proj, B, S, _MultiComm([_GatherComm([wd], 1, 2), gather(2, 0)]))
        else:
            yd, lt, wd = _sb_fwd(proj, B, S, _GatherComm([wd], 1, 2))
        Wo[l], Wd[l] = wo.reshape(D, D), wd.reshape(F, D)
        ys = (ya, yb, yc, yd)
        nxt = l + 1 < L
        ycn, x1, h2, gu, act, *got = _mix_out_swiglu(ys, row(mix_out_gain[l]), Wo[l], xc, row(norm_ffn[l]), Wgu[l], tm,
                                                     gather(2, l + 1) if nxt else None)
        x2, *got2 = _mm_res(act, Wd[l], x1, tm, gather(0, l + 1) if nxt else None)
        if nxt:
            Wgu[l + 1], Win[l + 1] = got[0], got2[0]
        saved.append((xc, h1, proj, bexp, wbd, sk, ys, lt, ycn, x1, h2, gu, act))
        xc = x2

    dx, g_final, loss_v = _final_loss(xc, row(norm_final), tgt, tm)

    tk = min(T, 2048)
    gW = [[None] * L for _ in range(4)]
    g_sgu_w, g_sgu_b, g_pool_w, g_pool_scale, g_sinks, g_bias = ([None] * L for _ in range(6))
    g_out_gain, g_mix, g_ffn = ([None] * L for _ in range(3))
    reduced = [None] * 4
    sums = {}

    def pair_comm(keys):
        return _PairExchangeComm([gW[pi][l] for pi, l in keys])

    def after_pair(keys, r1):
        for (pi, l), r in zip(keys, r1):
            sums[pi, l] = (r, _pair_add(gW[pi][l], r, kcidx, "grad_pair_add"))

    def chip_comm(keys):
        return _ChipExchangeComm([sums[k][1] for k in keys])

    def after_chip(keys, r2):
        for (pi, l), r in zip(keys, r2):
            idx = jnp.stack([2 * xi + yi, ci, jnp.int32(l)]).astype(jnp.int32)
            reduced[pi] = _chip_add(gW[pi][l], sums.pop((pi, l))[0], r, idx, reduced[pi], L, "grad_chip_add")

    for l in reversed(range(L)):
        x0, h1, proj, bexp, wbd, sk, ys, lt, ycn, x1, h2, gu, act = saved[l]
        keys = [(0, l + 1), (1, l + 1)]
        comm = chip_comm(keys) if l + 1 < L else None
        dgu, dx1, g_ffn[l], *r2 = _dact_dx(dx, Wd[l], gu, Wgu[l], x1, row(norm_ffn[l]), tm // 2, comm)
        if comm is not None:
            after_chip(keys, r2)
        gW[3][l] = _dw(act, dx, lambda t, s: (t, 0), D, 1, F // 2, tk // 2, "dw_down").reshape(N_CHIPS, F // N_CHIPS, D)
        gW[2][l] = _dw(h2, dgu, lambda t, s: (s // 2, t, s % 2), F // 2, N_CHIPS, D, tk, "dw_gate_up")
        gW[1][l] = _dw(ycn, dx1, lambda t, s: (t, 0), D, 1, D, tk, "dw_out").reshape(N_CHIPS, D // N_CHIPS, D)
        dya, dyb, dyc, dyd, g_out_gain[l] = _dycat(dx1, Wo[l], ys, row(mix_out_gain[l]), tm)
        keys = [(2, l), (3, l)]
        dpa, g_sgu_w[l], dbf, *r1 = _sgu_bwd(proj, sgu_w[l], bexp, dya, B, S, pair_comm(keys))
        after_pair(keys, r1)
        g_sgu_b[l] = dbf[:, ::HD].T
        dpb, dwbd, dsc = _pool_bwd(proj, wbd, row(pool_scale[l]), dyb, B, S)
        npg = len(POOL_WINDOWS)
        g_pool_w[l] = jnp.stack([dwbd[i * HD:(i + 1) * HD, i * HD:(i + 1) * HD] for i in range(npg)])
        g_pool_scale[l] = dsc[0]
        dcq, dckv, dsk, g_bias[l], *r2 = _swa_bwd(proj, sk, bias_tab, dyc, B, S, chip_comm([(3, l)]))
        after_chip([(3, l)], r2)
        g_sinks[l] = dsk[:, 0, 0] * float(BLK)
        ddq, ddk, ddv, *r2 = _sb_bwd(proj, lt, dyd, B, S, chip_comm([(2, l)]))
        after_chip([(2, l)], r2)
        dproj = [dpa, dpb, dcq, dckv, ddq, ddk, ddv]
        gW[0][l] = _dw_pieces(h1, dproj, w_in.shape[2], N_CHIPS, tk, "dw_in")
        keys = [(0, l), (1, l)]
        dx, g_mix[l], *r1 = _dx_norm_bwd(dproj, Win[l], x0, row(norm_mix[l]), dx1, tm, "dx_mix_exchange", pair_comm(keys))
        after_pair(keys, r1)
    grad_x = dx.reshape(B, S, D)

    after_chip(keys, _comm_only(chip_comm(keys), "grad_chip_exchange"))
    g_big = _pair_share(reduced, [g.shape[1] // 2 for g in reduced])

    g_rel_bias = _bias_reduce([g.reshape(4, -1) for g in g_bias], bucket).T
    small_g = [jnp.stack(g_sgu_w), jnp.stack(g_sgu_b), jnp.stack(g_pool_w), jnp.stack(g_pool_scale), jnp.stack(g_sinks),
               g_rel_bias, jnp.concatenate(g_out_gain), jnp.concatenate(g_mix), jnp.concatenate(g_ffn), g_final[0]]
    small_w = [sgu_w, sgu_b, pool_w, pool_scale, swa_sinks, rel_bias, mix_out_gain, norm_mix, norm_ffn, norm_final]
    small_m = [m_sgu_w, m_sgu_b, m_pool_w, m_pool_scale, m_swa_sinks, m_rel_bias, m_mix_out_gain, m_norm_mix, m_norm_ffn, m_norm_final]
    small_v = [v_sgu_w, v_sgu_b, v_pool_w, v_pool_scale, v_swa_sinks, v_rel_bias, v_mix_out_gain, v_norm_mix, v_norm_ffn, v_norm_final]
    shapes = [w.shape for w in small_w]
    bulk, fine = [small_g[0], small_g[2]], [small_g[i] for i in (1, 3, 4, 5, 6, 7, 8, 9)] + [loss_v[0, 0:1]]
    mine = [_pack(bulk), _pack(fine)]
    theirs = _comm_only(_SwapComm(mine), "small_pair_swap")
    slots_s = [_pair_sum_slot(a, b, kidx, dt) for a, b, dt in zip(mine, theirs, (bf16, f32))]
    shared = _comm_only(_SlotShareComm(slots_s), "small_chip_share")
    g_bulk = _unpack(_small_sum(shared[0]), [shapes[0], shapes[2]])
    *g_fine, loss = _unpack(_small_sum(shared[1]), [shapes[i] for i in (1, 3, 4, 5, 6, 7, 8, 9)] + [()])
    g_small = [g_bulk[0], g_fine[0], g_bulk[1]] + g_fine[1:]

    big_m = [m_w_in, m_w_out, m_w_gate_up, m_w_down]
    big_v = [v_w_in, v_w_out, v_w_gate_up, v_w_down]
    g_out, d_big, m_big, v_big = [], [], [], []
    for w, g, m, v in zip(big, g_big, big_m, big_v):
        two = lambda a: a.reshape(-1, a.shape[-1])
        rows = two(w).shape[0]
        cap = max(8, (1 << 21) // (4 * w.shape[-1]))
        tr = max(t for t in range(8, min(rows, cap) + 1, 8) if rows % t == 0)
        d2, m2, v2, g2 = _adamw(two(w), two(g), two(m), two(v), tr, "adamw_big", True)
        for lst, val in ((d_big, d2), (m_big, m2), (v_big, v2), (g_out, g2)):
            lst.append(val.reshape(w.shape))
    g_big = g_out

    g_small_packed = _pack(g_small)
    ds, ms, vs = _adamw(_pack(small_w), g_small_packed, _pack(small_m), _pack(small_v), g_small_packed.shape[0], "adamw_small")
    d_small, m_small, v_small = _unpack(ds, shapes), _unpack(ms, shapes), _unpack(vs, shapes)

    def order(bigs, smalls):
        return [bigs[0], bigs[1]] + list(smalls[0:9]) + [bigs[2], bigs[3], smalls[9]]

    return (loss, grad_x, *order(g_big, g_small), *order(d_big, d_small), *order(m_big, m_small), *order(v_big, v_small))
```

```python
import functools

import numpy as np
import jax
import jax.numpy as jnp
from jax import lax
from jax.experimental import pallas as pl
from jax.experimental.pallas import tpu as pltpu

f32 = jnp.float32
bf16 = jnp.bfloat16
_MXU = jnp.bfloat16

EPS = 1e-6
HD = 64
GW = 256
BLK = 128
SB_UNROLL = 2
SB_HEADS = 4
SB_CUT = -110.0
POOL_WINDOWS = (2, 4, 8, 16)
N_BUCKETS = 32
MAX_DISTANCE = 128
N_CHIPS = 4
VMEM_LIMIT = 48 * 1024 * 1024

ADAM_LR = 0.001
ADAM_B1 = 0.9
ADAM_B2 = 0.999
ADAM_EPS = 1e-08
ADAM_WD = 0.01
ADAM_STEP = 10

SDS = jax.ShapeDtypeStruct
MESH = pl.DeviceIdType.MESH
HIGHEST = lax.Precision.HIGHEST
RESIDENT = pl.Buffered(1)
NT = (((1,), (1,)), ((), ()))
TN = (((0,), (0,)), ((), ()))


def _cp(*sem):
    return pltpu.CompilerParams(dimension_semantics=sem if sem else None, vmem_limit_bytes=VMEM_LIMIT)


def _mx(v):
    return v.astype(_MXU)


def _iota(shape, dim):
    return lax.broadcasted_iota(jnp.int32, shape, dim)


def _split_dot(a, tri):
    hi = a.astype(bf16)
    lo = (a - hi.astype(f32)).astype(bf16)
    return jnp.dot(hi, tri, preferred_element_type=f32) + jnp.dot(lo, tri, preferred_element_type=f32)


def _rms(xv):
    return lax.rsqrt(jnp.mean(xv * xv, axis=-1, keepdims=True) + EPS)


def _hosted_call(body, steps, in_specs, out_specs, out_shape, scratch, args, name, comm):
    n_in, n_out = len(in_specs), len(out_specs)
    c_args, c_in, c_out, c_shapes, aliases, c_scratch = _host_specs(comm, n_in, n_out)
    step = lambda v: (lambda: pl.program_id(0) == v)
    return pl.pallas_call(
        _host(body, n_in, n_out, len(scratch), comm, step(0), step(steps - 1)), grid=(steps,),
        in_specs=list(in_specs) + c_in, out_specs=list(out_specs) + c_out, out_shape=list(out_shape) + c_shapes,
        input_output_aliases=aliases, scratch_shapes=list(scratch) + c_scratch,
        name=name if comm is None else name + "_comm", compiler_params=_cp("arbitrary"))(*args, *c_args)


def _norm_mm(x, gain, w, tm, comm=None):
    T, D = x.shape
    NS, _, ns = w.shape

    def body(x_ref, g_ref, w_ref, h_ref, o_ref):
        xv = x_ref[...]
        h = (xv * _rms(xv) * g_ref[...]).astype(bf16)
        h_ref[...] = h
        for s in range(NS):
            o_ref[:, s * ns:(s + 1) * ns] = jnp.dot(_mx(h), w_ref[s], preferred_element_type=f32).astype(bf16)

    return _hosted_call(
        body, T // tm,
        [pl.BlockSpec((tm, D), lambda i: (i, 0)),
         pl.BlockSpec((1, D), lambda i: (0, 0)),
         pl.BlockSpec((NS, D, ns), lambda i: (0, 0, 0), pipeline_mode=RESIDENT)],
        [pl.BlockSpec((tm, D), lambda i: (i, 0)), pl.BlockSpec((tm, NS * ns), lambda i: (i, 0))],
        [SDS((T, D), bf16), SDS((T, NS * ns), bf16)], [], (x, gain, w), "norm_mm_in", comm)


def _mix_out_swiglu(ys, gain_mix, wo, x, gain_ffn, w, tm, comm=None):
    T, D = x.shape
    NS, _, ns = w.shape
    half = NS // 2

    def body(ya, yb, yc, yd, gm_ref, wo_ref, x_ref, gf_ref, w_ref, yn_ref, x1_ref, h_ref, gu_ref, a_ref):
        parts = []
        for m, r in enumerate((ya, yb, yc, yd)):
            y = r[...].astype(f32)
            parts.append((y * _rms(y) * gm_ref[:, m * GW:(m + 1) * GW]).astype(bf16))
        yn = jnp.concatenate(parts, axis=1)
        yn_ref[...] = yn
        x1 = x_ref[...] + jnp.dot(_mx(yn), wo_ref[...], preferred_element_type=f32)
        x1_ref[...] = x1
        hb = (x1 * _rms(x1) * gf_ref[...]).astype(bf16)
        h_ref[...] = hb
        h = _mx(hb)
        for s in range(half):
            cols = slice(s * ns, (s + 1) * ns)
            g = jnp.dot(h, w_ref[s], preferred_element_type=f32)
            u = jnp.dot(h, w_ref[s + half], preferred_element_type=f32)
            gu_ref[0, :, cols] = g.astype(bf16)
            gu_ref[1, :, cols] = u.astype(bf16)
            a_ref[:, cols] = (jax.nn.silu(g) * u).astype(bf16)

    yspec = pl.BlockSpec((tm, GW), lambda i: (i, 0))
    row, tile = pl.BlockSpec((1, D), lambda i: (0, 0)), pl.BlockSpec((tm, D), lambda i: (i, 0))
    return _hosted_call(
        body, T // tm,
        [yspec, yspec, yspec, yspec, row,
         pl.BlockSpec((D, D), lambda i: (0, 0), pipeline_mode=RESIDENT), tile, row,
         pl.BlockSpec((NS, D, ns), lambda i: (0, 0, 0), pipeline_mode=RESIDENT)],
        [tile, tile, tile, pl.BlockSpec((2, tm, half * ns), lambda i: (0, i, 0)),
         pl.BlockSpec((tm, half * ns), lambda i: (i, 0))],
        [SDS((T, D), bf16), SDS((T, D), f32), SDS((T, D), bf16), SDS((2, T, half * ns), bf16), SDS((T, half * ns), bf16)],
        [], (*ys, gain_mix, wo, x, gain_ffn, w), "mix_out_swiglu", comm)


def _mm_res(a, w, x, tm, comm=None):
    T, D = x.shape
    K = a.shape[1]

    def body(a_ref, w_ref, x_ref, o_ref):
        o_ref[...] = x_ref[...] + jnp.dot(_mx(a_ref[...]), w_ref[...], preferred_element_type=f32)

    return _hosted_call(
        body, T // tm,
        [pl.BlockSpec((tm, K), lambda i: (i, 0)),
         pl.BlockSpec((K, D), lambda i: (0, 0), pipeline_mode=RESIDENT),
         pl.BlockSpec((tm, D), lambda i: (i, 0))],
        [pl.BlockSpec((tm, D), lambda i: (i, 0))], [SDS((T, D), f32)], [], (a, w, x), "mm_res_down", comm)


def _final_loss(x, gain, tgt, tm):
    T, D = x.shape

    def body(x_ref, g_ref, t_ref, dx_ref, dg_ref, l_ref):
        @pl.when(pl.program_id(0) == 0)
        def _():
            dg_ref[...] = jnp.zeros_like(dg_ref)
            l_ref[...] = jnp.zeros_like(l_ref)
        xv = x_ref[...]
        g = g_ref[...]
        r = _rms(xv)
        xh = xv * r
        err = xh * g - t_ref[...]
        l_ref[...] += 0.5 * jnp.sum(jnp.mean(err * err, axis=-1, keepdims=True), axis=0, keepdims=True)
        dy = err * (1.0 / D)
        dg_ref[...] += jnp.sum(dy * xh, axis=0, keepdims=True)
        dxh = dy * g
        dx_ref[...] = r * (dxh - xh * jnp.mean(dxh * xh, axis=-1, keepdims=True))

    return pl.pallas_call(
        body, grid=(T // tm,),
        in_specs=[pl.BlockSpec((tm, D), lambda i: (i, 0)),
                  pl.BlockSpec((1, D), lambda i: (0, 0)),
                  pl.BlockSpec((tm, D), lambda i: (i, 0))],
        out_specs=[pl.BlockSpec((tm, D), lambda i: (i, 0)),
                   pl.BlockSpec((1, D), lambda i: (0, 0)),
                   pl.BlockSpec((1, BLK), lambda i: (0, 0))],
        out_shape=[SDS((T, D), f32), SDS((1, D), f32), SDS((1, BLK), f32)],
        name="final_loss", compiler_params=_cp("arbitrary"))(x, gain, tgt)


def _dact_dx(dx, wd, gu, w, x, gain, tm, comm=None):
    T, D = dx.shape
    F = wd.shape[0]
    NS, _, ns = w.shape
    half = NS // 2

    def body(dx_ref, wd_ref, gu_ref, w_ref, x_ref, g_ref, dgu_ref, dx1_ref, dg_ref):
        @pl.when(pl.program_id(0) == 0)
        def _():
            dg_ref[...] = jnp.zeros_like(dg_ref)
        dxv = dx_ref[...]
        dxb = _mx(dxv)
        dh = None
        for s in range(half):
            cols = slice(s * ns, (s + 1) * ns)
            da = lax.dot_general(dxb, wd_ref[s * ns:(s + 1) * ns, :], NT, preferred_element_type=f32)
            g = gu_ref[0, :, cols].astype(f32)
            u = gu_ref[1, :, cols].astype(f32)
            sg = jax.nn.sigmoid(g)
            dgs = (da * u * (sg * (1.0 + g * (1.0 - sg)))).astype(bf16)
            dus = (da * (g * sg)).astype(bf16)
            dgu_ref[0, :, cols] = dgs
            dgu_ref[1, :, cols] = dus
            part = (lax.dot_general(_mx(dgs), w_ref[s], NT, preferred_element_type=f32)
                    + lax.dot_general(_mx(dus), w_ref[s + half], NT, preferred_element_type=f32))
            dh = part if dh is None else dh + part
        xv = x_ref[...]
        r = _rms(xv)
        xh = xv * r
        dg_ref[...] += jnp.sum(dh * xh, axis=0, keepdims=True)
        dxh = dh * g_ref[...]
        dx1_ref[...] = dxv + r * (dxh - xh * jnp.mean(dxh * xh, axis=-1, keepdims=True))

    return _hosted_call(
        body, T // tm,
        [pl.BlockSpec((tm, D), lambda i: (i, 0)),
         pl.BlockSpec((F, D), lambda i: (0, 0), pipeline_mode=RESIDENT),
         pl.BlockSpec((2, tm, F), lambda i: (0, i, 0)),
         pl.BlockSpec((NS, D, ns), lambda i: (0, 0, 0), pipeline_mode=RESIDENT),
         pl.BlockSpec((tm, D), lambda i: (i, 0)),
         pl.BlockSpec((1, D), lambda i: (0, 0))],
        [pl.BlockSpec((2, tm, F), lambda i: (0, i, 0)), pl.BlockSpec((tm, D), lambda i: (i, 0)),
         pl.BlockSpec((1, D), lambda i: (0, 0))],
        [SDS((2, T, F), bf16), SDS((T, D), f32), SDS((1, D), f32)], [], (dx, wd, gu, w, x, gain), "dact_dx", comm)


def _dw(a, b, b_map, ns, NS, tka, tk, name):
    T, Ka = a.shape
    b_block = (tk, ns) if b.ndim == 2 else (1, tk, ns)

    def body(a_ref, b_ref, o_ref):
        bv = b_ref[...] if b.ndim == 2 else b_ref[0]
        part = lax.dot_general(_mx(a_ref[...]), _mx(bv), TN, preferred_element_type=f32)

        @pl.when(pl.program_id(2) == 0)
        def _():
            o_ref[0] = part

        @pl.when(pl.program_id(2) > 0)
        def _():
            o_ref[0] += part

    return pl.pallas_call(
        body, grid=(NS, Ka // tka, T // tk),
        in_specs=[pl.BlockSpec((tk, tka), lambda s, k, t: (t, k)),
                  pl.BlockSpec(b_block, lambda s, k, t: b_map(t, s))],
        out_specs=pl.BlockSpec((1, tka, ns), lambda s, k, t: (s, k, 0)),
        out_shape=SDS((NS, Ka, ns), f32),
        name=name, compiler_params=_cp("parallel", "parallel", "arbitrary"))(a, b)


def _dw_pieces(a, pieces, ns, NS, tk, name):
    T, Ka = a.shape
    n = len(pieces)

    def body(*refs):
        a_ref, b_refs, o_ref = refs[0], refs[1:1 + n], refs[1 + n]
        full = jnp.concatenate([r[...] for r in b_refs], axis=1)
        av = _mx(a_ref[...])
        parts = [lax.dot_general(av, _mx(full[:, s * ns:(s + 1) * ns]), TN, preferred_element_type=f32) for s in range(NS)]

        @pl.when(pl.program_id(0) == 0)
        def _():
            for s in range(NS):
                o_ref[s] = parts[s]

        @pl.when(pl.program_id(0) > 0)
        def _():
            for s in range(NS):
                o_ref[s] += parts[s]

    return pl.pallas_call(
        body, grid=(T // tk,),
        in_specs=[pl.BlockSpec((tk, Ka), lambda t: (t, 0))] + [pl.BlockSpec((tk, p.shape[1]), lambda t: (t, 0)) for p in pieces],
        out_specs=pl.BlockSpec((NS, Ka, ns), lambda t: (0, 0, 0)),
        out_shape=SDS((NS, Ka, ns), f32),
        name=name, compiler_params=_cp("arbitrary"))(a, *pieces)


def _dx_norm_bwd(pieces, w, x, gain, dxin, tm, name, comm=None):
    T, D = x.shape
    NS, _, ns = w.shape
    n_dy = len(pieces)

    def body(*refs):
        dy_refs = refs[:n_dy]
        w_ref, x_ref, g_ref, dxin_ref, dx_ref, dg_ref = refs[n_dy:]

        @pl.when(pl.program_id(0) == 0)
        def _():
            dg_ref[...] = jnp.zeros_like(dg_ref)
        full = jnp.concatenate([r[...] for r in dy_refs], axis=1)
        dh = None
        for s in range(NS):
            part = lax.dot_general(_mx(full[:, s * ns:(s + 1) * ns]), w_ref[s], NT, preferred_element_type=f32)
            dh = part if dh is None else dh + part
        xv = x_ref[...]
        r = _rms(xv)
        xh = xv * r
        dg_ref[...] += jnp.sum(dh * xh, axis=0, keepdims=True)
        dxh = dh * g_ref[...]
        dx_ref[...] = dxin_ref[...] + r * (dxh - xh * jnp.mean(dxh * xh, axis=-1, keepdims=True))

    dy_specs = [pl.BlockSpec((tm, p.shape[1]), lambda i: (i, 0)) for p in pieces]
    return _hosted_call(
        body, T // tm,
        dy_specs + [pl.BlockSpec((NS, D, ns), lambda i: (0, 0, 0), pipeline_mode=RESIDENT),
                    pl.BlockSpec((tm, D), lambda i: (i, 0)),
                    pl.BlockSpec((1, D), lambda i: (0, 0)),
                    pl.BlockSpec((tm, D), lambda i: (i, 0))],
        [pl.BlockSpec((tm, D), lambda i: (i, 0)), pl.BlockSpec((1, D), lambda i: (0, 0))],
        [SDS((T, D), f32), SDS((1, D), f32)], [], (*pieces, w, x, gain, dxin), name, comm)


def _dycat(dx, w, ys, gain, tm, comm=None):
    T, D = dx.shape

    def body(dx_ref, w_ref, ya, yb, yc, yd, g_ref, da, db, dc, dd, dg_ref):
        @pl.when(pl.program_id(0) == 0)
        def _():
            dg_ref[...] = jnp.zeros_like(dg_ref)
        dyn = lax.dot_general(_mx(dx_ref[...]), w_ref[...], NT, preferred_element_type=f32)
        for m, (r, o) in enumerate(((ya, da), (yb, db), (yc, dc), (yd, dd))):
            cols = slice(m * GW, (m + 1) * GW)
            y = r[...].astype(f32)
            rs = _rms(y)
            yh = y * rs
            d = dyn[:, cols]
            dg_ref[:, cols] += jnp.sum(d * yh, axis=0, keepdims=True)
            dyh = d * g_ref[:, cols]
            o[...] = (rs * (dyh - yh * jnp.mean(dyh * yh, axis=-1, keepdims=True))).astype(bf16)

    yspec = pl.BlockSpec((tm, GW), lambda i: (i, 0))
    return _hosted_call(
        body, T // tm,
        [pl.BlockSpec((tm, D), lambda i: (i, 0)),
         pl.BlockSpec((D, D), lambda i: (0, 0), pipeline_mode=RESIDENT),
         yspec, yspec, yspec, yspec,
         pl.BlockSpec((1, D), lambda i: (0, 0))],
        [yspec, yspec, yspec, yspec, pl.BlockSpec((1, D), lambda i: (0, 0))],
        [SDS((T, GW), bf16)] * 4 + [SDS((1, D), f32)], [], (dx, w, *ys, gain), "dycat", comm)


def _sgu_consts():
    r, c = _iota((GW, GW), 0), _iota((GW, GW), 1)
    seg = (r // HD == c // HD).astype(f32)
    tr, ts = _iota((BLK, BLK), 0), _iota((BLK, BLK), 1)
    causal = ts <= tr
    lane_head = _iota((BLK, GW), 1) // HD
    return seg, causal, lane_head


def _split3_dot(a, ones):
    hi = a.astype(bf16)
    r1 = a - hi.astype(f32)
    mid = r1.astype(bf16)
    lo = (r1 - mid.astype(f32)).astype(bf16)
    dot = functools.partial(jnp.dot, preferred_element_type=f32)
    return dot(hi, ones) + dot(mid, ones) + dot(lo, ones)


def _sgu_chunks(aus, avs, w, bexp, consts):
    seg, causal, lane_head = consts
    segb = seg.astype(bf16)
    nh = GW // HD
    vs = [jax.nn.gelu(av) for av in avs]
    mus = [_split3_dot(v, segb) * (1.0 / HD) for v in vs]
    vcs = [v - mu for v, mu in zip(vs, mus)]
    vars_ = [_split3_dot(vc * vc, segb) * (1.0 / HD) for vc in vcs]
    vns = [_mx(vc * lax.rsqrt(var + EPS)) for vc, var in zip(vcs, vars_)]
    whs = [_mx(jnp.where(causal, w[h], 0.0)) for h in range(nh)]
    mixes = [[jnp.dot(whs[h], vn, preferred_element_type=f32) for h in range(nh)] for vn in vns]
    out = []
    for au, ms in zip(aus, mixes):
        mix = bexp
        for h in range(nh):
            mix = mix + jnp.where(lane_head == h, ms[h], 0.0)
        out.append(jax.nn.gelu(au) * mix)
    return out


def _sgu_group(S):
    nc = S // BLK
    return 4 if nc % 4 == 0 else (2 if nc % 2 == 0 else 1)


def _sgu_pool_fwd(proj, w, bexp, wbd, scale, B, S):
    G = _sgu_group(S)

    def body(au_ref, av_ref, w_ref, b_ref, p_ref, wb_ref, sc_ref, y_ref, yb_ref):
        _pool_fwd_body(p_ref, wb_ref, sc_ref, yb_ref)
        consts = _sgu_consts()
        wv, bv = w_ref[...], b_ref[...]

        def group(n, c):
            rows = [pl.ds(pl.multiple_of((n * G + j) * BLK, BLK), BLK) for j in range(G)]
            ys = _sgu_chunks([au_ref[r, :].astype(f32) for r in rows], [av_ref[r, :].astype(f32) for r in rows],
                             wv, bv, consts)
            for r, y in zip(rows, ys):
                y_ref[r, :] = y.astype(bf16)
            return c
        lax.fori_loop(0, S // BLK // G, group, 0)

    return pl.pallas_call(
        body, grid=(B,),
        in_specs=[pl.BlockSpec((S, GW), lambda b: (b, 0)),
                  pl.BlockSpec((S, GW), lambda b: (b, 1)),
                  pl.BlockSpec((GW // HD, BLK, BLK), lambda b: (0, 0, 0)),
                  pl.BlockSpec((BLK, GW), lambda b: (0, 0)),
                  pl.BlockSpec((S, GW), lambda b: (b, 2)),
                  pl.BlockSpec((GW, GW), lambda b: (0, 0)),
                  pl.BlockSpec((1, GW), lambda b: (0, 0))],
        out_specs=[pl.BlockSpec((S, GW), lambda b: (b, 0))] * 2,
        out_shape=[SDS((B * S, GW), bf16)] * 2,
        name="sgu_pool_fwd", compiler_params=_cp("parallel"))(proj, proj, w, bexp, proj, wbd, scale)


def _sgu_pool_bwd(proj, w, bexp, dy, wbd, scale, dyb, B, S, comm=None):
    def body(au_ref, av_ref, w_ref, b_ref, dy_ref, p_ref, wb_ref, sc_ref, dyb_ref,
             dp_ref, dw_ref, db_ref, dpb_ref, dwb_ref, dsc_ref):
        _pool_bwd_body(p_ref, wb_ref, sc_ref, dyb_ref, dpb_ref, dwb_ref, dsc_ref)

        @pl.when(pl.program_id(0) == 0)
        def _():
            dw_ref[...] = jnp.zeros_like(dw_ref)
            db_ref[...] = jnp.zeros_like(db_ref)
        consts = _sgu_consts()
        wv, bv = w_ref[...], b_ref[...]
        fn = lambda aus, avs, ww, bb: _sgu_chunks(aus, avs, ww, bb, consts)
        G = _sgu_group(S)

        def group(n, carry):
            dw_acc, db_acc = carry
            rows = [pl.ds(pl.multiple_of((n * G + j) * BLK, BLK), BLK) for j in range(G)]
            _, vjp = jax.vjp(fn, [au_ref[r, :].astype(f32) for r in rows], [av_ref[r, :].astype(f32) for r in rows], wv, bv)
            daus, davs, dwc, dbc = vjp([dy_ref[r, :].astype(f32) for r in rows])
            for r, dau, dav in zip(rows, daus, davs):
                dp_ref[r, 0:GW] = dau.astype(bf16)
                dp_ref[r, GW:2 * GW] = dav.astype(bf16)
            return dw_acc + dwc, db_acc + dbc
        dw_acc, db_acc = lax.fori_loop(0, S // BLK // G, group, (jnp.zeros(wv.shape, f32), jnp.zeros(bv.shape, f32)))
        dw_ref[...] += dw_acc
        db_ref[...] += jnp.dot(db_acc, consts[0], precision=HIGHEST, preferred_element_type=f32)

    return _hosted_call(
        body, B,
        [pl.BlockSpec((S, GW), lambda b: (b, 0)),
         pl.BlockSpec((S, GW), lambda b: (b, 1)),
         pl.BlockSpec((GW // HD, BLK, BLK), lambda b: (0, 0, 0)),
         pl.BlockSpec((BLK, GW), lambda b: (0, 0)),
         pl.BlockSpec((S, GW), lambda b: (b, 0)),
         pl.BlockSpec((S, GW), lambda b: (b, 2)),
         pl.BlockSpec((GW, GW), lambda b: (0, 0)),
         pl.BlockSpec((1, GW), lambda b: (0, 0)),
         pl.BlockSpec((S, GW), lambda b: (b, 0))],
        [pl.BlockSpec((S, 2 * GW), lambda b: (b, 0)),
         pl.BlockSpec((GW // HD, BLK, BLK), lambda b: (0, 0, 0)),
         pl.BlockSpec((BLK, GW), lambda b: (0, 0)),
         pl.BlockSpec((S, GW), lambda b: (b, 0)),
         pl.BlockSpec((GW, GW), lambda b: (0, 0)),
         pl.BlockSpec((1, GW), lambda b: (0, 0))],
        [SDS((B * S, 2 * GW), bf16), SDS((GW // HD, BLK, BLK), f32), SDS((BLK, GW), f32),
         SDS((B * S, GW), bf16), SDS((GW, GW), f32), SDS((1, GW), f32)], [],
        (proj, proj, w, bexp, dy, proj, wbd, scale, dyb), "sgu_pool_bwd", comm)


def _pool_parts(p):
    n = p.shape[0]
    r = _iota(p.shape, 0)
    lg = _iota(p.shape, 1) // HD

    def sh(v, k):
        return jnp.where(r >= k, pltpu.roll(v, k, 0), 0.0)
    s2 = p + sh(p, 1)
    s4 = s2 + sh(s2, 2)
    s8 = s4 + sh(s4, 4)
    s16 = s8 + sh(s8, 8)
    ws = jnp.where(lg == 0, s2, jnp.where(lg == 1, s4, jnp.where(lg == 2, s8, s16)))
    wlen = jnp.where(lg == 0, 2, jnp.where(lg == 1, 4, jnp.where(lg == 2, 8, 16)))
    cnt = jnp.minimum(r + 1, wlen).astype(f32)
    del n
    return ws / cnt - p, cnt, lg


def _pool_fwd_body(p_ref, w_ref, s_ref, y_ref):
    y, _, _ = _pool_parts(p_ref[...].astype(f32))
    y_ref[...] = (jnp.dot(_mx(y), _mx(w_ref[...]), preferred_element_type=f32) * s_ref[...]).astype(bf16)


def _pool_bwd_body(p_ref, w_ref, s_ref, dy_ref, dp_ref, dw_ref, ds_ref):
    @pl.when(pl.program_id(0) == 0)
    def _():
        dw_ref[...] = jnp.zeros_like(dw_ref)
        ds_ref[...] = jnp.zeros_like(ds_ref)
    y, cnt, lg = _pool_parts(p_ref[...].astype(f32))
    wv = _mx(w_ref[...])
    z = jnp.dot(_mx(y), wv, preferred_element_type=f32)
    dout = dy_ref[...].astype(f32)
    ds_ref[...] += jnp.sum(dout * z, axis=0, keepdims=True)
    dz = _mx(dout * s_ref[...])
    dw_ref[...] += lax.dot_general(_mx(y), dz, TN, preferred_element_type=f32)
    dyv = lax.dot_general(dz, wv, NT, preferred_element_type=f32)
    n = dyv.shape[0]
    r = _iota(dyv.shape, 0)

    def ush(v, k):
        return jnp.where(r < n - k, pltpu.roll(v, n - k, 0), 0.0)
    gq = dyv / cnt
    a2 = gq + ush(gq, 1)
    a4 = a2 + ush(a2, 2)
    a8 = a4 + ush(a4, 4)
    a16 = a8 + ush(a8, 8)
    adj = jnp.where(lg == 0, a2, jnp.where(lg == 1, a4, jnp.where(lg == 2, a8, a16)))
    dp_ref[...] = (adj - dyv).astype(bf16)


def _t5_bucket_table():
    dist = (np.arange(BLK)[:, None] + BLK) - np.arange(2 * BLK)[None, :]
    d = np.clip(dist, 0, BLK - 1)
    max_exact = N_BUCKETS // 2
    df = np.maximum(d, 1).astype(np.float32)
    large = max_exact + (np.log(df / max_exact) / np.float32(np.log(MAX_DISTANCE / max_exact))
                         * (N_BUCKETS - max_exact)).astype(np.int32)
    large = np.minimum(large, N_BUCKETS - 1)
    return np.where(d < max_exact, d, large).astype(np.int32)


def _swa_blocks(qs, kx, vx, sinks, biases, n):
    G = len(qs)
    heads = [(p, g) for p in range(2) for g in range(2)]
    ri, ci = _iota((BLK, BLK), 0), _iota((BLK, BLK), 1)
    qi, ki = _iota((BLK, 2 * BLK), 0), _iota((BLK, 2 * BLK), 1)
    dist = qi + BLK - ki
    band = (dist >= 0) & (dist < BLK)
    masks = [band & ((ki >= BLK) | (n > 0))] + [band] * (G - 1)
    kb, vb = _mx(kx), _mx(vx)
    qsel = [[None] * 4 for _ in range(G)]
    vs = []
    for h, (p, g) in enumerate(heads):
        selq = ((ri - g * HD == ci - p * HD) & (ri >= g * HD) & (ri < (g + 1) * HD)).astype(_MXU)
        selv = ((ci - g * HD == ri - p * HD) & (ci >= g * HD) & (ci < (g + 1) * HD)).astype(_MXU)
        for b in range(G):
            qsel[b][h] = _mx(jnp.dot(_mx(qs[b][p]), selq, preferred_element_type=f32))
        vs.append(_mx(jnp.dot(vb, selv, preferred_element_type=f32)))
    zs = [[lax.dot_general(qsel[b][h], kb[b * BLK:(b + 2) * BLK], NT, preferred_element_type=f32) * (HD ** -0.5)
           for h in range(4)] for b in range(G)]
    prs = [[None] * 4 for _ in range(G)]
    for b in range(G):
        for h in range(4):
            z = jnp.where(masks[b], zs[b][h] + biases[h], -1e30)
            s = jnp.mean(sinks[h], axis=-1, keepdims=True)
            m = jnp.maximum(jnp.max(z, axis=-1, keepdims=True), s)
            e = jnp.exp(z - m)
            prs[b][h] = _mx(e / (jnp.sum(e, axis=-1, keepdims=True) + jnp.exp(s - m)))
    outs = [[jnp.dot(prs[b][h], vs[h][b * BLK:(b + 2) * BLK], preferred_element_type=f32) for h in range(4)]
            for b in range(G)]
    return [[o[0] + o[1], o[2] + o[3]] for o in outs]


def _swa_group(S):
    return 2 if (S // BLK) % 2 == 0 else 1


def _swa_rows(n, G):
    blk = lambda j: pl.ds(pl.multiple_of(j * BLK, BLK), BLK)
    return [blk(jnp.maximum(n - 1, 0))] + [blk(n + b) for b in range(G)]


def _swa_fwd(proj, sinks, bias, B, S, comm=None):
    G = _swa_group(S)

    def body(q_ref, kv_ref, s_ref, b_ref, y_ref):
        def group(i, c):
            n = i * G
            rows = _swa_rows(n, G)
            kx = jnp.concatenate([kv_ref[r, 0:BLK] for r in rows], axis=0).astype(f32)
            vx = jnp.concatenate([kv_ref[r, BLK:2 * BLK] for r in rows], axis=0).astype(f32)
            qs = [[q_ref[r, 0:BLK].astype(f32), q_ref[r, BLK:2 * BLK].astype(f32)] for r in rows[1:]]
            outs = _swa_blocks(qs, kx, vx, [s_ref[h] for h in range(4)], [b_ref[h] for h in range(4)], n)
            for r, (o0, o1) in zip(rows[1:], outs):
                y_ref[r, 0:BLK] = o0.astype(bf16)
                y_ref[r, BLK:2 * BLK] = o1.astype(bf16)
            return c
        lax.fori_loop(0, S // BLK // G, group, 0)

    return _hosted_call(
        body, B,
        [pl.BlockSpec((S, GW), lambda b: (b, 3)),
         pl.BlockSpec((S, GW), lambda b: (b, 4)),
         pl.BlockSpec((4, 1, BLK), lambda b: (0, 0, 0)),
         pl.BlockSpec((4, BLK, 2 * BLK), lambda b: (0, 0, 0))],
        [pl.BlockSpec((S, GW), lambda b: (b, 0))], [SDS((B * S, GW), bf16)], [],
        (proj, proj, sinks, bias), "swa_fwd", comm)


def _swa_bwd(proj, sinks, bias, dy, B, S, comm=None):
    def body(q_ref, kv_ref, s_ref, b_ref, dy_ref, dq_ref, dkv_ref, ds_ref, db_ref, acc_ref):
        @pl.when(pl.program_id(0) == 0)
        def _():
            ds_ref[...] = jnp.zeros_like(ds_ref)
            db_ref[...] = jnp.zeros_like(db_ref)
        acc_ref[...] = jnp.zeros_like(acc_ref)

        G = _swa_group(S)

        def group(i, c):
            n = i * G
            rows = _swa_rows(n, G)
            kx = jnp.concatenate([kv_ref[r, 0:BLK] for r in rows], axis=0).astype(f32)
            vx = jnp.concatenate([kv_ref[r, BLK:2 * BLK] for r in rows], axis=0).astype(f32)
            qs = [[q_ref[r, 0:BLK].astype(f32), q_ref[r, BLK:2 * BLK].astype(f32)] for r in rows[1:]]
            dos = [[dy_ref[r, 0:BLK].astype(f32), dy_ref[r, BLK:2 * BLK].astype(f32)] for r in rows[1:]]
            fn = functools.partial(_swa_blocks, n=n)
            _, vjp = jax.vjp(fn, qs, kx, vx, [s_ref[h] for h in range(4)], [b_ref[h] for h in range(4)])
            dqs, dkx, dvx, dss, dbs = vjp(dos)
            for r, (dq0, dq1) in zip(rows[1:], dqs):
                dq_ref[r, 0:BLK] = dq0.astype(bf16)
                dq_ref[r, BLK:2 * BLK] = dq1.astype(bf16)
            for h in range(4):
                ds_ref[h] += dss[h]
                db_ref[h] += dbs[h]
            for j, r in enumerate(rows):
                acc_ref[r, 0:BLK] += dkx[j * BLK:(j + 1) * BLK]
                acc_ref[r, BLK:2 * BLK] += dvx[j * BLK:(j + 1) * BLK]
            return c
        lax.fori_loop(0, S // BLK // G, group, 0)
        dkv_ref[...] = acc_ref[...].astype(bf16)

    c_args, c_in, c_out, c_shapes, aliases, c_scratch = _host_specs(comm, 5, 4)
    step = lambda v: (lambda: pl.program_id(0) == v)
    return pl.pallas_call(
        _host(body, 5, 4, 1, comm, step(0), step(B - 1)), grid=(B,),
        in_specs=[pl.BlockSpec((S, GW), lambda b: (b, 3)),
                  pl.BlockSpec((S, GW), lambda b: (b, 4)),
                  pl.BlockSpec((4, 1, BLK), lambda b: (0, 0, 0)),
                  pl.BlockSpec((4, BLK, 2 * BLK), lambda b: (0, 0, 0)),
                  pl.BlockSpec((S, GW), lambda b: (b, 0))] + c_in,
        out_specs=[pl.BlockSpec((S, GW), lambda b: (b, 0)),
                   pl.BlockSpec((S, GW), lambda b: (b, 0)),
                   pl.BlockSpec((4, 1, BLK), lambda b: (0, 0, 0)),
                   pl.BlockSpec((4, BLK, 2 * BLK), lambda b: (0, 0, 0))] + c_out,
        out_shape=[SDS((B * S, GW), bf16), SDS((B * S, GW), bf16), SDS((4, 1, BLK), f32),
                   SDS((4, BLK, 2 * BLK), f32)] + c_shapes,
        input_output_aliases=aliases, scratch_shapes=[pltpu.VMEM((S, GW), f32)] + c_scratch,
        name="swa_bwd" if comm is None else "swa_bwd_exchange",
        compiler_params=_cp("arbitrary"))(proj, proj, sinks, bias, dy, *c_args)


def _log1m_parts(z):
    t = jnp.exp(-jnp.abs(z))
    return jnp.minimum(-z, 0.0) - jnp.log(1.0 + t), t


def _log1m(z):
    return _log1m_parts(z)[0]


def _sigmoid_from(z, t):
    return jnp.where(z >= 0.0, 1.0, t) / (1.0 + t)


def _sb_consts(tri):
    r2, c2 = _iota((2 * BLK, 2 * BLK), 0), _iota((2 * BLK, 2 * BLK), 1)
    tri2 = (tri(r2, c2) & (r2 // BLK == c2 // BLK)).astype(bf16)
    ri, ci = _iota((BLK, 2 * BLK), 0), _iota((BLK, 2 * BLK), 1)
    strict2 = (ci % BLK) < ri
    head0 = _iota((BLK, BLK), 1) < HD
    return tri2, strict2, head0


def _sb_stack_kv(k_ref, v_ref, kst_ref, vst_ref, head0, nb):
    def one(kb, c):
        krows = pl.ds(pl.multiple_of(kb * BLK, BLK), BLK)
        for p in range(2):
            for src, dst in ((k_ref, kst_ref), (v_ref, vst_ref)):
                t = src[krows, p * BLK:(p + 1) * BLK]
                dst[p, kb] = _mx(jnp.concatenate([jnp.where(head0, t, 0.0), jnp.where(head0, 0.0, t)], axis=0))
        return c
    lax.fori_loop(0, nb, one, 0)


def _sb_load_kv(kst_ref, vst_ref, kb):
    return [kst_ref[p, kb] for p in range(2)], [vst_ref[p, kb] for p in range(2)]


def _two_halves(a, b):
    return jnp.concatenate([jnp.broadcast_to(a, (BLK, BLK)), jnp.broadcast_to(b, (BLK, BLK))], axis=1)


def _half_sums(t):
    return jnp.sum(t[:, :BLK], axis=-1, keepdims=True), jnp.sum(t[:, BLK:], axis=-1, keepdims=True)


def _sb_fwd(proj, B, S, comm=None):
    def body(q_ref, k_ref, v_ref, y_ref, lt_ref, kst_ref, vst_ref):
        ci = _iota((BLK, BLK), 1)
        above2, strict2, head0 = _sb_consts(lambda r, c: r > c)
        _sb_stack_kv(k_ref, v_ref, kst_ref, vst_ref, head0, S // BLK)

        def step(qs, kbs, diags, carry):
            U = range(len(kbs))
            ok = [None if diags[u] else kbs[u] >= 0 for u in U]
            kv = [_sb_load_kv(kst_ref, vst_ref, jnp.maximum(kb, 0)) for kb in kbs]
            zs = [[lax.dot_general(qs[p], kks[p], NT, preferred_element_type=f32) for p in range(2)] for kks, _ in kv]
            Ls = [[jnp.where(strict2, _log1m(z), 0.0) if diags[u] else _log1m(z) for z in zs[u]] for u in U]
            tails = [[_split_dot(L, above2) for L in Lu] for Lu in Ls]
            carry = list(carry)
            for u in U:
                for p in range(2):
                    R0, R1, acc = carry[3 * p:3 * p + 3]
                    w = jnp.exp(zs[u][p] + Ls[u][p] + tails[u][p] + _two_halves(R0, R1))
                    s0, s1 = _half_sums(Ls[u][p])
                    if diags[u]:
                        w = jnp.where(strict2, w, 0.0)
                    else:
                        w, s0, s1 = (jnp.where(ok[u], t, 0.0) for t in (w, s0, s1))
                    acc = acc + jnp.dot(_mx(w), kv[u][1][p], preferred_element_type=f32)
                    carry[3 * p:3 * p + 3] = [R0 + s0, R1 + s1, acc]
            return tuple(carry)

        def qblock(n, c):
            qrows = pl.ds(pl.multiple_of(n * BLK, BLK), BLK)
            qs = [_mx(q_ref[qrows, p * BLK:(p + 1) * BLK] * (HD ** -0.5)) for p in range(2)]
            z1, z2 = jnp.zeros((BLK, 1), f32), jnp.zeros((BLK, BLK), f32)
            near = [n - 1 - u for u in range(SB_UNROLL)]
            carry = step(qs, [n] + near, [True] + [False] * SB_UNROLL, (z1, z1, z2, z1, z1, z2))
            far = jnp.maximum(n - SB_UNROLL, 0)
            trips = (far + SB_UNROLL - 1) // SB_UNROLL

            def live(st):
                worst = jnp.maximum(jnp.maximum(st[1], st[2]), jnp.maximum(st[4], st[5]))
                return (st[0] < trips) & (jnp.max(worst) > SB_CUT)

            def trip(st):
                i = st[0]
                kbs = [far - 1 - SB_UNROLL * i - u for u in range(SB_UNROLL)]
                return (i + 1,) + step(qs, kbs, [False] * SB_UNROLL, st[1:])
            done, *res = lax.while_loop(live, trip, (jnp.int32(0),) + carry)
            lt = jnp.where(ci == SB_HEADS, done.astype(f32), 0.0)
            for p in range(2):
                y_ref[qrows, p * BLK:(p + 1) * BLK] = res[3 * p + 2].astype(bf16)
                lt = lt + jnp.where(ci == 2 * p, res[3 * p], 0.0) + jnp.where(ci == 2 * p + 1, res[3 * p + 1], 0.0)
            lt_ref[qrows, :] = lt
            return c
        lax.fori_loop(0, S // BLK, qblock, 0)

    spec = lambda j: pl.BlockSpec((S, GW), lambda b: (b, j))
    c_args, c_in, c_out, c_shapes, aliases, c_scratch = _host_specs(comm, 3, 2)
    step = lambda v: (lambda: pl.program_id(0) == v)
    stacked = pltpu.VMEM((2, S // BLK, 2 * BLK, BLK), _MXU)
    return pl.pallas_call(
        _host(body, 3, 2, 2, comm, step(0), step(B - 1)), grid=(B,),
        in_specs=[spec(5), spec(6), spec(7)] + c_in,
        out_specs=[pl.BlockSpec((S, GW), lambda b: (b, 0)), pl.BlockSpec((S, BLK), lambda b: (b, 0))] + c_out,
        out_shape=[SDS((B * S, GW), bf16), SDS((B * S, BLK), f32)] + c_shapes,
        input_output_aliases=aliases, scratch_shapes=[stacked, stacked] + c_scratch,
        name="sb_fwd" if comm is None else "sb_fwd_gather",
        compiler_params=_cp("arbitrary"))(proj, proj, proj, *c_args)


def _sb_bwd(proj, ltot, dy, B, S, comm=None):
    def body(q_ref, k_ref, v_ref, lt_ref, dy_ref, dq_ref, dk_ref, dv_ref, dka_ref, dva_ref, kst_ref, vst_ref):
        ci = _iota((BLK, BLK), 1)
        upto2, strict2, head0 = _sb_consts(lambda r, c: r <= c)
        below2, _, _ = _sb_consts(lambda r, c: r < c)
        dka_ref[...] = jnp.zeros_like(dka_ref)
        dva_ref[...] = jnp.zeros_like(dva_ref)
        _sb_stack_kv(k_ref, v_ref, kst_ref, vst_ref, head0, S // BLK)

        def step(qs, dos, lts, kbs, ok, diags, top, carry):
            U = range(len(kbs))
            kbs = [jnp.clip(kb, 0, top) for kb in kbs]
            kv = [_sb_load_kv(kst_ref, vst_ref, kb) for kb in kbs]
            zs = [[lax.dot_general(qs[p], kv[u][0][p], NT, preferred_element_type=f32) for p in range(2)] for u in U]
            dws = [[lax.dot_general(dos[p], kv[u][1][p], NT, preferred_element_type=f32) for p in range(2)] for u in U]
            parts = [[_log1m_parts(z) for z in zu] for zu in zs]
            Ls = [[jnp.where(strict2, lt[0], 0.0) if diags[u] else lt[0] for lt in parts[u]] for u in U]
            pins = [[_split_dot(L, upto2) for L in Lu] for Lu in Ls]
            carry = list(carry)
            ws, das = [], []
            for u in U:
                wu, dau = [], []
                for p in range(2):
                    PL0, PL1 = carry[5 * p], carry[5 * p + 1]
                    tail = _two_halves(lts[2 * p] - PL0, lts[2 * p + 1] - PL1) - pins[u][p]
                    w = jnp.exp(zs[u][p] + Ls[u][p] + tail)
                    l0, l1 = _half_sums(Ls[u][p])
                    if diags[u]:
                        w = jnp.where(strict2, w, 0.0)
                    else:
                        w, l0, l1 = (jnp.where(ok[u], t, 0.0) for t in (w, l0, l1))
                    carry[5 * p], carry[5 * p + 1] = PL0 + l0, PL1 + l1
                    wu.append(w)
                    dau.append(w * dws[u][p])
                ws.append(wu)
                das.append(dau)
            pexs = [[_split_dot(da, below2) for da in dau] for dau in das]
            dzs = []
            for u in U:
                dzu = []
                for p in range(2):
                    dL = _two_halves(carry[5 * p + 2], carry[5 * p + 3]) + pexs[u][p]
                    sg = _sigmoid_from(zs[u][p], parts[u][p][1])
                    dz = das[u][p] * (1.0 - sg) - dL * sg
                    dz = jnp.where(strict2 if diags[u] else ok[u], dz, 0.0)
                    a0, a1 = _half_sums(das[u][p])
                    carry[5 * p + 2], carry[5 * p + 3] = carry[5 * p + 2] + a0, carry[5 * p + 3] + a1
                    dzu.append(_mx(dz))
                dzs.append(dzu)
            dqs = [[jnp.dot(dzs[u][p], kv[u][0][p], preferred_element_type=f32) for p in range(2)] for u in U]
            dks = [[lax.dot_general(dzs[u][p], qs[p], TN, preferred_element_type=f32) for p in range(2)] for u in U]
            dvs = [[lax.dot_general(_mx(ws[u][p]), dos[p], TN, preferred_element_type=f32) for p in range(2)] for u in U]
            for u in U:
                krows = pl.ds(pl.multiple_of(kbs[u] * BLK, BLK), BLK)
                for p in range(2):
                    lanes = slice(p * BLK, (p + 1) * BLK)
                    dka_ref[krows, lanes] += jnp.where(head0, dks[u][p][:BLK], dks[u][p][BLK:])
                    dva_ref[krows, lanes] += jnp.where(head0, dvs[u][p][:BLK], dvs[u][p][BLK:])
                    carry[5 * p + 4] = carry[5 * p + 4] + dqs[u][p]
            return tuple(carry)

        def qblock(n, c):
            qrows = pl.ds(pl.multiple_of(n * BLK, BLK), BLK)
            ltb = lt_ref[qrows, :]
            lts = [jnp.sum(jnp.where(ci == h, ltb, 0.0), axis=-1, keepdims=True) for h in range(4)]
            qs = [_mx(q_ref[qrows, p * BLK:(p + 1) * BLK] * (HD ** -0.5)) for p in range(2)]
            dos = [_mx(dy_ref[qrows, p * BLK:(p + 1) * BLK]) for p in range(2)]
            z1, z2 = jnp.zeros((BLK, 1), f32), jnp.zeros((BLK, BLK), f32)
            done = jnp.max(jnp.where(ci == SB_HEADS, ltb, 0.0)).astype(jnp.int32)
            far = jnp.maximum(n - SB_UNROLL, 0)
            first = jnp.maximum(far - SB_UNROLL * done, 0)

            def trip(i, cr):
                kbs = [first + SB_UNROLL * i + u for u in range(SB_UNROLL)]
                return step(qs, dos, lts, kbs, [kb < far for kb in kbs], [False] * SB_UNROLL, n, cr)
            carry = lax.fori_loop(0, (far - first + SB_UNROLL - 1) // SB_UNROLL, trip, (z1, z1, z1, z1, z2) * 2)
            near = [n - SB_UNROLL + u for u in range(SB_UNROLL)]
            res = step(qs, dos, lts, near + [n], [kb >= 0 for kb in near] + [None], [False] * SB_UNROLL + [True], n, carry)
            for p in range(2):
                dq_ref[qrows, p * BLK:(p + 1) * BLK] = (res[5 * p + 4] * (HD ** -0.5)).astype(bf16)
            return c
        lax.fori_loop(0, S // BLK, qblock, 0)
        dk_ref[...] = dka_ref[...].astype(bf16)
        dv_ref[...] = dva_ref[...].astype(bf16)

    spec = lambda j: pl.BlockSpec((S, GW), lambda b: (b, j))
    o = pl.BlockSpec((S, GW), lambda b: (b, 0))
    c_args, c_in, c_out, c_shapes, aliases, c_scratch = _host_specs(comm, 5, 3)
    step = lambda v: (lambda: pl.program_id(0) == v)
    stacked = pltpu.VMEM((2, S // BLK, 2 * BLK, BLK), _MXU)
    return pl.pallas_call(
        _host(body, 5, 3, 4, comm, step(0), step(B - 1)), grid=(B,),
        in_specs=[spec(5), spec(6), spec(7), pl.BlockSpec((S, BLK), lambda b: (b, 0)), o] + c_in,
        out_specs=[o, o, o] + c_out,
        out_shape=[SDS((B * S, GW), bf16)] * 3 + c_shapes,
        input_output_aliases=aliases,
        scratch_shapes=[pltpu.VMEM((S, GW), f32), pltpu.VMEM((S, GW), f32), stacked, stacked] + c_scratch,
        name="sb_bwd" if comm is None else "sb_bwd_exchange",
        compiler_params=_cp("arbitrary"))(proj, proj, proj, ltot, dy, *c_args)


def _bias_expand(rel_bias_t, bucket):
    n = bucket.shape[1]

    def body(r_ref, b_ref, o_ref):
        onehot = (_iota((N_BUCKETS, n), 0) == b_ref[...]).astype(f32)
        o_ref[...] = jnp.dot(r_ref[...], onehot, precision=HIGHEST, preferred_element_type=f32)
    return pl.pallas_call(body, out_shape=SDS((rel_bias_t.shape[0], n), f32), name="bias_expand",
                          compiler_params=_cp())(rel_bias_t, bucket)


def _bias_reduce(dbias, bucket):
    n = bucket.shape[1]

    def body(*refs):
        b_ref, g_ref = refs[-2], refs[-1]
        d = refs[0][...]
        for r in refs[1:-2]:
            d = d + r[...]
        onehot = (_iota((N_BUCKETS, n), 0) == b_ref[...]).astype(f32)
        g_ref[...] = lax.dot_general(d, onehot, NT, precision=HIGHEST, preferred_element_type=f32)
    return pl.pallas_call(body, out_shape=SDS((dbias[0].shape[0], N_BUCKETS), f32), name="bias_reduce",
                          compiler_params=_cp())(*dbias, bucket)


def _adamw(w, g, m, v, tr, name, emit_g=False):
    R, C = w.shape

    def body(w_ref, g_ref, m_ref, v_ref, d_ref, m2_ref, v2_ref, *g_out):
        gv = g_ref[...]
        if emit_g:
            g_out[0][...] = gv
        m2 = ADAM_B1 * m_ref[...] + (1.0 - ADAM_B1) * gv
        v2 = ADAM_B2 * v_ref[...] + (1.0 - ADAM_B2) * (gv * gv)
        m_hat = m2 / (1.0 - ADAM_B1 ** ADAM_STEP)
        v_hat = v2 / (1.0 - ADAM_B2 ** ADAM_STEP)
        d_ref[...] = -ADAM_LR * (m_hat / (jnp.sqrt(v_hat) + ADAM_EPS) + ADAM_WD * w_ref[...])
        m2_ref[...] = m2
        v2_ref[...] = v2

    spec = pl.BlockSpec((tr, C), lambda i: (i, 0))
    n_out = 4 if emit_g else 3
    return pl.pallas_call(
        body, grid=(R // tr,), in_specs=[spec] * 4, out_specs=[spec] * n_out,
        out_shape=[SDS((R, C), f32)] * n_out, name=name, compiler_params=_cp("parallel"))(w, g, m, v)


ANY = pl.BlockSpec(memory_space=pl.ANY)


def _place():
    x, y, c = lax.axis_index("x"), lax.axis_index("y"), lax.axis_index("c")
    chips = [(1 - x, y), (x, 1 - y), (1 - x, 1 - y)]
    return x, y, c, chips


def _cast_slots(w, kidx):
    L, a, b = w.shape
    ta = a // 2

    def body(k_ref, *refs):
        for l in range(L):
            refs[L + l][0] = refs[l][0].astype(bf16)

    return pl.pallas_call(
        body,
        grid_spec=pltpu.PrefetchScalarGridSpec(
            num_scalar_prefetch=1, grid=(a // ta,),
            in_specs=[pl.BlockSpec((1, ta, b), functools.partial(lambda i, k_ref, l: (l, i, 0), l=l)) for l in range(L)],
            out_specs=[pl.BlockSpec((1, ta, b), lambda i, k_ref: (k_ref[0], i, 0)) for _ in range(L)]),
        out_shape=[SDS((N_CHIPS, a, b), bf16)] * L,
        name="cast_slots", compiler_params=_cp("parallel"))(kidx, *([w] * L))


class _GatherComm:
    def __init__(self, bufs, part=0, parts=1):
        self.inputs = list(bufs)
        self.out_shape = [SDS(b.shape, b.dtype) for b in bufs]
        self.aliased = True
        self.scratch = [pltpu.SemaphoreType.DMA((3 * len(bufs),))] * 4
        self.part, self.parts = part, parts

    def _copies(self, i_refs, o_refs, sems):
        send1, recv1, send2, recv2 = sems
        x, y, c, chips = _place()
        k = 2 * x + y
        first, got1, second, got2 = [], [], [], []
        for i, buf in enumerate(self.inputs):
            h = buf.shape[1] // 2
            n = h // self.parts
            mine, theirs = pl.ds(c * h + self.part * n, n), pl.ds((1 - c) * h + self.part * n, n)
            for j, (cx, cy) in enumerate(chips):
                s = 3 * i + j
                first.append(pltpu.make_async_remote_copy(
                    src_ref=i_refs[i].at[k, mine], dst_ref=o_refs[i].at[k, mine], send_sem=send1.at[s],
                    recv_sem=recv1.at[s], device_id=(cx, cy, c), device_id_type=MESH))
                a = o_refs[i].at[2 * cx + cy, mine]
                got1.append(pltpu.make_async_remote_copy(
                    src_ref=a, dst_ref=a, send_sem=send1.at[s], recv_sem=recv1.at[s],
                    device_id=(cx, cy, c), device_id_type=MESH))
                second.append(pltpu.make_async_remote_copy(
                    src_ref=a, dst_ref=a, send_sem=send2.at[s], recv_sem=recv2.at[s],
                    device_id=(x, y, 1 - c), device_id_type=MESH))
                b = o_refs[i].at[2 * cx + cy, theirs]
                got2.append(pltpu.make_async_remote_copy(
                    src_ref=b, dst_ref=b, send_sem=send2.at[s], recv_sem=recv2.at[s],
                    device_id=(x, y, 1 - c), device_id_type=MESH))
        return first, got1, second, got2

    def start(self, i_refs, o_refs, sems):
        for cp in self._copies(i_refs, o_refs, sems)[0]:
            cp.start()

    def finish(self, i_refs, o_refs, sems):
        first, got1, second, got2 = self._copies(i_refs, o_refs, sems)
        for g, cp in zip(got1, second):
            g.wait_recv()
            cp.start()
        for g in got2:
            g.wait_recv()
        for cp in first + second:
            cp.wait_send()


class _MultiComm:
    def __init__(self, comms):
        self.comms = comms
        self.inputs = [a for c in comms for a in c.inputs]
        self.out_shape = [s for c in comms for s in c.out_shape]
        self.aliased = comms[0].aliased
        assert all(c.aliased == self.aliased for c in comms)
        self.scratch = [s for c in comms for s in c.scratch]

    def _split(self, i_refs, o_refs, sems):
        i = o = s = 0
        for c in self.comms:
            ni, no, ns = len(c.inputs), len(c.out_shape), len(c.scratch)
            yield c, i_refs[i:i + ni], o_refs[o:o + no], sems[s:s + ns]
            i, o, s = i + ni, o + no, s + ns

    def start(self, i_refs, o_refs, sems):
        for c, i, o, s in self._split(i_refs, o_refs, sems):
            c.start(i, o, s)

    def finish(self, i_refs, o_refs, sems):
        for c, i, o, s in self._split(i_refs, o_refs, sems):
            c.finish(i, o, s)


class _PairExchangeComm:
    def __init__(self, gs):
        self.inputs = list(gs)
        self.out_shape = [SDS((g.shape[0], g.shape[1] // 2, g.shape[2]), g.dtype) for g in gs]
        self.aliased = False
        self.scratch = [pltpu.SemaphoreType.DMA((len(gs),))] * 2

    def _copies(self, i_refs, o_refs, sems):
        send, recv = sems
        x, y, c, _ = _place()
        cps = []
        for i, g in enumerate(self.inputs):
            h = g.shape[1] // 2
            cps.append(pltpu.make_async_remote_copy(
                src_ref=i_refs[i].at[:, pl.ds((1 - c) * h, h)], dst_ref=o_refs[i], send_sem=send.at[i], recv_sem=recv.at[i],
                device_id=(x, y, 1 - c), device_id_type=MESH))
        return cps

    def start(self, i_refs, o_refs, sems):
        for cp in self._copies(i_refs, o_refs, sems):
            cp.start()

    def finish(self, i_refs, o_refs, sems):
        for cp in self._copies(i_refs, o_refs, sems):
            cp.wait()


class _ChipExchangeComm:
    def __init__(self, qs):
        self.inputs = list(qs)
        self.out_shape = [SDS(q.shape, q.dtype) for q in qs]
        self.aliased = False
        self.scratch = [pltpu.SemaphoreType.DMA((3 * len(qs),))] * 2

    def _copies(self, i_refs, o_refs, sems):
        send, recv = sems
        x, y, c, chips = _place()
        k = 2 * x + y
        cps, got = [], []
        for i in range(len(self.inputs)):
            for j, (cx, cy) in enumerate(chips):
                s = 3 * i + j
                cps.append(pltpu.make_async_remote_copy(
                    src_ref=i_refs[i].at[2 * cx + cy], dst_ref=o_refs[i].at[k], send_sem=send.at[s],
                    recv_sem=recv.at[s], device_id=(cx, cy, c), device_id_type=MESH))
                a = o_refs[i].at[2 * cx + cy]
                got.append(pltpu.make_async_remote_copy(
                    src_ref=a, dst_ref=a, send_sem=send.at[s], recv_sem=recv.at[s],
                    device_id=(cx, cy, c), device_id_type=MESH))
        return cps, got

    def start(self, i_refs, o_refs, sems):
        for cp in self._copies(i_refs, o_refs, sems)[0]:
            cp.start()

    def finish(self, i_refs, o_refs, sems):
        cps, got = self._copies(i_refs, o_refs, sems)
        for g in got:
            g.wait_recv()
        for cp in cps:
            cp.wait_send()


def _comm_only(comm, name):
    n = len(comm.inputs)

    def body(*refs):
        i_refs, o_refs, sems = refs[:n], refs[n:n + len(comm.out_shape)], refs[n + len(comm.out_shape):]
        comm.start(i_refs, o_refs, sems)
        comm.finish(i_refs, o_refs, sems)

    return pl.pallas_call(
        body, out_shape=comm.out_shape, in_specs=[ANY] * n, out_specs=[ANY] * len(comm.out_shape),
        input_output_aliases={i: i for i in range(n)} if comm.aliased else {},
        scratch_shapes=comm.scratch, name=name,
        compiler_params=pltpu.CompilerParams(has_side_effects=True))(*comm.inputs)


def _host(body, n_in, n_out, n_scratch, comm, first, last):
    if comm is None:
        return body
    ci, co = len(comm.inputs), len(comm.out_shape)

    def wrapped(*refs):
        o = 0
        parts = []
        for n in (n_in, ci, n_out, co, n_scratch):
            parts.append(refs[o:o + n])
            o += n
        hin, cin, hout, cout, hs = parts
        sems = refs[o:]

        @pl.when(first())
        def _():
            comm.start(cin, cout, sems)
        body(*hin, *hout, *hs)

        @pl.when(last())
        def _():
            comm.finish(cin, cout, sems)
    return wrapped


def _host_specs(comm, n_in, n_out):
    if comm is None:
        return [], [], [], [], {}, []
    ci, co = len(comm.inputs), len(comm.out_shape)
    aliases = {n_in + i: n_out + i for i in range(ci)} if comm.aliased else {}
    return comm.inputs, [ANY] * ci, [ANY] * co, comm.out_shape, aliases, comm.scratch


def _pair_add(g, r, kc, name):
    ns, a, b = g.shape
    h = a // 2
    th = h if h * b * 4 <= 4 * 1024 * 1024 else h // 2

    def body(kc_ref, g_ref, r_ref, qb_ref):
        qb_ref[...] = (g_ref[...] + r_ref[...]).astype(bf16)

    nb = h // th
    slot = lambda j, kc_ref: (kc_ref[0] + 1 + j) % ns
    spec = pl.BlockSpec((1, th, b), lambda j, i, kc_ref: (slot(j, kc_ref), i, 0))
    return pl.pallas_call(
        body,
        grid_spec=pltpu.PrefetchScalarGridSpec(
            num_scalar_prefetch=1, grid=(ns - 1, nb),
            in_specs=[pl.BlockSpec((1, th, b), lambda j, i, kc_ref: (slot(j, kc_ref), kc_ref[1] * nb + i, 0)), spec],
            out_specs=spec),
        out_shape=SDS((ns, h, b), bf16),
        name=name, compiler_params=_cp("parallel", "parallel"))(kc, g, r)


def _chip_add(g, r1, r2, idx, prev, L, name):
    ns, h, b = r2.shape
    th = h if h * b * 4 <= 4 * 1024 * 1024 else h // 2
    nb = h // th

    def body(s_ref, g_ref, r1_ref, a_ref, b_ref, c_ref, *rest):
        o_ref = rest[-1]
        o_ref[0] = (g_ref[0] + r1_ref[0]) + a_ref[0].astype(f32) + b_ref[0].astype(f32) + c_ref[0].astype(f32)

    other = lambda d: pl.BlockSpec((1, th, b), lambda i, s_ref: ((s_ref[0] + d) % ns, i, 0))
    in_specs = [pl.BlockSpec((1, th, b), lambda i, s_ref: (s_ref[0], s_ref[1] * nb + i, 0)),
                pl.BlockSpec((1, th, b), lambda i, s_ref: (s_ref[0], i, 0)), other(1), other(2), other(3)]
    args = [idx, g, r1, r2, r2, r2]
    aliases = {}
    if prev is not None:
        in_specs.append(ANY)
        args.append(prev)
        aliases = {6: 0}
    return pl.pallas_call(
        body,
        grid_spec=pltpu.PrefetchScalarGridSpec(
            num_scalar_prefetch=1, grid=(nb,), in_specs=in_specs,
            out_specs=pl.BlockSpec((1, th, b), lambda i, s_ref: (s_ref[2], s_ref[1] * nb + i, 0))),
        out_shape=SDS((L, 2 * h, b), f32), input_output_aliases=aliases,
        name=name, compiler_params=_cp("arbitrary"))(*args)


def _pair_share(gs, hs):
    n = len(gs)
    L = gs[0].shape[0]

    def body(*refs):
        i_refs, o_refs = refs[:n], refs[n:2 * n]
        send, recv = refs[2 * n:]
        x, y, c, _ = _place()
        cps = []
        for i in range(n):
            for l in range(L):
                mine = pl.ds(c * hs[i], hs[i])
                cp = pltpu.make_async_remote_copy(
                    src_ref=i_refs[i].at[l, mine], dst_ref=o_refs[i].at[l, mine], send_sem=send.at[i * L + l],
                    recv_sem=recv.at[i * L + l], device_id=(x, y, 1 - c), device_id_type=MESH)
                cp.start()
                cps.append(cp)
        for i in range(n):
            for l in range(L):
                got = o_refs[i].at[l, pl.ds((1 - c) * hs[i], hs[i])]
                pltpu.make_async_remote_copy(
                    src_ref=got, dst_ref=got, send_sem=send.at[i * L + l], recv_sem=recv.at[i * L + l],
                    device_id=(x, y, 1 - c), device_id_type=MESH).wait_recv()
        for cp in cps:
            cp.wait_send()

    return pl.pallas_call(
        body, out_shape=[SDS(g.shape, g.dtype) for g in gs], in_specs=[ANY] * n, out_specs=[ANY] * n,
        input_output_aliases={i: i for i in range(n)},
        scratch_shapes=[pltpu.SemaphoreType.DMA((n * L,))] * 2,
        name="grad_pair_share", compiler_params=pltpu.CompilerParams(has_side_effects=True))(*gs)


class _SwapComm:
    def __init__(self, arrays):
        self.inputs = list(arrays)
        self.out_shape = [SDS(a.shape, a.dtype) for a in arrays]
        self.aliased = False
        self.scratch = [pltpu.SemaphoreType.DMA((len(arrays),))] * 2

    def _copies(self, i_refs, o_refs, sems):
        send, recv = sems
        x, y, c, _ = _place()
        return [pltpu.make_async_remote_copy(
            src_ref=i_refs[i], dst_ref=o_refs[i], send_sem=send.at[i], recv_sem=recv.at[i],
            device_id=(x, y, 1 - c), device_id_type=MESH) for i in range(len(self.inputs))]

    def start(self, i_refs, o_refs, sems):
        for cp in self._copies(i_refs, o_refs, sems):
            cp.start()

    def finish(self, i_refs, o_refs, sems):
        for cp in self._copies(i_refs, o_refs, sems):
            cp.wait()


class _SlotShareComm:
    def __init__(self, bufs):
        self.inputs = list(bufs)
        self.out_shape = [SDS(b.shape, b.dtype) for b in bufs]
        self.aliased = True
        self.scratch = [pltpu.SemaphoreType.DMA((3 * len(bufs),))] * 2

    def _copies(self, i_refs, o_refs, sems):
        send, recv = sems
        x, y, c, chips = _place()
        k = 2 * x + y
        cps, got = [], []
        for i in range(len(self.inputs)):
            for j, (cx, cy) in enumerate(chips):
                s = 3 * i + j
                cps.append(pltpu.make_async_remote_copy(
                    src_ref=i_refs[i].at[k], dst_ref=o_refs[i].at[k], send_sem=send.at[s], recv_sem=recv.at[s],
                    device_id=(cx, cy, c), device_id_type=MESH))
                a = o_refs[i].at[2 * cx + cy]
                got.append(pltpu.make_async_remote_copy(
                    src_ref=a, dst_ref=a, send_sem=send.at[s], recv_sem=recv.at[s],
                    device_id=(cx, cy, c), device_id_type=MESH))
        return cps, got

    def start(self, i_refs, o_refs, sems):
        for cp in self._copies(i_refs, o_refs, sems)[0]:
            cp.start()

    def finish(self, i_refs, o_refs, sems):
        cps, got = self._copies(i_refs, o_refs, sems)
        for g in got:
            g.wait_recv()
        for cp in cps:
            cp.wait_send()


def _pair_sum_slot(mine, theirs, kidx, dtype):
    R, C = mine.shape

    def body(k_ref, a_ref, b_ref, o_ref):
        o_ref[0] = (a_ref[...] + b_ref[...]).astype(dtype)

    spec = pl.BlockSpec((R, C), lambda i, k_ref: (0, 0))
    return pl.pallas_call(
        body,
        grid_spec=pltpu.PrefetchScalarGridSpec(
            num_scalar_prefetch=1, grid=(1,), in_specs=[spec, spec],
            out_specs=pl.BlockSpec((1, R, C), lambda i, k_ref: (k_ref[0], 0, 0))),
        out_shape=SDS((N_CHIPS, R, C), dtype), name="small_pair_sum", compiler_params=_cp("arbitrary"))(kidx, mine, theirs)


def _small_sum(g):
    n, R, C = g.shape

    def body(g_ref, o_ref):
        acc = g_ref[0].astype(f32)
        for j in range(1, n):
            acc = acc + g_ref[j].astype(f32)
        o_ref[...] = acc
    return pl.pallas_call(body, out_shape=SDS((R, C), f32), name="small_sum", compiler_params=_cp())(g)


PACK_COLS = 1024


def _rows_of(shape):
    n = int(np.prod(shape)) if len(shape) else 1
    return -(-n // (8 * PACK_COLS)) * 8


def _pack(parts):
    blocks = []
    for p in parts:
        flat = p.reshape(-1)
        r = _rows_of(p.shape)
        blocks.append(jnp.pad(flat, (0, r * PACK_COLS - flat.shape[0])).reshape(r, PACK_COLS))
    return jnp.concatenate(blocks, axis=0)


def _unpack(buf, shapes):
    out, off = [], 0
    for s in shapes:
        n = int(np.prod(s)) if len(s) else 1
        r = _rows_of(s)
        out.append(buf[off:off + r].reshape(-1)[:n].reshape(s))
        off += r
    return out


def _block_diag(w):
    g, a, _ = w.shape
    out = jnp.zeros((g * a, g * a), w.dtype)
    for i in range(g):
        out = lax.dynamic_update_slice(out, w[i], (i * a, i * a))
    return out


def kernel(x, w_in, w_out, sgu_w, sgu_b, pool_w, pool_scale, swa_sinks, rel_bias, mix_out_gain, norm_mix, norm_ffn, w_gate_up, w_down, norm_final, loss_target, m_w_in, m_w_out, m_sgu_w, m_sgu_b, m_pool_w, m_pool_scale, m_swa_sinks, m_rel_bias, m_mix_out_gain, m_norm_mix, m_norm_ffn, m_w_gate_up, m_w_down, m_norm_final, v_w_in, v_w_out, v_sgu_w, v_sgu_b, v_pool_w, v_pool_scale, v_swa_sinks, v_rel_bias, v_mix_out_gain, v_norm_mix, v_norm_ffn, v_w_gate_up, v_w_down, v_norm_final):
    B, S, D = x.shape
    T = B * S
    L = w_in.shape[0]
    tm = min(512, T)
    F = w_down.shape[1] * N_CHIPS
    xi, yi, ci = lax.axis_index("x"), lax.axis_index("y"), lax.axis_index("c")
    kidx = jnp.reshape(2 * xi + yi, (1,)).astype(jnp.int32)
    kcidx = jnp.stack([2 * xi + yi, ci]).astype(jnp.int32)

    big = [w_in, w_out, w_gate_up, w_down]
    slots = [_cast_slots(w, kidx) for w in big]
    gather = lambda pi, l, part=0, parts=1: _GatherComm([slots[pi][l]], part, parts)
    Win, Wo, Wgu, Wd = ([None] * L for _ in range(4))
    Win[0], = _comm_only(gather(0, 0), "gather_weights")

    bucket = jnp.asarray(_t5_bucket_table().reshape(1, -1))
    bias_tab = _bias_expand(rel_bias.T, bucket).reshape(4, BLK, 2 * BLK)

    row = lambda v: v.reshape(1, -1)
    xc = x.reshape(T, D)
    tgt = loss_target.reshape(T, D)
    saved = []
    for l in range(L):
        bexp = jnp.repeat(sgu_b[l].T, HD, axis=1)
        wbd = _block_diag(pool_w[l])
        sk = jnp.broadcast_to(swa_sinks[l][:, None, None], (4, 1, BLK))
        h1, proj, wo = _norm_mm(xc, row(norm_mix[l]), Win[l], tm, gather(1, l))
        ya, yb = _sgu_pool_fwd(proj, sgu_w[l], bexp, wbd, row(pool_scale[l]), B, S)
        yc, wd = _swa_fwd(proj, sk, bias_tab, B, S, gather(3, l, 0, 2))
        if l == 0:
            yd, lt, wd, Wgu[0] = _sb_fwd(proj, B, S, _MultiComm([_GatherComm([wd], 1, 2), gather(2, 0)]))
        else:
            yd, lt, wd = _sb_fwd(proj, B, S, _GatherComm([wd], 1, 2))
        Wo[l], Wd[l] = wo.reshape(D, D), wd.reshape(F, D)
        ys = (ya, yb, yc, yd)
        nxt = l + 1 < L
        ycn, x1, h2, gu, act, *got = _mix_out_swiglu(ys, row(mix_out_gain[l]), Wo[l], xc, row(norm_ffn[l]), Wgu[l], tm,
                                                     gather(2, l + 1) if nxt else None)
        x2, *got2 = _mm_res(act, Wd[l], x1, tm, gather(0, l + 1) if nxt else None)
        if nxt:
            Wgu[l + 1], Win[l + 1] = got[0], got2[0]
        saved.append((xc, h1, proj, bexp, wbd, sk, ys, lt, ycn, x1, h2, gu, act))
        xc = x2

    dx, g_final, loss_v = _final_loss(xc, row(norm_final), tgt, tm)

    tk = min(T, 2048)
    gW = [[None] * L for _ in range(4)]
    g_sgu_w, g_sgu_b, g_pool_w, g_pool_scale, g_sinks, g_bias = ([None] * L for _ in range(6))
    g_out_gain, g_mix, g_ffn = ([None] * L for _ in range(3))
    reduced = [None] * 4
    sums = {}

    def pair_comm(keys):
        return _PairExchangeComm([gW[pi][l] for pi, l in keys])

    def after_pair(keys, r1):
        for (pi, l), r in zip(keys, r1):
            sums[pi, l] = (r, _pair_add(gW[pi][l], r, kcidx, "grad_pair_add"))

    def chip_comm(keys):
        return _ChipExchangeComm([sums[k][1] for k in keys])

    def after_chip(keys, r2):
        for (pi, l), r in zip(keys, r2):
            idx = jnp.stack([2 * xi + yi, ci, jnp.int32(l)]).astype(jnp.int32)
            reduced[pi] = _chip_add(gW[pi][l], sums.pop((pi, l))[0], r, idx, reduced[pi], L, "grad_chip_add")

    for l in reversed(range(L)):
        x0, h1, proj, bexp, wbd, sk, ys, lt, ycn, x1, h2, gu, act = saved[l]
        keys = [(0, l + 1), (1, l + 1)]
        comm = chip_comm(keys) if l + 1 < L else None
        dgu, dx1, g_ffn[l], *r2 = _dact_dx(dx, Wd[l], gu, Wgu[l], x1, row(norm_ffn[l]), tm // 2, comm)
        if comm is not None:
            after_chip(keys, r2)
        gW[3][l] = _dw(act, dx, lambda t, s: (t, 0), D, 1, F // 2, tk // 2, "dw_down").reshape(N_CHIPS, F // N_CHIPS, D)
        gW[2][l] = _dw(h2, dgu, lambda t, s: (s // 2, t, s % 2), F // 2, N_CHIPS, D, tk, "dw_gate_up")
        gW[1][l] = _dw(ycn, dx1, lambda t, s: (t, 0), D, 1, D, tk, "dw_out").reshape(N_CHIPS, D // N_CHIPS, D)
        dya, dyb, dyc, dyd, g_out_gain[l] = _dycat(dx1, Wo[l], ys, row(mix_out_gain[l]), tm)
        keys = [(2, l), (3, l)]
        dpa, g_sgu_w[l], dbf, dpb, dwbd, dsc, *r1 = _sgu_pool_bwd(proj, sgu_w[l], bexp, dya, wbd, row(pool_scale[l]), dyb,
                                                                  B, S, pair_comm(keys))
        after_pair(keys, r1)
        g_sgu_b[l] = dbf[:, ::HD].T
        npg = len(POOL_WINDOWS)
        g_pool_w[l] = jnp.stack([dwbd[i * HD:(i + 1) * HD, i * HD:(i + 1) * HD] for i in range(npg)])
        g_pool_scale[l] = dsc[0]
        dcq, dckv, dsk, g_bias[l], *r2 = _swa_bwd(proj, sk, bias_tab, dyc, B, S, chip_comm([(3, l)]))
        after_chip([(3, l)], r2)
        g_sinks[l] = dsk[:, 0, 0] * float(BLK)
        ddq, ddk, ddv, *r2 = _sb_bwd(proj, lt, dyd, B, S, chip_comm([(2, l)]))
        after_chip([(2, l)], r2)
        dproj = [dpa, dpb, dcq, dckv, ddq, ddk, ddv]
        gW[0][l] = _dw_pieces(h1, dproj, w_in.shape[2], N_CHIPS, tk, "dw_in")
        keys = [(0, l), (1, l)]
        dx, g_mix[l], *r1 = _dx_norm_bwd(dproj, Win[l], x0, row(norm_mix[l]), dx1, tm, "dx_mix_exchange", pair_comm(keys))
        after_pair(keys, r1)
    grad_x = dx.reshape(B, S, D)

    after_chip(keys, _comm_only(chip_comm(keys), "grad_chip_exchange"))
    g_big = _pair_share(reduced, [g.shape[1] // 2 for g in reduced])

    g_rel_bias = _bias_reduce([g.reshape(4, -1) for g in g_bias], bucket).T
    small_g = [jnp.stack(g_sgu_w), jnp.stack(g_sgu_b), jnp.stack(g_pool_w), jnp.stack(g_pool_scale), jnp.stack(g_sinks),
               g_rel_bias, jnp.concatenate(g_out_gain), jnp.concatenate(g_mix), jnp.concatenate(g_ffn), g_final[0]]
    small_w = [sgu_w, sgu_b, pool_w, pool_scale, swa_sinks, rel_bias, mix_out_gain, norm_mix, norm_ffn, norm_final]
    small_m = [m_sgu_w, m_sgu_b, m_pool_w, m_pool_scale, m_swa_sinks, m_rel_bias, m_mix_out_gain, m_norm_mix, m_norm_ffn, m_norm_final]
    small_v = [v_sgu_w, v_sgu_b, v_pool_w, v_pool_scale, v_swa_sinks, v_rel_bias, v_mix_out_gain, v_norm_mix, v_norm_ffn, v_norm_final]
    shapes = [w.shape for w in small_w]
    bulk, fine = [small_g[0], small_g[2]], [small_g[i] for i in (1, 3, 4, 5, 6, 7, 8, 9)] + [loss_v[0, 0:1]]
    mine = [_pack(bulk), _pack(fine)]
    theirs = _comm_only(_SwapComm(mine), "small_pair_swap")
    slots_s = [_pair_sum_slot(a, b, kidx, dt) for a, b, dt in zip(mine, theirs, (bf16, f32))]
    shared = _comm_only(_SlotShareComm(slots_s), "small_chip_share")
    g_bulk = _unpack(_small_sum(shared[0]), [shapes[0], shapes[2]])
    *g_fine, loss = _unpack(_small_sum(shared[1]), [shapes[i] for i in (1, 3, 4, 5, 6, 7, 8, 9)] + [()])
    g_small = [g_bulk[0], g_fine[0], g_bulk[1]] + g_fine[1:]

    big_m = [m_w_in, m_w_out, m_w_gate_up, m_w_down]
    big_v = [v_w_in, v_w_out, v_w_gate_up, v_w_down]
    g_out, d_big, m_big, v_big = [], [], [], []
    for w, g, m, v in zip(big, g_big, big_m, big_v):
        two = lambda a: a.reshape(-1, a.shape[-1])
        rows = two(w).shape[0]
        cap = max(8, (1 << 21) // (4 * w.shape[-1]))
        tr = max(t for t in range(8, min(rows, cap) + 1, 8) if rows % t == 0)
        d2, m2, v2, g2 = _adamw(two(w), two(g), two(m), two(v), tr, "adamw_big", True)
        for lst, val in ((d_big, d2), (m_big, m2), (v_big, v2), (g_out, g2)):
            lst.append(val.reshape(w.shape))
    g_big = g_out

    g_small_packed = _pack(g_small)
    ds, ms, vs = _adamw(_pack(small_w), g_small_packed, _pack(small_m), _pack(small_v), g_small_packed.shape[0], "adamw_small")
    d_small, m_small, v_small = _unpack(ds, shapes), _unpack(ms, shapes), _unpack(vs, shapes)

    def order(bigs, smalls):
        return [bigs[0], bigs[1]] + list(smalls[0:9]) + [bigs[2], bigs[3], smalls[9]]

    return (loss, grad_x, *order(g_big, g_small), *order(d_big, d_small), *order(m_big, m_small), *order(v_big, v_small))
```
